```python
import jax
import jax.numpy as jnp
from jax import lax
import numpy as np

D_MODEL = 1024
BATCH = 8
SEQ = 4096
DEPTH = 1

CHUNK = 64
Q_BLOCK = 128
N_MEM = 256
EPS = 1e-6
SB_HD = 128
SB_HEADS = D_MODEL // SB_HD
SB_W = SB_HEADS * SB_HD
ML_HEADS = 4
ML_HD = D_MODEL // ML_HEADS
ML_W = ML_HEADS * ML_HD
X_HEADS = 4
X_HD = D_MODEL // X_HEADS
X_W = X_HEADS * X_HD
CONV_W = 4
D_FF = 4 * D_MODEL
N_BRANCH = 3
IN_SIZES = (SB_W, SB_W, SB_W, ML_W, ML_W, ML_W, ML_W, ML_HEADS, ML_HEADS, X_W, N_BRANCH * D_MODEL)
IN_SPLITS = tuple(int(v) for v in np.cumsum(IN_SIZES)[:-1])
N_IN = int(sum(IN_SIZES))

kernel_name = "hybrid_sb_mlstm_xattn_block"


def _rmsnorm(x, g):
    xf = x.astype(jnp.float32)
    y = xf * lax.rsqrt(jnp.mean(xf * xf, axis=-1, keepdims=True) + EPS)
    return (y * g.astype(jnp.float32)).astype(x.dtype)


def _split_heads(t, n):
    b, s, _ = t.shape
    return t.reshape(b, s, n, -1).transpose(0, 2, 1, 3)


def _merge_heads(t):
    b, h, s, d = t.shape
    return t.transpose(0, 2, 1, 3).reshape(b, s, h * d)


def _causal_conv(u, w, b):
    s = u.shape[1]
    up = jnp.pad(u, ((0, 0), (CONV_W - 1, 0), (0, 0)))
    return sum((up[:, j:j + s, :] * w[j] for j in range(CONV_W)), b)


def _stick_breaking(q, k, v):
    _, _, s_len, dh = q.shape
    scale = dh ** -0.5
    kf = k.astype(jnp.float32)
    vf = v.astype(jnp.float32)
    outs = []
    for blk in range(s_len // Q_BLOCK):
        start, end = blk * Q_BLOCK, (blk + 1) * Q_BLOCK
        qb = q[:, :, start:end].astype(jnp.float32)
        z = jnp.einsum('bhqd,bhkd->bhqk', qb, kf[:, :, :end]) * scale
        q_pos = start + jnp.arange(Q_BLOCK)
        k_pos = jnp.arange(end)
        causal = k_pos[None, :] < q_pos[:, None]
        log_1m = jnp.where(causal, jax.nn.log_sigmoid(-z), 0.0)
        later = lax.cumsum(log_1m, axis=3, reverse=True) - log_1m
        log_a = jnp.where(causal, jax.nn.log_sigmoid(z) + later, -jnp.inf)
        outs.append(jnp.einsum('bhqk,bhkd->bhqd', jnp.exp(log_a), vf[:, :, :end]))
    return jnp.concatenate(outs, axis=2)


def _mlstm(q, k, v, i_pre, f_pre):
    f32 = jnp.float32
    b, h, s_len, dh = q.shape
    nc = s_len // CHUNK
    q = q.astype(f32)
    k = k.astype(f32) * (dh ** -0.5)
    v = v.astype(f32)
    i_pre = i_pre.astype(f32)
    log_f = jax.nn.log_sigmoid(f_pre.astype(f32))

    def chunks(t):
        return jnp.moveaxis(t.reshape(b, h, nc, CHUNK, *t.shape[3:]), 2, 0)

    tri = jnp.tril(jnp.ones((CHUNK, CHUNK), dtype=bool))

    def step(carry, xs):
        c_prev, n_prev, m_prev = carry
        qc, kc, vc, ic, lfc = xs
        bcum = jnp.cumsum(lfc, axis=-1)
        d = jnp.where(tri, bcum[..., :, None] - bcum[..., None, :] + ic[..., None, :], -jnp.inf)
        m_inter = bcum + m_prev[..., None]
        m_t = jnp.maximum(m_inter, jnp.max(d, axis=-1))
        w = jnp.exp(d - m_t[..., None])
        s_inter = jnp.exp(m_inter - m_t)
        sc = jnp.einsum('bhtd,bhsd->bhts', qc, kc) * w
        num = jnp.einsum('bhts,bhsd->bhtd', sc, vc) + s_inter[..., None] * jnp.einsum('bhvk,bhtk->bhtv', c_prev, qc)
        den = jnp.sum(sc, axis=-1) + s_inter * jnp.einsum('bhtk,bhk->bht', qc, n_prev)
        h_c = num / jnp.maximum(jnp.abs(den), jnp.exp(-m_t))[..., None]
        b_end = bcum[..., -1]
        g = b_end[..., None] - bcum + ic
        m_new = jnp.maximum(b_end + m_prev, jnp.max(g, axis=-1))
        decay = jnp.exp(b_end + m_prev - m_new)
        wk = jnp.exp(g - m_new[..., None])
        c_new = decay[..., None, None] * c_prev + jnp.einsum('bhsv,bhsk->bhvk', vc * wk[..., None], kc)
        n_new = decay[..., None] * n_prev + jnp.einsum('bhs,bhsk->bhk', wk, kc)
        return (c_new, n_new, m_new), h_c

    init = (jnp.zeros((b, h, dh, dh), f32), jnp.zeros((b, h, dh), f32), jnp.zeros((b, h), f32))
    _, hs = lax.scan(step, init, (chunks(q), chunks(k), chunks(v), chunks(i_pre), chunks(log_f)))
    return jnp.moveaxis(hs, 0, 2).reshape(b, h, s_len, dh)


def _cross_attention(q, mk, mv, gq, gk):
    dh = q.shape[-1]
    qn = _rmsnorm(q, gq)
    kn = _rmsnorm(mk, gk)
    logits = jnp.einsum('bhsd,bhmd->bhsm', qn, kn).astype(jnp.float32) * (dh ** -0.5)
    p = jax.nn.softmax(logits, axis=-1)
    return jnp.einsum('bhsm,bhmd->bhsd', p.astype(mv.dtype), mv)


def _fwd_setup_inputs(seed: int = 0) -> dict:
    key = jax.random.key(seed)
    ks = jax.random.split(key, 24)
    f32 = jnp.float32
    L = DEPTH

    def nrm(k, shape, scale):
        return jax.random.normal(k, shape, f32) * scale

    def gain(k, shape):
        return 1.0 + 0.02 * jax.random.normal(k, shape, f32)

    b_i = nrm(ks[4], (L, ML_HEADS), 0.1)
    b_f = jnp.linspace(3.0, 6.0, ML_HEADS, dtype=f32)[None, :] + nrm(ks[5], (L, ML_HEADS), 0.1)
    return {
        "x": nrm(ks[0], (BATCH, SEQ, D_MODEL), 1.0),
        "mem": nrm(ks[1], (BATCH, N_MEM, D_MODEL), 1.0),
        "g_mix": gain(ks[2], (L, D_MODEL)),
        "w_in": nrm(ks[3], (L, D_MODEL, N_IN), D_MODEL ** -0.5),
        "b_if": jnp.concatenate([b_i, b_f], axis=-1),
        "b_gate": nrm(ks[6], (L, N_BRANCH * D_MODEL), 0.02),
        "conv_w": nrm(ks[7], (L, CONV_W, 2 * ML_W), CONV_W ** -0.5),
        "conv_b": nrm(ks[8], (L, 2 * ML_W), 0.02),
        "ml_norm_g": gain(ks[9], (L, ML_W)),
        "g_mem": gain(ks[10], (L, D_MODEL)),
        "w_mem_kv": nrm(ks[11], (L, D_MODEL, 2 * X_W), D_MODEL ** -0.5),
        "q_norm_g": gain(ks[12], (L, X_HD)),
        "k_norm_g": gain(ks[13], (L, X_HD)),
        "w_sb_proj": nrm(ks[14], (L, SB_W, D_MODEL), SB_W ** -0.5),
        "w_ml_proj": nrm(ks[15], (L, ML_W, D_MODEL), ML_W ** -0.5),
        "w_x_proj": nrm(ks[16], (L, X_W, D_MODEL), X_W ** -0.5),
        "w_out": nrm(ks[17], (L, D_MODEL, D_MODEL), D_MODEL ** -0.5),
        "g_mlp": gain(ks[18], (L, D_MODEL)),
        "w_ff1": nrm(ks[19], (L, D_MODEL, D_FF), D_MODEL ** -0.5),
        "w_ff2": nrm(ks[20], (L, D_FF, D_MODEL), D_FF ** -0.5),
    }


def _fwd_reference(x, mem, g_mix, w_in, b_if, b_gate, conv_w, conv_b, ml_norm_g, g_mem, w_mem_kv,
              q_norm_g, k_norm_g, w_sb_proj, w_ml_proj, w_x_proj, w_out, g_mlp, w_ff1, w_ff2):
    dt = x.dtype
    bsz, seq, _ = x.shape
    for l in range(DEPTH):
        h = _rmsnorm(x, g_mix[l])
        z = h @ w_in[l]
        (sb_q, sb_k, sb_v, ml_q, ml_k, ml_v, ml_o, ml_i, ml_f, x_q, gate_pre) = jnp.split(z, IN_SPLITS, axis=-1)

        y_sb = _merge_heads(_stick_breaking(_split_heads(sb_q, SB_HEADS), _split_heads(sb_k, SB_HEADS),
                                            _split_heads(sb_v, SB_HEADS))).astype(dt)

        qk = jax.nn.silu(_causal_conv(jnp.concatenate([ml_q, ml_k], axis=-1), conv_w[l], conv_b[l]))
        mq, mk = jnp.split(qk, 2, axis=-1)
        i_pre = (ml_i + b_if[l, :ML_HEADS]).transpose(0, 2, 1)
        f_pre = (ml_f + b_if[l, ML_HEADS:]).transpose(0, 2, 1)
        hm = _mlstm(_split_heads(mq, ML_HEADS), _split_heads(mk, ML_HEADS), _split_heads(ml_v, ML_HEADS), i_pre, f_pre)
        hm = _rmsnorm(hm, ml_norm_g[l].reshape(ML_HEADS, 1, ML_HD))
        y_ml = (_merge_heads(hm) * jax.nn.sigmoid(ml_o.astype(jnp.float32))).astype(dt)

        kv = _rmsnorm(mem, g_mem[l]) @ w_mem_kv[l]
        mem_k, mem_v = jnp.split(kv, 2, axis=-1)
        y_x = _merge_heads(_cross_attention(_split_heads(x_q, X_HEADS), _split_heads(mem_k, X_HEADS),
                                            _split_heads(mem_v, X_HEADS), q_norm_g[l], k_norm_g[l])).astype(dt)

        gates = jax.nn.sigmoid(gate_pre + b_gate[l]).reshape(bsz, seq, N_BRANCH, D_MODEL)
        mixed = (gates[:, :, 0] * (y_sb @ w_sb_proj[l])
                 + gates[:, :, 1] * (y_ml @ w_ml_proj[l])
                 + gates[:, :, 2] * (y_x @ w_x_proj[l]))
        x = x + mixed @ w_out[l]

        u = _rmsnorm(x, g_mlp[l]) @ w_ff1[l]
        x = x + jnp.square(jax.nn.relu(u)) @ w_ff2[l]
    return x


import jax as _jax
import jax.numpy as _jnp

TWIN_FORMAT = 'train_step'
FWD_PARAMS = ['x', 'mem', 'g_mix', 'w_in', 'b_if', 'b_gate', 'conv_w', 'conv_b', 'ml_norm_g', 'g_mem', 'w_mem_kv', 'q_norm_g', 'k_norm_g', 'w_sb_proj', 'w_ml_proj', 'w_x_proj', 'w_out', 'g_mlp', 'w_ff1', 'w_ff2']
TWIN_WEIGHTS = ['g_mix', 'w_in', 'b_if', 'b_gate', 'conv_w', 'conv_b', 'ml_norm_g', 'g_mem', 'w_mem_kv', 'q_norm_g', 'k_norm_g', 'w_sb_proj', 'w_ml_proj', 'w_x_proj', 'w_out', 'g_mlp', 'w_ff1', 'w_ff2']
TWIN_DIFF_INPUT = 'x'
TWIN_INPUTS = ['x', 'mem', 'g_mix', 'w_in', 'b_if', 'b_gate', 'conv_w', 'conv_b', 'ml_norm_g', 'g_mem', 'w_mem_kv', 'q_norm_g', 'k_norm_g', 'w_sb_proj', 'w_ml_proj', 'w_x_proj', 'w_out', 'g_mlp', 'w_ff1', 'w_ff2', 'loss_target', 'm_g_mix', 'm_w_in', 'm_b_if', 'm_b_gate', 'm_conv_w', 'm_conv_b', 'm_ml_norm_g', 'm_g_mem', 'm_w_mem_kv', 'm_q_norm_g', 'm_k_norm_g', 'm_w_sb_proj', 'm_w_ml_proj', 'm_w_x_proj', 'm_w_out', 'm_g_mlp', 'm_w_ff1', 'm_w_ff2', 'v_g_mix', 'v_w_in', 'v_b_if', 'v_b_gate', 'v_conv_w', 'v_conv_b', 'v_ml_norm_g', 'v_g_mem', 'v_w_mem_kv', 'v_q_norm_g', 'v_k_norm_g', 'v_w_sb_proj', 'v_w_ml_proj', 'v_w_x_proj', 'v_w_out', 'v_g_mlp', 'v_w_ff1', 'v_w_ff2']
TWIN_OUTPUTS = ['loss', 'grad_x', 'grad_g_mix', 'grad_w_in', 'grad_b_if', 'grad_b_gate', 'grad_conv_w', 'grad_conv_b', 'grad_ml_norm_g', 'grad_g_mem', 'grad_w_mem_kv', 'grad_q_norm_g', 'grad_k_norm_g', 'grad_w_sb_proj', 'grad_w_ml_proj', 'grad_w_x_proj', 'grad_w_out', 'grad_g_mlp', 'grad_w_ff1', 'grad_w_ff2', 'delta_g_mix', 'delta_w_in', 'delta_b_if', 'delta_b_gate', 'delta_conv_w', 'delta_conv_b', 'delta_ml_norm_g', 'delta_g_mem', 'delta_w_mem_kv', 'delta_q_norm_g', 'delta_k_norm_g', 'delta_w_sb_proj', 'delta_w_ml_proj', 'delta_w_x_proj', 'delta_w_out', 'delta_g_mlp', 'delta_w_ff1', 'delta_w_ff2', 'new_m_g_mix', 'new_m_w_in', 'new_m_b_if', 'new_m_b_gate', 'new_m_conv_w', 'new_m_conv_b', 'new_m_ml_norm_g', 'new_m_g_mem', 'new_m_w_mem_kv', 'new_m_q_norm_g', 'new_m_k_norm_g', 'new_m_w_sb_proj', 'new_m_w_ml_proj', 'new_m_w_x_proj', 'new_m_w_out', 'new_m_g_mlp', 'new_m_w_ff1', 'new_m_w_ff2', 'new_v_g_mix', 'new_v_w_in', 'new_v_b_if', 'new_v_b_gate', 'new_v_conv_w', 'new_v_conv_b', 'new_v_ml_norm_g', 'new_v_g_mem', 'new_v_w_mem_kv', 'new_v_q_norm_g', 'new_v_k_norm_g', 'new_v_w_sb_proj', 'new_v_w_ml_proj', 'new_v_w_x_proj', 'new_v_w_out', 'new_v_g_mlp', 'new_v_w_ff1', 'new_v_w_ff2']
TWIN_LEAF_KINDS = {'loss': 'loss', 'grad_x': 'grad_x', 'grad_g_mix': 'grad_w', 'grad_w_in': 'grad_w', 'grad_b_if': 'grad_w', 'grad_b_gate': 'grad_w', 'grad_conv_w': 'grad_w', 'grad_conv_b': 'grad_w', 'grad_ml_norm_g': 'grad_w', 'grad_g_mem': 'grad_w', 'grad_w_mem_kv': 'grad_w', 'grad_q_norm_g': 'grad_w', 'grad_k_norm_g': 'grad_w', 'grad_w_sb_proj': 'grad_w', 'grad_w_ml_proj': 'grad_w', 'grad_w_x_proj': 'grad_w', 'grad_w_out': 'grad_w', 'grad_g_mlp': 'grad_w', 'grad_w_ff1': 'grad_w', 'grad_w_ff2': 'grad_w', 'delta_g_mix': 'delta_w', 'delta_w_in': 'delta_w', 'delta_b_if': 'delta_w', 'delta_b_gate': 'delta_w', 'delta_conv_w': 'delta_w', 'delta_conv_b': 'delta_w', 'delta_ml_norm_g': 'delta_w', 'delta_g_mem': 'delta_w', 'delta_w_mem_kv': 'delta_w', 'delta_q_norm_g': 'delta_w', 'delta_k_norm_g': 'delta_w', 'delta_w_sb_proj': 'delta_w', 'delta_w_ml_proj': 'delta_w', 'delta_w_x_proj': 'delta_w', 'delta_w_out': 'delta_w', 'delta_g_mlp': 'delta_w', 'delta_w_ff1': 'delta_w', 'delta_w_ff2': 'delta_w', 'new_m_g_mix': 'new_m', 'new_m_w_in': 'new_m', 'new_m_b_if': 'new_m', 'new_m_b_gate': 'new_m', 'new_m_conv_w': 'new_m', 'new_m_conv_b': 'new_m', 'new_m_ml_norm_g': 'new_m', 'new_m_g_mem': 'new_m', 'new_m_w_mem_kv': 'new_m', 'new_m_q_norm_g': 'new_m', 'new_m_k_norm_g': 'new_m', 'new_m_w_sb_proj': 'new_m', 'new_m_w_ml_proj': 'new_m', 'new_m_w_x_proj': 'new_m', 'new_m_w_out': 'new_m', 'new_m_g_mlp': 'new_m', 'new_m_w_ff1': 'new_m', 'new_m_w_ff2': 'new_m', 'new_v_g_mix': 'new_v', 'new_v_w_in': 'new_v', 'new_v_b_if': 'new_v', 'new_v_b_gate': 'new_v', 'new_v_conv_w': 'new_v', 'new_v_conv_b': 'new_v', 'new_v_ml_norm_g': 'new_v', 'new_v_g_mem': 'new_v', 'new_v_w_mem_kv': 'new_v', 'new_v_q_norm_g': 'new_v', 'new_v_k_norm_g': 'new_v', 'new_v_w_sb_proj': 'new_v', 'new_v_w_ml_proj': 'new_v', 'new_v_w_x_proj': 'new_v', 'new_v_w_out': 'new_v', 'new_v_g_mlp': 'new_v', 'new_v_w_ff1': 'new_v', 'new_v_w_ff2': 'new_v'}


def _forward(args):
    return _fwd_reference(*[args[k] for k in FWD_PARAMS])


def _output_shape():
    out = _jax.eval_shape(lambda: _forward(_fwd_setup_inputs(0)))
    return out.shape, out.dtype

N_MICROBATCH = 1
ADAM_LR = 0.001
ADAM_B1 = 0.9
ADAM_B2 = 0.999
ADAM_EPS = 1e-08
ADAM_WD = 0.01
ADAM_STEP = 10
PER_EXAMPLE_BATCH_AXIS = {'x': 0, 'mem': 0, 'loss_target': 0}
SHARED_INPUTS = []
_WEIGHT_DTYPES = {'g_mix': _jnp.float32, 'w_in': _jnp.float32, 'b_if': _jnp.float32, 'b_gate': _jnp.float32, 'conv_w': _jnp.float32, 'conv_b': _jnp.float32, 'ml_norm_g': _jnp.float32, 'g_mem': _jnp.float32, 'w_mem_kv': _jnp.float32, 'q_norm_g': _jnp.float32, 'k_norm_g': _jnp.float32, 'w_sb_proj': _jnp.float32, 'w_ml_proj': _jnp.float32, 'w_x_proj': _jnp.float32, 'w_out': _jnp.float32, 'g_mlp': _jnp.float32, 'w_ff1': _jnp.float32, 'w_ff2': _jnp.float32}
MOMENT_SCALE = {'g_mix': 6.842149e+00, 'w_in': 6.372304e-01, 'b_if': 8.217386e+00, 'b_gate': 1.028022e+00, 'conv_w': 1.643387e-01, 'conv_b': 2.775135e-01, 'ml_norm_g': 3.135058e+00, 'g_mem': 3.925718e-01, 'w_mem_kv': 2.582692e-01, 'q_norm_g': 3.104458e-01, 'k_norm_g': 3.147593e-01, 'w_sb_proj': 3.178221e-01, 'w_ml_proj': 1.535022e+00, 'w_x_proj': 4.008694e-01, 'w_out': 1.625674e+00, 'g_mlp': 9.547149e+01, 'w_ff1': 9.634068e-01, 'w_ff2': 7.801572e+00}


def _to_microbatches(a, axis):
    t = _jnp.moveaxis(a, axis, 0)
    t = t.reshape((N_MICROBATCH, t.shape[0] // N_MICROBATCH) + t.shape[1:])
    return _jnp.moveaxis(t, 1, axis + 1)


def setup_inputs(seed: int = 0) -> dict:
    inp = _fwd_setup_inputs(seed)
    key = _jax.random.fold_in(_jax.random.key(seed), 7919)
    shape, _ = _output_shape()
    out = dict(inp)
    out["loss_target"] = _jax.random.normal(_jax.random.fold_in(key, 0), shape, _jnp.float32)
    for i, name in enumerate(TWIN_WEIGHTS):
        w = inp[name].astype(_jnp.float32)
        if MOMENT_SCALE is None:
            s = _jnp.sqrt(_jnp.mean(_jnp.square(w)) + 1e-30)
        else:
            s = MOMENT_SCALE[name]
        km, kv = _jax.random.split(_jax.random.fold_in(key, i + 1))
        out[name] = w
        out["m_" + name] = s * _jax.random.normal(km, w.shape, _jnp.float32)
        out["v_" + name] = (s * s) * _jax.random.uniform(kv, w.shape, _jnp.float32, 0.5, 1.5)
    if N_MICROBATCH > 1:
        for name, axis in PER_EXAMPLE_BATCH_AXIS.items():
            out[name] = _to_microbatches(out[name], axis)
    return {'x': out['x'], 'mem': out['mem'], 'g_mix': out['g_mix'], 'w_in': out['w_in'], 'b_if': out['b_if'], 'b_gate': out['b_gate'], 'conv_w': out['conv_w'], 'conv_b': out['conv_b'], 'ml_norm_g': out['ml_norm_g'], 'g_mem': out['g_mem'], 'w_mem_kv': out['w_mem_kv'], 'q_norm_g': out['q_norm_g'], 'k_norm_g': out['k_norm_g'], 'w_sb_proj': out['w_sb_proj'], 'w_ml_proj': out['w_ml_proj'], 'w_x_proj': out['w_x_proj'], 'w_out': out['w_out'], 'g_mlp': out['g_mlp'], 'w_ff1': out['w_ff1'], 'w_ff2': out['w_ff2'], 'loss_target': out['loss_target'], 'm_g_mix': out['m_g_mix'], 'm_w_in': out['m_w_in'], 'm_b_if': out['m_b_if'], 'm_b_gate': out['m_b_gate'], 'm_conv_w': out['m_conv_w'], 'm_conv_b': out['m_conv_b'], 'm_ml_norm_g': out['m_ml_norm_g'], 'm_g_mem': out['m_g_mem'], 'm_w_mem_kv': out['m_w_mem_kv'], 'm_q_norm_g': out['m_q_norm_g'], 'm_k_norm_g': out['m_k_norm_g'], 'm_w_sb_proj': out['m_w_sb_proj'], 'm_w_ml_proj': out['m_w_ml_proj'], 'm_w_x_proj': out['m_w_x_proj'], 'm_w_out': out['m_w_out'], 'm_g_mlp': out['m_g_mlp'], 'm_w_ff1': out['m_w_ff1'], 'm_w_ff2': out['m_w_ff2'], 'v_g_mix': out['v_g_mix'], 'v_w_in': out['v_w_in'], 'v_b_if': out['v_b_if'], 'v_b_gate': out['v_b_gate'], 'v_conv_w': out['v_conv_w'], 'v_conv_b': out['v_conv_b'], 'v_ml_norm_g': out['v_ml_norm_g'], 'v_g_mem': out['v_g_mem'], 'v_w_mem_kv': out['v_w_mem_kv'], 'v_q_norm_g': out['v_q_norm_g'], 'v_k_norm_g': out['v_k_norm_g'], 'v_w_sb_proj': out['v_w_sb_proj'], 'v_w_ml_proj': out['v_w_ml_proj'], 'v_w_x_proj': out['v_w_x_proj'], 'v_w_out': out['v_w_out'], 'v_g_mlp': out['v_g_mlp'], 'v_w_ff1': out['v_w_ff1'], 'v_w_ff2': out['v_w_ff2']}


def _loss(weights, diff, rest, loss_target):
    with _jax.named_scope("forward"):
        args = {**rest, TWIN_DIFF_INPUT: diff, **{k: w.astype(_WEIGHT_DTYPES[k]) for k, w in weights.items()}}
        y = _forward(args)
    with _jax.named_scope("loss_head"):
        err = _jnp.square(y.astype(_jnp.float32) - loss_target)
        return 0.5 * _jnp.sum(_jnp.mean(err, axis=-1)) if err.ndim else 0.5 * err


def _adamw(w, g, m, v):
    m = ADAM_B1 * m + (1.0 - ADAM_B1) * g
    v = ADAM_B2 * v + (1.0 - ADAM_B2) * _jnp.square(g)
    m_hat = m / (1.0 - ADAM_B1 ** ADAM_STEP)
    v_hat = v / (1.0 - ADAM_B2 ** ADAM_STEP)
    delta = -ADAM_LR * (m_hat / (_jnp.sqrt(v_hat) + ADAM_EPS) + ADAM_WD * w)
    return delta, m, v


def reference(x, mem, g_mix, w_in, b_if, b_gate, conv_w, conv_b, ml_norm_g, g_mem, w_mem_kv, q_norm_g, k_norm_g, w_sb_proj, w_ml_proj, w_x_proj, w_out, g_mlp, w_ff1, w_ff2, loss_target, m_g_mix, m_w_in, m_b_if, m_b_gate, m_conv_w, m_conv_b, m_ml_norm_g, m_g_mem, m_w_mem_kv, m_q_norm_g, m_k_norm_g, m_w_sb_proj, m_w_ml_proj, m_w_x_proj, m_w_out, m_g_mlp, m_w_ff1, m_w_ff2, v_g_mix, v_w_in, v_b_if, v_b_gate, v_conv_w, v_conv_b, v_ml_norm_g, v_g_mem, v_w_mem_kv, v_q_norm_g, v_k_norm_g, v_w_sb_proj, v_w_ml_proj, v_w_x_proj, v_w_out, v_g_mlp, v_w_ff1, v_w_ff2):
    given = dict(x=x, mem=mem, g_mix=g_mix, w_in=w_in, b_if=b_if, b_gate=b_gate, conv_w=conv_w, conv_b=conv_b, ml_norm_g=ml_norm_g, g_mem=g_mem, w_mem_kv=w_mem_kv, q_norm_g=q_norm_g, k_norm_g=k_norm_g, w_sb_proj=w_sb_proj, w_ml_proj=w_ml_proj, w_x_proj=w_x_proj, w_out=w_out, g_mlp=g_mlp, w_ff1=w_ff1, w_ff2=w_ff2, loss_target=loss_target, m_g_mix=m_g_mix, m_w_in=m_w_in, m_b_if=m_b_if, m_b_gate=m_b_gate, m_conv_w=m_conv_w, m_conv_b=m_conv_b, m_ml_norm_g=m_ml_norm_g, m_g_mem=m_g_mem, m_w_mem_kv=m_w_mem_kv, m_q_norm_g=m_q_norm_g, m_k_norm_g=m_k_norm_g, m_w_sb_proj=m_w_sb_proj, m_w_ml_proj=m_w_ml_proj, m_w_x_proj=m_w_x_proj, m_w_out=m_w_out, m_g_mlp=m_g_mlp, m_w_ff1=m_w_ff1, m_w_ff2=m_w_ff2, v_g_mix=v_g_mix, v_w_in=v_w_in, v_b_if=v_b_if, v_b_gate=v_b_gate, v_conv_w=v_conv_w, v_conv_b=v_conv_b, v_ml_norm_g=v_ml_norm_g, v_g_mem=v_g_mem, v_w_mem_kv=v_w_mem_kv, v_q_norm_g=v_q_norm_g, v_k_norm_g=v_k_norm_g, v_w_sb_proj=v_w_sb_proj, v_w_ml_proj=v_w_ml_proj, v_w_x_proj=v_w_x_proj, v_w_out=v_w_out, v_g_mlp=v_g_mlp, v_w_ff1=v_w_ff1, v_w_ff2=v_w_ff2)
    weights = {n: given[n] for n in TWIN_WEIGHTS}
    shared = {n: given[n] for n in SHARED_INPUTS}
    per_example = {n: given[n] for n in ['x', 'mem']}
    grad_fn = _jax.value_and_grad(_loss, argnums=(0, 1))

    def one_microbatch(ex, loss_target):
        ex = dict(ex)
        diff = ex.pop(TWIN_DIFF_INPUT)
        return grad_fn(weights, diff, {**shared, **ex}, loss_target)

    if N_MICROBATCH == 1:
        loss, (grad_w, grad_x) = one_microbatch(per_example, given["loss_target"])
    else:
        def body(carry, xs):
            loss_sum, grad_sum = carry
            l_k, (gw_k, gx_k) = one_microbatch(xs[0], xs[1])
            with _jax.named_scope("update"):
                return (loss_sum + l_k, _jax.tree.map(_jnp.add, grad_sum, gw_k)), gx_k

        init = (_jnp.zeros((), _jnp.float32), _jax.tree.map(_jnp.zeros_like, weights))
        (loss, grad_w), grad_x = _jax.lax.scan(body, init, (per_example, given["loss_target"]))
    with _jax.named_scope("update"):
        delta_w, new_m, new_v = {}, {}, {}
        for n in TWIN_WEIGHTS:
            delta_w[n], new_m[n], new_v[n] = _adamw(weights[n], grad_w[n], given["m_" + n], given["v_" + n])
    return (loss, grad_x, *[grad_w[n] for n in TWIN_WEIGHTS], *[delta_w[n] for n in TWIN_WEIGHTS],
            *[new_m[n] for n in TWIN_WEIGHTS], *[new_v[n] for n in TWIN_WEIGHTS])
```

```python
import functools

import jax
import jax.numpy as jnp
from jax import lax
from jax.experimental import pallas as pl
from jax.experimental.pallas import tpu as pltpu

F32 = jnp.float32
BF16 = jnp.bfloat16
MESH = pl.DeviceIdType.MESH

EPS = 1e-6
SB_HD = 128
ML_HEADS = 4
X_HEADS = 4
CHUNK = 64
CONV_W = 4
LANES = 128
ADAM_LR = 0.001
ADAM_B1 = 0.9
ADAM_B2 = 0.999
ADAM_EPS = 1e-08
ADAM_WD = 0.01
ADAM_STEP = 10
VMEM_CAP = 56 * 1024 * 1024
NEG = -1e30

NT = (((1,), (1,)), ((), ()))
NN = (((1,), (0,)), ((), ()))
TN = (((0,), (0,)), ((), ()))


def _dot(a, b, dn=NN):
    return lax.dot_general(a.astype(BF16), b.astype(BF16), dn, preferred_element_type=F32)


def _dot01(x, u, dn=NN):
    hi = x.astype(BF16)
    lo = (x - hi.astype(F32)).astype(BF16)
    return (lax.dot_general(hi, u, dn, preferred_element_type=F32)
            + lax.dot_general(lo, u, dn, preferred_element_type=F32))


def _u01dot(u, x):
    hi = x.astype(BF16)
    lo = (x - hi.astype(F32)).astype(BF16)
    return (lax.dot_general(u, hi, NN, preferred_element_type=F32)
            + lax.dot_general(u, lo, NN, preferred_element_type=F32))


def _pick(n, cands):
    for c in cands:
        if c <= n and n % c == 0:
            return c
    return n


def _nbytes(shape, dtype):
    n = 1
    for s in shape:
        n *= s
    return n * jnp.dtype(dtype).itemsize


def _params(vmem_bytes):
    return pltpu.CompilerParams(vmem_limit_bytes=int(min(VMEM_CAP, max(vmem_bytes, 16 * 1024 * 1024))))


def _softplus(z):
    return jnp.maximum(z, 0.0) + jnp.log(1.0 + jnp.exp(-jnp.abs(z)))


def _sigmoid(z):
    return 1.0 / (1.0 + jnp.exp(-z))


def _rms_fwd(xv, g):
    r = lax.rsqrt(jnp.mean(xv * xv, axis=-1, keepdims=True) + EPS)
    return xv * r * g


def _rms_bwd(xv, g, dy):
    r = lax.rsqrt(jnp.mean(xv * xv, axis=-1, keepdims=True) + EPS)
    xh = xv * r
    dxh = dy * g
    dx = r * (dxh - xh * jnp.mean(dxh * xh, axis=-1, keepdims=True))
    return dx, dy * xh


def _mm(a, b, *, name, ta=False, tb=False, add=None, out_dtype=F32, bm=1024, bn=1024, bk=512):
    m, k = (a.shape[1], a.shape[0]) if ta else a.shape
    n = b.shape[0] if tb else b.shape[1]
    tm = _pick(m, (bm, 512, 256, 128))
    tn = _pick(n, (bn, 512, 256, 128))
    tk = _pick(k, (bk, 256, 128))
    nk = k // tk
    dn = (((0 if ta else 1,), (1 if tb else 0,)), ((), ()))
    has_add = add is not None

    def body(*refs):
        if has_add:
            a_ref, b_ref, c_ref, o_ref, acc_ref = refs
        else:
            a_ref, b_ref, o_ref, acc_ref = refs
        kk = pl.program_id(2)

        @pl.when(kk == 0)
        def _():
            acc_ref[...] = jnp.zeros_like(acc_ref)

        acc_ref[...] += lax.dot_general(a_ref[...].astype(BF16), b_ref[...].astype(BF16), dn,
                                        preferred_element_type=F32)

        @pl.when(kk == nk - 1)
        def _():
            r = acc_ref[...]
            if has_add:
                r = r + c_ref[...].astype(F32)
            o_ref[...] = r.astype(out_dtype)

    a_spec = pl.BlockSpec((tk, tm), lambda i, j, q: (q, i)) if ta else pl.BlockSpec((tm, tk), lambda i, j, q: (i, q))
    b_spec = pl.BlockSpec((tn, tk), lambda i, j, q: (j, q)) if tb else pl.BlockSpec((tk, tn), lambda i, j, q: (q, j))
    o_spec = pl.BlockSpec((tm, tn), lambda i, j, q: (i, j))
    ins, specs = [a, b], [a_spec, b_spec]
    vm = 2 * (_nbytes((tm, tk), a.dtype) + _nbytes((tk, tn), b.dtype) + _nbytes((tm, tn), out_dtype)) \
        + 3 * _nbytes((tm, tn), F32) + _nbytes((tm, tk), BF16) + _nbytes((tk, tn), BF16)
    if has_add:
        ins.append(add)
        specs.append(o_spec)
        vm += 2 * _nbytes((tm, tn), add.dtype)
    return pl.pallas_call(
        body, name=name, grid=(m // tm, n // tn, nk), in_specs=specs, out_specs=o_spec,
        out_shape=jax.ShapeDtypeStruct((m, n), out_dtype), scratch_shapes=[pltpu.VMEM((tm, tn), F32)],
        compiler_params=_params(vm + (4 << 20)),
    )(*ins)


def _rowwise(fn, rows, consts, outs, reds=(), *, name, tr=256, temps=6):
    rows = [r if isinstance(r, tuple) else (r, r.shape[1], 0) for r in rows]
    nrows = rows[0][0].shape[0]
    t = _pick(nrows, (tr, 128, 64, 32, 16, 8))
    nr, nc, no = len(rows), len(consts), len(outs)

    def body(*refs):
        rin, cin = refs[:nr], refs[nr:nr + nc]
        oref, rref = refs[nr + nc:nr + nc + no], refs[nr + nc + no:]
        res = fn(*[r[...] for r in rin], *[c[...] for c in cin])
        if not isinstance(res, (tuple, list)):
            res = (res,)
        for o, v in zip(oref, res[:no]):
            o[...] = v.astype(o.dtype)
        if rref:
            @pl.when(pl.program_id(0) == 0)
            def _():
                for r in rref:
                    r[...] = jnp.zeros_like(r)

            for r, v in zip(rref, res[no:]):
                r[...] += v

    in_specs = [pl.BlockSpec((t, w), functools.partial(lambda i, ci: (i, ci), ci=ci)) for (_, w, ci) in rows]
    in_specs += [pl.BlockSpec(c.shape, functools.partial(lambda i, nd: (0,) * nd, nd=c.ndim)) for c in consts]
    out_specs = [pl.BlockSpec((t, w), lambda i: (i, 0)) for (w, _) in outs]
    out_specs += [pl.BlockSpec((1, w), lambda i: (0, 0)) for w in reds]
    out_shape = [jax.ShapeDtypeStruct((nrows, w), dt) for (w, dt) in outs]
    out_shape += [jax.ShapeDtypeStruct((1, w), F32) for w in reds]
    widest = max([w for (_, w, _) in rows] + [w for (w, _) in outs])
    vm = 2 * sum(_nbytes((t, w), a.dtype) for (a, w, _) in rows) + 2 * sum(_nbytes((t, w), dt) for (w, dt) in outs)
    vm += temps * _nbytes((t, widest), F32) + (2 << 20)
    res = pl.pallas_call(
        body, name=name, grid=(nrows // t,), in_specs=in_specs, out_specs=out_specs, out_shape=out_shape,
        compiler_params=_params(vm),
    )(*[a for (a, _, _) in rows], *consts)
    return list(res)


def _sb_fwd(zm, heads, *, name, t=128):
    s = zm.shape[0]
    nq = s // t
    scale = SB_HD ** -0.5

    def body(q_ref, k_ref, v_ref, o_ref, lt_ref):
        i = pl.program_id(1)
        qb = q_ref[...].astype(BF16)
        r = lax.broadcasted_iota(jnp.int32, (t, t), 0)
        c = lax.broadcasted_iota(jnp.int32, (t, t), 1)
        causal = c < r
        usuf = (r > c).astype(BF16)

        def tile(j, carry, diag):
            acc, cl = carry
            rows = pl.ds(pl.multiple_of(j * t, t), t)
            kb = k_ref[rows, :].astype(BF16)
            vb = v_ref[rows, :].astype(BF16)
            z = lax.dot_general(qb, kb, NT, preferred_element_type=F32) * scale
            lsig = -_softplus(z)
            l = jnp.where(causal, lsig, 0.0) if diag else lsig
            loga = z + lsig + _dot01(l, usuf) + cl
            if diag:
                loga = jnp.where(causal, loga, NEG)
            a = jnp.exp(loga)
            acc = acc + lax.dot_general(a.astype(BF16), vb, NN, preferred_element_type=F32)
            return acc, cl + jnp.sum(l, axis=1, keepdims=True)

        carry = tile(i, (jnp.zeros((t, SB_HD), F32), jnp.zeros((t, 1), F32)), True)
        acc, cl = lax.fori_loop(0, i, lambda n, cr: tile(i - 1 - n, cr, False), carry)
        o_ref[...] = acc.astype(o_ref.dtype)
        lt_ref[...] = jnp.broadcast_to(cl, (t, LANES))

    blk = lambda off: pl.BlockSpec((s, SB_HD), functools.partial(lambda h, i, off: (0, off + h), off=off))
    return pl.pallas_call(
        body, name=name, grid=(heads, nq),
        in_specs=[pl.BlockSpec((t, SB_HD), lambda h, i: (i, h)), blk(heads), blk(2 * heads)],
        out_specs=[pl.BlockSpec((t, SB_HD), lambda h, i: (i, h)), pl.BlockSpec((t, LANES), lambda h, i: (i, h))],
        out_shape=[jax.ShapeDtypeStruct((s, heads * SB_HD), BF16), jax.ShapeDtypeStruct((s, heads * LANES), F32)],
        compiler_params=_params(8 * s * SB_HD * 4 + (8 << 20)),
    )(zm, zm, zm)


def _sb_bwd(zm, dy, ltot, heads, *, name, t=128):
    s = zm.shape[0]
    nq = s // t
    scale = SB_HD ** -0.5

    def body(q_ref, k_ref, v_ref, do_ref, lt_ref, dq_ref, dk_ref, dv_ref, dka, dva):
        i = pl.program_id(1)

        @pl.when(i == 0)
        def _():
            dka[...] = jnp.zeros_like(dka)
            dva[...] = jnp.zeros_like(dva)

        qb = q_ref[...].astype(BF16)
        dob = do_ref[...].astype(BF16)
        ltot_c = lt_ref[:, 0:1]
        r = lax.broadcasted_iota(jnp.int32, (t, t), 0)
        c = lax.broadcasted_iota(jnp.int32, (t, t), 1)
        causal = c < r
        uincl = (r <= c).astype(BF16)
        uexcl = (r < c).astype(BF16)

        def tile(j, carry, diag):
            dq, cl, cg = carry
            rows = pl.ds(pl.multiple_of(j * t, t), t)
            kb = k_ref[rows, :].astype(BF16)
            vb = v_ref[rows, :].astype(BF16)
            z = lax.dot_general(qb, kb, NT, preferred_element_type=F32) * scale
            lsig = -_softplus(z)
            l = jnp.where(causal, lsig, 0.0) if diag else lsig
            later = ltot_c - (cl + _dot01(l, uincl))
            loga = z + lsig + later
            if diag:
                loga = jnp.where(causal, loga, NEG)
            a = jnp.exp(loga)
            sig = jnp.exp(z + lsig)
            g = a * lax.dot_general(dob, vb, NT, preferred_element_type=F32)
            p = cg + _dot01(g, uexcl)
            dz = g * (1.0 - sig) - p * sig
            if diag:
                dz = jnp.where(causal, dz, 0.0)
            dzb = (dz * scale).astype(BF16)
            dva[rows, :] += lax.dot_general(a.astype(BF16), dob, TN, preferred_element_type=F32)
            dka[rows, :] += lax.dot_general(dzb, qb, TN, preferred_element_type=F32)
            dq = dq + lax.dot_general(dzb, kb, NN, preferred_element_type=F32)
            return dq, cl + jnp.sum(l, axis=1, keepdims=True), cg + jnp.sum(g, axis=1, keepdims=True)

        init = (jnp.zeros((t, SB_HD), F32), jnp.zeros((t, 1), F32), jnp.zeros((t, 1), F32))
        carry = lax.fori_loop(0, i, lambda j, cr: tile(j, cr, False), init)
        dq, _, _ = tile(i, carry, True)
        dq_ref[...] = dq.astype(dq_ref.dtype)

        @pl.when(i == nq - 1)
        def _():
            dk_ref[...] = dka[...].astype(dk_ref.dtype)
            dv_ref[...] = dva[...].astype(dv_ref.dtype)

    blk = lambda off: pl.BlockSpec((s, SB_HD), functools.partial(lambda h, i, off: (0, off + h), off=off))
    tile_spec = pl.BlockSpec((t, SB_HD), lambda h, i: (i, h))
    full = jax.ShapeDtypeStruct((s, heads * SB_HD), BF16)
    return pl.pallas_call(
        body, name=name, grid=(heads, nq),
        in_specs=[tile_spec, blk(heads), blk(2 * heads), tile_spec, pl.BlockSpec((t, LANES), lambda h, i: (i, h))],
        out_specs=[tile_spec, blk(0), blk(0)],
        out_shape=[full, full, full],
        scratch_shapes=[pltpu.VMEM((s, SB_HD), F32), pltpu.VMEM((s, SB_HD), F32)],
        compiler_params=_params(12 * s * SB_HD * 4 + (8 << 20)),
    )(zm, zm, zm, dy, ltot)


def _conv_taps(u, w_ref, rows_i):
    taps = []
    for j in range(CONV_W):
        sh = CONV_W - 1 - j
        if sh == 0:
            taps.append(u)
        else:
            taps.append(jnp.where(rows_i >= sh, pltpu.roll(u, sh, 0), 0.0))
    return taps


def _conv_fwd(zm, col0, width, cw, cb, *, name):
    s = zm.shape[0]
    bw = _pick(width, (LANES,))
    off = col0 // bw

    def body(u_ref, w_ref, b_ref, o_ref):
        u = u_ref[...]
        rows_i = lax.broadcasted_iota(jnp.int32, u.shape, 0)
        acc = jnp.broadcast_to(b_ref[...], u.shape)
        for j, tp in enumerate(_conv_taps(u, w_ref, rows_i)):
            acc = acc + tp * w_ref[j:j + 1, :]
        o_ref[...] = acc * _sigmoid(acc)

    return pl.pallas_call(
        body, name=name, grid=(width // bw,),
        in_specs=[pl.BlockSpec((s, bw), lambda j: (0, off + j)), pl.BlockSpec((CONV_W, bw), lambda j: (0, j)),
                  pl.BlockSpec((1, bw), lambda j: (0, j))],
        out_specs=pl.BlockSpec((s, bw), lambda j: (0, j)),
        out_shape=jax.ShapeDtypeStruct((s, width), F32),
        compiler_params=_params(12 * s * bw * 4 + (4 << 20)),
    )(zm, cw, cb)


def _conv_bwd(zm, col0, width, cw, cb, dqk, *, name):
    s = zm.shape[0]
    bw = _pick(width, (LANES,))
    off = col0 // bw

    def body(u_ref, w_ref, b_ref, d_ref, du_ref, dw_ref, db_ref):
        u = u_ref[...]
        rows_i = lax.broadcasted_iota(jnp.int32, u.shape, 0)
        taps = _conv_taps(u, w_ref, rows_i)
        acc = jnp.broadcast_to(b_ref[...], u.shape)
        for j, tp in enumerate(taps):
            acc = acc + tp * w_ref[j:j + 1, :]
        sg = _sigmoid(acc)
        dc = d_ref[...] * (sg * (1.0 + acc * (1.0 - sg)))
        du = jnp.zeros_like(u)
        for j in range(CONV_W):
            sh = CONV_W - 1 - j
            if sh == 0:
                du = du + dc * w_ref[j:j + 1, :]
            else:
                du = du + jnp.where(rows_i < s - sh, pltpu.roll(dc, s - sh, 0), 0.0) * w_ref[j:j + 1, :]
            dw_ref[j:j + 1, :] = jnp.sum(dc * taps[j], axis=0, keepdims=True)
        du_ref[...] = du.astype(du_ref.dtype)
        db_ref[...] = jnp.sum(dc, axis=0, keepdims=True)

    return pl.pallas_call(
        body, name=name, grid=(width // bw,),
        in_specs=[pl.BlockSpec((s, bw), lambda j: (0, off + j)), pl.BlockSpec((CONV_W, bw), lambda j: (0, j)),
                  pl.BlockSpec((1, bw), lambda j: (0, j)), pl.BlockSpec((s, bw), lambda j: (0, j))],
        out_specs=[pl.BlockSpec((s, bw), lambda j: (0, j)), pl.BlockSpec((CONV_W, bw), lambda j: (0, j)),
                   pl.BlockSpec((1, bw), lambda j: (0, j))],
        out_shape=[jax.ShapeDtypeStruct((s, width), BF16), jax.ShapeDtypeStruct((CONV_W, width), F32),
                   jax.ShapeDtypeStruct((1, width), F32)],
        compiler_params=_params(20 * s * bw * 4 + (4 << 20)),
    )(zm, cw, cb, dqk)


def _ml_gates(gcol_ref, grow_ref):
    l = CHUNK
    r = lax.broadcasted_iota(jnp.int32, (l, l), 0)
    c = lax.broadcasted_iota(jnp.int32, (l, l), 1)
    gcol = gcol_ref[...]
    grow = grow_ref[0]
    bcol = _u01dot((c <= r).astype(BF16), gcol)
    brow = _dot01(grow, (r <= c).astype(BF16))
    return gcol, grow, bcol, brow, r >= c


def _ml_chunk(h, dh, mq_ref, mk_ref, v_ref, gates, cp, n_prev, m_prev):
    gcol, grow, bcol, brow, tri = gates
    l = CHUNK
    sl = slice(h * dh, (h + 1) * dh)
    qc = mq_ref[:, sl]
    kc = mk_ref[:, sl] * (dh ** -0.5)
    vc = v_ref[:, sl]
    i_row = grow[h:h + 1, :]
    i_col = gcol[:, h:h + 1]
    b_col = bcol[:, ML_HEADS + h:ML_HEADS + h + 1]
    b_row = brow[ML_HEADS + h:ML_HEADS + h + 1, :]
    b_end = b_col[l - 1:l, :]
    d = jnp.where(tri, b_col - b_row + i_row, -jnp.inf)
    m_inter = b_col + m_prev
    m_t = jnp.maximum(m_inter, jnp.max(d, axis=1, keepdims=True))
    w = jnp.exp(d - m_t)
    s_inter = jnp.exp(m_inter - m_t)
    qb, kb, vb = qc.astype(BF16), kc.astype(BF16), vc.astype(BF16)
    cpb = cp.astype(BF16)
    a = lax.dot_general(qb, kb, NT, preferred_element_type=F32)
    sc = a * w
    qcp = lax.dot_general(qb, cpb, NT, preferred_element_type=F32)
    qn = jnp.sum(qc * n_prev, axis=1, keepdims=True)
    num = lax.dot_general(sc.astype(BF16), vb, NN, preferred_element_type=F32) + s_inter * qcp
    den = jnp.sum(sc, axis=1, keepdims=True) + s_inter * qn
    floor = jnp.exp(-m_t)
    dnm = jnp.maximum(jnp.abs(den), floor)
    g_col = b_end - b_col + i_col
    g_row = b_end - b_row + i_row
    m_new = jnp.maximum(b_end + m_prev, jnp.max(g_row, axis=1, keepdims=True))
    decay = jnp.exp(b_end + m_prev - m_new)
    wk = jnp.exp(g_col - m_new)
    return dict(qc=qc, kc=kc, vc=vc, qb=qb, kb=kb, vb=vb, cpb=cpb, w=w, s_inter=s_inter, a=a, sc=sc, qcp=qcp, qn=qn,
                num=num, den=den, floor=floor, dnm=dnm, m_new=m_new, decay=decay, wk=wk, sl=sl)


def _ml_fwd(mqk, zm, vcol, gcol, grow, d_model, *, name):
    s = zm.shape[0]
    nc = s // CHUNK
    dh = d_model // ML_HEADS
    hh = ML_HEADS

    def body(mq_ref, mk_ref, v_ref, gcol_ref, grow_ref, h_ref, cs_ref, ns_ref, ms_ref, c_s, n_s, m_s):
        @pl.when(pl.program_id(0) == 0)
        def _():
            c_s[...] = jnp.zeros_like(c_s)
            n_s[...] = jnp.zeros_like(n_s)
            m_s[...] = jnp.zeros_like(m_s)

        gates = _ml_gates(gcol_ref, grow_ref)
        for h in range(hh):
            cp, n_prev, m_prev = c_s[h], n_s[h], m_s[h][:, 0:1]
            cs_ref[0, h] = cp
            ns_ref[0, h] = n_prev
            ms_ref[0, h] = m_s[h]
            f = _ml_chunk(h, dh, mq_ref, mk_ref, v_ref, gates, cp, n_prev, m_prev)
            h_ref[:, f["sl"]] = f["num"] / f["dnm"]
            c_s[h] = f["decay"] * cp + lax.dot_general((f["vc"] * f["wk"]).astype(BF16), f["kb"], TN,
                                                       preferred_element_type=F32)
            n_s[h] = f["decay"] * n_prev + jnp.sum(f["wk"] * f["kc"], axis=0, keepdims=True)
            m_s[h] = jnp.broadcast_to(f["m_new"], (1, LANES))

    dblk = d_model
    return pl.pallas_call(
        body, name=name, grid=(nc,),
        in_specs=[pl.BlockSpec((CHUNK, dblk), lambda c: (c, 0)), pl.BlockSpec((CHUNK, dblk), lambda c: (c, 1)),
                  pl.BlockSpec((CHUNK, dblk), lambda c: (c, vcol // dblk)),
                  pl.BlockSpec((CHUNK, LANES), lambda c: (c, 0)), pl.BlockSpec((1, 8, CHUNK), lambda c: (c, 0, 0))],
        out_specs=[pl.BlockSpec((CHUNK, dblk), lambda c: (c, 0)),
                   pl.BlockSpec((1, hh, dh, dh), lambda c: (c, 0, 0, 0)),
                   pl.BlockSpec((1, hh, 1, dh), lambda c: (c, 0, 0, 0)),
                   pl.BlockSpec((1, hh, 1, LANES), lambda c: (c, 0, 0, 0))],
        out_shape=[jax.ShapeDtypeStruct((s, d_model), F32), jax.ShapeDtypeStruct((nc, hh, dh, dh), F32),
                   jax.ShapeDtypeStruct((nc, hh, 1, dh), F32), jax.ShapeDtypeStruct((nc, hh, 1, LANES), F32)],
        scratch_shapes=[pltpu.VMEM((hh, dh, dh), F32), pltpu.VMEM((hh, 1, dh), F32), pltpu.VMEM((hh, 1, LANES), F32)],
        compiler_params=_params(8 * hh * dh * dh * 4 + (16 << 20)),
    )(mqk, mqk, zm, gcol, grow)


def _ml_bwd(mqk, zm, vcol, gcol, grow, cs, ns, ms, dhm, d_model, *, name):
    s = zm.shape[0]
    nc = s // CHUNK
    dh = d_model // ML_HEADS
    hh = ML_HEADS
    l = CHUNK

    def body(mq_ref, mk_ref, v_ref, gcol_ref, grow_ref, cs_ref, ns_ref, ms_ref, dh_ref,
             dq_ref, dk_ref, dv_ref, dgc_ref, dgr_ref, dc_s, dn_s):
        @pl.when(pl.program_id(0) == 0)
        def _():
            dc_s[...] = jnp.zeros_like(dc_s)
            dn_s[...] = jnp.zeros_like(dn_s)

        gates = _ml_gates(gcol_ref, grow_ref)
        lane = lax.broadcasted_iota(jnp.int32, (l, LANES), 1)
        rowi = lax.broadcasted_iota(jnp.int32, (8, l), 0)
        lastrow = lax.broadcasted_iota(jnp.int32, (l, 1), 0) == l - 1
        dgc = jnp.zeros((l, LANES), F32)
        dgr = jnp.zeros((8, l), F32)
        for h in range(hh):
            cp, n_prev, m_prev = cs_ref[0, h], ns_ref[0, h], ms_ref[0, h][:, 0:1]
            f = _ml_chunk(h, dh, mq_ref, mk_ref, v_ref, gates, cp, n_prev, m_prev)
            dC, dn = dc_s[h], dn_s[h]
            dhv = dh_ref[:, f["sl"]]
            dnum = dhv / f["dnm"]
            hv = f["num"] / f["dnm"]
            ddnm = -jnp.sum(dhv * hv, axis=1, keepdims=True) / f["dnm"]
            dden = jnp.where(jnp.abs(f["den"]) >= f["floor"], ddnm * jnp.sign(f["den"]), 0.0)
            dnb = dnum.astype(BF16)
            dsc = lax.dot_general(dnb, f["vb"], NT, preferred_element_type=F32) + dden
            dvc = lax.dot_general(f["sc"].astype(BF16), dnb, TN, preferred_element_type=F32)
            ds_inter = jnp.sum(dnum * f["qcp"], axis=1, keepdims=True) + dden * f["qn"]
            sdn = (f["s_inter"] * dnum).astype(BF16)
            sdd = f["s_inter"] * dden
            da = dsc * f["w"]
            dab = da.astype(BF16)
            dqc = (lax.dot_general(dab, f["kb"], NN, preferred_element_type=F32)
                   + lax.dot_general(sdn, f["cpb"], NN, preferred_element_type=F32) + sdd * n_prev)
            dcp = f["decay"] * dC + lax.dot_general(sdn, f["qb"], TN, preferred_element_type=F32)
            dnp = f["decay"] * dn + jnp.sum(sdd * f["qc"], axis=0, keepdims=True)
            vw = (f["vc"] * f["wk"]).astype(BF16)
            dCb = dC.astype(BF16)
            dkc = (lax.dot_general(dab, f["qb"], TN, preferred_element_type=F32)
                   + lax.dot_general(vw, dCb, NN, preferred_element_type=F32) + f["wk"] * dn)
            e = lax.dot_general(f["kb"], dCb, NT, preferred_element_type=F32)
            dvc = dvc + e * f["wk"]
            dwk = jnp.sum(e * f["vc"], axis=1, keepdims=True) + jnp.sum(f["kc"] * dn, axis=1, keepdims=True)
            ddecay = jnp.sum(jnp.sum(dC * cp, axis=1, keepdims=True), axis=0, keepdims=True) \
                + jnp.sum(dn * n_prev, axis=1, keepdims=True)
            dd = dsc * f["sc"]
            dlw = dwk * f["wk"]
            db_end = jnp.sum(dlw, axis=0, keepdims=True) + ddecay * f["decay"]
            di_col = dlw
            db_col = jnp.sum(dd, axis=1, keepdims=True) + ds_inter * f["s_inter"] - dlw \
                + jnp.where(lastrow, db_end, 0.0)
            cs_dd = jnp.sum(dd, axis=0, keepdims=True)
            dgc = dgc + jnp.where(lane == h, di_col, 0.0) + jnp.where(lane == hh + h, db_col, 0.0)
            dgr = dgr + jnp.where(rowi == h, cs_dd, 0.0) - jnp.where(rowi == hh + h, cs_dd, 0.0)
            dq_ref[:, f["sl"]] = dqc
            dk_ref[:, f["sl"]] = dkc * (dh ** -0.5)
            dv_ref[:, f["sl"]] = dvc.astype(dv_ref.dtype)
            dc_s[h] = dcp
            dn_s[h] = dnp
        dgc_ref[...] = dgc
        dgr_ref[0] = dgr

    dblk = d_model
    rev = lambda c: nc - 1 - c
    return pl.pallas_call(
        body, name=name, grid=(nc,),
        in_specs=[pl.BlockSpec((l, dblk), lambda c: (rev(c), 0)), pl.BlockSpec((l, dblk), lambda c: (rev(c), 1)),
                  pl.BlockSpec((l, dblk), lambda c: (rev(c), vcol // dblk)),
                  pl.BlockSpec((l, LANES), lambda c: (rev(c), 0)), pl.BlockSpec((1, 8, l), lambda c: (rev(c), 0, 0)),
                  pl.BlockSpec((1, hh, dh, dh), lambda c: (rev(c), 0, 0, 0)),
                  pl.BlockSpec((1, hh, 1, dh), lambda c: (rev(c), 0, 0, 0)),
                  pl.BlockSpec((1, hh, 1, LANES), lambda c: (rev(c), 0, 0, 0)),
                  pl.BlockSpec((l, dblk), lambda c: (rev(c), 0))],
        out_specs=[pl.BlockSpec((l, dblk), lambda c: (rev(c), 0)), pl.BlockSpec((l, dblk), lambda c: (rev(c), 0)),
                   pl.BlockSpec((l, dblk), lambda c: (rev(c), 0)), pl.BlockSpec((l, LANES), lambda c: (rev(c), 0)),
                   pl.BlockSpec((1, 8, l), lambda c: (rev(c), 0, 0))],
        out_shape=[jax.ShapeDtypeStruct((s, d_model), F32), jax.ShapeDtypeStruct((s, d_model), F32),
                   jax.ShapeDtypeStruct((s, d_model), BF16), jax.ShapeDtypeStruct((s, LANES), F32),
                   jax.ShapeDtypeStruct((nc, 8, l), F32)],
        scratch_shapes=[pltpu.VMEM((hh, dh, dh), F32), pltpu.VMEM((hh, 1, dh), F32)],
        compiler_params=_params(10 * hh * dh * dh * 4 + (16 << 20)),
    )(mqk, mqk, zm, gcol, grow, cs, ns, ms, dhm)


def _xa_fwd(zm, qcol, kv, gq, gk, d_model, *, name, tq=256):
    s = zm.shape[0]
    nm = kv.shape[0]
    dh = d_model // X_HEADS
    tq = _pick(s, (tq, 128, 64))
    scale = dh ** -0.5

    def body(q_ref, k_ref, v_ref, gq_ref, gk_ref, o_ref):
        qn = _rms_fwd(q_ref[...], gq_ref[...])
        kn = _rms_fwd(k_ref[...], gk_ref[...])
        lg = _dot(qn, kn, NT) * scale
        lg = lg - jnp.max(lg, axis=1, keepdims=True)
        p = jnp.exp(lg)
        p = p / jnp.sum(p, axis=1, keepdims=True)
        o_ref[...] = _dot(p, v_ref[...], NN).astype(o_ref.dtype)

    return pl.pallas_call(
        body, name=name, grid=(X_HEADS, s // tq),
        in_specs=[pl.BlockSpec((tq, dh), lambda h, i: (i, qcol // dh + h)), pl.BlockSpec((nm, dh), lambda h, i: (0, h)),
                  pl.BlockSpec((nm, dh), lambda h, i: (0, X_HEADS + h)),
                  pl.BlockSpec((1, dh), lambda h, i: (0, 0)), pl.BlockSpec((1, dh), lambda h, i: (0, 0))],
        out_specs=pl.BlockSpec((tq, dh), lambda h, i: (i, h)),
        out_shape=jax.ShapeDtypeStruct((s, d_model), BF16),
        compiler_params=_params(32 << 20),
    )(zm, kv, kv, gq, gk)


def _xa_bwd(zm, qcol, kv, gq, gk, dy, d_model, *, name, tq=256):
    s = zm.shape[0]
    nm = kv.shape[0]
    dh = d_model // X_HEADS
    tq = _pick(s, (tq, 128, 64))
    nq = s // tq
    scale = dh ** -0.5

    def body(q_ref, k_ref, v_ref, gq_ref, gk_ref, do_ref, dq_ref, dkn_ref, dv_ref, dgq_ref):
        h, i = pl.program_id(0), pl.program_id(1)

        @pl.when(i == 0)
        def _():
            dkn_ref[...] = jnp.zeros_like(dkn_ref)
            dv_ref[...] = jnp.zeros_like(dv_ref)

        @pl.when((i == 0) & (h == 0))
        def _():
            dgq_ref[...] = jnp.zeros_like(dgq_ref)

        q = q_ref[...]
        qn = _rms_fwd(q, gq_ref[...])
        kn = _rms_fwd(k_ref[...], gk_ref[...])
        lg = _dot(qn, kn, NT) * scale
        lg = lg - jnp.max(lg, axis=1, keepdims=True)
        p = jnp.exp(lg)
        p = p / jnp.sum(p, axis=1, keepdims=True)
        do = do_ref[...]
        dv_ref[...] += _dot(p, do, TN)
        dp = _dot(do, v_ref[...], NT)
        dlg = p * (dp - jnp.sum(dp * p, axis=1, keepdims=True)) * scale
        dqn = _dot(dlg, kn, NN)
        dkn_ref[...] += _dot(dlg, qn, TN)
        dq, dgq = _rms_bwd(q, gq_ref[...], dqn)
        dq_ref[...] = dq.astype(dq_ref.dtype)
        dgq_ref[...] += jnp.sum(dgq, axis=0, keepdims=True)

    return pl.pallas_call(
        body, name=name, grid=(X_HEADS, nq),
        in_specs=[pl.BlockSpec((tq, dh), lambda h, i: (i, qcol // dh + h)), pl.BlockSpec((nm, dh), lambda h, i: (0, h)),
                  pl.BlockSpec((nm, dh), lambda h, i: (0, X_HEADS + h)),
                  pl.BlockSpec((1, dh), lambda h, i: (0, 0)), pl.BlockSpec((1, dh), lambda h, i: (0, 0)),
                  pl.BlockSpec((tq, dh), lambda h, i: (i, h))],
        out_specs=[pl.BlockSpec((tq, dh), lambda h, i: (i, h)), pl.BlockSpec((nm, dh), lambda h, i: (0, h)),
                   pl.BlockSpec((nm, dh), lambda h, i: (0, h)), pl.BlockSpec((1, dh), lambda h, i: (0, 0))],
        out_shape=[jax.ShapeDtypeStruct((s, d_model), BF16), jax.ShapeDtypeStruct((nm, d_model), F32),
                   jax.ShapeDtypeStruct((nm, d_model), F32), jax.ShapeDtypeStruct((1, dh), F32)],
        compiler_params=_params(32 << 20),
    )(zm, kv, kv, gq, gk, dy)


def _place():
    return lax.axis_index("x"), lax.axis_index("y"), lax.axis_index("c")


ANY = pl.BlockSpec(memory_space=pl.ANY)


def _allgather_quarters(shards, *, name):
    n = len(shards)

    def body(*refs):
        ins, outs = refs[:n], refs[n:2 * n]
        send, recv, loc = refs[2 * n:]
        x, y, c = _place()
        chips = [(1 - x, y), (x, 1 - y), (1 - x, 1 - y)]
        local = []
        for t in range(n):
            cp = pltpu.make_async_copy(ins[t], outs[t].at[2 * x + y], loc.at[t])
            cp.start()
            local.append(cp)

        def copy(t, j, slot):
            return pltpu.make_async_remote_copy(
                src_ref=ins[t], dst_ref=outs[t].at[slot], send_sem=send.at[3 * t + j], recv_sem=recv.at[3 * t + j],
                device_id=(chips[j][0], chips[j][1], c), device_id_type=MESH)

        for t in range(n):
            for j in range(3):
                copy(t, j, 2 * x + y).start()
        for t in range(n):
            for j in range(3):
                copy(t, j, 2 * chips[j][0] + chips[j][1]).wait_recv()
        for t in range(n):
            for j in range(3):
                copy(t, j, 2 * x + y).wait_send()
        for cp in local:
            cp.wait()

    return pl.pallas_call(
        body, name=name, in_specs=[ANY] * n, out_specs=[ANY] * n,
        out_shape=[jax.ShapeDtypeStruct((4,) + a.shape, a.dtype) for a in shards],
        scratch_shapes=[pltpu.SemaphoreType.DMA((3 * n,)), pltpu.SemaphoreType.DMA((3 * n,)),
                        pltpu.SemaphoreType.DMA((n,))],
    )(*shards)


def _exchange_grads(parts, *, name):
    n = len(parts)

    def body(*refs):
        ins, outs = refs[:n], refs[n:2 * n]
        send, recv, loc = refs[2 * n:]
        x, y, c = _place()
        me = 4 * x + 2 * y + c
        peers = [(x ^ ((j >> 2) & 1), y ^ ((j >> 1) & 1), c ^ (j & 1)) for j in range(1, 8)]
        local = []
        for t in range(n):
            cp = pltpu.make_async_copy(ins[t].at[2 * x + y, c], outs[t].at[me], loc.at[t])
            cp.start()
            local.append(cp)

        def copy(t, j, slot):
            px, py, pc = peers[j]
            return pltpu.make_async_remote_copy(
                src_ref=ins[t].at[2 * px + py, pc], dst_ref=outs[t].at[slot], send_sem=send.at[7 * t + j],
                recv_sem=recv.at[7 * t + j], device_id=(px, py, pc), device_id_type=MESH)

        for t in range(n):
            for j in range(7):
                copy(t, j, me).start()
        for t in range(n):
            for j in range(7):
                px, py, pc = peers[j]
                copy(t, j, 4 * px + 2 * py + pc).wait_recv()
        for t in range(n):
            for j in range(7):
                copy(t, j, me).wait_send()
        for cp in local:
            cp.wait()

    return pl.pallas_call(
        body, name=name, in_specs=[ANY] * n, out_specs=[ANY] * n,
        out_shape=[jax.ShapeDtypeStruct((8,) + a.shape[2:], a.dtype) for a in parts],
        scratch_shapes=[pltpu.SemaphoreType.DMA((7 * n,)), pltpu.SemaphoreType.DMA((7 * n,)),
                        pltpu.SemaphoreType.DMA((n,))],
    )(*parts)


def _sum8(parts, *, name):
    _, r, c = parts.shape
    t = _pick(r, (128, 64, 32, 16, 8))

    def body(p_ref, o_ref):
        acc = p_ref[0].astype(F32)
        for k in range(1, 8):
            acc = acc + p_ref[k].astype(F32)
        o_ref[...] = acc

    return pl.pallas_call(
        body, name=name, grid=(r // t,), in_specs=[pl.BlockSpec((8, t, c), lambda i: (0, i, 0))],
        out_specs=pl.BlockSpec((t, c), lambda i: (i, 0)), out_shape=jax.ShapeDtypeStruct((r, c), F32),
        compiler_params=_params(2 * 8 * t * c * 2 + 6 * t * c * 4 + (4 << 20)),
    )(parts)


def _swap_halves(halves, *, name):
    n = len(halves)

    def body(*refs):
        ins, outs = refs[:n], refs[n:2 * n]
        send, recv, loc = refs[2 * n:]
        x, y, c = _place()
        local = []
        for t in range(n):
            cp = pltpu.make_async_copy(ins[t], outs[t].at[c], loc.at[t])
            cp.start()
            local.append(cp)

        def copy(t, slot):
            return pltpu.make_async_remote_copy(
                src_ref=ins[t], dst_ref=outs[t].at[slot], send_sem=send.at[t], recv_sem=recv.at[t],
                device_id=(x, y, 1 - c), device_id_type=MESH)

        for t in range(n):
            copy(t, c).start()
        for t in range(n):
            copy(t, 1 - c).wait_recv()
        for t in range(n):
            copy(t, c).wait_send()
        for cp in local:
            cp.wait()

    return pl.pallas_call(
        body, name=name, in_specs=[ANY] * n, out_specs=[ANY] * n,
        out_shape=[jax.ShapeDtypeStruct((2,) + a.shape, a.dtype) for a in halves],
        scratch_shapes=[pltpu.SemaphoreType.DMA((n,)), pltpu.SemaphoreType.DMA((n,)), pltpu.SemaphoreType.DMA((n,))],
    )(*halves)


def _allreduce_small(p, *, name):
    r = p.shape[0]

    def body(p_ref, o_ref, buf, send, recv):
        x, y, c = _place()
        me = 4 * x + 2 * y + c
        peers = [(x ^ ((j >> 2) & 1), y ^ ((j >> 1) & 1), c ^ (j & 1)) for j in range(1, 8)]

        def copy(j, slot):
            return pltpu.make_async_remote_copy(
                src_ref=p_ref, dst_ref=buf.at[slot], send_sem=send.at[j], recv_sem=recv.at[j],
                device_id=peers[j], device_id_type=MESH)

        for j in range(7):
            copy(j, me).start()
        buf[me] = p_ref[...]
        for j in range(7):
            px, py, pc = peers[j]
            copy(j, 4 * px + 2 * py + pc).wait_recv()
        for j in range(7):
            copy(j, me).wait_send()
        acc = buf[0]
        for k in range(1, 8):
            acc = acc + buf[k]
        o_ref[...] = acc

    vspec = pl.BlockSpec(memory_space=pltpu.VMEM)
    return pl.pallas_call(
        body, name=name, in_specs=[vspec], out_specs=vspec, out_shape=jax.ShapeDtypeStruct((r, LANES), F32),
        scratch_shapes=[pltpu.VMEM((8, r, LANES), F32), pltpu.SemaphoreType.DMA((7,)), pltpu.SemaphoreType.DMA((7,))],
    )(p)


def _adamw_fn(w, g, m, v):
    m = ADAM_B1 * m + (1.0 - ADAM_B1) * g
    v = ADAM_B2 * v + (1.0 - ADAM_B2) * (g * g)
    m_hat = m / (1.0 - ADAM_B1 ** ADAM_STEP)
    v_hat = v / (1.0 - ADAM_B2 ** ADAM_STEP)
    delta = -ADAM_LR * (m_hat / (jnp.sqrt(v_hat) + ADAM_EPS) + ADAM_WD * w)
    return delta, m, v


def _adamw(w, g, m, v, *, name):
    c = w.shape[1]
    return _rowwise(_adamw_fn, [w, g, m, v], [], [(c, F32)] * 3, name=name, tr=128)


def _pack(vecs, rows):
    flat = jnp.concatenate([a.reshape(-1).astype(F32) for a in vecs])
    return jnp.pad(flat, (0, rows * LANES - flat.shape[0])).reshape(rows, LANES)


def _unpack(p, like):
    flat, out, o = p.reshape(-1), [], 0
    for a in like:
        out.append(flat[o:o + a.size].reshape(a.shape))
        o += a.size
    return out


def kernel(x, mem, g_mix, w_in, b_if, b_gate, conv_w, conv_b, ml_norm_g, g_mem, w_mem_kv, q_norm_g, k_norm_g, w_sb_proj, w_ml_proj, w_x_proj, w_out, g_mlp, w_ff1, w_ff2, loss_target, m_g_mix, m_w_in, m_b_if, m_b_gate, m_conv_w, m_conv_b, m_ml_norm_g, m_g_mem, m_w_mem_kv, m_q_norm_g, m_k_norm_g, m_w_sb_proj, m_w_ml_proj, m_w_x_proj, m_w_out, m_g_mlp, m_w_ff1, m_w_ff2, v_g_mix, v_w_in, v_b_if, v_b_gate, v_conv_w, v_conv_b, v_ml_norm_g, v_g_mem, v_w_mem_kv, v_q_norm_g, v_k_norm_g, v_w_sb_proj, v_w_ml_proj, v_w_x_proj, v_w_out, v_g_mlp, v_w_ff1, v_w_ff2):
    _, s, d = x.shape
    nm = mem.shape[1]
    n_in = 4 * w_in.shape[2]
    dff = 4 * w_ff1.shape[2]
    sbh = d // SB_HD
    hh = ML_HEADS
    dh = d // hh
    nc = s // CHUNK
    assert n_in == 11 * d + 2 * hh and d % (2 * LANES) == 0 and s % LANES == 0
    x2, mem2, tgt = x[0], mem[0], loss_target[0]

    q_shards = [w_in[0], w_mem_kv[0], w_sb_proj[0], w_ml_proj[0], w_x_proj[0], w_out[0], w_ff1[0], w_ff2[0]]
    gath = _allgather_quarters([a.astype(BF16) for a in q_shards] + [conv_w[0]], name="gather_weights")
    cols = lambda a: a.transpose(1, 0, 2).reshape(a.shape[1], 4 * a.shape[2])
    rws = lambda a: a.reshape(4 * a.shape[1], a.shape[2])
    w_in_f = cols(gath[0])
    w_main = jnp.concatenate([w_in_f[:, :7 * d], w_in_f[:, 7 * d + 2 * hh:]], axis=1)
    w_if = jnp.pad(w_in_f[:, 7 * d:7 * d + 2 * hh], ((0, 0), (0, LANES - 2 * hh)))
    w_kv, w_sbp, w_mlp, w_xp, w_o = cols(gath[1]), rws(gath[2]), rws(gath[3]), rws(gath[4]), rws(gath[5])
    w_f1, w_f2, conv_wf = cols(gath[6]), rws(gath[7]), cols(gath[8])
    b_if_p = jnp.pad(b_if, ((0, 0), (0, LANES - 2 * hh)))

    (hn,) = _rowwise(_rms_fwd, [x2], [g_mix], [(d, BF16)], name="norm_in")
    zm = _mm(hn, w_main, name="proj_in")
    zif = _mm(hn, w_if, name="proj_if")
    y_sb, ltot = _sb_fwd(zm, sbh, name="sb_fwd")

    def gate_fn(z, b):
        pre = z + b
        lane = lax.broadcasted_iota(jnp.int32, pre.shape, 1)
        return jnp.where(lane < hh, pre, -_softplus(-pre))

    (gcol,) = _rowwise(gate_fn, [zif], [b_if_p], [(LANES, F32)], name="ml_gates")
    grow = gcol[:, :8].T.reshape(8, nc, CHUNK).transpose(1, 0, 2)
    mqk = _conv_fwd(zm, 3 * d, 2 * d, conv_wf, conv_b, name="conv_fwd")
    hm, cst, nst, mst = _ml_fwd(mqk, zm, 5 * d, gcol, grow, d, name="ml_fwd")

    def mlout_fn(hv, o, g):
        ys = [_rms_fwd(hv[:, k * dh:(k + 1) * dh], g[:, k * dh:(k + 1) * dh]) for k in range(hh)]
        return jnp.concatenate(ys, axis=1) * _sigmoid(o)

    (y_ml,) = _rowwise(mlout_fn, [hm, (zm, d, 6)], [ml_norm_g], [(d, BF16)], name="ml_out")
    (memn,) = _rowwise(_rms_fwd, [mem2], [g_mem], [(d, BF16)], name="norm_mem")
    kv = _mm(memn, w_kv, name="proj_kv")
    y_x = _xa_fwd(zm, 7 * d, kv, q_norm_g, k_norm_g, d, name="xa_fwd")
    p_sb = _mm(y_sb, w_sbp, name="proj_sb")
    p_ml = _mm(y_ml, w_mlp, name="proj_ml")
    p_x = _mm(y_x, w_xp, name="proj_x")

    def merge_fn(a, b, c, g0, g1, g2, bg):
        return (_sigmoid(g0 + bg[:, :d]) * a + _sigmoid(g1 + bg[:, d:2 * d]) * b + _sigmoid(g2 + bg[:, 2 * d:]) * c)

    gate_cols = [(zm, d, 8), (zm, d, 9), (zm, d, 10)]
    (mixed,) = _rowwise(merge_fn, [p_sb, p_ml, p_x] + gate_cols, [b_gate], [(d, BF16)], name="merge")
    x1 = _mm(mixed, w_o, add=x2, name="proj_out")
    (h2,) = _rowwise(_rms_fwd, [x1], [g_mlp], [(d, BF16)], name="norm_mlp")
    u = _mm(h2, w_f1, name="ff1")
    (act,) = _rowwise(lambda uv: jnp.square(jnp.maximum(uv, 0.0)), [u], [], [(dff, BF16)], name="relu2", tr=128)
    yo = _mm(act, w_f2, add=x1, name="ff2")

    def loss_fn(yv, tv):
        e = yv - tv
        return e * (1.0 / d), jnp.sum(e * e, axis=0, keepdims=True) * (0.5 / d)

    dy, loss_cols = _rowwise(loss_fn, [yo, tgt], [], [(d, F32)], [d], name="loss")

    dact = _mm(dy, w_f2, tb=True, name="ff2_dx")
    dw_f2 = _mm(act, dy, ta=True, name="ff2_dw")
    (du,) = _rowwise(lambda g, uv: g * 2.0 * jnp.maximum(uv, 0.0), [dact, u], [], [(dff, BF16)], name="relu2_bwd",
                     tr=128)
    dw_f1 = _mm(h2, du, ta=True, name="ff1_dw")
    dh2 = _mm(du, w_f1, tb=True, name="ff1_dx")

    def norm_bwd_fn(xv, dyv, res, g):
        dx, dg = _rms_bwd(xv, g, dyv)
        return dx + res, jnp.sum(dg, axis=0, keepdims=True)

    dx1, dg_mlp = _rowwise(norm_bwd_fn, [x1, dh2, dy], [g_mlp], [(d, F32)], [d], name="norm_mlp_bwd")
    dmixed = _mm(dx1, w_o, tb=True, name="proj_out_dx")
    dw_o = _mm(mixed, dx1, ta=True, name="proj_out_dw")

    def merge_bwd_fn(dm, a, b, c, g0, g1, g2, bg):
        outs, dgs = [], []
        for p, g, k in ((a, g0, 0), (b, g1, 1), (c, g2, 2)):
            sg = _sigmoid(g + bg[:, k * d:(k + 1) * d])
            outs.append(dm * sg)
            dgs.append(dm * p * sg * (1.0 - sg))
        dgate = jnp.concatenate(dgs, axis=1)
        return (*outs, dgate, jnp.sum(dgate, axis=0, keepdims=True))

    dp_sb, dp_ml, dp_x, dgate, db_gate = _rowwise(
        merge_bwd_fn, [dmixed, p_sb, p_ml, p_x] + gate_cols, [b_gate], [(d, BF16)] * 3 + [(3 * d, BF16)], [3 * d],
        name="merge_bwd", tr=128)
    dw_sbp = _mm(y_sb, dp_sb, ta=True, name="proj_sb_dw")
    dw_mlp = _mm(y_ml, dp_ml, ta=True, name="proj_ml_dw")
    dw_xp = _mm(y_x, dp_x, ta=True, name="proj_x_dw")
    dy_sb = _mm(dp_sb, w_sbp, tb=True, out_dtype=BF16, name="proj_sb_dx")
    dy_ml = _mm(dp_ml, w_mlp, tb=True, name="proj_ml_dx")
    dy_x = _mm(dp_x, w_xp, tb=True, out_dtype=BF16, name="proj_x_dx")

    dsq, dsk, dsv = _sb_bwd(zm, dy_sb, ltot, sbh, name="sb_bwd")

    def mlout_bwd_fn(dyv, hv, o, g):
        sg = _sigmoid(o)
        dn = dyv * sg
        dxs, dgs, ys = [], [], []
        for k in range(hh):
            sl = slice(k * dh, (k + 1) * dh)
            ys.append(_rms_fwd(hv[:, sl], g[:, sl]))
            dxk, dgk = _rms_bwd(hv[:, sl], g[:, sl], dn[:, sl])
            dxs.append(dxk)
            dgs.append(dgk)
        do = dyv * jnp.concatenate(ys, axis=1) * sg * (1.0 - sg)
        return jnp.concatenate(dxs, axis=1), do, jnp.sum(jnp.concatenate(dgs, axis=1), axis=0, keepdims=True)

    dhm, dmlo, dg_mln = _rowwise(mlout_bwd_fn, [dy_ml, hm, (zm, d, 6)], [ml_norm_g], [(d, F32), (d, BF16)], [d],
                                 name="ml_out_bwd")
    dmq, dmk, dmlv, dgc, dgr = _ml_bwd(mqk, zm, 5 * d, gcol, grow, cst, nst, mst, dhm, d, name="ml_bwd")
    dmqk = jnp.concatenate([dmq, dmk], axis=1)
    dmlqk, dconv_w, dconv_b = _conv_bwd(zm, 3 * d, 2 * d, conv_wf, conv_b, dmqk, name="conv_bwd")
    dgr_t = jnp.pad(dgr.transpose(1, 0, 2).reshape(8, s).T, ((0, 0), (0, LANES - 8)))

    def gate_bwd_fn(a, b, z, bias):
        tot = a + b
        r = lax.broadcasted_iota(jnp.int32, (CHUNK, CHUNK), 0)
        c = lax.broadcasted_iota(jnp.int32, (CHUNK, CHUNK), 1)
        dlf = _u01dot((c >= r).astype(BF16), tot)
        lane = lax.broadcasted_iota(jnp.int32, tot.shape, 1)
        dz = jnp.where(lane < hh, tot, jnp.where(lane < 2 * hh, dlf * _sigmoid(-(z + bias)), 0.0))
        return dz, jnp.sum(dz, axis=0, keepdims=True)

    dzif, db_if_p = _rowwise(gate_bwd_fn, [dgc, dgr_t, zif], [b_if_p], [(LANES, BF16)], [LANES], name="ml_gates_bwd",
                             tr=CHUNK)
    dxq, dkn, dxv, dg_qn = _xa_bwd(zm, 7 * d, kv, q_norm_g, k_norm_g, dy_x, d, name="xa_bwd")

    def knorm_bwd_fn(kvv, dknv, dvv, g):
        dks, dgs = [], []
        for k in range(X_HEADS):
            sl = slice(k * dh, (k + 1) * dh)
            dk, dg = _rms_bwd(kvv[:, sl], g, dknv[:, sl])
            dks.append(dk)
            dgs.append(jnp.sum(dg, axis=0, keepdims=True))
        return jnp.concatenate(dks + [dvv], axis=1), dgs[0] + dgs[1] + dgs[2] + dgs[3]

    dkv, dg_kn = _rowwise(knorm_bwd_fn, [(kv, d, 0), dkn, dxv], [k_norm_g], [(2 * d, BF16)], [dh], name="xa_knorm_bwd")
    dw_kv = _mm(memn, dkv, ta=True, name="proj_kv_dw")
    dmemn = _mm(dkv, w_kv, tb=True, name="proj_kv_dx")

    def gmem_fn(mv, dv_, g):
        _, dg = _rms_bwd(mv, g, dv_)
        return (jnp.sum(dg, axis=0, keepdims=True),)

    (dg_mem,) = _rowwise(gmem_fn, [mem2, dmemn], [g_mem], [], [d], name="norm_mem_bwd")

    dzm = jnp.concatenate([dsq, dsk, dsv, dmlqk, dmlv, dmlo, dxq, dgate], axis=1)
    dw_main = _mm(hn, dzm, ta=True, name="proj_in_dw")
    dw_if = _mm(hn, dzif, ta=True, name="proj_if_dw")
    dhn = _mm(dzm, w_main, tb=True, name="proj_in_dx")
    dhn = _mm(dzif, w_if, tb=True, add=dhn, name="proj_if_dx")
    dx, dg_mix = _rowwise(norm_bwd_fn, [x2, dhn, dx1], [g_mix], [(d, F32)], [d], name="norm_in_bwd")

    dw_in = jnp.concatenate([dw_main[:, :7 * d], dw_if[:, :2 * hh], dw_main[:, 7 * d:]], axis=1)
    uncols = lambda a: a.reshape(a.shape[0], 4, a.shape[1] // 4).transpose(1, 0, 2)
    unrws = lambda a: a.reshape(4, a.shape[0] // 4, a.shape[1])
    quarters = [uncols(dw_in), uncols(dw_kv), unrws(dw_sbp), unrws(dw_mlp), unrws(dw_xp), unrws(dw_o), uncols(dw_f1),
                unrws(dw_f2)]
    parts = [q.astype(BF16).reshape(4, 2, q.shape[1] // 2, q.shape[2]) for q in quarters]
    recv = _exchange_grads(parts, name="exchange_grads")
    halves = [_sum8(r, name=f"sum_grads_{i}") for i, r in enumerate(recv)]
    both = _swap_halves(halves, name="swap_halves")
    g_big = [b.reshape(2 * b.shape[1], b.shape[2]) for b in both]

    small_g = [dg_mix, db_if_p[:, :2 * hh], db_gate, dconv_w, dconv_b, dg_mln, dg_mem, dg_qn, dg_kn, dg_mlp,
               jnp.sum(loss_cols).reshape(1, 1)]
    n_small = sum(a.size for a in small_g)
    rows = -(-n_small // (8 * LANES)) * 8
    g_small = _unpack(_allreduce_small(_pack(small_g, rows), name="allreduce_small"), small_g)
    loss = g_small[-1].reshape(())
    k4 = 2 * lax.axis_index("x") + lax.axis_index("y")
    qw = conv_w.shape[2]
    g_conv_w = lax.dynamic_slice_in_dim(g_small[3], k4 * qw, qw, axis=1)
    g_small_w = [g_small[0], g_small[1], g_small[2], g_conv_w] + g_small[4:10]
    sm_w = [g_mix, b_if, b_gate, conv_w[0], conv_b, ml_norm_g, g_mem, q_norm_g, k_norm_g, g_mlp]
    sm_m = [m_g_mix, m_b_if, m_b_gate, m_conv_w[0], m_conv_b, m_ml_norm_g, m_g_mem, m_q_norm_g, m_k_norm_g, m_g_mlp]
    sm_v = [v_g_mix, v_b_if, v_b_gate, v_conv_w[0], v_conv_b, v_ml_norm_g, v_g_mem, v_q_norm_g, v_k_norm_g, v_g_mlp]
    n_sw = sum(a.size for a in sm_w)
    rows_w = -(-n_sw // (8 * LANES)) * 8
    sm_out = _adamw(_pack(sm_w, rows_w), _pack(g_small_w, rows_w), _pack(sm_m, rows_w), _pack(sm_v, rows_w),
                    name="adamw_small")
    sm_delta, sm_newm, sm_newv = [_unpack(p, sm_w) for p in sm_out]

    big_w = [w_in[0], w_mem_kv[0], w_sb_proj[0], w_ml_proj[0], w_x_proj[0], w_out[0], w_ff1[0], w_ff2[0]]
    big_m = [m_w_in[0], m_w_mem_kv[0], m_w_sb_proj[0], m_w_ml_proj[0], m_w_x_proj[0], m_w_out[0], m_w_ff1[0],
             m_w_ff2[0]]
    big_v = [v_w_in[0], v_w_mem_kv[0], v_w_sb_proj[0], v_w_ml_proj[0], v_w_x_proj[0], v_w_out[0], v_w_ff1[0],
             v_w_ff2[0]]
    big_out = [_adamw(w, g, m, v, name=f"adamw_{i}") for i, (w, g, m, v) in enumerate(zip(big_w, g_big, big_m, big_v))]

    order = ["g_mix", "w_in", "b_if", "b_gate", "conv_w", "conv_b", "ml_norm_g", "g_mem", "w_mem_kv", "q_norm_g",
             "k_norm_g", "w_sb_proj", "w_ml_proj", "w_x_proj", "w_out", "g_mlp", "w_ff1", "w_ff2"]
    small_names = ["g_mix", "b_if", "b_gate", "conv_w", "conv_b", "ml_norm_g", "g_mem", "q_norm_g", "k_norm_g", "g_mlp"]
    big_names = ["w_in", "w_mem_kv", "w_sb_proj", "w_ml_proj", "w_x_proj", "w_out", "w_ff1", "w_ff2"]
    grads, deltas, new_m, new_v = {}, {}, {}, {}
    for i, nme in enumerate(small_names):
        shp = sm_w[i].shape if nme != "conv_w" else conv_w.shape
        grads[nme] = g_small_w[i].reshape(shp)
        deltas[nme], new_m[nme], new_v[nme] = (sm_delta[i].reshape(shp), sm_newm[i].reshape(shp),
                                               sm_newv[i].reshape(shp))
    for i, nme in enumerate(big_names):
        grads[nme] = g_big[i][None]
        deltas[nme], new_m[nme], new_v[nme] = (o[None] for o in big_out[i])
    return (loss, dx[None], *[grads[k] for k in order], *[deltas[k] for k in order], *[new_m[k] for k in order],
            *[new_v[k] for k in order])
```

```python
import functools

import jax
import jax.numpy as jnp
from jax import lax
from jax.experimental import pallas as pl
from jax.experimental.pallas import tpu as pltpu

F32 = jnp.float32
BF16 = jnp.bfloat16
MESH = pl.DeviceIdType.MESH

EPS = 1e-6
SB_HD = 128
ML_HEADS = 4
X_HEADS = 4
CHUNK = 64
CONV_W = 4
LANES = 128
ADAM_LR = 0.001
ADAM_B1 = 0.9
ADAM_B2 = 0.999
ADAM_EPS = 1e-08
ADAM_WD = 0.01
ADAM_STEP = 10
VMEM_CAP = 56 * 1024 * 1024
NEG = -1e30

NT = (((1,), (1,)), ((), ()))
NN = (((1,), (0,)), ((), ()))
TN = (((0,), (0,)), ((), ()))


def _dot(a, b, dn=NN):
    return lax.dot_general(a.astype(BF16), b.astype(BF16), dn, preferred_element_type=F32)


def _dot01(x, u, dn=NN):
    hi = x.astype(BF16)
    lo = (x - hi.astype(F32)).astype(BF16)
    return (lax.dot_general(hi, u, dn, preferred_element_type=F32)
            + lax.dot_general(lo, u, dn, preferred_element_type=F32))


def _u01dot(u, x):
    hi = x.astype(BF16)
    lo = (x - hi.astype(F32)).astype(BF16)
    return (lax.dot_general(u, hi, NN, preferred_element_type=F32)
            + lax.dot_general(u, lo, NN, preferred_element_type=F32))


def _pick(n, cands):
    for c in cands:
        if c <= n and n % c == 0:
            return c
    return n


def _nbytes(shape, dtype):
    n = 1
    for s in shape:
        n *= s
    return n * jnp.dtype(dtype).itemsize


def _params(vmem_bytes):
    return pltpu.CompilerParams(vmem_limit_bytes=int(min(VMEM_CAP, max(vmem_bytes, 16 * 1024 * 1024))))


def _softplus(z):
    return jnp.maximum(z, 0.0) + jnp.log(1.0 + jnp.exp(-jnp.abs(z)))


def _sigmoid(z):
    return 1.0 / (1.0 + jnp.exp(-z))


def _rms_fwd(xv, g):
    r = lax.rsqrt(jnp.mean(xv * xv, axis=-1, keepdims=True) + EPS)
    return xv * r * g


def _rms_bwd(xv, g, dy):
    r = lax.rsqrt(jnp.mean(xv * xv, axis=-1, keepdims=True) + EPS)
    xh = xv * r
    dxh = dy * g
    dx = r * (dxh - xh * jnp.mean(dxh * xh, axis=-1, keepdims=True))
    return dx, dy * xh


def _mm(a, b, *, name, ta=False, tb=False, add=None, out_dtype=F32, bm=1024, bn=1024, bk=512):
    m, k = (a.shape[1], a.shape[0]) if ta else a.shape
    n = b.shape[0] if tb else b.shape[1]
    tm = _pick(m, (bm, 512, 256, 128))
    tn = _pick(n, (bn, 512, 256, 128))
    tk = _pick(k, (bk, 256, 128))
    nk = k // tk
    dn = (((0 if ta else 1,), (1 if tb else 0,)), ((), ()))
    has_add = add is not None

    def body(*refs):
        if has_add:
            a_ref, b_ref, c_ref, o_ref, acc_ref = refs
        else:
            a_ref, b_ref, o_ref, acc_ref = refs
        kk = pl.program_id(2)

        @pl.when(kk == 0)
        def _():
            acc_ref[...] = jnp.zeros_like(acc_ref)

        acc_ref[...] += lax.dot_general(a_ref[...].astype(BF16), b_ref[...].astype(BF16), dn,
                                        preferred_element_type=F32)

        @pl.when(kk == nk - 1)
        def _():
            r = acc_ref[...]
            if has_add:
                r = r + c_ref[...].astype(F32)
            o_ref[...] = r.astype(out_dtype)

    a_spec = pl.BlockSpec((tk, tm), lambda i, j, q: (q, i)) if ta else pl.BlockSpec((tm, tk), lambda i, j, q: (i, q))
    b_spec = pl.BlockSpec((tn, tk), lambda i, j, q: (j, q)) if tb else pl.BlockSpec((tk, tn), lambda i, j, q: (q, j))
    o_spec = pl.BlockSpec((tm, tn), lambda i, j, q: (i, j))
    ins, specs = [a, b], [a_spec, b_spec]
    vm = 2 * (_nbytes((tm, tk), a.dtype) + _nbytes((tk, tn), b.dtype) + _nbytes((tm, tn), out_dtype)) \
        + 3 * _nbytes((tm, tn), F32) + _nbytes((tm, tk), BF16) + _nbytes((tk, tn), BF16)
    if has_add:
        ins.append(add)
        specs.append(o_spec)
        vm += 2 * _nbytes((tm, tn), add.dtype)
    return pl.pallas_call(
        body, name=name, grid=(m // tm, n // tn, nk), in_specs=specs, out_specs=o_spec,
        out_shape=jax.ShapeDtypeStruct((m, n), out_dtype), scratch_shapes=[pltpu.VMEM((tm, tn), F32)],
        compiler_params=_params(vm + (4 << 20)),
    )(*ins)


def _rowwise(fn, rows, consts, outs, reds=(), *, name, tr=256, temps=6):
    rows = [r if isinstance(r, tuple) else (r, r.shape[1], 0) for r in rows]
    nrows = rows[0][0].shape[0]
    t = _pick(nrows, (tr, 128, 64, 32, 16, 8))
    nr, nc, no = len(rows), len(consts), len(outs)

    def body(*refs):
        rin, cin = refs[:nr], refs[nr:nr + nc]
        oref, rref = refs[nr + nc:nr + nc + no], refs[nr + nc + no:]
        res = fn(*[r[...] for r in rin], *[c[...] for c in cin])
        if not isinstance(res, (tuple, list)):
            res = (res,)
        for o, v in zip(oref, res[:no]):
            o[...] = v.astype(o.dtype)
        if rref:
            @pl.when(pl.program_id(0) == 0)
            def _():
                for r in rref:
                    r[...] = jnp.zeros_like(r)

            for r, v in zip(rref, res[no:]):
                r[...] += v

    in_specs = [pl.BlockSpec((t, w), functools.partial(lambda i, ci: (i, ci), ci=ci)) for (_, w, ci) in rows]
    in_specs += [pl.BlockSpec(c.shape, functools.partial(lambda i, nd: (0,) * nd, nd=c.ndim)) for c in consts]
    out_specs = [pl.BlockSpec((t, w), lambda i: (i, 0)) for (w, _) in outs]
    out_specs += [pl.BlockSpec((1, w), lambda i: (0, 0)) for w in reds]
    out_shape = [jax.ShapeDtypeStruct((nrows, w), dt) for (w, dt) in outs]
    out_shape += [jax.ShapeDtypeStruct((1, w), F32) for w in reds]
    widest = max([w for (_, w, _) in rows] + [w for (w, _) in outs])
    vm = 2 * sum(_nbytes((t, w), a.dtype) for (a, w, _) in rows) + 2 * sum(_nbytes((t, w), dt) for (w, dt) in outs)
    vm += temps * _nbytes((t, widest), F32) + (2 << 20)
    res = pl.pallas_call(
        body, name=name, grid=(nrows // t,), in_specs=in_specs, out_specs=out_specs, out_shape=out_shape,
        compiler_params=_params(vm),
    )(*[a for (a, _, _) in rows], *consts)
    return list(res)


def _sb_fwd(zm, heads, *, name, t=256):
    s = zm.shape[0]
    nq = s // t
    scale = SB_HD ** -0.5

    def body(q_ref, k_ref, v_ref, o_ref, lt_ref):
        i = pl.program_id(1)
        qb = q_ref[...].astype(BF16)
        r = lax.broadcasted_iota(jnp.int32, (t, t), 0)
        c = lax.broadcasted_iota(jnp.int32, (t, t), 1)
        causal = c < r
        usuf = (r > c).astype(BF16)

        def tile(j, carry, diag):
            acc, cl = carry
            rows = pl.ds(pl.multiple_of(j * t, t), t)
            kb = k_ref[rows, :].astype(BF16)
            vb = v_ref[rows, :].astype(BF16)
            z = lax.dot_general(qb, kb, NT, preferred_element_type=F32) * scale
            lsig = -_softplus(z)
            l = jnp.where(causal, lsig, 0.0) if diag else lsig
            loga = z + lsig + _dot01(l, usuf) + cl
            if diag:
                loga = jnp.where(causal, loga, NEG)
            a = jnp.exp(loga)
            acc = acc + lax.dot_general(a.astype(BF16), vb, NN, preferred_element_type=F32)
            return acc, cl + jnp.sum(l, axis=1, keepdims=True)

        carry = tile(i, (jnp.zeros((t, SB_HD), F32), jnp.zeros((t, 1), F32)), True)
        acc, cl = lax.fori_loop(0, i, lambda n, cr: tile(i - 1 - n, cr, False), carry)
        o_ref[...] = acc.astype(o_ref.dtype)
        lt_ref[...] = jnp.broadcast_to(cl, (t, LANES))

    blk = lambda off: pl.BlockSpec((s, SB_HD), functools.partial(lambda h, i, off: (0, off + h), off=off))
    return pl.pallas_call(
        body, name=name, grid=(heads, nq),
        in_specs=[pl.BlockSpec((t, SB_HD), lambda h, i: (i, h)), blk(heads), blk(2 * heads)],
        out_specs=[pl.BlockSpec((t, SB_HD), lambda h, i: (i, h)), pl.BlockSpec((t, LANES), lambda h, i: (i, h))],
        out_shape=[jax.ShapeDtypeStruct((s, heads * SB_HD), BF16), jax.ShapeDtypeStruct((s, heads * LANES), F32)],
        compiler_params=_params(8 * s * SB_HD * 4 + (8 << 20)),
    )(zm, zm, zm)


def _sb_bwd(zm, dy, ltot, heads, *, name, t=256):
    s = zm.shape[0]
    nq = s // t
    scale = SB_HD ** -0.5

    def body(q_ref, k_ref, v_ref, do_ref, lt_ref, dq_ref, dk_ref, dv_ref, dka, dva):
        i = pl.program_id(1)

        @pl.when(i == 0)
        def _():
            dka[...] = jnp.zeros_like(dka)
            dva[...] = jnp.zeros_like(dva)

        qb = q_ref[...].astype(BF16)
        dob = do_ref[...].astype(BF16)
        ltot_c = lt_ref[:, 0:1]
        r = lax.broadcasted_iota(jnp.int32, (t, t), 0)
        c = lax.broadcasted_iota(jnp.int32, (t, t), 1)
        causal = c < r
        uincl = (r <= c).astype(BF16)
        uexcl = (r < c).astype(BF16)

        def tile(j, carry, diag):
            dq, cl, cg = carry
            rows = pl.ds(pl.multiple_of(j * t, t), t)
            kb = k_ref[rows, :].astype(BF16)
            vb = v_ref[rows, :].astype(BF16)
            z = lax.dot_general(qb, kb, NT, preferred_element_type=F32) * scale
            lsig = -_softplus(z)
            l = jnp.where(causal, lsig, 0.0) if diag else lsig
            later = ltot_c - (cl + _dot01(l, uincl))
            loga = z + lsig + later
            if diag:
                loga = jnp.where(causal, loga, NEG)
            a = jnp.exp(loga)
            sig = jnp.exp(z + lsig)
            g = a * lax.dot_general(dob, vb, NT, preferred_element_type=F32)
            p = cg + _dot01(g, uexcl)
            dz = g * (1.0 - sig) - p * sig
            if diag:
                dz = jnp.where(causal, dz, 0.0)
            dzb = (dz * scale).astype(BF16)
            dva[rows, :] += lax.dot_general(a.astype(BF16), dob, TN, preferred_element_type=F32)
            dka[rows, :] += lax.dot_general(dzb, qb, TN, preferred_element_type=F32)
            dq = dq + lax.dot_general(dzb, kb, NN, preferred_element_type=F32)
            return dq, cl + jnp.sum(l, axis=1, keepdims=True), cg + jnp.sum(g, axis=1, keepdims=True)

        init = (jnp.zeros((t, SB_HD), F32), jnp.zeros((t, 1), F32), jnp.zeros((t, 1), F32))
        carry = lax.fori_loop(0, i, lambda j, cr: tile(j, cr, False), init)
        dq, _, _ = tile(i, carry, True)
        dq_ref[...] = dq.astype(dq_ref.dtype)

        @pl.when(i == nq - 1)
        def _():
            dk_ref[...] = dka[...].astype(dk_ref.dtype)
            dv_ref[...] = dva[...].astype(dv_ref.dtype)

    blk = lambda off: pl.BlockSpec((s, SB_HD), functools.partial(lambda h, i, off: (0, off + h), off=off))
    tile_spec = pl.BlockSpec((t, SB_HD), lambda h, i: (i, h))
    full = jax.ShapeDtypeStruct((s, heads * SB_HD), BF16)
    return pl.pallas_call(
        body, name=name, grid=(heads, nq),
        in_specs=[tile_spec, blk(heads), blk(2 * heads), tile_spec, pl.BlockSpec((t, LANES), lambda h, i: (i, h))],
        out_specs=[tile_spec, blk(0), blk(0)],
        out_shape=[full, full, full],
        scratch_shapes=[pltpu.VMEM((s, SB_HD), F32), pltpu.VMEM((s, SB_HD), F32)],
        compiler_params=_params(12 * s * SB_HD * 4 + (8 << 20)),
    )(zm, zm, zm, dy, ltot)


def _conv_taps(u, w_ref, rows_i):
    taps = []
    for j in range(CONV_W):
        sh = CONV_W - 1 - j
        if sh == 0:
            taps.append(u)
        else:
            taps.append(jnp.where(rows_i >= sh, pltpu.roll(u, sh, 0), 0.0))
    return taps


def _conv_fwd(zm, col0, width, cw, cb, *, name):
    s = zm.shape[0]
    bw = _pick(width, (LANES,))
    off = col0 // bw

    def body(u_ref, w_ref, b_ref, o_ref):
        u = u_ref[...]
        rows_i = lax.broadcasted_iota(jnp.int32, u.shape, 0)
        acc = jnp.broadcast_to(b_ref[...], u.shape)
        for j, tp in enumerate(_conv_taps(u, w_ref, rows_i)):
            acc = acc + tp * w_ref[j:j + 1, :]
        o_ref[...] = acc * _sigmoid(acc)

    return pl.pallas_call(
        body, name=name, grid=(width // bw,),
        in_specs=[pl.BlockSpec((s, bw), lambda j: (0, off + j)), pl.BlockSpec((CONV_W, bw), lambda j: (0, j)),
                  pl.BlockSpec((1, bw), lambda j: (0, j))],
        out_specs=pl.BlockSpec((s, bw), lambda j: (0, j)),
        out_shape=jax.ShapeDtypeStruct((s, width), F32),
        compiler_params=_params(12 * s * bw * 4 + (4 << 20)),
    )(zm, cw, cb)


def _conv_bwd(zm, col0, width, cw, cb, dqk, *, name):
    s = zm.shape[0]
    bw = _pick(width, (LANES,))
    off = col0 // bw

    def body(u_ref, w_ref, b_ref, d_ref, du_ref, dw_ref, db_ref):
        u = u_ref[...]
        rows_i = lax.broadcasted_iota(jnp.int32, u.shape, 0)
        taps = _conv_taps(u, w_ref, rows_i)
        acc = jnp.broadcast_to(b_ref[...], u.shape)
        for j, tp in enumerate(taps):
            acc = acc + tp * w_ref[j:j + 1, :]
        sg = _sigmoid(acc)
        dc = d_ref[...] * (sg * (1.0 + acc * (1.0 - sg)))
        du = jnp.zeros_like(u)
        for j in range(CONV_W):
            sh = CONV_W - 1 - j
            if sh == 0:
                du = du + dc * w_ref[j:j + 1, :]
            else:
                du = du + jnp.where(rows_i < s - sh, pltpu.roll(dc, s - sh, 0), 0.0) * w_ref[j:j + 1, :]
            dw_ref[j:j + 1, :] = jnp.sum(dc * taps[j], axis=0, keepdims=True)
        du_ref[...] = du.astype(du_ref.dtype)
        db_ref[...] = jnp.sum(dc, axis=0, keepdims=True)

    return pl.pallas_call(
        body, name=name, grid=(width // bw,),
        in_specs=[pl.BlockSpec((s, bw), lambda j: (0, off + j)), pl.BlockSpec((CONV_W, bw), lambda j: (0, j)),
                  pl.BlockSpec((1, bw), lambda j: (0, j)), pl.BlockSpec((s, bw), lambda j: (0, j))],
        out_specs=[pl.BlockSpec((s, bw), lambda j: (0, j)), pl.BlockSpec((CONV_W, bw), lambda j: (0, j)),
                   pl.BlockSpec((1, bw), lambda j: (0, j))],
        out_shape=[jax.ShapeDtypeStruct((s, width), BF16), jax.ShapeDtypeStruct((CONV_W, width), F32),
                   jax.ShapeDtypeStruct((1, width), F32)],
        compiler_params=_params(20 * s * bw * 4 + (4 << 20)),
    )(zm, cw, cb, dqk)


def _ml_gates(gcol_ref, grow_ref):
    l = CHUNK
    r = lax.broadcasted_iota(jnp.int32, (l, l), 0)
    c = lax.broadcasted_iota(jnp.int32, (l, l), 1)
    gcol = gcol_ref[...]
    grow = grow_ref[0]
    bcol = _u01dot((c <= r).astype(BF16), gcol)
    brow = _dot01(grow, (r <= c).astype(BF16))
    return gcol, grow, bcol, brow, r >= c


def _ml_chunk(h, dh, mq_ref, mk_ref, v_ref, gates, cp, n_prev, m_prev):
    gcol, grow, bcol, brow, tri = gates
    l = CHUNK
    sl = slice(h * dh, (h + 1) * dh)
    qc = mq_ref[:, sl]
    kc = mk_ref[:, sl] * (dh ** -0.5)
    vc = v_ref[:, sl]
    i_row = grow[h:h + 1, :]
    i_col = gcol[:, h:h + 1]
    b_col = bcol[:, ML_HEADS + h:ML_HEADS + h + 1]
    b_row = brow[ML_HEADS + h:ML_HEADS + h + 1, :]
    b_end = b_col[l - 1:l, :]
    d = jnp.where(tri, b_col - b_row + i_row, -jnp.inf)
    m_inter = b_col + m_prev
    m_t = jnp.maximum(m_inter, jnp.max(d, axis=1, keepdims=True))
    w = jnp.exp(d - m_t)
    s_inter = jnp.exp(m_inter - m_t)
    qb, kb, vb = qc.astype(BF16), kc.astype(BF16), vc.astype(BF16)
    cpb = cp.astype(BF16)
    a = lax.dot_general(qb, kb, NT, preferred_element_type=F32)
    sc = a * w
    qcp = lax.dot_general(qb, cpb, NT, preferred_element_type=F32)
    qn = jnp.sum(qc * n_prev, axis=1, keepdims=True)
    num = lax.dot_general(sc.astype(BF16), vb, NN, preferred_element_type=F32) + s_inter * qcp
    den = jnp.sum(sc, axis=1, keepdims=True) + s_inter * qn
    floor = jnp.exp(-m_t)
    dnm = jnp.maximum(jnp.abs(den), floor)
    g_col = b_end - b_col + i_col
    g_row = b_end - b_row + i_row
    m_new = jnp.maximum(b_end + m_prev, jnp.max(g_row, axis=1, keepdims=True))
    decay = jnp.exp(b_end + m_prev - m_new)
    wk = jnp.exp(g_col - m_new)
    return dict(qc=qc, kc=kc, vc=vc, qb=qb, kb=kb, vb=vb, cpb=cpb, w=w, s_inter=s_inter, a=a, sc=sc, qcp=qcp, qn=qn,
                num=num, den=den, floor=floor, dnm=dnm, m_new=m_new, decay=decay, wk=wk, sl=sl)


def _ml_fwd(mqk, zm, vcol, gcol, grow, d_model, *, name):
    s = zm.shape[0]
    nc = s // CHUNK
    dh = d_model // ML_HEADS
    hh = ML_HEADS

    def body(mq_ref, mk_ref, v_ref, gcol_ref, grow_ref, h_ref, cs_ref, ns_ref, ms_ref, c_s, n_s, m_s):
        @pl.when(pl.program_id(0) == 0)
        def _():
            c_s[...] = jnp.zeros_like(c_s)
            n_s[...] = jnp.zeros_like(n_s)
            m_s[...] = jnp.zeros_like(m_s)

        gates = _ml_gates(gcol_ref, grow_ref)
        for h in range(hh):
            cp, n_prev, m_prev = c_s[h], n_s[h], m_s[h][:, 0:1]
            cs_ref[0, h] = cp
            ns_ref[0, h] = n_prev
            ms_ref[0, h] = m_s[h]
            f = _ml_chunk(h, dh, mq_ref, mk_ref, v_ref, gates, cp, n_prev, m_prev)
            h_ref[:, f["sl"]] = f["num"] / f["dnm"]
            c_s[h] = f["decay"] * cp + lax.dot_general((f["vc"] * f["wk"]).astype(BF16), f["kb"], TN,
                                                       preferred_element_type=F32)
            n_s[h] = f["decay"] * n_prev + jnp.sum(f["wk"] * f["kc"], axis=0, keepdims=True)
            m_s[h] = jnp.broadcast_to(f["m_new"], (1, LANES))

    dblk = d_model
    return pl.pallas_call(
        body, name=name, grid=(nc,),
        in_specs=[pl.BlockSpec((CHUNK, dblk), lambda c: (c, 0)), pl.BlockSpec((CHUNK, dblk), lambda c: (c, 1)),
                  pl.BlockSpec((CHUNK, dblk), lambda c: (c, vcol // dblk)),
                  pl.BlockSpec((CHUNK, LANES), lambda c: (c, 0)), pl.BlockSpec((1, 8, CHUNK), lambda c: (c, 0, 0))],
        out_specs=[pl.BlockSpec((CHUNK, dblk), lambda c: (c, 0)),
                   pl.BlockSpec((1, hh, dh, dh), lambda c: (c, 0, 0, 0)),
                   pl.BlockSpec((1, hh, 1, dh), lambda c: (c, 0, 0, 0)),
                   pl.BlockSpec((1, hh, 1, LANES), lambda c: (c, 0, 0, 0))],
        out_shape=[jax.ShapeDtypeStruct((s, d_model), F32), jax.ShapeDtypeStruct((nc, hh, dh, dh), F32),
                   jax.ShapeDtypeStruct((nc, hh, 1, dh), F32), jax.ShapeDtypeStruct((nc, hh, 1, LANES), F32)],
        scratch_shapes=[pltpu.VMEM((hh, dh, dh), F32), pltpu.VMEM((hh, 1, dh), F32), pltpu.VMEM((hh, 1, LANES), F32)],
        compiler_params=_params(8 * hh * dh * dh * 4 + (16 << 20)),
    )(mqk, mqk, zm, gcol, grow)


def _ml_bwd(mqk, zm, vcol, gcol, grow, cs, ns, ms, dhm, d_model, *, name):
    s = zm.shape[0]
    nc = s // CHUNK
    dh = d_model // ML_HEADS
    hh = ML_HEADS
    l = CHUNK

    def body(mq_ref, mk_ref, v_ref, gcol_ref, grow_ref, cs_ref, ns_ref, ms_ref, dh_ref,
             dq_ref, dk_ref, dv_ref, dgc_ref, dgr_ref, dc_s, dn_s):
        @pl.when(pl.program_id(0) == 0)
        def _():
            dc_s[...] = jnp.zeros_like(dc_s)
            dn_s[...] = jnp.zeros_like(dn_s)

        gates = _ml_gates(gcol_ref, grow_ref)
        lane = lax.broadcasted_iota(jnp.int32, (l, LANES), 1)
        rowi = lax.broadcasted_iota(jnp.int32, (8, l), 0)
        lastrow = lax.broadcasted_iota(jnp.int32, (l, 1), 0) == l - 1
        dgc = jnp.zeros((l, LANES), F32)
        dgr = jnp.zeros((8, l), F32)
        for h in range(hh):
            cp, n_prev, m_prev = cs_ref[0, h], ns_ref[0, h], ms_ref[0, h][:, 0:1]
            f = _ml_chunk(h, dh, mq_ref, mk_ref, v_ref, gates, cp, n_prev, m_prev)
            dC, dn = dc_s[h], dn_s[h]
            dhv = dh_ref[:, f["sl"]]
            dnum = dhv / f["dnm"]
            hv = f["num"] / f["dnm"]
            ddnm = -jnp.sum(dhv * hv, axis=1, keepdims=True) / f["dnm"]
            dden = jnp.where(jnp.abs(f["den"]) >= f["floor"], ddnm * jnp.sign(f["den"]), 0.0)
            dnb = dnum.astype(BF16)
            dsc = lax.dot_general(dnb, f["vb"], NT, preferred_element_type=F32) + dden
            dvc = lax.dot_general(f["sc"].astype(BF16), dnb, TN, preferred_element_type=F32)
            ds_inter = jnp.sum(dnum * f["qcp"], axis=1, keepdims=True) + dden * f["qn"]
            sdn = (f["s_inter"] * dnum).astype(BF16)
            sdd = f["s_inter"] * dden
            da = dsc * f["w"]
            dab = da.astype(BF16)
            dqc = (lax.dot_general(dab, f["kb"], NN, preferred_element_type=F32)
                   + lax.dot_general(sdn, f["cpb"], NN, preferred_element_type=F32) + sdd * n_prev)
            dcp = f["decay"] * dC + lax.dot_general(sdn, f["qb"], TN, preferred_element_type=F32)
            dnp = f["decay"] * dn + jnp.sum(sdd * f["qc"], axis=0, keepdims=True)
            vw = (f["vc"] * f["wk"]).astype(BF16)
            dCb = dC.astype(BF16)
            dkc = (lax.dot_general(dab, f["qb"], TN, preferred_element_type=F32)
                   + lax.dot_general(vw, dCb, NN, preferred_element_type=F32) + f["wk"] * dn)
            e = lax.dot_general(f["kb"], dCb, NT, preferred_element_type=F32)
            dvc = dvc + e * f["wk"]
            dwk = jnp.sum(e * f["vc"], axis=1, keepdims=True) + jnp.sum(f["kc"] * dn, axis=1, keepdims=True)
            ddecay = jnp.sum(jnp.sum(dC * cp, axis=1, keepdims=True), axis=0, keepdims=True) \
                + jnp.sum(dn * n_prev, axis=1, keepdims=True)
            dd = dsc * f["sc"]
            dlw = dwk * f["wk"]
            db_end = jnp.sum(dlw, axis=0, keepdims=True) + ddecay * f["decay"]
            di_col = dlw
            db_col = jnp.sum(dd, axis=1, keepdims=True) + ds_inter * f["s_inter"] - dlw \
                + jnp.where(lastrow, db_end, 0.0)
            cs_dd = jnp.sum(dd, axis=0, keepdims=True)
            dgc = dgc + jnp.where(lane == h, di_col, 0.0) + jnp.where(lane == hh + h, db_col, 0.0)
            dgr = dgr + jnp.where(rowi == h, cs_dd, 0.0) - jnp.where(rowi == hh + h, cs_dd, 0.0)
            dq_ref[:, f["sl"]] = dqc
            dk_ref[:, f["sl"]] = dkc * (dh ** -0.5)
            dv_ref[:, f["sl"]] = dvc.astype(dv_ref.dtype)
            dc_s[h] = dcp
            dn_s[h] = dnp
        dgc_ref[...] = dgc
        dgr_ref[0] = dgr

    dblk = d_model
    rev = lambda c: nc - 1 - c
    return pl.pallas_call(
        body, name=name, grid=(nc,),
        in_specs=[pl.BlockSpec((l, dblk), lambda c: (rev(c), 0)), pl.BlockSpec((l, dblk), lambda c: (rev(c), 1)),
                  pl.BlockSpec((l, dblk), lambda c: (rev(c), vcol // dblk)),
                  pl.BlockSpec((l, LANES), lambda c: (rev(c), 0)), pl.BlockSpec((1, 8, l), lambda c: (rev(c), 0, 0)),
                  pl.BlockSpec((1, hh, dh, dh), lambda c: (rev(c), 0, 0, 0)),
                  pl.BlockSpec((1, hh, 1, dh), lambda c: (rev(c), 0, 0, 0)),
                  pl.BlockSpec((1, hh, 1, LANES), lambda c: (rev(c), 0, 0, 0)),
                  pl.BlockSpec((l, dblk), lambda c: (rev(c), 0))],
        out_specs=[pl.BlockSpec((l, dblk), lambda c: (rev(c), 0)), pl.BlockSpec((l, dblk), lambda c: (rev(c), 0)),
                   pl.BlockSpec((l, dblk), lambda c: (rev(c), 0)), pl.BlockSpec((l, LANES), lambda c: (rev(c), 0)),
                   pl.BlockSpec((1, 8, l), lambda c: (rev(c), 0, 0))],
        out_shape=[jax.ShapeDtypeStruct((s, d_model), F32), jax.ShapeDtypeStruct((s, d_model), F32),
                   jax.ShapeDtypeStruct((s, d_model), BF16), jax.ShapeDtypeStruct((s, LANES), F32),
                   jax.ShapeDtypeStruct((nc, 8, l), F32)],
        scratch_shapes=[pltpu.VMEM((hh, dh, dh), F32), pltpu.VMEM((hh, 1, dh), F32)],
        compiler_params=_params(10 * hh * dh * dh * 4 + (16 << 20)),
    )(mqk, mqk, zm, gcol, grow, cs, ns, ms, dhm)


def _xa_fwd(zm, qcol, kv, gq, gk, d_model, *, name, tq=256):
    s = zm.shape[0]
    nm = kv.shape[0]
    dh = d_model // X_HEADS
    tq = _pick(s, (tq, 128, 64))
    scale = dh ** -0.5

    def body(q_ref, k_ref, v_ref, gq_ref, gk_ref, o_ref):
        qn = _rms_fwd(q_ref[...], gq_ref[...])
        kn = _rms_fwd(k_ref[...], gk_ref[...])
        lg = _dot(qn, kn, NT) * scale
        lg = lg - jnp.max(lg, axis=1, keepdims=True)
        p = jnp.exp(lg)
        p = p / jnp.sum(p, axis=1, keepdims=True)
        o_ref[...] = _dot(p, v_ref[...], NN).astype(o_ref.dtype)

    return pl.pallas_call(
        body, name=name, grid=(X_HEADS, s // tq),
        in_specs=[pl.BlockSpec((tq, dh), lambda h, i: (i, qcol // dh + h)), pl.BlockSpec((nm, dh), lambda h, i: (0, h)),
                  pl.BlockSpec((nm, dh), lambda h, i: (0, X_HEADS + h)),
                  pl.BlockSpec((1, dh), lambda h, i: (0, 0)), pl.BlockSpec((1, dh), lambda h, i: (0, 0))],
        out_specs=pl.BlockSpec((tq, dh), lambda h, i: (i, h)),
        out_shape=jax.ShapeDtypeStruct((s, d_model), BF16),
        compiler_params=_params(32 << 20),
    )(zm, kv, kv, gq, gk)


def _xa_bwd(zm, qcol, kv, gq, gk, dy, d_model, *, name, tq=256):
    s = zm.shape[0]
    nm = kv.shape[0]
    dh = d_model // X_HEADS
    tq = _pick(s, (tq, 128, 64))
    nq = s // tq
    scale = dh ** -0.5

    def body(q_ref, k_ref, v_ref, gq_ref, gk_ref, do_ref, dq_ref, dkn_ref, dv_ref, dgq_ref):
        h, i = pl.program_id(0), pl.program_id(1)

        @pl.when(i == 0)
        def _():
            dkn_ref[...] = jnp.zeros_like(dkn_ref)
            dv_ref[...] = jnp.zeros_like(dv_ref)

        @pl.when((i == 0) & (h == 0))
        def _():
            dgq_ref[...] = jnp.zeros_like(dgq_ref)

        q = q_ref[...]
        qn = _rms_fwd(q, gq_ref[...])
        kn = _rms_fwd(k_ref[...], gk_ref[...])
        lg = _dot(qn, kn, NT) * scale
        lg = lg - jnp.max(lg, axis=1, keepdims=True)
        p = jnp.exp(lg)
        p = p / jnp.sum(p, axis=1, keepdims=True)
        do = do_ref[...]
        dv_ref[...] += _dot(p, do, TN)
        dp = _dot(do, v_ref[...], NT)
        dlg = p * (dp - jnp.sum(dp * p, axis=1, keepdims=True)) * scale
        dqn = _dot(dlg, kn, NN)
        dkn_ref[...] += _dot(dlg, qn, TN)
        dq, dgq = _rms_bwd(q, gq_ref[...], dqn)
        dq_ref[...] = dq.astype(dq_ref.dtype)
        dgq_ref[...] += jnp.sum(dgq, axis=0, keepdims=True)

    return pl.pallas_call(
        body, name=name, grid=(X_HEADS, nq),
        in_specs=[pl.BlockSpec((tq, dh), lambda h, i: (i, qcol // dh + h)), pl.BlockSpec((nm, dh), lambda h, i: (0, h)),
                  pl.BlockSpec((nm, dh), lambda h, i: (0, X_HEADS + h)),
                  pl.BlockSpec((1, dh), lambda h, i: (0, 0)), pl.BlockSpec((1, dh), lambda h, i: (0, 0)),
                  pl.BlockSpec((tq, dh), lambda h, i: (i, h))],
        out_specs=[pl.BlockSpec((tq, dh), lambda h, i: (i, h)), pl.BlockSpec((nm, dh), lambda h, i: (0, h)),
                   pl.BlockSpec((nm, dh), lambda h, i: (0, h)), pl.BlockSpec((1, dh), lambda h, i: (0, 0))],
        out_shape=[jax.ShapeDtypeStruct((s, d_model), BF16), jax.ShapeDtypeStruct((nm, d_model), F32),
                   jax.ShapeDtypeStruct((nm, d_model), F32), jax.ShapeDtypeStruct((1, dh), F32)],
        compiler_params=_params(32 << 20),
    )(zm, kv, kv, gq, gk, dy)


def _place():
    return lax.axis_index("x"), lax.axis_index("y"), lax.axis_index("c")


ANY = pl.BlockSpec(memory_space=pl.ANY)


def _allgather_quarters(shards, *, name):
    n = len(shards)

    def body(*refs):
        ins, outs = refs[:n], refs[n:2 * n]
        send, recv, loc = refs[2 * n:]
        x, y, c = _place()
        chips = [(1 - x, y), (x, 1 - y), (1 - x, 1 - y)]
        local = []
        for t in range(n):
            cp = pltpu.make_async_copy(ins[t], outs[t].at[2 * x + y], loc.at[t])
            cp.start()
            local.append(cp)

        def copy(t, j, slot):
            return pltpu.make_async_remote_copy(
                src_ref=ins[t], dst_ref=outs[t].at[slot], send_sem=send.at[3 * t + j], recv_sem=recv.at[3 * t + j],
                device_id=(chips[j][0], chips[j][1], c), device_id_type=MESH)

        for t in range(n):
            for j in range(3):
                copy(t, j, 2 * x + y).start()
        for t in range(n):
            for j in range(3):
                copy(t, j, 2 * chips[j][0] + chips[j][1]).wait_recv()
        for t in range(n):
            for j in range(3):
                copy(t, j, 2 * x + y).wait_send()
        for cp in local:
            cp.wait()

    return pl.pallas_call(
        body, name=name, in_specs=[ANY] * n, out_specs=[ANY] * n,
        out_shape=[jax.ShapeDtypeStruct((4,) + a.shape, a.dtype) for a in shards],
        scratch_shapes=[pltpu.SemaphoreType.DMA((3 * n,)), pltpu.SemaphoreType.DMA((3 * n,)),
                        pltpu.SemaphoreType.DMA((n,))],
    )(*shards)


def _exchange_grads(parts, *, name):
    n = len(parts)

    def body(*refs):
        ins, outs = refs[:n], refs[n:2 * n]
        send, recv, loc = refs[2 * n:]
        x, y, c = _place()
        me = 4 * x + 2 * y + c
        peers = [(x ^ ((j >> 2) & 1), y ^ ((j >> 1) & 1), c ^ (j & 1)) for j in range(1, 8)]
        local = []
        for t in range(n):
            cp = pltpu.make_async_copy(ins[t].at[2 * x + y, c], outs[t].at[me], loc.at[t])
            cp.start()
            local.append(cp)

        def copy(t, j, slot):
            px, py, pc = peers[j]
            return pltpu.make_async_remote_copy(
                src_ref=ins[t].at[2 * px + py, pc], dst_ref=outs[t].at[slot], send_sem=send.at[7 * t + j],
                recv_sem=recv.at[7 * t + j], device_id=(px, py, pc), device_id_type=MESH)

        for t in range(n):
            for j in range(7):
                copy(t, j, me).start()
        for t in range(n):
            for j in range(7):
                px, py, pc = peers[j]
                copy(t, j, 4 * px + 2 * py + pc).wait_recv()
        for t in range(n):
            for j in range(7):
                copy(t, j, me).wait_send()
        for cp in local:
            cp.wait()

    return pl.pallas_call(
        body, name=name, in_specs=[ANY] * n, out_specs=[ANY] * n,
        out_shape=[jax.ShapeDtypeStruct((8,) + a.shape[2:], a.dtype) for a in parts],
        scratch_shapes=[pltpu.SemaphoreType.DMA((7 * n,)), pltpu.SemaphoreType.DMA((7 * n,)),
                        pltpu.SemaphoreType.DMA((n,))],
    )(*parts)


def _sum8(parts, *, name):
    _, r, c = parts.shape
    t = _pick(r, (128, 64, 32, 16, 8))

    def body(p_ref, o_ref):
        acc = p_ref[0].astype(F32)
        for k in range(1, 8):
            acc = acc + p_ref[k].astype(F32)
        o_ref[...] = acc

    return pl.pallas_call(
        body, name=name, grid=(r // t,), in_specs=[pl.BlockSpec((8, t, c), lambda i: (0, i, 0))],
        out_specs=pl.BlockSpec((t, c), lambda i: (i, 0)), out_shape=jax.ShapeDtypeStruct((r, c), F32),
        compiler_params=_params(2 * 8 * t * c * 2 + 6 * t * c * 4 + (4 << 20)),
    )(parts)


def _swap_halves(halves, *, name):
    n = len(halves)

    def body(*refs):
        ins, outs = refs[:n], refs[n:2 * n]
        sbuf, rbuf = refs[2 * n:3 * n], refs[3 * n:4 * n]
        send, recv, loc_own, loc_in, loc_out = refs[4 * n:]
        x, y, c = _place()
        local, stage = [], []
        for t in range(n):
            cp = pltpu.make_async_copy(ins[t], outs[t].at[c], loc_own.at[t])
            cp.start()
            local.append(cp)
            cp = pltpu.make_async_copy(ins[t], sbuf[t], loc_in.at[t])
            cp.start()
            stage.append(cp)

        def copy(t):
            return pltpu.make_async_remote_copy(
                src_ref=sbuf[t], dst_ref=rbuf[t], send_sem=send.at[t], recv_sem=recv.at[t],
                device_id=(x, y, 1 - c), device_id_type=MESH)

        for t in range(n):
            stage[t].wait()
            copy(t).start()
        for t in range(n):
            copy(t).wait_recv()
            cp = pltpu.make_async_copy(rbuf[t], outs[t].at[1 - c], loc_out.at[t])
            cp.start()
            local.append(cp)
        for t in range(n):
            copy(t).wait_send()
        for cp in local:
            cp.wait()

    stage_bytes = 2 * sum(_nbytes(a.shape, a.dtype) for a in halves)
    return pl.pallas_call(
        body, name=name, in_specs=[ANY] * n, out_specs=[ANY] * n,
        out_shape=[jax.ShapeDtypeStruct((2,) + a.shape, a.dtype) for a in halves],
        scratch_shapes=[pltpu.VMEM(a.shape, a.dtype) for a in halves] * 2 + [pltpu.SemaphoreType.DMA((n,))] * 5,
        compiler_params=_params(stage_bytes + (4 << 20)),
    )(*halves)


def _allreduce_small(p, *, name):
    r = p.shape[0]

    def body(p_ref, o_ref, buf, send, recv):
        x, y, c = _place()
        me = 4 * x + 2 * y + c
        peers = [(x ^ ((j >> 2) & 1), y ^ ((j >> 1) & 1), c ^ (j & 1)) for j in range(1, 8)]

        def copy(j, slot):
            return pltpu.make_async_remote_copy(
                src_ref=p_ref, dst_ref=buf.at[slot], send_sem=send.at[j], recv_sem=recv.at[j],
                device_id=peers[j], device_id_type=MESH)

        for j in range(7):
            copy(j, me).start()
        buf[me] = p_ref[...]
        for j in range(7):
            px, py, pc = peers[j]
            copy(j, 4 * px + 2 * py + pc).wait_recv()
        for j in range(7):
            copy(j, me).wait_send()
        acc = buf[0]
        for k in range(1, 8):
            acc = acc + buf[k]
        o_ref[...] = acc

    vspec = pl.BlockSpec(memory_space=pltpu.VMEM)
    return pl.pallas_call(
        body, name=name, in_specs=[vspec], out_specs=vspec, out_shape=jax.ShapeDtypeStruct((r, LANES), F32),
        scratch_shapes=[pltpu.VMEM((8, r, LANES), F32), pltpu.SemaphoreType.DMA((7,)), pltpu.SemaphoreType.DMA((7,))],
    )(p)


def _adamw_fn(w, g, m, v):
    m = ADAM_B1 * m + (1.0 - ADAM_B1) * g
    v = ADAM_B2 * v + (1.0 - ADAM_B2) * (g * g)
    m_hat = m / (1.0 - ADAM_B1 ** ADAM_STEP)
    v_hat = v / (1.0 - ADAM_B2 ** ADAM_STEP)
    delta = -ADAM_LR * (m_hat / (jnp.sqrt(v_hat) + ADAM_EPS) + ADAM_WD * w)
    return delta, m, v


def _adamw(w, g, m, v, *, name):
    c = w.shape[1]
    return _rowwise(_adamw_fn, [w, g, m, v], [], [(c, F32)] * 3, name=name, tr=128)


def _pack(vecs, rows):
    flat = jnp.concatenate([a.reshape(-1).astype(F32) for a in vecs])
    return jnp.pad(flat, (0, rows * LANES - flat.shape[0])).reshape(rows, LANES)


def _unpack(p, like):
    flat, out, o = p.reshape(-1), [], 0
    for a in like:
        out.append(flat[o:o + a.size].reshape(a.shape))
        o += a.size
    return out


def kernel(x, mem, g_mix, w_in, b_if, b_gate, conv_w, conv_b, ml_norm_g, g_mem, w_mem_kv, q_norm_g, k_norm_g, w_sb_proj, w_ml_proj, w_x_proj, w_out, g_mlp, w_ff1, w_ff2, loss_target, m_g_mix, m_w_in, m_b_if, m_b_gate, m_conv_w, m_conv_b, m_ml_norm_g, m_g_mem, m_w_mem_kv, m_q_norm_g, m_k_norm_g, m_w_sb_proj, m_w_ml_proj, m_w_x_proj, m_w_out, m_g_mlp, m_w_ff1, m_w_ff2, v_g_mix, v_w_in, v_b_if, v_b_gate, v_conv_w, v_conv_b, v_ml_norm_g, v_g_mem, v_w_mem_kv, v_q_norm_g, v_k_norm_g, v_w_sb_proj, v_w_ml_proj, v_w_x_proj, v_w_out, v_g_mlp, v_w_ff1, v_w_ff2):
    _, s, d = x.shape
    nm = mem.shape[1]
    n_in = 4 * w_in.shape[2]
    dff = 4 * w_ff1.shape[2]
    sbh = d // SB_HD
    hh = ML_HEADS
    dh = d // hh
    nc = s // CHUNK
    assert n_in == 11 * d + 2 * hh and d % (2 * LANES) == 0 and s % LANES == 0
    x2, mem2, tgt = x[0], mem[0], loss_target[0]

    q_shards = [w_in[0], w_mem_kv[0], w_sb_proj[0], w_ml_proj[0], w_x_proj[0], w_out[0], w_ff1[0], w_ff2[0]]
    gath = _allgather_quarters([a.astype(BF16) for a in q_shards] + [conv_w[0]], name="gather_weights")
    cols = lambda a: a.transpose(1, 0, 2).reshape(a.shape[1], 4 * a.shape[2])
    rws = lambda a: a.reshape(4 * a.shape[1], a.shape[2])
    w_in_f = cols(gath[0])
    w_main = jnp.concatenate([w_in_f[:, :7 * d], w_in_f[:, 7 * d + 2 * hh:]], axis=1)
    w_if = jnp.pad(w_in_f[:, 7 * d:7 * d + 2 * hh], ((0, 0), (0, LANES - 2 * hh)))
    w_kv, w_sbp, w_mlp, w_xp, w_o = cols(gath[1]), rws(gath[2]), rws(gath[3]), rws(gath[4]), rws(gath[5])
    w_f1, w_f2, conv_wf = cols(gath[6]), rws(gath[7]), cols(gath[8])
    b_if_p = jnp.pad(b_if, ((0, 0), (0, LANES - 2 * hh)))

    (hn,) = _rowwise(_rms_fwd, [x2], [g_mix], [(d, BF16)], name="norm_in")
    zm = _mm(hn, w_main, name="proj_in")
    zif = _mm(hn, w_if, name="proj_if")
    y_sb, ltot = _sb_fwd(zm, sbh, name="sb_fwd")

    def gate_fn(z, b):
        pre = z + b
        lane = lax.broadcasted_iota(jnp.int32, pre.shape, 1)
        return jnp.where(lane < hh, pre, -_softplus(-pre))

    (gcol,) = _rowwise(gate_fn, [zif], [b_if_p], [(LANES, F32)], name="ml_gates")
    grow = gcol[:, :8].T.reshape(8, nc, CHUNK).transpose(1, 0, 2)
    mqk = _conv_fwd(zm, 3 * d, 2 * d, conv_wf, conv_b, name="conv_fwd")
    hm, cst, nst, mst = _ml_fwd(mqk, zm, 5 * d, gcol, grow, d, name="ml_fwd")

    def mlout_fn(hv, o, g):
        ys = [_rms_fwd(hv[:, k * dh:(k + 1) * dh], g[:, k * dh:(k + 1) * dh]) for k in range(hh)]
        return jnp.concatenate(ys, axis=1) * _sigmoid(o)

    (y_ml,) = _rowwise(mlout_fn, [hm, (zm, d, 6)], [ml_norm_g], [(d, BF16)], name="ml_out")
    (memn,) = _rowwise(_rms_fwd, [mem2], [g_mem], [(d, BF16)], name="norm_mem")
    kv = _mm(memn, w_kv, name="proj_kv")
    y_x = _xa_fwd(zm, 7 * d, kv, q_norm_g, k_norm_g, d, name="xa_fwd")
    p_sb = _mm(y_sb, w_sbp, name="proj_sb")
    p_ml = _mm(y_ml, w_mlp, name="proj_ml")
    p_x = _mm(y_x, w_xp, name="proj_x")

    def merge_fn(a, b, c, g0, g1, g2, bg):
        return (_sigmoid(g0 + bg[:, :d]) * a + _sigmoid(g1 + bg[:, d:2 * d]) * b + _sigmoid(g2 + bg[:, 2 * d:]) * c)

    gate_cols = [(zm, d, 8), (zm, d, 9), (zm, d, 10)]
    (mixed,) = _rowwise(merge_fn, [p_sb, p_ml, p_x] + gate_cols, [b_gate], [(d, BF16)], name="merge")
    x1 = _mm(mixed, w_o, add=x2, name="proj_out")
    (h2,) = _rowwise(_rms_fwd, [x1], [g_mlp], [(d, BF16)], name="norm_mlp")
    u = _mm(h2, w_f1, name="ff1")
    (act,) = _rowwise(lambda uv: jnp.square(jnp.maximum(uv, 0.0)), [u], [], [(dff, BF16)], name="relu2", tr=128)
    yo = _mm(act, w_f2, add=x1, name="ff2")

    def loss_fn(yv, tv):
        e = yv - tv
        return e * (1.0 / d), jnp.sum(e * e, axis=0, keepdims=True) * (0.5 / d)

    dy, loss_cols = _rowwise(loss_fn, [yo, tgt], [], [(d, F32)], [d], name="loss")

    dact = _mm(dy, w_f2, tb=True, name="ff2_dx")
    dw_f2 = _mm(act, dy, ta=True, name="ff2_dw")
    (du,) = _rowwise(lambda g, uv: g * 2.0 * jnp.maximum(uv, 0.0), [dact, u], [], [(dff, BF16)], name="relu2_bwd",
                     tr=128)
    dw_f1 = _mm(h2, du, ta=True, name="ff1_dw")
    dh2 = _mm(du, w_f1, tb=True, name="ff1_dx")

    def norm_bwd_fn(xv, dyv, res, g):
        dx, dg = _rms_bwd(xv, g, dyv)
        return dx + res, jnp.sum(dg, axis=0, keepdims=True)

    dx1, dg_mlp = _rowwise(norm_bwd_fn, [x1, dh2, dy], [g_mlp], [(d, F32)], [d], name="norm_mlp_bwd")
    dmixed = _mm(dx1, w_o, tb=True, name="proj_out_dx")
    dw_o = _mm(mixed, dx1, ta=True, name="proj_out_dw")

    def merge_bwd_fn(dm, a, b, c, g0, g1, g2, bg):
        outs, dgs = [], []
        for p, g, k in ((a, g0, 0), (b, g1, 1), (c, g2, 2)):
            sg = _sigmoid(g + bg[:, k * d:(k + 1) * d])
            outs.append(dm * sg)
            dgs.append(dm * p * sg * (1.0 - sg))
        dgate = jnp.concatenate(dgs, axis=1)
        return (*outs, dgate, jnp.sum(dgate, axis=0, keepdims=True))

    dp_sb, dp_ml, dp_x, dgate, db_gate = _rowwise(
        merge_bwd_fn, [dmixed, p_sb, p_ml, p_x] + gate_cols, [b_gate], [(d, BF16)] * 3 + [(3 * d, BF16)], [3 * d],
        name="merge_bwd", tr=128)
    dw_sbp = _mm(y_sb, dp_sb, ta=True, name="proj_sb_dw")
    dw_mlp = _mm(y_ml, dp_ml, ta=True, name="proj_ml_dw")
    dw_xp = _mm(y_x, dp_x, ta=True, name="proj_x_dw")
    dy_sb = _mm(dp_sb, w_sbp, tb=True, out_dtype=BF16, name="proj_sb_dx")
    dy_ml = _mm(dp_ml, w_mlp, tb=True, name="proj_ml_dx")
    dy_x = _mm(dp_x, w_xp, tb=True, out_dtype=BF16, name="proj_x_dx")

    dsq, dsk, dsv = _sb_bwd(zm, dy_sb, ltot, sbh, name="sb_bwd")

    def mlout_bwd_fn(dyv, hv, o, g):
        sg = _sigmoid(o)
        dn = dyv * sg
        dxs, dgs, ys = [], [], []
        for k in range(hh):
            sl = slice(k * dh, (k + 1) * dh)
            ys.append(_rms_fwd(hv[:, sl], g[:, sl]))
            dxk, dgk = _rms_bwd(hv[:, sl], g[:, sl], dn[:, sl])
            dxs.append(dxk)
            dgs.append(dgk)
        do = dyv * jnp.concatenate(ys, axis=1) * sg * (1.0 - sg)
        return jnp.concatenate(dxs, axis=1), do, jnp.sum(jnp.concatenate(dgs, axis=1), axis=0, keepdims=True)

    dhm, dmlo, dg_mln = _rowwise(mlout_bwd_fn, [dy_ml, hm, (zm, d, 6)], [ml_norm_g], [(d, F32), (d, BF16)], [d],
                                 name="ml_out_bwd")
    dmq, dmk, dmlv, dgc, dgr = _ml_bwd(mqk, zm, 5 * d, gcol, grow, cst, nst, mst, dhm, d, name="ml_bwd")
    dmqk = jnp.concatenate([dmq, dmk], axis=1)
    dmlqk, dconv_w, dconv_b = _conv_bwd(zm, 3 * d, 2 * d, conv_wf, conv_b, dmqk, name="conv_bwd")
    dgr_t = jnp.pad(dgr.transpose(1, 0, 2).reshape(8, s).T, ((0, 0), (0, LANES - 8)))

    def gate_bwd_fn(a, b, z, bias):
        tot = a + b
        r = lax.broadcasted_iota(jnp.int32, (CHUNK, CHUNK), 0)
        c = lax.broadcasted_iota(jnp.int32, (CHUNK, CHUNK), 1)
        dlf = _u01dot((c >= r).astype(BF16), tot)
        lane = lax.broadcasted_iota(jnp.int32, tot.shape, 1)
        dz = jnp.where(lane < hh, tot, jnp.where(lane < 2 * hh, dlf * _sigmoid(-(z + bias)), 0.0))
        return dz, jnp.sum(dz, axis=0, keepdims=True)

    dzif, db_if_p = _rowwise(gate_bwd_fn, [dgc, dgr_t, zif], [b_if_p], [(LANES, BF16)], [LANES], name="ml_gates_bwd",
                             tr=CHUNK)
    dxq, dkn, dxv, dg_qn = _xa_bwd(zm, 7 * d, kv, q_norm_g, k_norm_g, dy_x, d, name="xa_bwd")

    def knorm_bwd_fn(kvv, dknv, dvv, g):
        dks, dgs = [], []
        for k in range(X_HEADS):
            sl = slice(k * dh, (k + 1) * dh)
            dk, dg = _rms_bwd(kvv[:, sl], g, dknv[:, sl])
            dks.append(dk)
            dgs.append(jnp.sum(dg, axis=0, keepdims=True))
        return jnp.concatenate(dks + [dvv], axis=1), dgs[0] + dgs[1] + dgs[2] + dgs[3]

    dkv, dg_kn = _rowwise(knorm_bwd_fn, [(kv, d, 0), dkn, dxv], [k_norm_g], [(2 * d, BF16)], [dh], name="xa_knorm_bwd")
    dw_kv = _mm(memn, dkv, ta=True, name="proj_kv_dw")
    dmemn = _mm(dkv, w_kv, tb=True, name="proj_kv_dx")

    def gmem_fn(mv, dv_, g):
        _, dg = _rms_bwd(mv, g, dv_)
        return (jnp.sum(dg, axis=0, keepdims=True),)

    (dg_mem,) = _rowwise(gmem_fn, [mem2, dmemn], [g_mem], [], [d], name="norm_mem_bwd")

    dzm = jnp.concatenate([dsq, dsk, dsv, dmlqk, dmlv, dmlo, dxq, dgate], axis=1)
    dw_main = _mm(hn, dzm, ta=True, name="proj_in_dw")
    dw_if = _mm(hn, dzif, ta=True, name="proj_if_dw")
    dhn = _mm(dzm, w_main, tb=True, name="proj_in_dx")
    dhn = _mm(dzif, w_if, tb=True, add=dhn, name="proj_if_dx")
    dx, dg_mix = _rowwise(norm_bwd_fn, [x2, dhn, dx1], [g_mix], [(d, F32)], [d], name="norm_in_bwd")

    dw_in = jnp.concatenate([dw_main[:, :7 * d], dw_if[:, :2 * hh], dw_main[:, 7 * d:]], axis=1)
    uncols = lambda a: a.reshape(a.shape[0], 4, a.shape[1] // 4).transpose(1, 0, 2)
    unrws = lambda a: a.reshape(4, a.shape[0] // 4, a.shape[1])
    quarters = [uncols(dw_in), uncols(dw_kv), unrws(dw_sbp), unrws(dw_mlp), unrws(dw_xp), unrws(dw_o), uncols(dw_f1),
                unrws(dw_f2)]
    parts = [q.astype(BF16).reshape(4, 2, q.shape[1] // 2, q.shape[2]) for q in quarters]
    recv = _exchange_grads(parts, name="exchange_grads")
    halves = [_sum8(r, name=f"sum_grads_{i}") for i, r in enumerate(recv)]
    both = _swap_halves(halves, name="swap_halves")
    g_big = [b.reshape(2 * b.shape[1], b.shape[2]) for b in both]

    small_g = [dg_mix, db_if_p[:, :2 * hh], db_gate, dconv_w, dconv_b, dg_mln, dg_mem, dg_qn, dg_kn, dg_mlp,
               jnp.sum(loss_cols).reshape(1, 1)]
    n_small = sum(a.size for a in small_g)
    rows = -(-n_small // (8 * LANES)) * 8
    g_small = _unpack(_allreduce_small(_pack(small_g, rows), name="allreduce_small"), small_g)
    loss = g_small[-1].reshape(())
    k4 = 2 * lax.axis_index("x") + lax.axis_index("y")
    qw = conv_w.shape[2]
    g_conv_w = lax.dynamic_slice_in_dim(g_small[3], k4 * qw, qw, axis=1)
    g_small_w = [g_small[0], g_small[1], g_small[2], g_conv_w] + g_small[4:10]
    sm_w = [g_mix, b_if, b_gate, conv_w[0], conv_b, ml_norm_g, g_mem, q_norm_g, k_norm_g, g_mlp]
    sm_m = [m_g_mix, m_b_if, m_b_gate, m_conv_w[0], m_conv_b, m_ml_norm_g, m_g_mem, m_q_norm_g, m_k_norm_g, m_g_mlp]
    sm_v = [v_g_mix, v_b_if, v_b_gate, v_conv_w[0], v_conv_b, v_ml_norm_g, v_g_mem, v_q_norm_g, v_k_norm_g, v_g_mlp]
    n_sw = sum(a.size for a in sm_w)
    rows_w = -(-n_sw // (8 * LANES)) * 8
    sm_out = _adamw(_pack(sm_w, rows_w), _pack(g_small_w, rows_w), _pack(sm_m, rows_w), _pack(sm_v, rows_w),
                    name="adamw_small")
    sm_delta, sm_newm, sm_newv = [_unpack(p, sm_w) for p in sm_out]

    big_w = [w_in[0], w_mem_kv[0], w_sb_proj[0], w_ml_proj[0], w_x_proj[0], w_out[0], w_ff1[0], w_ff2[0]]
    big_m = [m_w_in[0], m_w_mem_kv[0], m_w_sb_proj[0], m_w_ml_proj[0], m_w_x_proj[0], m_w_out[0], m_w_ff1[0],
             m_w_ff2[0]]
    big_v = [v_w_in[0], v_w_mem_kv[0], v_w_sb_proj[0], v_w_ml_proj[0], v_w_x_proj[0], v_w_out[0], v_w_ff1[0],
             v_w_ff2[0]]
    big_out = [_adamw(w, g, m, v, name=f"adamw_{i}") for i, (w, g, m, v) in enumerate(zip(big_w, g_big, big_m, big_v))]

    order = ["g_mix", "w_in", "b_if", "b_gate", "conv_w", "conv_b", "ml_norm_g", "g_mem", "w_mem_kv", "q_norm_g",
             "k_norm_g", "w_sb_proj", "w_ml_proj", "w_x_proj", "w_out", "g_mlp", "w_ff1", "w_ff2"]
    small_names = ["g_mix", "b_if", "b_gate", "conv_w", "conv_b", "ml_norm_g", "g_mem", "q_norm_g", "k_norm_g", "g_mlp"]
    big_names = ["w_in", "w_mem_kv", "w_sb_proj", "w_ml_proj", "w_x_proj", "w_out", "w_ff1", "w_ff2"]
    grads, deltas, new_m, new_v = {}, {}, {}, {}
    for i, nme in enumerate(small_names):
        shp = sm_w[i].shape if nme != "conv_w" else conv_w.shape
        grads[nme] = g_small_w[i].reshape(shp)
        deltas[nme], new_m[nme], new_v[nme] = (sm_delta[i].reshape(shp), sm_newm[i].reshape(shp),
                                               sm_newv[i].reshape(shp))
    for i, nme in enumerate(big_names):
        grads[nme] = g_big[i][None]
        deltas[nme], new_m[nme], new_v[nme] = (o[None] for o in big_out[i])
    return (loss, dx[None], *[grads[k] for k in order], *[deltas[k] for k in order], *[new_m[k] for k in order],
            *[new_v[k] for k in order])
```

```python
import functools

import jax
import jax.numpy as jnp
from jax import lax
from jax.experimental import pallas as pl
from jax.experimental.pallas import tpu as pltpu

F32 = jnp.float32
BF16 = jnp.bfloat16
MESH = pl.DeviceIdType.MESH

EPS = 1e-6
SB_HD = 128
ML_HEADS = 4
X_HEADS = 4
CHUNK = 64
CONV_W = 4
LANES = 128
ADAM_LR = 0.001
ADAM_B1 = 0.9
ADAM_B2 = 0.999
ADAM_EPS = 1e-08
ADAM_WD = 0.01
ADAM_STEP = 10
VMEM_CAP = 56 * 1024 * 1024
NEG = -1e30

NT = (((1,), (1,)), ((), ()))
NN = (((1,), (0,)), ((), ()))
TN = (((0,), (0,)), ((), ()))


def _dot(a, b, dn=NN):
    return lax.dot_general(a.astype(BF16), b.astype(BF16), dn, preferred_element_type=F32)


def _dot01(x, u, dn=NN):
    hi = x.astype(BF16)
    lo = (x - hi.astype(F32)).astype(BF16)
    return (lax.dot_general(hi, u, dn, preferred_element_type=F32)
            + lax.dot_general(lo, u, dn, preferred_element_type=F32))


def _u01dot(u, x):
    hi = x.astype(BF16)
    lo = (x - hi.astype(F32)).astype(BF16)
    return (lax.dot_general(u, hi, NN, preferred_element_type=F32)
            + lax.dot_general(u, lo, NN, preferred_element_type=F32))


def _pick(n, cands):
    for c in cands:
        if c <= n and n % c == 0:
            return c
    return n


def _nbytes(shape, dtype):
    n = 1
    for s in shape:
        n *= s
    return n * jnp.dtype(dtype).itemsize


def _params(vmem_bytes):
    return pltpu.CompilerParams(vmem_limit_bytes=int(min(VMEM_CAP, max(vmem_bytes, 16 * 1024 * 1024))))


def _hbm(a):
    return pltpu.with_memory_space_constraint(a, pltpu.HBM)


def _softplus(z):
    return jnp.maximum(z, 0.0) + jnp.log(1.0 + jnp.exp(-jnp.abs(z)))


def _sigmoid(z):
    return 1.0 / (1.0 + jnp.exp(-z))


def _rms_fwd(xv, g):
    r = lax.rsqrt(jnp.mean(xv * xv, axis=-1, keepdims=True) + EPS)
    return xv * r * g


def _rms_bwd(xv, g, dy):
    r = lax.rsqrt(jnp.mean(xv * xv, axis=-1, keepdims=True) + EPS)
    xh = xv * r
    dxh = dy * g
    dx = r * (dxh - xh * jnp.mean(dxh * xh, axis=-1, keepdims=True))
    return dx, dy * xh


def _mm(a, b, *, name, ta=False, tb=False, add=None, out_dtype=F32, bm=1024, bn=1024, bk=512):
    m, k = (a.shape[1], a.shape[0]) if ta else a.shape
    n = b.shape[0] if tb else b.shape[1]
    tm = _pick(m, (bm, 512, 256, 128))
    tn = _pick(n, (bn, 512, 256, 128))
    tk = _pick(k, (bk, 256, 128))
    nk = k // tk
    dn = (((0 if ta else 1,), (1 if tb else 0,)), ((), ()))
    has_add = add is not None

    def body(*refs):
        if has_add:
            a_ref, b_ref, c_ref, o_ref, acc_ref = refs
        else:
            a_ref, b_ref, o_ref, acc_ref = refs
        kk = pl.program_id(2)

        @pl.when(kk == 0)
        def _():
            acc_ref[...] = jnp.zeros_like(acc_ref)

        acc_ref[...] += lax.dot_general(a_ref[...].astype(BF16), b_ref[...].astype(BF16), dn,
                                        preferred_element_type=F32)

        @pl.when(kk == nk - 1)
        def _():
            r = acc_ref[...]
            if has_add:
                r = r + c_ref[...].astype(F32)
            o_ref[...] = r.astype(out_dtype)

    a_spec = pl.BlockSpec((tk, tm), lambda i, j, q: (q, i)) if ta else pl.BlockSpec((tm, tk), lambda i, j, q: (i, q))
    b_spec = pl.BlockSpec((tn, tk), lambda i, j, q: (j, q)) if tb else pl.BlockSpec((tk, tn), lambda i, j, q: (q, j))
    o_spec = pl.BlockSpec((tm, tn), lambda i, j, q: (i, j))
    ins, specs = [a, b], [a_spec, b_spec]
    vm = 2 * (_nbytes((tm, tk), a.dtype) + _nbytes((tk, tn), b.dtype) + _nbytes((tm, tn), out_dtype)) \
        + 3 * _nbytes((tm, tn), F32) + _nbytes((tm, tk), BF16) + _nbytes((tk, tn), BF16)
    if has_add:
        ins.append(add)
        specs.append(o_spec)
        vm += 2 * _nbytes((tm, tn), add.dtype)
    return pl.pallas_call(
        body, name=name, grid=(m // tm, n // tn, nk), in_specs=specs, out_specs=o_spec,
        out_shape=pltpu.HBM((m, n), out_dtype), scratch_shapes=[pltpu.VMEM((tm, tn), F32)],
        compiler_params=_params(vm + (4 << 20)),
    )(*[_hbm(v) for v in ins])


def _rowwise(fn, rows, consts, outs, reds=(), *, name, tr=256, temps=6):
    rows = [r if isinstance(r, tuple) else (r, r.shape[1], 0) for r in rows]
    nrows = rows[0][0].shape[0]
    t = _pick(nrows, (tr, 128, 64, 32, 16, 8))
    nr, nc, no = len(rows), len(consts), len(outs)

    def body(*refs):
        rin, cin = refs[:nr], refs[nr:nr + nc]
        oref, rref = refs[nr + nc:nr + nc + no], refs[nr + nc + no:]
        res = fn(*[r[...] for r in rin], *[c[...] for c in cin])
        if not isinstance(res, (tuple, list)):
            res = (res,)
        for o, v in zip(oref, res[:no]):
            o[...] = v.astype(o.dtype)
        if rref:
            @pl.when(pl.program_id(0) == 0)
            def _():
                for r in rref:
                    r[...] = jnp.zeros_like(r)

            for r, v in zip(rref, res[no:]):
                r[...] += v

    in_specs = [pl.BlockSpec((t, w), functools.partial(lambda i, ci: (i, ci), ci=ci)) for (_, w, ci) in rows]
    in_specs += [pl.BlockSpec(c.shape, functools.partial(lambda i, nd: (0,) * nd, nd=c.ndim)) for c in consts]
    out_specs = [pl.BlockSpec((t, w), lambda i: (i, 0)) for (w, _) in outs]
    out_specs += [pl.BlockSpec((1, w), lambda i: (0, 0)) for w in reds]
    out_shape = [pltpu.HBM((nrows, w), dt) for (w, dt) in outs]
    out_shape += [jax.ShapeDtypeStruct((1, w), F32) for w in reds]
    widest = max([w for (_, w, _) in rows] + [w for (w, _) in outs])
    vm = 2 * sum(_nbytes((t, w), a.dtype) for (a, w, _) in rows) + 2 * sum(_nbytes((t, w), dt) for (w, dt) in outs)
    vm += temps * _nbytes((t, widest), F32) + (2 << 20)
    res = pl.pallas_call(
        body, name=name, grid=(nrows // t,), in_specs=in_specs, out_specs=out_specs, out_shape=out_shape,
        compiler_params=_params(vm),
    )(*[_hbm(a) for (a, _, _) in rows], *consts)
    return list(res)


def _sb_tiles(s, tq, tk):
    tq = _pick(s, (tq, 256, 128))
    tk = _pick(tq, (tk, 128))
    return tq, tk, tq // tk


def _sb_fwd(zm, heads, *, name, tq=512, tk=256):
    s = zm.shape[0]
    tq, tk, nd = _sb_tiles(s, tq, tk)
    scale = SB_HD ** -0.5

    def body(q_ref, k_ref, v_ref, o_ref, lt_ref):
        i = pl.program_id(1)
        qb = q_ref[...].astype(BF16)
        r = lax.broadcasted_iota(jnp.int32, (tq, tk), 0)
        c = lax.broadcasted_iota(jnp.int32, (tq, tk), 1)
        ur = lax.broadcasted_iota(jnp.int32, (tk, tk), 0)
        uc = lax.broadcasted_iota(jnp.int32, (tk, tk), 1)
        usuf = (ur > uc).astype(BF16)

        def tile(j, carry, causal):
            acc, cl = carry
            rows = pl.ds(pl.multiple_of(j * tk, tk), tk)
            kb = k_ref[rows, :].astype(BF16)
            vb = v_ref[rows, :].astype(BF16)
            z = lax.dot_general(qb, kb, NT, preferred_element_type=F32) * scale
            lsig = -_softplus(z)
            l = lsig if causal is None else jnp.where(causal, lsig, 0.0)
            loga = z + lsig + _dot01(l, usuf) + cl
            if causal is not None:
                loga = jnp.where(causal, loga, NEG)
            a = jnp.exp(loga)
            acc = acc + lax.dot_general(a.astype(BF16), vb, NN, preferred_element_type=F32)
            return acc, cl + jnp.sum(l, axis=1, keepdims=True)

        carry = (jnp.zeros((tq, SB_HD), F32), jnp.zeros((tq, 1), F32))
        for dd in range(nd - 1, -1, -1):
            carry = tile(i * nd + dd, carry, c + dd * tk < r)
        acc, cl = lax.fori_loop(0, i * nd, lambda n, cr: tile(i * nd - 1 - n, cr, None), carry)
        o_ref[...] = acc.astype(o_ref.dtype)
        lt_ref[...] = jnp.broadcast_to(cl, (tq, LANES))

    blk = lambda off: pl.BlockSpec((s, SB_HD), functools.partial(lambda h, i, off: (0, off + h), off=off))
    return pl.pallas_call(
        body, name=name, grid=(heads, s // tq),
        in_specs=[pl.BlockSpec((tq, SB_HD), lambda h, i: (i, h)), blk(heads), blk(2 * heads)],
        out_specs=[pl.BlockSpec((tq, SB_HD), lambda h, i: (i, h)), pl.BlockSpec((tq, LANES), lambda h, i: (i, h))],
        out_shape=[pltpu.HBM((s, heads * SB_HD), BF16), pltpu.HBM((s, heads * LANES), F32)],
        compiler_params=_params(8 * s * SB_HD * 4 + 24 * tq * tk * 4 + (8 << 20)),
    )(_hbm(zm), _hbm(zm), _hbm(zm))


def _sb_bwd(zm, dy, ltot, heads, *, name, tq=512, tk=256):
    s = zm.shape[0]
    tq, tk, nd = _sb_tiles(s, tq, tk)
    nq = s // tq
    scale = SB_HD ** -0.5

    def body(q_ref, k_ref, v_ref, do_ref, lt_ref, dq_ref, dk_ref, dv_ref, dka, dva):
        i = pl.program_id(1)

        @pl.when(i == 0)
        def _():
            dka[...] = jnp.zeros_like(dka)
            dva[...] = jnp.zeros_like(dva)

        qb = q_ref[...].astype(BF16)
        dob = do_ref[...].astype(BF16)
        ltot_c = lt_ref[:, 0:1]
        r = lax.broadcasted_iota(jnp.int32, (tq, tk), 0)
        c = lax.broadcasted_iota(jnp.int32, (tq, tk), 1)
        ur = lax.broadcasted_iota(jnp.int32, (tk, tk), 0)
        uc = lax.broadcasted_iota(jnp.int32, (tk, tk), 1)
        uincl = (ur <= uc).astype(BF16)
        uexcl = (ur < uc).astype(BF16)

        def tile(j, carry, causal):
            dq, cl, cg = carry
            rows = pl.ds(pl.multiple_of(j * tk, tk), tk)
            kb = k_ref[rows, :].astype(BF16)
            vb = v_ref[rows, :].astype(BF16)
            z = lax.dot_general(qb, kb, NT, preferred_element_type=F32) * scale
            lsig = -_softplus(z)
            l = lsig if causal is None else jnp.where(causal, lsig, 0.0)
            later = ltot_c - (cl + _dot01(l, uincl))
            loga = z + lsig + later
            if causal is not None:
                loga = jnp.where(causal, loga, NEG)
            a = jnp.exp(loga)
            sig = jnp.exp(z + lsig)
            g = a * lax.dot_general(dob, vb, NT, preferred_element_type=F32)
            p = cg + _dot01(g, uexcl)
            dz = g * (1.0 - sig) - p * sig
            if causal is not None:
                dz = jnp.where(causal, dz, 0.0)
            dzb = (dz * scale).astype(BF16)
            dva[rows, :] += lax.dot_general(a.astype(BF16), dob, TN, preferred_element_type=F32)
            dka[rows, :] += lax.dot_general(dzb, qb, TN, preferred_element_type=F32)
            dq = dq + lax.dot_general(dzb, kb, NN, preferred_element_type=F32)
            return dq, cl + jnp.sum(l, axis=1, keepdims=True), cg + jnp.sum(g, axis=1, keepdims=True)

        init = (jnp.zeros((tq, SB_HD), F32), jnp.zeros((tq, 1), F32), jnp.zeros((tq, 1), F32))
        carry = lax.fori_loop(0, i * nd, lambda j, cr: tile(j, cr, None), init)
        for dd in range(nd):
            carry = tile(i * nd + dd, carry, c + dd * tk < r)
        dq_ref[...] = carry[0].astype(dq_ref.dtype)

        @pl.when(i == nq - 1)
        def _():
            dk_ref[...] = dka[...].astype(dk_ref.dtype)
            dv_ref[...] = dva[...].astype(dv_ref.dtype)

    blk = lambda off: pl.BlockSpec((s, SB_HD), functools.partial(lambda h, i, off: (0, off + h), off=off))
    tile_spec = pl.BlockSpec((tq, SB_HD), lambda h, i: (i, h))
    full = pltpu.HBM((s, heads * SB_HD), BF16)
    return pl.pallas_call(
        body, name=name, grid=(heads, nq),
        in_specs=[tile_spec, blk(heads), blk(2 * heads), tile_spec, pl.BlockSpec((tq, LANES), lambda h, i: (i, h))],
        out_specs=[tile_spec, blk(0), blk(0)],
        out_shape=[full, full, full],
        scratch_shapes=[pltpu.VMEM((s, SB_HD), F32), pltpu.VMEM((s, SB_HD), F32)],
        compiler_params=_params(12 * s * SB_HD * 4 + 32 * tq * tk * 4 + (8 << 20)),
    )(_hbm(zm), _hbm(zm), _hbm(zm), _hbm(dy), _hbm(ltot))


def _conv_taps(u, w_ref, rows_i):
    taps = []
    for j in range(CONV_W):
        sh = CONV_W - 1 - j
        if sh == 0:
            taps.append(u)
        else:
            taps.append(jnp.where(rows_i >= sh, pltpu.roll(u, sh, 0), 0.0))
    return taps


def _conv_fwd(zm, col0, width, cw, cb, *, name):
    s = zm.shape[0]
    bw = _pick(width, (LANES,))
    off = col0 // bw

    def body(u_ref, w_ref, b_ref, o_ref):
        u = u_ref[...]
        rows_i = lax.broadcasted_iota(jnp.int32, u.shape, 0)
        acc = jnp.broadcast_to(b_ref[...], u.shape)
        for j, tp in enumerate(_conv_taps(u, w_ref, rows_i)):
            acc = acc + tp * w_ref[j:j + 1, :]
        o_ref[...] = acc * _sigmoid(acc)

    return pl.pallas_call(
        body, name=name, grid=(width // bw,),
        in_specs=[pl.BlockSpec((s, bw), lambda j: (0, off + j)), pl.BlockSpec((CONV_W, bw), lambda j: (0, j)),
                  pl.BlockSpec((1, bw), lambda j: (0, j))],
        out_specs=pl.BlockSpec((s, bw), lambda j: (0, j)),
        out_shape=pltpu.HBM((s, width), F32),
        compiler_params=_params(12 * s * bw * 4 + (4 << 20)),
    )(_hbm(zm), cw, cb)


def _conv_bwd(zm, col0, width, cw, cb, dqk, *, name):
    s = zm.shape[0]
    bw = _pick(width, (LANES,))
    off = col0 // bw

    def body(u_ref, w_ref, b_ref, d_ref, du_ref, dw_ref, db_ref):
        u = u_ref[...]
        rows_i = lax.broadcasted_iota(jnp.int32, u.shape, 0)
        taps = _conv_taps(u, w_ref, rows_i)
        acc = jnp.broadcast_to(b_ref[...], u.shape)
        for j, tp in enumerate(taps):
            acc = acc + tp * w_ref[j:j + 1, :]
        sg = _sigmoid(acc)
        dc = d_ref[...] * (sg * (1.0 + acc * (1.0 - sg)))
        du = jnp.zeros_like(u)
        for j in range(CONV_W):
            sh = CONV_W - 1 - j
            if sh == 0:
                du = du + dc * w_ref[j:j + 1, :]
            else:
                du = du + jnp.where(rows_i < s - sh, pltpu.roll(dc, s - sh, 0), 0.0) * w_ref[j:j + 1, :]
            dw_ref[j:j + 1, :] = jnp.sum(dc * taps[j], axis=0, keepdims=True)
        du_ref[...] = du.astype(du_ref.dtype)
        db_ref[...] = jnp.sum(dc, axis=0, keepdims=True)

    return pl.pallas_call(
        body, name=name, grid=(width // bw,),
        in_specs=[pl.BlockSpec((s, bw), lambda j: (0, off + j)), pl.BlockSpec((CONV_W, bw), lambda j: (0, j)),
                  pl.BlockSpec((1, bw), lambda j: (0, j)), pl.BlockSpec((s, bw), lambda j: (0, j))],
        out_specs=[pl.BlockSpec((s, bw), lambda j: (0, j)), pl.BlockSpec((CONV_W, bw), lambda j: (0, j)),
                   pl.BlockSpec((1, bw), lambda j: (0, j))],
        out_shape=[pltpu.HBM((s, width), BF16), pltpu.HBM((CONV_W, width), F32),
                   pltpu.HBM((1, width), F32)],
        compiler_params=_params(20 * s * bw * 4 + (4 << 20)),
    )(_hbm(zm), cw, cb, _hbm(dqk))


def _ml_gates(gcol_ref, grow_ref):
    l = CHUNK
    r = lax.broadcasted_iota(jnp.int32, (l, l), 0)
    c = lax.broadcasted_iota(jnp.int32, (l, l), 1)
    gcol = gcol_ref[...]
    grow = grow_ref[0]
    bcol = _u01dot((c <= r).astype(BF16), gcol)
    brow = _dot01(grow, (r <= c).astype(BF16))
    return gcol, grow, bcol, brow, r >= c


def _ml_chunk(h, dh, mq_ref, mk_ref, v_ref, gates, cp, n_prev, m_prev):
    gcol, grow, bcol, brow, tri = gates
    l = CHUNK
    sl = slice(h * dh, (h + 1) * dh)
    qc = mq_ref[:, sl]
    kc = mk_ref[:, sl] * (dh ** -0.5)
    vc = v_ref[:, sl]
    i_row = grow[h:h + 1, :]
    i_col = gcol[:, h:h + 1]
    b_col = bcol[:, ML_HEADS + h:ML_HEADS + h + 1]
    b_row = brow[ML_HEADS + h:ML_HEADS + h + 1, :]
    b_end = b_col[l - 1:l, :]
    d = jnp.where(tri, b_col - b_row + i_row, -jnp.inf)
    m_inter = b_col + m_prev
    m_t = jnp.maximum(m_inter, jnp.max(d, axis=1, keepdims=True))
    w = jnp.exp(d - m_t)
    s_inter = jnp.exp(m_inter - m_t)
    qb, kb, vb = qc.astype(BF16), kc.astype(BF16), vc.astype(BF16)
    cpb = cp.astype(BF16)
    a = lax.dot_general(qb, kb, NT, preferred_element_type=F32)
    sc = a * w
    qcp = lax.dot_general(qb, cpb, NT, preferred_element_type=F32)
    qn = jnp.sum(qc * n_prev, axis=1, keepdims=True)
    num = lax.dot_general(sc.astype(BF16), vb, NN, preferred_element_type=F32) + s_inter * qcp
    den = jnp.sum(sc, axis=1, keepdims=True) + s_inter * qn
    floor = jnp.exp(-m_t)
    dnm = jnp.maximum(jnp.abs(den), floor)
    g_col = b_end - b_col + i_col
    g_row = b_end - b_row + i_row
    m_new = jnp.maximum(b_end + m_prev, jnp.max(g_row, axis=1, keepdims=True))
    decay = jnp.exp(b_end + m_prev - m_new)
    wk = jnp.exp(g_col - m_new)
    return dict(qc=qc, kc=kc, vc=vc, qb=qb, kb=kb, vb=vb, cpb=cpb, w=w, s_inter=s_inter, a=a, sc=sc, qcp=qcp, qn=qn,
                num=num, den=den, floor=floor, dnm=dnm, m_new=m_new, decay=decay, wk=wk, sl=sl)


def _ml_fwd(mqk, zm, vcol, gcol, grow, d_model, *, name):
    s = zm.shape[0]
    nc = s // CHUNK
    dh = d_model // ML_HEADS
    hh = ML_HEADS

    def body(mq_ref, mk_ref, v_ref, gcol_ref, grow_ref, h_ref, cs_ref, ns_ref, ms_ref, c_s, n_s, m_s):
        @pl.when(pl.program_id(0) == 0)
        def _():
            c_s[...] = jnp.zeros_like(c_s)
            n_s[...] = jnp.zeros_like(n_s)
            m_s[...] = jnp.zeros_like(m_s)

        gates = _ml_gates(gcol_ref, grow_ref)
        for h in range(hh):
            cp, n_prev, m_prev = c_s[h], n_s[h], m_s[h][:, 0:1]
            cs_ref[0, h] = cp
            ns_ref[0, h] = n_prev
            ms_ref[0, h] = m_s[h]
            f = _ml_chunk(h, dh, mq_ref, mk_ref, v_ref, gates, cp, n_prev, m_prev)
            h_ref[:, f["sl"]] = f["num"] / f["dnm"]
            c_s[h] = f["decay"] * cp + lax.dot_general((f["vc"] * f["wk"]).astype(BF16), f["kb"], TN,
                                                       preferred_element_type=F32)
            n_s[h] = f["decay"] * n_prev + jnp.sum(f["wk"] * f["kc"], axis=0, keepdims=True)
            m_s[h] = jnp.broadcast_to(f["m_new"], (1, LANES))

    dblk = d_model
    return pl.pallas_call(
        body, name=name, grid=(nc,),
        in_specs=[pl.BlockSpec((CHUNK, dblk), lambda c: (c, 0)), pl.BlockSpec((CHUNK, dblk), lambda c: (c, 1)),
                  pl.BlockSpec((CHUNK, dblk), lambda c: (c, vcol // dblk)),
                  pl.BlockSpec((CHUNK, LANES), lambda c: (c, 0)), pl.BlockSpec((1, 8, CHUNK), lambda c: (c, 0, 0))],
        out_specs=[pl.BlockSpec((CHUNK, dblk), lambda c: (c, 0)),
                   pl.BlockSpec((1, hh, dh, dh), lambda c: (c, 0, 0, 0)),
                   pl.BlockSpec((1, hh, 1, dh), lambda c: (c, 0, 0, 0)),
                   pl.BlockSpec((1, hh, 1, LANES), lambda c: (c, 0, 0, 0))],
        out_shape=[pltpu.HBM((s, d_model), F32), pltpu.HBM((nc, hh, dh, dh), F32),
                   pltpu.HBM((nc, hh, 1, dh), F32), pltpu.HBM((nc, hh, 1, LANES), F32)],
        scratch_shapes=[pltpu.VMEM((hh, dh, dh), F32), pltpu.VMEM((hh, 1, dh), F32), pltpu.VMEM((hh, 1, LANES), F32)],
        compiler_params=_params(8 * hh * dh * dh * 4 + (16 << 20)),
    )(_hbm(mqk), _hbm(mqk), _hbm(zm), _hbm(gcol), _hbm(grow))


def _ml_bwd(mqk, zm, vcol, gcol, grow, cs, ns, ms, dhm, d_model, *, name):
    s = zm.shape[0]
    nc = s // CHUNK
    dh = d_model // ML_HEADS
    hh = ML_HEADS
    l = CHUNK

    def body(mq_ref, mk_ref, v_ref, gcol_ref, grow_ref, cs_ref, ns_ref, ms_ref, dh_ref,
             dq_ref, dk_ref, dv_ref, dgc_ref, dgr_ref, dc_s, dn_s):
        @pl.when(pl.program_id(0) == 0)
        def _():
            dc_s[...] = jnp.zeros_like(dc_s)
            dn_s[...] = jnp.zeros_like(dn_s)

        gates = _ml_gates(gcol_ref, grow_ref)
        lane = lax.broadcasted_iota(jnp.int32, (l, LANES), 1)
        rowi = lax.broadcasted_iota(jnp.int32, (8, l), 0)
        lastrow = lax.broadcasted_iota(jnp.int32, (l, 1), 0) == l - 1
        dgc = jnp.zeros((l, LANES), F32)
        dgr = jnp.zeros((8, l), F32)
        for h in range(hh):
            cp, n_prev, m_prev = cs_ref[0, h], ns_ref[0, h], ms_ref[0, h][:, 0:1]
            f = _ml_chunk(h, dh, mq_ref, mk_ref, v_ref, gates, cp, n_prev, m_prev)
            dC, dn = dc_s[h], dn_s[h]
            dhv = dh_ref[:, f["sl"]]
            dnum = dhv / f["dnm"]
            hv = f["num"] / f["dnm"]
            ddnm = -jnp.sum(dhv * hv, axis=1, keepdims=True) / f["dnm"]
            dden = jnp.where(jnp.abs(f["den"]) >= f["floor"], ddnm * jnp.sign(f["den"]), 0.0)
            dnb = dnum.astype(BF16)
            dsc = lax.dot_general(dnb, f["vb"], NT, preferred_element_type=F32) + dden
            dvc = lax.dot_general(f["sc"].astype(BF16), dnb, TN, preferred_element_type=F32)
            ds_inter = jnp.sum(dnum * f["qcp"], axis=1, keepdims=True) + dden * f["qn"]
            sdn = (f["s_inter"] * dnum).astype(BF16)
            sdd = f["s_inter"] * dden
            da = dsc * f["w"]
            dab = da.astype(BF16)
            dqc = (lax.dot_general(dab, f["kb"], NN, preferred_element_type=F32)
                   + lax.dot_general(sdn, f["cpb"], NN, preferred_element_type=F32) + sdd * n_prev)
            dcp = f["decay"] * dC + lax.dot_general(sdn, f["qb"], TN, preferred_element_type=F32)
            dnp = f["decay"] * dn + jnp.sum(sdd * f["qc"], axis=0, keepdims=True)
            vw = (f["vc"] * f["wk"]).astype(BF16)
            dCb = dC.astype(BF16)
            dkc = (lax.dot_general(dab, f["qb"], TN, preferred_element_type=F32)
                   + lax.dot_general(vw, dCb, NN, preferred_element_type=F32) + f["wk"] * dn)
            e = lax.dot_general(f["kb"], dCb, NT, preferred_element_type=F32)
            dvc = dvc + e * f["wk"]
            dwk = jnp.sum(e * f["vc"], axis=1, keepdims=True) + jnp.sum(f["kc"] * dn, axis=1, keepdims=True)
            ddecay = jnp.sum(jnp.sum(dC * cp, axis=1, keepdims=True), axis=0, keepdims=True) \
                + jnp.sum(dn * n_prev, axis=1, keepdims=True)
            dd = dsc * f["sc"]
            dlw = dwk * f["wk"]
            db_end = jnp.sum(dlw, axis=0, keepdims=True) + ddecay * f["decay"]
            di_col = dlw
            db_col = jnp.sum(dd, axis=1, keepdims=True) + ds_inter * f["s_inter"] - dlw \
                + jnp.where(lastrow, db_end, 0.0)
            cs_dd = jnp.sum(dd, axis=0, keepdims=True)
            dgc = dgc + jnp.where(lane == h, di_col, 0.0) + jnp.where(lane == hh + h, db_col, 0.0)
            dgr = dgr + jnp.where(rowi == h, cs_dd, 0.0) - jnp.where(rowi == hh + h, cs_dd, 0.0)
            dq_ref[:, f["sl"]] = dqc
            dk_ref[:, f["sl"]] = dkc * (dh ** -0.5)
            dv_ref[:, f["sl"]] = dvc.astype(dv_ref.dtype)
            dc_s[h] = dcp
            dn_s[h] = dnp
        dgc_ref[...] = dgc
        dgr_ref[0] = dgr

    dblk = d_model
    rev = lambda c: nc - 1 - c
    return pl.pallas_call(
        body, name=name, grid=(nc,),
        in_specs=[pl.BlockSpec((l, dblk), lambda c: (rev(c), 0)), pl.BlockSpec((l, dblk), lambda c: (rev(c), 1)),
                  pl.BlockSpec((l, dblk), lambda c: (rev(c), vcol // dblk)),
                  pl.BlockSpec((l, LANES), lambda c: (rev(c), 0)), pl.BlockSpec((1, 8, l), lambda c: (rev(c), 0, 0)),
                  pl.BlockSpec((1, hh, dh, dh), lambda c: (rev(c), 0, 0, 0)),
                  pl.BlockSpec((1, hh, 1, dh), lambda c: (rev(c), 0, 0, 0)),
                  pl.BlockSpec((1, hh, 1, LANES), lambda c: (rev(c), 0, 0, 0)),
                  pl.BlockSpec((l, dblk), lambda c: (rev(c), 0))],
        out_specs=[pl.BlockSpec((l, dblk), lambda c: (rev(c), 0)), pl.BlockSpec((l, dblk), lambda c: (rev(c), 0)),
                   pl.BlockSpec((l, dblk), lambda c: (rev(c), 0)), pl.BlockSpec((l, LANES), lambda c: (rev(c), 0)),
                   pl.BlockSpec((1, 8, l), lambda c: (rev(c), 0, 0))],
        out_shape=[pltpu.HBM((s, d_model), F32), pltpu.HBM((s, d_model), F32),
                   pltpu.HBM((s, d_model), BF16), pltpu.HBM((s, LANES), F32),
                   pltpu.HBM((nc, 8, l), F32)],
        scratch_shapes=[pltpu.VMEM((hh, dh, dh), F32), pltpu.VMEM((hh, 1, dh), F32)],
        compiler_params=_params(10 * hh * dh * dh * 4 + (16 << 20)),
    )(*[_hbm(a) for a in (mqk, mqk, zm, gcol, grow, cs, ns, ms, dhm)])


def _xa_fwd(zm, qcol, kv, gq, gk, d_model, *, name, tq=256):
    s = zm.shape[0]
    nm = kv.shape[0]
    dh = d_model // X_HEADS
    tq = _pick(s, (tq, 128, 64))
    scale = dh ** -0.5

    def body(q_ref, k_ref, v_ref, gq_ref, gk_ref, o_ref):
        qn = _rms_fwd(q_ref[...], gq_ref[...])
        kn = _rms_fwd(k_ref[...], gk_ref[...])
        lg = _dot(qn, kn, NT) * scale
        lg = lg - jnp.max(lg, axis=1, keepdims=True)
        p = jnp.exp(lg)
        p = p / jnp.sum(p, axis=1, keepdims=True)
        o_ref[...] = _dot(p, v_ref[...], NN).astype(o_ref.dtype)

    return pl.pallas_call(
        body, name=name, grid=(X_HEADS, s // tq),
        in_specs=[pl.BlockSpec((tq, dh), lambda h, i: (i, qcol // dh + h)), pl.BlockSpec((nm, dh), lambda h, i: (0, h)),
                  pl.BlockSpec((nm, dh), lambda h, i: (0, X_HEADS + h)),
                  pl.BlockSpec((1, dh), lambda h, i: (0, 0)), pl.BlockSpec((1, dh), lambda h, i: (0, 0))],
        out_specs=pl.BlockSpec((tq, dh), lambda h, i: (i, h)),
        out_shape=pltpu.HBM((s, d_model), BF16),
        compiler_params=_params(32 << 20),
    )(_hbm(zm), _hbm(kv), _hbm(kv), gq, gk)


def _xa_bwd(zm, qcol, kv, gq, gk, dy, d_model, *, name, tq=256):
    s = zm.shape[0]
    nm = kv.shape[0]
    dh = d_model // X_HEADS
    tq = _pick(s, (tq, 128, 64))
    nq = s // tq
    scale = dh ** -0.5

    def body(q_ref, k_ref, v_ref, gq_ref, gk_ref, do_ref, dq_ref, dkn_ref, dv_ref, dgq_ref):
        h, i = pl.program_id(0), pl.program_id(1)

        @pl.when(i == 0)
        def _():
            dkn_ref[...] = jnp.zeros_like(dkn_ref)
            dv_ref[...] = jnp.zeros_like(dv_ref)

        @pl.when((i == 0) & (h == 0))
        def _():
            dgq_ref[...] = jnp.zeros_like(dgq_ref)

        q = q_ref[...]
        qn = _rms_fwd(q, gq_ref[...])
        kn = _rms_fwd(k_ref[...], gk_ref[...])
        lg = _dot(qn, kn, NT) * scale
        lg = lg - jnp.max(lg, axis=1, keepdims=True)
        p = jnp.exp(lg)
        p = p / jnp.sum(p, axis=1, keepdims=True)
        do = do_ref[...]
        dv_ref[...] += _dot(p, do, TN)
        dp = _dot(do, v_ref[...], NT)
        dlg = p * (dp - jnp.sum(dp * p, axis=1, keepdims=True)) * scale
        dqn = _dot(dlg, kn, NN)
        dkn_ref[...] += _dot(dlg, qn, TN)
        dq, dgq = _rms_bwd(q, gq_ref[...], dqn)
        dq_ref[...] = dq.astype(dq_ref.dtype)
        dgq_ref[...] += jnp.sum(dgq, axis=0, keepdims=True)

    return pl.pallas_call(
        body, name=name, grid=(X_HEADS, nq),
        in_specs=[pl.BlockSpec((tq, dh), lambda h, i: (i, qcol // dh + h)), pl.BlockSpec((nm, dh), lambda h, i: (0, h)),
                  pl.BlockSpec((nm, dh), lambda h, i: (0, X_HEADS + h)),
                  pl.BlockSpec((1, dh), lambda h, i: (0, 0)), pl.BlockSpec((1, dh), lambda h, i: (0, 0)),
                  pl.BlockSpec((tq, dh), lambda h, i: (i, h))],
        out_specs=[pl.BlockSpec((tq, dh), lambda h, i: (i, h)), pl.BlockSpec((nm, dh), lambda h, i: (0, h)),
                   pl.BlockSpec((nm, dh), lambda h, i: (0, h)), pl.BlockSpec((1, dh), lambda h, i: (0, 0))],
        out_shape=[pltpu.HBM((s, d_model), BF16), pltpu.HBM((nm, d_model), F32),
                   pltpu.HBM((nm, d_model), F32), pltpu.HBM((1, dh), F32)],
        compiler_params=_params(32 << 20),
    )(_hbm(zm), _hbm(kv), _hbm(kv), gq, gk, _hbm(dy))


def _place():
    return lax.axis_index("x"), lax.axis_index("y"), lax.axis_index("c")


ANY = pl.BlockSpec(memory_space=pl.ANY)


def _allgather_quarters(shards, *, name):
    n = len(shards)

    def body(*refs):
        ins, outs = refs[:n], refs[n:2 * n]
        send, recv, loc = refs[2 * n:]
        x, y, c = _place()
        chips = [(1 - x, y), (x, 1 - y), (1 - x, 1 - y)]
        local = []
        for t in range(n):
            cp = pltpu.make_async_copy(ins[t], outs[t].at[2 * x + y], loc.at[t])
            cp.start()
            local.append(cp)

        def copy(t, j, slot):
            return pltpu.make_async_remote_copy(
                src_ref=ins[t], dst_ref=outs[t].at[slot], send_sem=send.at[3 * t + j], recv_sem=recv.at[3 * t + j],
                device_id=(chips[j][0], chips[j][1], c), device_id_type=MESH)

        for t in range(n):
            for j in range(3):
                copy(t, j, 2 * x + y).start()
        for t in range(n):
            for j in range(3):
                copy(t, j, 2 * chips[j][0] + chips[j][1]).wait_recv()
        for t in range(n):
            for j in range(3):
                copy(t, j, 2 * x + y).wait_send()
        for cp in local:
            cp.wait()

    return pl.pallas_call(
        body, name=name, in_specs=[ANY] * n, out_specs=[ANY] * n,
        out_shape=[pltpu.HBM((4,) + a.shape, a.dtype) for a in shards],
        scratch_shapes=[pltpu.SemaphoreType.DMA((3 * n,)), pltpu.SemaphoreType.DMA((3 * n,)),
                        pltpu.SemaphoreType.DMA((n,))],
    )(*shards)


def _exchange_grads(parts, *, name):
    n = len(parts)

    def body(*refs):
        ins, outs = refs[:n], refs[n:2 * n]
        send, recv, loc = refs[2 * n:]
        x, y, c = _place()
        me = 4 * x + 2 * y + c
        peers = [(x ^ ((j >> 2) & 1), y ^ ((j >> 1) & 1), c ^ (j & 1)) for j in range(1, 8)]
        local = []
        for t in range(n):
            cp = pltpu.make_async_copy(ins[t].at[2 * x + y, c], outs[t].at[me], loc.at[t])
            cp.start()
            local.append(cp)

        def copy(t, j, slot):
            px, py, pc = peers[j]
            return pltpu.make_async_remote_copy(
                src_ref=ins[t].at[2 * px + py, pc], dst_ref=outs[t].at[slot], send_sem=send.at[7 * t + j],
                recv_sem=recv.at[7 * t + j], device_id=(px, py, pc), device_id_type=MESH)

        for t in range(n):
            for j in range(7):
                copy(t, j, me).start()
        for t in range(n):
            for j in range(7):
                px, py, pc = peers[j]
                copy(t, j, 4 * px + 2 * py + pc).wait_recv()
        for t in range(n):
            for j in range(7):
                copy(t, j, me).wait_send()
        for cp in local:
            cp.wait()

    return pl.pallas_call(
        body, name=name, in_specs=[ANY] * n, out_specs=[ANY] * n,
        out_shape=[pltpu.HBM((8,) + a.shape[2:], a.dtype) for a in parts],
        scratch_shapes=[pltpu.SemaphoreType.DMA((7 * n,)), pltpu.SemaphoreType.DMA((7 * n,)),
                        pltpu.SemaphoreType.DMA((n,))],
    )(*parts)


def _sum8(parts, *, name):
    _, r, c = parts.shape
    t = _pick(r, (128, 64, 32, 16, 8))

    def body(p_ref, o_ref):
        acc = p_ref[0].astype(F32)
        for k in range(1, 8):
            acc = acc + p_ref[k].astype(F32)
        o_ref[...] = acc

    return pl.pallas_call(
        body, name=name, grid=(r // t,), in_specs=[pl.BlockSpec((8, t, c), lambda i: (0, i, 0))],
        out_specs=pl.BlockSpec((t, c), lambda i: (i, 0)), out_shape=pltpu.HBM((r, c), F32),
        compiler_params=_params(2 * 8 * t * c * 2 + 6 * t * c * 4 + (4 << 20)),
    )(_hbm(parts))


def _swap_halves(halves, *, name, chunk_bytes=512 * 1024):
    n = len(halves)
    items = []
    for t, a in enumerate(halves):
        r = a.shape[0]
        k = 1
        while _nbytes(a.shape, a.dtype) // k > chunk_bytes and r % (2 * k) == 0 and (r // (2 * k)) % 8 == 0:
            k *= 2
        items += [(t, q * (r // k), r // k) for q in range(k)]
    m = len(items)

    def body(*refs):
        ins, outs = refs[:n], refs[n:2 * n]
        sbuf, rbuf = refs[2 * n:3 * n], refs[3 * n:4 * n]
        send, recv, loc_own, loc_in, loc_out = refs[4 * n:]
        x, y, c = _place()
        local, stage = [], []
        for t in range(n):
            cp = pltpu.make_async_copy(ins[t], outs[t].at[c], loc_own.at[t])
            cp.start()
            local.append(cp)
        for q, (t, r0, nr) in enumerate(items):
            cp = pltpu.make_async_copy(ins[t].at[pl.ds(r0, nr)], sbuf[t].at[pl.ds(r0, nr)], loc_in.at[q])
            cp.start()
            stage.append(cp)

        def copy(q):
            t, r0, nr = items[q]
            return pltpu.make_async_remote_copy(
                src_ref=sbuf[t].at[pl.ds(r0, nr)], dst_ref=rbuf[t].at[pl.ds(r0, nr)], send_sem=send.at[q],
                recv_sem=recv.at[q], device_id=(x, y, 1 - c), device_id_type=MESH)

        for q in range(m):
            stage[q].wait()
            copy(q).start()
        for q, (t, r0, nr) in enumerate(items):
            copy(q).wait_recv()
            cp = pltpu.make_async_copy(rbuf[t].at[pl.ds(r0, nr)], outs[t].at[1 - c, pl.ds(r0, nr)], loc_out.at[q])
            cp.start()
            local.append(cp)
        for q in range(m):
            copy(q).wait_send()
        for cp in local:
            cp.wait()

    stage_bytes = 2 * sum(_nbytes(a.shape, a.dtype) for a in halves)
    return pl.pallas_call(
        body, name=name, in_specs=[ANY] * n, out_specs=[ANY] * n,
        out_shape=[pltpu.HBM((2,) + a.shape, a.dtype) for a in halves],
        scratch_shapes=[pltpu.VMEM(a.shape, a.dtype) for a in halves] * 2
        + [pltpu.SemaphoreType.DMA((m,)), pltpu.SemaphoreType.DMA((m,)), pltpu.SemaphoreType.DMA((n,)),
           pltpu.SemaphoreType.DMA((m,)), pltpu.SemaphoreType.DMA((m,))],
        compiler_params=_params(stage_bytes + (4 << 20)),
    )(*halves)


def _allreduce_small(p, *, name):
    r = p.shape[0]

    def body(p_ref, o_ref, buf, send, recv):
        x, y, c = _place()
        me = 4 * x + 2 * y + c
        peers = [(x ^ ((j >> 2) & 1), y ^ ((j >> 1) & 1), c ^ (j & 1)) for j in range(1, 8)]

        def copy(j, slot):
            return pltpu.make_async_remote_copy(
                src_ref=p_ref, dst_ref=buf.at[slot], send_sem=send.at[j], recv_sem=recv.at[j],
                device_id=peers[j], device_id_type=MESH)

        for j in range(7):
            copy(j, me).start()
        buf[me] = p_ref[...]
        for j in range(7):
            px, py, pc = peers[j]
            copy(j, 4 * px + 2 * py + pc).wait_recv()
        for j in range(7):
            copy(j, me).wait_send()
        acc = buf[0]
        for k in range(1, 8):
            acc = acc + buf[k]
        o_ref[...] = acc

    vspec = pl.BlockSpec(memory_space=pltpu.VMEM)
    return pl.pallas_call(
        body, name=name, in_specs=[vspec], out_specs=vspec, out_shape=jax.ShapeDtypeStruct((r, LANES), F32),
        scratch_shapes=[pltpu.VMEM((8, r, LANES), F32), pltpu.SemaphoreType.DMA((7,)), pltpu.SemaphoreType.DMA((7,))],
    )(p)


def _adamw_fn(w, g, m, v):
    m = ADAM_B1 * m + (1.0 - ADAM_B1) * g
    v = ADAM_B2 * v + (1.0 - ADAM_B2) * (g * g)
    m_hat = m / (1.0 - ADAM_B1 ** ADAM_STEP)
    v_hat = v / (1.0 - ADAM_B2 ** ADAM_STEP)
    delta = -ADAM_LR * (m_hat / (jnp.sqrt(v_hat) + ADAM_EPS) + ADAM_WD * w)
    return delta, m, v


def _adamw(w, g, m, v, *, name):
    c = w.shape[1]
    return _rowwise(_adamw_fn, [w, g, m, v], [], [(c, F32)] * 3, name=name, tr=128)


def _pack(vecs, rows):
    flat = jnp.concatenate([a.reshape(-1).astype(F32) for a in vecs])
    return jnp.pad(flat, (0, rows * LANES - flat.shape[0])).reshape(rows, LANES)


def _unpack(p, like):
    flat, out, o = p.reshape(-1), [], 0
    for a in like:
        out.append(flat[o:o + a.size].reshape(a.shape))
        o += a.size
    return out


def kernel(x, mem, g_mix, w_in, b_if, b_gate, conv_w, conv_b, ml_norm_g, g_mem, w_mem_kv, q_norm_g, k_norm_g, w_sb_proj, w_ml_proj, w_x_proj, w_out, g_mlp, w_ff1, w_ff2, loss_target, m_g_mix, m_w_in, m_b_if, m_b_gate, m_conv_w, m_conv_b, m_ml_norm_g, m_g_mem, m_w_mem_kv, m_q_norm_g, m_k_norm_g, m_w_sb_proj, m_w_ml_proj, m_w_x_proj, m_w_out, m_g_mlp, m_w_ff1, m_w_ff2, v_g_mix, v_w_in, v_b_if, v_b_gate, v_conv_w, v_conv_b, v_ml_norm_g, v_g_mem, v_w_mem_kv, v_q_norm_g, v_k_norm_g, v_w_sb_proj, v_w_ml_proj, v_w_x_proj, v_w_out, v_g_mlp, v_w_ff1, v_w_ff2):
    _, s, d = x.shape
    nm = mem.shape[1]
    n_in = 4 * w_in.shape[2]
    dff = 4 * w_ff1.shape[2]
    sbh = d // SB_HD
    hh = ML_HEADS
    dh = d // hh
    nc = s // CHUNK
    assert n_in == 11 * d + 2 * hh and d % (2 * LANES) == 0 and s % LANES == 0
    x2, mem2, tgt = x[0], mem[0], loss_target[0]

    q_shards = [w_in[0], w_mem_kv[0], w_sb_proj[0], w_ml_proj[0], w_x_proj[0], w_out[0], w_ff1[0], w_ff2[0]]
    gath = _allgather_quarters([a.astype(BF16) for a in q_shards] + [conv_w[0]], name="gather_weights")
    cols = lambda a: a.transpose(1, 0, 2).reshape(a.shape[1], 4 * a.shape[2])
    rws = lambda a: a.reshape(4 * a.shape[1], a.shape[2])
    w_in_f = cols(gath[0])
    w_main = jnp.concatenate([w_in_f[:, :7 * d], w_in_f[:, 7 * d + 2 * hh:]], axis=1)
    w_if = jnp.pad(w_in_f[:, 7 * d:7 * d + 2 * hh], ((0, 0), (0, LANES - 2 * hh)))
    w_kv, w_sbp, w_mlp, w_xp, w_o = cols(gath[1]), rws(gath[2]), rws(gath[3]), rws(gath[4]), rws(gath[5])
    w_f1, w_f2, conv_wf = cols(gath[6]), rws(gath[7]), cols(gath[8])
    b_if_p = jnp.pad(b_if, ((0, 0), (0, LANES - 2 * hh)))

    (hn,) = _rowwise(_rms_fwd, [x2], [g_mix], [(d, BF16)], name="norm_in")
    zm = _mm(hn, w_main, name="proj_in")
    zif = _mm(hn, w_if, name="proj_if")
    y_sb, ltot = _sb_fwd(zm, sbh, name="sb_fwd")

    def gate_fn(z, b):
        pre = z + b
        lane = lax.broadcasted_iota(jnp.int32, pre.shape, 1)
        return jnp.where(lane < hh, pre, -_softplus(-pre))

    (gcol,) = _rowwise(gate_fn, [zif], [b_if_p], [(LANES, F32)], name="ml_gates")
    grow = gcol[:, :8].T.reshape(8, nc, CHUNK).transpose(1, 0, 2)
    mqk = _conv_fwd(zm, 3 * d, 2 * d, conv_wf, conv_b, name="conv_fwd")
    hm, cst, nst, mst = _ml_fwd(mqk, zm, 5 * d, gcol, grow, d, name="ml_fwd")

    def mlout_fn(hv, o, g):
        ys = [_rms_fwd(hv[:, k * dh:(k + 1) * dh], g[:, k * dh:(k + 1) * dh]) for k in range(hh)]
        return jnp.concatenate(ys, axis=1) * _sigmoid(o)

    (y_ml,) = _rowwise(mlout_fn, [hm, (zm, d, 6)], [ml_norm_g], [(d, BF16)], name="ml_out")
    (memn,) = _rowwise(_rms_fwd, [mem2], [g_mem], [(d, BF16)], name="norm_mem")
    kv = _mm(memn, w_kv, name="proj_kv")
    y_x = _xa_fwd(zm, 7 * d, kv, q_norm_g, k_norm_g, d, name="xa_fwd")
    p_sb = _mm(y_sb, w_sbp, name="proj_sb")
    p_ml = _mm(y_ml, w_mlp, name="proj_ml")
    p_x = _mm(y_x, w_xp, name="proj_x")

    def merge_fn(a, b, c, g0, g1, g2, bg):
        return (_sigmoid(g0 + bg[:, :d]) * a + _sigmoid(g1 + bg[:, d:2 * d]) * b + _sigmoid(g2 + bg[:, 2 * d:]) * c)

    gate_cols = [(zm, d, 8), (zm, d, 9), (zm, d, 10)]
    (mixed,) = _rowwise(merge_fn, [p_sb, p_ml, p_x] + gate_cols, [b_gate], [(d, BF16)], name="merge")
    x1 = _mm(mixed, w_o, add=x2, name="proj_out")
    (h2,) = _rowwise(_rms_fwd, [x1], [g_mlp], [(d, BF16)], name="norm_mlp")
    u = _mm(h2, w_f1, name="ff1")
    (act,) = _rowwise(lambda uv: jnp.square(jnp.maximum(uv, 0.0)), [u], [], [(dff, BF16)], name="relu2", tr=128)
    yo = _mm(act, w_f2, add=x1, name="ff2")

    def loss_fn(yv, tv):
        e = yv - tv
        return e * (1.0 / d), jnp.sum(e * e, axis=0, keepdims=True) * (0.5 / d)

    dy, loss_cols = _rowwise(loss_fn, [yo, tgt], [], [(d, F32)], [d], name="loss")

    dact = _mm(dy, w_f2, tb=True, name="ff2_dx")
    dw_f2 = _mm(act, dy, ta=True, name="ff2_dw")
    (du,) = _rowwise(lambda g, uv: g * 2.0 * jnp.maximum(uv, 0.0), [dact, u], [], [(dff, BF16)], name="relu2_bwd",
                     tr=128)
    dw_f1 = _mm(h2, du, ta=True, name="ff1_dw")
    dh2 = _mm(du, w_f1, tb=True, name="ff1_dx")

    def norm_bwd_fn(xv, dyv, res, g):
        dx, dg = _rms_bwd(xv, g, dyv)
        return dx + res, jnp.sum(dg, axis=0, keepdims=True)

    dx1, dg_mlp = _rowwise(norm_bwd_fn, [x1, dh2, dy], [g_mlp], [(d, F32)], [d], name="norm_mlp_bwd")
    dmixed = _mm(dx1, w_o, tb=True, name="proj_out_dx")
    dw_o = _mm(mixed, dx1, ta=True, name="proj_out_dw")

    def merge_bwd_fn(dm, a, b, c, g0, g1, g2, bg):
        outs, dgs = [], []
        for p, g, k in ((a, g0, 0), (b, g1, 1), (c, g2, 2)):
            sg = _sigmoid(g + bg[:, k * d:(k + 1) * d])
            outs.append(dm * sg)
            dgs.append(dm * p * sg * (1.0 - sg))
        dgate = jnp.concatenate(dgs, axis=1)
        return (*outs, dgate, jnp.sum(dgate, axis=0, keepdims=True))

    dp_sb, dp_ml, dp_x, dgate, db_gate = _rowwise(
        merge_bwd_fn, [dmixed, p_sb, p_ml, p_x] + gate_cols, [b_gate], [(d, BF16)] * 3 + [(3 * d, BF16)], [3 * d],
        name="merge_bwd", tr=128)
    dw_sbp = _mm(y_sb, dp_sb, ta=True, name="proj_sb_dw")
    dw_mlp = _mm(y_ml, dp_ml, ta=True, name="proj_ml_dw")
    dw_xp = _mm(y_x, dp_x, ta=True, name="proj_x_dw")
    dy_sb = _mm(dp_sb, w_sbp, tb=True, out_dtype=BF16, name="proj_sb_dx")
    dy_ml = _mm(dp_ml, w_mlp, tb=True, name="proj_ml_dx")
    dy_x = _mm(dp_x, w_xp, tb=True, out_dtype=BF16, name="proj_x_dx")

    dsq, dsk, dsv = _sb_bwd(zm, dy_sb, ltot, sbh, name="sb_bwd")

    def mlout_bwd_fn(dyv, hv, o, g):
        sg = _sigmoid(o)
        dn = dyv * sg
        dxs, dgs, ys = [], [], []
        for k in range(hh):
            sl = slice(k * dh, (k + 1) * dh)
            ys.append(_rms_fwd(hv[:, sl], g[:, sl]))
            dxk, dgk = _rms_bwd(hv[:, sl], g[:, sl], dn[:, sl])
            dxs.append(dxk)
            dgs.append(dgk)
        do = dyv * jnp.concatenate(ys, axis=1) * sg * (1.0 - sg)
        return jnp.concatenate(dxs, axis=1), do, jnp.sum(jnp.concatenate(dgs, axis=1), axis=0, keepdims=True)

    dhm, dmlo, dg_mln = _rowwise(mlout_bwd_fn, [dy_ml, hm, (zm, d, 6)], [ml_norm_g], [(d, F32), (d, BF16)], [d],
                                 name="ml_out_bwd")
    dmq, dmk, dmlv, dgc, dgr = _ml_bwd(mqk, zm, 5 * d, gcol, grow, cst, nst, mst, dhm, d, name="ml_bwd")
    dmqk = jnp.concatenate([dmq, dmk], axis=1)
    dmlqk, dconv_w, dconv_b = _conv_bwd(zm, 3 * d, 2 * d, conv_wf, conv_b, dmqk, name="conv_bwd")
    dgr_t = jnp.pad(dgr.transpose(1, 0, 2).reshape(8, s).T, ((0, 0), (0, LANES - 8)))

    def gate_bwd_fn(a, b, z, bias):
        tot = a + b
        r = lax.broadcasted_iota(jnp.int32, (CHUNK, CHUNK), 0)
        c = lax.broadcasted_iota(jnp.int32, (CHUNK, CHUNK), 1)
        dlf = _u01dot((c >= r).astype(BF16), tot)
        lane = lax.broadcasted_iota(jnp.int32, tot.shape, 1)
        dz = jnp.where(lane < hh, tot, jnp.where(lane < 2 * hh, dlf * _sigmoid(-(z + bias)), 0.0))
        return dz, jnp.sum(dz, axis=0, keepdims=True)

    dzif, db_if_p = _rowwise(gate_bwd_fn, [dgc, dgr_t, zif], [b_if_p], [(LANES, BF16)], [LANES], name="ml_gates_bwd",
                             tr=CHUNK)
    dxq, dkn, dxv, dg_qn = _xa_bwd(zm, 7 * d, kv, q_norm_g, k_norm_g, dy_x, d, name="xa_bwd")

    def knorm_bwd_fn(kvv, dknv, dvv, g):
        dks, dgs = [], []
        for k in range(X_HEADS):
            sl = slice(k * dh, (k + 1) * dh)
            dk, dg = _rms_bwd(kvv[:, sl], g, dknv[:, sl])
            dks.append(dk)
            dgs.append(jnp.sum(dg, axis=0, keepdims=True))
        return jnp.concatenate(dks + [dvv], axis=1), dgs[0] + dgs[1] + dgs[2] + dgs[3]

    dkv, dg_kn = _rowwise(knorm_bwd_fn, [(kv, d, 0), dkn, dxv], [k_norm_g], [(2 * d, BF16)], [dh], name="xa_knorm_bwd")
    dw_kv = _mm(memn, dkv, ta=True, name="proj_kv_dw")
    dmemn = _mm(dkv, w_kv, tb=True, name="proj_kv_dx")

    def gmem_fn(mv, dv_, g):
        _, dg = _rms_bwd(mv, g, dv_)
        return (jnp.sum(dg, axis=0, keepdims=True),)

    (dg_mem,) = _rowwise(gmem_fn, [mem2, dmemn], [g_mem], [], [d], name="norm_mem_bwd")

    dzm = jnp.concatenate([dsq, dsk, dsv, dmlqk, dmlv, dmlo, dxq, dgate], axis=1)
    dw_main = _mm(hn, dzm, ta=True, name="proj_in_dw")
    dw_if = _mm(hn, dzif, ta=True, name="proj_if_dw")
    dhn = _mm(dzm, w_main, tb=True, name="proj_in_dx")
    dhn = _mm(dzif, w_if, tb=True, add=dhn, name="proj_if_dx")
    dx, dg_mix = _rowwise(norm_bwd_fn, [x2, dhn, dx1], [g_mix], [(d, F32)], [d], name="norm_in_bwd")

    dw_in = jnp.concatenate([dw_main[:, :7 * d], dw_if[:, :2 * hh], dw_main[:, 7 * d:]], axis=1)
    uncols = lambda a: a.reshape(a.shape[0], 4, a.shape[1] // 4).transpose(1, 0, 2)
    unrws = lambda a: a.reshape(4, a.shape[0] // 4, a.shape[1])
    quarters = [uncols(dw_in), uncols(dw_kv), unrws(dw_sbp), unrws(dw_mlp), unrws(dw_xp), unrws(dw_o), uncols(dw_f1),
                unrws(dw_f2)]
    parts = [q.astype(BF16).reshape(4, 2, q.shape[1] // 2, q.shape[2]) for q in quarters]
    recv = _exchange_grads(parts, name="exchange_grads")
    halves = [_sum8(r, name=f"sum_grads_{i}") for i, r in enumerate(recv)]
    both = _swap_halves(halves, name="swap_halves")
    g_big = [b.reshape(2 * b.shape[1], b.shape[2]) for b in both]

    small_g = [dg_mix, db_if_p[:, :2 * hh], db_gate, dconv_w, dconv_b, dg_mln, dg_mem, dg_qn, dg_kn, dg_mlp,
               jnp.sum(loss_cols).reshape(1, 1)]
    n_small = sum(a.size for a in small_g)
    rows = -(-n_small // (8 * LANES)) * 8
    g_small = _unpack(_allreduce_small(_pack(small_g, rows), name="allreduce_small"), small_g)
    loss = g_small[-1].reshape(())
    k4 = 2 * lax.axis_index("x") + lax.axis_index("y")
    qw = conv_w.shape[2]
    g_conv_w = lax.dynamic_slice_in_dim(g_small[3], k4 * qw, qw, axis=1)
    g_small_w = [g_small[0], g_small[1], g_small[2], g_conv_w] + g_small[4:10]
    sm_w = [g_mix, b_if, b_gate, conv_w[0], conv_b, ml_norm_g, g_mem, q_norm_g, k_norm_g, g_mlp]
    sm_m = [m_g_mix, m_b_if, m_b_gate, m_conv_w[0], m_conv_b, m_ml_norm_g, m_g_mem, m_q_norm_g, m_k_norm_g, m_g_mlp]
    sm_v = [v_g_mix, v_b_if, v_b_gate, v_conv_w[0], v_conv_b, v_ml_norm_g, v_g_mem, v_q_norm_g, v_k_norm_g, v_g_mlp]
    n_sw = sum(a.size for a in sm_w)
    rows_w = -(-n_sw // (8 * LANES)) * 8
    sm_out = _adamw(_pack(sm_w, rows_w), _pack(g_small_w, rows_w), _pack(sm_m, rows_w), _pack(sm_v, rows_w),
                    name="adamw_small")
    sm_delta, sm_newm, sm_newv = [_unpack(p, sm_w) for p in sm_out]

    big_w = [w_in[0], w_mem_kv[0], w_sb_proj[0], w_ml_proj[0], w_x_proj[0], w_out[0], w_ff1[0], w_ff2[0]]
    big_m = [m_w_in[0], m_w_mem_kv[0], m_w_sb_proj[0], m_w_ml_proj[0], m_w_x_proj[0], m_w_out[0], m_w_ff1[0],
             m_w_ff2[0]]
    big_v = [v_w_in[0], v_w_mem_kv[0], v_w_sb_proj[0], v_w_ml_proj[0], v_w_x_proj[0], v_w_out[0], v_w_ff1[0],
             v_w_ff2[0]]
    big_out = [_adamw(w, g, m, v, name=f"adamw_{i}") for i, (w, g, m, v) in enumerate(zip(big_w, g_big, big_m, big_v))]

    order = ["g_mix", "w_in", "b_if", "b_gate", "conv_w", "conv_b", "ml_norm_g", "g_mem", "w_mem_kv", "q_norm_g",
             "k_norm_g", "w_sb_proj", "w_ml_proj", "w_x_proj", "w_out", "g_mlp", "w_ff1", "w_ff2"]
    small_names = ["g_mix", "b_if", "b_gate", "conv_w", "conv_b", "ml_norm_g", "g_mem", "q_norm_g", "k_norm_g", "g_mlp"]
    big_names = ["w_in", "w_mem_kv", "w_sb_proj", "w_ml_proj", "w_x_proj", "w_out", "w_ff1", "w_ff2"]
    grads, deltas, new_m, new_v = {}, {}, {}, {}
    for i, nme in enumerate(small_names):
        shp = sm_w[i].shape if nme != "conv_w" else conv_w.shape
        grads[nme] = g_small_w[i].reshape(shp)
        deltas[nme], new_m[nme], new_v[nme] = (sm_delta[i].reshape(shp), sm_newm[i].reshape(shp),
                                               sm_newv[i].reshape(shp))
    for i, nme in enumerate(big_names):
        grads[nme] = g_big[i][None]
        deltas[nme], new_m[nme], new_v[nme] = (o[None] for o in big_out[i])
    return (loss, dx[None], *[grads[k] for k in order], *[deltas[k] for k in order], *[new_m[k] for k in order],
            *[new_v[k] for k in order])
```

```python
import functools

import jax
import jax.numpy as jnp
from jax import lax
from jax.experimental import pallas as pl
from jax.experimental.pallas import tpu as pltpu

F32 = jnp.float32
BF16 = jnp.bfloat16
MESH = pl.DeviceIdType.MESH

EPS = 1e-6
SB_HD = 128
ML_HEADS = 4
X_HEADS = 4
CHUNK = 64
CONV_W = 4
LANES = 128
ADAM_LR = 0.001
ADAM_B1 = 0.9
ADAM_B2 = 0.999
ADAM_EPS = 1e-08
ADAM_WD = 0.01
ADAM_STEP = 10
VMEM_CAP = 56 * 1024 * 1024
NEG = -1e30

NT = (((1,), (1,)), ((), ()))
NN = (((1,), (0,)), ((), ()))
TN = (((0,), (0,)), ((), ()))


def _dot(a, b, dn=NN):
    return lax.dot_general(a.astype(BF16), b.astype(BF16), dn, preferred_element_type=F32)


def _dot01(x, u, dn=NN):
    hi = x.astype(BF16)
    lo = (x - hi.astype(F32)).astype(BF16)
    return (lax.dot_general(hi, u, dn, preferred_element_type=F32)
            + lax.dot_general(lo, u, dn, preferred_element_type=F32))


def _u01dot(u, x):
    hi = x.astype(BF16)
    lo = (x - hi.astype(F32)).astype(BF16)
    return (lax.dot_general(u, hi, NN, preferred_element_type=F32)
            + lax.dot_general(u, lo, NN, preferred_element_type=F32))


def _pick(n, cands):
    for c in cands:
        if c <= n and n % c == 0:
            return c
    return n


def _nbytes(shape, dtype):
    n = 1
    for s in shape:
        n *= s
    return n * jnp.dtype(dtype).itemsize


def _params(vmem_bytes):
    return pltpu.CompilerParams(vmem_limit_bytes=int(min(VMEM_CAP, max(vmem_bytes, 16 * 1024 * 1024))))


def _hbm(a):
    return pltpu.with_memory_space_constraint(a, pltpu.HBM)


def _softplus(z):
    return jnp.maximum(z, 0.0) + jnp.log(1.0 + jnp.exp(-jnp.abs(z)))


def _sigmoid(z):
    return 1.0 / (1.0 + jnp.exp(-z))


def _rms_fwd(xv, g):
    r = lax.rsqrt(jnp.mean(xv * xv, axis=-1, keepdims=True) + EPS)
    return xv * r * g


def _rms_bwd(xv, g, dy):
    r = lax.rsqrt(jnp.mean(xv * xv, axis=-1, keepdims=True) + EPS)
    xh = xv * r
    dxh = dy * g
    dx = r * (dxh - xh * jnp.mean(dxh * xh, axis=-1, keepdims=True))
    return dx, dy * xh


def _mm(a, b, *, name, ta=False, tb=False, add=None, out_dtype=F32, bm=1024, bn=1024, bk=1024, after=None):
    m, k = (a.shape[1], a.shape[0]) if ta else a.shape
    n = b.shape[0] if tb else b.shape[1]
    tm = _pick(m, (bm, 512, 256, 128))
    tn = _pick(n, (bn, 512, 256, 128))
    tk = _pick(k, (bk, 512, 256, 128))
    nk = k // tk
    dn = (((0 if ta else 1,), (1 if tb else 0,)), ((), ()))
    has_add = add is not None

    def body(*refs):
        a_ref, b_ref = refs[:2]
        c_ref = refs[2] if has_add else None
        o_ref = refs[2 + has_add + (after is not None)]
        part = lax.dot_general(a_ref[...].astype(BF16), b_ref[...].astype(BF16), dn, preferred_element_type=F32)

        def finish(r):
            if has_add:
                r = r + c_ref[...].astype(F32)
            o_ref[...] = r.astype(out_dtype)

        if nk == 1:
            finish(part)
        else:
            acc_ref = refs[-1]
            kk = pl.program_id(2)

            @pl.when(kk == 0)
            def _():
                acc_ref[...] = part

            @pl.when(kk > 0)
            def _():
                acc_ref[...] += part

            @pl.when(kk == nk - 1)
            def _():
                finish(acc_ref[...])

    a_spec = pl.BlockSpec((tk, tm), lambda i, j, q: (q, i)) if ta else pl.BlockSpec((tm, tk), lambda i, j, q: (i, q))
    b_spec = pl.BlockSpec((tn, tk), lambda i, j, q: (j, q)) if tb else pl.BlockSpec((tk, tn), lambda i, j, q: (q, j))
    o_spec = pl.BlockSpec((tm, tn), lambda i, j, q: (i, j))
    ins, specs = [_hbm(a), _hbm(b)], [a_spec, b_spec]
    vm = 2 * (_nbytes((tm, tk), a.dtype) + _nbytes((tk, tn), b.dtype) + _nbytes((tm, tn), out_dtype)) \
        + 3 * _nbytes((tm, tn), F32) + _nbytes((tm, tk), BF16) + _nbytes((tk, tn), BF16)
    if has_add:
        ins.append(_hbm(add))
        specs.append(o_spec)
        vm += 2 * _nbytes((tm, tn), add.dtype)
    if after is not None:
        ins.append(after)
        specs.append(ANY)
    return pl.pallas_call(
        body, name=name, grid=(m // tm, n // tn, nk), in_specs=specs, out_specs=o_spec,
        out_shape=pltpu.HBM((m, n), out_dtype), scratch_shapes=[pltpu.VMEM((tm, tn), F32)] if nk > 1 else [],
        compiler_params=_params(vm + (4 << 20)),
    )(*ins)


def _rowwise(fn, rows, consts, outs, reds=(), *, name, tr=256, temps=6):
    rows = [r if isinstance(r, tuple) else (r, r.shape[1], 0) for r in rows]
    nrows = rows[0][0].shape[0]
    t = _pick(nrows, (tr, 128, 64, 32, 16, 8))
    nr, nc, no = len(rows), len(consts), len(outs)

    def body(*refs):
        rin, cin = refs[:nr], refs[nr:nr + nc]
        oref, rref = refs[nr + nc:nr + nc + no], refs[nr + nc + no:]
        res = fn(*[r[...] for r in rin], *[c[...] for c in cin])
        if not isinstance(res, (tuple, list)):
            res = (res,)
        for o, v in zip(oref, res[:no]):
            o[...] = v.astype(o.dtype)
        if rref:
            @pl.when(pl.program_id(0) == 0)
            def _():
                for r in rref:
                    r[...] = jnp.zeros_like(r)

            for r, v in zip(rref, res[no:]):
                r[...] += v

    in_specs = [pl.BlockSpec((t, w), functools.partial(lambda i, ci: (i, ci), ci=ci)) for (_, w, ci) in rows]
    in_specs += [pl.BlockSpec(c.shape, functools.partial(lambda i, nd: (0,) * nd, nd=c.ndim)) for c in consts]
    out_specs = [pl.BlockSpec((t, w), lambda i: (i, 0)) for (w, _) in outs]
    out_specs += [pl.BlockSpec((1, w), lambda i: (0, 0)) for w in reds]
    out_shape = [pltpu.HBM((nrows, w), dt) for (w, dt) in outs]
    out_shape += [jax.ShapeDtypeStruct((1, w), F32) for w in reds]
    widest = max([w for (_, w, _) in rows] + [w for (w, _) in outs])
    vm = 2 * sum(_nbytes((t, w), a.dtype) for (a, w, _) in rows) + 2 * sum(_nbytes((t, w), dt) for (w, dt) in outs)
    vm += temps * _nbytes((t, widest), F32) + (2 << 20)
    res = pl.pallas_call(
        body, name=name, grid=(nrows // t,), in_specs=in_specs, out_specs=out_specs, out_shape=out_shape,
        compiler_params=_params(vm),
    )(*[_hbm(a) for (a, _, _) in rows], *consts)
    return list(res)


def _sb_tiles(s, tq, tk):
    tq = _pick(s, (tq, 256, 128))
    tk = _pick(tq, (tk, 128))
    return tq, tk, tq // tk


def _sb_fwd(zm, heads, *, name, tq=512, tk=256):
    s = zm.shape[0]
    tq, tk, nd = _sb_tiles(s, tq, tk)
    scale = SB_HD ** -0.5

    def body(q_ref, k_ref, v_ref, o_ref, lt_ref):
        i = pl.program_id(1)
        qb = q_ref[...].astype(BF16)
        r = lax.broadcasted_iota(jnp.int32, (tq, tk), 0)
        c = lax.broadcasted_iota(jnp.int32, (tq, tk), 1)
        ur = lax.broadcasted_iota(jnp.int32, (tk, tk), 0)
        uc = lax.broadcasted_iota(jnp.int32, (tk, tk), 1)
        usuf = (ur > uc).astype(BF16)

        def tile(j, carry, causal):
            acc, cl = carry
            rows = pl.ds(pl.multiple_of(j * tk, tk), tk)
            kb = k_ref[rows, :].astype(BF16)
            vb = v_ref[rows, :].astype(BF16)
            z = lax.dot_general(qb, kb, NT, preferred_element_type=F32) * scale
            lsig = -_softplus(z)
            l = lsig if causal is None else jnp.where(causal, lsig, 0.0)
            loga = z + lsig + _dot01(l, usuf) + cl
            if causal is not None:
                loga = jnp.where(causal, loga, NEG)
            a = jnp.exp(loga)
            acc = acc + lax.dot_general(a.astype(BF16), vb, NN, preferred_element_type=F32)
            return acc, cl + jnp.sum(l, axis=1, keepdims=True)

        carry = (jnp.zeros((tq, SB_HD), F32), jnp.zeros((tq, 1), F32))
        for dd in range(nd - 1, -1, -1):
            carry = tile(i * nd + dd, carry, c + dd * tk < r)
        acc, cl = lax.fori_loop(0, i * nd, lambda n, cr: tile(i * nd - 1 - n, cr, None), carry)
        o_ref[...] = acc.astype(o_ref.dtype)
        lt_ref[...] = jnp.broadcast_to(cl, (tq, LANES))

    blk = lambda off: pl.BlockSpec((s, SB_HD), functools.partial(lambda h, i, off: (0, off + h), off=off))
    return pl.pallas_call(
        body, name=name, grid=(heads, s // tq),
        in_specs=[pl.BlockSpec((tq, SB_HD), lambda h, i: (i, h)), blk(heads), blk(2 * heads)],
        out_specs=[pl.BlockSpec((tq, SB_HD), lambda h, i: (i, h)), pl.BlockSpec((tq, LANES), lambda h, i: (i, h))],
        out_shape=[pltpu.HBM((s, heads * SB_HD), BF16), pltpu.HBM((s, heads * LANES), F32)],
        compiler_params=_params(8 * s * SB_HD * 4 + 24 * tq * tk * 4 + (8 << 20)),
    )(_hbm(zm), _hbm(zm), _hbm(zm))


def _sb_bwd(zm, dy, ltot, after, heads, *, name, tq=512, tk=256):
    s = zm.shape[0]
    tq, tk, nd = _sb_tiles(s, tq, tk)
    nq = s // tq
    scale = SB_HD ** -0.5

    def body(q_ref, k_ref, v_ref, do_ref, lt_ref, after_ref, dq_ref, dk_ref, dv_ref, dka, dva):
        i = pl.program_id(1)

        @pl.when(i == 0)
        def _():
            dka[...] = jnp.zeros_like(dka)
            dva[...] = jnp.zeros_like(dva)

        qb = q_ref[...].astype(BF16)
        dob = do_ref[...].astype(BF16)
        ltot_c = lt_ref[:, 0:1]
        r = lax.broadcasted_iota(jnp.int32, (tq, tk), 0)
        c = lax.broadcasted_iota(jnp.int32, (tq, tk), 1)
        ur = lax.broadcasted_iota(jnp.int32, (tk, tk), 0)
        uc = lax.broadcasted_iota(jnp.int32, (tk, tk), 1)
        uincl = (ur <= uc).astype(BF16)
        uexcl = (ur < uc).astype(BF16)

        def tile(j, carry, causal):
            dq, cl, cg = carry
            rows = pl.ds(pl.multiple_of(j * tk, tk), tk)
            kb = k_ref[rows, :].astype(BF16)
            vb = v_ref[rows, :].astype(BF16)
            z = lax.dot_general(qb, kb, NT, preferred_element_type=F32) * scale
            lsig = -_softplus(z)
            l = lsig if causal is None else jnp.where(causal, lsig, 0.0)
            later = ltot_c - (cl + _dot01(l, uincl))
            loga = z + lsig + later
            if causal is not None:
                loga = jnp.where(causal, loga, NEG)
            a = jnp.exp(loga)
            sig = jnp.exp(z + lsig)
            g = a * lax.dot_general(dob, vb, NT, preferred_element_type=F32)
            p = cg + _dot01(g, uexcl)
            dz = g * (1.0 - sig) - p * sig
            if causal is not None:
                dz = jnp.where(causal, dz, 0.0)
            dzb = (dz * scale).astype(BF16)
            dva[rows, :] += lax.dot_general(a.astype(BF16), dob, TN, preferred_element_type=F32)
            dka[rows, :] += lax.dot_general(dzb, qb, TN, preferred_element_type=F32)
            dq = dq + lax.dot_general(dzb, kb, NN, preferred_element_type=F32)
            return dq, cl + jnp.sum(l, axis=1, keepdims=True), cg + jnp.sum(g, axis=1, keepdims=True)

        init = (jnp.zeros((tq, SB_HD), F32), jnp.zeros((tq, 1), F32), jnp.zeros((tq, 1), F32))
        carry = lax.fori_loop(0, i * nd, lambda j, cr: tile(j, cr, None), init)
        for dd in range(nd):
            carry = tile(i * nd + dd, carry, c + dd * tk < r)
        dq_ref[...] = carry[0].astype(dq_ref.dtype)

        @pl.when(i == nq - 1)
        def _():
            dk_ref[...] = dka[...].astype(dk_ref.dtype)
            dv_ref[...] = dva[...].astype(dv_ref.dtype)

    blk = lambda off: pl.BlockSpec((s, SB_HD), functools.partial(lambda h, i, off: (0, off + h), off=off))
    tile_spec = pl.BlockSpec((tq, SB_HD), lambda h, i: (i, h))
    full = pltpu.HBM((s, heads * SB_HD), BF16)
    return pl.pallas_call(
        body, name=name, grid=(heads, nq),
        in_specs=[tile_spec, blk(heads), blk(2 * heads), tile_spec, pl.BlockSpec((tq, LANES), lambda h, i: (i, h)),
                  ANY],
        out_specs=[tile_spec, blk(0), blk(0)],
        out_shape=[full, full, full],
        scratch_shapes=[pltpu.VMEM((s, SB_HD), F32), pltpu.VMEM((s, SB_HD), F32)],
        compiler_params=_params(12 * s * SB_HD * 4 + 32 * tq * tk * 4 + (8 << 20)),
    )(_hbm(zm), _hbm(zm), _hbm(zm), _hbm(dy), _hbm(ltot), after)


def _conv_taps(u, w_ref, rows_i):
    taps = []
    for j in range(CONV_W):
        sh = CONV_W - 1 - j
        if sh == 0:
            taps.append(u)
        else:
            taps.append(jnp.where(rows_i >= sh, pltpu.roll(u, sh, 0), 0.0))
    return taps


def _conv_fwd(zm, col0, width, cw, cb, *, name):
    s = zm.shape[0]
    bw = _pick(width, (LANES,))
    off = col0 // bw

    def body(u_ref, w_ref, b_ref, o_ref):
        u = u_ref[...]
        rows_i = lax.broadcasted_iota(jnp.int32, u.shape, 0)
        acc = jnp.broadcast_to(b_ref[...], u.shape)
        for j, tp in enumerate(_conv_taps(u, w_ref, rows_i)):
            acc = acc + tp * w_ref[j:j + 1, :]
        o_ref[...] = acc * _sigmoid(acc)

    return pl.pallas_call(
        body, name=name, grid=(width // bw,),
        in_specs=[pl.BlockSpec((s, bw), lambda j: (0, off + j)), pl.BlockSpec((CONV_W, bw), lambda j: (0, j)),
                  pl.BlockSpec((1, bw), lambda j: (0, j))],
        out_specs=pl.BlockSpec((s, bw), lambda j: (0, j)),
        out_shape=pltpu.HBM((s, width), F32),
        compiler_params=_params(12 * s * bw * 4 + (4 << 20)),
    )(_hbm(zm), cw, cb)


def _conv_bwd(zm, col0, width, cw, cb, dqk, *, name):
    s = zm.shape[0]
    bw = _pick(width, (LANES,))
    off = col0 // bw

    def body(u_ref, w_ref, b_ref, d_ref, du_ref, dw_ref, db_ref):
        u = u_ref[...]
        rows_i = lax.broadcasted_iota(jnp.int32, u.shape, 0)
        taps = _conv_taps(u, w_ref, rows_i)
        acc = jnp.broadcast_to(b_ref[...], u.shape)
        for j, tp in enumerate(taps):
            acc = acc + tp * w_ref[j:j + 1, :]
        sg = _sigmoid(acc)
        dc = d_ref[...] * (sg * (1.0 + acc * (1.0 - sg)))
        du = jnp.zeros_like(u)
        for j in range(CONV_W):
            sh = CONV_W - 1 - j
            if sh == 0:
                du = du + dc * w_ref[j:j + 1, :]
            else:
                du = du + jnp.where(rows_i < s - sh, pltpu.roll(dc, s - sh, 0), 0.0) * w_ref[j:j + 1, :]
            dw_ref[j:j + 1, :] = jnp.sum(dc * taps[j], axis=0, keepdims=True)
        du_ref[...] = du.astype(du_ref.dtype)
        db_ref[...] = jnp.sum(dc, axis=0, keepdims=True)

    return pl.pallas_call(
        body, name=name, grid=(width // bw,),
        in_specs=[pl.BlockSpec((s, bw), lambda j: (0, off + j)), pl.BlockSpec((CONV_W, bw), lambda j: (0, j)),
                  pl.BlockSpec((1, bw), lambda j: (0, j)), pl.BlockSpec((s, bw), lambda j: (0, j))],
        out_specs=[pl.BlockSpec((s, bw), lambda j: (0, j)), pl.BlockSpec((CONV_W, bw), lambda j: (0, j)),
                   pl.BlockSpec((1, bw), lambda j: (0, j))],
        out_shape=[pltpu.HBM((s, width), BF16), pltpu.HBM((CONV_W, width), F32),
                   pltpu.HBM((1, width), F32)],
        compiler_params=_params(20 * s * bw * 4 + (4 << 20)),
    )(_hbm(zm), cw, cb, _hbm(dqk))


def _ml_gates(gcol_ref, grow_ref):
    l = CHUNK
    r = lax.broadcasted_iota(jnp.int32, (l, l), 0)
    c = lax.broadcasted_iota(jnp.int32, (l, l), 1)
    gcol = gcol_ref[...]
    grow = grow_ref[0]
    bcol = _u01dot((c <= r).astype(BF16), gcol)
    brow = _dot01(grow, (r <= c).astype(BF16))
    return gcol, grow, bcol, brow, r >= c


def _ml_chunk(h, dh, mq_ref, mk_ref, v_ref, gates, cp, n_prev, m_prev):
    gcol, grow, bcol, brow, tri = gates
    l = CHUNK
    sl = slice(h * dh, (h + 1) * dh)
    qc = mq_ref[:, sl]
    kc = mk_ref[:, sl] * (dh ** -0.5)
    vc = v_ref[:, sl]
    i_row = grow[h:h + 1, :]
    i_col = gcol[:, h:h + 1]
    b_col = bcol[:, ML_HEADS + h:ML_HEADS + h + 1]
    b_row = brow[ML_HEADS + h:ML_HEADS + h + 1, :]
    b_end = b_col[l - 1:l, :]
    d = jnp.where(tri, b_col - b_row + i_row, -jnp.inf)
    m_inter = b_col + m_prev
    m_t = jnp.maximum(m_inter, jnp.max(d, axis=1, keepdims=True))
    w = jnp.exp(d - m_t)
    s_inter = jnp.exp(m_inter - m_t)
    qb, kb, vb = qc.astype(BF16), kc.astype(BF16), vc.astype(BF16)
    cpb = cp.astype(BF16)
    a = lax.dot_general(qb, kb, NT, preferred_element_type=F32)
    sc = a * w
    qcp = lax.dot_general(qb, cpb, NT, preferred_element_type=F32)
    qn = jnp.sum(qc * n_prev, axis=1, keepdims=True)
    num = lax.dot_general(sc.astype(BF16), vb, NN, preferred_element_type=F32) + s_inter * qcp
    den = jnp.sum(sc, axis=1, keepdims=True) + s_inter * qn
    floor = jnp.exp(-m_t)
    dnm = jnp.maximum(jnp.abs(den), floor)
    g_col = b_end - b_col + i_col
    g_row = b_end - b_row + i_row
    m_new = jnp.maximum(b_end + m_prev, jnp.max(g_row, axis=1, keepdims=True))
    decay = jnp.exp(b_end + m_prev - m_new)
    wk = jnp.exp(g_col - m_new)
    return dict(qc=qc, kc=kc, vc=vc, qb=qb, kb=kb, vb=vb, cpb=cpb, w=w, s_inter=s_inter, a=a, sc=sc, qcp=qcp, qn=qn,
                num=num, den=den, floor=floor, dnm=dnm, m_new=m_new, decay=decay, wk=wk, sl=sl)


def _ml_fwd(mqk, zm, vcol, gcol, grow, d_model, *, name):
    s = zm.shape[0]
    nc = s // CHUNK
    dh = d_model // ML_HEADS
    hh = ML_HEADS

    def body(mq_ref, mk_ref, v_ref, gcol_ref, grow_ref, h_ref, cs_ref, ns_ref, ms_ref, c_s, n_s, m_s):
        @pl.when(pl.program_id(0) == 0)
        def _():
            c_s[...] = jnp.zeros_like(c_s)
            n_s[...] = jnp.zeros_like(n_s)
            m_s[...] = jnp.zeros_like(m_s)

        gates = _ml_gates(gcol_ref, grow_ref)
        for h in range(hh):
            cp, n_prev, m_prev = c_s[h], n_s[h], m_s[h][:, 0:1]
            cs_ref[0, h] = cp
            ns_ref[0, h] = n_prev
            ms_ref[0, h] = m_s[h]
            f = _ml_chunk(h, dh, mq_ref, mk_ref, v_ref, gates, cp, n_prev, m_prev)
            h_ref[:, f["sl"]] = f["num"] / f["dnm"]
            c_s[h] = f["decay"] * cp + lax.dot_general((f["vc"] * f["wk"]).astype(BF16), f["kb"], TN,
                                                       preferred_element_type=F32)
            n_s[h] = f["decay"] * n_prev + jnp.sum(f["wk"] * f["kc"], axis=0, keepdims=True)
            m_s[h] = jnp.broadcast_to(f["m_new"], (1, LANES))

    dblk = d_model
    return pl.pallas_call(
        body, name=name, grid=(nc,),
        in_specs=[pl.BlockSpec((CHUNK, dblk), lambda c: (c, 0)), pl.BlockSpec((CHUNK, dblk), lambda c: (c, 1)),
                  pl.BlockSpec((CHUNK, dblk), lambda c: (c, vcol // dblk)),
                  pl.BlockSpec((CHUNK, LANES), lambda c: (c, 0)), pl.BlockSpec((1, 8, CHUNK), lambda c: (c, 0, 0))],
        out_specs=[pl.BlockSpec((CHUNK, dblk), lambda c: (c, 0)),
                   pl.BlockSpec((1, hh, dh, dh), lambda c: (c, 0, 0, 0)),
                   pl.BlockSpec((1, hh, 1, dh), lambda c: (c, 0, 0, 0)),
                   pl.BlockSpec((1, hh, 1, LANES), lambda c: (c, 0, 0, 0))],
        out_shape=[pltpu.HBM((s, d_model), F32), pltpu.HBM((nc, hh, dh, dh), F32),
                   pltpu.HBM((nc, hh, 1, dh), F32), pltpu.HBM((nc, hh, 1, LANES), F32)],
        scratch_shapes=[pltpu.VMEM((hh, dh, dh), F32), pltpu.VMEM((hh, 1, dh), F32), pltpu.VMEM((hh, 1, LANES), F32)],
        compiler_params=_params(8 * hh * dh * dh * 4 + (16 << 20)),
    )(_hbm(mqk), _hbm(mqk), _hbm(zm), _hbm(gcol), _hbm(grow))


def _ml_bwd(mqk, zm, vcol, gcol, grow, cs, ns, ms, dhm, d_model, *, name):
    s = zm.shape[0]
    nc = s // CHUNK
    dh = d_model // ML_HEADS
    hh = ML_HEADS
    l = CHUNK

    def body(mq_ref, mk_ref, v_ref, gcol_ref, grow_ref, cs_ref, ns_ref, ms_ref, dh_ref,
             dq_ref, dk_ref, dv_ref, dgc_ref, dgr_ref, dc_s, dn_s):
        @pl.when(pl.program_id(0) == 0)
        def _():
            dc_s[...] = jnp.zeros_like(dc_s)
            dn_s[...] = jnp.zeros_like(dn_s)

        gates = _ml_gates(gcol_ref, grow_ref)
        lane = lax.broadcasted_iota(jnp.int32, (l, LANES), 1)
        rowi = lax.broadcasted_iota(jnp.int32, (8, l), 0)
        lastrow = lax.broadcasted_iota(jnp.int32, (l, 1), 0) == l - 1
        dgc = jnp.zeros((l, LANES), F32)
        dgr = jnp.zeros((8, l), F32)
        for h in range(hh):
            cp, n_prev, m_prev = cs_ref[0, h], ns_ref[0, h], ms_ref[0, h][:, 0:1]
            f = _ml_chunk(h, dh, mq_ref, mk_ref, v_ref, gates, cp, n_prev, m_prev)
            dC, dn = dc_s[h], dn_s[h]
            dhv = dh_ref[:, f["sl"]]
            dnum = dhv / f["dnm"]
            hv = f["num"] / f["dnm"]
            ddnm = -jnp.sum(dhv * hv, axis=1, keepdims=True) / f["dnm"]
            dden = jnp.where(jnp.abs(f["den"]) >= f["floor"], ddnm * jnp.sign(f["den"]), 0.0)
            dnb = dnum.astype(BF16)
            dsc = lax.dot_general(dnb, f["vb"], NT, preferred_element_type=F32) + dden
            dvc = lax.dot_general(f["sc"].astype(BF16), dnb, TN, preferred_element_type=F32)
            ds_inter = jnp.sum(dnum * f["qcp"], axis=1, keepdims=True) + dden * f["qn"]
            sdn = (f["s_inter"] * dnum).astype(BF16)
            sdd = f["s_inter"] * dden
            da = dsc * f["w"]
            dab = da.astype(BF16)
            dqc = (lax.dot_general(dab, f["kb"], NN, preferred_element_type=F32)
                   + lax.dot_general(sdn, f["cpb"], NN, preferred_element_type=F32) + sdd * n_prev)
            dcp = f["decay"] * dC + lax.dot_general(sdn, f["qb"], TN, preferred_element_type=F32)
            dnp = f["decay"] * dn + jnp.sum(sdd * f["qc"], axis=0, keepdims=True)
            vw = (f["vc"] * f["wk"]).astype(BF16)
            dCb = dC.astype(BF16)
            dkc = (lax.dot_general(dab, f["qb"], TN, preferred_element_type=F32)
                   + lax.dot_general(vw, dCb, NN, preferred_element_type=F32) + f["wk"] * dn)
            e = lax.dot_general(f["kb"], dCb, NT, preferred_element_type=F32)
            dvc = dvc + e * f["wk"]
            dwk = jnp.sum(e * f["vc"], axis=1, keepdims=True) + jnp.sum(f["kc"] * dn, axis=1, keepdims=True)
            ddecay = jnp.sum(jnp.sum(dC * cp, axis=1, keepdims=True), axis=0, keepdims=True) \
                + jnp.sum(dn * n_prev, axis=1, keepdims=True)
            dd = dsc * f["sc"]
            dlw = dwk * f["wk"]
            db_end = jnp.sum(dlw, axis=0, keepdims=True) + ddecay * f["decay"]
            di_col = dlw
            db_col = jnp.sum(dd, axis=1, keepdims=True) + ds_inter * f["s_inter"] - dlw \
                + jnp.where(lastrow, db_end, 0.0)
            cs_dd = jnp.sum(dd, axis=0, keepdims=True)
            dgc = dgc + jnp.where(lane == h, di_col, 0.0) + jnp.where(lane == hh + h, db_col, 0.0)
            dgr = dgr + jnp.where(rowi == h, cs_dd, 0.0) - jnp.where(rowi == hh + h, cs_dd, 0.0)
            dq_ref[:, f["sl"]] = dqc
            dk_ref[:, f["sl"]] = dkc * (dh ** -0.5)
            dv_ref[:, f["sl"]] = dvc.astype(dv_ref.dtype)
            dc_s[h] = dcp
            dn_s[h] = dnp
        dgc_ref[...] = dgc
        dgr_ref[0] = dgr

    dblk = d_model
    rev = lambda c: nc - 1 - c
    return pl.pallas_call(
        body, name=name, grid=(nc,),
        in_specs=[pl.BlockSpec((l, dblk), lambda c: (rev(c), 0)), pl.BlockSpec((l, dblk), lambda c: (rev(c), 1)),
                  pl.BlockSpec((l, dblk), lambda c: (rev(c), vcol // dblk)),
                  pl.BlockSpec((l, LANES), lambda c: (rev(c), 0)), pl.BlockSpec((1, 8, l), lambda c: (rev(c), 0, 0)),
                  pl.BlockSpec((1, hh, dh, dh), lambda c: (rev(c), 0, 0, 0)),
                  pl.BlockSpec((1, hh, 1, dh), lambda c: (rev(c), 0, 0, 0)),
                  pl.BlockSpec((1, hh, 1, LANES), lambda c: (rev(c), 0, 0, 0)),
                  pl.BlockSpec((l, dblk), lambda c: (rev(c), 0))],
        out_specs=[pl.BlockSpec((l, dblk), lambda c: (rev(c), 0)), pl.BlockSpec((l, dblk), lambda c: (rev(c), 0)),
                   pl.BlockSpec((l, dblk), lambda c: (rev(c), 0)), pl.BlockSpec((l, LANES), lambda c: (rev(c), 0)),
                   pl.BlockSpec((1, 8, l), lambda c: (rev(c), 0, 0))],
        out_shape=[pltpu.HBM((s, d_model), F32), pltpu.HBM((s, d_model), F32),
                   pltpu.HBM((s, d_model), BF16), pltpu.HBM((s, LANES), F32),
                   pltpu.HBM((nc, 8, l), F32)],
        scratch_shapes=[pltpu.VMEM((hh, dh, dh), F32), pltpu.VMEM((hh, 1, dh), F32)],
        compiler_params=_params(10 * hh * dh * dh * 4 + (16 << 20)),
    )(*[_hbm(a) for a in (mqk, mqk, zm, gcol, grow, cs, ns, ms, dhm)])


def _xa_fwd(zm, qcol, kv, gq, gk, d_model, *, name, tq=256):
    s = zm.shape[0]
    nm = kv.shape[0]
    dh = d_model // X_HEADS
    tq = _pick(s, (tq, 128, 64))
    scale = dh ** -0.5

    def body(q_ref, k_ref, v_ref, gq_ref, gk_ref, o_ref):
        qn = _rms_fwd(q_ref[...], gq_ref[...])
        kn = _rms_fwd(k_ref[...], gk_ref[...])
        lg = _dot(qn, kn, NT) * scale
        lg = lg - jnp.max(lg, axis=1, keepdims=True)
        p = jnp.exp(lg)
        p = p / jnp.sum(p, axis=1, keepdims=True)
        o_ref[...] = _dot(p, v_ref[...], NN).astype(o_ref.dtype)

    return pl.pallas_call(
        body, name=name, grid=(X_HEADS, s // tq),
        in_specs=[pl.BlockSpec((tq, dh), lambda h, i: (i, qcol // dh + h)), pl.BlockSpec((nm, dh), lambda h, i: (0, h)),
                  pl.BlockSpec((nm, dh), lambda h, i: (0, X_HEADS + h)),
                  pl.BlockSpec((1, dh), lambda h, i: (0, 0)), pl.BlockSpec((1, dh), lambda h, i: (0, 0))],
        out_specs=pl.BlockSpec((tq, dh), lambda h, i: (i, h)),
        out_shape=pltpu.HBM((s, d_model), BF16),
        compiler_params=_params(32 << 20),
    )(_hbm(zm), _hbm(kv), _hbm(kv), gq, gk)


def _xa_bwd(zm, qcol, kv, gq, gk, dy, d_model, *, name, tq=256):
    s = zm.shape[0]
    nm = kv.shape[0]
    dh = d_model // X_HEADS
    tq = _pick(s, (tq, 128, 64))
    nq = s // tq
    scale = dh ** -0.5

    def body(q_ref, k_ref, v_ref, gq_ref, gk_ref, do_ref, dq_ref, dkn_ref, dv_ref, dgq_ref):
        h, i = pl.program_id(0), pl.program_id(1)

        @pl.when(i == 0)
        def _():
            dkn_ref[...] = jnp.zeros_like(dkn_ref)
            dv_ref[...] = jnp.zeros_like(dv_ref)

        @pl.when((i == 0) & (h == 0))
        def _():
            dgq_ref[...] = jnp.zeros_like(dgq_ref)

        q = q_ref[...]
        qn = _rms_fwd(q, gq_ref[...])
        kn = _rms_fwd(k_ref[...], gk_ref[...])
        lg = _dot(qn, kn, NT) * scale
        lg = lg - jnp.max(lg, axis=1, keepdims=True)
        p = jnp.exp(lg)
        p = p / jnp.sum(p, axis=1, keepdims=True)
        do = do_ref[...]
        dv_ref[...] += _dot(p, do, TN)
        dp = _dot(do, v_ref[...], NT)
        dlg = p * (dp - jnp.sum(dp * p, axis=1, keepdims=True)) * scale
        dqn = _dot(dlg, kn, NN)
        dkn_ref[...] += _dot(dlg, qn, TN)
        dq, dgq = _rms_bwd(q, gq_ref[...], dqn)
        dq_ref[...] = dq.astype(dq_ref.dtype)
        dgq_ref[...] += jnp.sum(dgq, axis=0, keepdims=True)

    return pl.pallas_call(
        body, name=name, grid=(X_HEADS, nq),
        in_specs=[pl.BlockSpec((tq, dh), lambda h, i: (i, qcol // dh + h)), pl.BlockSpec((nm, dh), lambda h, i: (0, h)),
                  pl.BlockSpec((nm, dh), lambda h, i: (0, X_HEADS + h)),
                  pl.BlockSpec((1, dh), lambda h, i: (0, 0)), pl.BlockSpec((1, dh), lambda h, i: (0, 0)),
                  pl.BlockSpec((tq, dh), lambda h, i: (i, h))],
        out_specs=[pl.BlockSpec((tq, dh), lambda h, i: (i, h)), pl.BlockSpec((nm, dh), lambda h, i: (0, h)),
                   pl.BlockSpec((nm, dh), lambda h, i: (0, h)), pl.BlockSpec((1, dh), lambda h, i: (0, 0))],
        out_shape=[pltpu.HBM((s, d_model), BF16), pltpu.HBM((nm, d_model), F32),
                   pltpu.HBM((nm, d_model), F32), pltpu.HBM((1, dh), F32)],
        compiler_params=_params(32 << 20),
    )(_hbm(zm), _hbm(kv), _hbm(kv), gq, gk, _hbm(dy))


def _place():
    return lax.axis_index("x"), lax.axis_index("y"), lax.axis_index("c")


ANY = pl.BlockSpec(memory_space=pl.ANY)


def _allgather_quarters(shards, *, name):
    n = len(shards)

    def body(*refs):
        ins, outs = refs[:n], refs[n:2 * n]
        send, recv, loc = refs[2 * n:]
        x, y, c = _place()
        chips = [(1 - x, y), (x, 1 - y), (1 - x, 1 - y)]
        local = []
        for t in range(n):
            cp = pltpu.make_async_copy(ins[t], outs[t].at[2 * x + y], loc.at[t])
            cp.start()
            local.append(cp)

        def copy(t, j, slot):
            return pltpu.make_async_remote_copy(
                src_ref=ins[t], dst_ref=outs[t].at[slot], send_sem=send.at[3 * t + j], recv_sem=recv.at[3 * t + j],
                device_id=(chips[j][0], chips[j][1], c), device_id_type=MESH)

        for t in range(n):
            for j in range(3):
                copy(t, j, 2 * x + y).start()
        for t in range(n):
            for j in range(3):
                copy(t, j, 2 * chips[j][0] + chips[j][1]).wait_recv()
        for t in range(n):
            for j in range(3):
                copy(t, j, 2 * x + y).wait_send()
        for cp in local:
            cp.wait()

    return pl.pallas_call(
        body, name=name, in_specs=[ANY] * n, out_specs=[ANY] * n,
        out_shape=[pltpu.HBM((4,) + a.shape, a.dtype) for a in shards],
        scratch_shapes=[pltpu.SemaphoreType.DMA((3 * n,)), pltpu.SemaphoreType.DMA((3 * n,)),
                        pltpu.SemaphoreType.DMA((n,))],
    )(*shards)


def _exchange_grads(parts, *, name):
    n = len(parts)

    def body(*refs):
        ins, outs = refs[:n], refs[n:2 * n]
        send, recv, loc = refs[2 * n:]
        x, y, c = _place()
        me = 4 * x + 2 * y + c
        peers = [(x ^ ((j >> 2) & 1), y ^ ((j >> 1) & 1), c ^ (j & 1)) for j in range(1, 8)]
        local = []
        for t in range(n):
            cp = pltpu.make_async_copy(ins[t].at[2 * x + y, c], outs[t].at[me], loc.at[t])
            cp.start()
            local.append(cp)

        def copy(t, j, slot):
            px, py, pc = peers[j]
            return pltpu.make_async_remote_copy(
                src_ref=ins[t].at[2 * px + py, pc], dst_ref=outs[t].at[slot], send_sem=send.at[7 * t + j],
                recv_sem=recv.at[7 * t + j], device_id=(px, py, pc), device_id_type=MESH)

        for t in range(n):
            for j in range(7):
                copy(t, j, me).start()
        for t in range(n):
            for j in range(7):
                px, py, pc = peers[j]
                copy(t, j, 4 * px + 2 * py + pc).wait_recv()
        for t in range(n):
            for j in range(7):
                copy(t, j, me).wait_send()
        for cp in local:
            cp.wait()

    return pl.pallas_call(
        body, name=name, in_specs=[ANY] * n, out_specs=[ANY] * n,
        out_shape=[pltpu.HBM((8,) + a.shape[2:], a.dtype) for a in parts],
        scratch_shapes=[pltpu.SemaphoreType.DMA((7 * n,)), pltpu.SemaphoreType.DMA((7 * n,)),
                        pltpu.SemaphoreType.DMA((n,))],
    )(*parts)


HBM_SPEC = pl.BlockSpec(memory_space=pltpu.HBM)
SEM_SPEC = pl.BlockSpec(memory_space=pltpu.SEMAPHORE)
EFFECT = pltpu.SideEffectType.DATAFLOW_SIDE_EFFECTING


def _split_copies(kind, srcs, lands, send, recv):
    x, y, c = _place()
    if kind == "quarters":
        peers = [(1 - x, y, c), (x, 1 - y, c), (1 - x, 1 - y, c)]
    else:
        peers = [(x ^ ((j >> 2) & 1), y ^ ((j >> 1) & 1), c ^ (j & 1)) for j in range(1, 8)]
    npeer = len(peers)
    out = []
    for t in range(len(srcs)):
        for j, (px, py, pc) in enumerate(peers):
            if kind == "quarters":
                src, mine, theirs = srcs[t], 2 * x + y, 2 * px + py
            else:
                src, mine, theirs = srcs[t].at[2 * px + py, pc], 4 * x + 2 * y + c, 4 * px + 2 * py + pc
            mk = functools.partial(
                pltpu.make_async_remote_copy, src_ref=src, send_sem=send.at[npeer * t + j],
                recv_sem=recv.at[npeer * t + j], device_id=(px, py, pc), device_id_type=MESH)
            out.append((functools.partial(mk, dst_ref=lands[t].at[mine]),
                        functools.partial(mk, dst_ref=lands[t].at[theirs])))
    return out


def _split_start(kind, srcs, land_shapes, after, *, name):
    n = len(srcs)
    ncopies = n * (3 if kind == "quarters" else 7)

    def body(*refs):
        ins, lands = refs[:n], refs[n:2 * n]
        send, recv = refs[2 * n + 1], refs[2 * n + 2]
        token = refs[-1]
        for start, _ in _split_copies(kind, ins, lands, send, recv):
            start().start()
        token[...] = jnp.zeros_like(token)

    lands = [_hbm(lax.empty(shp, a.dtype)) for shp, a in zip(land_shapes, srcs)]
    res = pl.pallas_call(
        body, name=name, in_specs=[HBM_SPEC] * (2 * n) + [ANY],
        out_specs=[SEM_SPEC, SEM_SPEC] + [HBM_SPEC] * (2 * n) + [pl.BlockSpec(memory_space=pltpu.VMEM)],
        out_shape=[pltpu.SemaphoreType.DMA((ncopies,)), pltpu.SemaphoreType.DMA((ncopies,))]
        + [pltpu.HBM(a.shape, a.dtype) for a in srcs] + [pltpu.HBM(shp, a.dtype) for shp, a in zip(land_shapes, srcs)]
        + [jax.ShapeDtypeStruct((8, LANES), F32)],
        input_output_aliases={i: 2 + i for i in range(2 * n)},
        compiler_params=pltpu.CompilerParams(has_side_effects=EFFECT),
    )(*[_hbm(a) for a in srcs], *lands, after)
    return res[0], res[1], list(res[2:2 + n]), list(res[2 + n:2 + 2 * n]), res[-1]


def _split_wait(kind, send, recv, srcs, lands, after, *, name):
    n = len(srcs)

    def body(*refs):
        ins, lnd = refs[:n], refs[n:2 * n]
        snd, rcv = refs[2 * n], refs[2 * n + 1]
        for start, arrive in _split_copies(kind, ins, lnd, snd, rcv):
            start().wait_send()
            arrive().wait_recv()

    res = pl.pallas_call(
        body, name=name, in_specs=[HBM_SPEC] * (2 * n) + [SEM_SPEC, SEM_SPEC] + [ANY] * len(after),
        out_specs=[HBM_SPEC] * (2 * n),
        out_shape=[pltpu.HBM(a.shape, a.dtype) for a in srcs] + [pltpu.HBM(a.shape, a.dtype) for a in lands],
        input_output_aliases={i: i for i in range(2 * n)},
        compiler_params=pltpu.CompilerParams(has_side_effects=EFFECT),
    )(*srcs, *lands, send, recv, *after)
    return list(res[n:])


def _sum8(parts, *, name):
    _, r, c = parts.shape
    t = _pick(r, (128, 64, 32, 16, 8))

    def body(p_ref, o_ref):
        acc = p_ref[0].astype(F32)
        for k in range(1, 8):
            acc = acc + p_ref[k].astype(F32)
        o_ref[...] = acc

    return pl.pallas_call(
        body, name=name, grid=(r // t,), in_specs=[pl.BlockSpec((8, t, c), lambda i: (0, i, 0))],
        out_specs=pl.BlockSpec((t, c), lambda i: (i, 0)), out_shape=pltpu.HBM((r, c), F32),
        compiler_params=_params(2 * 8 * t * c * 2 + 6 * t * c * 4 + (4 << 20)),
    )(_hbm(parts))


def _swap_halves(halves, *, name, chunk_bytes=512 * 1024):
    n = len(halves)
    items = []
    for t, a in enumerate(halves):
        r = a.shape[0]
        k = 1
        while _nbytes(a.shape, a.dtype) // k > chunk_bytes and r % (2 * k) == 0 and (r // (2 * k)) % 8 == 0:
            k *= 2
        items += [(t, q * (r // k), r // k) for q in range(k)]
    m = len(items)

    def body(*refs):
        ins, outs = refs[:n], refs[n:2 * n]
        sbuf, rbuf = refs[2 * n:3 * n], refs[3 * n:4 * n]
        send, recv, loc_own, loc_in, loc_out = refs[4 * n:]
        x, y, c = _place()
        local, stage = [], []
        for t in range(n):
            cp = pltpu.make_async_copy(ins[t], outs[t].at[c], loc_own.at[t])
            cp.start()
            local.append(cp)
        for q, (t, r0, nr) in enumerate(items):
            cp = pltpu.make_async_copy(ins[t].at[pl.ds(r0, nr)], sbuf[t].at[pl.ds(r0, nr)], loc_in.at[q])
            cp.start()
            stage.append(cp)

        def copy(q):
            t, r0, nr = items[q]
            return pltpu.make_async_remote_copy(
                src_ref=sbuf[t].at[pl.ds(r0, nr)], dst_ref=rbuf[t].at[pl.ds(r0, nr)], send_sem=send.at[q],
                recv_sem=recv.at[q], device_id=(x, y, 1 - c), device_id_type=MESH)

        for q in range(m):
            stage[q].wait()
            copy(q).start()
        for q, (t, r0, nr) in enumerate(items):
            copy(q).wait_recv()
            cp = pltpu.make_async_copy(rbuf[t].at[pl.ds(r0, nr)], outs[t].at[1 - c, pl.ds(r0, nr)], loc_out.at[q])
            cp.start()
            local.append(cp)
        for q in range(m):
            copy(q).wait_send()
        for cp in local:
            cp.wait()

    stage_bytes = 2 * sum(_nbytes(a.shape, a.dtype) for a in halves)
    return pl.pallas_call(
        body, name=name, in_specs=[ANY] * n, out_specs=[ANY] * n,
        out_shape=[pltpu.HBM((2,) + a.shape, a.dtype) for a in halves],
        scratch_shapes=[pltpu.VMEM(a.shape, a.dtype) for a in halves] * 2
        + [pltpu.SemaphoreType.DMA((m,)), pltpu.SemaphoreType.DMA((m,)), pltpu.SemaphoreType.DMA((n,)),
           pltpu.SemaphoreType.DMA((m,)), pltpu.SemaphoreType.DMA((m,))],
        compiler_params=_params(stage_bytes + (4 << 20)),
    )(*halves)


def _allreduce_small(p, *, name):
    r = p.shape[0]

    def body(p_ref, o_ref, buf, send, recv):
        x, y, c = _place()
        me = 4 * x + 2 * y + c
        peers = [(x ^ ((j >> 2) & 1), y ^ ((j >> 1) & 1), c ^ (j & 1)) for j in range(1, 8)]

        def copy(j, slot):
            return pltpu.make_async_remote_copy(
                src_ref=p_ref, dst_ref=buf.at[slot], send_sem=send.at[j], recv_sem=recv.at[j],
                device_id=peers[j], device_id_type=MESH)

        for j in range(7):
            copy(j, me).start()
        buf[me] = p_ref[...]
        for j in range(7):
            px, py, pc = peers[j]
            copy(j, 4 * px + 2 * py + pc).wait_recv()
        for j in range(7):
            copy(j, me).wait_send()
        acc = buf[0]
        for k in range(1, 8):
            acc = acc + buf[k]
        o_ref[...] = acc

    vspec = pl.BlockSpec(memory_space=pltpu.VMEM)
    return pl.pallas_call(
        body, name=name, in_specs=[vspec], out_specs=vspec, out_shape=jax.ShapeDtypeStruct((r, LANES), F32),
        scratch_shapes=[pltpu.VMEM((8, r, LANES), F32), pltpu.SemaphoreType.DMA((7,)), pltpu.SemaphoreType.DMA((7,))],
    )(p)


def _adamw_fn(w, g, m, v):
    m = ADAM_B1 * m + (1.0 - ADAM_B1) * g
    v = ADAM_B2 * v + (1.0 - ADAM_B2) * (g * g)
    m_hat = m / (1.0 - ADAM_B1 ** ADAM_STEP)
    v_hat = v / (1.0 - ADAM_B2 ** ADAM_STEP)
    delta = -ADAM_LR * (m_hat / (jnp.sqrt(v_hat) + ADAM_EPS) + ADAM_WD * w)
    return delta, m, v


def _adamw(w, g, m, v, *, name):
    c = w.shape[1]
    return _rowwise(_adamw_fn, [w, g, m, v], [], [(c, F32)] * 3, name=name, tr=128)


def _pack(vecs, rows):
    flat = jnp.concatenate([a.reshape(-1).astype(F32) for a in vecs])
    return jnp.pad(flat, (0, rows * LANES - flat.shape[0])).reshape(rows, LANES)


def _unpack(p, like):
    flat, out, o = p.reshape(-1), [], 0
    for a in like:
        out.append(flat[o:o + a.size].reshape(a.shape))
        o += a.size
    return out


def kernel(x, mem, g_mix, w_in, b_if, b_gate, conv_w, conv_b, ml_norm_g, g_mem, w_mem_kv, q_norm_g, k_norm_g, w_sb_proj, w_ml_proj, w_x_proj, w_out, g_mlp, w_ff1, w_ff2, loss_target, m_g_mix, m_w_in, m_b_if, m_b_gate, m_conv_w, m_conv_b, m_ml_norm_g, m_g_mem, m_w_mem_kv, m_q_norm_g, m_k_norm_g, m_w_sb_proj, m_w_ml_proj, m_w_x_proj, m_w_out, m_g_mlp, m_w_ff1, m_w_ff2, v_g_mix, v_w_in, v_b_if, v_b_gate, v_conv_w, v_conv_b, v_ml_norm_g, v_g_mem, v_w_mem_kv, v_q_norm_g, v_k_norm_g, v_w_sb_proj, v_w_ml_proj, v_w_x_proj, v_w_out, v_g_mlp, v_w_ff1, v_w_ff2):
    _, s, d = x.shape
    nm = mem.shape[1]
    n_in = 4 * w_in.shape[2]
    dff = 4 * w_ff1.shape[2]
    sbh = d // SB_HD
    hh = ML_HEADS
    dh = d // hh
    nc = s // CHUNK
    assert n_in == 11 * d + 2 * hh and d % (2 * LANES) == 0 and s % LANES == 0
    x2, mem2, tgt = x[0], mem[0], loss_target[0]

    k4 = 2 * lax.axis_index("x") + lax.axis_index("y")
    me = 2 * k4 + lax.axis_index("c")
    g_first = _allgather_quarters([w_in[0].astype(BF16), conv_w[0]], name="gather_w_in")
    later = [a[0].astype(BF16) for a in (w_mem_kv, w_sb_proj, w_ml_proj, w_x_proj, w_out, w_ff1, w_ff2)]
    gw_send, gw_recv, gw_src, gw_land, gw_token = _split_start(
        "quarters", later, [(4,) + a.shape for a in later], g_first[0], name="gather_rest_start")
    cols = lambda a: a.transpose(1, 0, 2).reshape(a.shape[1], 4 * a.shape[2])
    rws = lambda a: a.reshape(4 * a.shape[1], a.shape[2])
    w_in_f = cols(g_first[0])
    w_main = jnp.concatenate([w_in_f[:, :7 * d], w_in_f[:, 7 * d + 2 * hh:]], axis=1)
    w_if = jnp.pad(w_in_f[:, 7 * d:7 * d + 2 * hh], ((0, 0), (0, LANES - 2 * hh)))
    conv_wf = cols(g_first[1])
    b_if_p = jnp.pad(b_if, ((0, 0), (0, LANES - 2 * hh)))

    (hn,) = _rowwise(_rms_fwd, [x2], [g_mix], [(d, BF16)], name="norm_in")
    zm = _mm(hn, w_main, after=gw_token, name="proj_in")
    zif = _mm(hn, w_if, name="proj_if")
    y_sb, ltot = _sb_fwd(zm, sbh, name="sb_fwd")

    def gate_fn(z, b):
        pre = z + b
        lane = lax.broadcasted_iota(jnp.int32, pre.shape, 1)
        return jnp.where(lane < hh, pre, -_softplus(-pre))

    (gcol,) = _rowwise(gate_fn, [zif], [b_if_p], [(LANES, F32)], name="ml_gates")
    grow = gcol[:, :8].T.reshape(8, nc, CHUNK).transpose(1, 0, 2)
    mqk = _conv_fwd(zm, 3 * d, 2 * d, conv_wf, conv_b, name="conv_fwd")
    hm, cst, nst, mst = _ml_fwd(mqk, zm, 5 * d, gcol, grow, d, name="ml_fwd")

    def mlout_fn(hv, o, g):
        ys = [_rms_fwd(hv[:, k * dh:(k + 1) * dh], g[:, k * dh:(k + 1) * dh]) for k in range(hh)]
        return jnp.concatenate(ys, axis=1) * _sigmoid(o)

    (y_ml,) = _rowwise(mlout_fn, [hm, (zm, d, 6)], [ml_norm_g], [(d, BF16)], name="ml_out")
    gw_land = _split_wait("quarters", gw_send, gw_recv, gw_src, gw_land, [y_ml, y_sb], name="gather_rest_wait")
    gw = [lax.dynamic_update_index_in_dim(ld, a, k4, 0) for ld, a in zip(gw_land, later)]
    w_kv, w_sbp, w_mlp, w_xp, w_o, w_f1, w_f2 = (cols(gw[0]), rws(gw[1]), rws(gw[2]), rws(gw[3]), rws(gw[4]),
                                                 cols(gw[5]), rws(gw[6]))
    (memn,) = _rowwise(_rms_fwd, [mem2], [g_mem], [(d, BF16)], name="norm_mem")
    kv = _mm(memn, w_kv, name="proj_kv")
    y_x = _xa_fwd(zm, 7 * d, kv, q_norm_g, k_norm_g, d, name="xa_fwd")
    p_sb = _mm(y_sb, w_sbp, name="proj_sb")
    p_ml = _mm(y_ml, w_mlp, name="proj_ml")
    p_x = _mm(y_x, w_xp, name="proj_x")

    def merge_fn(a, b, c, g0, g1, g2, bg):
        return (_sigmoid(g0 + bg[:, :d]) * a + _sigmoid(g1 + bg[:, d:2 * d]) * b + _sigmoid(g2 + bg[:, 2 * d:]) * c)

    gate_cols = [(zm, d, 8), (zm, d, 9), (zm, d, 10)]
    (mixed,) = _rowwise(merge_fn, [p_sb, p_ml, p_x] + gate_cols, [b_gate], [(d, BF16)], name="merge")
    x1 = _mm(mixed, w_o, add=x2, name="proj_out")
    (h2,) = _rowwise(_rms_fwd, [x1], [g_mlp], [(d, BF16)], name="norm_mlp")
    u = _mm(h2, w_f1, name="ff1")
    (act,) = _rowwise(lambda uv: jnp.square(jnp.maximum(uv, 0.0)), [u], [], [(dff, BF16)], name="relu2", tr=128)
    yo = _mm(act, w_f2, add=x1, name="ff2")

    def loss_fn(yv, tv):
        e = yv - tv
        return e * (1.0 / d), jnp.sum(e * e, axis=0, keepdims=True) * (0.5 / d)

    dy, loss_cols = _rowwise(loss_fn, [yo, tgt], [], [(d, F32)], [d], name="loss")

    dact = _mm(dy, w_f2, tb=True, name="ff2_dx")
    dw_f2 = _mm(act, dy, ta=True, name="ff2_dw")
    (du,) = _rowwise(lambda g, uv: g * 2.0 * jnp.maximum(uv, 0.0), [dact, u], [], [(dff, BF16)], name="relu2_bwd",
                     tr=128)
    dw_f1 = _mm(h2, du, ta=True, name="ff1_dw")
    dh2 = _mm(du, w_f1, tb=True, name="ff1_dx")

    def norm_bwd_fn(xv, dyv, res, g):
        dx, dg = _rms_bwd(xv, g, dyv)
        return dx + res, jnp.sum(dg, axis=0, keepdims=True)

    dx1, dg_mlp = _rowwise(norm_bwd_fn, [x1, dh2, dy], [g_mlp], [(d, F32)], [d], name="norm_mlp_bwd")
    dmixed = _mm(dx1, w_o, tb=True, name="proj_out_dx")
    dw_o = _mm(mixed, dx1, ta=True, name="proj_out_dw")

    def merge_bwd_fn(dm, a, b, c, g0, g1, g2, bg):
        outs, dgs = [], []
        for p, g, k in ((a, g0, 0), (b, g1, 1), (c, g2, 2)):
            sg = _sigmoid(g + bg[:, k * d:(k + 1) * d])
            outs.append(dm * sg)
            dgs.append(dm * p * sg * (1.0 - sg))
        dgate = jnp.concatenate(dgs, axis=1)
        return (*outs, dgate, jnp.sum(dgate, axis=0, keepdims=True))

    dp_sb, dp_ml, dp_x, dgate, db_gate = _rowwise(
        merge_bwd_fn, [dmixed, p_sb, p_ml, p_x] + gate_cols, [b_gate], [(d, BF16)] * 3 + [(3 * d, BF16)], [3 * d],
        name="merge_bwd", tr=128)
    dw_sbp = _mm(y_sb, dp_sb, ta=True, name="proj_sb_dw")
    dw_mlp = _mm(y_ml, dp_ml, ta=True, name="proj_ml_dw")
    dw_xp = _mm(y_x, dp_x, ta=True, name="proj_x_dw")
    dy_sb = _mm(dp_sb, w_sbp, tb=True, out_dtype=BF16, name="proj_sb_dx")
    dy_ml = _mm(dp_ml, w_mlp, tb=True, name="proj_ml_dx")
    dy_x = _mm(dp_x, w_xp, tb=True, out_dtype=BF16, name="proj_x_dx")

    uncols = lambda a: a.reshape(a.shape[0], 4, a.shape[1] // 4).transpose(1, 0, 2)
    unrws = lambda a: a.reshape(4, a.shape[0] // 4, a.shape[1])
    to_parts = lambda q: q.astype(BF16).reshape(4, 2, q.shape[1] // 2, q.shape[2])
    early = [to_parts(q) for q in (unrws(dw_sbp), unrws(dw_mlp), unrws(dw_xp), unrws(dw_o), uncols(dw_f1),
                                   unrws(dw_f2))]
    ge_send, ge_recv, ge_src, ge_land, ge_token = _split_start(
        "grads", early, [(8,) + a.shape[2:] for a in early], dy_x, name="exchange_early_start")

    dsq, dsk, dsv = _sb_bwd(zm, dy_sb, ltot, ge_token, sbh, name="sb_bwd")

    def mlout_bwd_fn(dyv, hv, o, g):
        sg = _sigmoid(o)
        dn = dyv * sg
        dxs, dgs, ys = [], [], []
        for k in range(hh):
            sl = slice(k * dh, (k + 1) * dh)
            ys.append(_rms_fwd(hv[:, sl], g[:, sl]))
            dxk, dgk = _rms_bwd(hv[:, sl], g[:, sl], dn[:, sl])
            dxs.append(dxk)
            dgs.append(dgk)
        do = dyv * jnp.concatenate(ys, axis=1) * sg * (1.0 - sg)
        return jnp.concatenate(dxs, axis=1), do, jnp.sum(jnp.concatenate(dgs, axis=1), axis=0, keepdims=True)

    dhm, dmlo, dg_mln = _rowwise(mlout_bwd_fn, [dy_ml, hm, (zm, d, 6)], [ml_norm_g], [(d, F32), (d, BF16)], [d],
                                 name="ml_out_bwd")
    dmq, dmk, dmlv, dgc, dgr = _ml_bwd(mqk, zm, 5 * d, gcol, grow, cst, nst, mst, dhm, d, name="ml_bwd")
    dmqk = jnp.concatenate([dmq, dmk], axis=1)
    dmlqk, dconv_w, dconv_b = _conv_bwd(zm, 3 * d, 2 * d, conv_wf, conv_b, dmqk, name="conv_bwd")
    dgr_t = jnp.pad(dgr.transpose(1, 0, 2).reshape(8, s).T, ((0, 0), (0, LANES - 8)))

    def gate_bwd_fn(a, b, z, bias):
        tot = a + b
        r = lax.broadcasted_iota(jnp.int32, (CHUNK, CHUNK), 0)
        c = lax.broadcasted_iota(jnp.int32, (CHUNK, CHUNK), 1)
        dlf = _u01dot((c >= r).astype(BF16), tot)
        lane = lax.broadcasted_iota(jnp.int32, tot.shape, 1)
        dz = jnp.where(lane < hh, tot, jnp.where(lane < 2 * hh, dlf * _sigmoid(-(z + bias)), 0.0))
        return dz, jnp.sum(dz, axis=0, keepdims=True)

    dzif, db_if_p = _rowwise(gate_bwd_fn, [dgc, dgr_t, zif], [b_if_p], [(LANES, BF16)], [LANES], name="ml_gates_bwd",
                             tr=CHUNK)
    dxq, dkn, dxv, dg_qn = _xa_bwd(zm, 7 * d, kv, q_norm_g, k_norm_g, dy_x, d, name="xa_bwd")

    def knorm_bwd_fn(kvv, dknv, dvv, g):
        dks, dgs = [], []
        for k in range(X_HEADS):
            sl = slice(k * dh, (k + 1) * dh)
            dk, dg = _rms_bwd(kvv[:, sl], g, dknv[:, sl])
            dks.append(dk)
            dgs.append(jnp.sum(dg, axis=0, keepdims=True))
        return jnp.concatenate(dks + [dvv], axis=1), dgs[0] + dgs[1] + dgs[2] + dgs[3]

    dkv, dg_kn = _rowwise(knorm_bwd_fn, [(kv, d, 0), dkn, dxv], [k_norm_g], [(2 * d, BF16)], [dh], name="xa_knorm_bwd")
    dw_kv = _mm(memn, dkv, ta=True, name="proj_kv_dw")
    dmemn = _mm(dkv, w_kv, tb=True, name="proj_kv_dx")

    def gmem_fn(mv, dv_, g):
        _, dg = _rms_bwd(mv, g, dv_)
        return (jnp.sum(dg, axis=0, keepdims=True),)

    (dg_mem,) = _rowwise(gmem_fn, [mem2, dmemn], [g_mem], [], [d], name="norm_mem_bwd")

    dzm = jnp.concatenate([dsq, dsk, dsv, dmlqk, dmlv, dmlo, dxq, dgate], axis=1)
    dw_main = _mm(hn, dzm, ta=True, name="proj_in_dw")
    dw_if = _mm(hn, dzif, ta=True, name="proj_if_dw")
    dhn = _mm(dzm, w_main, tb=True, name="proj_in_dx")
    dhn = _mm(dzif, w_if, tb=True, add=dhn, name="proj_if_dx")
    dx, dg_mix = _rowwise(norm_bwd_fn, [x2, dhn, dx1], [g_mix], [(d, F32)], [d], name="norm_in_bwd")

    dw_in = jnp.concatenate([dw_main[:, :7 * d], dw_if[:, :2 * hh], dw_main[:, 7 * d:]], axis=1)
    recv_late = _exchange_grads([to_parts(uncols(dw_in)), to_parts(uncols(dw_kv))], name="exchange_grads")
    ge_land = _split_wait("grads", ge_send, ge_recv, ge_src, ge_land, [recv_late[0]], name="exchange_early_wait")
    own = lambda p: lax.dynamic_index_in_dim(lax.dynamic_index_in_dim(p, k4, 0, keepdims=False),
                                             lax.axis_index("c"), 0, keepdims=False)
    recv_early = [lax.dynamic_update_index_in_dim(ld, own(p), me, 0) for ld, p in zip(ge_land, early)]
    recv = list(recv_late) + recv_early
    halves = [_sum8(r, name=f"sum_grads_{i}") for i, r in enumerate(recv)]
    both = _swap_halves(halves, name="swap_halves")
    g_big = [b.reshape(2 * b.shape[1], b.shape[2]) for b in both]

    small_g = [dg_mix, db_if_p[:, :2 * hh], db_gate, dconv_w, dconv_b, dg_mln, dg_mem, dg_qn, dg_kn, dg_mlp,
               jnp.sum(loss_cols).reshape(1, 1)]
    n_small = sum(a.size for a in small_g)
    rows = -(-n_small // (8 * LANES)) * 8
    g_small = _unpack(_allreduce_small(_pack(small_g, rows), name="allreduce_small"), small_g)
    loss = g_small[-1].reshape(())
    k4 = 2 * lax.axis_index("x") + lax.axis_index("y")
    qw = conv_w.shape[2]
    g_conv_w = lax.dynamic_slice_in_dim(g_small[3], k4 * qw, qw, axis=1)
    g_small_w = [g_small[0], g_small[1], g_small[2], g_conv_w] + g_small[4:10]
    sm_w = [g_mix, b_if, b_gate, conv_w[0], conv_b, ml_norm_g, g_mem, q_norm_g, k_norm_g, g_mlp]
    sm_m = [m_g_mix, m_b_if, m_b_gate, m_conv_w[0], m_conv_b, m_ml_norm_g, m_g_mem, m_q_norm_g, m_k_norm_g, m_g_mlp]
    sm_v = [v_g_mix, v_b_if, v_b_gate, v_conv_w[0], v_conv_b, v_ml_norm_g, v_g_mem, v_q_norm_g, v_k_norm_g, v_g_mlp]
    n_sw = sum(a.size for a in sm_w)
    rows_w = -(-n_sw // (8 * LANES)) * 8
    sm_out = _adamw(_pack(sm_w, rows_w), _pack(g_small_w, rows_w), _pack(sm_m, rows_w), _pack(sm_v, rows_w),
                    name="adamw_small")
    sm_delta, sm_newm, sm_newv = [_unpack(p, sm_w) for p in sm_out]

    big_w = [w_in[0], w_mem_kv[0], w_sb_proj[0], w_ml_proj[0], w_x_proj[0], w_out[0], w_ff1[0], w_ff2[0]]
    big_m = [m_w_in[0], m_w_mem_kv[0], m_w_sb_proj[0], m_w_ml_proj[0], m_w_x_proj[0], m_w_out[0], m_w_ff1[0],
             m_w_ff2[0]]
    big_v = [v_w_in[0], v_w_mem_kv[0], v_w_sb_proj[0], v_w_ml_proj[0], v_w_x_proj[0], v_w_out[0], v_w_ff1[0],
             v_w_ff2[0]]
    big_out = [_adamw(w, g, m, v, name=f"adamw_{i}") for i, (w, g, m, v) in enumerate(zip(big_w, g_big, big_m, big_v))]

    order = ["g_mix", "w_in", "b_if", "b_gate", "conv_w", "conv_b", "ml_norm_g", "g_mem", "w_mem_kv", "q_norm_g",
             "k_norm_g", "w_sb_proj", "w_ml_proj", "w_x_proj", "w_out", "g_mlp", "w_ff1", "w_ff2"]
    small_names = ["g_mix", "b_if", "b_gate", "conv_w", "conv_b", "ml_norm_g", "g_mem", "q_norm_g", "k_norm_g", "g_mlp"]
    big_names = ["w_in", "w_mem_kv", "w_sb_proj", "w_ml_proj", "w_x_proj", "w_out", "w_ff1", "w_ff2"]
    grads, deltas, new_m, new_v = {}, {}, {}, {}
    for i, nme in enumerate(small_names):
        shp = sm_w[i].shape if nme != "conv_w" else conv_w.shape
        grads[nme] = g_small_w[i].reshape(shp)
        deltas[nme], new_m[nme], new_v[nme] = (sm_delta[i].reshape(shp), sm_newm[i].reshape(shp),
                                               sm_newv[i].reshape(shp))
    for i, nme in enumerate(big_names):
        grads[nme] = g_big[i][None]
        deltas[nme], new_m[nme], new_v[nme] = (o[None] for o in big_out[i])
    return (loss, dx[None], *[grads[k] for k in order], *[deltas[k] for k in order], *[new_m[k] for k in order],
            *[new_v[k] for k in order])
```

```python
import functools

import jax
import jax.numpy as jnp
from jax import lax
from jax.experimental import pallas as pl
from jax.experimental.pallas import tpu as pltpu

F32 = jnp.float32
BF16 = jnp.bfloat16
MESH = pl.DeviceIdType.MESH

EPS = 1e-6
SB_HD = 128
ML_HEADS = 4
X_HEADS = 4
CHUNK = 64
CONV_W = 4
LANES = 128
ADAM_LR = 0.001
ADAM_B1 = 0.9
ADAM_B2 = 0.999
ADAM_EPS = 1e-08
ADAM_WD = 0.01
ADAM_STEP = 10
VMEM_CAP = 56 * 1024 * 1024
NEG = -1e30

NT = (((1,), (1,)), ((), ()))
NN = (((1,), (0,)), ((), ()))
TN = (((0,), (0,)), ((), ()))


def _dot(a, b, dn=NN):
    return lax.dot_general(a.astype(BF16), b.astype(BF16), dn, preferred_element_type=F32)


def _dot01(x, u, dn=NN):
    hi = x.astype(BF16)
    lo = (x - hi.astype(F32)).astype(BF16)
    return (lax.dot_general(hi, u, dn, preferred_element_type=F32)
            + lax.dot_general(lo, u, dn, preferred_element_type=F32))


def _u01dot(u, x):
    hi = x.astype(BF16)
    lo = (x - hi.astype(F32)).astype(BF16)
    return (lax.dot_general(u, hi, NN, preferred_element_type=F32)
            + lax.dot_general(u, lo, NN, preferred_element_type=F32))


def _pick(n, cands):
    for c in cands:
        if c <= n and n % c == 0:
            return c
    return n


def _nbytes(shape, dtype):
    n = 1
    for s in shape:
        n *= s
    return n * jnp.dtype(dtype).itemsize


def _params(vmem_bytes):
    return pltpu.CompilerParams(vmem_limit_bytes=int(min(VMEM_CAP, max(vmem_bytes, 16 * 1024 * 1024))))


def _hbm(a):
    return pltpu.with_memory_space_constraint(a, pltpu.HBM)


def _softplus(z):
    return jnp.maximum(z, 0.0) + jnp.log(1.0 + jnp.exp(-jnp.abs(z)))


def _sigmoid(z):
    return 1.0 / (1.0 + jnp.exp(-z))


def _rms_fwd(xv, g):
    r = lax.rsqrt(jnp.mean(xv * xv, axis=-1, keepdims=True) + EPS)
    return xv * r * g


def _rms_bwd(xv, g, dy):
    r = lax.rsqrt(jnp.mean(xv * xv, axis=-1, keepdims=True) + EPS)
    xh = xv * r
    dxh = dy * g
    dx = r * (dxh - xh * jnp.mean(dxh * xh, axis=-1, keepdims=True))
    return dx, dy * xh


def _mm(a, b, *, name, ta=False, tb=False, add=None, out_dtype=F32, bm=1024, bn=1024, bk=1024, after=None):
    m, k = (a.shape[1], a.shape[0]) if ta else a.shape
    n = b.shape[0] if tb else b.shape[1]
    tm = _pick(m, (bm, 512, 256, 128))
    tn = _pick(n, (bn, 512, 256, 128))
    tk = _pick(k, (bk, 512, 256, 128))
    nk = k // tk
    dn = (((0 if ta else 1,), (1 if tb else 0,)), ((), ()))
    has_add = add is not None

    def body(*refs):
        a_ref, b_ref = refs[:2]
        c_ref = refs[2] if has_add else None
        o_ref = refs[2 + has_add + (after is not None)]
        part = lax.dot_general(a_ref[...].astype(BF16), b_ref[...].astype(BF16), dn, preferred_element_type=F32)

        def finish(r):
            if has_add:
                r = r + c_ref[...].astype(F32)
            o_ref[...] = r.astype(out_dtype)

        if nk == 1:
            finish(part)
        else:
            acc_ref = refs[-1]
            kk = pl.program_id(2)

            @pl.when(kk == 0)
            def _():
                acc_ref[...] = part

            @pl.when(kk > 0)
            def _():
                acc_ref[...] += part

            @pl.when(kk == nk - 1)
            def _():
                finish(acc_ref[...])

    a_spec = pl.BlockSpec((tk, tm), lambda i, j, q: (q, i)) if ta else pl.BlockSpec((tm, tk), lambda i, j, q: (i, q))
    b_spec = pl.BlockSpec((tn, tk), lambda i, j, q: (j, q)) if tb else pl.BlockSpec((tk, tn), lambda i, j, q: (q, j))
    o_spec = pl.BlockSpec((tm, tn), lambda i, j, q: (i, j))
    ins, specs = [_hbm(a), _hbm(b)], [a_spec, b_spec]
    vm = 2 * (_nbytes((tm, tk), a.dtype) + _nbytes((tk, tn), b.dtype) + _nbytes((tm, tn), out_dtype)) \
        + 3 * _nbytes((tm, tn), F32) + _nbytes((tm, tk), BF16) + _nbytes((tk, tn), BF16)
    if has_add:
        ins.append(_hbm(add))
        specs.append(o_spec)
        vm += 2 * _nbytes((tm, tn), add.dtype)
    if after is not None:
        ins.append(after)
        specs.append(ANY)
    return pl.pallas_call(
        body, name=name, grid=(m // tm, n // tn, nk), in_specs=specs, out_specs=o_spec,
        out_shape=pltpu.HBM((m, n), out_dtype), scratch_shapes=[pltpu.VMEM((tm, tn), F32)] if nk > 1 else [],
        compiler_params=_params(vm + (4 << 20)),
    )(*ins)


def _rowwise(fn, rows, consts, outs, reds=(), *, name, tr=256, temps=6):
    rows = [r if isinstance(r, tuple) else (r, r.shape[1], 0) for r in rows]
    nrows = rows[0][0].shape[0]
    t = _pick(nrows, (tr, 128, 64, 32, 16, 8))
    nr, nc, no = len(rows), len(consts), len(outs)

    def body(*refs):
        rin, cin = refs[:nr], refs[nr:nr + nc]
        oref, rref = refs[nr + nc:nr + nc + no], refs[nr + nc + no:]
        res = fn(*[r[...] for r in rin], *[c[...] for c in cin])
        if not isinstance(res, (tuple, list)):
            res = (res,)
        for o, v in zip(oref, res[:no]):
            o[...] = v.astype(o.dtype)
        if rref:
            @pl.when(pl.program_id(0) == 0)
            def _():
                for r in rref:
                    r[...] = jnp.zeros_like(r)

            for r, v in zip(rref, res[no:]):
                r[...] += v

    in_specs = [pl.BlockSpec((t, w), functools.partial(lambda i, ci: (i, ci), ci=ci)) for (_, w, ci) in rows]
    in_specs += [pl.BlockSpec(c.shape, functools.partial(lambda i, nd: (0,) * nd, nd=c.ndim)) for c in consts]
    out_specs = [pl.BlockSpec((t, w), lambda i: (i, 0)) for (w, _) in outs]
    out_specs += [pl.BlockSpec((1, w), lambda i: (0, 0)) for w in reds]
    out_shape = [pltpu.HBM((nrows, w), dt) for (w, dt) in outs]
    out_shape += [jax.ShapeDtypeStruct((1, w), F32) for w in reds]
    widest = max([w for (_, w, _) in rows] + [w for (w, _) in outs])
    vm = 2 * sum(_nbytes((t, w), a.dtype) for (a, w, _) in rows) + 2 * sum(_nbytes((t, w), dt) for (w, dt) in outs)
    vm += temps * _nbytes((t, widest), F32) + (2 << 20)
    res = pl.pallas_call(
        body, name=name, grid=(nrows // t,), in_specs=in_specs, out_specs=out_specs, out_shape=out_shape,
        compiler_params=_params(vm),
    )(*[_hbm(a) for (a, _, _) in rows], *consts)
    return list(res)


def _sb_tiles(s, tq, tk):
    tq = _pick(s, (tq, 256, 128))
    tk = _pick(tq, (tk, 128))
    return tq, tk, tq // tk


def _sb_fwd(zm, heads, *, name, tq=512, tk=256):
    s = zm.shape[0]
    tq, tk, nd = _sb_tiles(s, tq, tk)
    scale = SB_HD ** -0.5

    def body(q_ref, k_ref, v_ref, o_ref, lt_ref):
        i = pl.program_id(1)
        qb = q_ref[...].astype(BF16)
        r = lax.broadcasted_iota(jnp.int32, (tq, tk), 0)
        c = lax.broadcasted_iota(jnp.int32, (tq, tk), 1)
        ur = lax.broadcasted_iota(jnp.int32, (tk, tk), 0)
        uc = lax.broadcasted_iota(jnp.int32, (tk, tk), 1)
        usuf = (ur > uc).astype(BF16)

        def tile(j, carry, causal):
            acc, cl = carry
            rows = pl.ds(pl.multiple_of(j * tk, tk), tk)
            kb = k_ref[rows, :].astype(BF16)
            vb = v_ref[rows, :].astype(BF16)
            z = lax.dot_general(qb, kb, NT, preferred_element_type=F32) * scale
            lsig = -_softplus(z)
            l = lsig if causal is None else jnp.where(causal, lsig, 0.0)
            loga = z + lsig + _dot01(l, usuf) + cl
            if causal is not None:
                loga = jnp.where(causal, loga, NEG)
            a = jnp.exp(loga)
            acc = acc + lax.dot_general(a.astype(BF16), vb, NN, preferred_element_type=F32)
            return acc, cl + jnp.sum(l, axis=1, keepdims=True)

        carry = (jnp.zeros((tq, SB_HD), F32), jnp.zeros((tq, 1), F32))
        for dd in range(nd - 1, -1, -1):
            carry = tile(i * nd + dd, carry, c + dd * tk < r)
        acc, cl = lax.fori_loop(0, i * nd, lambda n, cr: tile(i * nd - 1 - n, cr, None), carry)
        o_ref[...] = acc.astype(o_ref.dtype)
        lt_ref[...] = jnp.broadcast_to(cl, (tq, LANES))

    blk = lambda off: pl.BlockSpec((s, SB_HD), functools.partial(lambda h, i, off: (0, off + h), off=off))
    return pl.pallas_call(
        body, name=name, grid=(heads, s // tq),
        in_specs=[pl.BlockSpec((tq, SB_HD), lambda h, i: (i, h)), blk(heads), blk(2 * heads)],
        out_specs=[pl.BlockSpec((tq, SB_HD), lambda h, i: (i, h)), pl.BlockSpec((tq, LANES), lambda h, i: (i, h))],
        out_shape=[pltpu.HBM((s, heads * SB_HD), BF16), pltpu.HBM((s, heads * LANES), F32)],
        compiler_params=_params(8 * s * SB_HD * 4 + 24 * tq * tk * 4 + (8 << 20)),
    )(_hbm(zm), _hbm(zm), _hbm(zm))


def _sb_bwd(zm, dy, ltot, after, heads, *, name, tq=512, tk=256):
    s = zm.shape[0]
    tq, tk, nd = _sb_tiles(s, tq, tk)
    nq = s // tq
    scale = SB_HD ** -0.5

    def body(q_ref, k_ref, v_ref, do_ref, lt_ref, after_ref, dq_ref, dk_ref, dv_ref, dka, dva):
        i = pl.program_id(1)

        @pl.when(i == 0)
        def _():
            dka[...] = jnp.zeros_like(dka)
            dva[...] = jnp.zeros_like(dva)

        qb = q_ref[...].astype(BF16)
        dob = do_ref[...].astype(BF16)
        ltot_c = lt_ref[:, 0:1]
        r = lax.broadcasted_iota(jnp.int32, (tq, tk), 0)
        c = lax.broadcasted_iota(jnp.int32, (tq, tk), 1)
        ur = lax.broadcasted_iota(jnp.int32, (tk, tk), 0)
        uc = lax.broadcasted_iota(jnp.int32, (tk, tk), 1)
        uincl = (ur <= uc).astype(BF16)
        uexcl = (ur < uc).astype(BF16)

        def tile(j, carry, causal):
            dq, cl, cg = carry
            rows = pl.ds(pl.multiple_of(j * tk, tk), tk)
            kb = k_ref[rows, :].astype(BF16)
            vb = v_ref[rows, :].astype(BF16)
            z = lax.dot_general(qb, kb, NT, preferred_element_type=F32) * scale
            lsig = -_softplus(z)
            l = lsig if causal is None else jnp.where(causal, lsig, 0.0)
            later = ltot_c - (cl + _dot01(l, uincl))
            loga = z + lsig + later
            if causal is not None:
                loga = jnp.where(causal, loga, NEG)
            a = jnp.exp(loga)
            sig = jnp.exp(z + lsig)
            g = a * lax.dot_general(dob, vb, NT, preferred_element_type=F32)
            p = cg + _dot01(g, uexcl)
            dz = g * (1.0 - sig) - p * sig
            if causal is not None:
                dz = jnp.where(causal, dz, 0.0)
            dzb = (dz * scale).astype(BF16)
            dva[rows, :] += lax.dot_general(a.astype(BF16), dob, TN, preferred_element_type=F32)
            dka[rows, :] += lax.dot_general(dzb, qb, TN, preferred_element_type=F32)
            dq = dq + lax.dot_general(dzb, kb, NN, preferred_element_type=F32)
            return dq, cl + jnp.sum(l, axis=1, keepdims=True), cg + jnp.sum(g, axis=1, keepdims=True)

        init = (jnp.zeros((tq, SB_HD), F32), jnp.zeros((tq, 1), F32), jnp.zeros((tq, 1), F32))
        carry = lax.fori_loop(0, i * nd, lambda j, cr: tile(j, cr, None), init)
        for dd in range(nd):
            carry = tile(i * nd + dd, carry, c + dd * tk < r)
        dq_ref[...] = carry[0].astype(dq_ref.dtype)

        @pl.when(i == nq - 1)
        def _():
            dk_ref[...] = dka[...].astype(dk_ref.dtype)
            dv_ref[...] = dva[...].astype(dv_ref.dtype)

    blk = lambda off: pl.BlockSpec((s, SB_HD), functools.partial(lambda h, i, off: (0, off + h), off=off))
    tile_spec = pl.BlockSpec((tq, SB_HD), lambda h, i: (i, h))
    full = pltpu.HBM((s, heads * SB_HD), BF16)
    return pl.pallas_call(
        body, name=name, grid=(heads, nq),
        in_specs=[tile_spec, blk(heads), blk(2 * heads), tile_spec, pl.BlockSpec((tq, LANES), lambda h, i: (i, h)),
                  ANY],
        out_specs=[tile_spec, blk(0), blk(0)],
        out_shape=[full, full, full],
        scratch_shapes=[pltpu.VMEM((s, SB_HD), F32), pltpu.VMEM((s, SB_HD), F32)],
        compiler_params=_params(12 * s * SB_HD * 4 + 32 * tq * tk * 4 + (8 << 20)),
    )(_hbm(zm), _hbm(zm), _hbm(zm), _hbm(dy), _hbm(ltot), after)


def _conv_taps(u, w_ref, rows_i):
    taps = []
    for j in range(CONV_W):
        sh = CONV_W - 1 - j
        if sh == 0:
            taps.append(u)
        else:
            taps.append(jnp.where(rows_i >= sh, pltpu.roll(u, sh, 0), 0.0))
    return taps


def _conv_fwd(zm, col0, width, cw, cb, *, name):
    s = zm.shape[0]
    bw = _pick(width, (LANES,))
    off = col0 // bw

    def body(u_ref, w_ref, b_ref, o_ref):
        u = u_ref[...]
        rows_i = lax.broadcasted_iota(jnp.int32, u.shape, 0)
        acc = jnp.broadcast_to(b_ref[...], u.shape)
        for j, tp in enumerate(_conv_taps(u, w_ref, rows_i)):
            acc = acc + tp * w_ref[j:j + 1, :]
        o_ref[...] = acc * _sigmoid(acc)

    return pl.pallas_call(
        body, name=name, grid=(width // bw,),
        in_specs=[pl.BlockSpec((s, bw), lambda j: (0, off + j)), pl.BlockSpec((CONV_W, bw), lambda j: (0, j)),
                  pl.BlockSpec((1, bw), lambda j: (0, j))],
        out_specs=pl.BlockSpec((s, bw), lambda j: (0, j)),
        out_shape=pltpu.HBM((s, width), F32),
        compiler_params=_params(12 * s * bw * 4 + (4 << 20)),
    )(_hbm(zm), cw, cb)


def _conv_bwd(zm, col0, width, cw, cb, dqk, *, name):
    s = zm.shape[0]
    bw = _pick(width, (LANES,))
    off = col0 // bw

    def body(u_ref, w_ref, b_ref, d_ref, du_ref, dw_ref, db_ref):
        u = u_ref[...]
        rows_i = lax.broadcasted_iota(jnp.int32, u.shape, 0)
        taps = _conv_taps(u, w_ref, rows_i)
        acc = jnp.broadcast_to(b_ref[...], u.shape)
        for j, tp in enumerate(taps):
            acc = acc + tp * w_ref[j:j + 1, :]
        sg = _sigmoid(acc)
        dc = d_ref[...] * (sg * (1.0 + acc * (1.0 - sg)))
        du = jnp.zeros_like(u)
        for j in range(CONV_W):
            sh = CONV_W - 1 - j
            if sh == 0:
                du = du + dc * w_ref[j:j + 1, :]
            else:
                du = du + jnp.where(rows_i < s - sh, pltpu.roll(dc, s - sh, 0), 0.0) * w_ref[j:j + 1, :]
            dw_ref[j:j + 1, :] = jnp.sum(dc * taps[j], axis=0, keepdims=True)
        du_ref[...] = du.astype(du_ref.dtype)
        db_ref[...] = jnp.sum(dc, axis=0, keepdims=True)

    return pl.pallas_call(
        body, name=name, grid=(width // bw,),
        in_specs=[pl.BlockSpec((s, bw), lambda j: (0, off + j)), pl.BlockSpec((CONV_W, bw), lambda j: (0, j)),
                  pl.BlockSpec((1, bw), lambda j: (0, j)), pl.BlockSpec((s, bw), lambda j: (0, j))],
        out_specs=[pl.BlockSpec((s, bw), lambda j: (0, j)), pl.BlockSpec((CONV_W, bw), lambda j: (0, j)),
                   pl.BlockSpec((1, bw), lambda j: (0, j))],
        out_shape=[pltpu.HBM((s, width), BF16), pltpu.HBM((CONV_W, width), F32),
                   pltpu.HBM((1, width), F32)],
        compiler_params=_params(20 * s * bw * 4 + (4 << 20)),
    )(_hbm(zm), cw, cb, _hbm(dqk))


def _ml_gates(gcol_ref, grow_ref):
    l = CHUNK
    r = lax.broadcasted_iota(jnp.int32, (l, l), 0)
    c = lax.broadcasted_iota(jnp.int32, (l, l), 1)
    gcol = gcol_ref[...]
    grow = grow_ref[0]
    bcol = _u01dot((c <= r).astype(BF16), gcol)
    brow = _dot01(grow, (r <= c).astype(BF16))
    return gcol, grow, bcol, brow, r >= c


def _ml_chunk(h, dh, mq_ref, mk_ref, v_ref, gates, cp, n_prev, m_prev):
    gcol, grow, bcol, brow, tri = gates
    l = CHUNK
    sl = slice(h * dh, (h + 1) * dh)
    qc = mq_ref[:, sl]
    kc = mk_ref[:, sl] * (dh ** -0.5)
    vc = v_ref[:, sl]
    i_row = grow[h:h + 1, :]
    i_col = gcol[:, h:h + 1]
    b_col = bcol[:, ML_HEADS + h:ML_HEADS + h + 1]
    b_row = brow[ML_HEADS + h:ML_HEADS + h + 1, :]
    b_end = b_col[l - 1:l, :]
    d = jnp.where(tri, b_col - b_row + i_row, -jnp.inf)
    m_inter = b_col + m_prev
    m_t = jnp.maximum(m_inter, jnp.max(d, axis=1, keepdims=True))
    w = jnp.exp(d - m_t)
    s_inter = jnp.exp(m_inter - m_t)
    qb, kb, vb = qc.astype(BF16), kc.astype(BF16), vc.astype(BF16)
    cpb = cp.astype(BF16)
    a = lax.dot_general(qb, kb, NT, preferred_element_type=F32)
    sc = a * w
    qcp = lax.dot_general(qb, cpb, NT, preferred_element_type=F32)
    qn = jnp.sum(qc * n_prev, axis=1, keepdims=True)
    num = lax.dot_general(sc.astype(BF16), vb, NN, preferred_element_type=F32) + s_inter * qcp
    den = jnp.sum(sc, axis=1, keepdims=True) + s_inter * qn
    floor = jnp.exp(-m_t)
    dnm = jnp.maximum(jnp.abs(den), floor)
    g_col = b_end - b_col + i_col
    g_row = b_end - b_row + i_row
    m_new = jnp.maximum(b_end + m_prev, jnp.max(g_row, axis=1, keepdims=True))
    decay = jnp.exp(b_end + m_prev - m_new)
    wk = jnp.exp(g_col - m_new)
    return dict(qc=qc, kc=kc, vc=vc, qb=qb, kb=kb, vb=vb, cpb=cpb, w=w, s_inter=s_inter, a=a, sc=sc, qcp=qcp, qn=qn,
                num=num, den=den, floor=floor, dnm=dnm, m_new=m_new, decay=decay, wk=wk, sl=sl)


def _ml_fwd(mqk, zm, vcol, gcol, grow, d_model, *, name):
    s = zm.shape[0]
    nc = s // CHUNK
    dh = d_model // ML_HEADS
    hh = ML_HEADS

    def body(mq_ref, mk_ref, v_ref, gcol_ref, grow_ref, h_ref, cs_ref, ns_ref, ms_ref, c_s, n_s, m_s):
        @pl.when(pl.program_id(0) == 0)
        def _():
            c_s[...] = jnp.zeros_like(c_s)
            n_s[...] = jnp.zeros_like(n_s)
            m_s[...] = jnp.zeros_like(m_s)

        gates = _ml_gates(gcol_ref, grow_ref)
        for h in range(hh):
            cp, n_prev, m_prev = c_s[h], n_s[h], m_s[h][:, 0:1]
            cs_ref[0, h] = cp
            ns_ref[0, h] = n_prev
            ms_ref[0, h] = m_s[h]
            f = _ml_chunk(h, dh, mq_ref, mk_ref, v_ref, gates, cp, n_prev, m_prev)
            h_ref[:, f["sl"]] = f["num"] / f["dnm"]
            c_s[h] = f["decay"] * cp + lax.dot_general((f["vc"] * f["wk"]).astype(BF16), f["kb"], TN,
                                                       preferred_element_type=F32)
            n_s[h] = f["decay"] * n_prev + jnp.sum(f["wk"] * f["kc"], axis=0, keepdims=True)
            m_s[h] = jnp.broadcast_to(f["m_new"], (1, LANES))

    dblk = d_model
    return pl.pallas_call(
        body, name=name, grid=(nc,),
        in_specs=[pl.BlockSpec((CHUNK, dblk), lambda c: (c, 0)), pl.BlockSpec((CHUNK, dblk), lambda c: (c, 1)),
                  pl.BlockSpec((CHUNK, dblk), lambda c: (c, vcol // dblk)),
                  pl.BlockSpec((CHUNK, LANES), lambda c: (c, 0)), pl.BlockSpec((1, 8, CHUNK), lambda c: (c, 0, 0))],
        out_specs=[pl.BlockSpec((CHUNK, dblk), lambda c: (c, 0)),
                   pl.BlockSpec((1, hh, dh, dh), lambda c: (c, 0, 0, 0)),
                   pl.BlockSpec((1, hh, 1, dh), lambda c: (c, 0, 0, 0)),
                   pl.BlockSpec((1, hh, 1, LANES), lambda c: (c, 0, 0, 0))],
        out_shape=[pltpu.HBM((s, d_model), F32), pltpu.HBM((nc, hh, dh, dh), F32),
                   pltpu.HBM((nc, hh, 1, dh), F32), pltpu.HBM((nc, hh, 1, LANES), F32)],
        scratch_shapes=[pltpu.VMEM((hh, dh, dh), F32), pltpu.VMEM((hh, 1, dh), F32), pltpu.VMEM((hh, 1, LANES), F32)],
        compiler_params=_params(8 * hh * dh * dh * 4 + (16 << 20)),
    )(_hbm(mqk), _hbm(mqk), _hbm(zm), _hbm(gcol), _hbm(grow))


def _ml_bwd(mqk, zm, vcol, gcol, grow, cs, ns, ms, dhm, d_model, *, name):
    s = zm.shape[0]
    nc = s // CHUNK
    dh = d_model // ML_HEADS
    hh = ML_HEADS
    l = CHUNK

    def body(mq_ref, mk_ref, v_ref, gcol_ref, grow_ref, cs_ref, ns_ref, ms_ref, dh_ref,
             dq_ref, dk_ref, dv_ref, dgc_ref, dgr_ref, dc_s, dn_s):
        @pl.when(pl.program_id(0) == 0)
        def _():
            dc_s[...] = jnp.zeros_like(dc_s)
            dn_s[...] = jnp.zeros_like(dn_s)

        gates = _ml_gates(gcol_ref, grow_ref)
        lane = lax.broadcasted_iota(jnp.int32, (l, LANES), 1)
        rowi = lax.broadcasted_iota(jnp.int32, (8, l), 0)
        lastrow = lax.broadcasted_iota(jnp.int32, (l, 1), 0) == l - 1
        dgc = jnp.zeros((l, LANES), F32)
        dgr = jnp.zeros((8, l), F32)
        for h in range(hh):
            cp, n_prev, m_prev = cs_ref[0, h], ns_ref[0, h], ms_ref[0, h][:, 0:1]
            f = _ml_chunk(h, dh, mq_ref, mk_ref, v_ref, gates, cp, n_prev, m_prev)
            dC, dn = dc_s[h], dn_s[h]
            dhv = dh_ref[:, f["sl"]]
            dnum = dhv / f["dnm"]
            hv = f["num"] / f["dnm"]
            ddnm = -jnp.sum(dhv * hv, axis=1, keepdims=True) / f["dnm"]
            dden = jnp.where(jnp.abs(f["den"]) >= f["floor"], ddnm * jnp.sign(f["den"]), 0.0)
            dnb = dnum.astype(BF16)
            dsc = lax.dot_general(dnb, f["vb"], NT, preferred_element_type=F32) + dden
            dvc = lax.dot_general(f["sc"].astype(BF16), dnb, TN, preferred_element_type=F32)
            ds_inter = jnp.sum(dnum * f["qcp"], axis=1, keepdims=True) + dden * f["qn"]
            sdn = (f["s_inter"] * dnum).astype(BF16)
            sdd = f["s_inter"] * dden
            da = dsc * f["w"]
            dab = da.astype(BF16)
            dqc = (lax.dot_general(dab, f["kb"], NN, preferred_element_type=F32)
                   + lax.dot_general(sdn, f["cpb"], NN, preferred_element_type=F32) + sdd * n_prev)
            dcp = f["decay"] * dC + lax.dot_general(sdn, f["qb"], TN, preferred_element_type=F32)
            dnp = f["decay"] * dn + jnp.sum(sdd * f["qc"], axis=0, keepdims=True)
            vw = (f["vc"] * f["wk"]).astype(BF16)
            dCb = dC.astype(BF16)
            dkc = (lax.dot_general(dab, f["qb"], TN, preferred_element_type=F32)
                   + lax.dot_general(vw, dCb, NN, preferred_element_type=F32) + f["wk"] * dn)
            e = lax.dot_general(f["kb"], dCb, NT, preferred_element_type=F32)
            dvc = dvc + e * f["wk"]
            dwk = jnp.sum(e * f["vc"], axis=1, keepdims=True) + jnp.sum(f["kc"] * dn, axis=1, keepdims=True)
            ddecay = jnp.sum(jnp.sum(dC * cp, axis=1, keepdims=True), axis=0, keepdims=True) \
                + jnp.sum(dn * n_prev, axis=1, keepdims=True)
            dd = dsc * f["sc"]
            dlw = dwk * f["wk"]
            db_end = jnp.sum(dlw, axis=0, keepdims=True) + ddecay * f["decay"]
            di_col = dlw
            db_col = jnp.sum(dd, axis=1, keepdims=True) + ds_inter * f["s_inter"] - dlw \
                + jnp.where(lastrow, db_end, 0.0)
            cs_dd = jnp.sum(dd, axis=0, keepdims=True)
            dgc = dgc + jnp.where(lane == h, di_col, 0.0) + jnp.where(lane == hh + h, db_col, 0.0)
            dgr = dgr + jnp.where(rowi == h, cs_dd, 0.0) - jnp.where(rowi == hh + h, cs_dd, 0.0)
            dq_ref[:, f["sl"]] = dqc
            dk_ref[:, f["sl"]] = dkc * (dh ** -0.5)
            dv_ref[:, f["sl"]] = dvc.astype(dv_ref.dtype)
            dc_s[h] = dcp
            dn_s[h] = dnp
        dgc_ref[...] = dgc
        dgr_ref[0] = dgr

    dblk = d_model
    rev = lambda c: nc - 1 - c
    return pl.pallas_call(
        body, name=name, grid=(nc,),
        in_specs=[pl.BlockSpec((l, dblk), lambda c: (rev(c), 0)), pl.BlockSpec((l, dblk), lambda c: (rev(c), 1)),
                  pl.BlockSpec((l, dblk), lambda c: (rev(c), vcol // dblk)),
                  pl.BlockSpec((l, LANES), lambda c: (rev(c), 0)), pl.BlockSpec((1, 8, l), lambda c: (rev(c), 0, 0)),
                  pl.BlockSpec((1, hh, dh, dh), lambda c: (rev(c), 0, 0, 0)),
                  pl.BlockSpec((1, hh, 1, dh), lambda c: (rev(c), 0, 0, 0)),
                  pl.BlockSpec((1, hh, 1, LANES), lambda c: (rev(c), 0, 0, 0)),
                  pl.BlockSpec((l, dblk), lambda c: (rev(c), 0))],
        out_specs=[pl.BlockSpec((l, dblk), lambda c: (rev(c), 0)), pl.BlockSpec((l, dblk), lambda c: (rev(c), 0)),
                   pl.BlockSpec((l, dblk), lambda c: (rev(c), 0)), pl.BlockSpec((l, LANES), lambda c: (rev(c), 0)),
                   pl.BlockSpec((1, 8, l), lambda c: (rev(c), 0, 0))],
        out_shape=[pltpu.HBM((s, d_model), F32), pltpu.HBM((s, d_model), F32),
                   pltpu.HBM((s, d_model), BF16), pltpu.HBM((s, LANES), F32),
                   pltpu.HBM((nc, 8, l), F32)],
        scratch_shapes=[pltpu.VMEM((hh, dh, dh), F32), pltpu.VMEM((hh, 1, dh), F32)],
        compiler_params=_params(10 * hh * dh * dh * 4 + (16 << 20)),
    )(*[_hbm(a) for a in (mqk, mqk, zm, gcol, grow, cs, ns, ms, dhm)])


def _xa_fwd(zm, qcol, kv, gq, gk, d_model, *, name, tq=256):
    s = zm.shape[0]
    nm = kv.shape[0]
    dh = d_model // X_HEADS
    tq = _pick(s, (tq, 128, 64))
    scale = dh ** -0.5

    def body(q_ref, k_ref, v_ref, gq_ref, gk_ref, o_ref):
        qn = _rms_fwd(q_ref[...], gq_ref[...])
        kn = _rms_fwd(k_ref[...], gk_ref[...])
        lg = _dot(qn, kn, NT) * scale
        lg = lg - jnp.max(lg, axis=1, keepdims=True)
        p = jnp.exp(lg)
        p = p / jnp.sum(p, axis=1, keepdims=True)
        o_ref[...] = _dot(p, v_ref[...], NN).astype(o_ref.dtype)

    return pl.pallas_call(
        body, name=name, grid=(X_HEADS, s // tq),
        in_specs=[pl.BlockSpec((tq, dh), lambda h, i: (i, qcol // dh + h)), pl.BlockSpec((nm, dh), lambda h, i: (0, h)),
                  pl.BlockSpec((nm, dh), lambda h, i: (0, X_HEADS + h)),
                  pl.BlockSpec((1, dh), lambda h, i: (0, 0)), pl.BlockSpec((1, dh), lambda h, i: (0, 0))],
        out_specs=pl.BlockSpec((tq, dh), lambda h, i: (i, h)),
        out_shape=pltpu.HBM((s, d_model), BF16),
        compiler_params=_params(32 << 20),
    )(_hbm(zm), _hbm(kv), _hbm(kv), gq, gk)


def _xa_bwd(zm, qcol, kv, gq, gk, dy, d_model, *, name, tq=256):
    s = zm.shape[0]
    nm = kv.shape[0]
    dh = d_model // X_HEADS
    tq = _pick(s, (tq, 128, 64))
    nq = s // tq
    scale = dh ** -0.5

    def body(q_ref, k_ref, v_ref, gq_ref, gk_ref, do_ref, dq_ref, dkn_ref, dv_ref, dgq_ref):
        h, i = pl.program_id(0), pl.program_id(1)

        @pl.when(i == 0)
        def _():
            dkn_ref[...] = jnp.zeros_like(dkn_ref)
            dv_ref[...] = jnp.zeros_like(dv_ref)

        @pl.when((i == 0) & (h == 0))
        def _():
            dgq_ref[...] = jnp.zeros_like(dgq_ref)

        q = q_ref[...]
        qn = _rms_fwd(q, gq_ref[...])
        kn = _rms_fwd(k_ref[...], gk_ref[...])
        lg = _dot(qn, kn, NT) * scale
        lg = lg - jnp.max(lg, axis=1, keepdims=True)
        p = jnp.exp(lg)
        p = p / jnp.sum(p, axis=1, keepdims=True)
        do = do_ref[...]
        dv_ref[...] += _dot(p, do, TN)
        dp = _dot(do, v_ref[...], NT)
        dlg = p * (dp - jnp.sum(dp * p, axis=1, keepdims=True)) * scale
        dqn = _dot(dlg, kn, NN)
        dkn_ref[...] += _dot(dlg, qn, TN)
        dq, dgq = _rms_bwd(q, gq_ref[...], dqn)
        dq_ref[...] = dq.astype(dq_ref.dtype)
        dgq_ref[...] += jnp.sum(dgq, axis=0, keepdims=True)

    return pl.pallas_call(
        body, name=name, grid=(X_HEADS, nq),
        in_specs=[pl.BlockSpec((tq, dh), lambda h, i: (i, qcol // dh + h)), pl.BlockSpec((nm, dh), lambda h, i: (0, h)),
                  pl.BlockSpec((nm, dh), lambda h, i: (0, X_HEADS + h)),
                  pl.BlockSpec((1, dh), lambda h, i: (0, 0)), pl.BlockSpec((1, dh), lambda h, i: (0, 0)),
                  pl.BlockSpec((tq, dh), lambda h, i: (i, h))],
        out_specs=[pl.BlockSpec((tq, dh), lambda h, i: (i, h)), pl.BlockSpec((nm, dh), lambda h, i: (0, h)),
                   pl.BlockSpec((nm, dh), lambda h, i: (0, h)), pl.BlockSpec((1, dh), lambda h, i: (0, 0))],
        out_shape=[pltpu.HBM((s, d_model), BF16), pltpu.HBM((nm, d_model), F32),
                   pltpu.HBM((nm, d_model), F32), pltpu.HBM((1, dh), F32)],
        compiler_params=_params(32 << 20),
    )(_hbm(zm), _hbm(kv), _hbm(kv), gq, gk, _hbm(dy))


def _place():
    return lax.axis_index("x"), lax.axis_index("y"), lax.axis_index("c")


ANY = pl.BlockSpec(memory_space=pl.ANY)


def _allgather_quarters(shards, *, name):
    n = len(shards)

    def body(*refs):
        ins, outs = refs[:n], refs[n:2 * n]
        send, recv, loc = refs[2 * n:]
        x, y, c = _place()
        chips = [(1 - x, y), (x, 1 - y), (1 - x, 1 - y)]
        local = []
        for t in range(n):
            cp = pltpu.make_async_copy(ins[t], outs[t].at[2 * x + y], loc.at[t])
            cp.start()
            local.append(cp)

        def copy(t, j, slot):
            return pltpu.make_async_remote_copy(
                src_ref=ins[t], dst_ref=outs[t].at[slot], send_sem=send.at[3 * t + j], recv_sem=recv.at[3 * t + j],
                device_id=(chips[j][0], chips[j][1], c), device_id_type=MESH)

        for t in range(n):
            for j in range(3):
                copy(t, j, 2 * x + y).start()
        for t in range(n):
            for j in range(3):
                copy(t, j, 2 * chips[j][0] + chips[j][1]).wait_recv()
        for t in range(n):
            for j in range(3):
                copy(t, j, 2 * x + y).wait_send()
        for cp in local:
            cp.wait()

    return pl.pallas_call(
        body, name=name, in_specs=[ANY] * n, out_specs=[ANY] * n,
        out_shape=[pltpu.HBM((4,) + a.shape, a.dtype) for a in shards],
        scratch_shapes=[pltpu.SemaphoreType.DMA((3 * n,)), pltpu.SemaphoreType.DMA((3 * n,)),
                        pltpu.SemaphoreType.DMA((n,))],
    )(*shards)


HBM_SPEC = pl.BlockSpec(memory_space=pltpu.HBM)
SEM_SPEC = pl.BlockSpec(memory_space=pltpu.SEMAPHORE)
EFFECT = pltpu.SideEffectType.DATAFLOW_SIDE_EFFECTING


def _split_copies(kind, srcs, lands, send, recv):
    x, y, c = _place()
    if kind == "quarters":
        peers = [(1 - x, y, c), (x, 1 - y, c), (1 - x, 1 - y, c)]
    else:
        peers = [(x ^ ((j >> 2) & 1), y ^ ((j >> 1) & 1), c ^ (j & 1)) for j in range(1, 8)]
    npeer = len(peers)
    out = []
    for t in range(len(srcs)):
        for j, (px, py, pc) in enumerate(peers):
            if kind == "quarters":
                src, mine, theirs = srcs[t], 2 * x + y, 2 * px + py
            else:
                src, mine, theirs = srcs[t].at[2 * px + py, pc], 4 * x + 2 * y + c, 4 * px + 2 * py + pc
            mk = functools.partial(
                pltpu.make_async_remote_copy, src_ref=src, send_sem=send.at[npeer * t + j],
                recv_sem=recv.at[npeer * t + j], device_id=(px, py, pc), device_id_type=MESH)
            out.append((functools.partial(mk, dst_ref=lands[t].at[mine]),
                        functools.partial(mk, dst_ref=lands[t].at[theirs])))
    return out


def _split_start(kind, srcs, land_shapes, after, *, name):
    n = len(srcs)
    ncopies = n * (3 if kind == "quarters" else 7)

    def body(*refs):
        ins, lands = refs[:n], refs[n:2 * n]
        send, recv = refs[2 * n + 1], refs[2 * n + 2]
        token = refs[-1]
        for start, _ in _split_copies(kind, ins, lands, send, recv):
            start().start()
        token[...] = jnp.zeros_like(token)

    lands = [_hbm(lax.empty(shp, a.dtype)) for shp, a in zip(land_shapes, srcs)]
    res = pl.pallas_call(
        body, name=name, in_specs=[HBM_SPEC] * (2 * n) + [ANY],
        out_specs=[SEM_SPEC, SEM_SPEC] + [HBM_SPEC] * (2 * n) + [pl.BlockSpec(memory_space=pltpu.VMEM)],
        out_shape=[pltpu.SemaphoreType.DMA((ncopies,)), pltpu.SemaphoreType.DMA((ncopies,))]
        + [pltpu.HBM(a.shape, a.dtype) for a in srcs] + [pltpu.HBM(shp, a.dtype) for shp, a in zip(land_shapes, srcs)]
        + [jax.ShapeDtypeStruct((8, LANES), F32)],
        input_output_aliases={i: 2 + i for i in range(2 * n)},
        compiler_params=pltpu.CompilerParams(has_side_effects=EFFECT),
    )(*[_hbm(a) for a in srcs], *lands, after)
    return res[0], res[1], list(res[2:2 + n]), list(res[2 + n:2 + 2 * n]), res[-1]


def _split_wait(kind, send, recv, srcs, lands, after, *, name):
    n = len(srcs)

    def body(*refs):
        ins, lnd = refs[:n], refs[n:2 * n]
        snd, rcv = refs[2 * n], refs[2 * n + 1]
        for start, arrive in _split_copies(kind, ins, lnd, snd, rcv):
            start().wait_send()
            arrive().wait_recv()

    res = pl.pallas_call(
        body, name=name, in_specs=[HBM_SPEC] * (2 * n) + [SEM_SPEC, SEM_SPEC] + [ANY] * len(after),
        out_specs=[HBM_SPEC] * (2 * n),
        out_shape=[pltpu.HBM(a.shape, a.dtype) for a in srcs] + [pltpu.HBM(a.shape, a.dtype) for a in lands],
        input_output_aliases={i: i for i in range(2 * n)},
        compiler_params=pltpu.CompilerParams(has_side_effects=EFFECT),
    )(*srcs, *lands, send, recv, *after)
    return list(res[n:])


def _sum8(parts, *, name):
    _, r, c = parts.shape
    t = _pick(r, (128, 64, 32, 16, 8))

    def body(p_ref, o_ref):
        acc = p_ref[0].astype(F32)
        for k in range(1, 8):
            acc = acc + p_ref[k].astype(F32)
        o_ref[...] = acc

    return pl.pallas_call(
        body, name=name, grid=(r // t,), in_specs=[pl.BlockSpec((8, t, c), lambda i: (0, i, 0))],
        out_specs=pl.BlockSpec((t, c), lambda i: (i, 0)), out_shape=pltpu.HBM((r, c), F32),
        compiler_params=_params(2 * 8 * t * c * 2 + 6 * t * c * 4 + (4 << 20)),
    )(_hbm(parts))


def _swap_halves(halves, *, name, chunk_bytes=512 * 1024):
    n = len(halves)
    items = []
    for t, a in enumerate(halves):
        r = a.shape[0]
        k = 1
        while _nbytes(a.shape, a.dtype) // k > chunk_bytes and r % (2 * k) == 0 and (r // (2 * k)) % 8 == 0:
            k *= 2
        items += [(t, q * (r // k), r // k) for q in range(k)]
    m = len(items)

    def body(*refs):
        ins, outs = refs[:n], refs[n:2 * n]
        sbuf, rbuf = refs[2 * n:3 * n], refs[3 * n:4 * n]
        send, recv, loc_own, loc_in, loc_out = refs[4 * n:]
        x, y, c = _place()
        local, stage = [], []
        for t in range(n):
            cp = pltpu.make_async_copy(ins[t], outs[t].at[c], loc_own.at[t])
            cp.start()
            local.append(cp)
        for q, (t, r0, nr) in enumerate(items):
            cp = pltpu.make_async_copy(ins[t].at[pl.ds(r0, nr)], sbuf[t].at[pl.ds(r0, nr)], loc_in.at[q])
            cp.start()
            stage.append(cp)

        def copy(q):
            t, r0, nr = items[q]
            return pltpu.make_async_remote_copy(
                src_ref=sbuf[t].at[pl.ds(r0, nr)], dst_ref=rbuf[t].at[pl.ds(r0, nr)], send_sem=send.at[q],
                recv_sem=recv.at[q], device_id=(x, y, 1 - c), device_id_type=MESH)

        for q in range(m):
            stage[q].wait()
            copy(q).start()
        for q, (t, r0, nr) in enumerate(items):
            copy(q).wait_recv()
            cp = pltpu.make_async_copy(rbuf[t].at[pl.ds(r0, nr)], outs[t].at[1 - c, pl.ds(r0, nr)], loc_out.at[q])
            cp.start()
            local.append(cp)
        for q in range(m):
            copy(q).wait_send()
        for cp in local:
            cp.wait()

    stage_bytes = 2 * sum(_nbytes(a.shape, a.dtype) for a in halves)
    return pl.pallas_call(
        body, name=name, in_specs=[ANY] * n, out_specs=[ANY] * n,
        out_shape=[pltpu.HBM((2,) + a.shape, a.dtype) for a in halves],
        scratch_shapes=[pltpu.VMEM(a.shape, a.dtype) for a in halves] * 2
        + [pltpu.SemaphoreType.DMA((m,)), pltpu.SemaphoreType.DMA((m,)), pltpu.SemaphoreType.DMA((n,)),
           pltpu.SemaphoreType.DMA((m,)), pltpu.SemaphoreType.DMA((m,))],
        compiler_params=_params(stage_bytes + (4 << 20)),
    )(*halves)


def _allreduce_small(p, *, name):
    r = p.shape[0]

    def body(p_ref, o_ref, buf, send, recv):
        x, y, c = _place()
        me = 4 * x + 2 * y + c
        peers = [(x ^ ((j >> 2) & 1), y ^ ((j >> 1) & 1), c ^ (j & 1)) for j in range(1, 8)]

        def copy(j, slot):
            return pltpu.make_async_remote_copy(
                src_ref=p_ref, dst_ref=buf.at[slot], send_sem=send.at[j], recv_sem=recv.at[j],
                device_id=peers[j], device_id_type=MESH)

        for j in range(7):
            copy(j, me).start()
        buf[me] = p_ref[...]
        for j in range(7):
            px, py, pc = peers[j]
            copy(j, 4 * px + 2 * py + pc).wait_recv()
        for j in range(7):
            copy(j, me).wait_send()
        acc = buf[0]
        for k in range(1, 8):
            acc = acc + buf[k]
        o_ref[...] = acc

    vspec = pl.BlockSpec(memory_space=pltpu.VMEM)
    return pl.pallas_call(
        body, name=name, in_specs=[vspec], out_specs=vspec, out_shape=jax.ShapeDtypeStruct((r, LANES), F32),
        scratch_shapes=[pltpu.VMEM((8, r, LANES), F32), pltpu.SemaphoreType.DMA((7,)), pltpu.SemaphoreType.DMA((7,))],
    )(p)


def _adamw_fn(w, g, m, v):
    m = ADAM_B1 * m + (1.0 - ADAM_B1) * g
    v = ADAM_B2 * v + (1.0 - ADAM_B2) * (g * g)
    m_hat = m / (1.0 - ADAM_B1 ** ADAM_STEP)
    v_hat = v / (1.0 - ADAM_B2 ** ADAM_STEP)
    delta = -ADAM_LR * (m_hat / (jnp.sqrt(v_hat) + ADAM_EPS) + ADAM_WD * w)
    return delta, m, v


def _adamw(w, g, m, v, *, name):
    c = w.shape[1]
    return _rowwise(_adamw_fn, [w, g, m, v], [], [(c, F32)] * 3, name=name, tr=128)


def _pack(vecs, rows):
    flat = jnp.concatenate([a.reshape(-1).astype(F32) for a in vecs])
    return jnp.pad(flat, (0, rows * LANES - flat.shape[0])).reshape(rows, LANES)


def _unpack(p, like):
    flat, out, o = p.reshape(-1), [], 0
    for a in like:
        out.append(flat[o:o + a.size].reshape(a.shape))
        o += a.size
    return out


def kernel(x, mem, g_mix, w_in, b_if, b_gate, conv_w, conv_b, ml_norm_g, g_mem, w_mem_kv, q_norm_g, k_norm_g, w_sb_proj, w_ml_proj, w_x_proj, w_out, g_mlp, w_ff1, w_ff2, loss_target, m_g_mix, m_w_in, m_b_if, m_b_gate, m_conv_w, m_conv_b, m_ml_norm_g, m_g_mem, m_w_mem_kv, m_q_norm_g, m_k_norm_g, m_w_sb_proj, m_w_ml_proj, m_w_x_proj, m_w_out, m_g_mlp, m_w_ff1, m_w_ff2, v_g_mix, v_w_in, v_b_if, v_b_gate, v_conv_w, v_conv_b, v_ml_norm_g, v_g_mem, v_w_mem_kv, v_q_norm_g, v_k_norm_g, v_w_sb_proj, v_w_ml_proj, v_w_x_proj, v_w_out, v_g_mlp, v_w_ff1, v_w_ff2):
    _, s, d = x.shape
    nm = mem.shape[1]
    n_in = 4 * w_in.shape[2]
    dff = 4 * w_ff1.shape[2]
    sbh = d // SB_HD
    hh = ML_HEADS
    dh = d // hh
    nc = s // CHUNK
    assert n_in == 11 * d + 2 * hh and d % (2 * LANES) == 0 and s % LANES == 0
    x2, mem2, tgt = x[0], mem[0], loss_target[0]

    k4 = 2 * lax.axis_index("x") + lax.axis_index("y")
    me = 2 * k4 + lax.axis_index("c")
    g_first = _allgather_quarters([w_in[0].astype(BF16), conv_w[0]], name="gather_w_in")
    later = [a[0].astype(BF16) for a in (w_mem_kv, w_sb_proj, w_ml_proj, w_x_proj, w_out, w_ff1, w_ff2)]
    gw_send, gw_recv, gw_src, gw_land, gw_token = _split_start(
        "quarters", later, [(4,) + a.shape for a in later], g_first[0], name="gather_rest_start")
    cols = lambda a: a.transpose(1, 0, 2).reshape(a.shape[1], 4 * a.shape[2])
    rws = lambda a: a.reshape(4 * a.shape[1], a.shape[2])
    w_in_f = cols(g_first[0])
    w_main = jnp.concatenate([w_in_f[:, :7 * d], w_in_f[:, 7 * d + 2 * hh:]], axis=1)
    w_if = jnp.pad(w_in_f[:, 7 * d:7 * d + 2 * hh], ((0, 0), (0, LANES - 2 * hh)))
    conv_wf = cols(g_first[1])
    b_if_p = jnp.pad(b_if, ((0, 0), (0, LANES - 2 * hh)))

    (hn,) = _rowwise(_rms_fwd, [x2], [g_mix], [(d, BF16)], name="norm_in")
    zm = _mm(hn, w_main, after=gw_token, name="proj_in")
    zif = _mm(hn, w_if, name="proj_if")
    y_sb, ltot = _sb_fwd(zm, sbh, name="sb_fwd")

    def gate_fn(z, b):
        pre = z + b
        lane = lax.broadcasted_iota(jnp.int32, pre.shape, 1)
        return jnp.where(lane < hh, pre, -_softplus(-pre))

    (gcol,) = _rowwise(gate_fn, [zif], [b_if_p], [(LANES, F32)], name="ml_gates")
    grow = gcol[:, :8].T.reshape(8, nc, CHUNK).transpose(1, 0, 2)
    mqk = _conv_fwd(zm, 3 * d, 2 * d, conv_wf, conv_b, name="conv_fwd")
    hm, cst, nst, mst = _ml_fwd(mqk, zm, 5 * d, gcol, grow, d, name="ml_fwd")

    def mlout_fn(hv, o, g):
        ys = [_rms_fwd(hv[:, k * dh:(k + 1) * dh], g[:, k * dh:(k + 1) * dh]) for k in range(hh)]
        return jnp.concatenate(ys, axis=1) * _sigmoid(o)

    (y_ml,) = _rowwise(mlout_fn, [hm, (zm, d, 6)], [ml_norm_g], [(d, BF16)], name="ml_out")
    gw_land = _split_wait("quarters", gw_send, gw_recv, gw_src, gw_land, [y_ml, y_sb], name="gather_rest_wait")
    gw = [lax.dynamic_update_index_in_dim(ld, a, k4, 0) for ld, a in zip(gw_land, later)]
    w_kv, w_sbp, w_mlp, w_xp, w_o, w_f1, w_f2 = (cols(gw[0]), rws(gw[1]), rws(gw[2]), rws(gw[3]), rws(gw[4]),
                                                 cols(gw[5]), rws(gw[6]))
    (memn,) = _rowwise(_rms_fwd, [mem2], [g_mem], [(d, BF16)], name="norm_mem")
    kv = _mm(memn, w_kv, name="proj_kv")
    y_x = _xa_fwd(zm, 7 * d, kv, q_norm_g, k_norm_g, d, name="xa_fwd")
    p_sb = _mm(y_sb, w_sbp, name="proj_sb")
    p_ml = _mm(y_ml, w_mlp, name="proj_ml")
    p_x = _mm(y_x, w_xp, name="proj_x")

    def merge_fn(a, b, c, g0, g1, g2, bg):
        return (_sigmoid(g0 + bg[:, :d]) * a + _sigmoid(g1 + bg[:, d:2 * d]) * b + _sigmoid(g2 + bg[:, 2 * d:]) * c)

    gate_cols = [(zm, d, 8), (zm, d, 9), (zm, d, 10)]
    (mixed,) = _rowwise(merge_fn, [p_sb, p_ml, p_x] + gate_cols, [b_gate], [(d, BF16)], name="merge")
    x1 = _mm(mixed, w_o, add=x2, name="proj_out")
    (h2,) = _rowwise(_rms_fwd, [x1], [g_mlp], [(d, BF16)], name="norm_mlp")
    u = _mm(h2, w_f1, name="ff1")
    (act,) = _rowwise(lambda uv: jnp.square(jnp.maximum(uv, 0.0)), [u], [], [(dff, BF16)], name="relu2", tr=128)
    yo = _mm(act, w_f2, add=x1, name="ff2")

    def loss_fn(yv, tv):
        e = yv - tv
        return e * (1.0 / d), jnp.sum(e * e, axis=0, keepdims=True) * (0.5 / d)

    dy, loss_cols = _rowwise(loss_fn, [yo, tgt], [], [(d, F32)], [d], name="loss")

    dact = _mm(dy, w_f2, tb=True, name="ff2_dx")
    dw_f2 = _mm(act, dy, ta=True, name="ff2_dw")
    (du,) = _rowwise(lambda g, uv: g * 2.0 * jnp.maximum(uv, 0.0), [dact, u], [], [(dff, BF16)], name="relu2_bwd",
                     tr=128)
    dw_f1 = _mm(h2, du, ta=True, name="ff1_dw")
    dh2 = _mm(du, w_f1, tb=True, name="ff1_dx")

    def norm_bwd_fn(xv, dyv, res, g):
        dx, dg = _rms_bwd(xv, g, dyv)
        return dx + res, jnp.sum(dg, axis=0, keepdims=True)

    dx1, dg_mlp = _rowwise(norm_bwd_fn, [x1, dh2, dy], [g_mlp], [(d, F32)], [d], name="norm_mlp_bwd")
    dmixed = _mm(dx1, w_o, tb=True, name="proj_out_dx")
    dw_o = _mm(mixed, dx1, ta=True, name="proj_out_dw")

    def merge_bwd_fn(dm, a, b, c, g0, g1, g2, bg):
        outs, dgs = [], []
        for p, g, k in ((a, g0, 0), (b, g1, 1), (c, g2, 2)):
            sg = _sigmoid(g + bg[:, k * d:(k + 1) * d])
            outs.append(dm * sg)
            dgs.append(dm * p * sg * (1.0 - sg))
        dgate = jnp.concatenate(dgs, axis=1)
        return (*outs, dgate, jnp.sum(dgate, axis=0, keepdims=True))

    dp_sb, dp_ml, dp_x, dgate, db_gate = _rowwise(
        merge_bwd_fn, [dmixed, p_sb, p_ml, p_x] + gate_cols, [b_gate], [(d, BF16)] * 3 + [(3 * d, BF16)], [3 * d],
        name="merge_bwd", tr=128)
    dw_sbp = _mm(y_sb, dp_sb, ta=True, name="proj_sb_dw")
    dw_mlp = _mm(y_ml, dp_ml, ta=True, name="proj_ml_dw")
    dw_xp = _mm(y_x, dp_x, ta=True, name="proj_x_dw")
    dy_sb = _mm(dp_sb, w_sbp, tb=True, out_dtype=BF16, name="proj_sb_dx")
    dy_ml = _mm(dp_ml, w_mlp, tb=True, name="proj_ml_dx")
    dy_x = _mm(dp_x, w_xp, tb=True, out_dtype=BF16, name="proj_x_dx")

    uncols = lambda a: a.reshape(a.shape[0], 4, a.shape[1] // 4).transpose(1, 0, 2)
    unrws = lambda a: a.reshape(4, a.shape[0] // 4, a.shape[1])
    to_parts = lambda q: q.astype(BF16).reshape(4, 2, q.shape[1] // 2, q.shape[2])
    early = [to_parts(q) for q in (unrws(dw_sbp), unrws(dw_mlp), unrws(dw_xp), unrws(dw_o), uncols(dw_f1),
                                   unrws(dw_f2))]
    ge_send, ge_recv, ge_src, ge_land, ge_token = _split_start(
        "grads", early, [(8,) + a.shape[2:] for a in early], dy_x, name="exchange_early_start")

    dsq, dsk, dsv = _sb_bwd(zm, dy_sb, ltot, ge_token, sbh, name="sb_bwd")

    def mlout_bwd_fn(dyv, hv, o, g):
        sg = _sigmoid(o)
        dn = dyv * sg
        dxs, dgs, ys = [], [], []
        for k in range(hh):
            sl = slice(k * dh, (k + 1) * dh)
            ys.append(_rms_fwd(hv[:, sl], g[:, sl]))
            dxk, dgk = _rms_bwd(hv[:, sl], g[:, sl], dn[:, sl])
            dxs.append(dxk)
            dgs.append(dgk)
        do = dyv * jnp.concatenate(ys, axis=1) * sg * (1.0 - sg)
        return jnp.concatenate(dxs, axis=1), do, jnp.sum(jnp.concatenate(dgs, axis=1), axis=0, keepdims=True)

    dhm, dmlo, dg_mln = _rowwise(mlout_bwd_fn, [dy_ml, hm, (zm, d, 6)], [ml_norm_g], [(d, F32), (d, BF16)], [d],
                                 name="ml_out_bwd")
    dmq, dmk, dmlv, dgc, dgr = _ml_bwd(mqk, zm, 5 * d, gcol, grow, cst, nst, mst, dhm, d, name="ml_bwd")
    dmqk = jnp.concatenate([dmq, dmk], axis=1)
    dmlqk, dconv_w, dconv_b = _conv_bwd(zm, 3 * d, 2 * d, conv_wf, conv_b, dmqk, name="conv_bwd")
    dgr_t = jnp.pad(dgr.transpose(1, 0, 2).reshape(8, s).T, ((0, 0), (0, LANES - 8)))

    def gate_bwd_fn(a, b, z, bias):
        tot = a + b
        r = lax.broadcasted_iota(jnp.int32, (CHUNK, CHUNK), 0)
        c = lax.broadcasted_iota(jnp.int32, (CHUNK, CHUNK), 1)
        dlf = _u01dot((c >= r).astype(BF16), tot)
        lane = lax.broadcasted_iota(jnp.int32, tot.shape, 1)
        dz = jnp.where(lane < hh, tot, jnp.where(lane < 2 * hh, dlf * _sigmoid(-(z + bias)), 0.0))
        return dz, jnp.sum(dz, axis=0, keepdims=True)

    dzif, db_if_p = _rowwise(gate_bwd_fn, [dgc, dgr_t, zif], [b_if_p], [(LANES, BF16)], [LANES], name="ml_gates_bwd",
                             tr=CHUNK)
    dxq, dkn, dxv, dg_qn = _xa_bwd(zm, 7 * d, kv, q_norm_g, k_norm_g, dy_x, d, name="xa_bwd")

    def knorm_bwd_fn(kvv, dknv, dvv, g):
        dks, dgs = [], []
        for k in range(X_HEADS):
            sl = slice(k * dh, (k + 1) * dh)
            dk, dg = _rms_bwd(kvv[:, sl], g, dknv[:, sl])
            dks.append(dk)
            dgs.append(jnp.sum(dg, axis=0, keepdims=True))
        return jnp.concatenate(dks + [dvv], axis=1), dgs[0] + dgs[1] + dgs[2] + dgs[3]

    dkv, dg_kn = _rowwise(knorm_bwd_fn, [(kv, d, 0), dkn, dxv], [k_norm_g], [(2 * d, BF16)], [dh], name="xa_knorm_bwd")
    dw_kv = _mm(memn, dkv, ta=True, name="proj_kv_dw")
    dmemn = _mm(dkv, w_kv, tb=True, name="proj_kv_dx")

    def gmem_fn(mv, dv_, g):
        _, dg = _rms_bwd(mv, g, dv_)
        return (jnp.sum(dg, axis=0, keepdims=True),)

    (dg_mem,) = _rowwise(gmem_fn, [mem2, dmemn], [g_mem], [], [d], name="norm_mem_bwd")

    dzm = jnp.concatenate([dsq, dsk, dsv, dmlqk, dmlv, dmlo, dxq, dgate], axis=1)
    dw_main = _mm(hn, dzm, ta=True, name="proj_in_dw")
    dw_if = _mm(hn, dzif, ta=True, name="proj_if_dw")
    dw_in = jnp.concatenate([dw_main[:, :7 * d], dw_if[:, :2 * hh], dw_main[:, 7 * d:]], axis=1)
    late = [to_parts(uncols(dw_in)), to_parts(uncols(dw_kv))]
    gl_send, gl_recv, gl_src, gl_land, gl_token = _split_start(
        "grads", late, [(8,) + a.shape[2:] for a in late], dw_if, name="exchange_late_start")
    dhn = _mm(dzm, w_main, tb=True, after=gl_token, name="proj_in_dx")
    dhn = _mm(dzif, w_if, tb=True, add=dhn, name="proj_if_dx")
    dx, dg_mix = _rowwise(norm_bwd_fn, [x2, dhn, dx1], [g_mix], [(d, F32)], [d], name="norm_in_bwd")

    small_g = [dg_mix, db_if_p[:, :2 * hh], db_gate, dconv_w, dconv_b, dg_mln, dg_mem, dg_qn, dg_kn, dg_mlp,
               jnp.sum(loss_cols).reshape(1, 1)]
    n_small = sum(a.size for a in small_g)
    rows = -(-n_small // (8 * LANES)) * 8
    g_small = _unpack(_allreduce_small(_pack(small_g, rows), name="allreduce_small"), small_g)
    loss = g_small[-1].reshape(())
    qw = conv_w.shape[2]
    g_conv_w = lax.dynamic_slice_in_dim(g_small[3], k4 * qw, qw, axis=1)
    g_small_w = [g_small[0], g_small[1], g_small[2], g_conv_w] + g_small[4:10]
    sm_w = [g_mix, b_if, b_gate, conv_w[0], conv_b, ml_norm_g, g_mem, q_norm_g, k_norm_g, g_mlp]
    sm_m = [m_g_mix, m_b_if, m_b_gate, m_conv_w[0], m_conv_b, m_ml_norm_g, m_g_mem, m_q_norm_g, m_k_norm_g, m_g_mlp]
    sm_v = [v_g_mix, v_b_if, v_b_gate, v_conv_w[0], v_conv_b, v_ml_norm_g, v_g_mem, v_q_norm_g, v_k_norm_g, v_g_mlp]
    n_sw = sum(a.size for a in sm_w)
    rows_w = -(-n_sw // (8 * LANES)) * 8
    sm_out = _adamw(_pack(sm_w, rows_w), _pack(g_small_w, rows_w), _pack(sm_m, rows_w), _pack(sm_v, rows_w),
                    name="adamw_small")
    sm_delta, sm_newm, sm_newv = [_unpack(p, sm_w) for p in sm_out]

    own = lambda p: lax.dynamic_index_in_dim(lax.dynamic_index_in_dim(p, k4, 0, keepdims=False),
                                             lax.axis_index("c"), 0, keepdims=False)

    def finish(tag, send, recv, src, land, parts, after, ws, ms, vs):
        land = _split_wait("grads", send, recv, src, land, after, name=f"exchange_{tag}_wait")
        got = [lax.dynamic_update_index_in_dim(ld, own(p), me, 0) for ld, p in zip(land, parts)]
        halves = [_sum8(r, name=f"sum_grads_{tag}{i}") for i, r in enumerate(got)]
        both = _swap_halves(halves, name=f"swap_halves_{tag}")
        gs = [b.reshape(2 * b.shape[1], b.shape[2]) for b in both]
        return gs, [_adamw(w[0], g, m[0], v[0], name=f"adamw_{tag}{i}")
                    for i, (w, g, m, v) in enumerate(zip(ws, gs, ms, vs))]

    g_early, out_early = finish(
        "early", ge_send, ge_recv, ge_src, ge_land, early, [dx, sm_out[0]],
        [w_sb_proj, w_ml_proj, w_x_proj, w_out, w_ff1, w_ff2],
        [m_w_sb_proj, m_w_ml_proj, m_w_x_proj, m_w_out, m_w_ff1, m_w_ff2],
        [v_w_sb_proj, v_w_ml_proj, v_w_x_proj, v_w_out, v_w_ff1, v_w_ff2])
    g_late, out_late = finish(
        "late", gl_send, gl_recv, gl_src, gl_land, late, [o[0] for o in out_early],
        [w_in, w_mem_kv], [m_w_in, m_w_mem_kv], [v_w_in, v_w_mem_kv])
    g_big = g_late + g_early
    big_out = out_late + out_early

    order = ["g_mix", "w_in", "b_if", "b_gate", "conv_w", "conv_b", "ml_norm_g", "g_mem", "w_mem_kv", "q_norm_g",
             "k_norm_g", "w_sb_proj", "w_ml_proj", "w_x_proj", "w_out", "g_mlp", "w_ff1", "w_ff2"]
    small_names = ["g_mix", "b_if", "b_gate", "conv_w", "conv_b", "ml_norm_g", "g_mem", "q_norm_g", "k_norm_g", "g_mlp"]
    big_names = ["w_in", "w_mem_kv", "w_sb_proj", "w_ml_proj", "w_x_proj", "w_out", "w_ff1", "w_ff2"]
    grads, deltas, new_m, new_v = {}, {}, {}, {}
    for i, nme in enumerate(small_names):
        shp = sm_w[i].shape if nme != "conv_w" else conv_w.shape
        grads[nme] = g_small_w[i].reshape(shp)
        deltas[nme], new_m[nme], new_v[nme] = (sm_delta[i].reshape(shp), sm_newm[i].reshape(shp),
                                               sm_newv[i].reshape(shp))
    for i, nme in enumerate(big_names):
        grads[nme] = g_big[i][None]
        deltas[nme], new_m[nme], new_v[nme] = (o[None] for o in big_out[i])
    return (loss, dx[None], *[grads[k] for k in order], *[deltas[k] for k in order], *[new_m[k] for k in order],
            *[new_v[k] for k in order])
```

```python
import functools

import jax
import jax.numpy as jnp
from jax import lax
from jax.experimental import pallas as pl
from jax.experimental.pallas import tpu as pltpu

F32 = jnp.float32
BF16 = jnp.bfloat16
MESH = pl.DeviceIdType.MESH

EPS = 1e-6
SB_HD = 128
ML_HEADS = 4
X_HEADS = 4
CHUNK = 64
CONV_W = 4
LANES = 128
ADAM_LR = 0.001
ADAM_B1 = 0.9
ADAM_B2 = 0.999
ADAM_EPS = 1e-08
ADAM_WD = 0.01
ADAM_STEP = 10
VMEM_CAP = 56 * 1024 * 1024
NEG = -1e30

NT = (((1,), (1,)), ((), ()))
NN = (((1,), (0,)), ((), ()))
TN = (((0,), (0,)), ((), ()))


def _dot(a, b, dn=NN):
    return lax.dot_general(a.astype(BF16), b.astype(BF16), dn, preferred_element_type=F32)


def _dot01(x, u, dn=NN):
    hi = x.astype(BF16)
    lo = (x - hi.astype(F32)).astype(BF16)
    return (lax.dot_general(hi, u, dn, preferred_element_type=F32)
            + lax.dot_general(lo, u, dn, preferred_element_type=F32))


def _u01dot(u, x):
    hi = x.astype(BF16)
    lo = (x - hi.astype(F32)).astype(BF16)
    return (lax.dot_general(u, hi, NN, preferred_element_type=F32)
            + lax.dot_general(u, lo, NN, preferred_element_type=F32))


def _pick(n, cands):
    for c in cands:
        if c <= n and n % c == 0:
            return c
    return n


def _nbytes(shape, dtype):
    n = 1
    for s in shape:
        n *= s
    return n * jnp.dtype(dtype).itemsize


def _params(vmem_bytes):
    return pltpu.CompilerParams(vmem_limit_bytes=int(min(VMEM_CAP, max(vmem_bytes, 16 * 1024 * 1024))))


def _hbm(a):
    return pltpu.with_memory_space_constraint(a, pltpu.HBM)


def _softplus(z):
    return jnp.maximum(z, 0.0) + jnp.log(1.0 + jnp.exp(-jnp.abs(z)))


def _sigmoid(z):
    return 1.0 / (1.0 + jnp.exp(-z))


def _rms_fwd(xv, g):
    r = lax.rsqrt(jnp.mean(xv * xv, axis=-1, keepdims=True) + EPS)
    return xv * r * g


def _rms_bwd(xv, g, dy):
    r = lax.rsqrt(jnp.mean(xv * xv, axis=-1, keepdims=True) + EPS)
    xh = xv * r
    dxh = dy * g
    dx = r * (dxh - xh * jnp.mean(dxh * xh, axis=-1, keepdims=True))
    return dx, dy * xh


def _mm(a, b, *, name, ta=False, tb=False, add=None, out_dtype=F32, bm=1024, bn=1024, bk=1024, after=None):
    m, k = (a.shape[1], a.shape[0]) if ta else a.shape
    n = b.shape[0] if tb else b.shape[1]
    tm = _pick(m, (bm, 512, 256, 128))
    tn = _pick(n, (bn, 512, 256, 128))
    tk = _pick(k, (bk, 512, 256, 128))
    nk = k // tk
    dn = (((0 if ta else 1,), (1 if tb else 0,)), ((), ()))
    has_add = add is not None

    def body(*refs):
        a_ref, b_ref = refs[:2]
        c_ref = refs[2] if has_add else None
        o_ref = refs[2 + has_add + (after is not None)]
        part = lax.dot_general(a_ref[...].astype(BF16), b_ref[...].astype(BF16), dn, preferred_element_type=F32)

        def finish(r):
            if has_add:
                r = r + c_ref[...].astype(F32)
            o_ref[...] = r.astype(out_dtype)

        if nk == 1:
            finish(part)
        else:
            acc_ref = refs[-1]
            kk = pl.program_id(2)

            @pl.when(kk == 0)
            def _():
                acc_ref[...] = part

            @pl.when(kk > 0)
            def _():
                acc_ref[...] += part

            @pl.when(kk == nk - 1)
            def _():
                finish(acc_ref[...])

    a_spec = pl.BlockSpec((tk, tm), lambda i, j, q: (q, i)) if ta else pl.BlockSpec((tm, tk), lambda i, j, q: (i, q))
    b_spec = pl.BlockSpec((tn, tk), lambda i, j, q: (j, q)) if tb else pl.BlockSpec((tk, tn), lambda i, j, q: (q, j))
    o_spec = pl.BlockSpec((tm, tn), lambda i, j, q: (i, j))
    ins, specs = [_hbm(a), _hbm(b)], [a_spec, b_spec]
    vm = 2 * (_nbytes((tm, tk), a.dtype) + _nbytes((tk, tn), b.dtype) + _nbytes((tm, tn), out_dtype)) \
        + 3 * _nbytes((tm, tn), F32) + _nbytes((tm, tk), BF16) + _nbytes((tk, tn), BF16)
    if has_add:
        ins.append(_hbm(add))
        specs.append(o_spec)
        vm += 2 * _nbytes((tm, tn), add.dtype)
    if after is not None:
        ins.append(after)
        specs.append(ANY)
    return pl.pallas_call(
        body, name=name, grid=(m // tm, n // tn, nk), in_specs=specs, out_specs=o_spec,
        out_shape=pltpu.HBM((m, n), out_dtype), scratch_shapes=[pltpu.VMEM((tm, tn), F32)] if nk > 1 else [],
        compiler_params=_params(vm + (4 << 20)),
    )(*ins)


def _rowwise(fn, rows, consts, outs, reds=(), *, name, tr=256, temps=6):
    rows = [r if isinstance(r, tuple) else (r, r.shape[1], 0) for r in rows]
    nrows = rows[0][0].shape[0]
    t = _pick(nrows, (tr, 128, 64, 32, 16, 8))
    nr, nc, no = len(rows), len(consts), len(outs)

    def body(*refs):
        rin, cin = refs[:nr], refs[nr:nr + nc]
        oref, rref = refs[nr + nc:nr + nc + no], refs[nr + nc + no:]
        res = fn(*[r[...] for r in rin], *[c[...] for c in cin])
        if not isinstance(res, (tuple, list)):
            res = (res,)
        for o, v in zip(oref, res[:no]):
            o[...] = v.astype(o.dtype)
        if rref:
            @pl.when(pl.program_id(0) == 0)
            def _():
                for r in rref:
                    r[...] = jnp.zeros_like(r)

            for r, v in zip(rref, res[no:]):
                r[...] += v

    in_specs = [pl.BlockSpec((t, w), functools.partial(lambda i, ci: (i, ci), ci=ci)) for (_, w, ci) in rows]
    in_specs += [pl.BlockSpec(c.shape, functools.partial(lambda i, nd: (0,) * nd, nd=c.ndim)) for c in consts]
    out_specs = [pl.BlockSpec((t, w), lambda i: (i, 0)) for (w, _) in outs]
    out_specs += [pl.BlockSpec((1, w), lambda i: (0, 0)) for w in reds]
    out_shape = [pltpu.HBM((nrows, w), dt) for (w, dt) in outs]
    out_shape += [jax.ShapeDtypeStruct((1, w), F32) for w in reds]
    widest = max([w for (_, w, _) in rows] + [w for (w, _) in outs])
    vm = 2 * sum(_nbytes((t, w), a.dtype) for (a, w, _) in rows) + 2 * sum(_nbytes((t, w), dt) for (w, dt) in outs)
    vm += temps * _nbytes((t, widest), F32) + (2 << 20)
    res = pl.pallas_call(
        body, name=name, grid=(nrows // t,), in_specs=in_specs, out_specs=out_specs, out_shape=out_shape,
        compiler_params=_params(vm),
    )(*[_hbm(a) for (a, _, _) in rows], *consts)
    return list(res)


def _sb_tiles(s, tq, tk):
    tq = _pick(s, (tq, 256, 128))
    tk = _pick(tq, (tk, 128))
    return tq, tk, tq // tk


def _sb_fwd(zm, heads, *, name, tq=512, tk=256):
    s = zm.shape[0]
    tq, tk, nd = _sb_tiles(s, tq, tk)
    scale = SB_HD ** -0.5

    def body(q_ref, k_ref, v_ref, o_ref, lt_ref):
        i = pl.program_id(1)
        qb = q_ref[...].astype(BF16)
        r = lax.broadcasted_iota(jnp.int32, (tq, tk), 0)
        c = lax.broadcasted_iota(jnp.int32, (tq, tk), 1)
        ur = lax.broadcasted_iota(jnp.int32, (tk, tk), 0)
        uc = lax.broadcasted_iota(jnp.int32, (tk, tk), 1)
        usuf = (ur > uc).astype(BF16)

        def tile(j, carry, causal):
            acc, cl = carry
            rows = pl.ds(pl.multiple_of(j * tk, tk), tk)
            kb = k_ref[rows, :].astype(BF16)
            vb = v_ref[rows, :].astype(BF16)
            z = lax.dot_general(qb, kb, NT, preferred_element_type=F32) * scale
            lsig = -_softplus(z)
            l = lsig if causal is None else jnp.where(causal, lsig, 0.0)
            loga = z + lsig + _dot01(l, usuf) + cl
            if causal is not None:
                loga = jnp.where(causal, loga, NEG)
            a = jnp.exp(loga)
            acc = acc + lax.dot_general(a.astype(BF16), vb, NN, preferred_element_type=F32)
            return acc, cl + jnp.sum(l, axis=1, keepdims=True)

        carry = (jnp.zeros((tq, SB_HD), F32), jnp.zeros((tq, 1), F32))
        for dd in range(nd - 1, -1, -1):
            carry = tile(i * nd + dd, carry, c + dd * tk < r)
        acc, cl = lax.fori_loop(0, i * nd, lambda n, cr: tile(i * nd - 1 - n, cr, None), carry)
        o_ref[...] = acc.astype(o_ref.dtype)
        lt_ref[...] = jnp.broadcast_to(cl, (tq, LANES))

    blk = lambda off: pl.BlockSpec((s, SB_HD), functools.partial(lambda h, i, off: (0, off + h), off=off))
    return pl.pallas_call(
        body, name=name, grid=(heads, s // tq),
        in_specs=[pl.BlockSpec((tq, SB_HD), lambda h, i: (i, h)), blk(heads), blk(2 * heads)],
        out_specs=[pl.BlockSpec((tq, SB_HD), lambda h, i: (i, h)), pl.BlockSpec((tq, LANES), lambda h, i: (i, h))],
        out_shape=[pltpu.HBM((s, heads * SB_HD), BF16), pltpu.HBM((s, heads * LANES), F32)],
        compiler_params=_params(8 * s * SB_HD * 4 + 24 * tq * tk * 4 + (8 << 20)),
    )(_hbm(zm), _hbm(zm), _hbm(zm))


def _sb_bwd(zm, dy, ltot, after, heads, *, name, tq=512, tk=256):
    s = zm.shape[0]
    tq, tk, nd = _sb_tiles(s, tq, tk)
    nq = s // tq
    scale = SB_HD ** -0.5

    def body(q_ref, k_ref, v_ref, do_ref, lt_ref, after_ref, dq_ref, dk_ref, dv_ref, dka, dva):
        i = pl.program_id(1)

        @pl.when(i == 0)
        def _():
            dka[...] = jnp.zeros_like(dka)
            dva[...] = jnp.zeros_like(dva)

        qb = q_ref[...].astype(BF16)
        dob = do_ref[...].astype(BF16)
        ltot_c = lt_ref[:, 0:1]
        r = lax.broadcasted_iota(jnp.int32, (tq, tk), 0)
        c = lax.broadcasted_iota(jnp.int32, (tq, tk), 1)
        ur = lax.broadcasted_iota(jnp.int32, (tk, tk), 0)
        uc = lax.broadcasted_iota(jnp.int32, (tk, tk), 1)
        uincl = (ur <= uc).astype(BF16)
        uexcl = (ur < uc).astype(BF16)

        def tile(j, carry, causal):
            dq, cl, cg = carry
            rows = pl.ds(pl.multiple_of(j * tk, tk), tk)
            kb = k_ref[rows, :].astype(BF16)
            vb = v_ref[rows, :].astype(BF16)
            z = lax.dot_general(qb, kb, NT, preferred_element_type=F32) * scale
            lsig = -_softplus(z)
            l = lsig if causal is None else jnp.where(causal, lsig, 0.0)
            later = ltot_c - (cl + _dot01(l, uincl))
            loga = z + lsig + later
            if causal is not None:
                loga = jnp.where(causal, loga, NEG)
            a = jnp.exp(loga)
            sig = jnp.exp(z + lsig)
            g = a * lax.dot_general(dob, vb, NT, preferred_element_type=F32)
            p = cg + _dot01(g, uexcl)
            dz = g * (1.0 - sig) - p * sig
            if causal is not None:
                dz = jnp.where(causal, dz, 0.0)
            dzb = (dz * scale).astype(BF16)
            dva[rows, :] += lax.dot_general(a.astype(BF16), dob, TN, preferred_element_type=F32)
            dka[rows, :] += lax.dot_general(dzb, qb, TN, preferred_element_type=F32)
            dq = dq + lax.dot_general(dzb, kb, NN, preferred_element_type=F32)
            return dq, cl + jnp.sum(l, axis=1, keepdims=True), cg + jnp.sum(g, axis=1, keepdims=True)

        init = (jnp.zeros((tq, SB_HD), F32), jnp.zeros((tq, 1), F32), jnp.zeros((tq, 1), F32))
        carry = lax.fori_loop(0, i * nd, lambda j, cr: tile(j, cr, None), init)
        for dd in range(nd):
            carry = tile(i * nd + dd, carry, c + dd * tk < r)
        dq_ref[...] = carry[0].astype(dq_ref.dtype)

        @pl.when(i == nq - 1)
        def _():
            dk_ref[...] = dka[...].astype(dk_ref.dtype)
            dv_ref[...] = dva[...].astype(dv_ref.dtype)

    blk = lambda off: pl.BlockSpec((s, SB_HD), functools.partial(lambda h, i, off: (0, off + h), off=off))
    tile_spec = pl.BlockSpec((tq, SB_HD), lambda h, i: (i, h))
    full = pltpu.HBM((s, heads * SB_HD), BF16)
    return pl.pallas_call(
        body, name=name, grid=(heads, nq),
        in_specs=[tile_spec, blk(heads), blk(2 * heads), tile_spec, pl.BlockSpec((tq, LANES), lambda h, i: (i, h)),
                  ANY],
        out_specs=[tile_spec, blk(0), blk(0)],
        out_shape=[full, full, full],
        scratch_shapes=[pltpu.VMEM((s, SB_HD), F32), pltpu.VMEM((s, SB_HD), F32)],
        compiler_params=_params(12 * s * SB_HD * 4 + 32 * tq * tk * 4 + (8 << 20)),
    )(_hbm(zm), _hbm(zm), _hbm(zm), _hbm(dy), _hbm(ltot), after)


def _conv_taps(u, w_ref, rows_i):
    taps = []
    for j in range(CONV_W):
        sh = CONV_W - 1 - j
        if sh == 0:
            taps.append(u)
        else:
            taps.append(jnp.where(rows_i >= sh, pltpu.roll(u, sh, 0), 0.0))
    return taps


def _conv_fwd(zm, col0, width, cw, cb, *, name):
    s = zm.shape[0]
    bw = _pick(width, (LANES,))
    off = col0 // bw

    def body(u_ref, w_ref, b_ref, o_ref):
        u = u_ref[...]
        rows_i = lax.broadcasted_iota(jnp.int32, u.shape, 0)
        acc = jnp.broadcast_to(b_ref[...], u.shape)
        for j, tp in enumerate(_conv_taps(u, w_ref, rows_i)):
            acc = acc + tp * w_ref[j:j + 1, :]
        o_ref[...] = acc * _sigmoid(acc)

    return pl.pallas_call(
        body, name=name, grid=(width // bw,),
        in_specs=[pl.BlockSpec((s, bw), lambda j: (0, off + j)), pl.BlockSpec((CONV_W, bw), lambda j: (0, j)),
                  pl.BlockSpec((1, bw), lambda j: (0, j))],
        out_specs=pl.BlockSpec((s, bw), lambda j: (0, j)),
        out_shape=pltpu.HBM((s, width), F32),
        compiler_params=_params(12 * s * bw * 4 + (4 << 20)),
    )(_hbm(zm), cw, cb)


def _conv_bwd(zm, col0, width, cw, cb, dqk, *, name):
    s = zm.shape[0]
    bw = _pick(width, (LANES,))
    off = col0 // bw

    def body(u_ref, w_ref, b_ref, d_ref, du_ref, dw_ref, db_ref):
        u = u_ref[...]
        rows_i = lax.broadcasted_iota(jnp.int32, u.shape, 0)
        taps = _conv_taps(u, w_ref, rows_i)
        acc = jnp.broadcast_to(b_ref[...], u.shape)
        for j, tp in enumerate(taps):
            acc = acc + tp * w_ref[j:j + 1, :]
        sg = _sigmoid(acc)
        dc = d_ref[...] * (sg * (1.0 + acc * (1.0 - sg)))
        du = jnp.zeros_like(u)
        for j in range(CONV_W):
            sh = CONV_W - 1 - j
            if sh == 0:
                du = du + dc * w_ref[j:j + 1, :]
            else:
                du = du + jnp.where(rows_i < s - sh, pltpu.roll(dc, s - sh, 0), 0.0) * w_ref[j:j + 1, :]
            dw_ref[j:j + 1, :] = jnp.sum(dc * taps[j], axis=0, keepdims=True)
        du_ref[...] = du.astype(du_ref.dtype)
        db_ref[...] = jnp.sum(dc, axis=0, keepdims=True)

    return pl.pallas_call(
        body, name=name, grid=(width // bw,),
        in_specs=[pl.BlockSpec((s, bw), lambda j: (0, off + j)), pl.BlockSpec((CONV_W, bw), lambda j: (0, j)),
                  pl.BlockSpec((1, bw), lambda j: (0, j)), pl.BlockSpec((s, bw), lambda j: (0, j))],
        out_specs=[pl.BlockSpec((s, bw), lambda j: (0, j)), pl.BlockSpec((CONV_W, bw), lambda j: (0, j)),
                   pl.BlockSpec((1, bw), lambda j: (0, j))],
        out_shape=[pltpu.HBM((s, width), BF16), pltpu.HBM((CONV_W, width), F32),
                   pltpu.HBM((1, width), F32)],
        compiler_params=_params(20 * s * bw * 4 + (4 << 20)),
    )(_hbm(zm), cw, cb, _hbm(dqk))


def _ml_gates(gcol_ref, grow_ref):
    l = CHUNK
    r = lax.broadcasted_iota(jnp.int32, (l, l), 0)
    c = lax.broadcasted_iota(jnp.int32, (l, l), 1)
    gcol = gcol_ref[...]
    grow = grow_ref[0]
    bcol = _u01dot((c <= r).astype(BF16), gcol)
    brow = _dot01(grow, (r <= c).astype(BF16))
    return gcol, grow, bcol, brow, r >= c


def _ml_chunk(h, dh, mq_ref, mk_ref, v_ref, gates, cp, n_prev, m_prev):
    gcol, grow, bcol, brow, tri = gates
    l = CHUNK
    sl = slice(h * dh, (h + 1) * dh)
    qc = mq_ref[:, sl]
    kc = mk_ref[:, sl] * (dh ** -0.5)
    vc = v_ref[:, sl]
    i_row = grow[h:h + 1, :]
    i_col = gcol[:, h:h + 1]
    b_col = bcol[:, ML_HEADS + h:ML_HEADS + h + 1]
    b_row = brow[ML_HEADS + h:ML_HEADS + h + 1, :]
    b_end = b_col[l - 1:l, :]
    d = jnp.where(tri, b_col - b_row + i_row, -jnp.inf)
    m_inter = b_col + m_prev
    m_t = jnp.maximum(m_inter, jnp.max(d, axis=1, keepdims=True))
    w = jnp.exp(d - m_t)
    s_inter = jnp.exp(m_inter - m_t)
    qb, kb, vb = qc.astype(BF16), kc.astype(BF16), vc.astype(BF16)
    cpb = cp.astype(BF16)
    a = lax.dot_general(qb, kb, NT, preferred_element_type=F32)
    sc = a * w
    qcp = lax.dot_general(qb, cpb, NT, preferred_element_type=F32)
    qn = jnp.sum(qc * n_prev, axis=1, keepdims=True)
    num = lax.dot_general(sc.astype(BF16), vb, NN, preferred_element_type=F32) + s_inter * qcp
    den = jnp.sum(sc, axis=1, keepdims=True) + s_inter * qn
    floor = jnp.exp(-m_t)
    dnm = jnp.maximum(jnp.abs(den), floor)
    g_col = b_end - b_col + i_col
    g_row = b_end - b_row + i_row
    m_new = jnp.maximum(b_end + m_prev, jnp.max(g_row, axis=1, keepdims=True))
    decay = jnp.exp(b_end + m_prev - m_new)
    wk = jnp.exp(g_col - m_new)
    return dict(qc=qc, kc=kc, vc=vc, qb=qb, kb=kb, vb=vb, cpb=cpb, w=w, s_inter=s_inter, a=a, sc=sc, qcp=qcp, qn=qn,
                num=num, den=den, floor=floor, dnm=dnm, m_new=m_new, decay=decay, wk=wk, sl=sl)


def _ml_fwd(mqk, zm, vcol, gcol, grow, d_model, *, name):
    s = zm.shape[0]
    nc = s // CHUNK
    dh = d_model // ML_HEADS
    hh = ML_HEADS

    def body(mq_ref, mk_ref, v_ref, gcol_ref, grow_ref, h_ref, cs_ref, ns_ref, ms_ref, c_s, n_s, m_s):
        @pl.when(pl.program_id(0) == 0)
        def _():
            c_s[...] = jnp.zeros_like(c_s)
            n_s[...] = jnp.zeros_like(n_s)
            m_s[...] = jnp.zeros_like(m_s)

        gates = _ml_gates(gcol_ref, grow_ref)
        for h in range(hh):
            cp, n_prev, m_prev = c_s[h], n_s[h], m_s[h][:, 0:1]
            cs_ref[0, h] = cp
            ns_ref[0, h] = n_prev
            ms_ref[0, h] = m_s[h]
            f = _ml_chunk(h, dh, mq_ref, mk_ref, v_ref, gates, cp, n_prev, m_prev)
            h_ref[:, f["sl"]] = f["num"] / f["dnm"]
            c_s[h] = f["decay"] * cp + lax.dot_general((f["vc"] * f["wk"]).astype(BF16), f["kb"], TN,
                                                       preferred_element_type=F32)
            n_s[h] = f["decay"] * n_prev + jnp.sum(f["wk"] * f["kc"], axis=0, keepdims=True)
            m_s[h] = jnp.broadcast_to(f["m_new"], (1, LANES))

    dblk = d_model
    return pl.pallas_call(
        body, name=name, grid=(nc,),
        in_specs=[pl.BlockSpec((CHUNK, dblk), lambda c: (c, 0)), pl.BlockSpec((CHUNK, dblk), lambda c: (c, 1)),
                  pl.BlockSpec((CHUNK, dblk), lambda c: (c, vcol // dblk)),
                  pl.BlockSpec((CHUNK, LANES), lambda c: (c, 0)), pl.BlockSpec((1, 8, CHUNK), lambda c: (c, 0, 0))],
        out_specs=[pl.BlockSpec((CHUNK, dblk), lambda c: (c, 0)),
                   pl.BlockSpec((1, hh, dh, dh), lambda c: (c, 0, 0, 0)),
                   pl.BlockSpec((1, hh, 1, dh), lambda c: (c, 0, 0, 0)),
                   pl.BlockSpec((1, hh, 1, LANES), lambda c: (c, 0, 0, 0))],
        out_shape=[pltpu.HBM((s, d_model), F32), pltpu.HBM((nc, hh, dh, dh), F32),
                   pltpu.HBM((nc, hh, 1, dh), F32), pltpu.HBM((nc, hh, 1, LANES), F32)],
        scratch_shapes=[pltpu.VMEM((hh, dh, dh), F32), pltpu.VMEM((hh, 1, dh), F32), pltpu.VMEM((hh, 1, LANES), F32)],
        compiler_params=_params(8 * hh * dh * dh * 4 + (16 << 20)),
    )(_hbm(mqk), _hbm(mqk), _hbm(zm), _hbm(gcol), _hbm(grow))


def _ml_bwd(mqk, zm, vcol, gcol, grow, cs, ns, ms, dhm, d_model, *, name):
    s = zm.shape[0]
    nc = s // CHUNK
    dh = d_model // ML_HEADS
    hh = ML_HEADS
    l = CHUNK

    def body(mq_ref, mk_ref, v_ref, gcol_ref, grow_ref, cs_ref, ns_ref, ms_ref, dh_ref,
             dqk_ref, dv_ref, dgc_ref, dgr_ref, dc_s, dn_s):
        @pl.when(pl.program_id(0) == 0)
        def _():
            dc_s[...] = jnp.zeros_like(dc_s)
            dn_s[...] = jnp.zeros_like(dn_s)

        gates = _ml_gates(gcol_ref, grow_ref)
        lane = lax.broadcasted_iota(jnp.int32, (l, LANES), 1)
        rowi = lax.broadcasted_iota(jnp.int32, (8, l), 0)
        lastrow = lax.broadcasted_iota(jnp.int32, (l, 1), 0) == l - 1
        dgc = jnp.zeros((l, LANES), F32)
        dgr = jnp.zeros((8, l), F32)
        for h in range(hh):
            cp, n_prev, m_prev = cs_ref[0, h], ns_ref[0, h], ms_ref[0, h][:, 0:1]
            f = _ml_chunk(h, dh, mq_ref, mk_ref, v_ref, gates, cp, n_prev, m_prev)
            dC, dn = dc_s[h], dn_s[h]
            dhv = dh_ref[:, f["sl"]]
            dnum = dhv / f["dnm"]
            hv = f["num"] / f["dnm"]
            ddnm = -jnp.sum(dhv * hv, axis=1, keepdims=True) / f["dnm"]
            dden = jnp.where(jnp.abs(f["den"]) >= f["floor"], ddnm * jnp.sign(f["den"]), 0.0)
            dnb = dnum.astype(BF16)
            dsc = lax.dot_general(dnb, f["vb"], NT, preferred_element_type=F32) + dden
            dvc = lax.dot_general(f["sc"].astype(BF16), dnb, TN, preferred_element_type=F32)
            ds_inter = jnp.sum(dnum * f["qcp"], axis=1, keepdims=True) + dden * f["qn"]
            sdn = (f["s_inter"] * dnum).astype(BF16)
            sdd = f["s_inter"] * dden
            da = dsc * f["w"]
            dab = da.astype(BF16)
            dqc = (lax.dot_general(dab, f["kb"], NN, preferred_element_type=F32)
                   + lax.dot_general(sdn, f["cpb"], NN, preferred_element_type=F32) + sdd * n_prev)
            dcp = f["decay"] * dC + lax.dot_general(sdn, f["qb"], TN, preferred_element_type=F32)
            dnp = f["decay"] * dn + jnp.sum(sdd * f["qc"], axis=0, keepdims=True)
            vw = (f["vc"] * f["wk"]).astype(BF16)
            dCb = dC.astype(BF16)
            dkc = (lax.dot_general(dab, f["qb"], TN, preferred_element_type=F32)
                   + lax.dot_general(vw, dCb, NN, preferred_element_type=F32) + f["wk"] * dn)
            e = lax.dot_general(f["kb"], dCb, NT, preferred_element_type=F32)
            dvc = dvc + e * f["wk"]
            dwk = jnp.sum(e * f["vc"], axis=1, keepdims=True) + jnp.sum(f["kc"] * dn, axis=1, keepdims=True)
            ddecay = jnp.sum(jnp.sum(dC * cp, axis=1, keepdims=True), axis=0, keepdims=True) \
                + jnp.sum(dn * n_prev, axis=1, keepdims=True)
            dd = dsc * f["sc"]
            dlw = dwk * f["wk"]
            db_end = jnp.sum(dlw, axis=0, keepdims=True) + ddecay * f["decay"]
            di_col = dlw
            db_col = jnp.sum(dd, axis=1, keepdims=True) + ds_inter * f["s_inter"] - dlw \
                + jnp.where(lastrow, db_end, 0.0)
            cs_dd = jnp.sum(dd, axis=0, keepdims=True)
            dgc = dgc + jnp.where(lane == h, di_col, 0.0) + jnp.where(lane == hh + h, db_col, 0.0)
            dgr = dgr + jnp.where(rowi == h, cs_dd, 0.0) - jnp.where(rowi == hh + h, cs_dd, 0.0)
            dqk_ref[:, f["sl"]] = dqc
            dqk_ref[:, d_model + h * dh:d_model + (h + 1) * dh] = dkc * (dh ** -0.5)
            dv_ref[:, f["sl"]] = dvc.astype(dv_ref.dtype)
            dc_s[h] = dcp
            dn_s[h] = dnp
        dgc_ref[...] = dgc
        dgr_ref[0] = dgr

    dblk = d_model
    rev = lambda c: nc - 1 - c
    return pl.pallas_call(
        body, name=name, grid=(nc,),
        in_specs=[pl.BlockSpec((l, dblk), lambda c: (rev(c), 0)), pl.BlockSpec((l, dblk), lambda c: (rev(c), 1)),
                  pl.BlockSpec((l, dblk), lambda c: (rev(c), vcol // dblk)),
                  pl.BlockSpec((l, LANES), lambda c: (rev(c), 0)), pl.BlockSpec((1, 8, l), lambda c: (rev(c), 0, 0)),
                  pl.BlockSpec((1, hh, dh, dh), lambda c: (rev(c), 0, 0, 0)),
                  pl.BlockSpec((1, hh, 1, dh), lambda c: (rev(c), 0, 0, 0)),
                  pl.BlockSpec((1, hh, 1, LANES), lambda c: (rev(c), 0, 0, 0)),
                  pl.BlockSpec((l, dblk), lambda c: (rev(c), 0))],
        out_specs=[pl.BlockSpec((l, 2 * dblk), lambda c: (rev(c), 0)),
                   pl.BlockSpec((l, dblk), lambda c: (rev(c), 0)), pl.BlockSpec((l, LANES), lambda c: (rev(c), 0)),
                   pl.BlockSpec((1, 8, l), lambda c: (rev(c), 0, 0))],
        out_shape=[pltpu.HBM((s, 2 * d_model), F32),
                   pltpu.HBM((s, d_model), BF16), pltpu.HBM((s, LANES), F32),
                   pltpu.HBM((nc, 8, l), F32)],
        scratch_shapes=[pltpu.VMEM((hh, dh, dh), F32), pltpu.VMEM((hh, 1, dh), F32)],
        compiler_params=_params(10 * hh * dh * dh * 4 + (16 << 20)),
    )(*[_hbm(a) for a in (mqk, mqk, zm, gcol, grow, cs, ns, ms, dhm)])


def _xa_fwd(zm, qcol, kv, gq, gk, d_model, *, name, tq=256):
    s = zm.shape[0]
    nm = kv.shape[0]
    dh = d_model // X_HEADS
    tq = _pick(s, (tq, 128, 64))
    scale = dh ** -0.5

    def body(q_ref, k_ref, v_ref, gq_ref, gk_ref, o_ref):
        qn = _rms_fwd(q_ref[...], gq_ref[...])
        kn = _rms_fwd(k_ref[...], gk_ref[...])
        lg = _dot(qn, kn, NT) * scale
        lg = lg - jnp.max(lg, axis=1, keepdims=True)
        p = jnp.exp(lg)
        p = p / jnp.sum(p, axis=1, keepdims=True)
        o_ref[...] = _dot(p, v_ref[...], NN).astype(o_ref.dtype)

    return pl.pallas_call(
        body, name=name, grid=(X_HEADS, s // tq),
        in_specs=[pl.BlockSpec((tq, dh), lambda h, i: (i, qcol // dh + h)), pl.BlockSpec((nm, dh), lambda h, i: (0, h)),
                  pl.BlockSpec((nm, dh), lambda h, i: (0, X_HEADS + h)),
                  pl.BlockSpec((1, dh), lambda h, i: (0, 0)), pl.BlockSpec((1, dh), lambda h, i: (0, 0))],
        out_specs=pl.BlockSpec((tq, dh), lambda h, i: (i, h)),
        out_shape=pltpu.HBM((s, d_model), BF16),
        compiler_params=_params(32 << 20),
    )(_hbm(zm), _hbm(kv), _hbm(kv), gq, gk)


def _xa_bwd(zm, qcol, kv, gq, gk, dy, d_model, *, name, tq=256):
    s = zm.shape[0]
    nm = kv.shape[0]
    dh = d_model // X_HEADS
    tq = _pick(s, (tq, 128, 64))
    nq = s // tq
    scale = dh ** -0.5

    def body(q_ref, k_ref, v_ref, gq_ref, gk_ref, do_ref, dq_ref, dkn_ref, dv_ref, dgq_ref):
        h, i = pl.program_id(0), pl.program_id(1)

        @pl.when(i == 0)
        def _():
            dkn_ref[...] = jnp.zeros_like(dkn_ref)
            dv_ref[...] = jnp.zeros_like(dv_ref)

        @pl.when((i == 0) & (h == 0))
        def _():
            dgq_ref[...] = jnp.zeros_like(dgq_ref)

        q = q_ref[...]
        qn = _rms_fwd(q, gq_ref[...])
        kn = _rms_fwd(k_ref[...], gk_ref[...])
        lg = _dot(qn, kn, NT) * scale
        lg = lg - jnp.max(lg, axis=1, keepdims=True)
        p = jnp.exp(lg)
        p = p / jnp.sum(p, axis=1, keepdims=True)
        do = do_ref[...]
        dv_ref[...] += _dot(p, do, TN)
        dp = _dot(do, v_ref[...], NT)
        dlg = p * (dp - jnp.sum(dp * p, axis=1, keepdims=True)) * scale
        dqn = _dot(dlg, kn, NN)
        dkn_ref[...] += _dot(dlg, qn, TN)
        dq, dgq = _rms_bwd(q, gq_ref[...], dqn)
        dq_ref[...] = dq.astype(dq_ref.dtype)
        dgq_ref[...] += jnp.sum(dgq, axis=0, keepdims=True)

    return pl.pallas_call(
        body, name=name, grid=(X_HEADS, nq),
        in_specs=[pl.BlockSpec((tq, dh), lambda h, i: (i, qcol // dh + h)), pl.BlockSpec((nm, dh), lambda h, i: (0, h)),
                  pl.BlockSpec((nm, dh), lambda h, i: (0, X_HEADS + h)),
                  pl.BlockSpec((1, dh), lambda h, i: (0, 0)), pl.BlockSpec((1, dh), lambda h, i: (0, 0)),
                  pl.BlockSpec((tq, dh), lambda h, i: (i, h))],
        out_specs=[pl.BlockSpec((tq, dh), lambda h, i: (i, h)), pl.BlockSpec((nm, dh), lambda h, i: (0, h)),
                   pl.BlockSpec((nm, dh), lambda h, i: (0, h)), pl.BlockSpec((1, dh), lambda h, i: (0, 0))],
        out_shape=[pltpu.HBM((s, d_model), BF16), pltpu.HBM((nm, d_model), F32),
                   pltpu.HBM((nm, d_model), F32), pltpu.HBM((1, dh), F32)],
        compiler_params=_params(32 << 20),
    )(_hbm(zm), _hbm(kv), _hbm(kv), gq, gk, _hbm(dy))


def _place():
    return lax.axis_index("x"), lax.axis_index("y"), lax.axis_index("c")


ANY = pl.BlockSpec(memory_space=pl.ANY)


def _allgather_two_level(big, small, *, name, chunk_rows=64):
    r = big.shape[0]
    half = r // 2
    nr = _pick(half, (chunk_rows, 32, 16))
    nq = half // nr

    def body(big_ref, small_ref, obig, osmall, send, recv, fsend, frecv, ssend, srecv, loc):
        x, y, c = _place()
        k = 2 * x + y
        chips = [(1 - x, y), (x, 1 - y), (1 - x, 1 - y)]
        own = [pltpu.make_async_copy(big_ref, obig.at[k], loc.at[0]),
               pltpu.make_async_copy(small_ref, osmall.at[k], loc.at[1])]
        for cp in own:
            cp.start()

        def rows(h, q):
            return pl.ds(pl.multiple_of(h * half + q * nr, nr), nr)

        def over_ici(j, q, slot, h):
            return pltpu.make_async_remote_copy(
                src_ref=big_ref.at[rows(h, q)], dst_ref=obig.at[slot, rows(h, q)], send_sem=send.at[nq * j + q],
                recv_sem=recv.at[nq * j + q], device_id=(chips[j][0], chips[j][1], c), device_id_type=MESH)

        def to_sibling(j, q, h):
            slot = 2 * chips[j][0] + chips[j][1]
            return pltpu.make_async_remote_copy(
                src_ref=obig.at[slot, rows(h, q)], dst_ref=obig.at[slot, rows(h, q)], send_sem=fsend.at[nq * j + q],
                recv_sem=frecv.at[nq * j + q], device_id=(x, y, 1 - c), device_id_type=MESH)

        def small_copy(j, slot):
            return pltpu.make_async_remote_copy(
                src_ref=small_ref, dst_ref=osmall.at[slot], send_sem=ssend.at[j], recv_sem=srecv.at[j],
                device_id=(chips[j][0], chips[j][1], c), device_id_type=MESH)

        for q in range(nq):
            for j in range(3):
                over_ici(j, q, k, c).start()
        for j in range(3):
            small_copy(j, k).start()
        for q in range(nq):
            for j in range(3):
                over_ici(j, q, 2 * chips[j][0] + chips[j][1], c).wait_recv()
                to_sibling(j, q, c).start()
        for q in range(nq):
            for j in range(3):
                to_sibling(j, q, 1 - c).wait_recv()
        for j in range(3):
            small_copy(j, 2 * chips[j][0] + chips[j][1]).wait_recv()
            small_copy(j, k).wait_send()
        for q in range(nq):
            for j in range(3):
                over_ici(j, q, k, c).wait_send()
                to_sibling(j, q, c).wait_send()
        for cp in own:
            cp.wait()

    return pl.pallas_call(
        body, name=name, in_specs=[ANY] * 2, out_specs=[ANY] * 2,
        out_shape=[pltpu.HBM((4,) + big.shape, big.dtype), pltpu.HBM((4,) + small.shape, small.dtype)],
        scratch_shapes=[pltpu.SemaphoreType.DMA((3 * nq,))] * 4
        + [pltpu.SemaphoreType.DMA((3,)), pltpu.SemaphoreType.DMA((3,)), pltpu.SemaphoreType.DMA((2,))],
    )(big, small)


HBM_SPEC = pl.BlockSpec(memory_space=pltpu.HBM)
SEM_SPEC = pl.BlockSpec(memory_space=pltpu.SEMAPHORE)
EFFECT = pltpu.SideEffectType.DATAFLOW_SIDE_EFFECTING


def _split_copies(kind, srcs, lands, send, recv):
    x, y, c = _place()
    if kind == "quarters":
        peers = [(1 - x, y, c), (x, 1 - y, c), (1 - x, 1 - y, c)]
    else:
        peers = [(x ^ ((j >> 2) & 1), y ^ ((j >> 1) & 1), c ^ (j & 1)) for j in range(1, 8)]
    npeer = len(peers)
    out = []
    for t in range(len(srcs)):
        for j, (px, py, pc) in enumerate(peers):
            if kind == "quarters":
                src, mine, theirs = srcs[t], 2 * x + y, 2 * px + py
            else:
                src, mine, theirs = srcs[t].at[2 * px + py, pc], 4 * x + 2 * y + c, 4 * px + 2 * py + pc
            mk = functools.partial(
                pltpu.make_async_remote_copy, src_ref=src, send_sem=send.at[npeer * t + j],
                recv_sem=recv.at[npeer * t + j], device_id=(px, py, pc), device_id_type=MESH)
            out.append((functools.partial(mk, dst_ref=lands[t].at[mine]),
                        functools.partial(mk, dst_ref=lands[t].at[theirs])))
    return out


def _split_start(kind, srcs, land_shapes, after, *, name):
    n = len(srcs)
    ncopies = n * (3 if kind == "quarters" else 7)

    def body(*refs):
        ins, lands = refs[:n], refs[n:2 * n]
        send, recv = refs[2 * n + 1], refs[2 * n + 2]
        token = refs[-1]
        for start, _ in _split_copies(kind, ins, lands, send, recv):
            start().start()
        token[...] = jnp.zeros_like(token)

    lands = [_hbm(lax.empty(shp, a.dtype)) for shp, a in zip(land_shapes, srcs)]
    res = pl.pallas_call(
        body, name=name, in_specs=[HBM_SPEC] * (2 * n) + [ANY],
        out_specs=[SEM_SPEC, SEM_SPEC] + [HBM_SPEC] * (2 * n) + [pl.BlockSpec(memory_space=pltpu.VMEM)],
        out_shape=[pltpu.SemaphoreType.DMA((ncopies,)), pltpu.SemaphoreType.DMA((ncopies,))]
        + [pltpu.HBM(a.shape, a.dtype) for a in srcs] + [pltpu.HBM(shp, a.dtype) for shp, a in zip(land_shapes, srcs)]
        + [jax.ShapeDtypeStruct((8, LANES), F32)],
        input_output_aliases={i: 2 + i for i in range(2 * n)},
        compiler_params=pltpu.CompilerParams(has_side_effects=EFFECT),
    )(*[_hbm(a) for a in srcs], *lands, after)
    return res[0], res[1], list(res[2:2 + n]), list(res[2 + n:2 + 2 * n]), res[-1]


def _split_wait(kind, send, recv, srcs, lands, after, *, name):
    n = len(srcs)

    def body(*refs):
        ins, lnd = refs[:n], refs[n:2 * n]
        snd, rcv = refs[2 * n], refs[2 * n + 1]
        for start, arrive in _split_copies(kind, ins, lnd, snd, rcv):
            start().wait_send()
            arrive().wait_recv()

    res = pl.pallas_call(
        body, name=name, in_specs=[HBM_SPEC] * (2 * n) + [SEM_SPEC, SEM_SPEC] + [ANY] * len(after),
        out_specs=[HBM_SPEC] * (2 * n),
        out_shape=[pltpu.HBM(a.shape, a.dtype) for a in srcs] + [pltpu.HBM(a.shape, a.dtype) for a in lands],
        input_output_aliases={i: i for i in range(2 * n)},
        compiler_params=pltpu.CompilerParams(has_side_effects=EFFECT),
    )(*srcs, *lands, send, recv, *after)
    return list(res[n:])


def _sum8(parts, *, name):
    _, r, c = parts.shape
    t = _pick(r, (128, 64, 32, 16, 8))

    def body(p_ref, o_ref):
        acc = p_ref[0].astype(F32)
        for k in range(1, 8):
            acc = acc + p_ref[k].astype(F32)
        o_ref[...] = acc

    return pl.pallas_call(
        body, name=name, grid=(r // t,), in_specs=[pl.BlockSpec((8, t, c), lambda i: (0, i, 0))],
        out_specs=pl.BlockSpec((t, c), lambda i: (i, 0)), out_shape=pltpu.HBM((r, c), F32),
        compiler_params=_params(2 * 8 * t * c * 2 + 6 * t * c * 4 + (4 << 20)),
    )(_hbm(parts))


def _swap_halves(halves, *, name, chunk_bytes=512 * 1024):
    n = len(halves)
    items = []
    for t, a in enumerate(halves):
        r = a.shape[0]
        k = 1
        while _nbytes(a.shape, a.dtype) // k > chunk_bytes and r % (2 * k) == 0 and (r // (2 * k)) % 8 == 0:
            k *= 2
        items += [(t, q * (r // k), r // k) for q in range(k)]
    m = len(items)

    def body(*refs):
        ins, outs = refs[:n], refs[n:2 * n]
        sbuf, rbuf = refs[2 * n:3 * n], refs[3 * n:4 * n]
        send, recv, loc_own, loc_in, loc_out = refs[4 * n:]
        x, y, c = _place()
        local, stage = [], []
        for t in range(n):
            cp = pltpu.make_async_copy(ins[t], outs[t].at[c], loc_own.at[t])
            cp.start()
            local.append(cp)
        for q, (t, r0, nr) in enumerate(items):
            cp = pltpu.make_async_copy(ins[t].at[pl.ds(r0, nr)], sbuf[t].at[pl.ds(r0, nr)], loc_in.at[q])
            cp.start()
            stage.append(cp)

        def copy(q):
            t, r0, nr = items[q]
            return pltpu.make_async_remote_copy(
                src_ref=sbuf[t].at[pl.ds(r0, nr)], dst_ref=rbuf[t].at[pl.ds(r0, nr)], send_sem=send.at[q],
                recv_sem=recv.at[q], device_id=(x, y, 1 - c), device_id_type=MESH)

        for q in range(m):
            stage[q].wait()
            copy(q).start()
        for q, (t, r0, nr) in enumerate(items):
            copy(q).wait_recv()
            cp = pltpu.make_async_copy(rbuf[t].at[pl.ds(r0, nr)], outs[t].at[1 - c, pl.ds(r0, nr)], loc_out.at[q])
            cp.start()
            local.append(cp)
        for q in range(m):
            copy(q).wait_send()
        for cp in local:
            cp.wait()

    stage_bytes = 2 * sum(_nbytes(a.shape, a.dtype) for a in halves)
    return pl.pallas_call(
        body, name=name, in_specs=[ANY] * n, out_specs=[ANY] * n,
        out_shape=[pltpu.HBM((2,) + a.shape, a.dtype) for a in halves],
        scratch_shapes=[pltpu.VMEM(a.shape, a.dtype) for a in halves] * 2
        + [pltpu.SemaphoreType.DMA((m,)), pltpu.SemaphoreType.DMA((m,)), pltpu.SemaphoreType.DMA((n,)),
           pltpu.SemaphoreType.DMA((m,)), pltpu.SemaphoreType.DMA((m,))],
        compiler_params=_params(stage_bytes + (4 << 20)),
    )(*halves)


def _allreduce_small(p, after, *, name):
    r = p.shape[0]

    def body(p_ref, after_ref, o_ref, buf, send, recv):
        x, y, c = _place()
        me = 4 * x + 2 * y + c
        peers = [(x ^ ((j >> 2) & 1), y ^ ((j >> 1) & 1), c ^ (j & 1)) for j in range(1, 8)]

        def copy(j, slot):
            return pltpu.make_async_remote_copy(
                src_ref=p_ref, dst_ref=buf.at[slot], send_sem=send.at[j], recv_sem=recv.at[j],
                device_id=peers[j], device_id_type=MESH)

        for j in range(7):
            copy(j, me).start()
        buf[me] = p_ref[...]
        for j in range(7):
            px, py, pc = peers[j]
            copy(j, 4 * px + 2 * py + pc).wait_recv()
        for j in range(7):
            copy(j, me).wait_send()
        acc = buf[0]
        for k in range(1, 8):
            acc = acc + buf[k]
        o_ref[...] = acc

    vspec = pl.BlockSpec(memory_space=pltpu.VMEM)
    return pl.pallas_call(
        body, name=name, in_specs=[vspec, ANY], out_specs=vspec, out_shape=jax.ShapeDtypeStruct((r, LANES), F32),
        scratch_shapes=[pltpu.VMEM((8, r, LANES), F32), pltpu.SemaphoreType.DMA((7,)), pltpu.SemaphoreType.DMA((7,))],
    )(p, after)


def _adamw_fn(w, g, m, v):
    m = ADAM_B1 * m + (1.0 - ADAM_B1) * g
    v = ADAM_B2 * v + (1.0 - ADAM_B2) * (g * g)
    m_hat = m / (1.0 - ADAM_B1 ** ADAM_STEP)
    v_hat = v / (1.0 - ADAM_B2 ** ADAM_STEP)
    delta = -ADAM_LR * (m_hat / (jnp.sqrt(v_hat) + ADAM_EPS) + ADAM_WD * w)
    return delta, m, v


def _adamw(w, g, m, v, *, name):
    c = w.shape[1]
    return _rowwise(_adamw_fn, [w, g, m, v], [], [(c, F32)] * 3, name=name, tr=128)


def _pack(vecs, rows):
    flat = jnp.concatenate([a.reshape(-1).astype(F32) for a in vecs])
    return jnp.pad(flat, (0, rows * LANES - flat.shape[0])).reshape(rows, LANES)


def _unpack(p, like):
    flat, out, o = p.reshape(-1), [], 0
    for a in like:
        out.append(flat[o:o + a.size].reshape(a.shape))
        o += a.size
    return out


def kernel(x, mem, g_mix, w_in, b_if, b_gate, conv_w, conv_b, ml_norm_g, g_mem, w_mem_kv, q_norm_g, k_norm_g, w_sb_proj, w_ml_proj, w_x_proj, w_out, g_mlp, w_ff1, w_ff2, loss_target, m_g_mix, m_w_in, m_b_if, m_b_gate, m_conv_w, m_conv_b, m_ml_norm_g, m_g_mem, m_w_mem_kv, m_q_norm_g, m_k_norm_g, m_w_sb_proj, m_w_ml_proj, m_w_x_proj, m_w_out, m_g_mlp, m_w_ff1, m_w_ff2, v_g_mix, v_w_in, v_b_if, v_b_gate, v_conv_w, v_conv_b, v_ml_norm_g, v_g_mem, v_w_mem_kv, v_q_norm_g, v_k_norm_g, v_w_sb_proj, v_w_ml_proj, v_w_x_proj, v_w_out, v_g_mlp, v_w_ff1, v_w_ff2):
    _, s, d = x.shape
    nm = mem.shape[1]
    n_in = 4 * w_in.shape[2]
    dff = 4 * w_ff1.shape[2]
    sbh = d // SB_HD
    hh = ML_HEADS
    dh = d // hh
    nc = s // CHUNK
    assert n_in == 11 * d + 2 * hh and d % (2 * LANES) == 0 and s % LANES == 0
    x2, mem2, tgt = x[0], mem[0], loss_target[0]

    k4 = 2 * lax.axis_index("x") + lax.axis_index("y")
    me = 2 * k4 + lax.axis_index("c")
    g_first = _allgather_two_level(w_in[0].astype(BF16), conv_w[0], name="gather_w_in")
    later = [a[0].astype(BF16) for a in (w_mem_kv, w_sb_proj, w_ml_proj, w_x_proj, w_out, w_ff1, w_ff2)]
    gw_send, gw_recv, gw_src, gw_land, gw_token = _split_start(
        "quarters", later, [(4,) + a.shape for a in later], g_first[0], name="gather_rest_start")
    cols = lambda a: a.transpose(1, 0, 2).reshape(a.shape[1], 4 * a.shape[2])
    rws = lambda a: a.reshape(4 * a.shape[1], a.shape[2])
    w_in_f = cols(g_first[0])
    w_main = jnp.concatenate([w_in_f[:, :7 * d], w_in_f[:, 7 * d + 2 * hh:]], axis=1)
    w_if = jnp.pad(w_in_f[:, 7 * d:7 * d + 2 * hh], ((0, 0), (0, LANES - 2 * hh)))
    conv_wf = cols(g_first[1])
    b_if_p = jnp.pad(b_if, ((0, 0), (0, LANES - 2 * hh)))

    (hn,) = _rowwise(_rms_fwd, [x2], [g_mix], [(d, BF16)], name="norm_in")
    zm = _mm(hn, w_main, after=gw_token, name="proj_in")
    zif = _mm(hn, w_if, name="proj_if")
    y_sb, ltot = _sb_fwd(zm, sbh, name="sb_fwd")

    def gate_fn(z, b):
        pre = z + b
        lane = lax.broadcasted_iota(jnp.int32, pre.shape, 1)
        return jnp.where(lane < hh, pre, -_softplus(-pre))

    (gcol,) = _rowwise(gate_fn, [zif], [b_if_p], [(LANES, F32)], name="ml_gates")
    grow = gcol[:, :8].T.reshape(8, nc, CHUNK).transpose(1, 0, 2)
    mqk = _conv_fwd(zm, 3 * d, 2 * d, conv_wf, conv_b, name="conv_fwd")
    hm, cst, nst, mst = _ml_fwd(mqk, zm, 5 * d, gcol, grow, d, name="ml_fwd")

    def mlout_fn(hv, o, g):
        ys = [_rms_fwd(hv[:, k * dh:(k + 1) * dh], g[:, k * dh:(k + 1) * dh]) for k in range(hh)]
        return jnp.concatenate(ys, axis=1) * _sigmoid(o)

    (y_ml,) = _rowwise(mlout_fn, [hm, (zm, d, 6)], [ml_norm_g], [(d, BF16)], name="ml_out")
    gw_land = _split_wait("quarters", gw_send, gw_recv, gw_src, gw_land, [y_ml, y_sb], name="gather_rest_wait")
    gw = [lax.dynamic_update_index_in_dim(ld, a, k4, 0) for ld, a in zip(gw_land, later)]
    w_kv, w_sbp, w_mlp, w_xp, w_o, w_f1, w_f2 = (cols(gw[0]), rws(gw[1]), rws(gw[2]), rws(gw[3]), rws(gw[4]),
                                                 cols(gw[5]), rws(gw[6]))
    (memn,) = _rowwise(_rms_fwd, [mem2], [g_mem], [(d, BF16)], name="norm_mem")
    kv = _mm(memn, w_kv, name="proj_kv")
    y_x = _xa_fwd(zm, 7 * d, kv, q_norm_g, k_norm_g, d, name="xa_fwd")
    p_sb = _mm(y_sb, w_sbp, name="proj_sb")
    p_ml = _mm(y_ml, w_mlp, name="proj_ml")
    p_x = _mm(y_x, w_xp, name="proj_x")

    def merge_fn(a, b, c, g0, g1, g2, bg):
        return (_sigmoid(g0 + bg[:, :d]) * a + _sigmoid(g1 + bg[:, d:2 * d]) * b + _sigmoid(g2 + bg[:, 2 * d:]) * c)

    gate_cols = [(zm, d, 8), (zm, d, 9), (zm, d, 10)]
    (mixed,) = _rowwise(merge_fn, [p_sb, p_ml, p_x] + gate_cols, [b_gate], [(d, BF16)], name="merge")
    x1 = _mm(mixed, w_o, add=x2, name="proj_out")
    (h2,) = _rowwise(_rms_fwd, [x1], [g_mlp], [(d, BF16)], name="norm_mlp")
    u = _mm(h2, w_f1, name="ff1")
    (act,) = _rowwise(lambda uv: jnp.square(jnp.maximum(uv, 0.0)), [u], [], [(dff, BF16)], name="relu2", tr=128)
    yo = _mm(act, w_f2, add=x1, name="ff2")

    def loss_fn(yv, tv):
        e = yv - tv
        return e * (1.0 / d), jnp.sum(e * e, axis=0, keepdims=True) * (0.5 / d)

    dy, loss_cols = _rowwise(loss_fn, [yo, tgt], [], [(d, F32)], [d], name="loss")

    dact = _mm(dy, w_f2, tb=True, name="ff2_dx")
    dw_f2 = _mm(act, dy, ta=True, name="ff2_dw")
    (du,) = _rowwise(lambda g, uv: g * 2.0 * jnp.maximum(uv, 0.0), [dact, u], [], [(dff, BF16)], name="relu2_bwd",
                     tr=128)
    dw_f1 = _mm(h2, du, ta=True, name="ff1_dw")
    dh2 = _mm(du, w_f1, tb=True, name="ff1_dx")

    def norm_bwd_fn(xv, dyv, res, g):
        dx, dg = _rms_bwd(xv, g, dyv)
        return dx + res, jnp.sum(dg, axis=0, keepdims=True)

    dx1, dg_mlp = _rowwise(norm_bwd_fn, [x1, dh2, dy], [g_mlp], [(d, F32)], [d], name="norm_mlp_bwd")
    dmixed = _mm(dx1, w_o, tb=True, name="proj_out_dx")
    dw_o = _mm(mixed, dx1, ta=True, name="proj_out_dw")

    def merge_bwd_fn(dm, a, b, c, g0, g1, g2, bg):
        outs, dgs = [], []
        for p, g, k in ((a, g0, 0), (b, g1, 1), (c, g2, 2)):
            sg = _sigmoid(g + bg[:, k * d:(k + 1) * d])
            outs.append(dm * sg)
            dgs.append(dm * p * sg * (1.0 - sg))
        dgate = jnp.concatenate(dgs, axis=1)
        return (*outs, dgate, jnp.sum(dgate, axis=0, keepdims=True))

    dp_sb, dp_ml, dp_x, dgate, db_gate = _rowwise(
        merge_bwd_fn, [dmixed, p_sb, p_ml, p_x] + gate_cols, [b_gate], [(d, BF16)] * 3 + [(3 * d, BF16)], [3 * d],
        name="merge_bwd", tr=128)
    dw_sbp = _mm(y_sb, dp_sb, ta=True, name="proj_sb_dw")
    dw_mlp = _mm(y_ml, dp_ml, ta=True, name="proj_ml_dw")
    dw_xp = _mm(y_x, dp_x, ta=True, name="proj_x_dw")
    dy_sb = _mm(dp_sb, w_sbp, tb=True, out_dtype=BF16, name="proj_sb_dx")
    dy_ml = _mm(dp_ml, w_mlp, tb=True, name="proj_ml_dx")
    dy_x = _mm(dp_x, w_xp, tb=True, out_dtype=BF16, name="proj_x_dx")

    uncols = lambda a: a.reshape(a.shape[0], 4, a.shape[1] // 4).transpose(1, 0, 2)
    unrws = lambda a: a.reshape(4, a.shape[0] // 4, a.shape[1])
    to_parts = lambda q: q.astype(BF16).reshape(4, 2, q.shape[1] // 2, q.shape[2])
    early = [to_parts(q) for q in (unrws(dw_sbp), unrws(dw_mlp), unrws(dw_xp), unrws(dw_o), uncols(dw_f1),
                                   unrws(dw_f2))]
    ge_send, ge_recv, ge_src, ge_land, ge_token = _split_start(
        "grads", early, [(8,) + a.shape[2:] for a in early], dy_x, name="exchange_early_start")

    dsq, dsk, dsv = _sb_bwd(zm, dy_sb, ltot, ge_token, sbh, name="sb_bwd")

    def mlout_bwd_fn(dyv, hv, o, g):
        sg = _sigmoid(o)
        dn = dyv * sg
        dxs, dgs, ys = [], [], []
        for k in range(hh):
            sl = slice(k * dh, (k + 1) * dh)
            ys.append(_rms_fwd(hv[:, sl], g[:, sl]))
            dxk, dgk = _rms_bwd(hv[:, sl], g[:, sl], dn[:, sl])
            dxs.append(dxk)
            dgs.append(dgk)
        do = dyv * jnp.concatenate(ys, axis=1) * sg * (1.0 - sg)
        return jnp.concatenate(dxs, axis=1), do, jnp.sum(jnp.concatenate(dgs, axis=1), axis=0, keepdims=True)

    dhm, dmlo, dg_mln = _rowwise(mlout_bwd_fn, [dy_ml, hm, (zm, d, 6)], [ml_norm_g], [(d, F32), (d, BF16)], [d],
                                 name="ml_out_bwd")
    dmqk, dmlv, dgc, dgr = _ml_bwd(mqk, zm, 5 * d, gcol, grow, cst, nst, mst, dhm, d, name="ml_bwd")
    dmlqk, dconv_w, dconv_b = _conv_bwd(zm, 3 * d, 2 * d, conv_wf, conv_b, dmqk, name="conv_bwd")
    dgr_t = jnp.pad(dgr.transpose(1, 0, 2).reshape(8, s).T, ((0, 0), (0, LANES - 8)))

    def gate_bwd_fn(a, b, z, bias):
        tot = a + b
        r = lax.broadcasted_iota(jnp.int32, (CHUNK, CHUNK), 0)
        c = lax.broadcasted_iota(jnp.int32, (CHUNK, CHUNK), 1)
        dlf = _u01dot((c >= r).astype(BF16), tot)
        lane = lax.broadcasted_iota(jnp.int32, tot.shape, 1)
        dz = jnp.where(lane < hh, tot, jnp.where(lane < 2 * hh, dlf * _sigmoid(-(z + bias)), 0.0))
        return dz, jnp.sum(dz, axis=0, keepdims=True)

    dzif, db_if_p = _rowwise(gate_bwd_fn, [dgc, dgr_t, zif], [b_if_p], [(LANES, BF16)], [LANES], name="ml_gates_bwd",
                             tr=CHUNK)
    dxq, dkn, dxv, dg_qn = _xa_bwd(zm, 7 * d, kv, q_norm_g, k_norm_g, dy_x, d, name="xa_bwd")

    def knorm_bwd_fn(kvv, dknv, dvv, g):
        dks, dgs = [], []
        for k in range(X_HEADS):
            sl = slice(k * dh, (k + 1) * dh)
            dk, dg = _rms_bwd(kvv[:, sl], g, dknv[:, sl])
            dks.append(dk)
            dgs.append(jnp.sum(dg, axis=0, keepdims=True))
        return jnp.concatenate(dks + [dvv], axis=1), dgs[0] + dgs[1] + dgs[2] + dgs[3]

    dkv, dg_kn = _rowwise(knorm_bwd_fn, [(kv, d, 0), dkn, dxv], [k_norm_g], [(2 * d, BF16)], [dh], name="xa_knorm_bwd")
    dw_kv = _mm(memn, dkv, ta=True, name="proj_kv_dw")
    dmemn = _mm(dkv, w_kv, tb=True, name="proj_kv_dx")

    def gmem_fn(mv, dv_, g):
        _, dg = _rms_bwd(mv, g, dv_)
        return (jnp.sum(dg, axis=0, keepdims=True),)

    (dg_mem,) = _rowwise(gmem_fn, [mem2, dmemn], [g_mem], [], [d], name="norm_mem_bwd")

    dzm = jnp.concatenate([dsq, dsk, dsv, dmlqk, dmlv, dmlo, dxq, dgate], axis=1)
    dw_main = _mm(hn, dzm, ta=True, out_dtype=BF16, name="proj_in_dw")
    dw_if = _mm(hn, dzif, ta=True, out_dtype=BF16, name="proj_if_dw")
    dw_in = jnp.concatenate([dw_main[:, :7 * d], dw_if[:, :2 * hh], dw_main[:, 7 * d:]], axis=1)
    late = [to_parts(uncols(dw_in)), to_parts(uncols(dw_kv))]
    gl_send, gl_recv, gl_src, gl_land, gl_token = _split_start(
        "grads", late, [(8,) + a.shape[2:] for a in late], dw_if, name="exchange_late_start")
    dhn = _mm(dzm, w_main, tb=True, after=gl_token, name="proj_in_dx")
    dhn = _mm(dzif, w_if, tb=True, add=dhn, name="proj_if_dx")
    dx, dg_mix = _rowwise(norm_bwd_fn, [x2, dhn, dx1], [g_mix], [(d, F32)], [d], name="norm_in_bwd")

    own = lambda p: lax.dynamic_index_in_dim(lax.dynamic_index_in_dim(p, k4, 0, keepdims=False),
                                             lax.axis_index("c"), 0, keepdims=False)

    def finish(tag, send, recv, src, land, parts, after, ws, ms, vs):
        land = _split_wait("grads", send, recv, src, land, after, name=f"exchange_{tag}_wait")
        got = [lax.dynamic_update_index_in_dim(ld, own(p), me, 0) for ld, p in zip(land, parts)]
        halves = [_sum8(r, name=f"sum_grads_{tag}{i}") for i, r in enumerate(got)]
        both = _swap_halves(halves, name=f"swap_halves_{tag}")
        gs = [b.reshape(2 * b.shape[1], b.shape[2]) for b in both]
        return gs, [_adamw(w[0], g, m[0], v[0], name=f"adamw_{tag}{i}")
                    for i, (w, g, m, v) in enumerate(zip(ws, gs, ms, vs))]

    g_early, out_early = finish(
        "early", ge_send, ge_recv, ge_src, ge_land, early, [dx],
        [w_sb_proj, w_ml_proj, w_x_proj, w_out, w_ff1, w_ff2],
        [m_w_sb_proj, m_w_ml_proj, m_w_x_proj, m_w_out, m_w_ff1, m_w_ff2],
        [v_w_sb_proj, v_w_ml_proj, v_w_x_proj, v_w_out, v_w_ff1, v_w_ff2])
    g_late, out_late = finish(
        "late", gl_send, gl_recv, gl_src, gl_land, late, [o[0] for o in out_early],
        [w_in, w_mem_kv], [m_w_in, m_w_mem_kv], [v_w_in, v_w_mem_kv])
    g_big = g_late + g_early
    big_out = out_late + out_early

    small_g = [dg_mix, db_if_p[:, :2 * hh], db_gate, dconv_w, dconv_b, dg_mln, dg_mem, dg_qn, dg_kn, dg_mlp,
               jnp.sum(loss_cols).reshape(1, 1)]
    n_small = sum(a.size for a in small_g)
    rows = -(-n_small // (8 * LANES)) * 8
    g_small = _unpack(_allreduce_small(_pack(small_g, rows), out_late[0][0], name="allreduce_small"), small_g)
    loss = g_small[-1].reshape(())
    qw = conv_w.shape[2]
    g_conv_w = lax.dynamic_slice_in_dim(g_small[3], k4 * qw, qw, axis=1)
    g_small_w = [g_small[0], g_small[1], g_small[2], g_conv_w] + g_small[4:10]
    sm_w = [g_mix, b_if, b_gate, conv_w[0], conv_b, ml_norm_g, g_mem, q_norm_g, k_norm_g, g_mlp]
    sm_m = [m_g_mix, m_b_if, m_b_gate, m_conv_w[0], m_conv_b, m_ml_norm_g, m_g_mem, m_q_norm_g, m_k_norm_g, m_g_mlp]
    sm_v = [v_g_mix, v_b_if, v_b_gate, v_conv_w[0], v_conv_b, v_ml_norm_g, v_g_mem, v_q_norm_g, v_k_norm_g, v_g_mlp]
    n_sw = sum(a.size for a in sm_w)
    rows_w = -(-n_sw // (8 * LANES)) * 8
    sm_out = _adamw(_pack(sm_w, rows_w), _pack(g_small_w, rows_w), _pack(sm_m, rows_w), _pack(sm_v, rows_w),
                    name="adamw_small")
    sm_delta, sm_newm, sm_newv = [_unpack(p, sm_w) for p in sm_out]

    order = ["g_mix", "w_in", "b_if", "b_gate", "conv_w", "conv_b", "ml_norm_g", "g_mem", "w_mem_kv", "q_norm_g",
             "k_norm_g", "w_sb_proj", "w_ml_proj", "w_x_proj", "w_out", "g_mlp", "w_ff1", "w_ff2"]
    small_names = ["g_mix", "b_if", "b_gate", "conv_w", "conv_b", "ml_norm_g", "g_mem", "q_norm_g", "k_norm_g", "g_mlp"]
    big_names = ["w_in", "w_mem_kv", "w_sb_proj", "w_ml_proj", "w_x_proj", "w_out", "w_ff1", "w_ff2"]
    grads, deltas, new_m, new_v = {}, {}, {}, {}
    for i, nme in enumerate(small_names):
        shp = sm_w[i].shape if nme != "conv_w" else conv_w.shape
        grads[nme] = g_small_w[i].reshape(shp)
        deltas[nme], new_m[nme], new_v[nme] = (sm_delta[i].reshape(shp), sm_newm[i].reshape(shp),
                                               sm_newv[i].reshape(shp))
    for i, nme in enumerate(big_names):
        grads[nme] = g_big[i][None]
        deltas[nme], new_m[nme], new_v[nme] = (o[None] for o in big_out[i])
    return (loss, dx[None], *[grads[k] for k in order], *[deltas[k] for k in order], *[new_m[k] for k in order],
            *[new_v[k] for k in order])
```

```python
import functools

import jax
import jax.numpy as jnp
from jax import lax
from jax.experimental import pallas as pl
from jax.experimental.pallas import tpu as pltpu

F32 = jnp.float32
BF16 = jnp.bfloat16
MESH = pl.DeviceIdType.MESH

EPS = 1e-6
SB_HD = 128
ML_HEADS = 4
X_HEADS = 4
CHUNK = 64
CONV_W = 4
LANES = 128
ADAM_LR = 0.001
ADAM_B1 = 0.9
ADAM_B2 = 0.999
ADAM_EPS = 1e-08
ADAM_WD = 0.01
ADAM_STEP = 10
VMEM_CAP = 56 * 1024 * 1024
NEG = -1e30

NT = (((1,), (1,)), ((), ()))
NN = (((1,), (0,)), ((), ()))
TN = (((0,), (0,)), ((), ()))


def _dot(a, b, dn=NN):
    return lax.dot_general(a.astype(BF16), b.astype(BF16), dn, preferred_element_type=F32)


def _dot01(x, u, dn=NN):
    hi = x.astype(BF16)
    lo = (x - hi.astype(F32)).astype(BF16)
    return (lax.dot_general(hi, u, dn, preferred_element_type=F32)
            + lax.dot_general(lo, u, dn, preferred_element_type=F32))


def _u01dot(u, x):
    hi = x.astype(BF16)
    lo = (x - hi.astype(F32)).astype(BF16)
    return (lax.dot_general(u, hi, NN, preferred_element_type=F32)
            + lax.dot_general(u, lo, NN, preferred_element_type=F32))


def _pick(n, cands):
    for c in cands:
        if c <= n and n % c == 0:
            return c
    return n


def _nbytes(shape, dtype):
    n = 1
    for s in shape:
        n *= s
    return n * jnp.dtype(dtype).itemsize


def _params(vmem_bytes):
    return pltpu.CompilerParams(vmem_limit_bytes=int(min(VMEM_CAP, max(vmem_bytes, 16 * 1024 * 1024))))


def _hbm(a):
    return pltpu.with_memory_space_constraint(a, pltpu.HBM)


def _softplus(z):
    return jnp.maximum(z, 0.0) + jnp.log(1.0 + jnp.exp(-jnp.abs(z)))


def _sigmoid(z):
    return 1.0 / (1.0 + jnp.exp(-z))


def _rms_fwd(xv, g):
    r = lax.rsqrt(jnp.mean(xv * xv, axis=-1, keepdims=True) + EPS)
    return xv * r * g


def _rms_bwd(xv, g, dy):
    r = lax.rsqrt(jnp.mean(xv * xv, axis=-1, keepdims=True) + EPS)
    xh = xv * r
    dxh = dy * g
    dx = r * (dxh - xh * jnp.mean(dxh * xh, axis=-1, keepdims=True))
    return dx, dy * xh


def _mm(a, b, *, name, ta=False, tb=False, add=None, out_dtype=F32, bm=1024, bn=1024, bk=1024, after=None):
    m, k = (a.shape[1], a.shape[0]) if ta else a.shape
    n = b.shape[0] if tb else b.shape[1]
    tm = _pick(m, (bm, 512, 256, 128))
    tn = _pick(n, (bn, 512, 256, 128))
    tk = _pick(k, (bk, 512, 256, 128))
    nk = k // tk
    dn = (((0 if ta else 1,), (1 if tb else 0,)), ((), ()))
    has_add = add is not None

    def body(*refs):
        a_ref, b_ref = refs[:2]
        c_ref = refs[2] if has_add else None
        o_ref = refs[2 + has_add + (after is not None)]
        part = lax.dot_general(a_ref[...].astype(BF16), b_ref[...].astype(BF16), dn, preferred_element_type=F32)

        def finish(r):
            if has_add:
                r = r + c_ref[...].astype(F32)
            o_ref[...] = r.astype(out_dtype)

        if nk == 1:
            finish(part)
        else:
            acc_ref = refs[-1]
            kk = pl.program_id(2)

            @pl.when(kk == 0)
            def _():
                acc_ref[...] = part

            @pl.when(kk > 0)
            def _():
                acc_ref[...] += part

            @pl.when(kk == nk - 1)
            def _():
                finish(acc_ref[...])

    a_spec = pl.BlockSpec((tk, tm), lambda i, j, q: (q, i)) if ta else pl.BlockSpec((tm, tk), lambda i, j, q: (i, q))
    b_spec = pl.BlockSpec((tn, tk), lambda i, j, q: (j, q)) if tb else pl.BlockSpec((tk, tn), lambda i, j, q: (q, j))
    o_spec = pl.BlockSpec((tm, tn), lambda i, j, q: (i, j))
    ins, specs = [_hbm(a), _hbm(b)], [a_spec, b_spec]
    vm = 2 * (_nbytes((tm, tk), a.dtype) + _nbytes((tk, tn), b.dtype) + _nbytes((tm, tn), out_dtype)) \
        + 3 * _nbytes((tm, tn), F32) + _nbytes((tm, tk), BF16) + _nbytes((tk, tn), BF16)
    if has_add:
        ins.append(_hbm(add))
        specs.append(o_spec)
        vm += 2 * _nbytes((tm, tn), add.dtype)
    if after is not None:
        ins.append(after)
        specs.append(ANY)
    return pl.pallas_call(
        body, name=name, grid=(m // tm, n // tn, nk), in_specs=specs, out_specs=o_spec,
        out_shape=pltpu.HBM((m, n), out_dtype), scratch_shapes=[pltpu.VMEM((tm, tn), F32)] if nk > 1 else [],
        compiler_params=_params(vm + (4 << 20)),
    )(*ins)


def _rowwise(fn, rows, consts, outs, reds=(), *, name, tr=256, temps=6):
    rows = [r if isinstance(r, tuple) else (r, r.shape[1], 0) for r in rows]
    nrows = rows[0][0].shape[0]
    t = _pick(nrows, (tr, 128, 64, 32, 16, 8))
    nr, nc, no = len(rows), len(consts), len(outs)

    def body(*refs):
        rin, cin = refs[:nr], refs[nr:nr + nc]
        oref, rref = refs[nr + nc:nr + nc + no], refs[nr + nc + no:]
        res = fn(*[r[...] for r in rin], *[c[...] for c in cin])
        if not isinstance(res, (tuple, list)):
            res = (res,)
        for o, v in zip(oref, res[:no]):
            o[...] = v.astype(o.dtype)
        if rref:
            @pl.when(pl.program_id(0) == 0)
            def _():
                for r in rref:
                    r[...] = jnp.zeros_like(r)

            for r, v in zip(rref, res[no:]):
                r[...] += v

    in_specs = [pl.BlockSpec((t, w), functools.partial(lambda i, ci: (i, ci), ci=ci)) for (_, w, ci) in rows]
    in_specs += [pl.BlockSpec(c.shape, functools.partial(lambda i, nd: (0,) * nd, nd=c.ndim)) for c in consts]
    out_specs = [pl.BlockSpec((t, w), lambda i: (i, 0)) for (w, _) in outs]
    out_specs += [pl.BlockSpec((1, w), lambda i: (0, 0)) for w in reds]
    out_shape = [pltpu.HBM((nrows, w), dt) for (w, dt) in outs]
    out_shape += [jax.ShapeDtypeStruct((1, w), F32) for w in reds]
    widest = max([w for (_, w, _) in rows] + [w for (w, _) in outs])
    vm = 2 * sum(_nbytes((t, w), a.dtype) for (a, w, _) in rows) + 2 * sum(_nbytes((t, w), dt) for (w, dt) in outs)
    vm += temps * _nbytes((t, widest), F32) + (2 << 20)
    res = pl.pallas_call(
        body, name=name, grid=(nrows // t,), in_specs=in_specs, out_specs=out_specs, out_shape=out_shape,
        compiler_params=_params(vm),
    )(*[_hbm(a) for (a, _, _) in rows], *consts)
    return list(res)


def _sb_tiles(s, tq, tk):
    tq = _pick(s, (tq, 256, 128))
    tk = _pick(tq, (tk, 128))
    return tq, tk, tq // tk


def _sb_fwd(zm, heads, *, name, tq=512, tk=256):
    s = zm.shape[0]
    tq, tk, nd = _sb_tiles(s, tq, tk)
    scale = SB_HD ** -0.5

    def body(q_ref, k_ref, v_ref, o_ref, lt_ref):
        i = pl.program_id(1)
        qb = (q_ref[...] * scale).astype(BF16)
        r = lax.broadcasted_iota(jnp.int32, (tq, tk), 0)
        c = lax.broadcasted_iota(jnp.int32, (tq, tk), 1)
        ur = lax.broadcasted_iota(jnp.int32, (tk, tk), 0)
        uc = lax.broadcasted_iota(jnp.int32, (tk, tk), 1)
        usuf = (ur > uc).astype(BF16)

        def tile(j, carry, causal):
            acc, cl = carry
            rows = pl.ds(pl.multiple_of(j * tk, tk), tk)
            kb = k_ref[rows, :].astype(BF16)
            vb = v_ref[rows, :].astype(BF16)
            z = lax.dot_general(qb, kb, NT, preferred_element_type=F32)
            lsig = -_softplus(z)
            l = lsig if causal is None else jnp.where(causal, lsig, 0.0)
            loga = z + lsig + _dot01(l, usuf) + cl
            if causal is not None:
                loga = jnp.where(causal, loga, NEG)
            a = jnp.exp(loga)
            acc = acc + lax.dot_general(a.astype(BF16), vb, NN, preferred_element_type=F32)
            return acc, cl + jnp.sum(l, axis=1, keepdims=True)

        carry = (jnp.zeros((tq, SB_HD), F32), jnp.zeros((tq, 1), F32))
        for dd in range(nd - 1, -1, -1):
            carry = tile(i * nd + dd, carry, c + dd * tk < r)
        acc, cl = lax.fori_loop(0, i * nd, lambda n, cr: tile(i * nd - 1 - n, cr, None), carry)
        o_ref[...] = acc.astype(o_ref.dtype)
        lt_ref[...] = jnp.broadcast_to(cl, (tq, LANES))

    blk = lambda off: pl.BlockSpec((s, SB_HD), functools.partial(lambda h, i, off: (0, off + h), off=off))
    return pl.pallas_call(
        body, name=name, grid=(heads, s // tq),
        in_specs=[pl.BlockSpec((tq, SB_HD), lambda h, i: (i, h)), blk(heads), blk(2 * heads)],
        out_specs=[pl.BlockSpec((tq, SB_HD), lambda h, i: (i, h)), pl.BlockSpec((tq, LANES), lambda h, i: (i, h))],
        out_shape=[pltpu.HBM((s, heads * SB_HD), BF16), pltpu.HBM((s, heads * LANES), F32)],
        compiler_params=_params(8 * s * SB_HD * 4 + 24 * tq * tk * 4 + (8 << 20)),
    )(_hbm(zm), _hbm(zm), _hbm(zm))


def _sb_bwd(zm, dy, ltot, after, heads, *, name, tq=512, tk=256):
    s = zm.shape[0]
    tq, tk, nd = _sb_tiles(s, tq, tk)
    nq = s // tq
    scale = SB_HD ** -0.5

    def body(q_ref, k_ref, v_ref, do_ref, lt_ref, after_ref, dq_ref, dk_ref, dv_ref, dka, dva):
        i = pl.program_id(1)

        @pl.when(i == 0)
        def _():
            dka[...] = jnp.zeros_like(dka)
            dva[...] = jnp.zeros_like(dva)

        qb = (q_ref[...] * scale).astype(BF16)
        dob = do_ref[...].astype(BF16)
        ltot_c = lt_ref[:, 0:1]
        r = lax.broadcasted_iota(jnp.int32, (tq, tk), 0)
        c = lax.broadcasted_iota(jnp.int32, (tq, tk), 1)
        ur = lax.broadcasted_iota(jnp.int32, (tk, tk), 0)
        uc = lax.broadcasted_iota(jnp.int32, (tk, tk), 1)
        uincl = (ur <= uc).astype(BF16)
        uexcl = (ur < uc).astype(BF16)

        def tile(j, carry, causal):
            dq, cl, cg = carry
            rows = pl.ds(pl.multiple_of(j * tk, tk), tk)
            kb = k_ref[rows, :].astype(BF16)
            vb = v_ref[rows, :].astype(BF16)
            z = lax.dot_general(qb, kb, NT, preferred_element_type=F32)
            lsig = -_softplus(z)
            l = lsig if causal is None else jnp.where(causal, lsig, 0.0)
            later = ltot_c - (cl + _dot01(l, uincl))
            loga = z + lsig + later
            if causal is not None:
                loga = jnp.where(causal, loga, NEG)
            a = jnp.exp(loga)
            sig = jnp.exp(z + lsig)
            g = a * lax.dot_general(dob, vb, NT, preferred_element_type=F32)
            p = cg + lax.dot_general(g.astype(BF16), uexcl, NN, preferred_element_type=F32)
            dz = g - sig * (g + p)
            if causal is not None:
                dz = jnp.where(causal, dz, 0.0)
            dzb = dz.astype(BF16)
            dva[rows, :] += lax.dot_general(a.astype(BF16), dob, TN, preferred_element_type=F32)
            dka[rows, :] += lax.dot_general(dzb, qb, TN, preferred_element_type=F32)
            dq = dq + lax.dot_general(dzb, kb, NN, preferred_element_type=F32)
            return dq, cl + jnp.sum(l, axis=1, keepdims=True), cg + jnp.sum(g, axis=1, keepdims=True)

        init = (jnp.zeros((tq, SB_HD), F32), jnp.zeros((tq, 1), F32), jnp.zeros((tq, 1), F32))
        carry = lax.fori_loop(0, i * nd, lambda j, cr: tile(j, cr, None), init)
        for dd in range(nd):
            carry = tile(i * nd + dd, carry, c + dd * tk < r)
        dq_ref[...] = (carry[0] * scale).astype(dq_ref.dtype)

        @pl.when(i == nq - 1)
        def _():
            dk_ref[...] = dka[...].astype(dk_ref.dtype)
            dv_ref[...] = dva[...].astype(dv_ref.dtype)

    blk = lambda off: pl.BlockSpec((s, SB_HD), functools.partial(lambda h, i, off: (0, off + h), off=off))
    tile_spec = pl.BlockSpec((tq, SB_HD), lambda h, i: (i, h))
    full = pltpu.HBM((s, heads * SB_HD), BF16)
    return pl.pallas_call(
        body, name=name, grid=(heads, nq),
        in_specs=[tile_spec, blk(heads), blk(2 * heads), tile_spec, pl.BlockSpec((tq, LANES), lambda h, i: (i, h)),
                  ANY],
        out_specs=[tile_spec, blk(0), blk(0)],
        out_shape=[full, full, full],
        scratch_shapes=[pltpu.VMEM((s, SB_HD), F32), pltpu.VMEM((s, SB_HD), F32)],
        compiler_params=_params(12 * s * SB_HD * 4 + 32 * tq * tk * 4 + (8 << 20)),
    )(_hbm(zm), _hbm(zm), _hbm(zm), _hbm(dy), _hbm(ltot), after)


def _conv_taps(u, w_ref, rows_i):
    taps = []
    for j in range(CONV_W):
        sh = CONV_W - 1 - j
        if sh == 0:
            taps.append(u)
        else:
            taps.append(jnp.where(rows_i >= sh, pltpu.roll(u, sh, 0), 0.0))
    return taps


def _conv_fwd(zm, col0, width, cw, cb, *, name):
    s = zm.shape[0]
    bw = _pick(width, (LANES,))
    off = col0 // bw

    def body(u_ref, w_ref, b_ref, o_ref):
        u = u_ref[...]
        rows_i = lax.broadcasted_iota(jnp.int32, u.shape, 0)
        acc = jnp.broadcast_to(b_ref[...], u.shape)
        for j, tp in enumerate(_conv_taps(u, w_ref, rows_i)):
            acc = acc + tp * w_ref[j:j + 1, :]
        o_ref[...] = acc * _sigmoid(acc)

    return pl.pallas_call(
        body, name=name, grid=(width // bw,),
        in_specs=[pl.BlockSpec((s, bw), lambda j: (0, off + j)), pl.BlockSpec((CONV_W, bw), lambda j: (0, j)),
                  pl.BlockSpec((1, bw), lambda j: (0, j))],
        out_specs=pl.BlockSpec((s, bw), lambda j: (0, j)),
        out_shape=pltpu.HBM((s, width), F32),
        compiler_params=_params(12 * s * bw * 4 + (4 << 20)),
    )(_hbm(zm), cw, cb)


def _conv_bwd(zm, col0, width, cw, cb, dqk, *, name):
    s = zm.shape[0]
    bw = _pick(width, (LANES,))
    off = col0 // bw

    def body(u_ref, w_ref, b_ref, d_ref, du_ref, dw_ref, db_ref):
        u = u_ref[...]
        rows_i = lax.broadcasted_iota(jnp.int32, u.shape, 0)
        taps = _conv_taps(u, w_ref, rows_i)
        acc = jnp.broadcast_to(b_ref[...], u.shape)
        for j, tp in enumerate(taps):
            acc = acc + tp * w_ref[j:j + 1, :]
        sg = _sigmoid(acc)
        dc = d_ref[...] * (sg * (1.0 + acc * (1.0 - sg)))
        du = jnp.zeros_like(u)
        for j in range(CONV_W):
            sh = CONV_W - 1 - j
            if sh == 0:
                du = du + dc * w_ref[j:j + 1, :]
            else:
                du = du + jnp.where(rows_i < s - sh, pltpu.roll(dc, s - sh, 0), 0.0) * w_ref[j:j + 1, :]
            dw_ref[j:j + 1, :] = jnp.sum(dc * taps[j], axis=0, keepdims=True)
        du_ref[...] = du.astype(du_ref.dtype)
        db_ref[...] = jnp.sum(dc, axis=0, keepdims=True)

    return pl.pallas_call(
        body, name=name, grid=(width // bw,),
        in_specs=[pl.BlockSpec((s, bw), lambda j: (0, off + j)), pl.BlockSpec((CONV_W, bw), lambda j: (0, j)),
                  pl.BlockSpec((1, bw), lambda j: (0, j)), pl.BlockSpec((s, bw), lambda j: (0, j))],
        out_specs=[pl.BlockSpec((s, bw), lambda j: (0, j)), pl.BlockSpec((CONV_W, bw), lambda j: (0, j)),
                   pl.BlockSpec((1, bw), lambda j: (0, j))],
        out_shape=[pltpu.HBM((s, width), BF16), pltpu.HBM((CONV_W, width), F32),
                   pltpu.HBM((1, width), F32)],
        compiler_params=_params(20 * s * bw * 4 + (4 << 20)),
    )(_hbm(zm), cw, cb, _hbm(dqk))


def _ml_gates(gcol_ref, grow_ref):
    l = CHUNK
    r = lax.broadcasted_iota(jnp.int32, (l, l), 0)
    c = lax.broadcasted_iota(jnp.int32, (l, l), 1)
    gcol = gcol_ref[...]
    grow = grow_ref[0]
    bcol = _u01dot((c <= r).astype(BF16), gcol)
    brow = _dot01(grow, (r <= c).astype(BF16))
    return gcol, grow, bcol, brow, r >= c


def _ml_chunk(h, dh, mq_ref, mk_ref, v_ref, gates, cp, n_prev, m_prev):
    gcol, grow, bcol, brow, tri = gates
    l = CHUNK
    sl = slice(h * dh, (h + 1) * dh)
    qc = mq_ref[:, sl]
    kc = mk_ref[:, sl] * (dh ** -0.5)
    vc = v_ref[:, sl]
    i_row = grow[h:h + 1, :]
    i_col = gcol[:, h:h + 1]
    b_col = bcol[:, ML_HEADS + h:ML_HEADS + h + 1]
    b_row = brow[ML_HEADS + h:ML_HEADS + h + 1, :]
    b_end = b_col[l - 1:l, :]
    d = jnp.where(tri, b_col - b_row + i_row, -jnp.inf)
    m_inter = b_col + m_prev
    m_t = jnp.maximum(m_inter, jnp.max(d, axis=1, keepdims=True))
    w = jnp.exp(d - m_t)
    s_inter = jnp.exp(m_inter - m_t)
    qb, kb, vb = qc.astype(BF16), kc.astype(BF16), vc.astype(BF16)
    cpb = cp.astype(BF16)
    a = lax.dot_general(qb, kb, NT, preferred_element_type=F32)
    sc = a * w
    qcp = lax.dot_general(qb, cpb, NT, preferred_element_type=F32)
    qn = jnp.sum(qc * n_prev, axis=1, keepdims=True)
    num = lax.dot_general(sc.astype(BF16), vb, NN, preferred_element_type=F32) + s_inter * qcp
    den = jnp.sum(sc, axis=1, keepdims=True) + s_inter * qn
    floor = jnp.exp(-m_t)
    dnm = jnp.maximum(jnp.abs(den), floor)
    g_col = b_end - b_col + i_col
    g_row = b_end - b_row + i_row
    m_new = jnp.maximum(b_end + m_prev, jnp.max(g_row, axis=1, keepdims=True))
    decay = jnp.exp(b_end + m_prev - m_new)
    wk = jnp.exp(g_col - m_new)
    return dict(qc=qc, kc=kc, vc=vc, qb=qb, kb=kb, vb=vb, cpb=cpb, w=w, s_inter=s_inter, a=a, sc=sc, qcp=qcp, qn=qn,
                num=num, den=den, floor=floor, dnm=dnm, m_new=m_new, decay=decay, wk=wk, sl=sl)


def _ml_fwd(mqk, zm, vcol, gcol, grow, d_model, *, name):
    s = zm.shape[0]
    nc = s // CHUNK
    dh = d_model // ML_HEADS
    hh = ML_HEADS

    def body(mq_ref, mk_ref, v_ref, gcol_ref, grow_ref, h_ref, cs_ref, ns_ref, ms_ref, c_s, n_s, m_s):
        @pl.when(pl.program_id(0) == 0)
        def _():
            c_s[...] = jnp.zeros_like(c_s)
            n_s[...] = jnp.zeros_like(n_s)
            m_s[...] = jnp.zeros_like(m_s)

        gates = _ml_gates(gcol_ref, grow_ref)
        for h in range(hh):
            cp, n_prev, m_prev = c_s[h], n_s[h], m_s[h][:, 0:1]
            cs_ref[0, h] = cp
            ns_ref[0, h] = n_prev
            ms_ref[0, h] = m_s[h]
            f = _ml_chunk(h, dh, mq_ref, mk_ref, v_ref, gates, cp, n_prev, m_prev)
            h_ref[:, f["sl"]] = f["num"] / f["dnm"]
            c_s[h] = f["decay"] * cp + lax.dot_general((f["vc"] * f["wk"]).astype(BF16), f["kb"], TN,
                                                       preferred_element_type=F32)
            n_s[h] = f["decay"] * n_prev + jnp.sum(f["wk"] * f["kc"], axis=0, keepdims=True)
            m_s[h] = jnp.broadcast_to(f["m_new"], (1, LANES))

    dblk = d_model
    return pl.pallas_call(
        body, name=name, grid=(nc,),
        in_specs=[pl.BlockSpec((CHUNK, dblk), lambda c: (c, 0)), pl.BlockSpec((CHUNK, dblk), lambda c: (c, 1)),
                  pl.BlockSpec((CHUNK, dblk), lambda c: (c, vcol // dblk)),
                  pl.BlockSpec((CHUNK, LANES), lambda c: (c, 0)), pl.BlockSpec((1, 8, CHUNK), lambda c: (c, 0, 0))],
        out_specs=[pl.BlockSpec((CHUNK, dblk), lambda c: (c, 0)),
                   pl.BlockSpec((1, hh, dh, dh), lambda c: (c, 0, 0, 0)),
                   pl.BlockSpec((1, hh, 1, dh), lambda c: (c, 0, 0, 0)),
                   pl.BlockSpec((1, hh, 1, LANES), lambda c: (c, 0, 0, 0))],
        out_shape=[pltpu.HBM((s, d_model), F32), pltpu.HBM((nc, hh, dh, dh), F32),
                   pltpu.HBM((nc, hh, 1, dh), F32), pltpu.HBM((nc, hh, 1, LANES), F32)],
        scratch_shapes=[pltpu.VMEM((hh, dh, dh), F32), pltpu.VMEM((hh, 1, dh), F32), pltpu.VMEM((hh, 1, LANES), F32)],
        compiler_params=_params(8 * hh * dh * dh * 4 + (16 << 20)),
    )(_hbm(mqk), _hbm(mqk), _hbm(zm), _hbm(gcol), _hbm(grow))


def _ml_bwd(mqk, zm, vcol, gcol, grow, cs, ns, ms, dhm, d_model, *, name):
    s = zm.shape[0]
    nc = s // CHUNK
    dh = d_model // ML_HEADS
    hh = ML_HEADS
    l = CHUNK

    def body(mq_ref, mk_ref, v_ref, gcol_ref, grow_ref, cs_ref, ns_ref, ms_ref, dh_ref,
             dqk_ref, dv_ref, dgc_ref, dgr_ref, dc_s, dn_s):
        @pl.when(pl.program_id(0) == 0)
        def _():
            dc_s[...] = jnp.zeros_like(dc_s)
            dn_s[...] = jnp.zeros_like(dn_s)

        gates = _ml_gates(gcol_ref, grow_ref)
        lane = lax.broadcasted_iota(jnp.int32, (l, LANES), 1)
        rowi = lax.broadcasted_iota(jnp.int32, (8, l), 0)
        lastrow = lax.broadcasted_iota(jnp.int32, (l, 1), 0) == l - 1
        dgc = jnp.zeros((l, LANES), F32)
        dgr = jnp.zeros((8, l), F32)
        for h in range(hh):
            cp, n_prev, m_prev = cs_ref[0, h], ns_ref[0, h], ms_ref[0, h][:, 0:1]
            f = _ml_chunk(h, dh, mq_ref, mk_ref, v_ref, gates, cp, n_prev, m_prev)
            dC, dn = dc_s[h], dn_s[h]
            dhv = dh_ref[:, f["sl"]]
            dnum = dhv / f["dnm"]
            hv = f["num"] / f["dnm"]
            ddnm = -jnp.sum(dhv * hv, axis=1, keepdims=True) / f["dnm"]
            dden = jnp.where(jnp.abs(f["den"]) >= f["floor"], ddnm * jnp.sign(f["den"]), 0.0)
            dnb = dnum.astype(BF16)
            dsc = lax.dot_general(dnb, f["vb"], NT, preferred_element_type=F32) + dden
            dvc = lax.dot_general(f["sc"].astype(BF16), dnb, TN, preferred_element_type=F32)
            ds_inter = jnp.sum(dnum * f["qcp"], axis=1, keepdims=True) + dden * f["qn"]
            sdn = (f["s_inter"] * dnum).astype(BF16)
            sdd = f["s_inter"] * dden
            da = dsc * f["w"]
            dab = da.astype(BF16)
            dqc = (lax.dot_general(dab, f["kb"], NN, preferred_element_type=F32)
                   + lax.dot_general(sdn, f["cpb"], NN, preferred_element_type=F32) + sdd * n_prev)
            dcp = f["decay"] * dC + lax.dot_general(sdn, f["qb"], TN, preferred_element_type=F32)
            dnp = f["decay"] * dn + jnp.sum(sdd * f["qc"], axis=0, keepdims=True)
            vw = (f["vc"] * f["wk"]).astype(BF16)
            dCb = dC.astype(BF16)
            dkc = (lax.dot_general(dab, f["qb"], TN, preferred_element_type=F32)
                   + lax.dot_general(vw, dCb, NN, preferred_element_type=F32) + f["wk"] * dn)
            e = lax.dot_general(f["kb"], dCb, NT, preferred_element_type=F32)
            dvc = dvc + e * f["wk"]
            dwk = jnp.sum(e * f["vc"], axis=1, keepdims=True) + jnp.sum(f["kc"] * dn, axis=1, keepdims=True)
            ddecay = jnp.sum(jnp.sum(dC * cp, axis=1, keepdims=True), axis=0, keepdims=True) \
                + jnp.sum(dn * n_prev, axis=1, keepdims=True)
            dd = dsc * f["sc"]
            dlw = dwk * f["wk"]
            db_end = jnp.sum(dlw, axis=0, keepdims=True) + ddecay * f["decay"]
            di_col = dlw
            db_col = jnp.sum(dd, axis=1, keepdims=True) + ds_inter * f["s_inter"] - dlw \
                + jnp.where(lastrow, db_end, 0.0)
            cs_dd = jnp.sum(dd, axis=0, keepdims=True)
            dgc = dgc + jnp.where(lane == h, di_col, 0.0) + jnp.where(lane == hh + h, db_col, 0.0)
            dgr = dgr + jnp.where(rowi == h, cs_dd, 0.0) - jnp.where(rowi == hh + h, cs_dd, 0.0)
            dqk_ref[:, f["sl"]] = dqc
            dqk_ref[:, d_model + h * dh:d_model + (h + 1) * dh] = dkc * (dh ** -0.5)
            dv_ref[:, f["sl"]] = dvc.astype(dv_ref.dtype)
            dc_s[h] = dcp
            dn_s[h] = dnp
        dgc_ref[...] = dgc
        dgr_ref[0] = dgr

    dblk = d_model
    rev = lambda c: nc - 1 - c
    return pl.pallas_call(
        body, name=name, grid=(nc,),
        in_specs=[pl.BlockSpec((l, dblk), lambda c: (rev(c), 0)), pl.BlockSpec((l, dblk), lambda c: (rev(c), 1)),
                  pl.BlockSpec((l, dblk), lambda c: (rev(c), vcol // dblk)),
                  pl.BlockSpec((l, LANES), lambda c: (rev(c), 0)), pl.BlockSpec((1, 8, l), lambda c: (rev(c), 0, 0)),
                  pl.BlockSpec((1, hh, dh, dh), lambda c: (rev(c), 0, 0, 0)),
                  pl.BlockSpec((1, hh, 1, dh), lambda c: (rev(c), 0, 0, 0)),
                  pl.BlockSpec((1, hh, 1, LANES), lambda c: (rev(c), 0, 0, 0)),
                  pl.BlockSpec((l, dblk), lambda c: (rev(c), 0))],
        out_specs=[pl.BlockSpec((l, 2 * dblk), lambda c: (rev(c), 0)),
                   pl.BlockSpec((l, dblk), lambda c: (rev(c), 0)), pl.BlockSpec((l, LANES), lambda c: (rev(c), 0)),
                   pl.BlockSpec((1, 8, l), lambda c: (rev(c), 0, 0))],
        out_shape=[pltpu.HBM((s, 2 * d_model), F32),
                   pltpu.HBM((s, d_model), BF16), pltpu.HBM((s, LANES), F32),
                   pltpu.HBM((nc, 8, l), F32)],
        scratch_shapes=[pltpu.VMEM((hh, dh, dh), F32), pltpu.VMEM((hh, 1, dh), F32)],
        compiler_params=_params(10 * hh * dh * dh * 4 + (16 << 20)),
    )(*[_hbm(a) for a in (mqk, mqk, zm, gcol, grow, cs, ns, ms, dhm)])


def _xa_fwd(zm, qcol, kv, gq, gk, d_model, *, name, tq=256):
    s = zm.shape[0]
    nm = kv.shape[0]
    dh = d_model // X_HEADS
    tq = _pick(s, (tq, 128, 64))
    scale = dh ** -0.5

    def body(q_ref, k_ref, v_ref, gq_ref, gk_ref, o_ref):
        qn = _rms_fwd(q_ref[...], gq_ref[...])
        kn = _rms_fwd(k_ref[...], gk_ref[...])
        lg = _dot(qn, kn, NT) * scale
        lg = lg - jnp.max(lg, axis=1, keepdims=True)
        p = jnp.exp(lg)
        p = p / jnp.sum(p, axis=1, keepdims=True)
        o_ref[...] = _dot(p, v_ref[...], NN).astype(o_ref.dtype)

    return pl.pallas_call(
        body, name=name, grid=(X_HEADS, s // tq),
        in_specs=[pl.BlockSpec((tq, dh), lambda h, i: (i, qcol // dh + h)), pl.BlockSpec((nm, dh), lambda h, i: (0, h)),
                  pl.BlockSpec((nm, dh), lambda h, i: (0, X_HEADS + h)),
                  pl.BlockSpec((1, dh), lambda h, i: (0, 0)), pl.BlockSpec((1, dh), lambda h, i: (0, 0))],
        out_specs=pl.BlockSpec((tq, dh), lambda h, i: (i, h)),
        out_shape=pltpu.HBM((s, d_model), BF16),
        compiler_params=_params(32 << 20),
    )(_hbm(zm), _hbm(kv), _hbm(kv), gq, gk)


def _xa_bwd(zm, qcol, kv, gq, gk, dy, d_model, *, name, tq=256):
    s = zm.shape[0]
    nm = kv.shape[0]
    dh = d_model // X_HEADS
    tq = _pick(s, (tq, 128, 64))
    nq = s // tq
    scale = dh ** -0.5

    def body(q_ref, k_ref, v_ref, gq_ref, gk_ref, do_ref, dq_ref, dkn_ref, dv_ref, dgq_ref):
        h, i = pl.program_id(0), pl.program_id(1)

        @pl.when(i == 0)
        def _():
            dkn_ref[...] = jnp.zeros_like(dkn_ref)
            dv_ref[...] = jnp.zeros_like(dv_ref)

        @pl.when((i == 0) & (h == 0))
        def _():
            dgq_ref[...] = jnp.zeros_like(dgq_ref)

        q = q_ref[...]
        qn = _rms_fwd(q, gq_ref[...])
        kn = _rms_fwd(k_ref[...], gk_ref[...])
        lg = _dot(qn, kn, NT) * scale
        lg = lg - jnp.max(lg, axis=1, keepdims=True)
        p = jnp.exp(lg)
        p = p / jnp.sum(p, axis=1, keepdims=True)
        do = do_ref[...]
        dv_ref[...] += _dot(p, do, TN)
        dp = _dot(do, v_ref[...], NT)
        dlg = p * (dp - jnp.sum(dp * p, axis=1, keepdims=True)) * scale
        dqn = _dot(dlg, kn, NN)
        dkn_ref[...] += _dot(dlg, qn, TN)
        dq, dgq = _rms_bwd(q, gq_ref[...], dqn)
        dq_ref[...] = dq.astype(dq_ref.dtype)
        dgq_ref[...] += jnp.sum(dgq, axis=0, keepdims=True)

    return pl.pallas_call(
        body, name=name, grid=(X_HEADS, nq),
        in_specs=[pl.BlockSpec((tq, dh), lambda h, i: (i, qcol // dh + h)), pl.BlockSpec((nm, dh), lambda h, i: (0, h)),
                  pl.BlockSpec((nm, dh), lambda h, i: (0, X_HEADS + h)),
                  pl.BlockSpec((1, dh), lambda h, i: (0, 0)), pl.BlockSpec((1, dh), lambda h, i: (0, 0)),
                  pl.BlockSpec((tq, dh), lambda h, i: (i, h))],
        out_specs=[pl.BlockSpec((tq, dh), lambda h, i: (i, h)), pl.BlockSpec((nm, dh), lambda h, i: (0, h)),
                   pl.BlockSpec((nm, dh), lambda h, i: (0, h)), pl.BlockSpec((1, dh), lambda h, i: (0, 0))],
        out_shape=[pltpu.HBM((s, d_model), BF16), pltpu.HBM((nm, d_model), F32),
                   pltpu.HBM((nm, d_model), F32), pltpu.HBM((1, dh), F32)],
        compiler_params=_params(32 << 20),
    )(_hbm(zm), _hbm(kv), _hbm(kv), gq, gk, _hbm(dy))


def _place():
    return lax.axis_index("x"), lax.axis_index("y"), lax.axis_index("c")


ANY = pl.BlockSpec(memory_space=pl.ANY)


def _allgather_two_level(big, small, *, name, chunk_rows=64):
    r = big.shape[0]
    half = r // 2
    nr = _pick(half, (chunk_rows, 32, 16))
    nq = half // nr

    def body(big_ref, small_ref, obig, osmall, send, recv, fsend, frecv, ssend, srecv, loc):
        x, y, c = _place()
        k = 2 * x + y
        chips = [(1 - x, y), (x, 1 - y), (1 - x, 1 - y)]
        own = [pltpu.make_async_copy(big_ref, obig.at[k], loc.at[0]),
               pltpu.make_async_copy(small_ref, osmall.at[k], loc.at[1])]
        for cp in own:
            cp.start()

        def rows(h, q):
            return pl.ds(pl.multiple_of(h * half + q * nr, nr), nr)

        def over_ici(j, q, slot, h):
            return pltpu.make_async_remote_copy(
                src_ref=big_ref.at[rows(h, q)], dst_ref=obig.at[slot, rows(h, q)], send_sem=send.at[nq * j + q],
                recv_sem=recv.at[nq * j + q], device_id=(chips[j][0], chips[j][1], c), device_id_type=MESH)

        def to_sibling(j, q, h):
            slot = 2 * chips[j][0] + chips[j][1]
            return pltpu.make_async_remote_copy(
                src_ref=obig.at[slot, rows(h, q)], dst_ref=obig.at[slot, rows(h, q)], send_sem=fsend.at[nq * j + q],
                recv_sem=frecv.at[nq * j + q], device_id=(x, y, 1 - c), device_id_type=MESH)

        def small_copy(j, slot):
            return pltpu.make_async_remote_copy(
                src_ref=small_ref, dst_ref=osmall.at[slot], send_sem=ssend.at[j], recv_sem=srecv.at[j],
                device_id=(chips[j][0], chips[j][1], c), device_id_type=MESH)

        for q in range(nq):
            for j in range(3):
                over_ici(j, q, k, c).start()
        for j in range(3):
            small_copy(j, k).start()
        for q in range(nq):
            for j in range(3):
                over_ici(j, q, 2 * chips[j][0] + chips[j][1], c).wait_recv()
                to_sibling(j, q, c).start()
        for q in range(nq):
            for j in range(3):
                to_sibling(j, q, 1 - c).wait_recv()
        for j in range(3):
            small_copy(j, 2 * chips[j][0] + chips[j][1]).wait_recv()
            small_copy(j, k).wait_send()
        for q in range(nq):
            for j in range(3):
                over_ici(j, q, k, c).wait_send()
                to_sibling(j, q, c).wait_send()
        for cp in own:
            cp.wait()

    return pl.pallas_call(
        body, name=name, in_specs=[ANY] * 2, out_specs=[ANY] * 2,
        out_shape=[pltpu.HBM((4,) + big.shape, big.dtype), pltpu.HBM((4,) + small.shape, small.dtype)],
        scratch_shapes=[pltpu.SemaphoreType.DMA((3 * nq,))] * 4
        + [pltpu.SemaphoreType.DMA((3,)), pltpu.SemaphoreType.DMA((3,)), pltpu.SemaphoreType.DMA((2,))],
    )(big, small)


HBM_SPEC = pl.BlockSpec(memory_space=pltpu.HBM)
SEM_SPEC = pl.BlockSpec(memory_space=pltpu.SEMAPHORE)
EFFECT = pltpu.SideEffectType.DATAFLOW_SIDE_EFFECTING


def _split_copies(kind, srcs, lands, send, recv):
    x, y, c = _place()
    if kind == "quarters":
        peers = [(1 - x, y, c), (x, 1 - y, c), (1 - x, 1 - y, c)]
    else:
        peers = [(x ^ ((j >> 2) & 1), y ^ ((j >> 1) & 1), c ^ (j & 1)) for j in range(1, 8)]
    npeer = len(peers)
    out = []
    for t in range(len(srcs)):
        for j, (px, py, pc) in enumerate(peers):
            if kind == "quarters":
                src, mine, theirs = srcs[t], 2 * x + y, 2 * px + py
            else:
                src, mine, theirs = srcs[t].at[2 * px + py, pc], 4 * x + 2 * y + c, 4 * px + 2 * py + pc
            mk = functools.partial(
                pltpu.make_async_remote_copy, src_ref=src, send_sem=send.at[npeer * t + j],
                recv_sem=recv.at[npeer * t + j], device_id=(px, py, pc), device_id_type=MESH)
            out.append((functools.partial(mk, dst_ref=lands[t].at[mine]),
                        functools.partial(mk, dst_ref=lands[t].at[theirs])))
    return out


def _split_start(kind, srcs, land_shapes, after, *, name):
    n = len(srcs)
    ncopies = n * (3 if kind == "quarters" else 7)

    def body(*refs):
        ins, lands = refs[:n], refs[n:2 * n]
        send, recv = refs[2 * n + 1], refs[2 * n + 2]
        token = refs[-1]
        for start, _ in _split_copies(kind, ins, lands, send, recv):
            start().start()
        token[...] = jnp.zeros_like(token)

    lands = [_hbm(lax.empty(shp, a.dtype)) for shp, a in zip(land_shapes, srcs)]
    res = pl.pallas_call(
        body, name=name, in_specs=[HBM_SPEC] * (2 * n) + [ANY],
        out_specs=[SEM_SPEC, SEM_SPEC] + [HBM_SPEC] * (2 * n) + [pl.BlockSpec(memory_space=pltpu.VMEM)],
        out_shape=[pltpu.SemaphoreType.DMA((ncopies,)), pltpu.SemaphoreType.DMA((ncopies,))]
        + [pltpu.HBM(a.shape, a.dtype) for a in srcs] + [pltpu.HBM(shp, a.dtype) for shp, a in zip(land_shapes, srcs)]
        + [jax.ShapeDtypeStruct((8, LANES), F32)],
        input_output_aliases={i: 2 + i for i in range(2 * n)},
        compiler_params=pltpu.CompilerParams(has_side_effects=EFFECT),
    )(*[_hbm(a) for a in srcs], *lands, after)
    return res[0], res[1], list(res[2:2 + n]), list(res[2 + n:2 + 2 * n]), res[-1]


def _split_wait(kind, send, recv, srcs, lands, after, *, name):
    n = len(srcs)

    def body(*refs):
        ins, lnd = refs[:n], refs[n:2 * n]
        snd, rcv = refs[2 * n], refs[2 * n + 1]
        for start, arrive in _split_copies(kind, ins, lnd, snd, rcv):
            start().wait_send()
            arrive().wait_recv()

    res = pl.pallas_call(
        body, name=name, in_specs=[HBM_SPEC] * (2 * n) + [SEM_SPEC, SEM_SPEC] + [ANY] * len(after),
        out_specs=[HBM_SPEC] * (2 * n),
        out_shape=[pltpu.HBM(a.shape, a.dtype) for a in srcs] + [pltpu.HBM(a.shape, a.dtype) for a in lands],
        input_output_aliases={i: i for i in range(2 * n)},
        compiler_params=pltpu.CompilerParams(has_side_effects=EFFECT),
    )(*srcs, *lands, send, recv, *after)
    return list(res[n:])


def _sum8(parts, *, name):
    _, r, c = parts.shape
    t = _pick(r, (128, 64, 32, 16, 8))

    def body(p_ref, o_ref):
        acc = p_ref[0].astype(F32)
        for k in range(1, 8):
            acc = acc + p_ref[k].astype(F32)
        o_ref[...] = acc

    return pl.pallas_call(
        body, name=name, grid=(r // t,), in_specs=[pl.BlockSpec((8, t, c), lambda i: (0, i, 0))],
        out_specs=pl.BlockSpec((t, c), lambda i: (i, 0)), out_shape=pltpu.HBM((r, c), F32),
        compiler_params=_params(2 * 8 * t * c * 2 + 6 * t * c * 4 + (4 << 20)),
    )(_hbm(parts))


def _swap_halves(halves, *, name, chunk_bytes=512 * 1024):
    n = len(halves)
    items = []
    for t, a in enumerate(halves):
        r = a.shape[0]
        k = 1
        while _nbytes(a.shape, a.dtype) // k > chunk_bytes and r % (2 * k) == 0 and (r // (2 * k)) % 8 == 0:
            k *= 2
        items += [(t, q * (r // k), r // k) for q in range(k)]
    m = len(items)

    def body(*refs):
        ins, outs = refs[:n], refs[n:2 * n]
        sbuf, rbuf = refs[2 * n:3 * n], refs[3 * n:4 * n]
        send, recv, loc_own, loc_in, loc_out = refs[4 * n:]
        x, y, c = _place()
        local, stage = [], []
        for t in range(n):
            cp = pltpu.make_async_copy(ins[t], outs[t].at[c], loc_own.at[t])
            cp.start()
            local.append(cp)
        for q, (t, r0, nr) in enumerate(items):
            cp = pltpu.make_async_copy(ins[t].at[pl.ds(r0, nr)], sbuf[t].at[pl.ds(r0, nr)], loc_in.at[q])
            cp.start()
            stage.append(cp)

        def copy(q):
            t, r0, nr = items[q]
            return pltpu.make_async_remote_copy(
                src_ref=sbuf[t].at[pl.ds(r0, nr)], dst_ref=rbuf[t].at[pl.ds(r0, nr)], send_sem=send.at[q],
                recv_sem=recv.at[q], device_id=(x, y, 1 - c), device_id_type=MESH)

        for q in range(m):
            stage[q].wait()
            copy(q).start()
        for q, (t, r0, nr) in enumerate(items):
            copy(q).wait_recv()
            cp = pltpu.make_async_copy(rbuf[t].at[pl.ds(r0, nr)], outs[t].at[1 - c, pl.ds(r0, nr)], loc_out.at[q])
            cp.start()
            local.append(cp)
        for q in range(m):
            copy(q).wait_send()
        for cp in local:
            cp.wait()

    stage_bytes = 2 * sum(_nbytes(a.shape, a.dtype) for a in halves)
    return pl.pallas_call(
        body, name=name, in_specs=[ANY] * n, out_specs=[ANY] * n,
        out_shape=[pltpu.HBM((2,) + a.shape, a.dtype) for a in halves],
        scratch_shapes=[pltpu.VMEM(a.shape, a.dtype) for a in halves] * 2
        + [pltpu.SemaphoreType.DMA((m,)), pltpu.SemaphoreType.DMA((m,)), pltpu.SemaphoreType.DMA((n,)),
           pltpu.SemaphoreType.DMA((m,)), pltpu.SemaphoreType.DMA((m,))],
        compiler_params=_params(stage_bytes + (4 << 20)),
    )(*halves)


def _allreduce_small(p, after, *, name):
    r = p.shape[0]

    def body(p_ref, after_ref, o_ref, buf, send, recv):
        x, y, c = _place()
        me = 4 * x + 2 * y + c
        peers = [(x ^ ((j >> 2) & 1), y ^ ((j >> 1) & 1), c ^ (j & 1)) for j in range(1, 8)]

        def copy(j, slot):
            return pltpu.make_async_remote_copy(
                src_ref=p_ref, dst_ref=buf.at[slot], send_sem=send.at[j], recv_sem=recv.at[j],
                device_id=peers[j], device_id_type=MESH)

        for j in range(7):
            copy(j, me).start()
        buf[me] = p_ref[...]
        for j in range(7):
            px, py, pc = peers[j]
            copy(j, 4 * px + 2 * py + pc).wait_recv()
        for j in range(7):
            copy(j, me).wait_send()
        acc = buf[0]
        for k in range(1, 8):
            acc = acc + buf[k]
        o_ref[...] = acc

    vspec = pl.BlockSpec(memory_space=pltpu.VMEM)
    return pl.pallas_call(
        body, name=name, in_specs=[vspec, ANY], out_specs=vspec, out_shape=jax.ShapeDtypeStruct((r, LANES), F32),
        scratch_shapes=[pltpu.VMEM((8, r, LANES), F32), pltpu.SemaphoreType.DMA((7,)), pltpu.SemaphoreType.DMA((7,))],
    )(p, after)


def _adamw_fn(w, g, m, v):
    m = ADAM_B1 * m + (1.0 - ADAM_B1) * g
    v = ADAM_B2 * v + (1.0 - ADAM_B2) * (g * g)
    m_hat = m / (1.0 - ADAM_B1 ** ADAM_STEP)
    v_hat = v / (1.0 - ADAM_B2 ** ADAM_STEP)
    delta = -ADAM_LR * (m_hat / (jnp.sqrt(v_hat) + ADAM_EPS) + ADAM_WD * w)
    return delta, m, v


def _adamw(w, g, m, v, *, name):
    c = w.shape[1]
    return _rowwise(_adamw_fn, [w, g, m, v], [], [(c, F32)] * 3, name=name, tr=128)


def _pack(vecs, rows):
    flat = jnp.concatenate([a.reshape(-1).astype(F32) for a in vecs])
    return jnp.pad(flat, (0, rows * LANES - flat.shape[0])).reshape(rows, LANES)


def _unpack(p, like):
    flat, out, o = p.reshape(-1), [], 0
    for a in like:
        out.append(flat[o:o + a.size].reshape(a.shape))
        o += a.size
    return out


def kernel(x, mem, g_mix, w_in, b_if, b_gate, conv_w, conv_b, ml_norm_g, g_mem, w_mem_kv, q_norm_g, k_norm_g, w_sb_proj, w_ml_proj, w_x_proj, w_out, g_mlp, w_ff1, w_ff2, loss_target, m_g_mix, m_w_in, m_b_if, m_b_gate, m_conv_w, m_conv_b, m_ml_norm_g, m_g_mem, m_w_mem_kv, m_q_norm_g, m_k_norm_g, m_w_sb_proj, m_w_ml_proj, m_w_x_proj, m_w_out, m_g_mlp, m_w_ff1, m_w_ff2, v_g_mix, v_w_in, v_b_if, v_b_gate, v_conv_w, v_conv_b, v_ml_norm_g, v_g_mem, v_w_mem_kv, v_q_norm_g, v_k_norm_g, v_w_sb_proj, v_w_ml_proj, v_w_x_proj, v_w_out, v_g_mlp, v_w_ff1, v_w_ff2):
    _, s, d = x.shape
    nm = mem.shape[1]
    n_in = 4 * w_in.shape[2]
    dff = 4 * w_ff1.shape[2]
    sbh = d // SB_HD
    hh = ML_HEADS
    dh = d // hh
    nc = s // CHUNK
    assert n_in == 11 * d + 2 * hh and d % (2 * LANES) == 0 and s % LANES == 0
    x2, mem2, tgt = x[0], mem[0], loss_target[0]

    k4 = 2 * lax.axis_index("x") + lax.axis_index("y")
    me = 2 * k4 + lax.axis_index("c")
    g_first = _allgather_two_level(w_in[0].astype(BF16), conv_w[0], name="gather_w_in")
    later = [a[0].astype(BF16) for a in (w_mem_kv, w_sb_proj, w_ml_proj, w_x_proj, w_out, w_ff1, w_ff2)]
    gw_send, gw_recv, gw_src, gw_land, gw_token = _split_start(
        "quarters", later, [(4,) + a.shape for a in later], g_first[0], name="gather_rest_start")
    cols = lambda a: a.transpose(1, 0, 2).reshape(a.shape[1], 4 * a.shape[2])
    rws = lambda a: a.reshape(4 * a.shape[1], a.shape[2])
    w_in_f = cols(g_first[0])
    w_main = jnp.concatenate([w_in_f[:, :7 * d], w_in_f[:, 7 * d + 2 * hh:]], axis=1)
    w_if = jnp.pad(w_in_f[:, 7 * d:7 * d + 2 * hh], ((0, 0), (0, LANES - 2 * hh)))
    conv_wf = cols(g_first[1])
    b_if_p = jnp.pad(b_if, ((0, 0), (0, LANES - 2 * hh)))

    (hn,) = _rowwise(_rms_fwd, [x2], [g_mix], [(d, BF16)], name="norm_in")
    zm = _mm(hn, w_main, after=gw_token, name="proj_in")
    zif = _mm(hn, w_if, name="proj_if")
    y_sb, ltot = _sb_fwd(zm, sbh, name="sb_fwd")

    def gate_fn(z, b):
        pre = z + b
        lane = lax.broadcasted_iota(jnp.int32, pre.shape, 1)
        return jnp.where(lane < hh, pre, -_softplus(-pre))

    (gcol,) = _rowwise(gate_fn, [zif], [b_if_p], [(LANES, F32)], name="ml_gates")
    grow = gcol[:, :8].T.reshape(8, nc, CHUNK).transpose(1, 0, 2)
    mqk = _conv_fwd(zm, 3 * d, 2 * d, conv_wf, conv_b, name="conv_fwd")
    hm, cst, nst, mst = _ml_fwd(mqk, zm, 5 * d, gcol, grow, d, name="ml_fwd")

    def mlout_fn(hv, o, g):
        ys = [_rms_fwd(hv[:, k * dh:(k + 1) * dh], g[:, k * dh:(k + 1) * dh]) for k in range(hh)]
        return jnp.concatenate(ys, axis=1) * _sigmoid(o)

    (y_ml,) = _rowwise(mlout_fn, [hm, (zm, d, 6)], [ml_norm_g], [(d, BF16)], name="ml_out")
    gw_land = _split_wait("quarters", gw_send, gw_recv, gw_src, gw_land, [y_ml, y_sb], name="gather_rest_wait")
    gw = [lax.dynamic_update_index_in_dim(ld, a, k4, 0) for ld, a in zip(gw_land, later)]
    w_kv, w_sbp, w_mlp, w_xp, w_o, w_f1, w_f2 = (cols(gw[0]), rws(gw[1]), rws(gw[2]), rws(gw[3]), rws(gw[4]),
                                                 cols(gw[5]), rws(gw[6]))
    (memn,) = _rowwise(_rms_fwd, [mem2], [g_mem], [(d, BF16)], name="norm_mem")
    kv = _mm(memn, w_kv, name="proj_kv")
    y_x = _xa_fwd(zm, 7 * d, kv, q_norm_g, k_norm_g, d, name="xa_fwd")
    p_sb = _mm(y_sb, w_sbp, name="proj_sb")
    p_ml = _mm(y_ml, w_mlp, name="proj_ml")
    p_x = _mm(y_x, w_xp, name="proj_x")

    def merge_fn(a, b, c, g0, g1, g2, bg):
        return (_sigmoid(g0 + bg[:, :d]) * a + _sigmoid(g1 + bg[:, d:2 * d]) * b + _sigmoid(g2 + bg[:, 2 * d:]) * c)

    gate_cols = [(zm, d, 8), (zm, d, 9), (zm, d, 10)]
    (mixed,) = _rowwise(merge_fn, [p_sb, p_ml, p_x] + gate_cols, [b_gate], [(d, BF16)], name="merge")
    x1 = _mm(mixed, w_o, add=x2, name="proj_out")
    (h2,) = _rowwise(_rms_fwd, [x1], [g_mlp], [(d, BF16)], name="norm_mlp")
    u = _mm(h2, w_f1, name="ff1")
    (act,) = _rowwise(lambda uv: jnp.square(jnp.maximum(uv, 0.0)), [u], [], [(dff, BF16)], name="relu2", tr=128)
    yo = _mm(act, w_f2, add=x1, name="ff2")

    def loss_fn(yv, tv):
        e = yv - tv
        return e * (1.0 / d), jnp.sum(e * e, axis=0, keepdims=True) * (0.5 / d)

    dy, loss_cols = _rowwise(loss_fn, [yo, tgt], [], [(d, F32)], [d], name="loss")

    dact = _mm(dy, w_f2, tb=True, name="ff2_dx")
    dw_f2 = _mm(act, dy, ta=True, name="ff2_dw")
    (du,) = _rowwise(lambda g, uv: g * 2.0 * jnp.maximum(uv, 0.0), [dact, u], [], [(dff, BF16)], name="relu2_bwd",
                     tr=128)
    dw_f1 = _mm(h2, du, ta=True, name="ff1_dw")
    dh2 = _mm(du, w_f1, tb=True, name="ff1_dx")

    def norm_bwd_fn(xv, dyv, res, g):
        dx, dg = _rms_bwd(xv, g, dyv)
        return dx + res, jnp.sum(dg, axis=0, keepdims=True)

    dx1, dg_mlp = _rowwise(norm_bwd_fn, [x1, dh2, dy], [g_mlp], [(d, F32)], [d], name="norm_mlp_bwd")
    dmixed = _mm(dx1, w_o, tb=True, name="proj_out_dx")
    dw_o = _mm(mixed, dx1, ta=True, name="proj_out_dw")

    def merge_bwd_fn(dm, a, b, c, g0, g1, g2, bg):
        outs, dgs = [], []
        for p, g, k in ((a, g0, 0), (b, g1, 1), (c, g2, 2)):
            sg = _sigmoid(g + bg[:, k * d:(k + 1) * d])
            outs.append(dm * sg)
            dgs.append(dm * p * sg * (1.0 - sg))
        dgate = jnp.concatenate(dgs, axis=1)
        return (*outs, dgate, jnp.sum(dgate, axis=0, keepdims=True))

    dp_sb, dp_ml, dp_x, dgate, db_gate = _rowwise(
        merge_bwd_fn, [dmixed, p_sb, p_ml, p_x] + gate_cols, [b_gate], [(d, BF16)] * 3 + [(3 * d, BF16)], [3 * d],
        name="merge_bwd", tr=128)
    dw_sbp = _mm(y_sb, dp_sb, ta=True, name="proj_sb_dw")
    dw_mlp = _mm(y_ml, dp_ml, ta=True, name="proj_ml_dw")
    dw_xp = _mm(y_x, dp_x, ta=True, name="proj_x_dw")
    dy_sb = _mm(dp_sb, w_sbp, tb=True, out_dtype=BF16, name="proj_sb_dx")
    dy_ml = _mm(dp_ml, w_mlp, tb=True, name="proj_ml_dx")
    dy_x = _mm(dp_x, w_xp, tb=True, out_dtype=BF16, name="proj_x_dx")

    uncols = lambda a: a.reshape(a.shape[0], 4, a.shape[1] // 4).transpose(1, 0, 2)
    unrws = lambda a: a.reshape(4, a.shape[0] // 4, a.shape[1])
    to_parts = lambda q: q.astype(BF16).reshape(4, 2, q.shape[1] // 2, q.shape[2])
    early = [to_parts(q) for q in (unrws(dw_sbp), unrws(dw_mlp), unrws(dw_xp), unrws(dw_o), uncols(dw_f1),
                                   unrws(dw_f2))]
    ge_send, ge_recv, ge_src, ge_land, ge_token = _split_start(
        "grads", early, [(8,) + a.shape[2:] for a in early], dy_x, name="exchange_early_start")

    dsq, dsk, dsv = _sb_bwd(zm, dy_sb, ltot, ge_token, sbh, name="sb_bwd")

    def mlout_bwd_fn(dyv, hv, o, g):
        sg = _sigmoid(o)
        dn = dyv * sg
        dxs, dgs, ys = [], [], []
        for k in range(hh):
            sl = slice(k * dh, (k + 1) * dh)
            ys.append(_rms_fwd(hv[:, sl], g[:, sl]))
            dxk, dgk = _rms_bwd(hv[:, sl], g[:, sl], dn[:, sl])
            dxs.append(dxk)
            dgs.append(dgk)
        do = dyv * jnp.concatenate(ys, axis=1) * sg * (1.0 - sg)
        return jnp.concatenate(dxs, axis=1), do, jnp.sum(jnp.concatenate(dgs, axis=1), axis=0, keepdims=True)

    dhm, dmlo, dg_mln = _rowwise(mlout_bwd_fn, [dy_ml, hm, (zm, d, 6)], [ml_norm_g], [(d, F32), (d, BF16)], [d],
                                 name="ml_out_bwd")
    dmqk, dmlv, dgc, dgr = _ml_bwd(mqk, zm, 5 * d, gcol, grow, cst, nst, mst, dhm, d, name="ml_bwd")
    dmlqk, dconv_w, dconv_b = _conv_bwd(zm, 3 * d, 2 * d, conv_wf, conv_b, dmqk, name="conv_bwd")
    dgr_t = jnp.pad(dgr.transpose(1, 0, 2).reshape(8, s).T, ((0, 0), (0, LANES - 8)))

    def gate_bwd_fn(a, b, z, bias):
        tot = a + b
        rows_t = tot.shape[0]
        r = lax.broadcasted_iota(jnp.int32, (rows_t, rows_t), 0)
        c = lax.broadcasted_iota(jnp.int32, (rows_t, rows_t), 1)
        sh = CHUNK.bit_length() - 1
        same_chunk = jnp.right_shift(r, sh) == jnp.right_shift(c, sh)
        dlf = _u01dot(((c >= r) & same_chunk).astype(BF16), tot)
        lane = lax.broadcasted_iota(jnp.int32, tot.shape, 1)
        dz = jnp.where(lane < hh, tot, jnp.where(lane < 2 * hh, dlf * _sigmoid(-(z + bias)), 0.0))
        return dz, jnp.sum(dz, axis=0, keepdims=True)

    dzif, db_if_p = _rowwise(gate_bwd_fn, [dgc, dgr_t, zif], [b_if_p], [(LANES, BF16)], [LANES], name="ml_gates_bwd",
                             tr=8 * CHUNK)
    dxq, dkn, dxv, dg_qn = _xa_bwd(zm, 7 * d, kv, q_norm_g, k_norm_g, dy_x, d, name="xa_bwd")

    def knorm_bwd_fn(kvv, dknv, dvv, g):
        dks, dgs = [], []
        for k in range(X_HEADS):
            sl = slice(k * dh, (k + 1) * dh)
            dk, dg = _rms_bwd(kvv[:, sl], g, dknv[:, sl])
            dks.append(dk)
            dgs.append(jnp.sum(dg, axis=0, keepdims=True))
        return jnp.concatenate(dks + [dvv], axis=1), dgs[0] + dgs[1] + dgs[2] + dgs[3]

    dkv, dg_kn = _rowwise(knorm_bwd_fn, [(kv, d, 0), dkn, dxv], [k_norm_g], [(2 * d, BF16)], [dh], name="xa_knorm_bwd")
    dw_kv = _mm(memn, dkv, ta=True, name="proj_kv_dw")
    dmemn = _mm(dkv, w_kv, tb=True, name="proj_kv_dx")

    def gmem_fn(mv, dv_, g):
        _, dg = _rms_bwd(mv, g, dv_)
        return (jnp.sum(dg, axis=0, keepdims=True),)

    (dg_mem,) = _rowwise(gmem_fn, [mem2, dmemn], [g_mem], [], [d], name="norm_mem_bwd")

    dzm = jnp.concatenate([dsq, dsk, dsv, dmlqk, dmlv, dmlo, dxq, dgate], axis=1)
    dw_main = _mm(hn, dzm, ta=True, out_dtype=BF16, name="proj_in_dw")
    dw_if = _mm(hn, dzif, ta=True, out_dtype=BF16, name="proj_if_dw")
    dw_in = jnp.concatenate([dw_main[:, :7 * d], dw_if[:, :2 * hh], dw_main[:, 7 * d:]], axis=1)
    late = [to_parts(uncols(dw_in)), to_parts(uncols(dw_kv))]
    gl_send, gl_recv, gl_src, gl_land, gl_token = _split_start(
        "grads", late, [(8,) + a.shape[2:] for a in late], dw_if, name="exchange_late_start")
    dhn = _mm(dzm, w_main, tb=True, after=gl_token, name="proj_in_dx")
    dhn = _mm(dzif, w_if, tb=True, add=dhn, name="proj_if_dx")
    dx, dg_mix = _rowwise(norm_bwd_fn, [x2, dhn, dx1], [g_mix], [(d, F32)], [d], name="norm_in_bwd")

    own = lambda p: lax.dynamic_index_in_dim(lax.dynamic_index_in_dim(p, k4, 0, keepdims=False),
                                             lax.axis_index("c"), 0, keepdims=False)

    def finish(tag, send, recv, src, land, parts, after, ws, ms, vs):
        land = _split_wait("grads", send, recv, src, land, after, name=f"exchange_{tag}_wait")
        got = [lax.dynamic_update_index_in_dim(ld, own(p), me, 0) for ld, p in zip(land, parts)]
        halves = [_sum8(r, name=f"sum_grads_{tag}{i}") for i, r in enumerate(got)]
        both = _swap_halves(halves, name=f"swap_halves_{tag}")
        gs = [b.reshape(2 * b.shape[1], b.shape[2]) for b in both]
        return gs, [_adamw(w[0], g, m[0], v[0], name=f"adamw_{tag}{i}")
                    for i, (w, g, m, v) in enumerate(zip(ws, gs, ms, vs))]

    g_early, out_early = finish(
        "early", ge_send, ge_recv, ge_src, ge_land, early, [dx],
        [w_sb_proj, w_ml_proj, w_x_proj, w_out, w_ff1, w_ff2],
        [m_w_sb_proj, m_w_ml_proj, m_w_x_proj, m_w_out, m_w_ff1, m_w_ff2],
        [v_w_sb_proj, v_w_ml_proj, v_w_x_proj, v_w_out, v_w_ff1, v_w_ff2])
    g_late, out_late = finish(
        "late", gl_send, gl_recv, gl_src, gl_land, late, [o[0] for o in out_early],
        [w_in, w_mem_kv], [m_w_in, m_w_mem_kv], [v_w_in, v_w_mem_kv])
    g_big = g_late + g_early
    big_out = out_late + out_early

    small_g = [dg_mix, db_if_p[:, :2 * hh], db_gate, dconv_w, dconv_b, dg_mln, dg_mem, dg_qn, dg_kn, dg_mlp,
               jnp.sum(loss_cols).reshape(1, 1)]
    n_small = sum(a.size for a in small_g)
    rows = -(-n_small // (8 * LANES)) * 8
    g_small = _unpack(_allreduce_small(_pack(small_g, rows), out_late[0][0], name="allreduce_small"), small_g)
    loss = g_small[-1].reshape(())
    qw = conv_w.shape[2]
    g_conv_w = lax.dynamic_slice_in_dim(g_small[3], k4 * qw, qw, axis=1)
    g_small_w = [g_small[0], g_small[1], g_small[2], g_conv_w] + g_small[4:10]
    sm_w = [g_mix, b_if, b_gate, conv_w[0], conv_b, ml_norm_g, g_mem, q_norm_g, k_norm_g, g_mlp]
    sm_m = [m_g_mix, m_b_if, m_b_gate, m_conv_w[0], m_conv_b, m_ml_norm_g, m_g_mem, m_q_norm_g, m_k_norm_g, m_g_mlp]
    sm_v = [v_g_mix, v_b_if, v_b_gate, v_conv_w[0], v_conv_b, v_ml_norm_g, v_g_mem, v_q_norm_g, v_k_norm_g, v_g_mlp]
    n_sw = sum(a.size for a in sm_w)
    rows_w = -(-n_sw // (8 * LANES)) * 8
    sm_out = _adamw(_pack(sm_w, rows_w), _pack(g_small_w, rows_w), _pack(sm_m, rows_w), _pack(sm_v, rows_w),
                    name="adamw_small")
    sm_delta, sm_newm, sm_newv = [_unpack(p, sm_w) for p in sm_out]

    order = ["g_mix", "w_in", "b_if", "b_gate", "conv_w", "conv_b", "ml_norm_g", "g_mem", "w_mem_kv", "q_norm_g",
             "k_norm_g", "w_sb_proj", "w_ml_proj", "w_x_proj", "w_out", "g_mlp", "w_ff1", "w_ff2"]
    small_names = ["g_mix", "b_if", "b_gate", "conv_w", "conv_b", "ml_norm_g", "g_mem", "q_norm_g", "k_norm_g", "g_mlp"]
    big_names = ["w_in", "w_mem_kv", "w_sb_proj", "w_ml_proj", "w_x_proj", "w_out", "w_ff1", "w_ff2"]
    grads, deltas, new_m, new_v = {}, {}, {}, {}
    for i, nme in enumerate(small_names):
        shp = sm_w[i].shape if nme != "conv_w" else conv_w.shape
        grads[nme] = g_small_w[i].reshape(shp)
        deltas[nme], new_m[nme], new_v[nme] = (sm_delta[i].reshape(shp), sm_newm[i].reshape(shp),
                                               sm_newv[i].reshape(shp))
    for i, nme in enumerate(big_names):
        grads[nme] = g_big[i][None]
        deltas[nme], new_m[nme], new_v[nme] = (o[None] for o in big_out[i])
    return (loss, dx[None], *[grads[k] for k in order], *[deltas[k] for k in order], *[new_m[k] for k in order],
            *[new_v[k] for k in order])
```

```python
import functools

import jax
import jax.numpy as jnp
from jax import lax
from jax.experimental import pallas as pl
from jax.experimental.pallas import tpu as pltpu

F32 = jnp.float32
BF16 = jnp.bfloat16
MESH = pl.DeviceIdType.MESH

EPS = 1e-6
SB_HD = 128
ML_HEADS = 4
X_HEADS = 4
CHUNK = 64
CONV_W = 4
LANES = 128
ADAM_LR = 0.001
ADAM_B1 = 0.9
ADAM_B2 = 0.999
ADAM_EPS = 1e-08
ADAM_WD = 0.01
ADAM_STEP = 10
VMEM_CAP = 56 * 1024 * 1024
NEG = -1e30

NT = (((1,), (1,)), ((), ()))
NN = (((1,), (0,)), ((), ()))
TN = (((0,), (0,)), ((), ()))


def _dot(a, b, dn=NN):
    return lax.dot_general(a.astype(BF16), b.astype(BF16), dn, preferred_element_type=F32)


def _dot01(x, u, dn=NN):
    hi = x.astype(BF16)
    lo = (x - hi.astype(F32)).astype(BF16)
    return (lax.dot_general(hi, u, dn, preferred_element_type=F32)
            + lax.dot_general(lo, u, dn, preferred_element_type=F32))


def _u01dot(u, x):
    hi = x.astype(BF16)
    lo = (x - hi.astype(F32)).astype(BF16)
    return (lax.dot_general(u, hi, NN, preferred_element_type=F32)
            + lax.dot_general(u, lo, NN, preferred_element_type=F32))


def _pick(n, cands):
    for c in cands:
        if c <= n and n % c == 0:
            return c
    return n


def _nbytes(shape, dtype):
    n = 1
    for s in shape:
        n *= s
    return n * jnp.dtype(dtype).itemsize


def _params(vmem_bytes):
    return pltpu.CompilerParams(vmem_limit_bytes=int(min(VMEM_CAP, max(vmem_bytes, 16 * 1024 * 1024))))


def _hbm(a):
    return pltpu.with_memory_space_constraint(a, pltpu.HBM)


def _softplus(z):
    return jnp.maximum(z, 0.0) + jnp.log(1.0 + jnp.exp(-jnp.abs(z)))


def _sigmoid(z):
    return 1.0 / (1.0 + jnp.exp(-z))


def _rms_fwd(xv, g):
    r = lax.rsqrt(jnp.mean(xv * xv, axis=-1, keepdims=True) + EPS)
    return xv * r * g


def _rms_bwd(xv, g, dy):
    r = lax.rsqrt(jnp.mean(xv * xv, axis=-1, keepdims=True) + EPS)
    xh = xv * r
    dxh = dy * g
    dx = r * (dxh - xh * jnp.mean(dxh * xh, axis=-1, keepdims=True))
    return dx, dy * xh


def _mm(a, b, *, name, ta=False, tb=False, add=None, out_dtype=F32, bm=1024, bn=1024, bk=1024, after=None):
    m, k = (a.shape[1], a.shape[0]) if ta else a.shape
    n = b.shape[0] if tb else b.shape[1]
    tm = _pick(m, (bm, 512, 256, 128))
    tn = _pick(n, (bn, 512, 256, 128))
    tk = _pick(k, (bk, 512, 256, 128))
    nk = k // tk
    dn = (((0 if ta else 1,), (1 if tb else 0,)), ((), ()))
    has_add = add is not None

    def body(*refs):
        a_ref, b_ref = refs[:2]
        c_ref = refs[2] if has_add else None
        o_ref = refs[2 + has_add + (after is not None)]
        part = lax.dot_general(a_ref[...].astype(BF16), b_ref[...].astype(BF16), dn, preferred_element_type=F32)

        def finish(r):
            if has_add:
                r = r + c_ref[...].astype(F32)
            o_ref[...] = r.astype(out_dtype)

        if nk == 1:
            finish(part)
        else:
            acc_ref = refs[-1]
            kk = pl.program_id(2)

            @pl.when(kk == 0)
            def _():
                acc_ref[...] = part

            @pl.when(kk > 0)
            def _():
                acc_ref[...] += part

            @pl.when(kk == nk - 1)
            def _():
                finish(acc_ref[...])

    a_spec = pl.BlockSpec((tk, tm), lambda i, j, q: (q, i)) if ta else pl.BlockSpec((tm, tk), lambda i, j, q: (i, q))
    b_spec = pl.BlockSpec((tn, tk), lambda i, j, q: (j, q)) if tb else pl.BlockSpec((tk, tn), lambda i, j, q: (q, j))
    o_spec = pl.BlockSpec((tm, tn), lambda i, j, q: (i, j))
    ins, specs = [_hbm(a), _hbm(b)], [a_spec, b_spec]
    vm = 2 * (_nbytes((tm, tk), a.dtype) + _nbytes((tk, tn), b.dtype) + _nbytes((tm, tn), out_dtype)) \
        + 3 * _nbytes((tm, tn), F32) + _nbytes((tm, tk), BF16) + _nbytes((tk, tn), BF16)
    if has_add:
        ins.append(_hbm(add))
        specs.append(o_spec)
        vm += 2 * _nbytes((tm, tn), add.dtype)
    if after is not None:
        ins.append(after)
        specs.append(ANY)
    return pl.pallas_call(
        body, name=name, grid=(m // tm, n // tn, nk), in_specs=specs, out_specs=o_spec,
        out_shape=pltpu.HBM((m, n), out_dtype), scratch_shapes=[pltpu.VMEM((tm, tn), F32)] if nk > 1 else [],
        compiler_params=_params(vm + (4 << 20)),
    )(*ins)


def _rowwise(fn, rows, consts, outs, reds=(), *, name, tr=256, temps=6):
    rows = [r if isinstance(r, tuple) else (r, r.shape[1], 0) for r in rows]
    nrows = rows[0][0].shape[0]
    t = _pick(nrows, (tr, 128, 64, 32, 16, 8))
    nr, nc, no = len(rows), len(consts), len(outs)

    def body(*refs):
        rin, cin = refs[:nr], refs[nr:nr + nc]
        oref, rref = refs[nr + nc:nr + nc + no], refs[nr + nc + no:]
        res = fn(*[r[...] for r in rin], *[c[...] for c in cin])
        if not isinstance(res, (tuple, list)):
            res = (res,)
        for o, v in zip(oref, res[:no]):
            o[...] = v.astype(o.dtype)
        if rref:
            @pl.when(pl.program_id(0) == 0)
            def _():
                for r in rref:
                    r[...] = jnp.zeros_like(r)

            for r, v in zip(rref, res[no:]):
                r[...] += v

    in_specs = [pl.BlockSpec((t, w), functools.partial(lambda i, ci: (i, ci), ci=ci)) for (_, w, ci) in rows]
    in_specs += [pl.BlockSpec(c.shape, functools.partial(lambda i, nd: (0,) * nd, nd=c.ndim)) for c in consts]
    out_specs = [pl.BlockSpec((t, w), lambda i: (i, 0)) for (w, _) in outs]
    out_specs += [pl.BlockSpec((1, w), lambda i: (0, 0)) for w in reds]
    out_shape = [pltpu.HBM((nrows, w), dt) for (w, dt) in outs]
    out_shape += [jax.ShapeDtypeStruct((1, w), F32) for w in reds]
    widest = max([w for (_, w, _) in rows] + [w for (w, _) in outs])
    vm = 2 * sum(_nbytes((t, w), a.dtype) for (a, w, _) in rows) + 2 * sum(_nbytes((t, w), dt) for (w, dt) in outs)
    vm += temps * _nbytes((t, widest), F32) + (2 << 20)
    res = pl.pallas_call(
        body, name=name, grid=(nrows // t,), in_specs=in_specs, out_specs=out_specs, out_shape=out_shape,
        compiler_params=_params(vm),
    )(*[_hbm(a) for (a, _, _) in rows], *consts)
    return list(res)


def _sb_tiles(s, tq, tk):
    tq = _pick(s, (tq, 256, 128))
    tk = _pick(tq, (tk, 128))
    return tq, tk, tq // tk


def _sb_fwd(zm, heads, *, name, tq=512, tk=256):
    s = zm.shape[0]
    tq, tk, nd = _sb_tiles(s, tq, tk)
    scale = SB_HD ** -0.5

    def body(q_ref, k_ref, v_ref, o_ref, lt_ref):
        i = pl.program_id(1)
        qb = (q_ref[...] * scale).astype(BF16)
        r = lax.broadcasted_iota(jnp.int32, (tq, tk), 0)
        c = lax.broadcasted_iota(jnp.int32, (tq, tk), 1)
        ur = lax.broadcasted_iota(jnp.int32, (tk, tk), 0)
        uc = lax.broadcasted_iota(jnp.int32, (tk, tk), 1)
        usuf = (ur > uc).astype(BF16)

        def tile(j, carry, causal):
            acc, cl = carry
            rows = pl.ds(pl.multiple_of(j * tk, tk), tk)
            kb = k_ref[rows, :].astype(BF16)
            vb = v_ref[rows, :].astype(BF16)
            z = lax.dot_general(qb, kb, NT, preferred_element_type=F32)
            lsig = -_softplus(z)
            l = lsig if causal is None else jnp.where(causal, lsig, 0.0)
            loga = z + lsig + _dot01(l, usuf) + cl
            if causal is not None:
                loga = jnp.where(causal, loga, NEG)
            a = jnp.exp(loga)
            acc = acc + lax.dot_general(a.astype(BF16), vb, NN, preferred_element_type=F32)
            return acc, cl + jnp.sum(l, axis=1, keepdims=True)

        carry = (jnp.zeros((tq, SB_HD), F32), jnp.zeros((tq, 1), F32))
        for dd in range(nd - 1, -1, -1):
            carry = tile(i * nd + dd, carry, c + dd * tk < r)
        acc, cl = lax.fori_loop(0, i * nd, lambda n, cr: tile(i * nd - 1 - n, cr, None), carry)
        o_ref[...] = acc.astype(o_ref.dtype)
        lt_ref[...] = jnp.broadcast_to(cl, (tq, LANES))

    blk = lambda off: pl.BlockSpec((s, SB_HD), functools.partial(lambda h, i, off: (0, off + h), off=off))
    return pl.pallas_call(
        body, name=name, grid=(heads, s // tq),
        in_specs=[pl.BlockSpec((tq, SB_HD), lambda h, i: (i, h)), blk(heads), blk(2 * heads)],
        out_specs=[pl.BlockSpec((tq, SB_HD), lambda h, i: (i, h)), pl.BlockSpec((tq, LANES), lambda h, i: (i, h))],
        out_shape=[pltpu.HBM((s, heads * SB_HD), BF16), pltpu.HBM((s, heads * LANES), F32)],
        compiler_params=_params(8 * s * SB_HD * 4 + 24 * tq * tk * 4 + (8 << 20)),
    )(_hbm(zm), _hbm(zm), _hbm(zm))


def _sb_bwd(zm, dy, ltot, after, heads, *, name, tq=512, tk=256):
    s = zm.shape[0]
    tq, tk, nd = _sb_tiles(s, tq, tk)
    nq = s // tq
    scale = SB_HD ** -0.5

    def body(q_ref, k_ref, v_ref, do_ref, lt_ref, after_ref, dq_ref, dk_ref, dv_ref, dka, dva):
        i = pl.program_id(1)

        @pl.when(i == 0)
        def _():
            dka[...] = jnp.zeros_like(dka)
            dva[...] = jnp.zeros_like(dva)

        qb = (q_ref[...] * scale).astype(BF16)
        dob = do_ref[...].astype(BF16)
        ltot_c = lt_ref[:, 0:1]
        r = lax.broadcasted_iota(jnp.int32, (tq, tk), 0)
        c = lax.broadcasted_iota(jnp.int32, (tq, tk), 1)
        ur = lax.broadcasted_iota(jnp.int32, (tk, tk), 0)
        uc = lax.broadcasted_iota(jnp.int32, (tk, tk), 1)
        uincl = (ur <= uc).astype(BF16)
        uexcl = (ur < uc).astype(BF16)

        def tile(j, carry, causal):
            dq, cl, cg = carry
            rows = pl.ds(pl.multiple_of(j * tk, tk), tk)
            kb = k_ref[rows, :].astype(BF16)
            vb = v_ref[rows, :].astype(BF16)
            z = lax.dot_general(qb, kb, NT, preferred_element_type=F32)
            lsig = -_softplus(z)
            l = lsig if causal is None else jnp.where(causal, lsig, 0.0)
            later = ltot_c - (cl + _dot01(l, uincl))
            loga = z + lsig + later
            if causal is not None:
                loga = jnp.where(causal, loga, NEG)
            a = jnp.exp(loga)
            sig = jnp.exp(z + lsig)
            g = a * lax.dot_general(dob, vb, NT, preferred_element_type=F32)
            p = cg + lax.dot_general(g.astype(BF16), uexcl, NN, preferred_element_type=F32)
            dz = g - sig * (g + p)
            if causal is not None:
                dz = jnp.where(causal, dz, 0.0)
            dzb = dz.astype(BF16)
            dva[rows, :] += lax.dot_general(a.astype(BF16), dob, TN, preferred_element_type=F32)
            dka[rows, :] += lax.dot_general(dzb, qb, TN, preferred_element_type=F32)
            dq = dq + lax.dot_general(dzb, kb, NN, preferred_element_type=F32)
            return dq, cl + jnp.sum(l, axis=1, keepdims=True), cg + jnp.sum(g, axis=1, keepdims=True)

        init = (jnp.zeros((tq, SB_HD), F32), jnp.zeros((tq, 1), F32), jnp.zeros((tq, 1), F32))
        carry = lax.fori_loop(0, i * nd, lambda j, cr: tile(j, cr, None), init)
        for dd in range(nd):
            carry = tile(i * nd + dd, carry, c + dd * tk < r)
        dq_ref[...] = (carry[0] * scale).astype(dq_ref.dtype)

        @pl.when(i == nq - 1)
        def _():
            dk_ref[...] = dka[...].astype(dk_ref.dtype)
            dv_ref[...] = dva[...].astype(dv_ref.dtype)

    blk = lambda off: pl.BlockSpec((s, SB_HD), functools.partial(lambda h, i, off: (0, off + h), off=off))
    tile_spec = pl.BlockSpec((tq, SB_HD), lambda h, i: (i, h))
    full = pltpu.HBM((s, heads * SB_HD), BF16)
    return pl.pallas_call(
        body, name=name, grid=(heads, nq),
        in_specs=[tile_spec, blk(heads), blk(2 * heads), tile_spec, pl.BlockSpec((tq, LANES), lambda h, i: (i, h)),
                  ANY],
        out_specs=[tile_spec, blk(0), blk(0)],
        out_shape=[full, full, full],
        scratch_shapes=[pltpu.VMEM((s, SB_HD), F32), pltpu.VMEM((s, SB_HD), F32)],
        compiler_params=_params(12 * s * SB_HD * 4 + 32 * tq * tk * 4 + (8 << 20)),
    )(_hbm(zm), _hbm(zm), _hbm(zm), _hbm(dy), _hbm(ltot), after)


def _conv_taps(u, w_ref, rows_i):
    taps = []
    for j in range(CONV_W):
        sh = CONV_W - 1 - j
        if sh == 0:
            taps.append(u)
        else:
            taps.append(jnp.where(rows_i >= sh, pltpu.roll(u, sh, 0), 0.0))
    return taps


def _conv_fwd(zm, col0, width, cw, cb, *, name):
    s = zm.shape[0]
    bw = _pick(width, (LANES,))
    off = col0 // bw

    def body(u_ref, w_ref, b_ref, o_ref):
        u = u_ref[...]
        rows_i = lax.broadcasted_iota(jnp.int32, u.shape, 0)
        acc = jnp.broadcast_to(b_ref[...], u.shape)
        for j, tp in enumerate(_conv_taps(u, w_ref, rows_i)):
            acc = acc + tp * w_ref[j:j + 1, :]
        o_ref[...] = acc * _sigmoid(acc)

    return pl.pallas_call(
        body, name=name, grid=(width // bw,),
        in_specs=[pl.BlockSpec((s, bw), lambda j: (0, off + j)), pl.BlockSpec((CONV_W, bw), lambda j: (0, j)),
                  pl.BlockSpec((1, bw), lambda j: (0, j))],
        out_specs=pl.BlockSpec((s, bw), lambda j: (0, j)),
        out_shape=pltpu.HBM((s, width), F32),
        compiler_params=_params(12 * s * bw * 4 + (4 << 20)),
    )(_hbm(zm), cw, cb)


def _conv_bwd(zm, col0, width, cw, cb, dqk, *, name):
    s = zm.shape[0]
    bw = _pick(width, (LANES,))
    off = col0 // bw

    def body(u_ref, w_ref, b_ref, d_ref, du_ref, dw_ref, db_ref):
        u = u_ref[...]
        rows_i = lax.broadcasted_iota(jnp.int32, u.shape, 0)
        taps = _conv_taps(u, w_ref, rows_i)
        acc = jnp.broadcast_to(b_ref[...], u.shape)
        for j, tp in enumerate(taps):
            acc = acc + tp * w_ref[j:j + 1, :]
        sg = _sigmoid(acc)
        dc = d_ref[...] * (sg * (1.0 + acc * (1.0 - sg)))
        du = jnp.zeros_like(u)
        for j in range(CONV_W):
            sh = CONV_W - 1 - j
            if sh == 0:
                du = du + dc * w_ref[j:j + 1, :]
            else:
                du = du + jnp.where(rows_i < s - sh, pltpu.roll(dc, s - sh, 0), 0.0) * w_ref[j:j + 1, :]
            dw_ref[j:j + 1, :] = jnp.sum(dc * taps[j], axis=0, keepdims=True)
        du_ref[...] = du.astype(du_ref.dtype)
        db_ref[...] = jnp.sum(dc, axis=0, keepdims=True)

    return pl.pallas_call(
        body, name=name, grid=(width // bw,),
        in_specs=[pl.BlockSpec((s, bw), lambda j: (0, off + j)), pl.BlockSpec((CONV_W, bw), lambda j: (0, j)),
                  pl.BlockSpec((1, bw), lambda j: (0, j)), pl.BlockSpec((s, bw), lambda j: (0, j))],
        out_specs=[pl.BlockSpec((s, bw), lambda j: (0, j)), pl.BlockSpec((CONV_W, bw), lambda j: (0, j)),
                   pl.BlockSpec((1, bw), lambda j: (0, j))],
        out_shape=[pltpu.HBM((s, width), BF16), pltpu.HBM((CONV_W, width), F32),
                   pltpu.HBM((1, width), F32)],
        compiler_params=_params(20 * s * bw * 4 + (4 << 20)),
    )(_hbm(zm), cw, cb, _hbm(dqk))


def _ml_gates(gcol_ref, grow_ref):
    l = CHUNK
    r = lax.broadcasted_iota(jnp.int32, (l, l), 0)
    c = lax.broadcasted_iota(jnp.int32, (l, l), 1)
    gcol = gcol_ref[...]
    grow = grow_ref[0]
    bcol = _u01dot((c <= r).astype(BF16), gcol)
    brow = _dot01(grow, (r <= c).astype(BF16))
    return gcol, grow, bcol, brow, r >= c


def _ml_chunk(h, dh, mq_ref, mk_ref, v_ref, gates, cp, n_prev, m_prev):
    gcol, grow, bcol, brow, tri = gates
    l = CHUNK
    sl = slice(h * dh, (h + 1) * dh)
    qc = mq_ref[:, sl]
    kc = mk_ref[:, sl] * (dh ** -0.5)
    vc = v_ref[:, sl]
    i_row = grow[h:h + 1, :]
    i_col = gcol[:, h:h + 1]
    b_col = bcol[:, ML_HEADS + h:ML_HEADS + h + 1]
    b_row = brow[ML_HEADS + h:ML_HEADS + h + 1, :]
    b_end = b_col[l - 1:l, :]
    d = jnp.where(tri, b_col - b_row + i_row, -jnp.inf)
    m_inter = b_col + m_prev
    m_t = jnp.maximum(m_inter, jnp.max(d, axis=1, keepdims=True))
    w = jnp.exp(d - m_t)
    s_inter = jnp.exp(m_inter - m_t)
    qb, kb, vb = qc.astype(BF16), kc.astype(BF16), vc.astype(BF16)
    cpb = cp.astype(BF16)
    a = lax.dot_general(qb, kb, NT, preferred_element_type=F32)
    sc = a * w
    qcp = lax.dot_general(qb, cpb, NT, preferred_element_type=F32)
    qn = jnp.sum(qc * n_prev, axis=1, keepdims=True)
    num = lax.dot_general(sc.astype(BF16), vb, NN, preferred_element_type=F32) + s_inter * qcp
    den = jnp.sum(sc, axis=1, keepdims=True) + s_inter * qn
    floor = jnp.exp(-m_t)
    dnm = jnp.maximum(jnp.abs(den), floor)
    g_col = b_end - b_col + i_col
    g_row = b_end - b_row + i_row
    m_new = jnp.maximum(b_end + m_prev, jnp.max(g_row, axis=1, keepdims=True))
    decay = jnp.exp(b_end + m_prev - m_new)
    wk = jnp.exp(g_col - m_new)
    return dict(qc=qc, kc=kc, vc=vc, qb=qb, kb=kb, vb=vb, cpb=cpb, w=w, s_inter=s_inter, a=a, sc=sc, qcp=qcp, qn=qn,
                num=num, den=den, floor=floor, dnm=dnm, m_new=m_new, decay=decay, wk=wk, sl=sl)


def _ml_fwd(mqk, zm, vcol, gcol, grow, d_model, *, name):
    s = zm.shape[0]
    nc = s // CHUNK
    dh = d_model // ML_HEADS
    hh = ML_HEADS

    def body(mq_ref, mk_ref, v_ref, gcol_ref, grow_ref, h_ref, cs_ref, ns_ref, ms_ref, c_s, n_s, m_s):
        @pl.when(pl.program_id(0) == 0)
        def _():
            c_s[...] = jnp.zeros_like(c_s)
            n_s[...] = jnp.zeros_like(n_s)
            m_s[...] = jnp.zeros_like(m_s)

        gates = _ml_gates(gcol_ref, grow_ref)
        for h in range(hh):
            cp, n_prev, m_prev = c_s[h], n_s[h], m_s[h][:, 0:1]
            cs_ref[0, h] = cp
            ns_ref[0, h] = n_prev
            ms_ref[0, h] = m_s[h]
            f = _ml_chunk(h, dh, mq_ref, mk_ref, v_ref, gates, cp, n_prev, m_prev)
            h_ref[:, f["sl"]] = f["num"] / f["dnm"]
            c_s[h] = f["decay"] * cp + lax.dot_general((f["vc"] * f["wk"]).astype(BF16), f["kb"], TN,
                                                       preferred_element_type=F32)
            n_s[h] = f["decay"] * n_prev + jnp.sum(f["wk"] * f["kc"], axis=0, keepdims=True)
            m_s[h] = jnp.broadcast_to(f["m_new"], (1, LANES))

    dblk = d_model
    return pl.pallas_call(
        body, name=name, grid=(nc,),
        in_specs=[pl.BlockSpec((CHUNK, dblk), lambda c: (c, 0)), pl.BlockSpec((CHUNK, dblk), lambda c: (c, 1)),
                  pl.BlockSpec((CHUNK, dblk), lambda c: (c, vcol // dblk)),
                  pl.BlockSpec((CHUNK, LANES), lambda c: (c, 0)), pl.BlockSpec((1, 8, CHUNK), lambda c: (c, 0, 0))],
        out_specs=[pl.BlockSpec((CHUNK, dblk), lambda c: (c, 0)),
                   pl.BlockSpec((1, hh, dh, dh), lambda c: (c, 0, 0, 0)),
                   pl.BlockSpec((1, hh, 1, dh), lambda c: (c, 0, 0, 0)),
                   pl.BlockSpec((1, hh, 1, LANES), lambda c: (c, 0, 0, 0))],
        out_shape=[pltpu.HBM((s, d_model), F32), pltpu.HBM((nc, hh, dh, dh), F32),
                   pltpu.HBM((nc, hh, 1, dh), F32), pltpu.HBM((nc, hh, 1, LANES), F32)],
        scratch_shapes=[pltpu.VMEM((hh, dh, dh), F32), pltpu.VMEM((hh, 1, dh), F32), pltpu.VMEM((hh, 1, LANES), F32)],
        compiler_params=_params(8 * hh * dh * dh * 4 + (16 << 20)),
    )(_hbm(mqk), _hbm(mqk), _hbm(zm), _hbm(gcol), _hbm(grow))


def _ml_bwd(mqk, zm, vcol, gcol, grow, cs, ns, ms, dhm, d_model, *, name):
    s = zm.shape[0]
    nc = s // CHUNK
    dh = d_model // ML_HEADS
    hh = ML_HEADS
    l = CHUNK

    def body(mq_ref, mk_ref, v_ref, gcol_ref, grow_ref, cs_ref, ns_ref, ms_ref, dh_ref,
             dqk_ref, dv_ref, dgc_ref, dgr_ref, dc_s, dn_s):
        @pl.when(pl.program_id(0) == 0)
        def _():
            dc_s[...] = jnp.zeros_like(dc_s)
            dn_s[...] = jnp.zeros_like(dn_s)

        gates = _ml_gates(gcol_ref, grow_ref)
        lane = lax.broadcasted_iota(jnp.int32, (l, LANES), 1)
        rowi = lax.broadcasted_iota(jnp.int32, (8, l), 0)
        lastrow = lax.broadcasted_iota(jnp.int32, (l, 1), 0) == l - 1
        dgc = jnp.zeros((l, LANES), F32)
        dgr = jnp.zeros((8, l), F32)
        for h in range(hh):
            cp, n_prev, m_prev = cs_ref[0, h], ns_ref[0, h], ms_ref[0, h][:, 0:1]
            f = _ml_chunk(h, dh, mq_ref, mk_ref, v_ref, gates, cp, n_prev, m_prev)
            dC, dn = dc_s[h], dn_s[h]
            dhv = dh_ref[:, f["sl"]]
            dnum = dhv / f["dnm"]
            hv = f["num"] / f["dnm"]
            ddnm = -jnp.sum(dhv * hv, axis=1, keepdims=True) / f["dnm"]
            dden = jnp.where(jnp.abs(f["den"]) >= f["floor"], ddnm * jnp.sign(f["den"]), 0.0)
            dnb = dnum.astype(BF16)
            dsc = lax.dot_general(dnb, f["vb"], NT, preferred_element_type=F32) + dden
            dvc = lax.dot_general(f["sc"].astype(BF16), dnb, TN, preferred_element_type=F32)
            ds_inter = jnp.sum(dnum * f["qcp"], axis=1, keepdims=True) + dden * f["qn"]
            sdn = (f["s_inter"] * dnum).astype(BF16)
            sdd = f["s_inter"] * dden
            da = dsc * f["w"]
            dab = da.astype(BF16)
            dqc = (lax.dot_general(dab, f["kb"], NN, preferred_element_type=F32)
                   + lax.dot_general(sdn, f["cpb"], NN, preferred_element_type=F32) + sdd * n_prev)
            dcp = f["decay"] * dC + lax.dot_general(sdn, f["qb"], TN, preferred_element_type=F32)
            dnp = f["decay"] * dn + jnp.sum(sdd * f["qc"], axis=0, keepdims=True)
            vw = (f["vc"] * f["wk"]).astype(BF16)
            dCb = dC.astype(BF16)
            dkc = (lax.dot_general(dab, f["qb"], TN, preferred_element_type=F32)
                   + lax.dot_general(vw, dCb, NN, preferred_element_type=F32) + f["wk"] * dn)
            e = lax.dot_general(f["kb"], dCb, NT, preferred_element_type=F32)
            dvc = dvc + e * f["wk"]
            dwk = jnp.sum(e * f["vc"], axis=1, keepdims=True) + jnp.sum(f["kc"] * dn, axis=1, keepdims=True)
            ddecay = jnp.sum(jnp.sum(dC * cp, axis=1, keepdims=True), axis=0, keepdims=True) \
                + jnp.sum(dn * n_prev, axis=1, keepdims=True)
            dd = dsc * f["sc"]
            dlw = dwk * f["wk"]
            db_end = jnp.sum(dlw, axis=0, keepdims=True) + ddecay * f["decay"]
            di_col = dlw
            db_col = jnp.sum(dd, axis=1, keepdims=True) + ds_inter * f["s_inter"] - dlw \
                + jnp.where(lastrow, db_end, 0.0)
            cs_dd = jnp.sum(dd, axis=0, keepdims=True)
            dgc = dgc + jnp.where(lane == h, di_col, 0.0) + jnp.where(lane == hh + h, db_col, 0.0)
            dgr = dgr + jnp.where(rowi == h, cs_dd, 0.0) - jnp.where(rowi == hh + h, cs_dd, 0.0)
            dqk_ref[:, f["sl"]] = dqc
            dqk_ref[:, d_model + h * dh:d_model + (h + 1) * dh] = dkc * (dh ** -0.5)
            dv_ref[:, f["sl"]] = dvc.astype(dv_ref.dtype)
            dc_s[h] = dcp
            dn_s[h] = dnp
        dgc_ref[...] = dgc
        dgr_ref[0] = dgr

    dblk = d_model
    rev = lambda c: nc - 1 - c
    return pl.pallas_call(
        body, name=name, grid=(nc,),
        in_specs=[pl.BlockSpec((l, dblk), lambda c: (rev(c), 0)), pl.BlockSpec((l, dblk), lambda c: (rev(c), 1)),
                  pl.BlockSpec((l, dblk), lambda c: (rev(c), vcol // dblk)),
                  pl.BlockSpec((l, LANES), lambda c: (rev(c), 0)), pl.BlockSpec((1, 8, l), lambda c: (rev(c), 0, 0)),
                  pl.BlockSpec((1, hh, dh, dh), lambda c: (rev(c), 0, 0, 0)),
                  pl.BlockSpec((1, hh, 1, dh), lambda c: (rev(c), 0, 0, 0)),
                  pl.BlockSpec((1, hh, 1, LANES), lambda c: (rev(c), 0, 0, 0)),
                  pl.BlockSpec((l, dblk), lambda c: (rev(c), 0))],
        out_specs=[pl.BlockSpec((l, 2 * dblk), lambda c: (rev(c), 0)),
                   pl.BlockSpec((l, dblk), lambda c: (rev(c), 0)), pl.BlockSpec((l, LANES), lambda c: (rev(c), 0)),
                   pl.BlockSpec((1, 8, l), lambda c: (rev(c), 0, 0))],
        out_shape=[pltpu.HBM((s, 2 * d_model), F32),
                   pltpu.HBM((s, d_model), BF16), pltpu.HBM((s, LANES), F32),
                   pltpu.HBM((nc, 8, l), F32)],
        scratch_shapes=[pltpu.VMEM((hh, dh, dh), F32), pltpu.VMEM((hh, 1, dh), F32)],
        compiler_params=_params(10 * hh * dh * dh * 4 + (16 << 20)),
    )(*[_hbm(a) for a in (mqk, mqk, zm, gcol, grow, cs, ns, ms, dhm)])


def _xa_fwd(zm, qcol, kv, gq, gk, d_model, *, name, tq=256):
    s = zm.shape[0]
    nm = kv.shape[0]
    dh = d_model // X_HEADS
    tq = _pick(s, (tq, 128, 64))
    scale = dh ** -0.5

    def body(q_ref, k_ref, v_ref, gq_ref, gk_ref, o_ref):
        qn = _rms_fwd(q_ref[...], gq_ref[...])
        kn = _rms_fwd(k_ref[...], gk_ref[...])
        lg = _dot(qn, kn, NT) * scale
        lg = lg - jnp.max(lg, axis=1, keepdims=True)
        p = jnp.exp(lg)
        p = p / jnp.sum(p, axis=1, keepdims=True)
        o_ref[...] = _dot(p, v_ref[...], NN).astype(o_ref.dtype)

    return pl.pallas_call(
        body, name=name, grid=(X_HEADS, s // tq),
        in_specs=[pl.BlockSpec((tq, dh), lambda h, i: (i, qcol // dh + h)), pl.BlockSpec((nm, dh), lambda h, i: (0, h)),
                  pl.BlockSpec((nm, dh), lambda h, i: (0, X_HEADS + h)),
                  pl.BlockSpec((1, dh), lambda h, i: (0, 0)), pl.BlockSpec((1, dh), lambda h, i: (0, 0))],
        out_specs=pl.BlockSpec((tq, dh), lambda h, i: (i, h)),
        out_shape=pltpu.HBM((s, d_model), BF16),
        compiler_params=_params(32 << 20),
    )(_hbm(zm), _hbm(kv), _hbm(kv), gq, gk)


def _xa_bwd(zm, qcol, kv, gq, gk, dy, d_model, *, name, tq=256):
    s = zm.shape[0]
    nm = kv.shape[0]
    dh = d_model // X_HEADS
    tq = _pick(s, (tq, 128, 64))
    nq = s // tq
    scale = dh ** -0.5

    def body(q_ref, k_ref, v_ref, gq_ref, gk_ref, do_ref, dq_ref, dkn_ref, dv_ref, dgq_ref):
        h, i = pl.program_id(0), pl.program_id(1)

        @pl.when(i == 0)
        def _():
            dkn_ref[...] = jnp.zeros_like(dkn_ref)
            dv_ref[...] = jnp.zeros_like(dv_ref)

        @pl.when((i == 0) & (h == 0))
        def _():
            dgq_ref[...] = jnp.zeros_like(dgq_ref)

        q = q_ref[...]
        qn = _rms_fwd(q, gq_ref[...])
        kn = _rms_fwd(k_ref[...], gk_ref[...])
        lg = _dot(qn, kn, NT) * scale
        lg = lg - jnp.max(lg, axis=1, keepdims=True)
        p = jnp.exp(lg)
        p = p / jnp.sum(p, axis=1, keepdims=True)
        do = do_ref[...]
        dv_ref[...] += _dot(p, do, TN)
        dp = _dot(do, v_ref[...], NT)
        dlg = p * (dp - jnp.sum(dp * p, axis=1, keepdims=True)) * scale
        dqn = _dot(dlg, kn, NN)
        dkn_ref[...] += _dot(dlg, qn, TN)
        dq, dgq = _rms_bwd(q, gq_ref[...], dqn)
        dq_ref[...] = dq.astype(dq_ref.dtype)
        dgq_ref[...] += jnp.sum(dgq, axis=0, keepdims=True)

    return pl.pallas_call(
        body, name=name, grid=(X_HEADS, nq),
        in_specs=[pl.BlockSpec((tq, dh), lambda h, i: (i, qcol // dh + h)), pl.BlockSpec((nm, dh), lambda h, i: (0, h)),
                  pl.BlockSpec((nm, dh), lambda h, i: (0, X_HEADS + h)),
                  pl.BlockSpec((1, dh), lambda h, i: (0, 0)), pl.BlockSpec((1, dh), lambda h, i: (0, 0)),
                  pl.BlockSpec((tq, dh), lambda h, i: (i, h))],
        out_specs=[pl.BlockSpec((tq, dh), lambda h, i: (i, h)), pl.BlockSpec((nm, dh), lambda h, i: (0, h)),
                   pl.BlockSpec((nm, dh), lambda h, i: (0, h)), pl.BlockSpec((1, dh), lambda h, i: (0, 0))],
        out_shape=[pltpu.HBM((s, d_model), BF16), pltpu.HBM((nm, d_model), F32),
                   pltpu.HBM((nm, d_model), F32), pltpu.HBM((1, dh), F32)],
        compiler_params=_params(32 << 20),
    )(_hbm(zm), _hbm(kv), _hbm(kv), gq, gk, _hbm(dy))


def _place():
    return lax.axis_index("x"), lax.axis_index("y"), lax.axis_index("c")


ANY = pl.BlockSpec(memory_space=pl.ANY)


def _allgather_two_level(big, small, *, name, chunk_rows=64):
    r = big.shape[0]
    half = r // 2
    nr = _pick(half, (chunk_rows, 32, 16))
    nq = half // nr

    def body(big_ref, small_ref, obig, osmall, send, recv, fsend, frecv, ssend, srecv, loc):
        x, y, c = _place()
        k = 2 * x + y
        chips = [(1 - x, y), (x, 1 - y), (1 - x, 1 - y)]
        own = [pltpu.make_async_copy(big_ref, obig.at[k], loc.at[0]),
               pltpu.make_async_copy(small_ref, osmall.at[k], loc.at[1])]
        for cp in own:
            cp.start()

        def rows(h, q):
            return pl.ds(pl.multiple_of(h * half + q * nr, nr), nr)

        def over_ici(j, q, slot, h):
            return pltpu.make_async_remote_copy(
                src_ref=big_ref.at[rows(h, q)], dst_ref=obig.at[slot, rows(h, q)], send_sem=send.at[nq * j + q],
                recv_sem=recv.at[nq * j + q], device_id=(chips[j][0], chips[j][1], c), device_id_type=MESH)

        def to_sibling(j, q, h):
            slot = 2 * chips[j][0] + chips[j][1]
            return pltpu.make_async_remote_copy(
                src_ref=obig.at[slot, rows(h, q)], dst_ref=obig.at[slot, rows(h, q)], send_sem=fsend.at[nq * j + q],
                recv_sem=frecv.at[nq * j + q], device_id=(x, y, 1 - c), device_id_type=MESH)

        def small_copy(j, slot):
            return pltpu.make_async_remote_copy(
                src_ref=small_ref, dst_ref=osmall.at[slot], send_sem=ssend.at[j], recv_sem=srecv.at[j],
                device_id=(chips[j][0], chips[j][1], c), device_id_type=MESH)

        for q in range(nq):
            for j in range(3):
                over_ici(j, q, k, c).start()
        for j in range(3):
            small_copy(j, k).start()
        for q in range(nq):
            for j in range(3):
                over_ici(j, q, 2 * chips[j][0] + chips[j][1], c).wait_recv()
                to_sibling(j, q, c).start()
        for q in range(nq):
            for j in range(3):
                to_sibling(j, q, 1 - c).wait_recv()
        for j in range(3):
            small_copy(j, 2 * chips[j][0] + chips[j][1]).wait_recv()
            small_copy(j, k).wait_send()
        for q in range(nq):
            for j in range(3):
                over_ici(j, q, k, c).wait_send()
                to_sibling(j, q, c).wait_send()
        for cp in own:
            cp.wait()

    return pl.pallas_call(
        body, name=name, in_specs=[ANY] * 2, out_specs=[ANY] * 2,
        out_shape=[pltpu.HBM((4,) + big.shape, big.dtype), pltpu.HBM((4,) + small.shape, small.dtype)],
        scratch_shapes=[pltpu.SemaphoreType.DMA((3 * nq,))] * 4
        + [pltpu.SemaphoreType.DMA((3,)), pltpu.SemaphoreType.DMA((3,)), pltpu.SemaphoreType.DMA((2,))],
    )(big, small)


HBM_SPEC = pl.BlockSpec(memory_space=pltpu.HBM)
SEM_SPEC = pl.BlockSpec(memory_space=pltpu.SEMAPHORE)
EFFECT = pltpu.SideEffectType.DATAFLOW_SIDE_EFFECTING


def _split_copies(kind, srcs, lands, send, recv):
    x, y, c = _place()
    if kind == "quarters":
        peers = [(1 - x, y, c), (x, 1 - y, c), (1 - x, 1 - y, c)]
    else:
        peers = [(x ^ ((j >> 2) & 1), y ^ ((j >> 1) & 1), c ^ (j & 1)) for j in range(1, 8)]
    npeer = len(peers)
    out = []
    for t in range(len(srcs)):
        for j, (px, py, pc) in enumerate(peers):
            if kind == "quarters":
                src, mine, theirs = srcs[t], 2 * x + y, 2 * px + py
            else:
                src, mine, theirs = srcs[t].at[2 * px + py, pc], 4 * x + 2 * y + c, 4 * px + 2 * py + pc
            mk = functools.partial(
                pltpu.make_async_remote_copy, src_ref=src, send_sem=send.at[npeer * t + j],
                recv_sem=recv.at[npeer * t + j], device_id=(px, py, pc), device_id_type=MESH)
            out.append((functools.partial(mk, dst_ref=lands[t].at[mine]),
                        functools.partial(mk, dst_ref=lands[t].at[theirs])))
    return out


def _split_start(kind, srcs, land_shapes, after, *, name):
    n = len(srcs)
    ncopies = n * (3 if kind == "quarters" else 7)

    def body(*refs):
        ins, lands = refs[:n], refs[n:2 * n]
        send, recv = refs[2 * n + 1], refs[2 * n + 2]
        token = refs[-1]
        for start, _ in _split_copies(kind, ins, lands, send, recv):
            start().start()
        token[...] = jnp.zeros_like(token)

    lands = [_hbm(lax.empty(shp, a.dtype)) for shp, a in zip(land_shapes, srcs)]
    res = pl.pallas_call(
        body, name=name, in_specs=[HBM_SPEC] * (2 * n) + [ANY],
        out_specs=[SEM_SPEC, SEM_SPEC] + [HBM_SPEC] * (2 * n) + [pl.BlockSpec(memory_space=pltpu.VMEM)],
        out_shape=[pltpu.SemaphoreType.DMA((ncopies,)), pltpu.SemaphoreType.DMA((ncopies,))]
        + [pltpu.HBM(a.shape, a.dtype) for a in srcs] + [pltpu.HBM(shp, a.dtype) for shp, a in zip(land_shapes, srcs)]
        + [jax.ShapeDtypeStruct((8, LANES), F32)],
        input_output_aliases={i: 2 + i for i in range(2 * n)},
        compiler_params=pltpu.CompilerParams(has_side_effects=EFFECT),
    )(*[_hbm(a) for a in srcs], *lands, after)
    return res[0], res[1], list(res[2:2 + n]), list(res[2 + n:2 + 2 * n]), res[-1]


def _split_wait(kind, send, recv, srcs, lands, after, *, name):
    n = len(srcs)

    def body(*refs):
        ins, lnd = refs[:n], refs[n:2 * n]
        snd, rcv = refs[2 * n], refs[2 * n + 1]
        for start, arrive in _split_copies(kind, ins, lnd, snd, rcv):
            start().wait_send()
            arrive().wait_recv()

    res = pl.pallas_call(
        body, name=name, in_specs=[HBM_SPEC] * (2 * n) + [SEM_SPEC, SEM_SPEC] + [ANY] * len(after),
        out_specs=[HBM_SPEC] * (2 * n),
        out_shape=[pltpu.HBM(a.shape, a.dtype) for a in srcs] + [pltpu.HBM(a.shape, a.dtype) for a in lands],
        input_output_aliases={i: i for i in range(2 * n)},
        compiler_params=pltpu.CompilerParams(has_side_effects=EFFECT),
    )(*srcs, *lands, send, recv, *after)
    return list(res[n:])


def _sum8(parts, *, name):
    _, r, c = parts.shape
    t = _pick(r, (128, 64, 32, 16, 8))

    def body(p_ref, o_ref):
        acc = p_ref[0].astype(F32)
        for k in range(1, 8):
            acc = acc + p_ref[k].astype(F32)
        o_ref[...] = acc

    return pl.pallas_call(
        body, name=name, grid=(r // t,), in_specs=[pl.BlockSpec((8, t, c), lambda i: (0, i, 0))],
        out_specs=pl.BlockSpec((t, c), lambda i: (i, 0)), out_shape=pltpu.HBM((r, c), F32),
        compiler_params=_params(2 * 8 * t * c * 2 + 6 * t * c * 4 + (4 << 20)),
    )(_hbm(parts))


def _swap_halves(halves, *, name, chunk_bytes=512 * 1024):
    n = len(halves)
    items = []
    for t, a in enumerate(halves):
        r = a.shape[0]
        k = 1
        while _nbytes(a.shape, a.dtype) // k > chunk_bytes and r % (2 * k) == 0 and (r // (2 * k)) % 8 == 0:
            k *= 2
        items += [(t, q * (r // k), r // k) for q in range(k)]
    m = len(items)

    def body(*refs):
        ins, outs = refs[:n], refs[n:2 * n]
        sbuf, rbuf = refs[2 * n:3 * n], refs[3 * n:4 * n]
        send, recv, loc_own, loc_in, loc_out = refs[4 * n:]
        x, y, c = _place()
        local, stage = [], []
        for t in range(n):
            cp = pltpu.make_async_copy(ins[t], outs[t].at[c], loc_own.at[t])
            cp.start()
            local.append(cp)
        for q, (t, r0, nr) in enumerate(items):
            cp = pltpu.make_async_copy(ins[t].at[pl.ds(r0, nr)], sbuf[t].at[pl.ds(r0, nr)], loc_in.at[q])
            cp.start()
            stage.append(cp)

        def copy(q):
            t, r0, nr = items[q]
            return pltpu.make_async_remote_copy(
                src_ref=sbuf[t].at[pl.ds(r0, nr)], dst_ref=rbuf[t].at[pl.ds(r0, nr)], send_sem=send.at[q],
                recv_sem=recv.at[q], device_id=(x, y, 1 - c), device_id_type=MESH)

        for q in range(m):
            stage[q].wait()
            copy(q).start()
        for q, (t, r0, nr) in enumerate(items):
            copy(q).wait_recv()
            cp = pltpu.make_async_copy(rbuf[t].at[pl.ds(r0, nr)], outs[t].at[1 - c, pl.ds(r0, nr)], loc_out.at[q])
            cp.start()
            local.append(cp)
        for q in range(m):
            copy(q).wait_send()
        for cp in local:
            cp.wait()

    stage_bytes = 2 * sum(_nbytes(a.shape, a.dtype) for a in halves)
    return pl.pallas_call(
        body, name=name, in_specs=[ANY] * n, out_specs=[ANY] * n,
        out_shape=[pltpu.HBM((2,) + a.shape, a.dtype) for a in halves],
        scratch_shapes=[pltpu.VMEM(a.shape, a.dtype) for a in halves] * 2
        + [pltpu.SemaphoreType.DMA((m,)), pltpu.SemaphoreType.DMA((m,)), pltpu.SemaphoreType.DMA((n,)),
           pltpu.SemaphoreType.DMA((m,)), pltpu.SemaphoreType.DMA((m,))],
        compiler_params=_params(stage_bytes + (4 << 20)),
    )(*halves)


def _allreduce_small(p, after, *, name):
    r = p.shape[0]

    def body(p_ref, after_ref, o_ref, buf, send, recv):
        x, y, c = _place()
        me = 4 * x + 2 * y + c
        peers = [(x ^ ((j >> 2) & 1), y ^ ((j >> 1) & 1), c ^ (j & 1)) for j in range(1, 8)]

        def copy(j, slot):
            return pltpu.make_async_remote_copy(
                src_ref=p_ref, dst_ref=buf.at[slot], send_sem=send.at[j], recv_sem=recv.at[j],
                device_id=peers[j], device_id_type=MESH)

        for j in range(7):
            copy(j, me).start()
        buf[me] = p_ref[...]
        for j in range(7):
            px, py, pc = peers[j]
            copy(j, 4 * px + 2 * py + pc).wait_recv()
        for j in range(7):
            copy(j, me).wait_send()
        acc = buf[0]
        for k in range(1, 8):
            acc = acc + buf[k]
        o_ref[...] = acc

    vspec = pl.BlockSpec(memory_space=pltpu.VMEM)
    return pl.pallas_call(
        body, name=name, in_specs=[vspec, ANY], out_specs=vspec, out_shape=jax.ShapeDtypeStruct((r, LANES), F32),
        scratch_shapes=[pltpu.VMEM((8, r, LANES), F32), pltpu.SemaphoreType.DMA((7,)), pltpu.SemaphoreType.DMA((7,))],
    )(p, after)


def _adamw_fn(w, g, m, v):
    m = ADAM_B1 * m + (1.0 - ADAM_B1) * g
    v = ADAM_B2 * v + (1.0 - ADAM_B2) * (g * g)
    m_hat = m / (1.0 - ADAM_B1 ** ADAM_STEP)
    v_hat = v / (1.0 - ADAM_B2 ** ADAM_STEP)
    delta = -ADAM_LR * (m_hat / (jnp.sqrt(v_hat) + ADAM_EPS) + ADAM_WD * w)
    return delta, m, v


def _adamw(w, g, m, v, *, name):
    c = w.shape[1]
    return _rowwise(_adamw_fn, [w, g, m, v], [], [(c, F32)] * 3, name=name, tr=128)


def _pack(vecs, rows):
    flat = jnp.concatenate([a.reshape(-1).astype(F32) for a in vecs])
    return jnp.pad(flat, (0, rows * LANES - flat.shape[0])).reshape(rows, LANES)


def _unpack(p, like):
    flat, out, o = p.reshape(-1), [], 0
    for a in like:
        out.append(flat[o:o + a.size].reshape(a.shape))
        o += a.size
    return out


def kernel(x, mem, g_mix, w_in, b_if, b_gate, conv_w, conv_b, ml_norm_g, g_mem, w_mem_kv, q_norm_g, k_norm_g, w_sb_proj, w_ml_proj, w_x_proj, w_out, g_mlp, w_ff1, w_ff2, loss_target, m_g_mix, m_w_in, m_b_if, m_b_gate, m_conv_w, m_conv_b, m_ml_norm_g, m_g_mem, m_w_mem_kv, m_q_norm_g, m_k_norm_g, m_w_sb_proj, m_w_ml_proj, m_w_x_proj, m_w_out, m_g_mlp, m_w_ff1, m_w_ff2, v_g_mix, v_w_in, v_b_if, v_b_gate, v_conv_w, v_conv_b, v_ml_norm_g, v_g_mem, v_w_mem_kv, v_q_norm_g, v_k_norm_g, v_w_sb_proj, v_w_ml_proj, v_w_x_proj, v_w_out, v_g_mlp, v_w_ff1, v_w_ff2):
    _, s, d = x.shape
    nm = mem.shape[1]
    n_in = 4 * w_in.shape[2]
    dff = 4 * w_ff1.shape[2]
    sbh = d // SB_HD
    hh = ML_HEADS
    dh = d // hh
    nc = s // CHUNK
    assert n_in == 11 * d + 2 * hh and d % (2 * LANES) == 0 and s % LANES == 0
    x2, mem2, tgt = x[0], mem[0], loss_target[0]

    k4 = 2 * lax.axis_index("x") + lax.axis_index("y")
    me = 2 * k4 + lax.axis_index("c")
    qn = w_in.shape[2]
    qp = -(-qn // 256) * 256
    to_t = lambda a: jnp.pad(jnp.transpose(a[0]), ((0, qp - qn), (0, 0)))
    from_t = lambda a: jnp.transpose(a[:qn])[None]
    g_first = _allgather_two_level(to_t(w_in).astype(BF16), conv_w[0], name="gather_w_in")
    later = [a[0].astype(BF16) for a in (w_mem_kv, w_sb_proj, w_ml_proj, w_x_proj, w_out, w_ff1, w_ff2)]
    gw_send, gw_recv, gw_src, gw_land, gw_token = _split_start(
        "quarters", later, [(4,) + a.shape for a in later], g_first[0], name="gather_rest_start")
    cols = lambda a: a.transpose(1, 0, 2).reshape(a.shape[1], 4 * a.shape[2])
    rws = lambda a: a.reshape(4 * a.shape[1], a.shape[2])
    w_in_t = g_first[0][:, :qn].reshape(4 * qn, d)
    w_main_t = jnp.concatenate([w_in_t[:7 * d], w_in_t[7 * d + 2 * hh:]], axis=0)
    w_if_t = jnp.pad(w_in_t[7 * d:7 * d + 2 * hh], ((0, LANES - 2 * hh), (0, 0)))
    conv_wf = cols(g_first[1])
    b_if_p = jnp.pad(b_if, ((0, 0), (0, LANES - 2 * hh)))

    (hn,) = _rowwise(_rms_fwd, [x2], [g_mix], [(d, BF16)], name="norm_in")
    zm = _mm(hn, w_main_t, tb=True, after=gw_token, name="proj_in")
    zif = _mm(hn, w_if_t, tb=True, name="proj_if")
    y_sb, ltot = _sb_fwd(zm, sbh, name="sb_fwd")

    def gate_fn(z, b):
        pre = z + b
        lane = lax.broadcasted_iota(jnp.int32, pre.shape, 1)
        return jnp.where(lane < hh, pre, -_softplus(-pre))

    (gcol,) = _rowwise(gate_fn, [zif], [b_if_p], [(LANES, F32)], name="ml_gates")
    grow = gcol[:, :8].T.reshape(8, nc, CHUNK).transpose(1, 0, 2)
    mqk = _conv_fwd(zm, 3 * d, 2 * d, conv_wf, conv_b, name="conv_fwd")
    hm, cst, nst, mst = _ml_fwd(mqk, zm, 5 * d, gcol, grow, d, name="ml_fwd")

    def mlout_fn(hv, o, g):
        ys = [_rms_fwd(hv[:, k * dh:(k + 1) * dh], g[:, k * dh:(k + 1) * dh]) for k in range(hh)]
        return jnp.concatenate(ys, axis=1) * _sigmoid(o)

    (y_ml,) = _rowwise(mlout_fn, [hm, (zm, d, 6)], [ml_norm_g], [(d, BF16)], name="ml_out")
    gw_land = _split_wait("quarters", gw_send, gw_recv, gw_src, gw_land, [y_ml, y_sb], name="gather_rest_wait")
    gw = [lax.dynamic_update_index_in_dim(ld, a, k4, 0) for ld, a in zip(gw_land, later)]
    w_kv, w_sbp, w_mlp, w_xp, w_o, w_f1, w_f2 = (cols(gw[0]), rws(gw[1]), rws(gw[2]), rws(gw[3]), rws(gw[4]),
                                                 cols(gw[5]), rws(gw[6]))
    (memn,) = _rowwise(_rms_fwd, [mem2], [g_mem], [(d, BF16)], name="norm_mem")
    kv = _mm(memn, w_kv, name="proj_kv")
    y_x = _xa_fwd(zm, 7 * d, kv, q_norm_g, k_norm_g, d, name="xa_fwd")
    p_sb = _mm(y_sb, w_sbp, name="proj_sb")
    p_ml = _mm(y_ml, w_mlp, name="proj_ml")
    p_x = _mm(y_x, w_xp, name="proj_x")

    def merge_fn(a, b, c, g0, g1, g2, bg):
        return (_sigmoid(g0 + bg[:, :d]) * a + _sigmoid(g1 + bg[:, d:2 * d]) * b + _sigmoid(g2 + bg[:, 2 * d:]) * c)

    gate_cols = [(zm, d, 8), (zm, d, 9), (zm, d, 10)]
    (mixed,) = _rowwise(merge_fn, [p_sb, p_ml, p_x] + gate_cols, [b_gate], [(d, BF16)], name="merge")
    x1 = _mm(mixed, w_o, add=x2, name="proj_out")
    (h2,) = _rowwise(_rms_fwd, [x1], [g_mlp], [(d, BF16)], name="norm_mlp")
    u = _mm(h2, w_f1, name="ff1")
    (act,) = _rowwise(lambda uv: jnp.square(jnp.maximum(uv, 0.0)), [u], [], [(dff, BF16)], name="relu2", tr=128)
    yo = _mm(act, w_f2, add=x1, name="ff2")

    def loss_fn(yv, tv):
        e = yv - tv
        return e * (1.0 / d), jnp.sum(e * e, axis=0, keepdims=True) * (0.5 / d)

    dy, loss_cols = _rowwise(loss_fn, [yo, tgt], [], [(d, F32)], [d], name="loss")

    dact = _mm(dy, w_f2, tb=True, name="ff2_dx")
    dw_f2 = _mm(act, dy, ta=True, name="ff2_dw")
    (du,) = _rowwise(lambda g, uv: g * 2.0 * jnp.maximum(uv, 0.0), [dact, u], [], [(dff, BF16)], name="relu2_bwd",
                     tr=128)
    dw_f1 = _mm(h2, du, ta=True, name="ff1_dw")
    dh2 = _mm(du, w_f1, tb=True, name="ff1_dx")

    def norm_bwd_fn(xv, dyv, res, g):
        dx, dg = _rms_bwd(xv, g, dyv)
        return dx + res, jnp.sum(dg, axis=0, keepdims=True)

    dx1, dg_mlp = _rowwise(norm_bwd_fn, [x1, dh2, dy], [g_mlp], [(d, F32)], [d], name="norm_mlp_bwd")
    dmixed = _mm(dx1, w_o, tb=True, name="proj_out_dx")
    dw_o = _mm(mixed, dx1, ta=True, name="proj_out_dw")

    def merge_bwd_fn(dm, a, b, c, g0, g1, g2, bg):
        outs, dgs = [], []
        for p, g, k in ((a, g0, 0), (b, g1, 1), (c, g2, 2)):
            sg = _sigmoid(g + bg[:, k * d:(k + 1) * d])
            outs.append(dm * sg)
            dgs.append(dm * p * sg * (1.0 - sg))
        dgate = jnp.concatenate(dgs, axis=1)
        return (*outs, dgate, jnp.sum(dgate, axis=0, keepdims=True))

    dp_sb, dp_ml, dp_x, dgate, db_gate = _rowwise(
        merge_bwd_fn, [dmixed, p_sb, p_ml, p_x] + gate_cols, [b_gate], [(d, BF16)] * 3 + [(3 * d, BF16)], [3 * d],
        name="merge_bwd", tr=128)
    dw_sbp = _mm(y_sb, dp_sb, ta=True, name="proj_sb_dw")
    dw_mlp = _mm(y_ml, dp_ml, ta=True, name="proj_ml_dw")
    dw_xp = _mm(y_x, dp_x, ta=True, name="proj_x_dw")
    dy_sb = _mm(dp_sb, w_sbp, tb=True, out_dtype=BF16, name="proj_sb_dx")
    dy_ml = _mm(dp_ml, w_mlp, tb=True, name="proj_ml_dx")
    dy_x = _mm(dp_x, w_xp, tb=True, out_dtype=BF16, name="proj_x_dx")

    uncols = lambda a: a.reshape(a.shape[0], 4, a.shape[1] // 4).transpose(1, 0, 2)
    unrws = lambda a: a.reshape(4, a.shape[0] // 4, a.shape[1])
    to_parts = lambda q: q.astype(BF16).reshape(4, 2, q.shape[1] // 2, q.shape[2])
    early = [to_parts(q) for q in (unrws(dw_sbp), unrws(dw_mlp), unrws(dw_xp), unrws(dw_o), uncols(dw_f1),
                                   unrws(dw_f2))]
    ge_send, ge_recv, ge_src, ge_land, ge_token = _split_start(
        "grads", early, [(8,) + a.shape[2:] for a in early], dy_x, name="exchange_early_start")

    dsq, dsk, dsv = _sb_bwd(zm, dy_sb, ltot, ge_token, sbh, name="sb_bwd")

    def mlout_bwd_fn(dyv, hv, o, g):
        sg = _sigmoid(o)
        dn = dyv * sg
        dxs, dgs, ys = [], [], []
        for k in range(hh):
            sl = slice(k * dh, (k + 1) * dh)
            ys.append(_rms_fwd(hv[:, sl], g[:, sl]))
            dxk, dgk = _rms_bwd(hv[:, sl], g[:, sl], dn[:, sl])
            dxs.append(dxk)
            dgs.append(dgk)
        do = dyv * jnp.concatenate(ys, axis=1) * sg * (1.0 - sg)
        return jnp.concatenate(dxs, axis=1), do, jnp.sum(jnp.concatenate(dgs, axis=1), axis=0, keepdims=True)

    dhm, dmlo, dg_mln = _rowwise(mlout_bwd_fn, [dy_ml, hm, (zm, d, 6)], [ml_norm_g], [(d, F32), (d, BF16)], [d],
                                 name="ml_out_bwd")
    dmqk, dmlv, dgc, dgr = _ml_bwd(mqk, zm, 5 * d, gcol, grow, cst, nst, mst, dhm, d, name="ml_bwd")
    dmlqk, dconv_w, dconv_b = _conv_bwd(zm, 3 * d, 2 * d, conv_wf, conv_b, dmqk, name="conv_bwd")
    dgr_t = jnp.pad(dgr.transpose(1, 0, 2).reshape(8, s).T, ((0, 0), (0, LANES - 8)))

    def gate_bwd_fn(a, b, z, bias):
        tot = a + b
        rows_t = tot.shape[0]
        r = lax.broadcasted_iota(jnp.int32, (rows_t, rows_t), 0)
        c = lax.broadcasted_iota(jnp.int32, (rows_t, rows_t), 1)
        sh = CHUNK.bit_length() - 1
        same_chunk = jnp.right_shift(r, sh) == jnp.right_shift(c, sh)
        dlf = _u01dot(((c >= r) & same_chunk).astype(BF16), tot)
        lane = lax.broadcasted_iota(jnp.int32, tot.shape, 1)
        dz = jnp.where(lane < hh, tot, jnp.where(lane < 2 * hh, dlf * _sigmoid(-(z + bias)), 0.0))
        return dz, jnp.sum(dz, axis=0, keepdims=True)

    dzif, db_if_p = _rowwise(gate_bwd_fn, [dgc, dgr_t, zif], [b_if_p], [(LANES, BF16)], [LANES], name="ml_gates_bwd",
                             tr=8 * CHUNK)
    dxq, dkn, dxv, dg_qn = _xa_bwd(zm, 7 * d, kv, q_norm_g, k_norm_g, dy_x, d, name="xa_bwd")

    def knorm_bwd_fn(kvv, dknv, dvv, g):
        dks, dgs = [], []
        for k in range(X_HEADS):
            sl = slice(k * dh, (k + 1) * dh)
            dk, dg = _rms_bwd(kvv[:, sl], g, dknv[:, sl])
            dks.append(dk)
            dgs.append(jnp.sum(dg, axis=0, keepdims=True))
        return jnp.concatenate(dks + [dvv], axis=1), dgs[0] + dgs[1] + dgs[2] + dgs[3]

    dkv, dg_kn = _rowwise(knorm_bwd_fn, [(kv, d, 0), dkn, dxv], [k_norm_g], [(2 * d, BF16)], [dh], name="xa_knorm_bwd")
    dw_kv = _mm(memn, dkv, ta=True, name="proj_kv_dw")
    dmemn = _mm(dkv, w_kv, tb=True, name="proj_kv_dx")

    def gmem_fn(mv, dv_, g):
        _, dg = _rms_bwd(mv, g, dv_)
        return (jnp.sum(dg, axis=0, keepdims=True),)

    (dg_mem,) = _rowwise(gmem_fn, [mem2, dmemn], [g_mem], [], [d], name="norm_mem_bwd")

    dzm = jnp.concatenate([dsq, dsk, dsv, dmlqk, dmlv, dmlo, dxq, dgate], axis=1)
    dw_main_t = _mm(dzm, hn, ta=True, out_dtype=BF16, name="proj_in_dw")
    dw_if_t = _mm(dzif, hn, ta=True, out_dtype=BF16, name="proj_if_dw")
    dw_in_t = jnp.concatenate([dw_main_t[:7 * d], dw_if_t[:2 * hh], dw_main_t[7 * d:]], axis=0)
    dw_in_q = jnp.pad(dw_in_t.reshape(4, qn, d), ((0, 0), (0, qp - qn), (0, 0))).reshape(4, 2, qp // 2, d)
    late = [dw_in_q, to_parts(uncols(dw_kv))]
    gl_send, gl_recv, gl_src, gl_land, gl_token = _split_start(
        "grads", late, [(8,) + a.shape[2:] for a in late], dw_if_t, name="exchange_late_start")
    dhn = _mm(dzm, w_main_t, after=gl_token, name="proj_in_dx")
    dhn = _mm(dzif, w_if_t, add=dhn, name="proj_if_dx")
    dx, dg_mix = _rowwise(norm_bwd_fn, [x2, dhn, dx1], [g_mix], [(d, F32)], [d], name="norm_in_bwd")

    own = lambda p: lax.dynamic_index_in_dim(lax.dynamic_index_in_dim(p, k4, 0, keepdims=False),
                                             lax.axis_index("c"), 0, keepdims=False)

    def finish(tag, send, recv, src, land, parts, after, ws, ms, vs):
        land = _split_wait("grads", send, recv, src, land, after, name=f"exchange_{tag}_wait")
        got = [lax.dynamic_update_index_in_dim(ld, own(p), me, 0) for ld, p in zip(land, parts)]
        halves = [_sum8(r, name=f"sum_grads_{tag}{i}") for i, r in enumerate(got)]
        both = _swap_halves(halves, name=f"swap_halves_{tag}")
        gs = [b.reshape(2 * b.shape[1], b.shape[2]) for b in both]
        return gs, [_adamw(w, g, m, v, name=f"adamw_{tag}{i}") for i, (w, g, m, v) in enumerate(zip(ws, gs, ms, vs))]

    first = lambda arrs: [a[0] for a in arrs]
    g_early, out_early = finish(
        "early", ge_send, ge_recv, ge_src, ge_land, early, [dx],
        first([w_sb_proj, w_ml_proj, w_x_proj, w_out, w_ff1, w_ff2]),
        first([m_w_sb_proj, m_w_ml_proj, m_w_x_proj, m_w_out, m_w_ff1, m_w_ff2]),
        first([v_w_sb_proj, v_w_ml_proj, v_w_x_proj, v_w_out, v_w_ff1, v_w_ff2]))
    g_late, out_late = finish(
        "late", gl_send, gl_recv, gl_src, gl_land, late, [o[0] for o in out_early],
        [to_t(w_in), w_mem_kv[0]], [to_t(m_w_in), m_w_mem_kv[0]], [to_t(v_w_in), v_w_mem_kv[0]])
    g_big = [from_t(g_late[0]), g_late[1][None]] + [g[None] for g in g_early]
    big_out = [[from_t(o) for o in out_late[0]], [o[None] for o in out_late[1]]] \
        + [[o[None] for o in outs] for outs in out_early]

    small_g = [dg_mix, db_if_p[:, :2 * hh], db_gate, dconv_w, dconv_b, dg_mln, dg_mem, dg_qn, dg_kn, dg_mlp,
               jnp.sum(loss_cols).reshape(1, 1)]
    n_small = sum(a.size for a in small_g)
    rows = -(-n_small // (8 * LANES)) * 8
    g_small = _unpack(_allreduce_small(_pack(small_g, rows), out_late[0][0], name="allreduce_small"), small_g)
    loss = g_small[-1].reshape(())
    qw = conv_w.shape[2]
    g_conv_w = lax.dynamic_slice_in_dim(g_small[3], k4 * qw, qw, axis=1)
    g_small_w = [g_small[0], g_small[1], g_small[2], g_conv_w] + g_small[4:10]
    sm_w = [g_mix, b_if, b_gate, conv_w[0], conv_b, ml_norm_g, g_mem, q_norm_g, k_norm_g, g_mlp]
    sm_m = [m_g_mix, m_b_if, m_b_gate, m_conv_w[0], m_conv_b, m_ml_norm_g, m_g_mem, m_q_norm_g, m_k_norm_g, m_g_mlp]
    sm_v = [v_g_mix, v_b_if, v_b_gate, v_conv_w[0], v_conv_b, v_ml_norm_g, v_g_mem, v_q_norm_g, v_k_norm_g, v_g_mlp]
    n_sw = sum(a.size for a in sm_w)
    rows_w = -(-n_sw // (8 * LANES)) * 8
    sm_out = _adamw(_pack(sm_w, rows_w), _pack(g_small_w, rows_w), _pack(sm_m, rows_w), _pack(sm_v, rows_w),
                    name="adamw_small")
    sm_delta, sm_newm, sm_newv = [_unpack(p, sm_w) for p in sm_out]

    order = ["g_mix", "w_in", "b_if", "b_gate", "conv_w", "conv_b", "ml_norm_g", "g_mem", "w_mem_kv", "q_norm_g",
             "k_norm_g", "w_sb_proj", "w_ml_proj", "w_x_proj", "w_out", "g_mlp", "w_ff1", "w_ff2"]
    small_names = ["g_mix", "b_if", "b_gate", "conv_w", "conv_b", "ml_norm_g", "g_mem", "q_norm_g", "k_norm_g", "g_mlp"]
    big_names = ["w_in", "w_mem_kv", "w_sb_proj", "w_ml_proj", "w_x_proj", "w_out", "w_ff1", "w_ff2"]
    grads, deltas, new_m, new_v = {}, {}, {}, {}
    for i, nme in enumerate(small_names):
        shp = sm_w[i].shape if nme != "conv_w" else conv_w.shape
        grads[nme] = g_small_w[i].reshape(shp)
        deltas[nme], new_m[nme], new_v[nme] = (sm_delta[i].reshape(shp), sm_newm[i].reshape(shp),
                                               sm_newv[i].reshape(shp))
    for i, nme in enumerate(big_names):
        grads[nme] = g_big[i]
        deltas[nme], new_m[nme], new_v[nme] = big_out[i]
    return (loss, dx[None], *[grads[k] for k in order], *[deltas[k] for k in order], *[new_m[k] for k in order],
            *[new_v[k] for k in order])
```

```python
import functools

import jax
import jax.numpy as jnp
from jax import lax
from jax.experimental import pallas as pl
from jax.experimental.pallas import tpu as pltpu

F32 = jnp.float32
BF16 = jnp.bfloat16
MESH = pl.DeviceIdType.MESH

EPS = 1e-6
SB_HD = 128
ML_HEADS = 4
X_HEADS = 4
CHUNK = 64
CONV_W = 4
LANES = 128
ADAM_LR = 0.001
ADAM_B1 = 0.9
ADAM_B2 = 0.999
ADAM_EPS = 1e-08
ADAM_WD = 0.01
ADAM_STEP = 10
VMEM_CAP = 56 * 1024 * 1024
NEG = -1e30

NT = (((1,), (1,)), ((), ()))
NN = (((1,), (0,)), ((), ()))
TN = (((0,), (0,)), ((), ()))


def _dot(a, b, dn=NN):
    return lax.dot_general(a.astype(BF16), b.astype(BF16), dn, preferred_element_type=F32)


def _dot01(x, u, dn=NN):
    hi = x.astype(BF16)
    lo = (x - hi.astype(F32)).astype(BF16)
    return (lax.dot_general(hi, u, dn, preferred_element_type=F32)
            + lax.dot_general(lo, u, dn, preferred_element_type=F32))


def _u01dot(u, x):
    hi = x.astype(BF16)
    lo = (x - hi.astype(F32)).astype(BF16)
    return (lax.dot_general(u, hi, NN, preferred_element_type=F32)
            + lax.dot_general(u, lo, NN, preferred_element_type=F32))


def _pick(n, cands):
    for c in cands:
        if c <= n and n % c == 0:
            return c
    return n


def _nbytes(shape, dtype):
    n = 1
    for s in shape:
        n *= s
    return n * jnp.dtype(dtype).itemsize


def _params(vmem_bytes):
    return pltpu.CompilerParams(vmem_limit_bytes=int(min(VMEM_CAP, max(vmem_bytes, 16 * 1024 * 1024))))


def _hbm(a):
    return pltpu.with_memory_space_constraint(a, pltpu.HBM)


def _softplus(z):
    return jnp.maximum(z, 0.0) + jnp.log(1.0 + jnp.exp(-jnp.abs(z)))


def _sigmoid(z):
    return 1.0 / (1.0 + jnp.exp(-z))


def _rms_fwd(xv, g):
    r = lax.rsqrt(jnp.mean(xv * xv, axis=-1, keepdims=True) + EPS)
    return xv * r * g


def _rms_bwd(xv, g, dy):
    r = lax.rsqrt(jnp.mean(xv * xv, axis=-1, keepdims=True) + EPS)
    xh = xv * r
    dxh = dy * g
    dx = r * (dxh - xh * jnp.mean(dxh * xh, axis=-1, keepdims=True))
    return dx, dy * xh


def _mm(a, b, *, name, ta=False, tb=False, tiles=(), post=None, out_dtype=F32, bm=1024, bn=1024, bk=1024, after=None):
    m, k = (a.shape[1], a.shape[0]) if ta else a.shape
    n = b.shape[0] if tb else b.shape[1]
    tm = _pick(m, (bm, 512, 256, 128))
    tn = _pick(n, (bn, 512, 256, 128))
    tk = _pick(k, (bk, 512, 256, 128))
    nk = k // tk
    dn = (((0 if ta else 1,), (1 if tb else 0,)), ((), ()))
    dts = out_dtype if isinstance(out_dtype, tuple) else (out_dtype,)
    nt, no = len(tiles), len(dts)
    if post is None:
        post = lambda r, *ts: sum((t.astype(F32) for t in ts), r)

    def body(*refs):
        a_ref, b_ref = refs[:2]
        t_refs = refs[2:2 + nt]
        o_refs = refs[2 + nt + (after is not None):2 + nt + (after is not None) + no]
        part = lax.dot_general(a_ref[...].astype(BF16), b_ref[...].astype(BF16), dn, preferred_element_type=F32)

        def finish(r):
            res = post(r, *[t[...] for t in t_refs])
            res = res if isinstance(res, tuple) else (res,)
            for o, v in zip(o_refs, res):
                o[...] = v.astype(o.dtype)

        if nk == 1:
            finish(part)
        else:
            acc_ref = refs[-1]
            kk = pl.program_id(2)

            @pl.when(kk == 0)
            def _():
                acc_ref[...] = part

            @pl.when(kk > 0)
            def _():
                acc_ref[...] += part

            @pl.when(kk == nk - 1)
            def _():
                finish(acc_ref[...])

    a_spec = pl.BlockSpec((tk, tm), lambda i, j, q: (q, i)) if ta else pl.BlockSpec((tm, tk), lambda i, j, q: (i, q))
    b_spec = pl.BlockSpec((tn, tk), lambda i, j, q: (j, q)) if tb else pl.BlockSpec((tk, tn), lambda i, j, q: (q, j))
    o_spec = pl.BlockSpec((tm, tn), lambda i, j, q: (i, j))
    ins, specs = [_hbm(a), _hbm(b)] + [_hbm(t) for t in tiles], [a_spec, b_spec] + [o_spec] * nt
    vm = 2 * (_nbytes((tm, tk), a.dtype) + _nbytes((tk, tn), b.dtype)) + 3 * _nbytes((tm, tn), F32) \
        + _nbytes((tm, tk), BF16) + _nbytes((tk, tn), BF16) \
        + 2 * sum(_nbytes((tm, tn), t.dtype) for t in tiles) + 2 * sum(_nbytes((tm, tn), dt) for dt in dts)
    if after is not None:
        ins.append(after)
        specs.append(ANY)
    res = pl.pallas_call(
        body, name=name, grid=(m // tm, n // tn, nk), in_specs=specs, out_specs=[o_spec] * no,
        out_shape=[pltpu.HBM((m, n), dt) for dt in dts], scratch_shapes=[pltpu.VMEM((tm, tn), F32)] if nk > 1 else [],
        compiler_params=_params(vm + (4 << 20)),
    )(*ins)
    return res[0] if no == 1 else tuple(res)


def _mm_pieces(pieces, other, *, name, side, out_dtype=F32, after=None, blk=1024):
    r = pieces[0].shape[0]
    widths = [p.shape[1] for p in pieces]
    tw = _pick(min(widths), (blk // 2, 256, 128))
    assert all(w % tw == 0 for w in widths)
    starts, nblk, acc = [], [], 0
    for w in widths:
        starts.append(acc)
        nblk.append(w // tw)
        acc += w // tw
    np_ = len(pieces)
    if side == "k":
        n = other.shape[0]
        tm, tn = _pick(r, (blk, 512, 256, 128)), _pick(n, (blk, 512, 256, 128))
        grid, nk = (r // tm, n // tn, acc), acc
        dn = NT
        o_shape, o_spec = (r, n), pl.BlockSpec((tm, tn), lambda i, j, q: (i, j))
        other_spec = pl.BlockSpec((tn, tw), lambda i, j, q: (j, q))
        which = lambda i, j, q: q

        def piece_spec(s0, nb):
            return pl.BlockSpec((tm, tw), lambda i, j, q: (i, jnp.clip(q - s0, 0, nb - 1)))
    else:
        m = other.shape[1]
        tm, tk = _pick(m, (blk, 512, 256, 128)), _pick(r, (blk, 512, 256, 128))
        grid, nk = (m // tm, acc, r // tk), r // tk
        dn = TN
        o_shape, o_spec = (m, acc * tw), pl.BlockSpec((tm, tw), lambda i, j, q: (i, j))
        other_spec = pl.BlockSpec((tk, tm), lambda i, j, q: (q, i))
        which = lambda i, j, q: j

        def piece_spec(s0, nb):
            return pl.BlockSpec((tk, tw), lambda i, j, q: (jnp.where((j >= s0) & (j < s0 + nb), q, 0),
                                                           jnp.clip(j - s0, 0, nb - 1)))

    def body(*refs):
        p_refs, o_ref_in = refs[:np_], refs[np_]
        out_ref, acc_ref = refs[np_ + 1 + (after is not None)], refs[-1]
        sel = which(pl.program_id(0), pl.program_id(1), pl.program_id(2))
        kk = pl.program_id(2)
        for p_ref, s0, nb in zip(p_refs, starts, nblk):
            @pl.when((sel >= s0) & (sel < s0 + nb))
            def _():
                if side == "k":
                    part = lax.dot_general(p_ref[...].astype(BF16), o_ref_in[...].astype(BF16), dn,
                                           preferred_element_type=F32)
                else:
                    part = lax.dot_general(o_ref_in[...].astype(BF16), p_ref[...].astype(BF16), dn,
                                           preferred_element_type=F32)

                @pl.when(kk == 0)
                def _():
                    acc_ref[...] = part

                @pl.when(kk > 0)
                def _():
                    acc_ref[...] += part

        @pl.when(kk == nk - 1)
        def _():
            out_ref[...] = acc_ref[...].astype(out_dtype)

    ins = [_hbm(p) for p in pieces] + [_hbm(other)]
    specs = [piece_spec(s0, nb) for s0, nb in zip(starts, nblk)] + [other_spec]
    if after is not None:
        ins.append(after)
        specs.append(ANY)
    vm = sum(2 * _nbytes(sp.block_shape, a.dtype) for sp, a in zip(specs, pieces + [other])) \
        + 6 * _nbytes(o_spec.block_shape, F32)
    return pl.pallas_call(
        body, name=name, grid=grid, in_specs=specs, out_specs=o_spec, out_shape=pltpu.HBM(o_shape, out_dtype),
        scratch_shapes=[pltpu.VMEM(o_spec.block_shape, F32)],
        compiler_params=_params(vm + (4 << 20)),
    )(*ins)


def _rowwise(fn, rows, consts, outs, reds=(), *, name, tr=256, temps=6):
    rows = [r if isinstance(r, tuple) else (r, r.shape[1], 0) for r in rows]
    nrows = rows[0][0].shape[0]
    t = _pick(nrows, (tr, 128, 64, 32, 16, 8))
    nr, nc, no = len(rows), len(consts), len(outs)

    def body(*refs):
        rin, cin = refs[:nr], refs[nr:nr + nc]
        oref, rref = refs[nr + nc:nr + nc + no], refs[nr + nc + no:]
        res = fn(*[r[...] for r in rin], *[c[...] for c in cin])
        if not isinstance(res, (tuple, list)):
            res = (res,)
        for o, v in zip(oref, res[:no]):
            o[...] = v.astype(o.dtype)
        if rref:
            @pl.when(pl.program_id(0) == 0)
            def _():
                for r in rref:
                    r[...] = jnp.zeros_like(r)

            for r, v in zip(rref, res[no:]):
                r[...] += v

    in_specs = [pl.BlockSpec((t, w), functools.partial(lambda i, ci: (i, ci), ci=ci)) for (_, w, ci) in rows]
    in_specs += [pl.BlockSpec(c.shape, functools.partial(lambda i, nd: (0,) * nd, nd=c.ndim)) for c in consts]
    out_specs = [pl.BlockSpec((t, w), lambda i: (i, 0)) for (w, _) in outs]
    out_specs += [pl.BlockSpec((1, w), lambda i: (0, 0)) for w in reds]
    out_shape = [pltpu.HBM((nrows, w), dt) for (w, dt) in outs]
    out_shape += [jax.ShapeDtypeStruct((1, w), F32) for w in reds]
    widest = max([w for (_, w, _) in rows] + [w for (w, _) in outs])
    vm = 2 * sum(_nbytes((t, w), a.dtype) for (a, w, _) in rows) + 2 * sum(_nbytes((t, w), dt) for (w, dt) in outs)
    vm += temps * _nbytes((t, widest), F32) + (2 << 20)
    res = pl.pallas_call(
        body, name=name, grid=(nrows // t,), in_specs=in_specs, out_specs=out_specs, out_shape=out_shape,
        compiler_params=_params(vm),
    )(*[_hbm(a) for (a, _, _) in rows], *consts)
    return list(res)


def _sb_tiles(s, tq, tk):
    tq = _pick(s, (tq, 256, 128))
    tk = _pick(tq, (tk, 128))
    return tq, tk, tq // tk


def _sb_fwd(zm, heads, *, name, tq=512, tk=256):
    s = zm.shape[0]
    tq, tk, nd = _sb_tiles(s, tq, tk)
    scale = SB_HD ** -0.5

    def body(q_ref, k_ref, v_ref, o_ref, lt_ref):
        i = pl.program_id(1)
        qb = (q_ref[...] * scale).astype(BF16)
        r = lax.broadcasted_iota(jnp.int32, (tq, tk), 0)
        c = lax.broadcasted_iota(jnp.int32, (tq, tk), 1)
        ur = lax.broadcasted_iota(jnp.int32, (tk, tk), 0)
        uc = lax.broadcasted_iota(jnp.int32, (tk, tk), 1)
        usuf = (ur > uc).astype(BF16)

        def tile(j, carry, causal):
            acc, cl = carry
            rows = pl.ds(pl.multiple_of(j * tk, tk), tk)
            kb = k_ref[rows, :].astype(BF16)
            vb = v_ref[rows, :].astype(BF16)
            z = lax.dot_general(qb, kb, NT, preferred_element_type=F32)
            lsig = -_softplus(z)
            l = lsig if causal is None else jnp.where(causal, lsig, 0.0)
            loga = z + lsig + _dot01(l, usuf) + cl
            if causal is not None:
                loga = jnp.where(causal, loga, NEG)
            a = jnp.exp(loga)
            acc = acc + lax.dot_general(a.astype(BF16), vb, NN, preferred_element_type=F32)
            return acc, cl + jnp.sum(l, axis=1, keepdims=True)

        carry = (jnp.zeros((tq, SB_HD), F32), jnp.zeros((tq, 1), F32))
        for dd in range(nd - 1, -1, -1):
            carry = tile(i * nd + dd, carry, c + dd * tk < r)
        acc, cl = lax.fori_loop(0, i * nd, lambda n, cr: tile(i * nd - 1 - n, cr, None), carry)
        o_ref[...] = acc.astype(o_ref.dtype)
        lt_ref[...] = jnp.broadcast_to(cl, (tq, LANES))

    blk = lambda off: pl.BlockSpec((s, SB_HD), functools.partial(lambda h, i, off: (0, off + h), off=off))
    return pl.pallas_call(
        body, name=name, grid=(heads, s // tq),
        in_specs=[pl.BlockSpec((tq, SB_HD), lambda h, i: (i, h)), blk(heads), blk(2 * heads)],
        out_specs=[pl.BlockSpec((tq, SB_HD), lambda h, i: (i, h)), pl.BlockSpec((tq, LANES), lambda h, i: (i, h))],
        out_shape=[pltpu.HBM((s, heads * SB_HD), BF16), pltpu.HBM((s, heads * LANES), F32)],
        compiler_params=_params(8 * s * SB_HD * 4 + 24 * tq * tk * 4 + (8 << 20)),
    )(_hbm(zm), _hbm(zm), _hbm(zm))


def _sb_bwd(zm, dy, ltot, after, heads, *, name, tq=512, tk=256):
    s = zm.shape[0]
    tq, tk, nd = _sb_tiles(s, tq, tk)
    nq = s // tq
    scale = SB_HD ** -0.5

    def body(q_ref, k_ref, v_ref, do_ref, lt_ref, after_ref, dq_ref, dk_ref, dv_ref, dka, dva):
        i = pl.program_id(1)

        @pl.when(i == 0)
        def _():
            dka[...] = jnp.zeros_like(dka)
            dva[...] = jnp.zeros_like(dva)

        qb = (q_ref[...] * scale).astype(BF16)
        dob = do_ref[...].astype(BF16)
        ltot_c = lt_ref[:, 0:1]
        r = lax.broadcasted_iota(jnp.int32, (tq, tk), 0)
        c = lax.broadcasted_iota(jnp.int32, (tq, tk), 1)
        ur = lax.broadcasted_iota(jnp.int32, (tk, tk), 0)
        uc = lax.broadcasted_iota(jnp.int32, (tk, tk), 1)
        uincl = (ur <= uc).astype(BF16)
        uexcl = (ur < uc).astype(BF16)

        def tile(j, carry, causal):
            dq, cl, cg = carry
            rows = pl.ds(pl.multiple_of(j * tk, tk), tk)
            kb = k_ref[rows, :].astype(BF16)
            vb = v_ref[rows, :].astype(BF16)
            z = lax.dot_general(qb, kb, NT, preferred_element_type=F32)
            lsig = -_softplus(z)
            l = lsig if causal is None else jnp.where(causal, lsig, 0.0)
            later = ltot_c - (cl + _dot01(l, uincl))
            loga = z + lsig + later
            if causal is not None:
                loga = jnp.where(causal, loga, NEG)
            a = jnp.exp(loga)
            sig = jnp.exp(z + lsig)
            g = a * lax.dot_general(dob, vb, NT, preferred_element_type=F32)
            p = cg + lax.dot_general(g.astype(BF16), uexcl, NN, preferred_element_type=F32)
            dz = g - sig * (g + p)
            if causal is not None:
                dz = jnp.where(causal, dz, 0.0)
            dzb = dz.astype(BF16)
            dva[rows, :] += lax.dot_general(a.astype(BF16), dob, TN, preferred_element_type=F32)
            dka[rows, :] += lax.dot_general(dzb, qb, TN, preferred_element_type=F32)
            dq = dq + lax.dot_general(dzb, kb, NN, preferred_element_type=F32)
            return dq, cl + jnp.sum(l, axis=1, keepdims=True), cg + jnp.sum(g, axis=1, keepdims=True)

        init = (jnp.zeros((tq, SB_HD), F32), jnp.zeros((tq, 1), F32), jnp.zeros((tq, 1), F32))
        carry = lax.fori_loop(0, i * nd, lambda j, cr: tile(j, cr, None), init)
        for dd in range(nd):
            carry = tile(i * nd + dd, carry, c + dd * tk < r)
        dq_ref[...] = (carry[0] * scale).astype(dq_ref.dtype)

        @pl.when(i == nq - 1)
        def _():
            dk_ref[...] = dka[...].astype(dk_ref.dtype)
            dv_ref[...] = dva[...].astype(dv_ref.dtype)

    blk = lambda off: pl.BlockSpec((s, SB_HD), functools.partial(lambda h, i, off: (0, off + h), off=off))
    tile_spec = pl.BlockSpec((tq, SB_HD), lambda h, i: (i, h))
    full = pltpu.HBM((s, heads * SB_HD), BF16)
    return pl.pallas_call(
        body, name=name, grid=(heads, nq),
        in_specs=[tile_spec, blk(heads), blk(2 * heads), tile_spec, pl.BlockSpec((tq, LANES), lambda h, i: (i, h)),
                  ANY],
        out_specs=[tile_spec, blk(0), blk(0)],
        out_shape=[full, full, full],
        scratch_shapes=[pltpu.VMEM((s, SB_HD), F32), pltpu.VMEM((s, SB_HD), F32)],
        compiler_params=_params(12 * s * SB_HD * 4 + 32 * tq * tk * 4 + (8 << 20)),
    )(_hbm(zm), _hbm(zm), _hbm(zm), _hbm(dy), _hbm(ltot), after)


def _conv_taps(u, w_ref, rows_i):
    taps = []
    for j in range(CONV_W):
        sh = CONV_W - 1 - j
        if sh == 0:
            taps.append(u)
        else:
            taps.append(jnp.where(rows_i >= sh, pltpu.roll(u, sh, 0), 0.0))
    return taps


def _conv_fwd(zm, col0, width, cw, cb, *, name):
    s = zm.shape[0]
    bw = _pick(width, (LANES,))
    off = col0 // bw

    def body(u_ref, w_ref, b_ref, o_ref):
        u = u_ref[...]
        rows_i = lax.broadcasted_iota(jnp.int32, u.shape, 0)
        acc = jnp.broadcast_to(b_ref[...], u.shape)
        for j, tp in enumerate(_conv_taps(u, w_ref, rows_i)):
            acc = acc + tp * w_ref[j:j + 1, :]
        o_ref[...] = acc * _sigmoid(acc)

    return pl.pallas_call(
        body, name=name, grid=(width // bw,),
        in_specs=[pl.BlockSpec((s, bw), lambda j: (0, off + j)), pl.BlockSpec((CONV_W, bw), lambda j: (0, j)),
                  pl.BlockSpec((1, bw), lambda j: (0, j))],
        out_specs=pl.BlockSpec((s, bw), lambda j: (0, j)),
        out_shape=pltpu.HBM((s, width), F32),
        compiler_params=_params(12 * s * bw * 4 + (4 << 20)),
    )(_hbm(zm), cw, cb)


def _conv_bwd(zm, col0, width, cw, cb, dqk, *, name):
    s = zm.shape[0]
    bw = _pick(width, (LANES,))
    off = col0 // bw

    def body(u_ref, w_ref, b_ref, d_ref, du_ref, dw_ref, db_ref):
        u = u_ref[...]
        rows_i = lax.broadcasted_iota(jnp.int32, u.shape, 0)
        taps = _conv_taps(u, w_ref, rows_i)
        acc = jnp.broadcast_to(b_ref[...], u.shape)
        for j, tp in enumerate(taps):
            acc = acc + tp * w_ref[j:j + 1, :]
        sg = _sigmoid(acc)
        dc = d_ref[...] * (sg * (1.0 + acc * (1.0 - sg)))
        du = jnp.zeros_like(u)
        for j in range(CONV_W):
            sh = CONV_W - 1 - j
            if sh == 0:
                du = du + dc * w_ref[j:j + 1, :]
            else:
                du = du + jnp.where(rows_i < s - sh, pltpu.roll(dc, s - sh, 0), 0.0) * w_ref[j:j + 1, :]
            dw_ref[j:j + 1, :] = jnp.sum(dc * taps[j], axis=0, keepdims=True)
        du_ref[...] = du.astype(du_ref.dtype)
        db_ref[...] = jnp.sum(dc, axis=0, keepdims=True)

    return pl.pallas_call(
        body, name=name, grid=(width // bw,),
        in_specs=[pl.BlockSpec((s, bw), lambda j: (0, off + j)), pl.BlockSpec((CONV_W, bw), lambda j: (0, j)),
                  pl.BlockSpec((1, bw), lambda j: (0, j)), pl.BlockSpec((s, bw), lambda j: (0, j))],
        out_specs=[pl.BlockSpec((s, bw), lambda j: (0, j)), pl.BlockSpec((CONV_W, bw), lambda j: (0, j)),
                   pl.BlockSpec((1, bw), lambda j: (0, j))],
        out_shape=[pltpu.HBM((s, width), BF16), pltpu.HBM((CONV_W, width), F32),
                   pltpu.HBM((1, width), F32)],
        compiler_params=_params(20 * s * bw * 4 + (4 << 20)),
    )(_hbm(zm), cw, cb, _hbm(dqk))


def _ml_gates(gcol_ref, grow_ref):
    l = CHUNK
    r = lax.broadcasted_iota(jnp.int32, (l, l), 0)
    c = lax.broadcasted_iota(jnp.int32, (l, l), 1)
    gcol = gcol_ref[...]
    grow = grow_ref[0]
    bcol = _u01dot((c <= r).astype(BF16), gcol)
    brow = _dot01(grow, (r <= c).astype(BF16))
    return gcol, grow, bcol, brow, r >= c


def _ml_chunk(h, dh, mq_ref, mk_ref, v_ref, gates, cp, n_prev, m_prev):
    gcol, grow, bcol, brow, tri = gates
    l = CHUNK
    sl = slice(h * dh, (h + 1) * dh)
    qc = mq_ref[:, sl]
    kc = mk_ref[:, sl] * (dh ** -0.5)
    vc = v_ref[:, sl]
    i_row = grow[h:h + 1, :]
    i_col = gcol[:, h:h + 1]
    b_col = bcol[:, ML_HEADS + h:ML_HEADS + h + 1]
    b_row = brow[ML_HEADS + h:ML_HEADS + h + 1, :]
    b_end = b_col[l - 1:l, :]
    d = jnp.where(tri, b_col - b_row + i_row, -jnp.inf)
    m_inter = b_col + m_prev
    m_t = jnp.maximum(m_inter, jnp.max(d, axis=1, keepdims=True))
    w = jnp.exp(d - m_t)
    s_inter = jnp.exp(m_inter - m_t)
    qb, kb, vb = qc.astype(BF16), kc.astype(BF16), vc.astype(BF16)
    cpb = cp.astype(BF16)
    a = lax.dot_general(qb, kb, NT, preferred_element_type=F32)
    sc = a * w
    qcp = lax.dot_general(qb, cpb, NT, preferred_element_type=F32)
    qn = jnp.sum(qc * n_prev, axis=1, keepdims=True)
    num = lax.dot_general(sc.astype(BF16), vb, NN, preferred_element_type=F32) + s_inter * qcp
    den = jnp.sum(sc, axis=1, keepdims=True) + s_inter * qn
    floor = jnp.exp(-m_t)
    dnm = jnp.maximum(jnp.abs(den), floor)
    g_col = b_end - b_col + i_col
    g_row = b_end - b_row + i_row
    m_new = jnp.maximum(b_end + m_prev, jnp.max(g_row, axis=1, keepdims=True))
    decay = jnp.exp(b_end + m_prev - m_new)
    wk = jnp.exp(g_col - m_new)
    return dict(qc=qc, kc=kc, vc=vc, qb=qb, kb=kb, vb=vb, cpb=cpb, w=w, s_inter=s_inter, a=a, sc=sc, qcp=qcp, qn=qn,
                num=num, den=den, floor=floor, dnm=dnm, m_new=m_new, decay=decay, wk=wk, sl=sl)


def _ml_fwd(mqk, zm, vcol, gcol, grow, d_model, *, name):
    s = zm.shape[0]
    nc = s // CHUNK
    dh = d_model // ML_HEADS
    hh = ML_HEADS

    def body(mq_ref, mk_ref, v_ref, gcol_ref, grow_ref, h_ref, cs_ref, ns_ref, ms_ref, c_s, n_s, m_s):
        @pl.when(pl.program_id(0) == 0)
        def _():
            c_s[...] = jnp.zeros_like(c_s)
            n_s[...] = jnp.zeros_like(n_s)
            m_s[...] = jnp.zeros_like(m_s)

        gates = _ml_gates(gcol_ref, grow_ref)
        for h in range(hh):
            cp, n_prev, m_prev = c_s[h], n_s[h], m_s[h][:, 0:1]
            cs_ref[0, h] = cp
            ns_ref[0, h] = n_prev
            ms_ref[0, h] = m_s[h]
            f = _ml_chunk(h, dh, mq_ref, mk_ref, v_ref, gates, cp, n_prev, m_prev)
            h_ref[:, f["sl"]] = f["num"] / f["dnm"]
            c_s[h] = f["decay"] * cp + lax.dot_general((f["vc"] * f["wk"]).astype(BF16), f["kb"], TN,
                                                       preferred_element_type=F32)
            n_s[h] = f["decay"] * n_prev + jnp.sum(f["wk"] * f["kc"], axis=0, keepdims=True)
            m_s[h] = jnp.broadcast_to(f["m_new"], (1, LANES))

    dblk = d_model
    return pl.pallas_call(
        body, name=name, grid=(nc,),
        in_specs=[pl.BlockSpec((CHUNK, dblk), lambda c: (c, 0)), pl.BlockSpec((CHUNK, dblk), lambda c: (c, 1)),
                  pl.BlockSpec((CHUNK, dblk), lambda c: (c, vcol // dblk)),
                  pl.BlockSpec((CHUNK, LANES), lambda c: (c, 0)), pl.BlockSpec((1, 8, CHUNK), lambda c: (c, 0, 0))],
        out_specs=[pl.BlockSpec((CHUNK, dblk), lambda c: (c, 0)),
                   pl.BlockSpec((1, hh, dh, dh), lambda c: (c, 0, 0, 0)),
                   pl.BlockSpec((1, hh, 1, dh), lambda c: (c, 0, 0, 0)),
                   pl.BlockSpec((1, hh, 1, LANES), lambda c: (c, 0, 0, 0))],
        out_shape=[pltpu.HBM((s, d_model), F32), pltpu.HBM((nc, hh, dh, dh), F32),
                   pltpu.HBM((nc, hh, 1, dh), F32), pltpu.HBM((nc, hh, 1, LANES), F32)],
        scratch_shapes=[pltpu.VMEM((hh, dh, dh), F32), pltpu.VMEM((hh, 1, dh), F32), pltpu.VMEM((hh, 1, LANES), F32)],
        compiler_params=_params(8 * hh * dh * dh * 4 + (16 << 20)),
    )(_hbm(mqk), _hbm(mqk), _hbm(zm), _hbm(gcol), _hbm(grow))


def _ml_bwd(mqk, zm, vcol, gcol, grow, cs, ns, ms, dhm, d_model, *, name):
    s = zm.shape[0]
    nc = s // CHUNK
    dh = d_model // ML_HEADS
    hh = ML_HEADS
    l = CHUNK

    def body(mq_ref, mk_ref, v_ref, gcol_ref, grow_ref, cs_ref, ns_ref, ms_ref, dh_ref,
             dqk_ref, dv_ref, dgc_ref, dgr_ref, dc_s, dn_s):
        @pl.when(pl.program_id(0) == 0)
        def _():
            dc_s[...] = jnp.zeros_like(dc_s)
            dn_s[...] = jnp.zeros_like(dn_s)

        gates = _ml_gates(gcol_ref, grow_ref)
        lane = lax.broadcasted_iota(jnp.int32, (l, LANES), 1)
        rowi = lax.broadcasted_iota(jnp.int32, (8, l), 0)
        lastrow = lax.broadcasted_iota(jnp.int32, (l, 1), 0) == l - 1
        dgc = jnp.zeros((l, LANES), F32)
        dgr = jnp.zeros((8, l), F32)
        for h in range(hh):
            cp, n_prev, m_prev = cs_ref[0, h], ns_ref[0, h], ms_ref[0, h][:, 0:1]
            f = _ml_chunk(h, dh, mq_ref, mk_ref, v_ref, gates, cp, n_prev, m_prev)
            dC, dn = dc_s[h], dn_s[h]
            dhv = dh_ref[:, f["sl"]]
            dnum = dhv / f["dnm"]
            hv = f["num"] / f["dnm"]
            ddnm = -jnp.sum(dhv * hv, axis=1, keepdims=True) / f["dnm"]
            dden = jnp.where(jnp.abs(f["den"]) >= f["floor"], ddnm * jnp.sign(f["den"]), 0.0)
            dnb = dnum.astype(BF16)
            dsc = lax.dot_general(dnb, f["vb"], NT, preferred_element_type=F32) + dden
            dvc = lax.dot_general(f["sc"].astype(BF16), dnb, TN, preferred_element_type=F32)
            ds_inter = jnp.sum(dnum * f["qcp"], axis=1, keepdims=True) + dden * f["qn"]
            sdn = (f["s_inter"] * dnum).astype(BF16)
            sdd = f["s_inter"] * dden
            da = dsc * f["w"]
            dab = da.astype(BF16)
            dqc = (lax.dot_general(dab, f["kb"], NN, preferred_element_type=F32)
                   + lax.dot_general(sdn, f["cpb"], NN, preferred_element_type=F32) + sdd * n_prev)
            dcp = f["decay"] * dC + lax.dot_general(sdn, f["qb"], TN, preferred_element_type=F32)
            dnp = f["decay"] * dn + jnp.sum(sdd * f["qc"], axis=0, keepdims=True)
            vw = (f["vc"] * f["wk"]).astype(BF16)
            dCb = dC.astype(BF16)
            dkc = (lax.dot_general(dab, f["qb"], TN, preferred_element_type=F32)
                   + lax.dot_general(vw, dCb, NN, preferred_element_type=F32) + f["wk"] * dn)
            e = lax.dot_general(f["kb"], dCb, NT, preferred_element_type=F32)
            dvc = dvc + e * f["wk"]
            dwk = jnp.sum(e * f["vc"], axis=1, keepdims=True) + jnp.sum(f["kc"] * dn, axis=1, keepdims=True)
            ddecay = jnp.sum(jnp.sum(dC * cp, axis=1, keepdims=True), axis=0, keepdims=True) \
                + jnp.sum(dn * n_prev, axis=1, keepdims=True)
            dd = dsc * f["sc"]
            dlw = dwk * f["wk"]
            db_end = jnp.sum(dlw, axis=0, keepdims=True) + ddecay * f["decay"]
            di_col = dlw
            db_col = jnp.sum(dd, axis=1, keepdims=True) + ds_inter * f["s_inter"] - dlw \
                + jnp.where(lastrow, db_end, 0.0)
            cs_dd = jnp.sum(dd, axis=0, keepdims=True)
            dgc = dgc + jnp.where(lane == h, di_col, 0.0) + jnp.where(lane == hh + h, db_col, 0.0)
            dgr = dgr + jnp.where(rowi == h, cs_dd, 0.0) - jnp.where(rowi == hh + h, cs_dd, 0.0)
            dqk_ref[:, f["sl"]] = dqc
            dqk_ref[:, d_model + h * dh:d_model + (h + 1) * dh] = dkc * (dh ** -0.5)
            dv_ref[:, f["sl"]] = dvc.astype(dv_ref.dtype)
            dc_s[h] = dcp
            dn_s[h] = dnp
        dgc_ref[...] = dgc
        dgr_ref[0] = dgr

    dblk = d_model
    rev = lambda c: nc - 1 - c
    return pl.pallas_call(
        body, name=name, grid=(nc,),
        in_specs=[pl.BlockSpec((l, dblk), lambda c: (rev(c), 0)), pl.BlockSpec((l, dblk), lambda c: (rev(c), 1)),
                  pl.BlockSpec((l, dblk), lambda c: (rev(c), vcol // dblk)),
                  pl.BlockSpec((l, LANES), lambda c: (rev(c), 0)), pl.BlockSpec((1, 8, l), lambda c: (rev(c), 0, 0)),
                  pl.BlockSpec((1, hh, dh, dh), lambda c: (rev(c), 0, 0, 0)),
                  pl.BlockSpec((1, hh, 1, dh), lambda c: (rev(c), 0, 0, 0)),
                  pl.BlockSpec((1, hh, 1, LANES), lambda c: (rev(c), 0, 0, 0)),
                  pl.BlockSpec((l, dblk), lambda c: (rev(c), 0))],
        out_specs=[pl.BlockSpec((l, 2 * dblk), lambda c: (rev(c), 0)),
                   pl.BlockSpec((l, dblk), lambda c: (rev(c), 0)), pl.BlockSpec((l, LANES), lambda c: (rev(c), 0)),
                   pl.BlockSpec((1, 8, l), lambda c: (rev(c), 0, 0))],
        out_shape=[pltpu.HBM((s, 2 * d_model), F32),
                   pltpu.HBM((s, d_model), BF16), pltpu.HBM((s, LANES), F32),
                   pltpu.HBM((nc, 8, l), F32)],
        scratch_shapes=[pltpu.VMEM((hh, dh, dh), F32), pltpu.VMEM((hh, 1, dh), F32)],
        compiler_params=_params(10 * hh * dh * dh * 4 + (16 << 20)),
    )(*[_hbm(a) for a in (mqk, mqk, zm, gcol, grow, cs, ns, ms, dhm)])


def _xa_fwd(zm, qcol, kv, gq, gk, d_model, *, name, tq=256):
    s = zm.shape[0]
    nm = kv.shape[0]
    dh = d_model // X_HEADS
    tq = _pick(s, (tq, 128, 64))
    scale = dh ** -0.5

    def body(q_ref, k_ref, v_ref, gq_ref, gk_ref, o_ref):
        qn = _rms_fwd(q_ref[...], gq_ref[...])
        kn = _rms_fwd(k_ref[...], gk_ref[...])
        lg = _dot(qn, kn, NT) * scale
        lg = lg - jnp.max(lg, axis=1, keepdims=True)
        p = jnp.exp(lg)
        p = p / jnp.sum(p, axis=1, keepdims=True)
        o_ref[...] = _dot(p, v_ref[...], NN).astype(o_ref.dtype)

    return pl.pallas_call(
        body, name=name, grid=(X_HEADS, s // tq),
        in_specs=[pl.BlockSpec((tq, dh), lambda h, i: (i, qcol // dh + h)), pl.BlockSpec((nm, dh), lambda h, i: (0, h)),
                  pl.BlockSpec((nm, dh), lambda h, i: (0, X_HEADS + h)),
                  pl.BlockSpec((1, dh), lambda h, i: (0, 0)), pl.BlockSpec((1, dh), lambda h, i: (0, 0))],
        out_specs=pl.BlockSpec((tq, dh), lambda h, i: (i, h)),
        out_shape=pltpu.HBM((s, d_model), BF16),
        compiler_params=_params(32 << 20),
    )(_hbm(zm), _hbm(kv), _hbm(kv), gq, gk)


def _xa_bwd(zm, qcol, kv, gq, gk, dy, d_model, *, name, tq=256):
    s = zm.shape[0]
    nm = kv.shape[0]
    dh = d_model // X_HEADS
    tq = _pick(s, (tq, 128, 64))
    nq = s // tq
    scale = dh ** -0.5

    def body(q_ref, k_ref, v_ref, gq_ref, gk_ref, do_ref, dq_ref, dkn_ref, dv_ref, dgq_ref):
        h, i = pl.program_id(0), pl.program_id(1)

        @pl.when(i == 0)
        def _():
            dkn_ref[...] = jnp.zeros_like(dkn_ref)
            dv_ref[...] = jnp.zeros_like(dv_ref)

        @pl.when((i == 0) & (h == 0))
        def _():
            dgq_ref[...] = jnp.zeros_like(dgq_ref)

        q = q_ref[...]
        qn = _rms_fwd(q, gq_ref[...])
        kn = _rms_fwd(k_ref[...], gk_ref[...])
        lg = _dot(qn, kn, NT) * scale
        lg = lg - jnp.max(lg, axis=1, keepdims=True)
        p = jnp.exp(lg)
        p = p / jnp.sum(p, axis=1, keepdims=True)
        do = do_ref[...]
        dv_ref[...] += _dot(p, do, TN)
        dp = _dot(do, v_ref[...], NT)
        dlg = p * (dp - jnp.sum(dp * p, axis=1, keepdims=True)) * scale
        dqn = _dot(dlg, kn, NN)
        dkn_ref[...] += _dot(dlg, qn, TN)
        dq, dgq = _rms_bwd(q, gq_ref[...], dqn)
        dq_ref[...] = dq.astype(dq_ref.dtype)
        dgq_ref[...] += jnp.sum(dgq, axis=0, keepdims=True)

    return pl.pallas_call(
        body, name=name, grid=(X_HEADS, nq),
        in_specs=[pl.BlockSpec((tq, dh), lambda h, i: (i, qcol // dh + h)), pl.BlockSpec((nm, dh), lambda h, i: (0, h)),
                  pl.BlockSpec((nm, dh), lambda h, i: (0, X_HEADS + h)),
                  pl.BlockSpec((1, dh), lambda h, i: (0, 0)), pl.BlockSpec((1, dh), lambda h, i: (0, 0)),
                  pl.BlockSpec((tq, dh), lambda h, i: (i, h))],
        out_specs=[pl.BlockSpec((tq, dh), lambda h, i: (i, h)), pl.BlockSpec((nm, dh), lambda h, i: (0, h)),
                   pl.BlockSpec((nm, dh), lambda h, i: (0, h)), pl.BlockSpec((1, dh), lambda h, i: (0, 0))],
        out_shape=[pltpu.HBM((s, d_model), BF16), pltpu.HBM((nm, d_model), F32),
                   pltpu.HBM((nm, d_model), F32), pltpu.HBM((1, dh), F32)],
        compiler_params=_params(32 << 20),
    )(_hbm(zm), _hbm(kv), _hbm(kv), gq, gk, _hbm(dy))


def _place():
    return lax.axis_index("x"), lax.axis_index("y"), lax.axis_index("c")


ANY = pl.BlockSpec(memory_space=pl.ANY)


def _allgather_two_level(big, small, *, name, chunk_rows=64):
    r = big.shape[0]
    half = r // 2
    nr = _pick(half, (chunk_rows, 32, 16))
    nq = half // nr

    def body(big_ref, small_ref, obig, osmall, send, recv, fsend, frecv, ssend, srecv, loc):
        x, y, c = _place()
        k = 2 * x + y
        chips = [(1 - x, y), (x, 1 - y), (1 - x, 1 - y)]
        own = [pltpu.make_async_copy(big_ref, obig.at[k], loc.at[0]),
               pltpu.make_async_copy(small_ref, osmall.at[k], loc.at[1])]
        for cp in own:
            cp.start()

        def rows(h, q):
            return pl.ds(pl.multiple_of(h * half + q * nr, nr), nr)

        def over_ici(j, q, slot, h):
            return pltpu.make_async_remote_copy(
                src_ref=big_ref.at[rows(h, q)], dst_ref=obig.at[slot, rows(h, q)], send_sem=send.at[nq * j + q],
                recv_sem=recv.at[nq * j + q], device_id=(chips[j][0], chips[j][1], c), device_id_type=MESH)

        def to_sibling(j, q, h):
            slot = 2 * chips[j][0] + chips[j][1]
            return pltpu.make_async_remote_copy(
                src_ref=obig.at[slot, rows(h, q)], dst_ref=obig.at[slot, rows(h, q)], send_sem=fsend.at[nq * j + q],
                recv_sem=frecv.at[nq * j + q], device_id=(x, y, 1 - c), device_id_type=MESH)

        def small_copy(j, slot):
            return pltpu.make_async_remote_copy(
                src_ref=small_ref, dst_ref=osmall.at[slot], send_sem=ssend.at[j], recv_sem=srecv.at[j],
                device_id=(chips[j][0], chips[j][1], c), device_id_type=MESH)

        for q in range(nq):
            for j in range(3):
                over_ici(j, q, k, c).start()
        for j in range(3):
            small_copy(j, k).start()
        for q in range(nq):
            for j in range(3):
                over_ici(j, q, 2 * chips[j][0] + chips[j][1], c).wait_recv()
                to_sibling(j, q, c).start()
        for q in range(nq):
            for j in range(3):
                to_sibling(j, q, 1 - c).wait_recv()
        for j in range(3):
            small_copy(j, 2 * chips[j][0] + chips[j][1]).wait_recv()
            small_copy(j, k).wait_send()
        for q in range(nq):
            for j in range(3):
                over_ici(j, q, k, c).wait_send()
                to_sibling(j, q, c).wait_send()
        for cp in own:
            cp.wait()

    return pl.pallas_call(
        body, name=name, in_specs=[ANY] * 2, out_specs=[ANY] * 2,
        out_shape=[pltpu.HBM((4,) + big.shape, big.dtype), pltpu.HBM((4,) + small.shape, small.dtype)],
        scratch_shapes=[pltpu.SemaphoreType.DMA((3 * nq,))] * 4
        + [pltpu.SemaphoreType.DMA((3,)), pltpu.SemaphoreType.DMA((3,)), pltpu.SemaphoreType.DMA((2,))],
    )(big, small)


HBM_SPEC = pl.BlockSpec(memory_space=pltpu.HBM)
SEM_SPEC = pl.BlockSpec(memory_space=pltpu.SEMAPHORE)
EFFECT = pltpu.SideEffectType.DATAFLOW_SIDE_EFFECTING


def _split_copies(kind, srcs, lands, send, recv):
    x, y, c = _place()
    if kind == "quarters":
        peers = [(1 - x, y, c), (x, 1 - y, c), (1 - x, 1 - y, c)]
    else:
        peers = [(x ^ ((j >> 2) & 1), y ^ ((j >> 1) & 1), c ^ (j & 1)) for j in range(1, 8)]
    npeer = len(peers)
    out = []
    for t in range(len(srcs)):
        for j, (px, py, pc) in enumerate(peers):
            if kind == "quarters":
                src, mine, theirs = srcs[t], 2 * x + y, 2 * px + py
            else:
                src, mine, theirs = srcs[t].at[2 * px + py, pc], 4 * x + 2 * y + c, 4 * px + 2 * py + pc
            mk = functools.partial(
                pltpu.make_async_remote_copy, src_ref=src, send_sem=send.at[npeer * t + j],
                recv_sem=recv.at[npeer * t + j], device_id=(px, py, pc), device_id_type=MESH)
            out.append((functools.partial(mk, dst_ref=lands[t].at[mine]),
                        functools.partial(mk, dst_ref=lands[t].at[theirs])))
    return out


def _split_start(kind, srcs, land_shapes, after, *, name):
    n = len(srcs)
    ncopies = n * (3 if kind == "quarters" else 7)

    def body(*refs):
        ins, lands = refs[:n], refs[n:2 * n]
        send, recv = refs[2 * n + 1], refs[2 * n + 2]
        token = refs[-1]
        for start, _ in _split_copies(kind, ins, lands, send, recv):
            start().start()
        token[...] = jnp.zeros_like(token)

    lands = [_hbm(lax.empty(shp, a.dtype)) for shp, a in zip(land_shapes, srcs)]
    res = pl.pallas_call(
        body, name=name, in_specs=[HBM_SPEC] * (2 * n) + [ANY],
        out_specs=[SEM_SPEC, SEM_SPEC] + [HBM_SPEC] * (2 * n) + [pl.BlockSpec(memory_space=pltpu.VMEM)],
        out_shape=[pltpu.SemaphoreType.DMA((ncopies,)), pltpu.SemaphoreType.DMA((ncopies,))]
        + [pltpu.HBM(a.shape, a.dtype) for a in srcs] + [pltpu.HBM(shp, a.dtype) for shp, a in zip(land_shapes, srcs)]
        + [jax.ShapeDtypeStruct((8, LANES), F32)],
        input_output_aliases={i: 2 + i for i in range(2 * n)},
        compiler_params=pltpu.CompilerParams(has_side_effects=EFFECT),
    )(*[_hbm(a) for a in srcs], *lands, after)
    return res[0], res[1], list(res[2:2 + n]), list(res[2 + n:2 + 2 * n]), res[-1]


def _split_wait(kind, send, recv, srcs, lands, after, *, name):
    n = len(srcs)

    def body(*refs):
        ins, lnd = refs[:n], refs[n:2 * n]
        snd, rcv = refs[2 * n], refs[2 * n + 1]
        for start, arrive in _split_copies(kind, ins, lnd, snd, rcv):
            start().wait_send()
            arrive().wait_recv()

    res = pl.pallas_call(
        body, name=name, in_specs=[HBM_SPEC] * (2 * n) + [SEM_SPEC, SEM_SPEC] + [ANY] * len(after),
        out_specs=[HBM_SPEC] * (2 * n),
        out_shape=[pltpu.HBM(a.shape, a.dtype) for a in srcs] + [pltpu.HBM(a.shape, a.dtype) for a in lands],
        input_output_aliases={i: i for i in range(2 * n)},
        compiler_params=pltpu.CompilerParams(has_side_effects=EFFECT),
    )(*srcs, *lands, send, recv, *after)
    return list(res[n:])


def _sum8(parts, *, name):
    _, r, c = parts.shape
    t = _pick(r, (128, 64, 32, 16, 8))

    def body(p_ref, o_ref):
        acc = p_ref[0].astype(F32)
        for k in range(1, 8):
            acc = acc + p_ref[k].astype(F32)
        o_ref[...] = acc

    return pl.pallas_call(
        body, name=name, grid=(r // t,), in_specs=[pl.BlockSpec((8, t, c), lambda i: (0, i, 0))],
        out_specs=pl.BlockSpec((t, c), lambda i: (i, 0)), out_shape=pltpu.HBM((r, c), F32),
        compiler_params=_params(2 * 8 * t * c * 2 + 6 * t * c * 4 + (4 << 20)),
    )(_hbm(parts))


def _swap_halves(halves, *, name, chunk_bytes=512 * 1024):
    n = len(halves)
    items = []
    for t, a in enumerate(halves):
        r = a.shape[0]
        k = 1
        while _nbytes(a.shape, a.dtype) // k > chunk_bytes and r % (2 * k) == 0 and (r // (2 * k)) % 8 == 0:
            k *= 2
        items += [(t, q * (r // k), r // k) for q in range(k)]
    m = len(items)

    def body(*refs):
        ins, outs = refs[:n], refs[n:2 * n]
        sbuf, rbuf = refs[2 * n:3 * n], refs[3 * n:4 * n]
        send, recv, loc_own, loc_in, loc_out = refs[4 * n:]
        x, y, c = _place()
        local, stage = [], []
        for t in range(n):
            cp = pltpu.make_async_copy(ins[t], outs[t].at[c], loc_own.at[t])
            cp.start()
            local.append(cp)
        for q, (t, r0, nr) in enumerate(items):
            cp = pltpu.make_async_copy(ins[t].at[pl.ds(r0, nr)], sbuf[t].at[pl.ds(r0, nr)], loc_in.at[q])
            cp.start()
            stage.append(cp)

        def copy(q):
            t, r0, nr = items[q]
            return pltpu.make_async_remote_copy(
                src_ref=sbuf[t].at[pl.ds(r0, nr)], dst_ref=rbuf[t].at[pl.ds(r0, nr)], send_sem=send.at[q],
                recv_sem=recv.at[q], device_id=(x, y, 1 - c), device_id_type=MESH)

        for q in range(m):
            stage[q].wait()
            copy(q).start()
        for q, (t, r0, nr) in enumerate(items):
            copy(q).wait_recv()
            cp = pltpu.make_async_copy(rbuf[t].at[pl.ds(r0, nr)], outs[t].at[1 - c, pl.ds(r0, nr)], loc_out.at[q])
            cp.start()
            local.append(cp)
        for q in range(m):
            copy(q).wait_send()
        for cp in local:
            cp.wait()

    stage_bytes = 2 * sum(_nbytes(a.shape, a.dtype) for a in halves)
    return pl.pallas_call(
        body, name=name, in_specs=[ANY] * n, out_specs=[ANY] * n,
        out_shape=[pltpu.HBM((2,) + a.shape, a.dtype) for a in halves],
        scratch_shapes=[pltpu.VMEM(a.shape, a.dtype) for a in halves] * 2
        + [pltpu.SemaphoreType.DMA((m,)), pltpu.SemaphoreType.DMA((m,)), pltpu.SemaphoreType.DMA((n,)),
           pltpu.SemaphoreType.DMA((m,)), pltpu.SemaphoreType.DMA((m,))],
        compiler_params=_params(stage_bytes + (4 << 20)),
    )(*halves)


def _allreduce_small(p, after, *, name):
    r = p.shape[0]

    def body(p_ref, after_ref, o_ref, buf, send, recv):
        x, y, c = _place()
        me = 4 * x + 2 * y + c
        peers = [(x ^ ((j >> 2) & 1), y ^ ((j >> 1) & 1), c ^ (j & 1)) for j in range(1, 8)]

        def copy(j, slot):
            return pltpu.make_async_remote_copy(
                src_ref=p_ref, dst_ref=buf.at[slot], send_sem=send.at[j], recv_sem=recv.at[j],
                device_id=peers[j], device_id_type=MESH)

        for j in range(7):
            copy(j, me).start()
        buf[me] = p_ref[...]
        for j in range(7):
            px, py, pc = peers[j]
            copy(j, 4 * px + 2 * py + pc).wait_recv()
        for j in range(7):
            copy(j, me).wait_send()
        acc = buf[0]
        for k in range(1, 8):
            acc = acc + buf[k]
        o_ref[...] = acc

    vspec = pl.BlockSpec(memory_space=pltpu.VMEM)
    return pl.pallas_call(
        body, name=name, in_specs=[vspec, ANY], out_specs=vspec, out_shape=jax.ShapeDtypeStruct((r, LANES), F32),
        scratch_shapes=[pltpu.VMEM((8, r, LANES), F32), pltpu.SemaphoreType.DMA((7,)), pltpu.SemaphoreType.DMA((7,))],
    )(p, after)


def _adamw_fn(w, g, m, v):
    m = ADAM_B1 * m + (1.0 - ADAM_B1) * g
    v = ADAM_B2 * v + (1.0 - ADAM_B2) * (g * g)
    m_hat = m / (1.0 - ADAM_B1 ** ADAM_STEP)
    v_hat = v / (1.0 - ADAM_B2 ** ADAM_STEP)
    delta = -ADAM_LR * (m_hat / (jnp.sqrt(v_hat) + ADAM_EPS) + ADAM_WD * w)
    return delta, m, v


def _adamw(w, g, m, v, *, name):
    c = w.shape[1]
    return _rowwise(_adamw_fn, [w, g, m, v], [], [(c, F32)] * 3, name=name, tr=128)


def _pack(vecs, rows):
    flat = jnp.concatenate([a.reshape(-1).astype(F32) for a in vecs])
    return jnp.pad(flat, (0, rows * LANES - flat.shape[0])).reshape(rows, LANES)


def _unpack(p, like):
    flat, out, o = p.reshape(-1), [], 0
    for a in like:
        out.append(flat[o:o + a.size].reshape(a.shape))
        o += a.size
    return out


def kernel(x, mem, g_mix, w_in, b_if, b_gate, conv_w, conv_b, ml_norm_g, g_mem, w_mem_kv, q_norm_g, k_norm_g, w_sb_proj, w_ml_proj, w_x_proj, w_out, g_mlp, w_ff1, w_ff2, loss_target, m_g_mix, m_w_in, m_b_if, m_b_gate, m_conv_w, m_conv_b, m_ml_norm_g, m_g_mem, m_w_mem_kv, m_q_norm_g, m_k_norm_g, m_w_sb_proj, m_w_ml_proj, m_w_x_proj, m_w_out, m_g_mlp, m_w_ff1, m_w_ff2, v_g_mix, v_w_in, v_b_if, v_b_gate, v_conv_w, v_conv_b, v_ml_norm_g, v_g_mem, v_w_mem_kv, v_q_norm_g, v_k_norm_g, v_w_sb_proj, v_w_ml_proj, v_w_x_proj, v_w_out, v_g_mlp, v_w_ff1, v_w_ff2):
    _, s, d = x.shape
    nm = mem.shape[1]
    n_in = 4 * w_in.shape[2]
    dff = 4 * w_ff1.shape[2]
    sbh = d // SB_HD
    hh = ML_HEADS
    dh = d // hh
    nc = s // CHUNK
    assert n_in == 11 * d + 2 * hh and d % (2 * LANES) == 0 and s % LANES == 0
    x2, mem2, tgt = x[0], mem[0], loss_target[0]

    k4 = 2 * lax.axis_index("x") + lax.axis_index("y")
    me = 2 * k4 + lax.axis_index("c")
    g_first = _allgather_two_level(w_in[0].astype(BF16), conv_w[0], name="gather_w_in")
    later = [a[0].astype(BF16) for a in (w_mem_kv, w_sb_proj, w_ml_proj, w_x_proj, w_out, w_ff1, w_ff2)]
    gw_send, gw_recv, gw_src, gw_land, gw_token = _split_start(
        "quarters", later, [(4,) + a.shape for a in later], g_first[0], name="gather_rest_start")
    cols = lambda a: a.transpose(1, 0, 2).reshape(a.shape[1], 4 * a.shape[2])
    rws = lambda a: a.reshape(4 * a.shape[1], a.shape[2])
    w_in_f = cols(g_first[0])
    w_main = jnp.concatenate([w_in_f[:, :7 * d], w_in_f[:, 7 * d + 2 * hh:]], axis=1)
    w_if = jnp.pad(w_in_f[:, 7 * d:7 * d + 2 * hh], ((0, 0), (0, LANES - 2 * hh)))
    conv_wf = cols(g_first[1])
    b_if_p = jnp.pad(b_if, ((0, 0), (0, LANES - 2 * hh)))

    (hn,) = _rowwise(_rms_fwd, [x2], [g_mix], [(d, BF16)], name="norm_in")
    zm = _mm(hn, w_main, after=gw_token, name="proj_in")
    zif = _mm(hn, w_if, name="proj_if")
    y_sb, ltot = _sb_fwd(zm, sbh, name="sb_fwd")

    def gate_fn(z, b):
        pre = z + b
        lane = lax.broadcasted_iota(jnp.int32, pre.shape, 1)
        return jnp.where(lane < hh, pre, -_softplus(-pre))

    (gcol,) = _rowwise(gate_fn, [zif], [b_if_p], [(LANES, F32)], name="ml_gates")
    grow = gcol[:, :8].T.reshape(8, nc, CHUNK).transpose(1, 0, 2)
    mqk = _conv_fwd(zm, 3 * d, 2 * d, conv_wf, conv_b, name="conv_fwd")
    hm, cst, nst, mst = _ml_fwd(mqk, zm, 5 * d, gcol, grow, d, name="ml_fwd")

    def mlout_fn(hv, o, g):
        ys = [_rms_fwd(hv[:, k * dh:(k + 1) * dh], g[:, k * dh:(k + 1) * dh]) for k in range(hh)]
        return jnp.concatenate(ys, axis=1) * _sigmoid(o)

    (y_ml,) = _rowwise(mlout_fn, [hm, (zm, d, 6)], [ml_norm_g], [(d, BF16)], name="ml_out")
    gw_land = _split_wait("quarters", gw_send, gw_recv, gw_src, gw_land, [y_ml, y_sb], name="gather_rest_wait")
    gw = [lax.dynamic_update_index_in_dim(ld, a, k4, 0) for ld, a in zip(gw_land, later)]
    w_kv, w_sbp, w_mlp, w_xp, w_o, w_f1, w_f2 = (cols(gw[0]), rws(gw[1]), rws(gw[2]), rws(gw[3]), rws(gw[4]),
                                                 cols(gw[5]), rws(gw[6]))
    (memn,) = _rowwise(_rms_fwd, [mem2], [g_mem], [(d, BF16)], name="norm_mem")
    kv = _mm(memn, w_kv, name="proj_kv")
    y_x = _xa_fwd(zm, 7 * d, kv, q_norm_g, k_norm_g, d, name="xa_fwd")
    p_sb = _mm(y_sb, w_sbp, name="proj_sb")
    p_ml = _mm(y_ml, w_mlp, name="proj_ml")
    p_x = _mm(y_x, w_xp, name="proj_x")

    def merge_fn(a, b, c, g0, g1, g2, bg):
        return (_sigmoid(g0 + bg[:, :d]) * a + _sigmoid(g1 + bg[:, d:2 * d]) * b + _sigmoid(g2 + bg[:, 2 * d:]) * c)

    gate_cols = [(zm, d, 8), (zm, d, 9), (zm, d, 10)]
    (mixed,) = _rowwise(merge_fn, [p_sb, p_ml, p_x] + gate_cols, [b_gate], [(d, BF16)], name="merge")
    x1 = _mm(mixed, w_o, tiles=[x2], name="proj_out")
    (h2,) = _rowwise(_rms_fwd, [x1], [g_mlp], [(d, BF16)], name="norm_mlp")
    u, act = _mm(h2, w_f1, post=lambda r: (r, jnp.square(jnp.maximum(r, 0.0))), out_dtype=(F32, BF16), name="ff1")
    dy = _mm(act, w_f2, tiles=[x1, tgt], post=lambda r, xv, tv: (r + xv - tv) * (1.0 / d), name="ff2")
    (loss_cols,) = _rowwise(lambda g: (jnp.sum(g * g, axis=0, keepdims=True) * (0.5 * d),), [dy], [], [], [d],
                            name="loss")

    du = _mm(dy, w_f2, tb=True, tiles=[u], post=lambda r, uv: r * 2.0 * jnp.maximum(uv, 0.0), out_dtype=BF16,
             name="ff2_dx")
    dw_f2 = _mm(act, dy, ta=True, name="ff2_dw")
    dw_f1 = _mm(h2, du, ta=True, name="ff1_dw")
    dh2 = _mm(du, w_f1, tb=True, name="ff1_dx")

    def norm_bwd_fn(xv, dyv, res, g):
        dx, dg = _rms_bwd(xv, g, dyv)
        return dx + res, jnp.sum(dg, axis=0, keepdims=True)

    dx1, dg_mlp = _rowwise(norm_bwd_fn, [x1, dh2, dy], [g_mlp], [(d, F32)], [d], name="norm_mlp_bwd")
    dmixed = _mm(dx1, w_o, tb=True, name="proj_out_dx")
    dw_o = _mm(mixed, dx1, ta=True, name="proj_out_dw")

    def merge_bwd_fn(dm, a, b, c, g0, g1, g2, bg):
        outs, dgs = [], []
        for p, g, k in ((a, g0, 0), (b, g1, 1), (c, g2, 2)):
            sg = _sigmoid(g + bg[:, k * d:(k + 1) * d])
            outs.append(dm * sg)
            dgs.append(dm * p * sg * (1.0 - sg))
        dgate = jnp.concatenate(dgs, axis=1)
        return (*outs, dgate, jnp.sum(dgate, axis=0, keepdims=True))

    dp_sb, dp_ml, dp_x, dgate, db_gate = _rowwise(
        merge_bwd_fn, [dmixed, p_sb, p_ml, p_x] + gate_cols, [b_gate], [(d, BF16)] * 3 + [(3 * d, BF16)], [3 * d],
        name="merge_bwd", tr=128)
    dw_sbp = _mm(y_sb, dp_sb, ta=True, name="proj_sb_dw")
    dw_mlp = _mm(y_ml, dp_ml, ta=True, name="proj_ml_dw")
    dw_xp = _mm(y_x, dp_x, ta=True, name="proj_x_dw")
    dy_sb = _mm(dp_sb, w_sbp, tb=True, out_dtype=BF16, name="proj_sb_dx")
    dy_ml = _mm(dp_ml, w_mlp, tb=True, name="proj_ml_dx")
    dy_x = _mm(dp_x, w_xp, tb=True, out_dtype=BF16, name="proj_x_dx")

    dxq, dkn, dxv, dg_qn = _xa_bwd(zm, 7 * d, kv, q_norm_g, k_norm_g, dy_x, d, name="xa_bwd")

    def knorm_bwd_fn(kvv, dknv, dvv, g):
        dks, dgs = [], []
        for k in range(X_HEADS):
            sl = slice(k * dh, (k + 1) * dh)
            dk, dg = _rms_bwd(kvv[:, sl], g, dknv[:, sl])
            dks.append(dk)
            dgs.append(jnp.sum(dg, axis=0, keepdims=True))
        return jnp.concatenate(dks + [dvv], axis=1), dgs[0] + dgs[1] + dgs[2] + dgs[3]

    dkv, dg_kn = _rowwise(knorm_bwd_fn, [(kv, d, 0), dkn, dxv], [k_norm_g], [(2 * d, BF16)], [dh], name="xa_knorm_bwd")
    dw_kv = _mm(memn, dkv, ta=True, name="proj_kv_dw")
    dmemn = _mm(dkv, w_kv, tb=True, name="proj_kv_dx")

    def gmem_fn(mv, dv_, g):
        _, dg = _rms_bwd(mv, g, dv_)
        return (jnp.sum(dg, axis=0, keepdims=True),)

    (dg_mem,) = _rowwise(gmem_fn, [mem2, dmemn], [g_mem], [], [d], name="norm_mem_bwd")

    uncols = lambda a: a.reshape(a.shape[0], 4, a.shape[1] // 4).transpose(1, 0, 2)
    unrws = lambda a: a.reshape(4, a.shape[0] // 4, a.shape[1])
    to_parts = lambda q: q.astype(BF16).reshape(4, 2, q.shape[1] // 2, q.shape[2])
    early = [to_parts(q) for q in (uncols(dw_kv), unrws(dw_sbp), unrws(dw_mlp), unrws(dw_xp), unrws(dw_o),
                                   uncols(dw_f1), unrws(dw_f2))]
    ge_send, ge_recv, ge_src, ge_land, ge_token = _split_start(
        "grads", early, [(8,) + a.shape[2:] for a in early], dg_mem, name="exchange_early_start")

    dsq, dsk, dsv = _sb_bwd(zm, dy_sb, ltot, ge_token, sbh, name="sb_bwd")

    def mlout_bwd_fn(dyv, hv, o, g):
        sg = _sigmoid(o)
        dn = dyv * sg
        dxs, dgs, ys = [], [], []
        for k in range(hh):
            sl = slice(k * dh, (k + 1) * dh)
            ys.append(_rms_fwd(hv[:, sl], g[:, sl]))
            dxk, dgk = _rms_bwd(hv[:, sl], g[:, sl], dn[:, sl])
            dxs.append(dxk)
            dgs.append(dgk)
        do = dyv * jnp.concatenate(ys, axis=1) * sg * (1.0 - sg)
        return jnp.concatenate(dxs, axis=1), do, jnp.sum(jnp.concatenate(dgs, axis=1), axis=0, keepdims=True)

    dhm, dmlo, dg_mln = _rowwise(mlout_bwd_fn, [dy_ml, hm, (zm, d, 6)], [ml_norm_g], [(d, F32), (d, BF16)], [d],
                                 name="ml_out_bwd")
    dmqk, dmlv, dgc, dgr = _ml_bwd(mqk, zm, 5 * d, gcol, grow, cst, nst, mst, dhm, d, name="ml_bwd")
    dmlqk, dconv_w, dconv_b = _conv_bwd(zm, 3 * d, 2 * d, conv_wf, conv_b, dmqk, name="conv_bwd")
    dgr_t = jnp.pad(dgr.transpose(1, 0, 2).reshape(8, s).T, ((0, 0), (0, LANES - 8)))

    def gate_bwd_fn(a, b, z, bias):
        tot = a + b
        rows_t = tot.shape[0]
        r = lax.broadcasted_iota(jnp.int32, (rows_t, rows_t), 0)
        c = lax.broadcasted_iota(jnp.int32, (rows_t, rows_t), 1)
        sh = CHUNK.bit_length() - 1
        same_chunk = jnp.right_shift(r, sh) == jnp.right_shift(c, sh)
        dlf = _u01dot(((c >= r) & same_chunk).astype(BF16), tot)
        lane = lax.broadcasted_iota(jnp.int32, tot.shape, 1)
        dz = jnp.where(lane < hh, tot, jnp.where(lane < 2 * hh, dlf * _sigmoid(-(z + bias)), 0.0))
        return dz, jnp.sum(dz, axis=0, keepdims=True)

    dzif, db_if_p = _rowwise(gate_bwd_fn, [dgc, dgr_t, zif], [b_if_p], [(LANES, BF16)], [LANES], name="ml_gates_bwd",
                             tr=8 * CHUNK)
    dz_pieces = [dsq, dsk, dsv, dmlqk, dmlv, dmlo, dxq, dgate]
    dw_main = _mm_pieces(dz_pieces, hn, side="n", out_dtype=BF16, name="proj_in_dw")
    dw_if = _mm(hn, dzif, ta=True, out_dtype=BF16, name="proj_if_dw")
    dw_in = jnp.concatenate([dw_main[:, :7 * d], dw_if[:, :2 * hh], dw_main[:, 7 * d:]], axis=1)
    late = [to_parts(uncols(dw_in))]
    gl_send, gl_recv, gl_src, gl_land, gl_token = _split_start(
        "grads", late, [(8,) + a.shape[2:] for a in late], dw_if, name="exchange_late_start")
    dhn = _mm_pieces(dz_pieces, w_main, side="k", after=gl_token, name="proj_in_dx")
    dhn = _mm(dzif, w_if, tb=True, tiles=[dhn], name="proj_if_dx")
    dx, dg_mix = _rowwise(norm_bwd_fn, [x2, dhn, dx1], [g_mix], [(d, F32)], [d], name="norm_in_bwd")

    own = lambda p: lax.dynamic_index_in_dim(lax.dynamic_index_in_dim(p, k4, 0, keepdims=False),
                                             lax.axis_index("c"), 0, keepdims=False)

    def finish(tag, send, recv, src, land, parts, after, ws, ms, vs):
        land = _split_wait("grads", send, recv, src, land, after, name=f"exchange_{tag}_wait")
        got = [lax.dynamic_update_index_in_dim(ld, own(p), me, 0) for ld, p in zip(land, parts)]
        halves = [_sum8(r, name=f"sum_grads_{tag}{i}") for i, r in enumerate(got)]
        both = _swap_halves(halves, name=f"swap_halves_{tag}")
        gs = [b.reshape(2 * b.shape[1], b.shape[2]) for b in both]
        return gs, [_adamw(w, g, m, v, name=f"adamw_{tag}{i}") for i, (w, g, m, v) in enumerate(zip(ws, gs, ms, vs))]

    first = lambda arrs: [a[0] for a in arrs]
    g_early, out_early = finish(
        "early", ge_send, ge_recv, ge_src, ge_land, early, [dx],
        first([w_mem_kv, w_sb_proj, w_ml_proj, w_x_proj, w_out, w_ff1, w_ff2]),
        first([m_w_mem_kv, m_w_sb_proj, m_w_ml_proj, m_w_x_proj, m_w_out, m_w_ff1, m_w_ff2]),
        first([v_w_mem_kv, v_w_sb_proj, v_w_ml_proj, v_w_x_proj, v_w_out, v_w_ff1, v_w_ff2]))
    g_late, out_late = finish(
        "late", gl_send, gl_recv, gl_src, gl_land, late, [o[0] for o in out_early],
        first([w_in]), first([m_w_in]), first([v_w_in]))
    g_big = [g[None] for g in g_late + g_early]
    big_out = [[o[None] for o in outs] for outs in out_late + out_early]

    small_g = [dg_mix, db_if_p[:, :2 * hh], db_gate, dconv_w, dconv_b, dg_mln, dg_mem, dg_qn, dg_kn, dg_mlp,
               jnp.sum(loss_cols).reshape(1, 1)]
    n_small = sum(a.size for a in small_g)
    rows = -(-n_small // (8 * LANES)) * 8
    g_small = _unpack(_allreduce_small(_pack(small_g, rows), out_late[0][0], name="allreduce_small"), small_g)
    loss = g_small[-1].reshape(())
    qw = conv_w.shape[2]
    g_conv_w = lax.dynamic_slice_in_dim(g_small[3], k4 * qw, qw, axis=1)
    g_small_w = [g_small[0], g_small[1], g_small[2], g_conv_w] + g_small[4:10]
    sm_w = [g_mix, b_if, b_gate, conv_w[0], conv_b, ml_norm_g, g_mem, q_norm_g, k_norm_g, g_mlp]
    sm_m = [m_g_mix, m_b_if, m_b_gate, m_conv_w[0], m_conv_b, m_ml_norm_g, m_g_mem, m_q_norm_g, m_k_norm_g, m_g_mlp]
    sm_v = [v_g_mix, v_b_if, v_b_gate, v_conv_w[0], v_conv_b, v_ml_norm_g, v_g_mem, v_q_norm_g, v_k_norm_g, v_g_mlp]
    n_sw = sum(a.size for a in sm_w)
    rows_w = -(-n_sw // (8 * LANES)) * 8
    sm_out = _adamw(_pack(sm_w, rows_w), _pack(g_small_w, rows_w), _pack(sm_m, rows_w), _pack(sm_v, rows_w),
                    name="adamw_small")
    sm_delta, sm_newm, sm_newv = [_unpack(p, sm_w) for p in sm_out]

    order = ["g_mix", "w_in", "b_if", "b_gate", "conv_w", "conv_b", "ml_norm_g", "g_mem", "w_mem_kv", "q_norm_g",
             "k_norm_g", "w_sb_proj", "w_ml_proj", "w_x_proj", "w_out", "g_mlp", "w_ff1", "w_ff2"]
    small_names = ["g_mix", "b_if", "b_gate", "conv_w", "conv_b", "ml_norm_g", "g_mem", "q_norm_g", "k_norm_g", "g_mlp"]
    big_names = ["w_in", "w_mem_kv", "w_sb_proj", "w_ml_proj", "w_x_proj", "w_out", "w_ff1", "w_ff2"]
    grads, deltas, new_m, new_v = {}, {}, {}, {}
    for i, nme in enumerate(small_names):
        shp = sm_w[i].shape if nme != "conv_w" else conv_w.shape
        grads[nme] = g_small_w[i].reshape(shp)
        deltas[nme], new_m[nme], new_v[nme] = (sm_delta[i].reshape(shp), sm_newm[i].reshape(shp),
                                               sm_newv[i].reshape(shp))
    for i, nme in enumerate(big_names):
        grads[nme] = g_big[i]
        deltas[nme], new_m[nme], new_v[nme] = big_out[i]
    return (loss, dx[None], *[grads[k] for k in order], *[deltas[k] for k in order], *[new_m[k] for k in order],
            *[new_v[k] for k in order])
```

```python
import functools

import jax
import jax.numpy as jnp
from jax import lax
from jax.experimental import pallas as pl
from jax.experimental.pallas import tpu as pltpu

F32 = jnp.float32
BF16 = jnp.bfloat16
MESH = pl.DeviceIdType.MESH

EPS = 1e-6
SB_HD = 128
ML_HEADS = 4
X_HEADS = 4
CHUNK = 64
CONV_W = 4
LANES = 128
ADAM_LR = 0.001
ADAM_B1 = 0.9
ADAM_B2 = 0.999
ADAM_EPS = 1e-08
ADAM_WD = 0.01
ADAM_STEP = 10
VMEM_CAP = 56 * 1024 * 1024
NEG = -1e30

NT = (((1,), (1,)), ((), ()))
NN = (((1,), (0,)), ((), ()))
TN = (((0,), (0,)), ((), ()))


def _dot(a, b, dn=NN):
    return lax.dot_general(a.astype(BF16), b.astype(BF16), dn, preferred_element_type=F32)


def _dot01(x, u, dn=NN):
    hi = x.astype(BF16)
    lo = (x - hi.astype(F32)).astype(BF16)
    return (lax.dot_general(hi, u, dn, preferred_element_type=F32)
            + lax.dot_general(lo, u, dn, preferred_element_type=F32))


def _u01dot(u, x):
    hi = x.astype(BF16)
    lo = (x - hi.astype(F32)).astype(BF16)
    return (lax.dot_general(u, hi, NN, preferred_element_type=F32)
            + lax.dot_general(u, lo, NN, preferred_element_type=F32))


def _pick(n, cands):
    for c in cands:
        if c <= n and n % c == 0:
            return c
    return n


def _nbytes(shape, dtype):
    n = 1
    for s in shape:
        n *= s
    return n * jnp.dtype(dtype).itemsize


def _params(vmem_bytes):
    return pltpu.CompilerParams(vmem_limit_bytes=int(min(VMEM_CAP, max(vmem_bytes, 16 * 1024 * 1024))))


def _hbm(a):
    return pltpu.with_memory_space_constraint(a, pltpu.HBM)


def _softplus(z):
    return jnp.maximum(z, 0.0) + jnp.log(1.0 + jnp.exp(-jnp.abs(z)))


def _sigmoid(z):
    return 1.0 / (1.0 + jnp.exp(-z))


def _rms_fwd(xv, g):
    r = lax.rsqrt(jnp.mean(xv * xv, axis=-1, keepdims=True) + EPS)
    return xv * r * g


def _rms_bwd(xv, g, dy):
    r = lax.rsqrt(jnp.mean(xv * xv, axis=-1, keepdims=True) + EPS)
    xh = xv * r
    dxh = dy * g
    dx = r * (dxh - xh * jnp.mean(dxh * xh, axis=-1, keepdims=True))
    return dx, dy * xh


def _mm(a, b, *, name, ta=False, tb=False, tiles=(), post=None, out_dtype=F32, bm=1024, bn=1024, bk=1024, after=None):
    m, k = (a.shape[1], a.shape[0]) if ta else a.shape
    n = b.shape[0] if tb else b.shape[1]
    tm = _pick(m, (bm, 512, 256, 128))
    tn = _pick(n, (bn, 512, 256, 128))
    tk = _pick(k, (bk, 512, 256, 128))
    nk = k // tk
    dn = (((0 if ta else 1,), (1 if tb else 0,)), ((), ()))
    dts = out_dtype if isinstance(out_dtype, tuple) else (out_dtype,)
    nt, no = len(tiles), len(dts)
    if post is None:
        post = lambda r, *ts: sum((t.astype(F32) for t in ts), r)

    def body(*refs):
        a_ref, b_ref = refs[:2]
        t_refs = refs[2:2 + nt]
        o_refs = refs[2 + nt + (after is not None):2 + nt + (after is not None) + no]
        part = lax.dot_general(a_ref[...].astype(BF16), b_ref[...].astype(BF16), dn, preferred_element_type=F32)

        def finish(r):
            res = post(r, *[t[...] for t in t_refs])
            res = res if isinstance(res, tuple) else (res,)
            for o, v in zip(o_refs, res):
                o[...] = v.astype(o.dtype)

        if nk == 1:
            finish(part)
        else:
            acc_ref = refs[-1]
            kk = pl.program_id(2)

            @pl.when(kk == 0)
            def _():
                acc_ref[...] = part

            @pl.when(kk > 0)
            def _():
                acc_ref[...] += part

            @pl.when(kk == nk - 1)
            def _():
                finish(acc_ref[...])

    a_spec = pl.BlockSpec((tk, tm), lambda i, j, q: (q, i)) if ta else pl.BlockSpec((tm, tk), lambda i, j, q: (i, q))
    b_spec = pl.BlockSpec((tn, tk), lambda i, j, q: (j, q)) if tb else pl.BlockSpec((tk, tn), lambda i, j, q: (q, j))
    o_spec = pl.BlockSpec((tm, tn), lambda i, j, q: (i, j))
    ins, specs = [_hbm(a), _hbm(b)] + [_hbm(t) for t in tiles], [a_spec, b_spec] + [o_spec] * nt
    vm = 2 * (_nbytes((tm, tk), a.dtype) + _nbytes((tk, tn), b.dtype)) + 3 * _nbytes((tm, tn), F32) \
        + _nbytes((tm, tk), BF16) + _nbytes((tk, tn), BF16) \
        + 2 * sum(_nbytes((tm, tn), t.dtype) for t in tiles) + 2 * sum(_nbytes((tm, tn), dt) for dt in dts)
    if after is not None:
        ins.append(after)
        specs.append(ANY)
    res = pl.pallas_call(
        body, name=name, grid=(m // tm, n // tn, nk), in_specs=specs, out_specs=[o_spec] * no,
        out_shape=[pltpu.HBM((m, n), dt) for dt in dts], scratch_shapes=[pltpu.VMEM((tm, tn), F32)] if nk > 1 else [],
        compiler_params=_params(vm + (4 << 20)),
    )(*ins)
    return res[0] if no == 1 else tuple(res)


def _rowwise(fn, rows, consts, outs, reds=(), *, name, tr=256, temps=6):
    rows = [r if isinstance(r, tuple) else (r, r.shape[1], 0) for r in rows]
    nrows = rows[0][0].shape[0]
    t = _pick(nrows, (tr, 128, 64, 32, 16, 8))
    nr, nc, no = len(rows), len(consts), len(outs)

    def body(*refs):
        rin, cin = refs[:nr], refs[nr:nr + nc]
        oref, rref = refs[nr + nc:nr + nc + no], refs[nr + nc + no:]
        res = fn(*[r[...] for r in rin], *[c[...] for c in cin])
        if not isinstance(res, (tuple, list)):
            res = (res,)
        for o, v in zip(oref, res[:no]):
            o[...] = v.astype(o.dtype)
        if rref:
            @pl.when(pl.program_id(0) == 0)
            def _():
                for r in rref:
                    r[...] = jnp.zeros_like(r)

            for r, v in zip(rref, res[no:]):
                r[...] += v

    in_specs = [pl.BlockSpec((t, w), functools.partial(lambda i, ci: (i, ci), ci=ci)) for (_, w, ci) in rows]
    in_specs += [pl.BlockSpec(c.shape, functools.partial(lambda i, nd: (0,) * nd, nd=c.ndim)) for c in consts]
    out_specs = [pl.BlockSpec((t, w), lambda i: (i, 0)) for (w, _) in outs]
    out_specs += [pl.BlockSpec((1, w), lambda i: (0, 0)) for w in reds]
    out_shape = [pltpu.HBM((nrows, w), dt) for (w, dt) in outs]
    out_shape += [jax.ShapeDtypeStruct((1, w), F32) for w in reds]
    widest = max([w for (_, w, _) in rows] + [w for (w, _) in outs])
    vm = 2 * sum(_nbytes((t, w), a.dtype) for (a, w, _) in rows) + 2 * sum(_nbytes((t, w), dt) for (w, dt) in outs)
    vm += temps * _nbytes((t, widest), F32) + (2 << 20)
    res = pl.pallas_call(
        body, name=name, grid=(nrows // t,), in_specs=in_specs, out_specs=out_specs, out_shape=out_shape,
        compiler_params=_params(vm),
    )(*[_hbm(a) for (a, _, _) in rows], *consts)
    return list(res)


def _sb_tiles(s, tq, tk):
    tq = _pick(s, (tq, 256, 128))
    tk = _pick(tq, (tk, 128))
    return tq, tk, tq // tk


def _sb_fwd(zm, heads, *, name, tq=512, tk=256):
    s = zm.shape[0]
    tq, tk, nd = _sb_tiles(s, tq, tk)
    scale = SB_HD ** -0.5

    def body(q_ref, k_ref, v_ref, o_ref, a_out, stage, sem):
        h, i = pl.program_id(0), pl.program_id(1)
        qb = (q_ref[...] * scale).astype(BF16)
        r = lax.broadcasted_iota(jnp.int32, (tq, tk), 0)
        c = lax.broadcasted_iota(jnp.int32, (tq, tk), 1)
        ur = lax.broadcasted_iota(jnp.int32, (tk, tk), 0)
        uc = lax.broadcasted_iota(jnp.int32, (tk, tk), 1)
        usuf = (ur > uc).astype(BF16)

        def out_copy(slot, j):
            return pltpu.make_async_copy(stage.at[slot], a_out.at[h, i, j], sem.at[slot])

        def tile(j, carry, causal, slot, reuse):
            acc, cl = carry
            if reuse is True:
                out_copy(slot, 0).wait()
            elif reuse is not None:
                @pl.when(reuse)
                def _():
                    out_copy(slot, 0).wait()
            rows = pl.ds(pl.multiple_of(j * tk, tk), tk)
            kb = k_ref[rows, :].astype(BF16)
            vb = v_ref[rows, :].astype(BF16)
            z = lax.dot_general(qb, kb, NT, preferred_element_type=F32)
            lsig = -_softplus(z)
            l = lsig if causal is None else jnp.where(causal, lsig, 0.0)
            loga = z + lsig + _dot01(l, usuf) + cl
            if causal is not None:
                loga = jnp.where(causal, loga, NEG)
            ab = jnp.exp(loga).astype(BF16)
            acc = acc + lax.dot_general(ab, vb, NN, preferred_element_type=F32)
            stage[slot] = ab
            out_copy(slot, j).start()
            return acc, cl + jnp.sum(l, axis=1, keepdims=True)

        carry = (jnp.zeros((tq, SB_HD), F32), jnp.zeros((tq, 1), F32))
        for n, dd in enumerate(range(nd - 1, -1, -1)):
            carry = tile(i * nd + dd, carry, c + dd * tk < r, n % 2, None)

        def rest(n, cr):
            return tile(i * nd - 1 - n, cr, None, (nd + n) % 2, True if nd >= 2 else nd + n >= 2)

        acc, _ = lax.fori_loop(0, i * nd, rest, carry)
        total = (i + 1) * nd
        out_copy((total - 1) % 2, 0).wait()

        @pl.when(total >= 2)
        def _():
            out_copy(total % 2, 0).wait()

        o_ref[...] = acc.astype(o_ref.dtype)

    assert nd <= 2
    blk = lambda off: pl.BlockSpec((s, SB_HD), functools.partial(lambda h, i, off: (0, off + h), off=off))
    return pl.pallas_call(
        body, name=name, grid=(heads, s // tq),
        in_specs=[pl.BlockSpec((tq, SB_HD), lambda h, i: (i, h)), blk(heads), blk(2 * heads)],
        out_specs=[pl.BlockSpec((tq, SB_HD), lambda h, i: (i, h)), ANY],
        out_shape=[pltpu.HBM((s, heads * SB_HD), BF16), pltpu.HBM((heads, s // tq, s // tk, tq, tk), BF16)],
        scratch_shapes=[pltpu.VMEM((2, tq, tk), BF16), pltpu.SemaphoreType.DMA((2,))],
        compiler_params=_params(8 * s * SB_HD * 4 + 24 * tq * tk * 4 + (8 << 20)),
    )(_hbm(zm), _hbm(zm), _hbm(zm))


def _sb_bwd(zm, dy, a_all, after, heads, *, name, tq=512, tk=256):
    s = zm.shape[0]
    tq, tk, nd = _sb_tiles(s, tq, tk)
    nq = s // tq
    scale = SB_HD ** -0.5

    def body(q_ref, k_ref, v_ref, do_ref, a_in, after_ref, dq_ref, dk_ref, dv_ref, dka, dva, abuf, sem):
        h, i = pl.program_id(0), pl.program_id(1)

        @pl.when(i == 0)
        def _():
            dka[...] = jnp.zeros_like(dka)
            dva[...] = jnp.zeros_like(dva)

        qb = (q_ref[...] * scale).astype(BF16)
        dob = do_ref[...].astype(BF16)
        r = lax.broadcasted_iota(jnp.int32, (tq, tk), 0)
        c = lax.broadcasted_iota(jnp.int32, (tq, tk), 1)
        ur = lax.broadcasted_iota(jnp.int32, (tk, tk), 0)
        uc = lax.broadcasted_iota(jnp.int32, (tk, tk), 1)
        uexcl = (ur < uc).astype(BF16)

        def fetch(j, slot):
            return pltpu.make_async_copy(a_in.at[h, i, j], abuf.at[slot], sem.at[slot])

        def tile(j, carry, causal, slot, more):
            dq, cg = carry
            fetch(j, slot).wait()
            if more:
                fetch(j + 1, 1 - slot).start()
            rows = pl.ds(pl.multiple_of(j * tk, tk), tk)
            kb = k_ref[rows, :].astype(BF16)
            vb = v_ref[rows, :].astype(BF16)
            z = lax.dot_general(qb, kb, NT, preferred_element_type=F32)
            sig = 1.0 / (1.0 + jnp.exp(-z))
            ab = abuf[slot]
            g = ab.astype(F32) * lax.dot_general(dob, vb, NT, preferred_element_type=F32)
            p = cg + lax.dot_general(g.astype(BF16), uexcl, NN, preferred_element_type=F32)
            dz = g - sig * (g + p)
            if causal is not None:
                dz = jnp.where(causal, dz, 0.0)
            dzb = dz.astype(BF16)
            dva[rows, :] += lax.dot_general(ab, dob, TN, preferred_element_type=F32)
            dka[rows, :] += lax.dot_general(dzb, qb, TN, preferred_element_type=F32)
            dq = dq + lax.dot_general(dzb, kb, NN, preferred_element_type=F32)
            return dq, cg + jnp.sum(g, axis=1, keepdims=True)

        fetch(0, 0).start()
        init = (jnp.zeros((tq, SB_HD), F32), jnp.zeros((tq, 1), F32))
        carry = lax.fori_loop(0, i * nd, lambda j, cr: tile(j, cr, None, j % 2, True), init)
        for dd in range(nd):
            carry = tile(i * nd + dd, carry, c + dd * tk < r, (i * nd + dd) % 2, dd + 1 < nd)
        dq_ref[...] = (carry[0] * scale).astype(dq_ref.dtype)

        @pl.when(i == nq - 1)
        def _():
            dk_ref[...] = dka[...].astype(dk_ref.dtype)
            dv_ref[...] = dva[...].astype(dv_ref.dtype)

    blk = lambda off: pl.BlockSpec((s, SB_HD), functools.partial(lambda h, i, off: (0, off + h), off=off))
    tile_spec = pl.BlockSpec((tq, SB_HD), lambda h, i: (i, h))
    full = pltpu.HBM((s, heads * SB_HD), BF16)
    return pl.pallas_call(
        body, name=name, grid=(heads, nq),
        in_specs=[tile_spec, blk(heads), blk(2 * heads), tile_spec, ANY, ANY],
        out_specs=[tile_spec, blk(0), blk(0)],
        out_shape=[full, full, full],
        scratch_shapes=[pltpu.VMEM((s, SB_HD), F32), pltpu.VMEM((s, SB_HD), F32), pltpu.VMEM((2, tq, tk), BF16),
                        pltpu.SemaphoreType.DMA((2,))],
        compiler_params=_params(12 * s * SB_HD * 4 + 32 * tq * tk * 4 + (8 << 20)),
    )(_hbm(zm), _hbm(zm), _hbm(zm), _hbm(dy), a_all, after)


def _conv_taps(u, w_ref, rows_i):
    taps = []
    for j in range(CONV_W):
        sh = CONV_W - 1 - j
        if sh == 0:
            taps.append(u)
        else:
            taps.append(jnp.where(rows_i >= sh, pltpu.roll(u, sh, 0), 0.0))
    return taps


def _conv_fwd(zm, col0, width, cw, cb, *, name):
    s = zm.shape[0]
    bw = _pick(width, (LANES,))
    off = col0 // bw

    def body(u_ref, w_ref, b_ref, o_ref):
        u = u_ref[...]
        rows_i = lax.broadcasted_iota(jnp.int32, u.shape, 0)
        acc = jnp.broadcast_to(b_ref[...], u.shape)
        for j, tp in enumerate(_conv_taps(u, w_ref, rows_i)):
            acc = acc + tp * w_ref[j:j + 1, :]
        o_ref[...] = acc * _sigmoid(acc)

    return pl.pallas_call(
        body, name=name, grid=(width // bw,),
        in_specs=[pl.BlockSpec((s, bw), lambda j: (0, off + j)), pl.BlockSpec((CONV_W, bw), lambda j: (0, j)),
                  pl.BlockSpec((1, bw), lambda j: (0, j))],
        out_specs=pl.BlockSpec((s, bw), lambda j: (0, j)),
        out_shape=pltpu.HBM((s, width), F32),
        compiler_params=_params(12 * s * bw * 4 + (4 << 20)),
    )(_hbm(zm), cw, cb)


def _conv_bwd(zm, col0, width, cw, cb, dqk, *, name):
    s = zm.shape[0]
    bw = _pick(width, (LANES,))
    off = col0 // bw

    def body(u_ref, w_ref, b_ref, d_ref, du_ref, dw_ref, db_ref):
        u = u_ref[...]
        rows_i = lax.broadcasted_iota(jnp.int32, u.shape, 0)
        taps = _conv_taps(u, w_ref, rows_i)
        acc = jnp.broadcast_to(b_ref[...], u.shape)
        for j, tp in enumerate(taps):
            acc = acc + tp * w_ref[j:j + 1, :]
        sg = _sigmoid(acc)
        dc = d_ref[...] * (sg * (1.0 + acc * (1.0 - sg)))
        du = jnp.zeros_like(u)
        for j in range(CONV_W):
            sh = CONV_W - 1 - j
            if sh == 0:
                du = du + dc * w_ref[j:j + 1, :]
            else:
                du = du + jnp.where(rows_i < s - sh, pltpu.roll(dc, s - sh, 0), 0.0) * w_ref[j:j + 1, :]
            dw_ref[j:j + 1, :] = jnp.sum(dc * taps[j], axis=0, keepdims=True)
        du_ref[...] = du.astype(du_ref.dtype)
        db_ref[...] = jnp.sum(dc, axis=0, keepdims=True)

    return pl.pallas_call(
        body, name=name, grid=(width // bw,),
        in_specs=[pl.BlockSpec((s, bw), lambda j: (0, off + j)), pl.BlockSpec((CONV_W, bw), lambda j: (0, j)),
                  pl.BlockSpec((1, bw), lambda j: (0, j)), pl.BlockSpec((s, bw), lambda j: (0, j))],
        out_specs=[pl.BlockSpec((s, bw), lambda j: (0, j)), pl.BlockSpec((CONV_W, bw), lambda j: (0, j)),
                   pl.BlockSpec((1, bw), lambda j: (0, j))],
        out_shape=[pltpu.HBM((s, width), BF16), pltpu.HBM((CONV_W, width), F32),
                   pltpu.HBM((1, width), F32)],
        compiler_params=_params(20 * s * bw * 4 + (4 << 20)),
    )(_hbm(zm), cw, cb, _hbm(dqk))


def _ml_gates(gcol_ref, grow_ref):
    l = CHUNK
    r = lax.broadcasted_iota(jnp.int32, (l, l), 0)
    c = lax.broadcasted_iota(jnp.int32, (l, l), 1)
    gcol = gcol_ref[...]
    grow = grow_ref[0]
    bcol = _u01dot((c <= r).astype(BF16), gcol)
    brow = _dot01(grow, (r <= c).astype(BF16))
    return gcol, grow, bcol, brow, r >= c


def _ml_chunk(h, dh, mq_ref, mk_ref, v_ref, gates, cp, n_prev, m_prev):
    gcol, grow, bcol, brow, tri = gates
    l = CHUNK
    sl = slice(h * dh, (h + 1) * dh)
    qc = mq_ref[:, sl]
    kc = mk_ref[:, sl] * (dh ** -0.5)
    vc = v_ref[:, sl]
    i_row = grow[h:h + 1, :]
    i_col = gcol[:, h:h + 1]
    b_col = bcol[:, ML_HEADS + h:ML_HEADS + h + 1]
    b_row = brow[ML_HEADS + h:ML_HEADS + h + 1, :]
    b_end = b_col[l - 1:l, :]
    d = jnp.where(tri, b_col - b_row + i_row, -jnp.inf)
    m_inter = b_col + m_prev
    m_t = jnp.maximum(m_inter, jnp.max(d, axis=1, keepdims=True))
    w = jnp.exp(d - m_t)
    s_inter = jnp.exp(m_inter - m_t)
    qb, kb, vb = qc.astype(BF16), kc.astype(BF16), vc.astype(BF16)
    cpb = cp.astype(BF16)
    a = lax.dot_general(qb, kb, NT, preferred_element_type=F32)
    sc = a * w
    qcp = lax.dot_general(qb, cpb, NT, preferred_element_type=F32)
    qn = jnp.sum(qc * n_prev, axis=1, keepdims=True)
    num = lax.dot_general(sc.astype(BF16), vb, NN, preferred_element_type=F32) + s_inter * qcp
    den = jnp.sum(sc, axis=1, keepdims=True) + s_inter * qn
    floor = jnp.exp(-m_t)
    dnm = jnp.maximum(jnp.abs(den), floor)
    g_col = b_end - b_col + i_col
    g_row = b_end - b_row + i_row
    m_new = jnp.maximum(b_end + m_prev, jnp.max(g_row, axis=1, keepdims=True))
    decay = jnp.exp(b_end + m_prev - m_new)
    wk = jnp.exp(g_col - m_new)
    return dict(qc=qc, kc=kc, vc=vc, qb=qb, kb=kb, vb=vb, cpb=cpb, w=w, s_inter=s_inter, a=a, sc=sc, qcp=qcp, qn=qn,
                num=num, den=den, floor=floor, dnm=dnm, m_new=m_new, decay=decay, wk=wk, sl=sl)


def _ml_fwd(mqk, zm, vcol, gcol, grow, d_model, *, name):
    s = zm.shape[0]
    nc = s // CHUNK
    dh = d_model // ML_HEADS
    hh = ML_HEADS

    def body(mq_ref, mk_ref, v_ref, gcol_ref, grow_ref, h_ref, cs_ref, ns_ref, ms_ref, c_s, n_s, m_s):
        @pl.when(pl.program_id(0) == 0)
        def _():
            c_s[...] = jnp.zeros_like(c_s)
            n_s[...] = jnp.zeros_like(n_s)
            m_s[...] = jnp.zeros_like(m_s)

        gates = _ml_gates(gcol_ref, grow_ref)
        for h in range(hh):
            cp, n_prev, m_prev = c_s[h], n_s[h], m_s[h][:, 0:1]
            cs_ref[0, h] = cp
            ns_ref[0, h] = n_prev
            ms_ref[0, h] = m_s[h]
            f = _ml_chunk(h, dh, mq_ref, mk_ref, v_ref, gates, cp, n_prev, m_prev)
            h_ref[:, f["sl"]] = f["num"] / f["dnm"]
            c_s[h] = f["decay"] * cp + lax.dot_general((f["vc"] * f["wk"]).astype(BF16), f["kb"], TN,
                                                       preferred_element_type=F32)
            n_s[h] = f["decay"] * n_prev + jnp.sum(f["wk"] * f["kc"], axis=0, keepdims=True)
            m_s[h] = jnp.broadcast_to(f["m_new"], (1, LANES))

    dblk = d_model
    return pl.pallas_call(
        body, name=name, grid=(nc,),
        in_specs=[pl.BlockSpec((CHUNK, dblk), lambda c: (c, 0)), pl.BlockSpec((CHUNK, dblk), lambda c: (c, 1)),
                  pl.BlockSpec((CHUNK, dblk), lambda c: (c, vcol // dblk)),
                  pl.BlockSpec((CHUNK, LANES), lambda c: (c, 0)), pl.BlockSpec((1, 8, CHUNK), lambda c: (c, 0, 0))],
        out_specs=[pl.BlockSpec((CHUNK, dblk), lambda c: (c, 0)),
                   pl.BlockSpec((1, hh, dh, dh), lambda c: (c, 0, 0, 0)),
                   pl.BlockSpec((1, hh, 1, dh), lambda c: (c, 0, 0, 0)),
                   pl.BlockSpec((1, hh, 1, LANES), lambda c: (c, 0, 0, 0))],
        out_shape=[pltpu.HBM((s, d_model), F32), pltpu.HBM((nc, hh, dh, dh), F32),
                   pltpu.HBM((nc, hh, 1, dh), F32), pltpu.HBM((nc, hh, 1, LANES), F32)],
        scratch_shapes=[pltpu.VMEM((hh, dh, dh), F32), pltpu.VMEM((hh, 1, dh), F32), pltpu.VMEM((hh, 1, LANES), F32)],
        compiler_params=_params(8 * hh * dh * dh * 4 + (16 << 20)),
    )(_hbm(mqk), _hbm(mqk), _hbm(zm), _hbm(gcol), _hbm(grow))


def _ml_bwd(mqk, zm, vcol, gcol, grow, cs, ns, ms, dhm, d_model, *, name):
    s = zm.shape[0]
    nc = s // CHUNK
    dh = d_model // ML_HEADS
    hh = ML_HEADS
    l = CHUNK

    def body(mq_ref, mk_ref, v_ref, gcol_ref, grow_ref, cs_ref, ns_ref, ms_ref, dh_ref,
             dqk_ref, dv_ref, dgc_ref, dgr_ref, dc_s, dn_s):
        @pl.when(pl.program_id(0) == 0)
        def _():
            dc_s[...] = jnp.zeros_like(dc_s)
            dn_s[...] = jnp.zeros_like(dn_s)

        gates = _ml_gates(gcol_ref, grow_ref)
        lane = lax.broadcasted_iota(jnp.int32, (l, LANES), 1)
        rowi = lax.broadcasted_iota(jnp.int32, (8, l), 0)
        lastrow = lax.broadcasted_iota(jnp.int32, (l, 1), 0) == l - 1
        dgc = jnp.zeros((l, LANES), F32)
        dgr = jnp.zeros((8, l), F32)
        for h in range(hh):
            cp, n_prev, m_prev = cs_ref[0, h], ns_ref[0, h], ms_ref[0, h][:, 0:1]
            f = _ml_chunk(h, dh, mq_ref, mk_ref, v_ref, gates, cp, n_prev, m_prev)
            dC, dn = dc_s[h], dn_s[h]
            dhv = dh_ref[:, f["sl"]]
            dnum = dhv / f["dnm"]
            hv = f["num"] / f["dnm"]
            ddnm = -jnp.sum(dhv * hv, axis=1, keepdims=True) / f["dnm"]
            dden = jnp.where(jnp.abs(f["den"]) >= f["floor"], ddnm * jnp.sign(f["den"]), 0.0)
            dnb = dnum.astype(BF16)
            dsc = lax.dot_general(dnb, f["vb"], NT, preferred_element_type=F32) + dden
            dvc = lax.dot_general(f["sc"].astype(BF16), dnb, TN, preferred_element_type=F32)
            ds_inter = jnp.sum(dnum * f["qcp"], axis=1, keepdims=True) + dden * f["qn"]
            sdn = (f["s_inter"] * dnum).astype(BF16)
            sdd = f["s_inter"] * dden
            da = dsc * f["w"]
            dab = da.astype(BF16)
            dqc = (lax.dot_general(dab, f["kb"], NN, preferred_element_type=F32)
                   + lax.dot_general(sdn, f["cpb"], NN, preferred_element_type=F32) + sdd * n_prev)
            dcp = f["decay"] * dC + lax.dot_general(sdn, f["qb"], TN, preferred_element_type=F32)
            dnp = f["decay"] * dn + jnp.sum(sdd * f["qc"], axis=0, keepdims=True)
            vw = (f["vc"] * f["wk"]).astype(BF16)
            dCb = dC.astype(BF16)
            dkc = (lax.dot_general(dab, f["qb"], TN, preferred_element_type=F32)
                   + lax.dot_general(vw, dCb, NN, preferred_element_type=F32) + f["wk"] * dn)
            e = lax.dot_general(f["kb"], dCb, NT, preferred_element_type=F32)
            dvc = dvc + e * f["wk"]
            dwk = jnp.sum(e * f["vc"], axis=1, keepdims=True) + jnp.sum(f["kc"] * dn, axis=1, keepdims=True)
            ddecay = jnp.sum(jnp.sum(dC * cp, axis=1, keepdims=True), axis=0, keepdims=True) \
                + jnp.sum(dn * n_prev, axis=1, keepdims=True)
            dd = dsc * f["sc"]
            dlw = dwk * f["wk"]
            db_end = jnp.sum(dlw, axis=0, keepdims=True) + ddecay * f["decay"]
            di_col = dlw
            db_col = jnp.sum(dd, axis=1, keepdims=True) + ds_inter * f["s_inter"] - dlw \
                + jnp.where(lastrow, db_end, 0.0)
            cs_dd = jnp.sum(dd, axis=0, keepdims=True)
            dgc = dgc + jnp.where(lane == h, di_col, 0.0) + jnp.where(lane == hh + h, db_col, 0.0)
            dgr = dgr + jnp.where(rowi == h, cs_dd, 0.0) - jnp.where(rowi == hh + h, cs_dd, 0.0)
            dqk_ref[:, f["sl"]] = dqc
            dqk_ref[:, d_model + h * dh:d_model + (h + 1) * dh] = dkc * (dh ** -0.5)
            dv_ref[:, f["sl"]] = dvc.astype(dv_ref.dtype)
            dc_s[h] = dcp
            dn_s[h] = dnp
        dgc_ref[...] = dgc
        dgr_ref[0] = dgr

    dblk = d_model
    rev = lambda c: nc - 1 - c
    return pl.pallas_call(
        body, name=name, grid=(nc,),
        in_specs=[pl.BlockSpec((l, dblk), lambda c: (rev(c), 0)), pl.BlockSpec((l, dblk), lambda c: (rev(c), 1)),
                  pl.BlockSpec((l, dblk), lambda c: (rev(c), vcol // dblk)),
                  pl.BlockSpec((l, LANES), lambda c: (rev(c), 0)), pl.BlockSpec((1, 8, l), lambda c: (rev(c), 0, 0)),
                  pl.BlockSpec((1, hh, dh, dh), lambda c: (rev(c), 0, 0, 0)),
                  pl.BlockSpec((1, hh, 1, dh), lambda c: (rev(c), 0, 0, 0)),
                  pl.BlockSpec((1, hh, 1, LANES), lambda c: (rev(c), 0, 0, 0)),
                  pl.BlockSpec((l, dblk), lambda c: (rev(c), 0))],
        out_specs=[pl.BlockSpec((l, 2 * dblk), lambda c: (rev(c), 0)),
                   pl.BlockSpec((l, dblk), lambda c: (rev(c), 0)), pl.BlockSpec((l, LANES), lambda c: (rev(c), 0)),
                   pl.BlockSpec((1, 8, l), lambda c: (rev(c), 0, 0))],
        out_shape=[pltpu.HBM((s, 2 * d_model), F32),
                   pltpu.HBM((s, d_model), BF16), pltpu.HBM((s, LANES), F32),
                   pltpu.HBM((nc, 8, l), F32)],
        scratch_shapes=[pltpu.VMEM((hh, dh, dh), F32), pltpu.VMEM((hh, 1, dh), F32)],
        compiler_params=_params(10 * hh * dh * dh * 4 + (16 << 20)),
    )(*[_hbm(a) for a in (mqk, mqk, zm, gcol, grow, cs, ns, ms, dhm)])


def _xa_fwd(zm, qcol, kv, gq, gk, d_model, *, name, tq=256):
    s = zm.shape[0]
    nm = kv.shape[0]
    dh = d_model // X_HEADS
    tq = _pick(s, (tq, 128, 64))
    scale = dh ** -0.5

    def body(q_ref, k_ref, v_ref, gq_ref, gk_ref, o_ref):
        qn = _rms_fwd(q_ref[...], gq_ref[...])
        kn = _rms_fwd(k_ref[...], gk_ref[...])
        lg = _dot(qn, kn, NT) * scale
        lg = lg - jnp.max(lg, axis=1, keepdims=True)
        p = jnp.exp(lg)
        p = p / jnp.sum(p, axis=1, keepdims=True)
        o_ref[...] = _dot(p, v_ref[...], NN).astype(o_ref.dtype)

    return pl.pallas_call(
        body, name=name, grid=(X_HEADS, s // tq),
        in_specs=[pl.BlockSpec((tq, dh), lambda h, i: (i, qcol // dh + h)), pl.BlockSpec((nm, dh), lambda h, i: (0, h)),
                  pl.BlockSpec((nm, dh), lambda h, i: (0, X_HEADS + h)),
                  pl.BlockSpec((1, dh), lambda h, i: (0, 0)), pl.BlockSpec((1, dh), lambda h, i: (0, 0))],
        out_specs=pl.BlockSpec((tq, dh), lambda h, i: (i, h)),
        out_shape=pltpu.HBM((s, d_model), BF16),
        compiler_params=_params(32 << 20),
    )(_hbm(zm), _hbm(kv), _hbm(kv), gq, gk)


def _xa_bwd(zm, qcol, kv, gq, gk, dy, d_model, *, name, tq=256):
    s = zm.shape[0]
    nm = kv.shape[0]
    dh = d_model // X_HEADS
    tq = _pick(s, (tq, 128, 64))
    nq = s // tq
    scale = dh ** -0.5

    def body(q_ref, k_ref, v_ref, gq_ref, gk_ref, do_ref, dq_ref, dkn_ref, dv_ref, dgq_ref):
        h, i = pl.program_id(0), pl.program_id(1)

        @pl.when(i == 0)
        def _():
            dkn_ref[...] = jnp.zeros_like(dkn_ref)
            dv_ref[...] = jnp.zeros_like(dv_ref)

        @pl.when((i == 0) & (h == 0))
        def _():
            dgq_ref[...] = jnp.zeros_like(dgq_ref)

        q = q_ref[...]
        qn = _rms_fwd(q, gq_ref[...])
        kn = _rms_fwd(k_ref[...], gk_ref[...])
        lg = _dot(qn, kn, NT) * scale
        lg = lg - jnp.max(lg, axis=1, keepdims=True)
        p = jnp.exp(lg)
        p = p / jnp.sum(p, axis=1, keepdims=True)
        do = do_ref[...]
        dv_ref[...] += _dot(p, do, TN)
        dp = _dot(do, v_ref[...], NT)
        dlg = p * (dp - jnp.sum(dp * p, axis=1, keepdims=True)) * scale
        dqn = _dot(dlg, kn, NN)
        dkn_ref[...] += _dot(dlg, qn, TN)
        dq, dgq = _rms_bwd(q, gq_ref[...], dqn)
        dq_ref[...] = dq.astype(dq_ref.dtype)
        dgq_ref[...] += jnp.sum(dgq, axis=0, keepdims=True)

    return pl.pallas_call(
        body, name=name, grid=(X_HEADS, nq),
        in_specs=[pl.BlockSpec((tq, dh), lambda h, i: (i, qcol // dh + h)), pl.BlockSpec((nm, dh), lambda h, i: (0, h)),
                  pl.BlockSpec((nm, dh), lambda h, i: (0, X_HEADS + h)),
                  pl.BlockSpec((1, dh), lambda h, i: (0, 0)), pl.BlockSpec((1, dh), lambda h, i: (0, 0)),
                  pl.BlockSpec((tq, dh), lambda h, i: (i, h))],
        out_specs=[pl.BlockSpec((tq, dh), lambda h, i: (i, h)), pl.BlockSpec((nm, dh), lambda h, i: (0, h)),
                   pl.BlockSpec((nm, dh), lambda h, i: (0, h)), pl.BlockSpec((1, dh), lambda h, i: (0, 0))],
        out_shape=[pltpu.HBM((s, d_model), BF16), pltpu.HBM((nm, d_model), F32),
                   pltpu.HBM((nm, d_model), F32), pltpu.HBM((1, dh), F32)],
        compiler_params=_params(32 << 20),
    )(_hbm(zm), _hbm(kv), _hbm(kv), gq, gk, _hbm(dy))


def _place():
    return lax.axis_index("x"), lax.axis_index("y"), lax.axis_index("c")


ANY = pl.BlockSpec(memory_space=pl.ANY)


def _allgather_two_level(big, small, *, name, chunk_rows=64):
    r = big.shape[0]
    half = r // 2
    nr = _pick(half, (chunk_rows, 32, 16))
    nq = half // nr

    def body(big_ref, small_ref, obig, osmall, send, recv, fsend, frecv, ssend, srecv, loc):
        x, y, c = _place()
        k = 2 * x + y
        chips = [(1 - x, y), (x, 1 - y), (1 - x, 1 - y)]
        own = [pltpu.make_async_copy(big_ref, obig.at[k], loc.at[0]),
               pltpu.make_async_copy(small_ref, osmall.at[k], loc.at[1])]
        for cp in own:
            cp.start()

        def rows(h, q):
            return pl.ds(pl.multiple_of(h * half + q * nr, nr), nr)

        def over_ici(j, q, slot, h):
            return pltpu.make_async_remote_copy(
                src_ref=big_ref.at[rows(h, q)], dst_ref=obig.at[slot, rows(h, q)], send_sem=send.at[nq * j + q],
                recv_sem=recv.at[nq * j + q], device_id=(chips[j][0], chips[j][1], c), device_id_type=MESH)

        def to_sibling(j, q, h):
            slot = 2 * chips[j][0] + chips[j][1]
            return pltpu.make_async_remote_copy(
                src_ref=obig.at[slot, rows(h, q)], dst_ref=obig.at[slot, rows(h, q)], send_sem=fsend.at[nq * j + q],
                recv_sem=frecv.at[nq * j + q], device_id=(x, y, 1 - c), device_id_type=MESH)

        def small_copy(j, slot):
            return pltpu.make_async_remote_copy(
                src_ref=small_ref, dst_ref=osmall.at[slot], send_sem=ssend.at[j], recv_sem=srecv.at[j],
                device_id=(chips[j][0], chips[j][1], c), device_id_type=MESH)

        for q in range(nq):
            for j in range(3):
                over_ici(j, q, k, c).start()
        for j in range(3):
            small_copy(j, k).start()
        for q in range(nq):
            for j in range(3):
                over_ici(j, q, 2 * chips[j][0] + chips[j][1], c).wait_recv()
                to_sibling(j, q, c).start()
        for q in range(nq):
            for j in range(3):
                to_sibling(j, q, 1 - c).wait_recv()
        for j in range(3):
            small_copy(j, 2 * chips[j][0] + chips[j][1]).wait_recv()
            small_copy(j, k).wait_send()
        for q in range(nq):
            for j in range(3):
                over_ici(j, q, k, c).wait_send()
                to_sibling(j, q, c).wait_send()
        for cp in own:
            cp.wait()

    return pl.pallas_call(
        body, name=name, in_specs=[ANY] * 2, out_specs=[ANY] * 2,
        out_shape=[pltpu.HBM((4,) + big.shape, big.dtype), pltpu.HBM((4,) + small.shape, small.dtype)],
        scratch_shapes=[pltpu.SemaphoreType.DMA((3 * nq,))] * 4
        + [pltpu.SemaphoreType.DMA((3,)), pltpu.SemaphoreType.DMA((3,)), pltpu.SemaphoreType.DMA((2,))],
    )(big, small)


HBM_SPEC = pl.BlockSpec(memory_space=pltpu.HBM)
SEM_SPEC = pl.BlockSpec(memory_space=pltpu.SEMAPHORE)
EFFECT = pltpu.SideEffectType.DATAFLOW_SIDE_EFFECTING


def _split_copies(kind, srcs, lands, send, recv):
    x, y, c = _place()
    if kind == "quarters":
        peers = [(1 - x, y, c), (x, 1 - y, c), (1 - x, 1 - y, c)]
    else:
        peers = [(x ^ ((j >> 2) & 1), y ^ ((j >> 1) & 1), c ^ (j & 1)) for j in range(1, 8)]
    npeer = len(peers)
    out = []
    for t in range(len(srcs)):
        for j, (px, py, pc) in enumerate(peers):
            if kind == "quarters":
                src, mine, theirs = srcs[t], 2 * x + y, 2 * px + py
            else:
                src, mine, theirs = srcs[t].at[2 * px + py, pc], 4 * x + 2 * y + c, 4 * px + 2 * py + pc
            mk = functools.partial(
                pltpu.make_async_remote_copy, src_ref=src, send_sem=send.at[npeer * t + j],
                recv_sem=recv.at[npeer * t + j], device_id=(px, py, pc), device_id_type=MESH)
            out.append((functools.partial(mk, dst_ref=lands[t].at[mine]),
                        functools.partial(mk, dst_ref=lands[t].at[theirs])))
    return out


def _split_start(kind, srcs, land_shapes, after, *, name):
    n = len(srcs)
    ncopies = n * (3 if kind == "quarters" else 7)

    def body(*refs):
        ins, lands = refs[:n], refs[n:2 * n]
        send, recv = refs[2 * n + 1], refs[2 * n + 2]
        token = refs[-1]
        for start, _ in _split_copies(kind, ins, lands, send, recv):
            start().start()
        token[...] = jnp.zeros_like(token)

    lands = [_hbm(lax.empty(shp, a.dtype)) for shp, a in zip(land_shapes, srcs)]
    res = pl.pallas_call(
        body, name=name, in_specs=[HBM_SPEC] * (2 * n) + [ANY],
        out_specs=[SEM_SPEC, SEM_SPEC] + [HBM_SPEC] * (2 * n) + [pl.BlockSpec(memory_space=pltpu.VMEM)],
        out_shape=[pltpu.SemaphoreType.DMA((ncopies,)), pltpu.SemaphoreType.DMA((ncopies,))]
        + [pltpu.HBM(a.shape, a.dtype) for a in srcs] + [pltpu.HBM(shp, a.dtype) for shp, a in zip(land_shapes, srcs)]
        + [jax.ShapeDtypeStruct((8, LANES), F32)],
        input_output_aliases={i: 2 + i for i in range(2 * n)},
        compiler_params=pltpu.CompilerParams(has_side_effects=EFFECT),
    )(*[_hbm(a) for a in srcs], *lands, after)
    return res[0], res[1], list(res[2:2 + n]), list(res[2 + n:2 + 2 * n]), res[-1]


def _split_wait(kind, send, recv, srcs, lands, after, *, name):
    n = len(srcs)

    def body(*refs):
        ins, lnd = refs[:n], refs[n:2 * n]
        snd, rcv = refs[2 * n], refs[2 * n + 1]
        for start, arrive in _split_copies(kind, ins, lnd, snd, rcv):
            start().wait_send()
            arrive().wait_recv()

    res = pl.pallas_call(
        body, name=name, in_specs=[HBM_SPEC] * (2 * n) + [SEM_SPEC, SEM_SPEC] + [ANY] * len(after),
        out_specs=[HBM_SPEC] * (2 * n),
        out_shape=[pltpu.HBM(a.shape, a.dtype) for a in srcs] + [pltpu.HBM(a.shape, a.dtype) for a in lands],
        input_output_aliases={i: i for i in range(2 * n)},
        compiler_params=pltpu.CompilerParams(has_side_effects=EFFECT),
    )(*srcs, *lands, send, recv, *after)
    return list(res[n:])


def _sum8(parts, *, name):
    _, r, c = parts.shape
    t = _pick(r, (128, 64, 32, 16, 8))

    def body(p_ref, o_ref):
        acc = p_ref[0].astype(F32)
        for k in range(1, 8):
            acc = acc + p_ref[k].astype(F32)
        o_ref[...] = acc

    return pl.pallas_call(
        body, name=name, grid=(r // t,), in_specs=[pl.BlockSpec((8, t, c), lambda i: (0, i, 0))],
        out_specs=pl.BlockSpec((t, c), lambda i: (i, 0)), out_shape=pltpu.HBM((r, c), F32),
        compiler_params=_params(2 * 8 * t * c * 2 + 6 * t * c * 4 + (4 << 20)),
    )(_hbm(parts))


def _swap_halves(halves, *, name, chunk_bytes=512 * 1024):
    n = len(halves)
    items = []
    for t, a in enumerate(halves):
        r = a.shape[0]
        k = 1
        while _nbytes(a.shape, a.dtype) // k > chunk_bytes and r % (2 * k) == 0 and (r // (2 * k)) % 8 == 0:
            k *= 2
        items += [(t, q * (r // k), r // k) for q in range(k)]
    m = len(items)

    def body(*refs):
        ins, outs = refs[:n], refs[n:2 * n]
        sbuf, rbuf = refs[2 * n:3 * n], refs[3 * n:4 * n]
        send, recv, loc_own, loc_in, loc_out = refs[4 * n:]
        x, y, c = _place()
        local, stage = [], []
        for t in range(n):
            cp = pltpu.make_async_copy(ins[t], outs[t].at[c], loc_own.at[t])
            cp.start()
            local.append(cp)
        for q, (t, r0, nr) in enumerate(items):
            cp = pltpu.make_async_copy(ins[t].at[pl.ds(r0, nr)], sbuf[t].at[pl.ds(r0, nr)], loc_in.at[q])
            cp.start()
            stage.append(cp)

        def copy(q):
            t, r0, nr = items[q]
            return pltpu.make_async_remote_copy(
                src_ref=sbuf[t].at[pl.ds(r0, nr)], dst_ref=rbuf[t].at[pl.ds(r0, nr)], send_sem=send.at[q],
                recv_sem=recv.at[q], device_id=(x, y, 1 - c), device_id_type=MESH)

        for q in range(m):
            stage[q].wait()
            copy(q).start()
        for q, (t, r0, nr) in enumerate(items):
            copy(q).wait_recv()
            cp = pltpu.make_async_copy(rbuf[t].at[pl.ds(r0, nr)], outs[t].at[1 - c, pl.ds(r0, nr)], loc_out.at[q])
            cp.start()
            local.append(cp)
        for q in range(m):
            copy(q).wait_send()
        for cp in local:
            cp.wait()

    stage_bytes = 2 * sum(_nbytes(a.shape, a.dtype) for a in halves)
    return pl.pallas_call(
        body, name=name, in_specs=[ANY] * n, out_specs=[ANY] * n,
        out_shape=[pltpu.HBM((2,) + a.shape, a.dtype) for a in halves],
        scratch_shapes=[pltpu.VMEM(a.shape, a.dtype) for a in halves] * 2
        + [pltpu.SemaphoreType.DMA((m,)), pltpu.SemaphoreType.DMA((m,)), pltpu.SemaphoreType.DMA((n,)),
           pltpu.SemaphoreType.DMA((m,)), pltpu.SemaphoreType.DMA((m,))],
        compiler_params=_params(stage_bytes + (4 << 20)),
    )(*halves)


def _allreduce_small(p, after, *, name):
    r = p.shape[0]

    def body(p_ref, after_ref, o_ref, buf, send, recv):
        x, y, c = _place()
        me = 4 * x + 2 * y + c
        peers = [(x ^ ((j >> 2) & 1), y ^ ((j >> 1) & 1), c ^ (j & 1)) for j in range(1, 8)]

        def copy(j, slot):
            return pltpu.make_async_remote_copy(
                src_ref=p_ref, dst_ref=buf.at[slot], send_sem=send.at[j], recv_sem=recv.at[j],
                device_id=peers[j], device_id_type=MESH)

        for j in range(7):
            copy(j, me).start()
        buf[me] = p_ref[...]
        for j in range(7):
            px, py, pc = peers[j]
            copy(j, 4 * px + 2 * py + pc).wait_recv()
        for j in range(7):
            copy(j, me).wait_send()
        acc = buf[0]
        for k in range(1, 8):
            acc = acc + buf[k]
        o_ref[...] = acc

    vspec = pl.BlockSpec(memory_space=pltpu.VMEM)
    return pl.pallas_call(
        body, name=name, in_specs=[vspec, ANY], out_specs=vspec, out_shape=jax.ShapeDtypeStruct((r, LANES), F32),
        scratch_shapes=[pltpu.VMEM((8, r, LANES), F32), pltpu.SemaphoreType.DMA((7,)), pltpu.SemaphoreType.DMA((7,))],
    )(p, after)


def _adamw_fn(w, g, m, v):
    m = ADAM_B1 * m + (1.0 - ADAM_B1) * g
    v = ADAM_B2 * v + (1.0 - ADAM_B2) * (g * g)
    m_hat = m / (1.0 - ADAM_B1 ** ADAM_STEP)
    v_hat = v / (1.0 - ADAM_B2 ** ADAM_STEP)
    delta = -ADAM_LR * (m_hat / (jnp.sqrt(v_hat) + ADAM_EPS) + ADAM_WD * w)
    return delta, m, v


def _adamw(w, g, m, v, *, name):
    c = w.shape[1]
    return _rowwise(_adamw_fn, [w, g, m, v], [], [(c, F32)] * 3, name=name, tr=128)


def _pack(vecs, rows):
    flat = jnp.concatenate([a.reshape(-1).astype(F32) for a in vecs])
    return jnp.pad(flat, (0, rows * LANES - flat.shape[0])).reshape(rows, LANES)


def _unpack(p, like):
    flat, out, o = p.reshape(-1), [], 0
    for a in like:
        out.append(flat[o:o + a.size].reshape(a.shape))
        o += a.size
    return out


def kernel(x, mem, g_mix, w_in, b_if, b_gate, conv_w, conv_b, ml_norm_g, g_mem, w_mem_kv, q_norm_g, k_norm_g, w_sb_proj, w_ml_proj, w_x_proj, w_out, g_mlp, w_ff1, w_ff2, loss_target, m_g_mix, m_w_in, m_b_if, m_b_gate, m_conv_w, m_conv_b, m_ml_norm_g, m_g_mem, m_w_mem_kv, m_q_norm_g, m_k_norm_g, m_w_sb_proj, m_w_ml_proj, m_w_x_proj, m_w_out, m_g_mlp, m_w_ff1, m_w_ff2, v_g_mix, v_w_in, v_b_if, v_b_gate, v_conv_w, v_conv_b, v_ml_norm_g, v_g_mem, v_w_mem_kv, v_q_norm_g, v_k_norm_g, v_w_sb_proj, v_w_ml_proj, v_w_x_proj, v_w_out, v_g_mlp, v_w_ff1, v_w_ff2):
    _, s, d = x.shape
    nm = mem.shape[1]
    n_in = 4 * w_in.shape[2]
    dff = 4 * w_ff1.shape[2]
    sbh = d // SB_HD
    hh = ML_HEADS
    dh = d // hh
    nc = s // CHUNK
    assert n_in == 11 * d + 2 * hh and d % (2 * LANES) == 0 and s % LANES == 0
    x2, mem2, tgt = x[0], mem[0], loss_target[0]

    k4 = 2 * lax.axis_index("x") + lax.axis_index("y")
    me = 2 * k4 + lax.axis_index("c")
    g_first = _allgather_two_level(w_in[0].astype(BF16), conv_w[0], name="gather_w_in")
    later = [a[0].astype(BF16) for a in (w_mem_kv, w_sb_proj, w_ml_proj, w_x_proj, w_out, w_ff1, w_ff2)]
    gw_send, gw_recv, gw_src, gw_land, gw_token = _split_start(
        "quarters", later, [(4,) + a.shape for a in later], g_first[0], name="gather_rest_start")
    cols = lambda a: a.transpose(1, 0, 2).reshape(a.shape[1], 4 * a.shape[2])
    rws = lambda a: a.reshape(4 * a.shape[1], a.shape[2])
    w_in_f = cols(g_first[0])
    w_main = jnp.concatenate([w_in_f[:, :7 * d], w_in_f[:, 7 * d + 2 * hh:]], axis=1)
    w_if = jnp.pad(w_in_f[:, 7 * d:7 * d + 2 * hh], ((0, 0), (0, LANES - 2 * hh)))
    conv_wf = cols(g_first[1])
    b_if_p = jnp.pad(b_if, ((0, 0), (0, LANES - 2 * hh)))

    (hn,) = _rowwise(_rms_fwd, [x2], [g_mix], [(d, BF16)], name="norm_in")
    zm = _mm(hn, w_main, after=gw_token, name="proj_in")
    zif = _mm(hn, w_if, name="proj_if")
    y_sb, a_sb = _sb_fwd(zm, sbh, name="sb_fwd")

    def gate_fn(z, b):
        pre = z + b
        lane = lax.broadcasted_iota(jnp.int32, pre.shape, 1)
        return jnp.where(lane < hh, pre, -_softplus(-pre))

    (gcol,) = _rowwise(gate_fn, [zif], [b_if_p], [(LANES, F32)], name="ml_gates")
    grow = gcol[:, :8].T.reshape(8, nc, CHUNK).transpose(1, 0, 2)
    mqk = _conv_fwd(zm, 3 * d, 2 * d, conv_wf, conv_b, name="conv_fwd")
    hm, cst, nst, mst = _ml_fwd(mqk, zm, 5 * d, gcol, grow, d, name="ml_fwd")

    def mlout_fn(hv, o, g):
        ys = [_rms_fwd(hv[:, k * dh:(k + 1) * dh], g[:, k * dh:(k + 1) * dh]) for k in range(hh)]
        return jnp.concatenate(ys, axis=1) * _sigmoid(o)

    (y_ml,) = _rowwise(mlout_fn, [hm, (zm, d, 6)], [ml_norm_g], [(d, BF16)], name="ml_out")
    gw_land = _split_wait("quarters", gw_send, gw_recv, gw_src, gw_land, [y_ml, y_sb], name="gather_rest_wait")
    gw = [lax.dynamic_update_index_in_dim(ld, a, k4, 0) for ld, a in zip(gw_land, later)]
    w_kv, w_sbp, w_mlp, w_xp, w_o, w_f1, w_f2 = (cols(gw[0]), rws(gw[1]), rws(gw[2]), rws(gw[3]), rws(gw[4]),
                                                 cols(gw[5]), rws(gw[6]))
    (memn,) = _rowwise(_rms_fwd, [mem2], [g_mem], [(d, BF16)], name="norm_mem")
    kv = _mm(memn, w_kv, name="proj_kv")
    y_x = _xa_fwd(zm, 7 * d, kv, q_norm_g, k_norm_g, d, name="xa_fwd")
    p_sb = _mm(y_sb, w_sbp, name="proj_sb")
    p_ml = _mm(y_ml, w_mlp, name="proj_ml")
    p_x = _mm(y_x, w_xp, name="proj_x")

    def merge_fn(a, b, c, g0, g1, g2, bg):
        return (_sigmoid(g0 + bg[:, :d]) * a + _sigmoid(g1 + bg[:, d:2 * d]) * b + _sigmoid(g2 + bg[:, 2 * d:]) * c)

    gate_cols = [(zm, d, 8), (zm, d, 9), (zm, d, 10)]
    (mixed,) = _rowwise(merge_fn, [p_sb, p_ml, p_x] + gate_cols, [b_gate], [(d, BF16)], name="merge")
    x1 = _mm(mixed, w_o, tiles=[x2], name="proj_out")
    (h2,) = _rowwise(_rms_fwd, [x1], [g_mlp], [(d, BF16)], name="norm_mlp")
    u, act = _mm(h2, w_f1, post=lambda r: (r, jnp.square(jnp.maximum(r, 0.0))), out_dtype=(F32, BF16), name="ff1")
    dy = _mm(act, w_f2, tiles=[x1, tgt], post=lambda r, xv, tv: (r + xv - tv) * (1.0 / d), name="ff2")
    (loss_cols,) = _rowwise(lambda g: (jnp.sum(g * g, axis=0, keepdims=True) * (0.5 * d),), [dy], [], [], [d],
                            name="loss")

    du = _mm(dy, w_f2, tb=True, tiles=[u], post=lambda r, uv: r * 2.0 * jnp.maximum(uv, 0.0), out_dtype=BF16,
             name="ff2_dx")
    dw_f2 = _mm(act, dy, ta=True, name="ff2_dw")
    dw_f1 = _mm(h2, du, ta=True, name="ff1_dw")
    dh2 = _mm(du, w_f1, tb=True, name="ff1_dx")

    def norm_bwd_fn(xv, dyv, res, g):
        dx, dg = _rms_bwd(xv, g, dyv)
        return dx + res, jnp.sum(dg, axis=0, keepdims=True)

    dx1, dg_mlp = _rowwise(norm_bwd_fn, [x1, dh2, dy], [g_mlp], [(d, F32)], [d], name="norm_mlp_bwd")
    dmixed = _mm(dx1, w_o, tb=True, name="proj_out_dx")
    dw_o = _mm(mixed, dx1, ta=True, name="proj_out_dw")

    def merge_bwd_fn(dm, a, b, c, g0, g1, g2, bg):
        outs, dgs = [], []
        for p, g, k in ((a, g0, 0), (b, g1, 1), (c, g2, 2)):
            sg = _sigmoid(g + bg[:, k * d:(k + 1) * d])
            outs.append(dm * sg)
            dgs.append(dm * p * sg * (1.0 - sg))
        dgate = jnp.concatenate(dgs, axis=1)
        return (*outs, dgate, jnp.sum(dgate, axis=0, keepdims=True))

    dp_sb, dp_ml, dp_x, dgate, db_gate = _rowwise(
        merge_bwd_fn, [dmixed, p_sb, p_ml, p_x] + gate_cols, [b_gate], [(d, BF16)] * 3 + [(3 * d, BF16)], [3 * d],
        name="merge_bwd", tr=128)
    dw_sbp = _mm(y_sb, dp_sb, ta=True, name="proj_sb_dw")
    dw_mlp = _mm(y_ml, dp_ml, ta=True, name="proj_ml_dw")
    dw_xp = _mm(y_x, dp_x, ta=True, name="proj_x_dw")
    dy_sb = _mm(dp_sb, w_sbp, tb=True, out_dtype=BF16, name="proj_sb_dx")
    dy_ml = _mm(dp_ml, w_mlp, tb=True, name="proj_ml_dx")
    dy_x = _mm(dp_x, w_xp, tb=True, out_dtype=BF16, name="proj_x_dx")

    dxq, dkn, dxv, dg_qn = _xa_bwd(zm, 7 * d, kv, q_norm_g, k_norm_g, dy_x, d, name="xa_bwd")

    def knorm_bwd_fn(kvv, dknv, dvv, g):
        dks, dgs = [], []
        for k in range(X_HEADS):
            sl = slice(k * dh, (k + 1) * dh)
            dk, dg = _rms_bwd(kvv[:, sl], g, dknv[:, sl])
            dks.append(dk)
            dgs.append(jnp.sum(dg, axis=0, keepdims=True))
        return jnp.concatenate(dks + [dvv], axis=1), dgs[0] + dgs[1] + dgs[2] + dgs[3]

    dkv, dg_kn = _rowwise(knorm_bwd_fn, [(kv, d, 0), dkn, dxv], [k_norm_g], [(2 * d, BF16)], [dh], name="xa_knorm_bwd")
    dw_kv = _mm(memn, dkv, ta=True, name="proj_kv_dw")
    dmemn = _mm(dkv, w_kv, tb=True, name="proj_kv_dx")

    def gmem_fn(mv, dv_, g):
        _, dg = _rms_bwd(mv, g, dv_)
        return (jnp.sum(dg, axis=0, keepdims=True),)

    (dg_mem,) = _rowwise(gmem_fn, [mem2, dmemn], [g_mem], [], [d], name="norm_mem_bwd")

    uncols = lambda a: a.reshape(a.shape[0], 4, a.shape[1] // 4).transpose(1, 0, 2)
    unrws = lambda a: a.reshape(4, a.shape[0] // 4, a.shape[1])
    to_parts = lambda q: q.astype(BF16).reshape(4, 2, q.shape[1] // 2, q.shape[2])
    early = [to_parts(q) for q in (uncols(dw_kv), unrws(dw_sbp), unrws(dw_mlp), unrws(dw_xp), unrws(dw_o),
                                   uncols(dw_f1), unrws(dw_f2))]
    ge_send, ge_recv, ge_src, ge_land, ge_token = _split_start(
        "grads", early, [(8,) + a.shape[2:] for a in early], dg_mem, name="exchange_early_start")

    dsq, dsk, dsv = _sb_bwd(zm, dy_sb, a_sb, ge_token, sbh, name="sb_bwd")

    def mlout_bwd_fn(dyv, hv, o, g):
        sg = _sigmoid(o)
        dn = dyv * sg
        dxs, dgs, ys = [], [], []
        for k in range(hh):
            sl = slice(k * dh, (k + 1) * dh)
            ys.append(_rms_fwd(hv[:, sl], g[:, sl]))
            dxk, dgk = _rms_bwd(hv[:, sl], g[:, sl], dn[:, sl])
            dxs.append(dxk)
            dgs.append(dgk)
        do = dyv * jnp.concatenate(ys, axis=1) * sg * (1.0 - sg)
        return jnp.concatenate(dxs, axis=1), do, jnp.sum(jnp.concatenate(dgs, axis=1), axis=0, keepdims=True)

    dhm, dmlo, dg_mln = _rowwise(mlout_bwd_fn, [dy_ml, hm, (zm, d, 6)], [ml_norm_g], [(d, F32), (d, BF16)], [d],
                                 name="ml_out_bwd")
    dmqk, dmlv, dgc, dgr = _ml_bwd(mqk, zm, 5 * d, gcol, grow, cst, nst, mst, dhm, d, name="ml_bwd")
    dmlqk, dconv_w, dconv_b = _conv_bwd(zm, 3 * d, 2 * d, conv_wf, conv_b, dmqk, name="conv_bwd")
    dgr_t = jnp.pad(dgr.transpose(1, 0, 2).reshape(8, s).T, ((0, 0), (0, LANES - 8)))

    def gate_bwd_fn(a, b, z, bias):
        tot = a + b
        rows_t = tot.shape[0]
        r = lax.broadcasted_iota(jnp.int32, (rows_t, rows_t), 0)
        c = lax.broadcasted_iota(jnp.int32, (rows_t, rows_t), 1)
        sh = CHUNK.bit_length() - 1
        same_chunk = jnp.right_shift(r, sh) == jnp.right_shift(c, sh)
        dlf = _u01dot(((c >= r) & same_chunk).astype(BF16), tot)
        lane = lax.broadcasted_iota(jnp.int32, tot.shape, 1)
        dz = jnp.where(lane < hh, tot, jnp.where(lane < 2 * hh, dlf * _sigmoid(-(z + bias)), 0.0))
        return dz, jnp.sum(dz, axis=0, keepdims=True)

    dzif, db_if_p = _rowwise(gate_bwd_fn, [dgc, dgr_t, zif], [b_if_p], [(LANES, BF16)], [LANES], name="ml_gates_bwd",
                             tr=8 * CHUNK)
    dzm = jnp.concatenate([dsq, dsk, dsv, dmlqk, dmlv, dmlo, dxq, dgate], axis=1)
    dw_main = _mm(hn, dzm, ta=True, out_dtype=BF16, name="proj_in_dw")
    dw_if = _mm(hn, dzif, ta=True, out_dtype=BF16, name="proj_if_dw")
    dw_in = jnp.concatenate([dw_main[:, :7 * d], dw_if[:, :2 * hh], dw_main[:, 7 * d:]], axis=1)
    late = [to_parts(uncols(dw_in))]
    gl_send, gl_recv, gl_src, gl_land, gl_token = _split_start(
        "grads", late, [(8,) + a.shape[2:] for a in late], dw_if, name="exchange_late_start")
    dhn = _mm(dzm, w_main, tb=True, after=gl_token, name="proj_in_dx")
    dhn = _mm(dzif, w_if, tb=True, tiles=[dhn], name="proj_if_dx")
    dx, dg_mix = _rowwise(norm_bwd_fn, [x2, dhn, dx1], [g_mix], [(d, F32)], [d], name="norm_in_bwd")

    own = lambda p: lax.dynamic_index_in_dim(lax.dynamic_index_in_dim(p, k4, 0, keepdims=False),
                                             lax.axis_index("c"), 0, keepdims=False)

    def finish(tag, send, recv, src, land, parts, after, ws, ms, vs):
        land = _split_wait("grads", send, recv, src, land, after, name=f"exchange_{tag}_wait")
        got = [lax.dynamic_update_index_in_dim(ld, own(p), me, 0) for ld, p in zip(land, parts)]
        halves = [_sum8(r, name=f"sum_grads_{tag}{i}") for i, r in enumerate(got)]
        both = _swap_halves(halves, name=f"swap_halves_{tag}")
        gs = [b.reshape(2 * b.shape[1], b.shape[2]) for b in both]
        return gs, [_adamw(w, g, m, v, name=f"adamw_{tag}{i}") for i, (w, g, m, v) in enumerate(zip(ws, gs, ms, vs))]

    first = lambda arrs: [a[0] for a in arrs]
    g_early, out_early = finish(
        "early", ge_send, ge_recv, ge_src, ge_land, early, [dx],
        first([w_mem_kv, w_sb_proj, w_ml_proj, w_x_proj, w_out, w_ff1, w_ff2]),
        first([m_w_mem_kv, m_w_sb_proj, m_w_ml_proj, m_w_x_proj, m_w_out, m_w_ff1, m_w_ff2]),
        first([v_w_mem_kv, v_w_sb_proj, v_w_ml_proj, v_w_x_proj, v_w_out, v_w_ff1, v_w_ff2]))
    g_late, out_late = finish(
        "late", gl_send, gl_recv, gl_src, gl_land, late, [o[0] for o in out_early],
        first([w_in]), first([m_w_in]), first([v_w_in]))
    g_big = [g[None] for g in g_late + g_early]
    big_out = [[o[None] for o in outs] for outs in out_late + out_early]

    small_g = [dg_mix, db_if_p[:, :2 * hh], db_gate, dconv_w, dconv_b, dg_mln, dg_mem, dg_qn, dg_kn, dg_mlp,
               jnp.sum(loss_cols).reshape(1, 1)]
    n_small = sum(a.size for a in small_g)
    rows = -(-n_small // (8 * LANES)) * 8
    g_small = _unpack(_allreduce_small(_pack(small_g, rows), out_late[0][0], name="allreduce_small"), small_g)
    loss = g_small[-1].reshape(())
    qw = conv_w.shape[2]
    g_conv_w = lax.dynamic_slice_in_dim(g_small[3], k4 * qw, qw, axis=1)
    g_small_w = [g_small[0], g_small[1], g_small[2], g_conv_w] + g_small[4:10]
    sm_w = [g_mix, b_if, b_gate, conv_w[0], conv_b, ml_norm_g, g_mem, q_norm_g, k_norm_g, g_mlp]
    sm_m = [m_g_mix, m_b_if, m_b_gate, m_conv_w[0], m_conv_b, m_ml_norm_g, m_g_mem, m_q_norm_g, m_k_norm_g, m_g_mlp]
    sm_v = [v_g_mix, v_b_if, v_b_gate, v_conv_w[0], v_conv_b, v_ml_norm_g, v_g_mem, v_q_norm_g, v_k_norm_g, v_g_mlp]
    n_sw = sum(a.size for a in sm_w)
    rows_w = -(-n_sw // (8 * LANES)) * 8
    sm_out = _adamw(_pack(sm_w, rows_w), _pack(g_small_w, rows_w), _pack(sm_m, rows_w), _pack(sm_v, rows_w),
                    name="adamw_small")
    sm_delta, sm_newm, sm_newv = [_unpack(p, sm_w) for p in sm_out]

    order = ["g_mix", "w_in", "b_if", "b_gate", "conv_w", "conv_b", "ml_norm_g", "g_mem", "w_mem_kv", "q_norm_g",
             "k_norm_g", "w_sb_proj", "w_ml_proj", "w_x_proj", "w_out", "g_mlp", "w_ff1", "w_ff2"]
    small_names = ["g_mix", "b_if", "b_gate", "conv_w", "conv_b", "ml_norm_g", "g_mem", "q_norm_g", "k_norm_g", "g_mlp"]
    big_names = ["w_in", "w_mem_kv", "w_sb_proj", "w_ml_proj", "w_x_proj", "w_out", "w_ff1", "w_ff2"]
    grads, deltas, new_m, new_v = {}, {}, {}, {}
    for i, nme in enumerate(small_names):
        shp = sm_w[i].shape if nme != "conv_w" else conv_w.shape
        grads[nme] = g_small_w[i].reshape(shp)
        deltas[nme], new_m[nme], new_v[nme] = (sm_delta[i].reshape(shp), sm_newm[i].reshape(shp),
                                               sm_newv[i].reshape(shp))
    for i, nme in enumerate(big_names):
        grads[nme] = g_big[i]
        deltas[nme], new_m[nme], new_v[nme] = big_out[i]
    return (loss, dx[None], *[grads[k] for k in order], *[deltas[k] for k in order], *[new_m[k] for k in order],
            *[new_v[k] for k in order])
```

```python
import functools

import jax
import jax.numpy as jnp
from jax import lax
from jax.experimental import pallas as pl
from jax.experimental.pallas import tpu as pltpu

F32 = jnp.float32
BF16 = jnp.bfloat16
MESH = pl.DeviceIdType.MESH

EPS = 1e-6
SB_HD = 128
SB_SLOTS = 4
ML_HEADS = 4
X_HEADS = 4
CHUNK = 64
CONV_W = 4
LANES = 128
ADAM_LR = 0.001
ADAM_B1 = 0.9
ADAM_B2 = 0.999
ADAM_EPS = 1e-08
ADAM_WD = 0.01
ADAM_STEP = 10
VMEM_CAP = 56 * 1024 * 1024
NEG = -1e30

NT = (((1,), (1,)), ((), ()))
NN = (((1,), (0,)), ((), ()))
TN = (((0,), (0,)), ((), ()))


def _dot(a, b, dn=NN):
    return lax.dot_general(a.astype(BF16), b.astype(BF16), dn, preferred_element_type=F32)


def _dot01(x, u, dn=NN):
    hi = x.astype(BF16)
    lo = (x - hi.astype(F32)).astype(BF16)
    return (lax.dot_general(hi, u, dn, preferred_element_type=F32)
            + lax.dot_general(lo, u, dn, preferred_element_type=F32))


def _u01dot(u, x):
    hi = x.astype(BF16)
    lo = (x - hi.astype(F32)).astype(BF16)
    return (lax.dot_general(u, hi, NN, preferred_element_type=F32)
            + lax.dot_general(u, lo, NN, preferred_element_type=F32))


def _pick(n, cands):
    for c in cands:
        if c <= n and n % c == 0:
            return c
    return n


def _nbytes(shape, dtype):
    n = 1
    for s in shape:
        n *= s
    return n * jnp.dtype(dtype).itemsize


def _params(vmem_bytes):
    return pltpu.CompilerParams(vmem_limit_bytes=int(min(VMEM_CAP, max(vmem_bytes, 16 * 1024 * 1024))))


def _hbm(a):
    return pltpu.with_memory_space_constraint(a, pltpu.HBM)


def _softplus(z):
    return jnp.maximum(z, 0.0) + jnp.log(1.0 + jnp.exp(-jnp.abs(z)))


def _sigmoid(z):
    return 1.0 / (1.0 + jnp.exp(-z))


def _rms_fwd(xv, g):
    r = lax.rsqrt(jnp.mean(xv * xv, axis=-1, keepdims=True) + EPS)
    return xv * r * g


def _rms_bwd(xv, g, dy):
    r = lax.rsqrt(jnp.mean(xv * xv, axis=-1, keepdims=True) + EPS)
    xh = xv * r
    dxh = dy * g
    dx = r * (dxh - xh * jnp.mean(dxh * xh, axis=-1, keepdims=True))
    return dx, dy * xh


def _mm(a, b, *, name, ta=False, tb=False, tiles=(), post=None, out_dtype=F32, bm=1024, bn=1024, bk=1024, after=None):
    m, k = (a.shape[1], a.shape[0]) if ta else a.shape
    n = b.shape[0] if tb else b.shape[1]
    tm = _pick(m, (bm, 512, 256, 128))
    tn = _pick(n, (bn, 512, 256, 128))
    tk = _pick(k, (bk, 512, 256, 128))
    nk = k // tk
    dn = (((0 if ta else 1,), (1 if tb else 0,)), ((), ()))
    dts = out_dtype if isinstance(out_dtype, tuple) else (out_dtype,)
    nt, no = len(tiles), len(dts)
    if post is None:
        post = lambda r, *ts: sum((t.astype(F32) for t in ts), r)

    def body(*refs):
        a_ref, b_ref = refs[:2]
        t_refs = refs[2:2 + nt]
        o_refs = refs[2 + nt + (after is not None):2 + nt + (after is not None) + no]
        part = lax.dot_general(a_ref[...].astype(BF16), b_ref[...].astype(BF16), dn, preferred_element_type=F32)

        def finish(r):
            res = post(r, *[t[...] for t in t_refs])
            res = res if isinstance(res, tuple) else (res,)
            for o, v in zip(o_refs, res):
                o[...] = v.astype(o.dtype)

        if nk == 1:
            finish(part)
        else:
            acc_ref = refs[-1]
            kk = pl.program_id(2)

            @pl.when(kk == 0)
            def _():
                acc_ref[...] = part

            @pl.when(kk > 0)
            def _():
                acc_ref[...] += part

            @pl.when(kk == nk - 1)
            def _():
                finish(acc_ref[...])

    a_spec = pl.BlockSpec((tk, tm), lambda i, j, q: (q, i)) if ta else pl.BlockSpec((tm, tk), lambda i, j, q: (i, q))
    b_spec = pl.BlockSpec((tn, tk), lambda i, j, q: (j, q)) if tb else pl.BlockSpec((tk, tn), lambda i, j, q: (q, j))
    o_spec = pl.BlockSpec((tm, tn), lambda i, j, q: (i, j))
    ins, specs = [_hbm(a), _hbm(b)] + [_hbm(t) for t in tiles], [a_spec, b_spec] + [o_spec] * nt
    vm = 2 * (_nbytes((tm, tk), a.dtype) + _nbytes((tk, tn), b.dtype)) + 3 * _nbytes((tm, tn), F32) \
        + _nbytes((tm, tk), BF16) + _nbytes((tk, tn), BF16) \
        + 2 * sum(_nbytes((tm, tn), t.dtype) for t in tiles) + 2 * sum(_nbytes((tm, tn), dt) for dt in dts)
    if after is not None:
        ins.append(after)
        specs.append(ANY)
    res = pl.pallas_call(
        body, name=name, grid=(m // tm, n // tn, nk), in_specs=specs, out_specs=[o_spec] * no,
        out_shape=[pltpu.HBM((m, n), dt) for dt in dts], scratch_shapes=[pltpu.VMEM((tm, tn), F32)] if nk > 1 else [],
        compiler_params=_params(vm + (4 << 20)),
    )(*ins)
    return res[0] if no == 1 else tuple(res)


def _rowwise(fn, rows, consts, outs, reds=(), *, name, tr=256, temps=6):
    rows = [r if isinstance(r, tuple) else (r, r.shape[1], 0) for r in rows]
    nrows = rows[0][0].shape[0]
    t = _pick(nrows, (tr, 128, 64, 32, 16, 8))
    nr, nc, no = len(rows), len(consts), len(outs)

    def body(*refs):
        rin, cin = refs[:nr], refs[nr:nr + nc]
        oref, rref = refs[nr + nc:nr + nc + no], refs[nr + nc + no:]
        res = fn(*[r[...] for r in rin], *[c[...] for c in cin])
        if not isinstance(res, (tuple, list)):
            res = (res,)
        for o, v in zip(oref, res[:no]):
            o[...] = v.astype(o.dtype)
        if rref:
            @pl.when(pl.program_id(0) == 0)
            def _():
                for r in rref:
                    r[...] = jnp.zeros_like(r)

            for r, v in zip(rref, res[no:]):
                r[...] += v

    in_specs = [pl.BlockSpec((t, w), functools.partial(lambda i, ci: (i, ci), ci=ci)) for (_, w, ci) in rows]
    in_specs += [pl.BlockSpec(c.shape, functools.partial(lambda i, nd: (0,) * nd, nd=c.ndim)) for c in consts]
    out_specs = [pl.BlockSpec((t, w), lambda i: (i, 0)) for (w, _) in outs]
    out_specs += [pl.BlockSpec((1, w), lambda i: (0, 0)) for w in reds]
    out_shape = [pltpu.HBM((nrows, w), dt) for (w, dt) in outs]
    out_shape += [jax.ShapeDtypeStruct((1, w), F32) for w in reds]
    widest = max([w for (_, w, _) in rows] + [w for (w, _) in outs])
    vm = 2 * sum(_nbytes((t, w), a.dtype) for (a, w, _) in rows) + 2 * sum(_nbytes((t, w), dt) for (w, dt) in outs)
    vm += temps * _nbytes((t, widest), F32) + (2 << 20)
    res = pl.pallas_call(
        body, name=name, grid=(nrows // t,), in_specs=in_specs, out_specs=out_specs, out_shape=out_shape,
        compiler_params=_params(vm),
    )(*[_hbm(a) for (a, _, _) in rows], *consts)
    return list(res)


def _sb_tiles(s, tq, tk):
    tq = _pick(s, (tq, 256, 128))
    tk = _pick(tq, (tk, 128))
    return tq, tk, tq // tk


def _sb_fwd(zm, heads, *, name, tq=512, tk=256):
    s = zm.shape[0]
    tq, tk, nd = _sb_tiles(s, tq, tk)
    scale = SB_HD ** -0.5

    def body(q_ref, k_ref, v_ref, o_ref, a_out, stage, sem):
        h, i = pl.program_id(0), pl.program_id(1)
        qb = (q_ref[...] * scale).astype(BF16)
        r = lax.broadcasted_iota(jnp.int32, (tq, tk), 0)
        c = lax.broadcasted_iota(jnp.int32, (tq, tk), 1)
        ur = lax.broadcasted_iota(jnp.int32, (tk, tk), 0)
        uc = lax.broadcasted_iota(jnp.int32, (tk, tk), 1)
        usuf = (ur > uc).astype(BF16)

        def out_copy(slot, j):
            return pltpu.make_async_copy(stage.at[slot], a_out.at[h, i, j], sem.at[slot])

        def tile(j, carry, causal, slot, reuse):
            acc, cl = carry
            if reuse is True:
                out_copy(slot, 0).wait()
            elif reuse is not None:
                @pl.when(reuse)
                def _():
                    out_copy(slot, 0).wait()
            rows = pl.ds(pl.multiple_of(j * tk, tk), tk)
            kb = k_ref[rows, :].astype(BF16)
            vb = v_ref[rows, :].astype(BF16)
            z = lax.dot_general(qb, kb, NT, preferred_element_type=F32)
            lsig = -_softplus(z)
            l = lsig if causal is None else jnp.where(causal, lsig, 0.0)
            loga = z + lsig + _dot01(l, usuf) + cl
            if causal is not None:
                loga = jnp.where(causal, loga, NEG)
            ab = jnp.exp(loga).astype(BF16)
            acc = acc + lax.dot_general(ab, vb, NN, preferred_element_type=F32)
            stage[slot] = ab
            out_copy(slot, j).start()
            return acc, cl + jnp.sum(l, axis=1, keepdims=True)

        carry = (jnp.zeros((tq, SB_HD), F32), jnp.zeros((tq, 1), F32))
        for n, dd in enumerate(range(nd - 1, -1, -1)):
            carry = tile(i * nd + dd, carry, c + dd * tk < r, n, None)

        def rest(n, cr):
            return tile(i * nd - 1 - n, cr, None, (nd + n) % SB_SLOTS, nd + n >= SB_SLOTS)

        acc, _ = lax.fori_loop(0, i * nd, rest, carry)
        total = (i + 1) * nd
        for back in range(1, SB_SLOTS + 1):
            @pl.when(total >= back)
            def _():
                out_copy((total - back) % SB_SLOTS, 0).wait()

        o_ref[...] = acc.astype(o_ref.dtype)

    assert nd <= SB_SLOTS
    blk = lambda off: pl.BlockSpec((s, SB_HD), functools.partial(lambda h, i, off: (0, off + h), off=off))
    return pl.pallas_call(
        body, name=name, grid=(heads, s // tq),
        in_specs=[pl.BlockSpec((tq, SB_HD), lambda h, i: (i, h)), blk(heads), blk(2 * heads)],
        out_specs=[pl.BlockSpec((tq, SB_HD), lambda h, i: (i, h)), ANY],
        out_shape=[pltpu.HBM((s, heads * SB_HD), BF16), pltpu.HBM((heads, s // tq, s // tk, tq, tk), BF16)],
        scratch_shapes=[pltpu.VMEM((SB_SLOTS, tq, tk), BF16), pltpu.SemaphoreType.DMA((SB_SLOTS,))],
        compiler_params=_params(8 * s * SB_HD * 4 + 24 * tq * tk * 4 + (8 << 20)),
    )(_hbm(zm), _hbm(zm), _hbm(zm))


def _sb_bwd(zm, dy, a_all, after, heads, *, name, tq=512, tk=256):
    s = zm.shape[0]
    tq, tk, nd = _sb_tiles(s, tq, tk)
    nq = s // tq
    scale = SB_HD ** -0.5

    def body(q_ref, k_ref, v_ref, do_ref, a_in, after_ref, dq_ref, dk_ref, dv_ref, dka, dva, abuf, sem):
        h, i = pl.program_id(0), pl.program_id(1)

        @pl.when(i == 0)
        def _():
            dka[...] = jnp.zeros_like(dka)
            dva[...] = jnp.zeros_like(dva)

        qb = (q_ref[...] * scale).astype(BF16)
        dob = do_ref[...].astype(BF16)
        r = lax.broadcasted_iota(jnp.int32, (tq, tk), 0)
        c = lax.broadcasted_iota(jnp.int32, (tq, tk), 1)
        ur = lax.broadcasted_iota(jnp.int32, (tk, tk), 0)
        uc = lax.broadcasted_iota(jnp.int32, (tk, tk), 1)
        uexcl = (ur < uc).astype(BF16)

        def fetch(j, slot):
            return pltpu.make_async_copy(a_in.at[h, i, j], abuf.at[slot], sem.at[slot])

        total = (i + 1) * nd
        ahead = SB_SLOTS - 1

        def tile(j, carry, causal):
            dq, cg = carry
            slot = j % SB_SLOTS
            fetch(j, slot).wait()

            @pl.when(j + ahead < total)
            def _():
                fetch(j + ahead, (j + ahead) % SB_SLOTS).start()

            rows = pl.ds(pl.multiple_of(j * tk, tk), tk)
            kb = k_ref[rows, :].astype(BF16)
            vb = v_ref[rows, :].astype(BF16)
            z = lax.dot_general(qb, kb, NT, preferred_element_type=F32)
            sig = 1.0 / (1.0 + jnp.exp(-z))
            ab = abuf[slot]
            g = ab.astype(F32) * lax.dot_general(dob, vb, NT, preferred_element_type=F32)
            p = cg + lax.dot_general(g.astype(BF16), uexcl, NN, preferred_element_type=F32)
            dz = g - sig * (g + p)
            if causal is not None:
                dz = jnp.where(causal, dz, 0.0)
            dzb = dz.astype(BF16)
            dva[rows, :] += lax.dot_general(ab, dob, TN, preferred_element_type=F32)
            dka[rows, :] += lax.dot_general(dzb, qb, TN, preferred_element_type=F32)
            dq = dq + lax.dot_general(dzb, kb, NN, preferred_element_type=F32)
            return dq, cg + jnp.sum(g, axis=1, keepdims=True)

        for first in range(ahead):
            @pl.when(first < total)
            def _():
                fetch(first, first).start()

        init = (jnp.zeros((tq, SB_HD), F32), jnp.zeros((tq, 1), F32))
        carry = lax.fori_loop(0, i * nd, lambda j, cr: tile(j, cr, None), init)
        for dd in range(nd):
            carry = tile(i * nd + dd, carry, c + dd * tk < r)
        dq_ref[...] = (carry[0] * scale).astype(dq_ref.dtype)

        @pl.when(i == nq - 1)
        def _():
            dk_ref[...] = dka[...].astype(dk_ref.dtype)
            dv_ref[...] = dva[...].astype(dv_ref.dtype)

    blk = lambda off: pl.BlockSpec((s, SB_HD), functools.partial(lambda h, i, off: (0, off + h), off=off))
    tile_spec = pl.BlockSpec((tq, SB_HD), lambda h, i: (i, h))
    full = pltpu.HBM((s, heads * SB_HD), BF16)
    return pl.pallas_call(
        body, name=name, grid=(heads, nq),
        in_specs=[tile_spec, blk(heads), blk(2 * heads), tile_spec, ANY, ANY],
        out_specs=[tile_spec, blk(0), blk(0)],
        out_shape=[full, full, full],
        scratch_shapes=[pltpu.VMEM((s, SB_HD), F32), pltpu.VMEM((s, SB_HD), F32),
                        pltpu.VMEM((SB_SLOTS, tq, tk), BF16), pltpu.SemaphoreType.DMA((SB_SLOTS,))],
        compiler_params=_params(12 * s * SB_HD * 4 + 32 * tq * tk * 4 + (8 << 20)),
    )(_hbm(zm), _hbm(zm), _hbm(zm), _hbm(dy), a_all, after)


def _conv_taps(u, w_ref, rows_i):
    taps = []
    for j in range(CONV_W):
        sh = CONV_W - 1 - j
        if sh == 0:
            taps.append(u)
        else:
            taps.append(jnp.where(rows_i >= sh, pltpu.roll(u, sh, 0), 0.0))
    return taps


def _conv_fwd(zm, col0, width, cw, cb, *, name):
    s = zm.shape[0]
    bw = _pick(width, (LANES,))
    off = col0 // bw

    def body(u_ref, w_ref, b_ref, o_ref):
        u = u_ref[...]
        rows_i = lax.broadcasted_iota(jnp.int32, u.shape, 0)
        acc = jnp.broadcast_to(b_ref[...], u.shape)
        for j, tp in enumerate(_conv_taps(u, w_ref, rows_i)):
            acc = acc + tp * w_ref[j:j + 1, :]
        o_ref[...] = acc * _sigmoid(acc)

    return pl.pallas_call(
        body, name=name, grid=(width // bw,),
        in_specs=[pl.BlockSpec((s, bw), lambda j: (0, off + j)), pl.BlockSpec((CONV_W, bw), lambda j: (0, j)),
                  pl.BlockSpec((1, bw), lambda j: (0, j))],
        out_specs=pl.BlockSpec((s, bw), lambda j: (0, j)),
        out_shape=pltpu.HBM((s, width), F32),
        compiler_params=_params(12 * s * bw * 4 + (4 << 20)),
    )(_hbm(zm), cw, cb)


def _conv_bwd(zm, col0, width, cw, cb, dqk, *, name):
    s = zm.shape[0]
    bw = _pick(width, (LANES,))
    off = col0 // bw

    def body(u_ref, w_ref, b_ref, d_ref, du_ref, dw_ref, db_ref):
        u = u_ref[...]
        rows_i = lax.broadcasted_iota(jnp.int32, u.shape, 0)
        taps = _conv_taps(u, w_ref, rows_i)
        acc = jnp.broadcast_to(b_ref[...], u.shape)
        for j, tp in enumerate(taps):
            acc = acc + tp * w_ref[j:j + 1, :]
        sg = _sigmoid(acc)
        dc = d_ref[...] * (sg * (1.0 + acc * (1.0 - sg)))
        du = jnp.zeros_like(u)
        for j in range(CONV_W):
            sh = CONV_W - 1 - j
            if sh == 0:
                du = du + dc * w_ref[j:j + 1, :]
            else:
                du = du + jnp.where(rows_i < s - sh, pltpu.roll(dc, s - sh, 0), 0.0) * w_ref[j:j + 1, :]
            dw_ref[j:j + 1, :] = jnp.sum(dc * taps[j], axis=0, keepdims=True)
        du_ref[...] = du.astype(du_ref.dtype)
        db_ref[...] = jnp.sum(dc, axis=0, keepdims=True)

    return pl.pallas_call(
        body, name=name, grid=(width // bw,),
        in_specs=[pl.BlockSpec((s, bw), lambda j: (0, off + j)), pl.BlockSpec((CONV_W, bw), lambda j: (0, j)),
                  pl.BlockSpec((1, bw), lambda j: (0, j)), pl.BlockSpec((s, bw), lambda j: (0, j))],
        out_specs=[pl.BlockSpec((s, bw), lambda j: (0, j)), pl.BlockSpec((CONV_W, bw), lambda j: (0, j)),
                   pl.BlockSpec((1, bw), lambda j: (0, j))],
        out_shape=[pltpu.HBM((s, width), BF16), pltpu.HBM((CONV_W, width), F32),
                   pltpu.HBM((1, width), F32)],
        compiler_params=_params(20 * s * bw * 4 + (4 << 20)),
    )(_hbm(zm), cw, cb, _hbm(dqk))


def _ml_gates(gcol_ref, grow_ref):
    l = CHUNK
    r = lax.broadcasted_iota(jnp.int32, (l, l), 0)
    c = lax.broadcasted_iota(jnp.int32, (l, l), 1)
    gcol = gcol_ref[...]
    grow = grow_ref[0]
    bcol = _u01dot((c <= r).astype(BF16), gcol)
    brow = _dot01(grow, (r <= c).astype(BF16))
    return gcol, grow, bcol, brow, r >= c


def _ml_chunk(h, dh, mq_ref, mk_ref, v_ref, gates, cp, n_prev, m_prev):
    gcol, grow, bcol, brow, tri = gates
    l = CHUNK
    sl = slice(h * dh, (h + 1) * dh)
    qc = mq_ref[:, sl]
    kc = mk_ref[:, sl] * (dh ** -0.5)
    vc = v_ref[:, sl]
    i_row = grow[h:h + 1, :]
    i_col = gcol[:, h:h + 1]
    b_col = bcol[:, ML_HEADS + h:ML_HEADS + h + 1]
    b_row = brow[ML_HEADS + h:ML_HEADS + h + 1, :]
    b_end = b_col[l - 1:l, :]
    d = jnp.where(tri, b_col - b_row + i_row, -jnp.inf)
    m_inter = b_col + m_prev
    m_t = jnp.maximum(m_inter, jnp.max(d, axis=1, keepdims=True))
    w = jnp.exp(d - m_t)
    s_inter = jnp.exp(m_inter - m_t)
    qb, kb, vb = qc.astype(BF16), kc.astype(BF16), vc.astype(BF16)
    cpb = cp.astype(BF16)
    a = lax.dot_general(qb, kb, NT, preferred_element_type=F32)
    sc = a * w
    qcp = lax.dot_general(qb, cpb, NT, preferred_element_type=F32)
    qn = jnp.sum(qc * n_prev, axis=1, keepdims=True)
    num = lax.dot_general(sc.astype(BF16), vb, NN, preferred_element_type=F32) + s_inter * qcp
    den = jnp.sum(sc, axis=1, keepdims=True) + s_inter * qn
    floor = jnp.exp(-m_t)
    dnm = jnp.maximum(jnp.abs(den), floor)
    g_col = b_end - b_col + i_col
    g_row = b_end - b_row + i_row
    m_new = jnp.maximum(b_end + m_prev, jnp.max(g_row, axis=1, keepdims=True))
    decay = jnp.exp(b_end + m_prev - m_new)
    wk = jnp.exp(g_col - m_new)
    return dict(qc=qc, kc=kc, vc=vc, qb=qb, kb=kb, vb=vb, cpb=cpb, w=w, s_inter=s_inter, a=a, sc=sc, qcp=qcp, qn=qn,
                num=num, den=den, floor=floor, dnm=dnm, m_new=m_new, decay=decay, wk=wk, sl=sl)


def _ml_fwd(mqk, zm, vcol, gcol, grow, d_model, *, name):
    s = zm.shape[0]
    nc = s // CHUNK
    dh = d_model // ML_HEADS
    hh = ML_HEADS

    def body(mq_ref, mk_ref, v_ref, gcol_ref, grow_ref, h_ref, cs_ref, ns_ref, ms_ref, c_s, n_s, m_s):
        @pl.when(pl.program_id(0) == 0)
        def _():
            c_s[...] = jnp.zeros_like(c_s)
            n_s[...] = jnp.zeros_like(n_s)
            m_s[...] = jnp.zeros_like(m_s)

        gates = _ml_gates(gcol_ref, grow_ref)
        for h in range(hh):
            cp, n_prev, m_prev = c_s[h], n_s[h], m_s[h][:, 0:1]
            cs_ref[0, h] = cp
            ns_ref[0, h] = n_prev
            ms_ref[0, h] = m_s[h]
            f = _ml_chunk(h, dh, mq_ref, mk_ref, v_ref, gates, cp, n_prev, m_prev)
            h_ref[:, f["sl"]] = f["num"] / f["dnm"]
            c_s[h] = f["decay"] * cp + lax.dot_general((f["vc"] * f["wk"]).astype(BF16), f["kb"], TN,
                                                       preferred_element_type=F32)
            n_s[h] = f["decay"] * n_prev + jnp.sum(f["wk"] * f["kc"], axis=0, keepdims=True)
            m_s[h] = jnp.broadcast_to(f["m_new"], (1, LANES))

    dblk = d_model
    return pl.pallas_call(
        body, name=name, grid=(nc,),
        in_specs=[pl.BlockSpec((CHUNK, dblk), lambda c: (c, 0)), pl.BlockSpec((CHUNK, dblk), lambda c: (c, 1)),
                  pl.BlockSpec((CHUNK, dblk), lambda c: (c, vcol // dblk)),
                  pl.BlockSpec((CHUNK, LANES), lambda c: (c, 0)), pl.BlockSpec((1, 8, CHUNK), lambda c: (c, 0, 0))],
        out_specs=[pl.BlockSpec((CHUNK, dblk), lambda c: (c, 0)),
                   pl.BlockSpec((1, hh, dh, dh), lambda c: (c, 0, 0, 0)),
                   pl.BlockSpec((1, hh, 1, dh), lambda c: (c, 0, 0, 0)),
                   pl.BlockSpec((1, hh, 1, LANES), lambda c: (c, 0, 0, 0))],
        out_shape=[pltpu.HBM((s, d_model), F32), pltpu.HBM((nc, hh, dh, dh), F32),
                   pltpu.HBM((nc, hh, 1, dh), F32), pltpu.HBM((nc, hh, 1, LANES), F32)],
        scratch_shapes=[pltpu.VMEM((hh, dh, dh), F32), pltpu.VMEM((hh, 1, dh), F32), pltpu.VMEM((hh, 1, LANES), F32)],
        compiler_params=_params(8 * hh * dh * dh * 4 + (16 << 20)),
    )(_hbm(mqk), _hbm(mqk), _hbm(zm), _hbm(gcol), _hbm(grow))


def _ml_bwd(mqk, zm, vcol, gcol, grow, cs, ns, ms, dhm, d_model, *, name):
    s = zm.shape[0]
    nc = s // CHUNK
    dh = d_model // ML_HEADS
    hh = ML_HEADS
    l = CHUNK

    def body(mq_ref, mk_ref, v_ref, gcol_ref, grow_ref, cs_ref, ns_ref, ms_ref, dh_ref,
             dqk_ref, dv_ref, dgc_ref, dgr_ref, dc_s, dn_s):
        @pl.when(pl.program_id(0) == 0)
        def _():
            dc_s[...] = jnp.zeros_like(dc_s)
            dn_s[...] = jnp.zeros_like(dn_s)

        gates = _ml_gates(gcol_ref, grow_ref)
        lane = lax.broadcasted_iota(jnp.int32, (l, LANES), 1)
        rowi = lax.broadcasted_iota(jnp.int32, (8, l), 0)
        lastrow = lax.broadcasted_iota(jnp.int32, (l, 1), 0) == l - 1
        dgc = jnp.zeros((l, LANES), F32)
        dgr = jnp.zeros((8, l), F32)
        for h in range(hh):
            cp, n_prev, m_prev = cs_ref[0, h], ns_ref[0, h], ms_ref[0, h][:, 0:1]
            f = _ml_chunk(h, dh, mq_ref, mk_ref, v_ref, gates, cp, n_prev, m_prev)
            dC, dn = dc_s[h], dn_s[h]
            dhv = dh_ref[:, f["sl"]]
            dnum = dhv / f["dnm"]
            hv = f["num"] / f["dnm"]
            ddnm = -jnp.sum(dhv * hv, axis=1, keepdims=True) / f["dnm"]
            dden = jnp.where(jnp.abs(f["den"]) >= f["floor"], ddnm * jnp.sign(f["den"]), 0.0)
            dnb = dnum.astype(BF16)
            dsc = lax.dot_general(dnb, f["vb"], NT, preferred_element_type=F32) + dden
            dvc = lax.dot_general(f["sc"].astype(BF16), dnb, TN, preferred_element_type=F32)
            ds_inter = jnp.sum(dnum * f["qcp"], axis=1, keepdims=True) + dden * f["qn"]
            sdn = (f["s_inter"] * dnum).astype(BF16)
            sdd = f["s_inter"] * dden
            da = dsc * f["w"]
            dab = da.astype(BF16)
            dqc = (lax.dot_general(dab, f["kb"], NN, preferred_element_type=F32)
                   + lax.dot_general(sdn, f["cpb"], NN, preferred_element_type=F32) + sdd * n_prev)
            dcp = f["decay"] * dC + lax.dot_general(sdn, f["qb"], TN, preferred_element_type=F32)
            dnp = f["decay"] * dn + jnp.sum(sdd * f["qc"], axis=0, keepdims=True)
            vw = (f["vc"] * f["wk"]).astype(BF16)
            dCb = dC.astype(BF16)
            dkc = (lax.dot_general(dab, f["qb"], TN, preferred_element_type=F32)
                   + lax.dot_general(vw, dCb, NN, preferred_element_type=F32) + f["wk"] * dn)
            e = lax.dot_general(f["kb"], dCb, NT, preferred_element_type=F32)
            dvc = dvc + e * f["wk"]
            dwk = jnp.sum(e * f["vc"], axis=1, keepdims=True) + jnp.sum(f["kc"] * dn, axis=1, keepdims=True)
            ddecay = jnp.sum(jnp.sum(dC * cp, axis=1, keepdims=True), axis=0, keepdims=True) \
                + jnp.sum(dn * n_prev, axis=1, keepdims=True)
            dd = dsc * f["sc"]
            dlw = dwk * f["wk"]
            db_end = jnp.sum(dlw, axis=0, keepdims=True) + ddecay * f["decay"]
            di_col = dlw
            db_col = jnp.sum(dd, axis=1, keepdims=True) + ds_inter * f["s_inter"] - dlw \
                + jnp.where(lastrow, db_end, 0.0)
            cs_dd = jnp.sum(dd, axis=0, keepdims=True)
            dgc = dgc + jnp.where(lane == h, di_col, 0.0) + jnp.where(lane == hh + h, db_col, 0.0)
            dgr = dgr + jnp.where(rowi == h, cs_dd, 0.0) - jnp.where(rowi == hh + h, cs_dd, 0.0)
            dqk_ref[:, f["sl"]] = dqc
            dqk_ref[:, d_model + h * dh:d_model + (h + 1) * dh] = dkc * (dh ** -0.5)
            dv_ref[:, f["sl"]] = dvc.astype(dv_ref.dtype)
            dc_s[h] = dcp
            dn_s[h] = dnp
        dgc_ref[...] = dgc
        dgr_ref[0] = dgr

    dblk = d_model
    rev = lambda c: nc - 1 - c
    return pl.pallas_call(
        body, name=name, grid=(nc,),
        in_specs=[pl.BlockSpec((l, dblk), lambda c: (rev(c), 0)), pl.BlockSpec((l, dblk), lambda c: (rev(c), 1)),
                  pl.BlockSpec((l, dblk), lambda c: (rev(c), vcol // dblk)),
                  pl.BlockSpec((l, LANES), lambda c: (rev(c), 0)), pl.BlockSpec((1, 8, l), lambda c: (rev(c), 0, 0)),
                  pl.BlockSpec((1, hh, dh, dh), lambda c: (rev(c), 0, 0, 0)),
                  pl.BlockSpec((1, hh, 1, dh), lambda c: (rev(c), 0, 0, 0)),
                  pl.BlockSpec((1, hh, 1, LANES), lambda c: (rev(c), 0, 0, 0)),
                  pl.BlockSpec((l, dblk), lambda c: (rev(c), 0))],
        out_specs=[pl.BlockSpec((l, 2 * dblk), lambda c: (rev(c), 0)),
                   pl.BlockSpec((l, dblk), lambda c: (rev(c), 0)), pl.BlockSpec((l, LANES), lambda c: (rev(c), 0)),
                   pl.BlockSpec((1, 8, l), lambda c: (rev(c), 0, 0))],
        out_shape=[pltpu.HBM((s, 2 * d_model), F32),
                   pltpu.HBM((s, d_model), BF16), pltpu.HBM((s, LANES), F32),
                   pltpu.HBM((nc, 8, l), F32)],
        scratch_shapes=[pltpu.VMEM((hh, dh, dh), F32), pltpu.VMEM((hh, 1, dh), F32)],
        compiler_params=_params(10 * hh * dh * dh * 4 + (16 << 20)),
    )(*[_hbm(a) for a in (mqk, mqk, zm, gcol, grow, cs, ns, ms, dhm)])


def _xa_fwd(zm, qcol, kv, gq, gk, d_model, *, name, tq=256):
    s = zm.shape[0]
    nm = kv.shape[0]
    dh = d_model // X_HEADS
    tq = _pick(s, (tq, 128, 64))
    scale = dh ** -0.5

    def body(q_ref, k_ref, v_ref, gq_ref, gk_ref, o_ref):
        qn = _rms_fwd(q_ref[...], gq_ref[...])
        kn = _rms_fwd(k_ref[...], gk_ref[...])
        lg = _dot(qn, kn, NT) * scale
        lg = lg - jnp.max(lg, axis=1, keepdims=True)
        p = jnp.exp(lg)
        p = p / jnp.sum(p, axis=1, keepdims=True)
        o_ref[...] = _dot(p, v_ref[...], NN).astype(o_ref.dtype)

    return pl.pallas_call(
        body, name=name, grid=(X_HEADS, s // tq),
        in_specs=[pl.BlockSpec((tq, dh), lambda h, i: (i, qcol // dh + h)), pl.BlockSpec((nm, dh), lambda h, i: (0, h)),
                  pl.BlockSpec((nm, dh), lambda h, i: (0, X_HEADS + h)),
                  pl.BlockSpec((1, dh), lambda h, i: (0, 0)), pl.BlockSpec((1, dh), lambda h, i: (0, 0))],
        out_specs=pl.BlockSpec((tq, dh), lambda h, i: (i, h)),
        out_shape=pltpu.HBM((s, d_model), BF16),
        compiler_params=_params(32 << 20),
    )(_hbm(zm), _hbm(kv), _hbm(kv), gq, gk)


def _xa_bwd(zm, qcol, kv, gq, gk, dy, d_model, *, name, tq=256):
    s = zm.shape[0]
    nm = kv.shape[0]
    dh = d_model // X_HEADS
    tq = _pick(s, (tq, 128, 64))
    nq = s // tq
    scale = dh ** -0.5

    def body(q_ref, k_ref, v_ref, gq_ref, gk_ref, do_ref, dq_ref, dkn_ref, dv_ref, dgq_ref):
        h, i = pl.program_id(0), pl.program_id(1)

        @pl.when(i == 0)
        def _():
            dkn_ref[...] = jnp.zeros_like(dkn_ref)
            dv_ref[...] = jnp.zeros_like(dv_ref)

        @pl.when((i == 0) & (h == 0))
        def _():
            dgq_ref[...] = jnp.zeros_like(dgq_ref)

        q = q_ref[...]
        qn = _rms_fwd(q, gq_ref[...])
        kn = _rms_fwd(k_ref[...], gk_ref[...])
        lg = _dot(qn, kn, NT) * scale
        lg = lg - jnp.max(lg, axis=1, keepdims=True)
        p = jnp.exp(lg)
        p = p / jnp.sum(p, axis=1, keepdims=True)
        do = do_ref[...]
        dv_ref[...] += _dot(p, do, TN)
        dp = _dot(do, v_ref[...], NT)
        dlg = p * (dp - jnp.sum(dp * p, axis=1, keepdims=True)) * scale
        dqn = _dot(dlg, kn, NN)
        dkn_ref[...] += _dot(dlg, qn, TN)
        dq, dgq = _rms_bwd(q, gq_ref[...], dqn)
        dq_ref[...] = dq.astype(dq_ref.dtype)
        dgq_ref[...] += jnp.sum(dgq, axis=0, keepdims=True)

    return pl.pallas_call(
        body, name=name, grid=(X_HEADS, nq),
        in_specs=[pl.BlockSpec((tq, dh), lambda h, i: (i, qcol // dh + h)), pl.BlockSpec((nm, dh), lambda h, i: (0, h)),
                  pl.BlockSpec((nm, dh), lambda h, i: (0, X_HEADS + h)),
                  pl.BlockSpec((1, dh), lambda h, i: (0, 0)), pl.BlockSpec((1, dh), lambda h, i: (0, 0)),
                  pl.BlockSpec((tq, dh), lambda h, i: (i, h))],
        out_specs=[pl.BlockSpec((tq, dh), lambda h, i: (i, h)), pl.BlockSpec((nm, dh), lambda h, i: (0, h)),
                   pl.BlockSpec((nm, dh), lambda h, i: (0, h)), pl.BlockSpec((1, dh), lambda h, i: (0, 0))],
        out_shape=[pltpu.HBM((s, d_model), BF16), pltpu.HBM((nm, d_model), F32),
                   pltpu.HBM((nm, d_model), F32), pltpu.HBM((1, dh), F32)],
        compiler_params=_params(32 << 20),
    )(_hbm(zm), _hbm(kv), _hbm(kv), gq, gk, _hbm(dy))


def _place():
    return lax.axis_index("x"), lax.axis_index("y"), lax.axis_index("c")


ANY = pl.BlockSpec(memory_space=pl.ANY)


def _allgather_two_level(big, small, *, name, chunk_rows=64):
    r = big.shape[0]
    half = r // 2
    nr = _pick(half, (chunk_rows, 32, 16))
    nq = half // nr

    def body(big_ref, small_ref, obig, osmall, send, recv, fsend, frecv, ssend, srecv, loc):
        x, y, c = _place()
        k = 2 * x + y
        chips = [(1 - x, y), (x, 1 - y), (1 - x, 1 - y)]
        own = [pltpu.make_async_copy(big_ref, obig.at[k], loc.at[0]),
               pltpu.make_async_copy(small_ref, osmall.at[k], loc.at[1])]
        for cp in own:
            cp.start()

        def rows(h, q):
            return pl.ds(pl.multiple_of(h * half + q * nr, nr), nr)

        def over_ici(j, q, slot, h):
            return pltpu.make_async_remote_copy(
                src_ref=big_ref.at[rows(h, q)], dst_ref=obig.at[slot, rows(h, q)], send_sem=send.at[nq * j + q],
                recv_sem=recv.at[nq * j + q], device_id=(chips[j][0], chips[j][1], c), device_id_type=MESH)

        def to_sibling(j, q, h):
            slot = 2 * chips[j][0] + chips[j][1]
            return pltpu.make_async_remote_copy(
                src_ref=obig.at[slot, rows(h, q)], dst_ref=obig.at[slot, rows(h, q)], send_sem=fsend.at[nq * j + q],
                recv_sem=frecv.at[nq * j + q], device_id=(x, y, 1 - c), device_id_type=MESH)

        def small_copy(j, slot):
            return pltpu.make_async_remote_copy(
                src_ref=small_ref, dst_ref=osmall.at[slot], send_sem=ssend.at[j], recv_sem=srecv.at[j],
                device_id=(chips[j][0], chips[j][1], c), device_id_type=MESH)

        for q in range(nq):
            for j in range(3):
                over_ici(j, q, k, c).start()
        for j in range(3):
            small_copy(j, k).start()
        for q in range(nq):
            for j in range(3):
                over_ici(j, q, 2 * chips[j][0] + chips[j][1], c).wait_recv()
                to_sibling(j, q, c).start()
        for q in range(nq):
            for j in range(3):
                to_sibling(j, q, 1 - c).wait_recv()
        for j in range(3):
            small_copy(j, 2 * chips[j][0] + chips[j][1]).wait_recv()
            small_copy(j, k).wait_send()
        for q in range(nq):
            for j in range(3):
                over_ici(j, q, k, c).wait_send()
                to_sibling(j, q, c).wait_send()
        for cp in own:
            cp.wait()

    return pl.pallas_call(
        body, name=name, in_specs=[ANY] * 2, out_specs=[ANY] * 2,
        out_shape=[pltpu.HBM((4,) + big.shape, big.dtype), pltpu.HBM((4,) + small.shape, small.dtype)],
        scratch_shapes=[pltpu.SemaphoreType.DMA((3 * nq,))] * 4
        + [pltpu.SemaphoreType.DMA((3,)), pltpu.SemaphoreType.DMA((3,)), pltpu.SemaphoreType.DMA((2,))],
    )(big, small)


HBM_SPEC = pl.BlockSpec(memory_space=pltpu.HBM)
SEM_SPEC = pl.BlockSpec(memory_space=pltpu.SEMAPHORE)
EFFECT = pltpu.SideEffectType.DATAFLOW_SIDE_EFFECTING


def _split_copies(kind, srcs, lands, send, recv):
    x, y, c = _place()
    if kind == "quarters":
        peers = [(1 - x, y, c), (x, 1 - y, c), (1 - x, 1 - y, c)]
    else:
        peers = [(x ^ ((j >> 2) & 1), y ^ ((j >> 1) & 1), c ^ (j & 1)) for j in range(1, 8)]
    npeer = len(peers)
    out = []
    for t in range(len(srcs)):
        for j, (px, py, pc) in enumerate(peers):
            if kind == "quarters":
                src, mine, theirs = srcs[t], 2 * x + y, 2 * px + py
            else:
                src, mine, theirs = srcs[t].at[2 * px + py, pc], 4 * x + 2 * y + c, 4 * px + 2 * py + pc
            mk = functools.partial(
                pltpu.make_async_remote_copy, src_ref=src, send_sem=send.at[npeer * t + j],
                recv_sem=recv.at[npeer * t + j], device_id=(px, py, pc), device_id_type=MESH)
            out.append((functools.partial(mk, dst_ref=lands[t].at[mine]),
                        functools.partial(mk, dst_ref=lands[t].at[theirs])))
    return out


def _split_start(kind, srcs, land_shapes, after, *, name):
    n = len(srcs)
    ncopies = n * (3 if kind == "quarters" else 7)

    def body(*refs):
        ins, lands = refs[:n], refs[n:2 * n]
        send, recv = refs[2 * n + 1], refs[2 * n + 2]
        token = refs[-1]
        for start, _ in _split_copies(kind, ins, lands, send, recv):
            start().start()
        token[...] = jnp.zeros_like(token)

    lands = [_hbm(lax.empty(shp, a.dtype)) for shp, a in zip(land_shapes, srcs)]
    res = pl.pallas_call(
        body, name=name, in_specs=[HBM_SPEC] * (2 * n) + [ANY],
        out_specs=[SEM_SPEC, SEM_SPEC] + [HBM_SPEC] * (2 * n) + [pl.BlockSpec(memory_space=pltpu.VMEM)],
        out_shape=[pltpu.SemaphoreType.DMA((ncopies,)), pltpu.SemaphoreType.DMA((ncopies,))]
        + [pltpu.HBM(a.shape, a.dtype) for a in srcs] + [pltpu.HBM(shp, a.dtype) for shp, a in zip(land_shapes, srcs)]
        + [jax.ShapeDtypeStruct((8, LANES), F32)],
        input_output_aliases={i: 2 + i for i in range(2 * n)},
        compiler_params=pltpu.CompilerParams(has_side_effects=EFFECT),
    )(*[_hbm(a) for a in srcs], *lands, after)
    return res[0], res[1], list(res[2:2 + n]), list(res[2 + n:2 + 2 * n]), res[-1]


def _split_wait(kind, send, recv, srcs, lands, after, *, name):
    n = len(srcs)

    def body(*refs):
        ins, lnd = refs[:n], refs[n:2 * n]
        snd, rcv = refs[2 * n], refs[2 * n + 1]
        for start, arrive in _split_copies(kind, ins, lnd, snd, rcv):
            start().wait_send()
            arrive().wait_recv()

    res = pl.pallas_call(
        body, name=name, in_specs=[HBM_SPEC] * (2 * n) + [SEM_SPEC, SEM_SPEC] + [ANY] * len(after),
        out_specs=[HBM_SPEC] * (2 * n),
        out_shape=[pltpu.HBM(a.shape, a.dtype) for a in srcs] + [pltpu.HBM(a.shape, a.dtype) for a in lands],
        input_output_aliases={i: i for i in range(2 * n)},
        compiler_params=pltpu.CompilerParams(has_side_effects=EFFECT),
    )(*srcs, *lands, send, recv, *after)
    return list(res[n:])


def _sum8(parts, *, name):
    _, r, c = parts.shape
    t = _pick(r, (128, 64, 32, 16, 8))

    def body(p_ref, o_ref):
        acc = p_ref[0].astype(F32)
        for k in range(1, 8):
            acc = acc + p_ref[k].astype(F32)
        o_ref[...] = acc

    return pl.pallas_call(
        body, name=name, grid=(r // t,), in_specs=[pl.BlockSpec((8, t, c), lambda i: (0, i, 0))],
        out_specs=pl.BlockSpec((t, c), lambda i: (i, 0)), out_shape=pltpu.HBM((r, c), F32),
        compiler_params=_params(2 * 8 * t * c * 2 + 6 * t * c * 4 + (4 << 20)),
    )(_hbm(parts))


def _swap_halves(halves, *, name, chunk_bytes=512 * 1024):
    n = len(halves)
    items = []
    for t, a in enumerate(halves):
        r = a.shape[0]
        k = 1
        while _nbytes(a.shape, a.dtype) // k > chunk_bytes and r % (2 * k) == 0 and (r // (2 * k)) % 8 == 0:
            k *= 2
        items += [(t, q * (r // k), r // k) for q in range(k)]
    m = len(items)

    def body(*refs):
        ins, outs = refs[:n], refs[n:2 * n]
        sbuf, rbuf = refs[2 * n:3 * n], refs[3 * n:4 * n]
        send, recv, loc_own, loc_in, loc_out = refs[4 * n:]
        x, y, c = _place()
        local, stage = [], []
        for t in range(n):
            cp = pltpu.make_async_copy(ins[t], outs[t].at[c], loc_own.at[t])
            cp.start()
            local.append(cp)
        for q, (t, r0, nr) in enumerate(items):
            cp = pltpu.make_async_copy(ins[t].at[pl.ds(r0, nr)], sbuf[t].at[pl.ds(r0, nr)], loc_in.at[q])
            cp.start()
            stage.append(cp)

        def copy(q):
            t, r0, nr = items[q]
            return pltpu.make_async_remote_copy(
                src_ref=sbuf[t].at[pl.ds(r0, nr)], dst_ref=rbuf[t].at[pl.ds(r0, nr)], send_sem=send.at[q],
                recv_sem=recv.at[q], device_id=(x, y, 1 - c), device_id_type=MESH)

        for q in range(m):
            stage[q].wait()
            copy(q).start()
        for q, (t, r0, nr) in enumerate(items):
            copy(q).wait_recv()
            cp = pltpu.make_async_copy(rbuf[t].at[pl.ds(r0, nr)], outs[t].at[1 - c, pl.ds(r0, nr)], loc_out.at[q])
            cp.start()
            local.append(cp)
        for q in range(m):
            copy(q).wait_send()
        for cp in local:
            cp.wait()

    stage_bytes = 2 * sum(_nbytes(a.shape, a.dtype) for a in halves)
    return pl.pallas_call(
        body, name=name, in_specs=[ANY] * n, out_specs=[ANY] * n,
        out_shape=[pltpu.HBM((2,) + a.shape, a.dtype) for a in halves],
        scratch_shapes=[pltpu.VMEM(a.shape, a.dtype) for a in halves] * 2
        + [pltpu.SemaphoreType.DMA((m,)), pltpu.SemaphoreType.DMA((m,)), pltpu.SemaphoreType.DMA((n,)),
           pltpu.SemaphoreType.DMA((m,)), pltpu.SemaphoreType.DMA((m,))],
        compiler_params=_params(stage_bytes + (4 << 20)),
    )(*halves)


def _allreduce_small(p, after, *, name):
    r = p.shape[0]

    def body(p_ref, after_ref, o_ref, buf, send, recv):
        x, y, c = _place()
        me = 4 * x + 2 * y + c
        peers = [(x ^ ((j >> 2) & 1), y ^ ((j >> 1) & 1), c ^ (j & 1)) for j in range(1, 8)]

        def copy(j, slot):
            return pltpu.make_async_remote_copy(
                src_ref=p_ref, dst_ref=buf.at[slot], send_sem=send.at[j], recv_sem=recv.at[j],
                device_id=peers[j], device_id_type=MESH)

        for j in range(7):
            copy(j, me).start()
        buf[me] = p_ref[...]
        for j in range(7):
            px, py, pc = peers[j]
            copy(j, 4 * px + 2 * py + pc).wait_recv()
        for j in range(7):
            copy(j, me).wait_send()
        acc = buf[0]
        for k in range(1, 8):
            acc = acc + buf[k]
        o_ref[...] = acc

    vspec = pl.BlockSpec(memory_space=pltpu.VMEM)
    return pl.pallas_call(
        body, name=name, in_specs=[vspec, ANY], out_specs=vspec, out_shape=jax.ShapeDtypeStruct((r, LANES), F32),
        scratch_shapes=[pltpu.VMEM((8, r, LANES), F32), pltpu.SemaphoreType.DMA((7,)), pltpu.SemaphoreType.DMA((7,))],
    )(p, after)


def _adamw_fn(w, g, m, v):
    m = ADAM_B1 * m + (1.0 - ADAM_B1) * g
    v = ADAM_B2 * v + (1.0 - ADAM_B2) * (g * g)
    m_hat = m / (1.0 - ADAM_B1 ** ADAM_STEP)
    v_hat = v / (1.0 - ADAM_B2 ** ADAM_STEP)
    delta = -ADAM_LR * (m_hat / (jnp.sqrt(v_hat) + ADAM_EPS) + ADAM_WD * w)
    return delta, m, v


def _adamw(w, g, m, v, *, name):
    c = w.shape[1]
    return _rowwise(_adamw_fn, [w, g, m, v], [], [(c, F32)] * 3, name=name, tr=128)


def _pack(vecs, rows):
    flat = jnp.concatenate([a.reshape(-1).astype(F32) for a in vecs])
    return jnp.pad(flat, (0, rows * LANES - flat.shape[0])).reshape(rows, LANES)


def _unpack(p, like):
    flat, out, o = p.reshape(-1), [], 0
    for a in like:
        out.append(flat[o:o + a.size].reshape(a.shape))
        o += a.size
    return out


def kernel(x, mem, g_mix, w_in, b_if, b_gate, conv_w, conv_b, ml_norm_g, g_mem, w_mem_kv, q_norm_g, k_norm_g, w_sb_proj, w_ml_proj, w_x_proj, w_out, g_mlp, w_ff1, w_ff2, loss_target, m_g_mix, m_w_in, m_b_if, m_b_gate, m_conv_w, m_conv_b, m_ml_norm_g, m_g_mem, m_w_mem_kv, m_q_norm_g, m_k_norm_g, m_w_sb_proj, m_w_ml_proj, m_w_x_proj, m_w_out, m_g_mlp, m_w_ff1, m_w_ff2, v_g_mix, v_w_in, v_b_if, v_b_gate, v_conv_w, v_conv_b, v_ml_norm_g, v_g_mem, v_w_mem_kv, v_q_norm_g, v_k_norm_g, v_w_sb_proj, v_w_ml_proj, v_w_x_proj, v_w_out, v_g_mlp, v_w_ff1, v_w_ff2):
    _, s, d = x.shape
    nm = mem.shape[1]
    n_in = 4 * w_in.shape[2]
    dff = 4 * w_ff1.shape[2]
    sbh = d // SB_HD
    hh = ML_HEADS
    dh = d // hh
    nc = s // CHUNK
    assert n_in == 11 * d + 2 * hh and d % (2 * LANES) == 0 and s % LANES == 0
    x2, mem2, tgt = x[0], mem[0], loss_target[0]

    k4 = 2 * lax.axis_index("x") + lax.axis_index("y")
    me = 2 * k4 + lax.axis_index("c")
    g_first = _allgather_two_level(w_in[0].astype(BF16), conv_w[0], name="gather_w_in")
    later = [a[0].astype(BF16) for a in (w_mem_kv, w_sb_proj, w_ml_proj, w_x_proj, w_out, w_ff1, w_ff2)]
    gw_send, gw_recv, gw_src, gw_land, gw_token = _split_start(
        "quarters", later, [(4,) + a.shape for a in later], g_first[0], name="gather_rest_start")
    cols = lambda a: a.transpose(1, 0, 2).reshape(a.shape[1], 4 * a.shape[2])
    rws = lambda a: a.reshape(4 * a.shape[1], a.shape[2])
    w_in_f = cols(g_first[0])
    w_main = jnp.concatenate([w_in_f[:, :7 * d], w_in_f[:, 7 * d + 2 * hh:]], axis=1)
    w_if = jnp.pad(w_in_f[:, 7 * d:7 * d + 2 * hh], ((0, 0), (0, LANES - 2 * hh)))
    conv_wf = cols(g_first[1])
    b_if_p = jnp.pad(b_if, ((0, 0), (0, LANES - 2 * hh)))

    (hn,) = _rowwise(_rms_fwd, [x2], [g_mix], [(d, BF16)], name="norm_in")
    zm = _mm(hn, w_main, after=gw_token, name="proj_in")
    zif = _mm(hn, w_if, name="proj_if")
    y_sb, a_sb = _sb_fwd(zm, sbh, name="sb_fwd")

    def gate_fn(z, b):
        pre = z + b
        lane = lax.broadcasted_iota(jnp.int32, pre.shape, 1)
        return jnp.where(lane < hh, pre, -_softplus(-pre))

    (gcol,) = _rowwise(gate_fn, [zif], [b_if_p], [(LANES, F32)], name="ml_gates")
    grow = gcol[:, :8].T.reshape(8, nc, CHUNK).transpose(1, 0, 2)
    mqk = _conv_fwd(zm, 3 * d, 2 * d, conv_wf, conv_b, name="conv_fwd")
    hm, cst, nst, mst = _ml_fwd(mqk, zm, 5 * d, gcol, grow, d, name="ml_fwd")

    def mlout_fn(hv, o, g):
        ys = [_rms_fwd(hv[:, k * dh:(k + 1) * dh], g[:, k * dh:(k + 1) * dh]) for k in range(hh)]
        return jnp.concatenate(ys, axis=1) * _sigmoid(o)

    (y_ml,) = _rowwise(mlout_fn, [hm, (zm, d, 6)], [ml_norm_g], [(d, BF16)], name="ml_out")
    gw_land = _split_wait("quarters", gw_send, gw_recv, gw_src, gw_land, [y_ml, y_sb], name="gather_rest_wait")
    gw = [lax.dynamic_update_index_in_dim(ld, a, k4, 0) for ld, a in zip(gw_land, later)]
    w_kv, w_sbp, w_mlp, w_xp, w_o, w_f1, w_f2 = (cols(gw[0]), rws(gw[1]), rws(gw[2]), rws(gw[3]), rws(gw[4]),
                                                 cols(gw[5]), rws(gw[6]))
    (memn,) = _rowwise(_rms_fwd, [mem2], [g_mem], [(d, BF16)], name="norm_mem")
    kv = _mm(memn, w_kv, name="proj_kv")
    y_x = _xa_fwd(zm, 7 * d, kv, q_norm_g, k_norm_g, d, name="xa_fwd")
    p_sb = _mm(y_sb, w_sbp, name="proj_sb")
    p_ml = _mm(y_ml, w_mlp, name="proj_ml")
    p_x = _mm(y_x, w_xp, name="proj_x")

    def merge_fn(a, b, c, g0, g1, g2, bg):
        return (_sigmoid(g0 + bg[:, :d]) * a + _sigmoid(g1 + bg[:, d:2 * d]) * b + _sigmoid(g2 + bg[:, 2 * d:]) * c)

    gate_cols = [(zm, d, 8), (zm, d, 9), (zm, d, 10)]
    (mixed,) = _rowwise(merge_fn, [p_sb, p_ml, p_x] + gate_cols, [b_gate], [(d, BF16)], name="merge")
    x1 = _mm(mixed, w_o, tiles=[x2], name="proj_out")
    (h2,) = _rowwise(_rms_fwd, [x1], [g_mlp], [(d, BF16)], name="norm_mlp")
    u, act = _mm(h2, w_f1, post=lambda r: (r, jnp.square(jnp.maximum(r, 0.0))), out_dtype=(F32, BF16), name="ff1")
    dy = _mm(act, w_f2, tiles=[x1, tgt], post=lambda r, xv, tv: (r + xv - tv) * (1.0 / d), name="ff2")
    (loss_cols,) = _rowwise(lambda g: (jnp.sum(g * g, axis=0, keepdims=True) * (0.5 * d),), [dy], [], [], [d],
                            name="loss")

    du = _mm(dy, w_f2, tb=True, tiles=[u], post=lambda r, uv: r * 2.0 * jnp.maximum(uv, 0.0), out_dtype=BF16,
             name="ff2_dx")
    dw_f2 = _mm(act, dy, ta=True, name="ff2_dw")
    dw_f1 = _mm(h2, du, ta=True, name="ff1_dw")
    dh2 = _mm(du, w_f1, tb=True, name="ff1_dx")

    def norm_bwd_fn(xv, dyv, res, g):
        dx, dg = _rms_bwd(xv, g, dyv)
        return dx + res, jnp.sum(dg, axis=0, keepdims=True)

    dx1, dg_mlp = _rowwise(norm_bwd_fn, [x1, dh2, dy], [g_mlp], [(d, F32)], [d], name="norm_mlp_bwd")
    dmixed = _mm(dx1, w_o, tb=True, name="proj_out_dx")
    dw_o = _mm(mixed, dx1, ta=True, name="proj_out_dw")

    def merge_bwd_fn(dm, a, b, c, g0, g1, g2, bg):
        outs, dgs = [], []
        for p, g, k in ((a, g0, 0), (b, g1, 1), (c, g2, 2)):
            sg = _sigmoid(g + bg[:, k * d:(k + 1) * d])
            outs.append(dm * sg)
            dgs.append(dm * p * sg * (1.0 - sg))
        dgate = jnp.concatenate(dgs, axis=1)
        return (*outs, dgate, jnp.sum(dgate, axis=0, keepdims=True))

    dp_sb, dp_ml, dp_x, dgate, db_gate = _rowwise(
        merge_bwd_fn, [dmixed, p_sb, p_ml, p_x] + gate_cols, [b_gate], [(d, BF16)] * 3 + [(3 * d, BF16)], [3 * d],
        name="merge_bwd", tr=128)
    dw_sbp = _mm(y_sb, dp_sb, ta=True, name="proj_sb_dw")
    dw_mlp = _mm(y_ml, dp_ml, ta=True, name="proj_ml_dw")
    dw_xp = _mm(y_x, dp_x, ta=True, name="proj_x_dw")
    dy_sb = _mm(dp_sb, w_sbp, tb=True, out_dtype=BF16, name="proj_sb_dx")
    dy_ml = _mm(dp_ml, w_mlp, tb=True, name="proj_ml_dx")
    dy_x = _mm(dp_x, w_xp, tb=True, out_dtype=BF16, name="proj_x_dx")

    dxq, dkn, dxv, dg_qn = _xa_bwd(zm, 7 * d, kv, q_norm_g, k_norm_g, dy_x, d, name="xa_bwd")

    def knorm_bwd_fn(kvv, dknv, dvv, g):
        dks, dgs = [], []
        for k in range(X_HEADS):
            sl = slice(k * dh, (k + 1) * dh)
            dk, dg = _rms_bwd(kvv[:, sl], g, dknv[:, sl])
            dks.append(dk)
            dgs.append(jnp.sum(dg, axis=0, keepdims=True))
        return jnp.concatenate(dks + [dvv], axis=1), dgs[0] + dgs[1] + dgs[2] + dgs[3]

    dkv, dg_kn = _rowwise(knorm_bwd_fn, [(kv, d, 0), dkn, dxv], [k_norm_g], [(2 * d, BF16)], [dh], name="xa_knorm_bwd")
    dw_kv = _mm(memn, dkv, ta=True, name="proj_kv_dw")
    dmemn = _mm(dkv, w_kv, tb=True, name="proj_kv_dx")

    def gmem_fn(mv, dv_, g):
        _, dg = _rms_bwd(mv, g, dv_)
        return (jnp.sum(dg, axis=0, keepdims=True),)

    (dg_mem,) = _rowwise(gmem_fn, [mem2, dmemn], [g_mem], [], [d], name="norm_mem_bwd")

    uncols = lambda a: a.reshape(a.shape[0], 4, a.shape[1] // 4).transpose(1, 0, 2)
    unrws = lambda a: a.reshape(4, a.shape[0] // 4, a.shape[1])
    to_parts = lambda q: q.astype(BF16).reshape(4, 2, q.shape[1] // 2, q.shape[2])
    early = [to_parts(q) for q in (uncols(dw_kv), unrws(dw_sbp), unrws(dw_mlp), unrws(dw_xp), unrws(dw_o),
                                   uncols(dw_f1), unrws(dw_f2))]
    ge_send, ge_recv, ge_src, ge_land, ge_token = _split_start(
        "grads", early, [(8,) + a.shape[2:] for a in early], dg_mem, name="exchange_early_start")

    dsq, dsk, dsv = _sb_bwd(zm, dy_sb, a_sb, ge_token, sbh, name="sb_bwd")

    def mlout_bwd_fn(dyv, hv, o, g):
        sg = _sigmoid(o)
        dn = dyv * sg
        dxs, dgs, ys = [], [], []
        for k in range(hh):
            sl = slice(k * dh, (k + 1) * dh)
            ys.append(_rms_fwd(hv[:, sl], g[:, sl]))
            dxk, dgk = _rms_bwd(hv[:, sl], g[:, sl], dn[:, sl])
            dxs.append(dxk)
            dgs.append(dgk)
        do = dyv * jnp.concatenate(ys, axis=1) * sg * (1.0 - sg)
        return jnp.concatenate(dxs, axis=1), do, jnp.sum(jnp.concatenate(dgs, axis=1), axis=0, keepdims=True)

    dhm, dmlo, dg_mln = _rowwise(mlout_bwd_fn, [dy_ml, hm, (zm, d, 6)], [ml_norm_g], [(d, F32), (d, BF16)], [d],
                                 name="ml_out_bwd")
    dmqk, dmlv, dgc, dgr = _ml_bwd(mqk, zm, 5 * d, gcol, grow, cst, nst, mst, dhm, d, name="ml_bwd")
    dmlqk, dconv_w, dconv_b = _conv_bwd(zm, 3 * d, 2 * d, conv_wf, conv_b, dmqk, name="conv_bwd")
    dgr_t = jnp.pad(dgr.transpose(1, 0, 2).reshape(8, s).T, ((0, 0), (0, LANES - 8)))

    def gate_bwd_fn(a, b, z, bias):
        tot = a + b
        rows_t = tot.shape[0]
        r = lax.broadcasted_iota(jnp.int32, (rows_t, rows_t), 0)
        c = lax.broadcasted_iota(jnp.int32, (rows_t, rows_t), 1)
        sh = CHUNK.bit_length() - 1
        same_chunk = jnp.right_shift(r, sh) == jnp.right_shift(c, sh)
        dlf = _u01dot(((c >= r) & same_chunk).astype(BF16), tot)
        lane = lax.broadcasted_iota(jnp.int32, tot.shape, 1)
        dz = jnp.where(lane < hh, tot, jnp.where(lane < 2 * hh, dlf * _sigmoid(-(z + bias)), 0.0))
        return dz, jnp.sum(dz, axis=0, keepdims=True)

    dzif, db_if_p = _rowwise(gate_bwd_fn, [dgc, dgr_t, zif], [b_if_p], [(LANES, BF16)], [LANES], name="ml_gates_bwd",
                             tr=8 * CHUNK)
    dzm = jnp.concatenate([dsq, dsk, dsv, dmlqk, dmlv, dmlo, dxq, dgate], axis=1)
    dw_main = _mm(hn, dzm, ta=True, out_dtype=BF16, name="proj_in_dw")
    dw_if = _mm(hn, dzif, ta=True, out_dtype=BF16, name="proj_if_dw")
    dw_in = jnp.concatenate([dw_main[:, :7 * d], dw_if[:, :2 * hh], dw_main[:, 7 * d:]], axis=1)
    late = [to_parts(uncols(dw_in))]
    gl_send, gl_recv, gl_src, gl_land, gl_token = _split_start(
        "grads", late, [(8,) + a.shape[2:] for a in late], dw_if, name="exchange_late_start")
    dhn = _mm(dzm, w_main, tb=True, after=gl_token, name="proj_in_dx")
    dhn = _mm(dzif, w_if, tb=True, tiles=[dhn], name="proj_if_dx")
    dx, dg_mix = _rowwise(norm_bwd_fn, [x2, dhn, dx1], [g_mix], [(d, F32)], [d], name="norm_in_bwd")

    own = lambda p: lax.dynamic_index_in_dim(lax.dynamic_index_in_dim(p, k4, 0, keepdims=False),
                                             lax.axis_index("c"), 0, keepdims=False)

    def finish(tag, send, recv, src, land, parts, after, ws, ms, vs):
        land = _split_wait("grads", send, recv, src, land, after, name=f"exchange_{tag}_wait")
        got = [lax.dynamic_update_index_in_dim(ld, own(p), me, 0) for ld, p in zip(land, parts)]
        halves = [_sum8(r, name=f"sum_grads_{tag}{i}") for i, r in enumerate(got)]
        both = _swap_halves(halves, name=f"swap_halves_{tag}")
        gs = [b.reshape(2 * b.shape[1], b.shape[2]) for b in both]
        return gs, [_adamw(w, g, m, v, name=f"adamw_{tag}{i}") for i, (w, g, m, v) in enumerate(zip(ws, gs, ms, vs))]

    first = lambda arrs: [a[0] for a in arrs]
    g_early, out_early = finish(
        "early", ge_send, ge_recv, ge_src, ge_land, early, [dx],
        first([w_mem_kv, w_sb_proj, w_ml_proj, w_x_proj, w_out, w_ff1, w_ff2]),
        first([m_w_mem_kv, m_w_sb_proj, m_w_ml_proj, m_w_x_proj, m_w_out, m_w_ff1, m_w_ff2]),
        first([v_w_mem_kv, v_w_sb_proj, v_w_ml_proj, v_w_x_proj, v_w_out, v_w_ff1, v_w_ff2]))
    g_late, out_late = finish(
        "late", gl_send, gl_recv, gl_src, gl_land, late, [o[0] for o in out_early],
        first([w_in]), first([m_w_in]), first([v_w_in]))
    g_big = [g[None] for g in g_late + g_early]
    big_out = [[o[None] for o in outs] for outs in out_late + out_early]

    small_g = [dg_mix, db_if_p[:, :2 * hh], db_gate, dconv_w, dconv_b, dg_mln, dg_mem, dg_qn, dg_kn, dg_mlp,
               jnp.sum(loss_cols).reshape(1, 1)]
    n_small = sum(a.size for a in small_g)
    rows = -(-n_small // (8 * LANES)) * 8
    g_small = _unpack(_allreduce_small(_pack(small_g, rows), out_late[0][0], name="allreduce_small"), small_g)
    loss = g_small[-1].reshape(())
    qw = conv_w.shape[2]
    g_conv_w = lax.dynamic_slice_in_dim(g_small[3], k4 * qw, qw, axis=1)
    g_small_w = [g_small[0], g_small[1], g_small[2], g_conv_w] + g_small[4:10]
    sm_w = [g_mix, b_if, b_gate, conv_w[0], conv_b, ml_norm_g, g_mem, q_norm_g, k_norm_g, g_mlp]
    sm_m = [m_g_mix, m_b_if, m_b_gate, m_conv_w[0], m_conv_b, m_ml_norm_g, m_g_mem, m_q_norm_g, m_k_norm_g, m_g_mlp]
    sm_v = [v_g_mix, v_b_if, v_b_gate, v_conv_w[0], v_conv_b, v_ml_norm_g, v_g_mem, v_q_norm_g, v_k_norm_g, v_g_mlp]
    n_sw = sum(a.size for a in sm_w)
    rows_w = -(-n_sw // (8 * LANES)) * 8
    sm_out = _adamw(_pack(sm_w, rows_w), _pack(g_small_w, rows_w), _pack(sm_m, rows_w), _pack(sm_v, rows_w),
                    name="adamw_small")
    sm_delta, sm_newm, sm_newv = [_unpack(p, sm_w) for p in sm_out]

    order = ["g_mix", "w_in", "b_if", "b_gate", "conv_w", "conv_b", "ml_norm_g", "g_mem", "w_mem_kv", "q_norm_g",
             "k_norm_g", "w_sb_proj", "w_ml_proj", "w_x_proj", "w_out", "g_mlp", "w_ff1", "w_ff2"]
    small_names = ["g_mix", "b_if", "b_gate", "conv_w", "conv_b", "ml_norm_g", "g_mem", "q_norm_g", "k_norm_g", "g_mlp"]
    big_names = ["w_in", "w_mem_kv", "w_sb_proj", "w_ml_proj", "w_x_proj", "w_out", "w_ff1", "w_ff2"]
    grads, deltas, new_m, new_v = {}, {}, {}, {}
    for i, nme in enumerate(small_names):
        shp = sm_w[i].shape if nme != "conv_w" else conv_w.shape
        grads[nme] = g_small_w[i].reshape(shp)
        deltas[nme], new_m[nme], new_v[nme] = (sm_delta[i].reshape(shp), sm_newm[i].reshape(shp),
                                               sm_newv[i].reshape(shp))
    for i, nme in enumerate(big_names):
        grads[nme] = g_big[i]
        deltas[nme], new_m[nme], new_v[nme] = big_out[i]
    return (loss, dx[None], *[grads[k] for k in order], *[deltas[k] for k in order], *[new_m[k] for k in order],
            *[new_v[k] for k in order])
```

```python
import functools

import jax
import jax.numpy as jnp
from jax import lax
from jax.experimental import pallas as pl
from jax.experimental.pallas import tpu as pltpu

F32 = jnp.float32
BF16 = jnp.bfloat16
MESH = pl.DeviceIdType.MESH

EPS = 1e-6
SB_HD = 128
SB_SLOTS = 8
ML_HEADS = 4
X_HEADS = 4
CHUNK = 64
CONV_W = 4
LANES = 128
ADAM_LR = 0.001
ADAM_B1 = 0.9
ADAM_B2 = 0.999
ADAM_EPS = 1e-08
ADAM_WD = 0.01
ADAM_STEP = 10
VMEM_CAP = 56 * 1024 * 1024
NEG = -1e30

NT = (((1,), (1,)), ((), ()))
NN = (((1,), (0,)), ((), ()))
TN = (((0,), (0,)), ((), ()))


def _dot(a, b, dn=NN):
    return lax.dot_general(a.astype(BF16), b.astype(BF16), dn, preferred_element_type=F32)


def _dot01(x, u, dn=NN):
    hi = x.astype(BF16)
    lo = (x - hi.astype(F32)).astype(BF16)
    return (lax.dot_general(hi, u, dn, preferred_element_type=F32)
            + lax.dot_general(lo, u, dn, preferred_element_type=F32))


def _u01dot(u, x):
    hi = x.astype(BF16)
    lo = (x - hi.astype(F32)).astype(BF16)
    return (lax.dot_general(u, hi, NN, preferred_element_type=F32)
            + lax.dot_general(u, lo, NN, preferred_element_type=F32))


def _pick(n, cands):
    for c in cands:
        if c <= n and n % c == 0:
            return c
    return n


def _nbytes(shape, dtype):
    n = 1
    for s in shape:
        n *= s
    return n * jnp.dtype(dtype).itemsize


def _params(vmem_bytes):
    return pltpu.CompilerParams(vmem_limit_bytes=int(min(VMEM_CAP, max(vmem_bytes, 16 * 1024 * 1024))))


def _hbm(a):
    return pltpu.with_memory_space_constraint(a, pltpu.HBM)


def _softplus(z):
    return jnp.maximum(z, 0.0) + jnp.log(1.0 + jnp.exp(-jnp.abs(z)))


def _sigmoid(z):
    return 1.0 / (1.0 + jnp.exp(-z))


def _rms_fwd(xv, g):
    r = lax.rsqrt(jnp.mean(xv * xv, axis=-1, keepdims=True) + EPS)
    return xv * r * g


def _rms_bwd(xv, g, dy):
    r = lax.rsqrt(jnp.mean(xv * xv, axis=-1, keepdims=True) + EPS)
    xh = xv * r
    dxh = dy * g
    dx = r * (dxh - xh * jnp.mean(dxh * xh, axis=-1, keepdims=True))
    return dx, dy * xh


def _mm(a, b, *, name, ta=False, tb=False, tiles=(), post=None, out_dtype=F32, bm=1024, bn=1024, bk=1024, after=None):
    m, k = (a.shape[1], a.shape[0]) if ta else a.shape
    n = b.shape[0] if tb else b.shape[1]
    tm = _pick(m, (bm, 512, 256, 128))
    tn = _pick(n, (bn, 512, 256, 128))
    tk = _pick(k, (bk, 512, 256, 128))
    nk = k // tk
    if (m // tm) * (n // tn) * nk < 8 and tm % 256 == 0:
        tm //= 2
    dn = (((0 if ta else 1,), (1 if tb else 0,)), ((), ()))
    dts = out_dtype if isinstance(out_dtype, tuple) else (out_dtype,)
    nt, no = len(tiles), len(dts)
    if post is None:
        post = lambda r, *ts: sum((t.astype(F32) for t in ts), r)

    def body(*refs):
        a_ref, b_ref = refs[:2]
        t_refs = refs[2:2 + nt]
        o_refs = refs[2 + nt + (after is not None):2 + nt + (after is not None) + no]
        part = lax.dot_general(a_ref[...].astype(BF16), b_ref[...].astype(BF16), dn, preferred_element_type=F32)

        def finish(r):
            res = post(r, *[t[...] for t in t_refs])
            res = res if isinstance(res, tuple) else (res,)
            for o, v in zip(o_refs, res):
                o[...] = v.astype(o.dtype)

        if nk == 1:
            finish(part)
        else:
            acc_ref = refs[-1]
            kk = pl.program_id(2)

            @pl.when(kk == 0)
            def _():
                acc_ref[...] = part

            @pl.when(kk > 0)
            def _():
                acc_ref[...] += part

            @pl.when(kk == nk - 1)
            def _():
                finish(acc_ref[...])

    a_spec = pl.BlockSpec((tk, tm), lambda i, j, q: (q, i)) if ta else pl.BlockSpec((tm, tk), lambda i, j, q: (i, q))
    b_spec = pl.BlockSpec((tn, tk), lambda i, j, q: (j, q)) if tb else pl.BlockSpec((tk, tn), lambda i, j, q: (q, j))
    o_spec = pl.BlockSpec((tm, tn), lambda i, j, q: (i, j))
    ins, specs = [_hbm(a), _hbm(b)] + [_hbm(t) for t in tiles], [a_spec, b_spec] + [o_spec] * nt
    vm = 2 * (_nbytes((tm, tk), a.dtype) + _nbytes((tk, tn), b.dtype)) + 3 * _nbytes((tm, tn), F32) \
        + _nbytes((tm, tk), BF16) + _nbytes((tk, tn), BF16) \
        + 2 * sum(_nbytes((tm, tn), t.dtype) for t in tiles) + 2 * sum(_nbytes((tm, tn), dt) for dt in dts)
    if after is not None:
        ins.append(after)
        specs.append(ANY)
    res = pl.pallas_call(
        body, name=name, grid=(m // tm, n // tn, nk), in_specs=specs, out_specs=[o_spec] * no,
        out_shape=[pltpu.HBM((m, n), dt) for dt in dts], scratch_shapes=[pltpu.VMEM((tm, tn), F32)] if nk > 1 else [],
        compiler_params=_params(vm + (4 << 20)),
    )(*ins)
    return res[0] if no == 1 else tuple(res)


def _rowwise(fn, rows, consts, outs, reds=(), *, name, tr=256, temps=6):
    rows = [r if isinstance(r, tuple) else (r, r.shape[1], 0) for r in rows]
    nrows = rows[0][0].shape[0]
    t = _pick(nrows, (tr, 128, 64, 32, 16, 8))
    nr, nc, no = len(rows), len(consts), len(outs)

    def body(*refs):
        rin, cin = refs[:nr], refs[nr:nr + nc]
        oref, rref = refs[nr + nc:nr + nc + no], refs[nr + nc + no:]
        res = fn(*[r[...] for r in rin], *[c[...] for c in cin])
        if not isinstance(res, (tuple, list)):
            res = (res,)
        for o, v in zip(oref, res[:no]):
            o[...] = v.astype(o.dtype)
        if rref:
            @pl.when(pl.program_id(0) == 0)
            def _():
                for r in rref:
                    r[...] = jnp.zeros_like(r)

            for r, v in zip(rref, res[no:]):
                r[...] += v

    in_specs = [pl.BlockSpec((t, w), functools.partial(lambda i, ci: (i, ci), ci=ci)) for (_, w, ci) in rows]
    in_specs += [pl.BlockSpec(c.shape, functools.partial(lambda i, nd: (0,) * nd, nd=c.ndim)) for c in consts]
    out_specs = [pl.BlockSpec((t, w), lambda i: (i, 0)) for (w, _) in outs]
    out_specs += [pl.BlockSpec((1, w), lambda i: (0, 0)) for w in reds]
    out_shape = [pltpu.HBM((nrows, w), dt) for (w, dt) in outs]
    out_shape += [jax.ShapeDtypeStruct((1, w), F32) for w in reds]
    widest = max([w for (_, w, _) in rows] + [w for (w, _) in outs])
    vm = 2 * sum(_nbytes((t, w), a.dtype) for (a, w, _) in rows) + 2 * sum(_nbytes((t, w), dt) for (w, dt) in outs)
    vm += temps * _nbytes((t, widest), F32) + (2 << 20)
    res = pl.pallas_call(
        body, name=name, grid=(nrows // t,), in_specs=in_specs, out_specs=out_specs, out_shape=out_shape,
        compiler_params=_params(vm),
    )(*[_hbm(a) for (a, _, _) in rows], *consts)
    return list(res)


def _sb_tiles(s, tq, tk):
    tq = _pick(s, (tq, 256, 128))
    tk = _pick(tq, (tk, 128))
    return tq, tk, tq // tk


def _sb_fwd(zm, heads, *, name, tq=512, tk=256):
    s = zm.shape[0]
    tq, tk, nd = _sb_tiles(s, tq, tk)
    scale = SB_HD ** -0.5

    def body(q_ref, k_ref, v_ref, o_ref, a_out, stage, sem):
        h, i = pl.program_id(0), pl.program_id(1)
        qb = (q_ref[...] * scale).astype(BF16)
        r = lax.broadcasted_iota(jnp.int32, (tq, tk), 0)
        c = lax.broadcasted_iota(jnp.int32, (tq, tk), 1)
        ur = lax.broadcasted_iota(jnp.int32, (tk, tk), 0)
        uc = lax.broadcasted_iota(jnp.int32, (tk, tk), 1)
        usuf = (ur > uc).astype(BF16)

        def out_copy(slot, j):
            return pltpu.make_async_copy(stage.at[slot], a_out.at[h, i, j], sem.at[slot])

        def tile(j, carry, causal, slot, reuse):
            acc, cl = carry
            if reuse is True:
                out_copy(slot, 0).wait()
            elif reuse is not None:
                @pl.when(reuse)
                def _():
                    out_copy(slot, 0).wait()
            rows = pl.ds(pl.multiple_of(j * tk, tk), tk)
            kb = k_ref[rows, :].astype(BF16)
            vb = v_ref[rows, :].astype(BF16)
            z = lax.dot_general(qb, kb, NT, preferred_element_type=F32)
            lsig = -_softplus(z)
            l = lsig if causal is None else jnp.where(causal, lsig, 0.0)
            loga = z + lsig + _dot01(l, usuf) + cl
            if causal is not None:
                loga = jnp.where(causal, loga, NEG)
            ab = jnp.exp(loga).astype(BF16)
            acc = acc + lax.dot_general(ab, vb, NN, preferred_element_type=F32)
            stage[slot] = ab
            out_copy(slot, j).start()
            return acc, cl + jnp.sum(l, axis=1, keepdims=True)

        carry = (jnp.zeros((tq, SB_HD), F32), jnp.zeros((tq, 1), F32))
        for n, dd in enumerate(range(nd - 1, -1, -1)):
            carry = tile(i * nd + dd, carry, c + dd * tk < r, n, None)

        def rest(n, cr):
            return tile(i * nd - 1 - n, cr, None, (nd + n) % SB_SLOTS, nd + n >= SB_SLOTS)

        acc, _ = lax.fori_loop(0, i * nd, rest, carry)
        total = (i + 1) * nd
        for back in range(1, SB_SLOTS + 1):
            @pl.when(total >= back)
            def _():
                out_copy((total - back) % SB_SLOTS, 0).wait()

        o_ref[...] = acc.astype(o_ref.dtype)

    assert nd <= SB_SLOTS
    blk = lambda off: pl.BlockSpec((s, SB_HD), functools.partial(lambda h, i, off: (0, off + h), off=off))
    return pl.pallas_call(
        body, name=name, grid=(heads, s // tq),
        in_specs=[pl.BlockSpec((tq, SB_HD), lambda h, i: (i, h)), blk(heads), blk(2 * heads)],
        out_specs=[pl.BlockSpec((tq, SB_HD), lambda h, i: (i, h)), ANY],
        out_shape=[pltpu.HBM((s, heads * SB_HD), BF16), pltpu.HBM((heads, s // tq, s // tk, tq, tk), BF16)],
        scratch_shapes=[pltpu.VMEM((SB_SLOTS, tq, tk), BF16), pltpu.SemaphoreType.DMA((SB_SLOTS,))],
        compiler_params=_params(8 * s * SB_HD * 4 + 24 * tq * tk * 4 + (8 << 20)),
    )(_hbm(zm), _hbm(zm), _hbm(zm))


def _sb_bwd(zm, dy, a_all, after, heads, *, name, tq=512, tk=256):
    s = zm.shape[0]
    tq, tk, nd = _sb_tiles(s, tq, tk)
    nq = s // tq
    scale = SB_HD ** -0.5

    def body(q_ref, k_ref, v_ref, do_ref, a_in, after_ref, dq_ref, dk_ref, dv_ref, dka, dva, abuf, sem):
        h, i = pl.program_id(0), pl.program_id(1)

        @pl.when(i == 0)
        def _():
            dka[...] = jnp.zeros_like(dka)
            dva[...] = jnp.zeros_like(dva)

        qb = (q_ref[...] * scale).astype(BF16)
        dob = do_ref[...].astype(BF16)
        r = lax.broadcasted_iota(jnp.int32, (tq, tk), 0)
        c = lax.broadcasted_iota(jnp.int32, (tq, tk), 1)
        ur = lax.broadcasted_iota(jnp.int32, (tk, tk), 0)
        uc = lax.broadcasted_iota(jnp.int32, (tk, tk), 1)
        uexcl = (ur < uc).astype(BF16)

        def fetch(j, slot):
            return pltpu.make_async_copy(a_in.at[h, i, j], abuf.at[slot], sem.at[slot])

        total = (i + 1) * nd
        ahead = SB_SLOTS - 1

        def tile(j, carry, causal):
            dq, cg = carry
            slot = j % SB_SLOTS
            fetch(j, slot).wait()

            @pl.when(j + ahead < total)
            def _():
                fetch(j + ahead, (j + ahead) % SB_SLOTS).start()

            rows = pl.ds(pl.multiple_of(j * tk, tk), tk)
            kb = k_ref[rows, :].astype(BF16)
            vb = v_ref[rows, :].astype(BF16)
            z = lax.dot_general(qb, kb, NT, preferred_element_type=F32)
            sig = 1.0 / (1.0 + jnp.exp(-z))
            ab = abuf[slot]
            g = ab.astype(F32) * lax.dot_general(dob, vb, NT, preferred_element_type=F32)
            p = cg + lax.dot_general(g.astype(BF16), uexcl, NN, preferred_element_type=F32)
            dz = g - sig * (g + p)
            if causal is not None:
                dz = jnp.where(causal, dz, 0.0)
            dzb = dz.astype(BF16)
            dva[rows, :] += lax.dot_general(ab, dob, TN, preferred_element_type=F32)
            dka[rows, :] += lax.dot_general(dzb, qb, TN, preferred_element_type=F32)
            dq = dq + lax.dot_general(dzb, kb, NN, preferred_element_type=F32)
            return dq, cg + jnp.sum(g, axis=1, keepdims=True)

        for first in range(ahead):
            @pl.when(first < total)
            def _():
                fetch(first, first).start()

        init = (jnp.zeros((tq, SB_HD), F32), jnp.zeros((tq, 1), F32))
        carry = lax.fori_loop(0, i * nd, lambda j, cr: tile(j, cr, None), init)
        for dd in range(nd):
            carry = tile(i * nd + dd, carry, c + dd * tk < r)
        dq_ref[...] = (carry[0] * scale).astype(dq_ref.dtype)

        @pl.when(i == nq - 1)
        def _():
            dk_ref[...] = dka[...].astype(dk_ref.dtype)
            dv_ref[...] = dva[...].astype(dv_ref.dtype)

    blk = lambda off: pl.BlockSpec((s, SB_HD), functools.partial(lambda h, i, off: (0, off + h), off=off))
    tile_spec = pl.BlockSpec((tq, SB_HD), lambda h, i: (i, h))
    full = pltpu.HBM((s, heads * SB_HD), BF16)
    return pl.pallas_call(
        body, name=name, grid=(heads, nq),
        in_specs=[tile_spec, blk(heads), blk(2 * heads), tile_spec, ANY, ANY],
        out_specs=[tile_spec, blk(0), blk(0)],
        out_shape=[full, full, full],
        scratch_shapes=[pltpu.VMEM((s, SB_HD), F32), pltpu.VMEM((s, SB_HD), F32),
                        pltpu.VMEM((SB_SLOTS, tq, tk), BF16), pltpu.SemaphoreType.DMA((SB_SLOTS,))],
        compiler_params=_params(12 * s * SB_HD * 4 + 32 * tq * tk * 4 + (8 << 20)),
    )(_hbm(zm), _hbm(zm), _hbm(zm), _hbm(dy), a_all, after)


def _conv_taps(u, w_ref, rows_i):
    taps = []
    for j in range(CONV_W):
        sh = CONV_W - 1 - j
        if sh == 0:
            taps.append(u)
        else:
            taps.append(jnp.where(rows_i >= sh, pltpu.roll(u, sh, 0), 0.0))
    return taps


def _conv_fwd(zm, col0, width, cw, cb, *, name):
    s = zm.shape[0]
    bw = _pick(width, (LANES,))
    off = col0 // bw

    def body(u_ref, w_ref, b_ref, o_ref):
        u = u_ref[...]
        rows_i = lax.broadcasted_iota(jnp.int32, u.shape, 0)
        acc = jnp.broadcast_to(b_ref[...], u.shape)
        for j, tp in enumerate(_conv_taps(u, w_ref, rows_i)):
            acc = acc + tp * w_ref[j:j + 1, :]
        o_ref[...] = acc * _sigmoid(acc)

    return pl.pallas_call(
        body, name=name, grid=(width // bw,),
        in_specs=[pl.BlockSpec((s, bw), lambda j: (0, off + j)), pl.BlockSpec((CONV_W, bw), lambda j: (0, j)),
                  pl.BlockSpec((1, bw), lambda j: (0, j))],
        out_specs=pl.BlockSpec((s, bw), lambda j: (0, j)),
        out_shape=pltpu.HBM((s, width), F32),
        compiler_params=_params(12 * s * bw * 4 + (4 << 20)),
    )(_hbm(zm), cw, cb)


def _conv_bwd(zm, col0, width, cw, cb, dqk, *, name):
    s = zm.shape[0]
    bw = _pick(width, (LANES,))
    off = col0 // bw

    def body(u_ref, w_ref, b_ref, d_ref, du_ref, dw_ref, db_ref):
        u = u_ref[...]
        rows_i = lax.broadcasted_iota(jnp.int32, u.shape, 0)
        taps = _conv_taps(u, w_ref, rows_i)
        acc = jnp.broadcast_to(b_ref[...], u.shape)
        for j, tp in enumerate(taps):
            acc = acc + tp * w_ref[j:j + 1, :]
        sg = _sigmoid(acc)
        dc = d_ref[...] * (sg * (1.0 + acc * (1.0 - sg)))
        du = jnp.zeros_like(u)
        for j in range(CONV_W):
            sh = CONV_W - 1 - j
            if sh == 0:
                du = du + dc * w_ref[j:j + 1, :]
            else:
                du = du + jnp.where(rows_i < s - sh, pltpu.roll(dc, s - sh, 0), 0.0) * w_ref[j:j + 1, :]
            dw_ref[j:j + 1, :] = jnp.sum(dc * taps[j], axis=0, keepdims=True)
        du_ref[...] = du.astype(du_ref.dtype)
        db_ref[...] = jnp.sum(dc, axis=0, keepdims=True)

    return pl.pallas_call(
        body, name=name, grid=(width // bw,),
        in_specs=[pl.BlockSpec((s, bw), lambda j: (0, off + j)), pl.BlockSpec((CONV_W, bw), lambda j: (0, j)),
                  pl.BlockSpec((1, bw), lambda j: (0, j)), pl.BlockSpec((s, bw), lambda j: (0, j))],
        out_specs=[pl.BlockSpec((s, bw), lambda j: (0, j)), pl.BlockSpec((CONV_W, bw), lambda j: (0, j)),
                   pl.BlockSpec((1, bw), lambda j: (0, j))],
        out_shape=[pltpu.HBM((s, width), BF16), pltpu.HBM((CONV_W, width), F32),
                   pltpu.HBM((1, width), F32)],
        compiler_params=_params(20 * s * bw * 4 + (4 << 20)),
    )(_hbm(zm), cw, cb, _hbm(dqk))


def _ml_gates(gcol_ref, grow_ref):
    l = CHUNK
    r = lax.broadcasted_iota(jnp.int32, (l, l), 0)
    c = lax.broadcasted_iota(jnp.int32, (l, l), 1)
    gcol = gcol_ref[...]
    grow = grow_ref[0]
    bcol = _u01dot((c <= r).astype(BF16), gcol)
    brow = _dot01(grow, (r <= c).astype(BF16))
    return gcol, grow, bcol, brow, r >= c


def _ml_chunk(h, dh, mq_ref, mk_ref, v_ref, gates, cp, n_prev, m_prev):
    gcol, grow, bcol, brow, tri = gates
    l = CHUNK
    sl = slice(h * dh, (h + 1) * dh)
    qc = mq_ref[:, sl]
    kc = mk_ref[:, sl] * (dh ** -0.5)
    vc = v_ref[:, sl]
    i_row = grow[h:h + 1, :]
    i_col = gcol[:, h:h + 1]
    b_col = bcol[:, ML_HEADS + h:ML_HEADS + h + 1]
    b_row = brow[ML_HEADS + h:ML_HEADS + h + 1, :]
    b_end = b_col[l - 1:l, :]
    d = jnp.where(tri, b_col - b_row + i_row, -jnp.inf)
    m_inter = b_col + m_prev
    m_t = jnp.maximum(m_inter, jnp.max(d, axis=1, keepdims=True))
    w = jnp.exp(d - m_t)
    s_inter = jnp.exp(m_inter - m_t)
    qb, kb, vb = qc.astype(BF16), kc.astype(BF16), vc.astype(BF16)
    cpb = cp.astype(BF16)
    a = lax.dot_general(qb, kb, NT, preferred_element_type=F32)
    sc = a * w
    qcp = lax.dot_general(qb, cpb, NT, preferred_element_type=F32)
    qn = jnp.sum(qc * n_prev, axis=1, keepdims=True)
    num = lax.dot_general(sc.astype(BF16), vb, NN, preferred_element_type=F32) + s_inter * qcp
    den = jnp.sum(sc, axis=1, keepdims=True) + s_inter * qn
    floor = jnp.exp(-m_t)
    dnm = jnp.maximum(jnp.abs(den), floor)
    g_col = b_end - b_col + i_col
    g_row = b_end - b_row + i_row
    m_new = jnp.maximum(b_end + m_prev, jnp.max(g_row, axis=1, keepdims=True))
    decay = jnp.exp(b_end + m_prev - m_new)
    wk = jnp.exp(g_col - m_new)
    return dict(qc=qc, kc=kc, vc=vc, qb=qb, kb=kb, vb=vb, cpb=cpb, w=w, s_inter=s_inter, a=a, sc=sc, qcp=qcp, qn=qn,
                num=num, den=den, floor=floor, dnm=dnm, m_new=m_new, decay=decay, wk=wk, sl=sl)


def _ml_fwd(mqk, zm, vcol, gcol, grow, d_model, *, name):
    s = zm.shape[0]
    nc = s // CHUNK
    dh = d_model // ML_HEADS
    hh = ML_HEADS

    def body(mq_ref, mk_ref, v_ref, gcol_ref, grow_ref, h_ref, cs_ref, ns_ref, ms_ref, c_s, n_s, m_s):
        @pl.when(pl.program_id(0) == 0)
        def _():
            c_s[...] = jnp.zeros_like(c_s)
            n_s[...] = jnp.zeros_like(n_s)
            m_s[...] = jnp.zeros_like(m_s)

        gates = _ml_gates(gcol_ref, grow_ref)
        for h in range(hh):
            cp, n_prev, m_prev = c_s[h], n_s[h], m_s[h][:, 0:1]
            cs_ref[0, h] = cp
            ns_ref[0, h] = n_prev
            ms_ref[0, h] = m_s[h]
            f = _ml_chunk(h, dh, mq_ref, mk_ref, v_ref, gates, cp, n_prev, m_prev)
            h_ref[:, f["sl"]] = f["num"] / f["dnm"]
            c_s[h] = f["decay"] * cp + lax.dot_general((f["vc"] * f["wk"]).astype(BF16), f["kb"], TN,
                                                       preferred_element_type=F32)
            n_s[h] = f["decay"] * n_prev + jnp.sum(f["wk"] * f["kc"], axis=0, keepdims=True)
            m_s[h] = jnp.broadcast_to(f["m_new"], (1, LANES))

    dblk = d_model
    return pl.pallas_call(
        body, name=name, grid=(nc,),
        in_specs=[pl.BlockSpec((CHUNK, dblk), lambda c: (c, 0)), pl.BlockSpec((CHUNK, dblk), lambda c: (c, 1)),
                  pl.BlockSpec((CHUNK, dblk), lambda c: (c, vcol // dblk)),
                  pl.BlockSpec((CHUNK, LANES), lambda c: (c, 0)), pl.BlockSpec((1, 8, CHUNK), lambda c: (c, 0, 0))],
        out_specs=[pl.BlockSpec((CHUNK, dblk), lambda c: (c, 0)),
                   pl.BlockSpec((1, hh, dh, dh), lambda c: (c, 0, 0, 0)),
                   pl.BlockSpec((1, hh, 1, dh), lambda c: (c, 0, 0, 0)),
                   pl.BlockSpec((1, hh, 1, LANES), lambda c: (c, 0, 0, 0))],
        out_shape=[pltpu.HBM((s, d_model), F32), pltpu.HBM((nc, hh, dh, dh), F32),
                   pltpu.HBM((nc, hh, 1, dh), F32), pltpu.HBM((nc, hh, 1, LANES), F32)],
        scratch_shapes=[pltpu.VMEM((hh, dh, dh), F32), pltpu.VMEM((hh, 1, dh), F32), pltpu.VMEM((hh, 1, LANES), F32)],
        compiler_params=_params(8 * hh * dh * dh * 4 + (16 << 20)),
    )(_hbm(mqk), _hbm(mqk), _hbm(zm), _hbm(gcol), _hbm(grow))


def _ml_bwd(mqk, zm, vcol, gcol, grow, cs, ns, ms, dhm, d_model, *, name):
    s = zm.shape[0]
    nc = s // CHUNK
    dh = d_model // ML_HEADS
    hh = ML_HEADS
    l = CHUNK

    def body(mq_ref, mk_ref, v_ref, gcol_ref, grow_ref, cs_ref, ns_ref, ms_ref, dh_ref,
             dqk_ref, dv_ref, dgc_ref, dgr_ref, dc_s, dn_s):
        @pl.when(pl.program_id(0) == 0)
        def _():
            dc_s[...] = jnp.zeros_like(dc_s)
            dn_s[...] = jnp.zeros_like(dn_s)

        gates = _ml_gates(gcol_ref, grow_ref)
        lane = lax.broadcasted_iota(jnp.int32, (l, LANES), 1)
        rowi = lax.broadcasted_iota(jnp.int32, (8, l), 0)
        lastrow = lax.broadcasted_iota(jnp.int32, (l, 1), 0) == l - 1
        dgc = jnp.zeros((l, LANES), F32)
        dgr = jnp.zeros((8, l), F32)
        for h in range(hh):
            cp, n_prev, m_prev = cs_ref[0, h], ns_ref[0, h], ms_ref[0, h][:, 0:1]
            f = _ml_chunk(h, dh, mq_ref, mk_ref, v_ref, gates, cp, n_prev, m_prev)
            dC, dn = dc_s[h], dn_s[h]
            dhv = dh_ref[:, f["sl"]]
            dnum = dhv / f["dnm"]
            hv = f["num"] / f["dnm"]
            ddnm = -jnp.sum(dhv * hv, axis=1, keepdims=True) / f["dnm"]
            dden = jnp.where(jnp.abs(f["den"]) >= f["floor"], ddnm * jnp.sign(f["den"]), 0.0)
            dnb = dnum.astype(BF16)
            dsc = lax.dot_general(dnb, f["vb"], NT, preferred_element_type=F32) + dden
            dvc = lax.dot_general(f["sc"].astype(BF16), dnb, TN, preferred_element_type=F32)
            ds_inter = jnp.sum(dnum * f["qcp"], axis=1, keepdims=True) + dden * f["qn"]
            sdn = (f["s_inter"] * dnum).astype(BF16)
            sdd = f["s_inter"] * dden
            da = dsc * f["w"]
            dab = da.astype(BF16)
            dqc = (lax.dot_general(dab, f["kb"], NN, preferred_element_type=F32)
                   + lax.dot_general(sdn, f["cpb"], NN, preferred_element_type=F32) + sdd * n_prev)
            dcp = f["decay"] * dC + lax.dot_general(sdn, f["qb"], TN, preferred_element_type=F32)
            dnp = f["decay"] * dn + jnp.sum(sdd * f["qc"], axis=0, keepdims=True)
            vw = (f["vc"] * f["wk"]).astype(BF16)
            dCb = dC.astype(BF16)
            dkc = (lax.dot_general(dab, f["qb"], TN, preferred_element_type=F32)
                   + lax.dot_general(vw, dCb, NN, preferred_element_type=F32) + f["wk"] * dn)
            e = lax.dot_general(f["kb"], dCb, NT, preferred_element_type=F32)
            dvc = dvc + e * f["wk"]
            dwk = jnp.sum(e * f["vc"], axis=1, keepdims=True) + jnp.sum(f["kc"] * dn, axis=1, keepdims=True)
            ddecay = jnp.sum(jnp.sum(dC * cp, axis=1, keepdims=True), axis=0, keepdims=True) \
                + jnp.sum(dn * n_prev, axis=1, keepdims=True)
            dd = dsc * f["sc"]
            dlw = dwk * f["wk"]
            db_end = jnp.sum(dlw, axis=0, keepdims=True) + ddecay * f["decay"]
            di_col = dlw
            db_col = jnp.sum(dd, axis=1, keepdims=True) + ds_inter * f["s_inter"] - dlw \
                + jnp.where(lastrow, db_end, 0.0)
            cs_dd = jnp.sum(dd, axis=0, keepdims=True)
            dgc = dgc + jnp.where(lane == h, di_col, 0.0) + jnp.where(lane == hh + h, db_col, 0.0)
            dgr = dgr + jnp.where(rowi == h, cs_dd, 0.0) - jnp.where(rowi == hh + h, cs_dd, 0.0)
            dqk_ref[:, f["sl"]] = dqc
            dqk_ref[:, d_model + h * dh:d_model + (h + 1) * dh] = dkc * (dh ** -0.5)
            dv_ref[:, f["sl"]] = dvc.astype(dv_ref.dtype)
            dc_s[h] = dcp
            dn_s[h] = dnp
        dgc_ref[...] = dgc
        dgr_ref[0] = dgr

    dblk = d_model
    rev = lambda c: nc - 1 - c
    return pl.pallas_call(
        body, name=name, grid=(nc,),
        in_specs=[pl.BlockSpec((l, dblk), lambda c: (rev(c), 0)), pl.BlockSpec((l, dblk), lambda c: (rev(c), 1)),
                  pl.BlockSpec((l, dblk), lambda c: (rev(c), vcol // dblk)),
                  pl.BlockSpec((l, LANES), lambda c: (rev(c), 0)), pl.BlockSpec((1, 8, l), lambda c: (rev(c), 0, 0)),
                  pl.BlockSpec((1, hh, dh, dh), lambda c: (rev(c), 0, 0, 0)),
                  pl.BlockSpec((1, hh, 1, dh), lambda c: (rev(c), 0, 0, 0)),
                  pl.BlockSpec((1, hh, 1, LANES), lambda c: (rev(c), 0, 0, 0)),
                  pl.BlockSpec((l, dblk), lambda c: (rev(c), 0))],
        out_specs=[pl.BlockSpec((l, 2 * dblk), lambda c: (rev(c), 0)),
                   pl.BlockSpec((l, dblk), lambda c: (rev(c), 0)), pl.BlockSpec((l, LANES), lambda c: (rev(c), 0)),
                   pl.BlockSpec((1, 8, l), lambda c: (rev(c), 0, 0))],
        out_shape=[pltpu.HBM((s, 2 * d_model), F32),
                   pltpu.HBM((s, d_model), BF16), pltpu.HBM((s, LANES), F32),
                   pltpu.HBM((nc, 8, l), F32)],
        scratch_shapes=[pltpu.VMEM((hh, dh, dh), F32), pltpu.VMEM((hh, 1, dh), F32)],
        compiler_params=_params(10 * hh * dh * dh * 4 + (16 << 20)),
    )(*[_hbm(a) for a in (mqk, mqk, zm, gcol, grow, cs, ns, ms, dhm)])


def _xa_fwd(zm, qcol, kv, gq, gk, d_model, *, name, tq=512):
    s = zm.shape[0]
    nm = kv.shape[0]
    dh = d_model // X_HEADS
    tq = _pick(s, (tq, 128, 64))
    scale = dh ** -0.5

    def body(q_ref, k_ref, v_ref, gq_ref, gk_ref, o_ref):
        qn = _rms_fwd(q_ref[...], gq_ref[...])
        kn = _rms_fwd(k_ref[...], gk_ref[...])
        lg = _dot(qn, kn, NT) * scale
        lg = lg - jnp.max(lg, axis=1, keepdims=True)
        p = jnp.exp(lg)
        p = p / jnp.sum(p, axis=1, keepdims=True)
        o_ref[...] = _dot(p, v_ref[...], NN).astype(o_ref.dtype)

    return pl.pallas_call(
        body, name=name, grid=(X_HEADS, s // tq),
        in_specs=[pl.BlockSpec((tq, dh), lambda h, i: (i, qcol // dh + h)), pl.BlockSpec((nm, dh), lambda h, i: (0, h)),
                  pl.BlockSpec((nm, dh), lambda h, i: (0, X_HEADS + h)),
                  pl.BlockSpec((1, dh), lambda h, i: (0, 0)), pl.BlockSpec((1, dh), lambda h, i: (0, 0))],
        out_specs=pl.BlockSpec((tq, dh), lambda h, i: (i, h)),
        out_shape=pltpu.HBM((s, d_model), BF16),
        compiler_params=_params(32 << 20),
    )(_hbm(zm), _hbm(kv), _hbm(kv), gq, gk)


def _xa_bwd(zm, qcol, kv, gq, gk, dy, d_model, *, name, tq=512):
    s = zm.shape[0]
    nm = kv.shape[0]
    dh = d_model // X_HEADS
    tq = _pick(s, (tq, 128, 64))
    nq = s // tq
    scale = dh ** -0.5

    def body(q_ref, k_ref, v_ref, gq_ref, gk_ref, do_ref, dq_ref, dkn_ref, dv_ref, dgq_ref):
        h, i = pl.program_id(0), pl.program_id(1)

        @pl.when(i == 0)
        def _():
            dkn_ref[...] = jnp.zeros_like(dkn_ref)
            dv_ref[...] = jnp.zeros_like(dv_ref)

        @pl.when((i == 0) & (h == 0))
        def _():
            dgq_ref[...] = jnp.zeros_like(dgq_ref)

        q = q_ref[...]
        qn = _rms_fwd(q, gq_ref[...])
        kn = _rms_fwd(k_ref[...], gk_ref[...])
        lg = _dot(qn, kn, NT) * scale
        lg = lg - jnp.max(lg, axis=1, keepdims=True)
        p = jnp.exp(lg)
        p = p / jnp.sum(p, axis=1, keepdims=True)
        do = do_ref[...]
        dv_ref[...] += _dot(p, do, TN)
        dp = _dot(do, v_ref[...], NT)
        dlg = p * (dp - jnp.sum(dp * p, axis=1, keepdims=True)) * scale
        dqn = _dot(dlg, kn, NN)
        dkn_ref[...] += _dot(dlg, qn, TN)
        dq, dgq = _rms_bwd(q, gq_ref[...], dqn)
        dq_ref[...] = dq.astype(dq_ref.dtype)
        dgq_ref[...] += jnp.sum(dgq, axis=0, keepdims=True)

    return pl.pallas_call(
        body, name=name, grid=(X_HEADS, nq),
        in_specs=[pl.BlockSpec((tq, dh), lambda h, i: (i, qcol // dh + h)), pl.BlockSpec((nm, dh), lambda h, i: (0, h)),
                  pl.BlockSpec((nm, dh), lambda h, i: (0, X_HEADS + h)),
                  pl.BlockSpec((1, dh), lambda h, i: (0, 0)), pl.BlockSpec((1, dh), lambda h, i: (0, 0)),
                  pl.BlockSpec((tq, dh), lambda h, i: (i, h))],
        out_specs=[pl.BlockSpec((tq, dh), lambda h, i: (i, h)), pl.BlockSpec((nm, dh), lambda h, i: (0, h)),
                   pl.BlockSpec((nm, dh), lambda h, i: (0, h)), pl.BlockSpec((1, dh), lambda h, i: (0, 0))],
        out_shape=[pltpu.HBM((s, d_model), BF16), pltpu.HBM((nm, d_model), F32),
                   pltpu.HBM((nm, d_model), F32), pltpu.HBM((1, dh), F32)],
        compiler_params=_params(32 << 20),
    )(_hbm(zm), _hbm(kv), _hbm(kv), gq, gk, _hbm(dy))


def _place():
    return lax.axis_index("x"), lax.axis_index("y"), lax.axis_index("c")


ANY = pl.BlockSpec(memory_space=pl.ANY)


def _allgather_two_level(big, small, *, name, chunk_rows=32):
    r = big.shape[0]
    half = r // 2
    nr = _pick(half, (chunk_rows, 32, 16))
    nq = half // nr

    def body(big_ref, small_ref, obig, osmall, send, recv, fsend, frecv, ssend, srecv, loc):
        x, y, c = _place()
        k = 2 * x + y
        chips = [(1 - x, y), (x, 1 - y), (1 - x, 1 - y)]
        own = [pltpu.make_async_copy(big_ref, obig.at[k], loc.at[0]),
               pltpu.make_async_copy(small_ref, osmall.at[k], loc.at[1])]
        for cp in own:
            cp.start()

        def rows(h, q):
            return pl.ds(pl.multiple_of(h * half + q * nr, nr), nr)

        def over_ici(j, q, slot, h):
            return pltpu.make_async_remote_copy(
                src_ref=big_ref.at[rows(h, q)], dst_ref=obig.at[slot, rows(h, q)], send_sem=send.at[nq * j + q],
                recv_sem=recv.at[nq * j + q], device_id=(chips[j][0], chips[j][1], c), device_id_type=MESH)

        def to_sibling(j, q, h):
            slot = 2 * chips[j][0] + chips[j][1]
            return pltpu.make_async_remote_copy(
                src_ref=obig.at[slot, rows(h, q)], dst_ref=obig.at[slot, rows(h, q)], send_sem=fsend.at[nq * j + q],
                recv_sem=frecv.at[nq * j + q], device_id=(x, y, 1 - c), device_id_type=MESH)

        def small_copy(j, slot):
            return pltpu.make_async_remote_copy(
                src_ref=small_ref, dst_ref=osmall.at[slot], send_sem=ssend.at[j], recv_sem=srecv.at[j],
                device_id=(chips[j][0], chips[j][1], c), device_id_type=MESH)

        for q in range(nq):
            for j in range(3):
                over_ici(j, q, k, c).start()
        for j in range(3):
            small_copy(j, k).start()
        for q in range(nq):
            for j in range(3):
                over_ici(j, q, 2 * chips[j][0] + chips[j][1], c).wait_recv()
                to_sibling(j, q, c).start()
        for q in range(nq):
            for j in range(3):
                to_sibling(j, q, 1 - c).wait_recv()
        for j in range(3):
            small_copy(j, 2 * chips[j][0] + chips[j][1]).wait_recv()
            small_copy(j, k).wait_send()
        for q in range(nq):
            for j in range(3):
                over_ici(j, q, k, c).wait_send()
                to_sibling(j, q, c).wait_send()
        for cp in own:
            cp.wait()

    return pl.pallas_call(
        body, name=name, in_specs=[ANY] * 2, out_specs=[ANY] * 2,
        out_shape=[pltpu.HBM((4,) + big.shape, big.dtype), pltpu.HBM((4,) + small.shape, small.dtype)],
        scratch_shapes=[pltpu.SemaphoreType.DMA((3 * nq,))] * 4
        + [pltpu.SemaphoreType.DMA((3,)), pltpu.SemaphoreType.DMA((3,)), pltpu.SemaphoreType.DMA((2,))],
    )(big, small)


HBM_SPEC = pl.BlockSpec(memory_space=pltpu.HBM)
SEM_SPEC = pl.BlockSpec(memory_space=pltpu.SEMAPHORE)
EFFECT = pltpu.SideEffectType.DATAFLOW_SIDE_EFFECTING


def _split_copies(kind, srcs, lands, send, recv):
    x, y, c = _place()
    if kind == "quarters":
        peers = [(1 - x, y, c), (x, 1 - y, c), (1 - x, 1 - y, c)]
    else:
        peers = [(x ^ ((j >> 2) & 1), y ^ ((j >> 1) & 1), c ^ (j & 1)) for j in range(1, 8)]
    npeer = len(peers)
    out = []
    for t in range(len(srcs)):
        for j, (px, py, pc) in enumerate(peers):
            if kind == "quarters":
                src, mine, theirs = srcs[t], 2 * x + y, 2 * px + py
            else:
                src, mine, theirs = srcs[t].at[2 * px + py, pc], 4 * x + 2 * y + c, 4 * px + 2 * py + pc
            mk = functools.partial(
                pltpu.make_async_remote_copy, src_ref=src, send_sem=send.at[npeer * t + j],
                recv_sem=recv.at[npeer * t + j], device_id=(px, py, pc), device_id_type=MESH)
            out.append((functools.partial(mk, dst_ref=lands[t].at[mine]),
                        functools.partial(mk, dst_ref=lands[t].at[theirs])))
    return out


def _split_start(kind, srcs, land_shapes, after, *, name):
    n = len(srcs)
    ncopies = n * (3 if kind == "quarters" else 7)

    def body(*refs):
        ins, lands = refs[:n], refs[n:2 * n]
        send, recv = refs[2 * n + 1], refs[2 * n + 2]
        token = refs[-1]
        for start, _ in _split_copies(kind, ins, lands, send, recv):
            start().start()
        token[...] = jnp.zeros_like(token)

    lands = [_hbm(lax.empty(shp, a.dtype)) for shp, a in zip(land_shapes, srcs)]
    res = pl.pallas_call(
        body, name=name, in_specs=[HBM_SPEC] * (2 * n) + [ANY],
        out_specs=[SEM_SPEC, SEM_SPEC] + [HBM_SPEC] * (2 * n) + [pl.BlockSpec(memory_space=pltpu.VMEM)],
        out_shape=[pltpu.SemaphoreType.DMA((ncopies,)), pltpu.SemaphoreType.DMA((ncopies,))]
        + [pltpu.HBM(a.shape, a.dtype) for a in srcs] + [pltpu.HBM(shp, a.dtype) for shp, a in zip(land_shapes, srcs)]
        + [jax.ShapeDtypeStruct((8, LANES), F32)],
        input_output_aliases={i: 2 + i for i in range(2 * n)},
        compiler_params=pltpu.CompilerParams(has_side_effects=EFFECT),
    )(*[_hbm(a) for a in srcs], *lands, after)
    return res[0], res[1], list(res[2:2 + n]), list(res[2 + n:2 + 2 * n]), res[-1]


def _split_wait(kind, send, recv, srcs, lands, after, *, name):
    n = len(srcs)

    def body(*refs):
        ins, lnd = refs[:n], refs[n:2 * n]
        snd, rcv = refs[2 * n], refs[2 * n + 1]
        for start, arrive in _split_copies(kind, ins, lnd, snd, rcv):
            start().wait_send()
            arrive().wait_recv()

    res = pl.pallas_call(
        body, name=name, in_specs=[HBM_SPEC] * (2 * n) + [SEM_SPEC, SEM_SPEC] + [ANY] * len(after),
        out_specs=[HBM_SPEC] * (2 * n),
        out_shape=[pltpu.HBM(a.shape, a.dtype) for a in srcs] + [pltpu.HBM(a.shape, a.dtype) for a in lands],
        input_output_aliases={i: i for i in range(2 * n)},
        compiler_params=pltpu.CompilerParams(has_side_effects=EFFECT),
    )(*srcs, *lands, send, recv, *after)
    return list(res[n:])


def _sum8(parts, *, name):
    _, r, c = parts.shape
    t = _pick(r, (128, 64, 32, 16, 8))

    def body(p_ref, o_ref):
        acc = p_ref[0].astype(F32)
        for k in range(1, 8):
            acc = acc + p_ref[k].astype(F32)
        o_ref[...] = acc

    return pl.pallas_call(
        body, name=name, grid=(r // t,), in_specs=[pl.BlockSpec((8, t, c), lambda i: (0, i, 0))],
        out_specs=pl.BlockSpec((t, c), lambda i: (i, 0)), out_shape=pltpu.HBM((r, c), F32),
        compiler_params=_params(2 * 8 * t * c * 2 + 6 * t * c * 4 + (4 << 20)),
    )(_hbm(parts))


def _swap_halves(halves, *, name, chunk_bytes=512 * 1024):
    n = len(halves)
    items = []
    for t, a in enumerate(halves):
        r = a.shape[0]
        k = 1
        while _nbytes(a.shape, a.dtype) // k > chunk_bytes and r % (2 * k) == 0 and (r // (2 * k)) % 8 == 0:
            k *= 2
        items += [(t, q * (r // k), r // k) for q in range(k)]
    m = len(items)

    def body(*refs):
        ins, outs = refs[:n], refs[n:2 * n]
        sbuf, rbuf = refs[2 * n:3 * n], refs[3 * n:4 * n]
        send, recv, loc_own, loc_in, loc_out = refs[4 * n:]
        x, y, c = _place()
        local, stage = [], []
        for t in range(n):
            cp = pltpu.make_async_copy(ins[t], outs[t].at[c], loc_own.at[t])
            cp.start()
            local.append(cp)
        for q, (t, r0, nr) in enumerate(items):
            cp = pltpu.make_async_copy(ins[t].at[pl.ds(r0, nr)], sbuf[t].at[pl.ds(r0, nr)], loc_in.at[q])
            cp.start()
            stage.append(cp)

        def copy(q):
            t, r0, nr = items[q]
            return pltpu.make_async_remote_copy(
                src_ref=sbuf[t].at[pl.ds(r0, nr)], dst_ref=rbuf[t].at[pl.ds(r0, nr)], send_sem=send.at[q],
                recv_sem=recv.at[q], device_id=(x, y, 1 - c), device_id_type=MESH)

        for q in range(m):
            stage[q].wait()
            copy(q).start()
        for q, (t, r0, nr) in enumerate(items):
            copy(q).wait_recv()
            cp = pltpu.make_async_copy(rbuf[t].at[pl.ds(r0, nr)], outs[t].at[1 - c, pl.ds(r0, nr)], loc_out.at[q])
            cp.start()
            local.append(cp)
        for q in range(m):
            copy(q).wait_send()
        for cp in local:
            cp.wait()

    stage_bytes = 2 * sum(_nbytes(a.shape, a.dtype) for a in halves)
    return pl.pallas_call(
        body, name=name, in_specs=[ANY] * n, out_specs=[ANY] * n,
        out_shape=[pltpu.HBM((2,) + a.shape, a.dtype) for a in halves],
        scratch_shapes=[pltpu.VMEM(a.shape, a.dtype) for a in halves] * 2
        + [pltpu.SemaphoreType.DMA((m,)), pltpu.SemaphoreType.DMA((m,)), pltpu.SemaphoreType.DMA((n,)),
           pltpu.SemaphoreType.DMA((m,)), pltpu.SemaphoreType.DMA((m,))],
        compiler_params=_params(stage_bytes + (4 << 20)),
    )(*halves)


def _allreduce_small(p, after, *, name):
    r = p.shape[0]

    def body(p_ref, after_ref, o_ref, buf, send, recv):
        x, y, c = _place()
        me = 4 * x + 2 * y + c
        peers = [(x ^ ((j >> 2) & 1), y ^ ((j >> 1) & 1), c ^ (j & 1)) for j in range(1, 8)]

        def copy(j, slot):
            return pltpu.make_async_remote_copy(
                src_ref=p_ref, dst_ref=buf.at[slot], send_sem=send.at[j], recv_sem=recv.at[j],
                device_id=peers[j], device_id_type=MESH)

        for j in range(7):
            copy(j, me).start()
        buf[me] = p_ref[...]
        for j in range(7):
            px, py, pc = peers[j]
            copy(j, 4 * px + 2 * py + pc).wait_recv()
        for j in range(7):
            copy(j, me).wait_send()
        acc = buf[0]
        for k in range(1, 8):
            acc = acc + buf[k]
        o_ref[...] = acc

    vspec = pl.BlockSpec(memory_space=pltpu.VMEM)
    return pl.pallas_call(
        body, name=name, in_specs=[vspec, ANY], out_specs=vspec, out_shape=jax.ShapeDtypeStruct((r, LANES), F32),
        scratch_shapes=[pltpu.VMEM((8, r, LANES), F32), pltpu.SemaphoreType.DMA((7,)), pltpu.SemaphoreType.DMA((7,))],
    )(p, after)


def _adamw_fn(w, g, m, v):
    m = ADAM_B1 * m + (1.0 - ADAM_B1) * g
    v = ADAM_B2 * v + (1.0 - ADAM_B2) * (g * g)
    m_hat = m / (1.0 - ADAM_B1 ** ADAM_STEP)
    v_hat = v / (1.0 - ADAM_B2 ** ADAM_STEP)
    delta = -ADAM_LR * (m_hat / (jnp.sqrt(v_hat) + ADAM_EPS) + ADAM_WD * w)
    return delta, m, v


def _adamw(w, g, m, v, *, name):
    c = w.shape[1]
    return _rowwise(_adamw_fn, [w, g, m, v], [], [(c, F32)] * 3, name=name, tr=128)


def _pack(vecs, rows):
    flat = jnp.concatenate([a.reshape(-1).astype(F32) for a in vecs])
    return jnp.pad(flat, (0, rows * LANES - flat.shape[0])).reshape(rows, LANES)


def _unpack(p, like):
    flat, out, o = p.reshape(-1), [], 0
    for a in like:
        out.append(flat[o:o + a.size].reshape(a.shape))
        o += a.size
    return out


def kernel(x, mem, g_mix, w_in, b_if, b_gate, conv_w, conv_b, ml_norm_g, g_mem, w_mem_kv, q_norm_g, k_norm_g, w_sb_proj, w_ml_proj, w_x_proj, w_out, g_mlp, w_ff1, w_ff2, loss_target, m_g_mix, m_w_in, m_b_if, m_b_gate, m_conv_w, m_conv_b, m_ml_norm_g, m_g_mem, m_w_mem_kv, m_q_norm_g, m_k_norm_g, m_w_sb_proj, m_w_ml_proj, m_w_x_proj, m_w_out, m_g_mlp, m_w_ff1, m_w_ff2, v_g_mix, v_w_in, v_b_if, v_b_gate, v_conv_w, v_conv_b, v_ml_norm_g, v_g_mem, v_w_mem_kv, v_q_norm_g, v_k_norm_g, v_w_sb_proj, v_w_ml_proj, v_w_x_proj, v_w_out, v_g_mlp, v_w_ff1, v_w_ff2):
    _, s, d = x.shape
    nm = mem.shape[1]
    n_in = 4 * w_in.shape[2]
    dff = 4 * w_ff1.shape[2]
    sbh = d // SB_HD
    hh = ML_HEADS
    dh = d // hh
    nc = s // CHUNK
    assert n_in == 11 * d + 2 * hh and d % (2 * LANES) == 0 and s % LANES == 0
    x2, mem2, tgt = x[0], mem[0], loss_target[0]

    k4 = 2 * lax.axis_index("x") + lax.axis_index("y")
    me = 2 * k4 + lax.axis_index("c")
    g_first = _allgather_two_level(w_in[0].astype(BF16), conv_w[0], name="gather_w_in")
    later = [a[0].astype(BF16) for a in (w_mem_kv, w_sb_proj, w_ml_proj, w_x_proj, w_out, w_ff1, w_ff2)]
    gw_send, gw_recv, gw_src, gw_land, gw_token = _split_start(
        "quarters", later, [(4,) + a.shape for a in later], g_first[0], name="gather_rest_start")
    cols = lambda a: a.transpose(1, 0, 2).reshape(a.shape[1], 4 * a.shape[2])
    rws = lambda a: a.reshape(4 * a.shape[1], a.shape[2])
    w_in_f = cols(g_first[0])
    w_main = jnp.concatenate([w_in_f[:, :7 * d], w_in_f[:, 7 * d + 2 * hh:]], axis=1)
    w_if = jnp.pad(w_in_f[:, 7 * d:7 * d + 2 * hh], ((0, 0), (0, LANES - 2 * hh)))
    conv_wf = cols(g_first[1])
    b_if_p = jnp.pad(b_if, ((0, 0), (0, LANES - 2 * hh)))

    (hn,) = _rowwise(_rms_fwd, [x2], [g_mix], [(d, BF16)], name="norm_in")
    zm = _mm(hn, w_main, after=gw_token, name="proj_in")
    zif = _mm(hn, w_if, name="proj_if")
    y_sb, a_sb = _sb_fwd(zm, sbh, name="sb_fwd")

    def gate_fn(z, b):
        pre = z + b
        lane = lax.broadcasted_iota(jnp.int32, pre.shape, 1)
        return jnp.where(lane < hh, pre, -_softplus(-pre))

    (gcol,) = _rowwise(gate_fn, [zif], [b_if_p], [(LANES, F32)], name="ml_gates")
    grow = gcol[:, :8].T.reshape(8, nc, CHUNK).transpose(1, 0, 2)
    mqk = _conv_fwd(zm, 3 * d, 2 * d, conv_wf, conv_b, name="conv_fwd")
    hm, cst, nst, mst = _ml_fwd(mqk, zm, 5 * d, gcol, grow, d, name="ml_fwd")

    def mlout_fn(hv, o, g):
        ys = [_rms_fwd(hv[:, k * dh:(k + 1) * dh], g[:, k * dh:(k + 1) * dh]) for k in range(hh)]
        return jnp.concatenate(ys, axis=1) * _sigmoid(o)

    (y_ml,) = _rowwise(mlout_fn, [hm, (zm, d, 6)], [ml_norm_g], [(d, BF16)], name="ml_out")
    gw_land = _split_wait("quarters", gw_send, gw_recv, gw_src, gw_land, [y_ml, y_sb], name="gather_rest_wait")
    gw = [lax.dynamic_update_index_in_dim(ld, a, k4, 0) for ld, a in zip(gw_land, later)]
    w_kv, w_sbp, w_mlp, w_xp, w_o, w_f1, w_f2 = (cols(gw[0]), rws(gw[1]), rws(gw[2]), rws(gw[3]), rws(gw[4]),
                                                 cols(gw[5]), rws(gw[6]))
    (memn,) = _rowwise(_rms_fwd, [mem2], [g_mem], [(d, BF16)], name="norm_mem")
    kv = _mm(memn, w_kv, name="proj_kv")
    y_x = _xa_fwd(zm, 7 * d, kv, q_norm_g, k_norm_g, d, name="xa_fwd")
    p_sb = _mm(y_sb, w_sbp, name="proj_sb")
    p_ml = _mm(y_ml, w_mlp, name="proj_ml")
    p_x = _mm(y_x, w_xp, name="proj_x")

    def merge_fn(a, b, c, g0, g1, g2, bg):
        return (_sigmoid(g0 + bg[:, :d]) * a + _sigmoid(g1 + bg[:, d:2 * d]) * b + _sigmoid(g2 + bg[:, 2 * d:]) * c)

    gate_cols = [(zm, d, 8), (zm, d, 9), (zm, d, 10)]
    (mixed,) = _rowwise(merge_fn, [p_sb, p_ml, p_x] + gate_cols, [b_gate], [(d, BF16)], name="merge")
    x1 = _mm(mixed, w_o, tiles=[x2], name="proj_out")
    (h2,) = _rowwise(_rms_fwd, [x1], [g_mlp], [(d, BF16)], name="norm_mlp")
    u, act = _mm(h2, w_f1, post=lambda r: (r, jnp.square(jnp.maximum(r, 0.0))), out_dtype=(F32, BF16), name="ff1")
    dy = _mm(act, w_f2, tiles=[x1, tgt], post=lambda r, xv, tv: (r + xv - tv) * (1.0 / d), name="ff2")
    (loss_cols,) = _rowwise(lambda g: (jnp.sum(g * g, axis=0, keepdims=True) * (0.5 * d),), [dy], [], [], [d],
                            name="loss")

    du = _mm(dy, w_f2, tb=True, tiles=[u], post=lambda r, uv: r * 2.0 * jnp.maximum(uv, 0.0), out_dtype=BF16,
             name="ff2_dx")
    dw_f2 = _mm(act, dy, ta=True, name="ff2_dw")
    dw_f1 = _mm(h2, du, ta=True, name="ff1_dw")
    dh2 = _mm(du, w_f1, tb=True, name="ff1_dx")

    def norm_bwd_fn(xv, dyv, res, g):
        dx, dg = _rms_bwd(xv, g, dyv)
        return dx + res, jnp.sum(dg, axis=0, keepdims=True)

    dx1, dg_mlp = _rowwise(norm_bwd_fn, [x1, dh2, dy], [g_mlp], [(d, F32)], [d], name="norm_mlp_bwd")
    dmixed = _mm(dx1, w_o, tb=True, name="proj_out_dx")
    dw_o = _mm(mixed, dx1, ta=True, name="proj_out_dw")

    def merge_bwd_fn(dm, a, b, c, g0, g1, g2, bg):
        outs, dgs = [], []
        for p, g, k in ((a, g0, 0), (b, g1, 1), (c, g2, 2)):
            sg = _sigmoid(g + bg[:, k * d:(k + 1) * d])
            outs.append(dm * sg)
            dgs.append(dm * p * sg * (1.0 - sg))
        dgate = jnp.concatenate(dgs, axis=1)
        return (*outs, dgate, jnp.sum(dgate, axis=0, keepdims=True))

    dp_sb, dp_ml, dp_x, dgate, db_gate = _rowwise(
        merge_bwd_fn, [dmixed, p_sb, p_ml, p_x] + gate_cols, [b_gate], [(d, BF16)] * 3 + [(3 * d, BF16)], [3 * d],
        name="merge_bwd", tr=128)
    dw_sbp = _mm(y_sb, dp_sb, ta=True, name="proj_sb_dw")
    dw_mlp = _mm(y_ml, dp_ml, ta=True, name="proj_ml_dw")
    dw_xp = _mm(y_x, dp_x, ta=True, name="proj_x_dw")
    dy_sb = _mm(dp_sb, w_sbp, tb=True, out_dtype=BF16, name="proj_sb_dx")
    dy_ml = _mm(dp_ml, w_mlp, tb=True, name="proj_ml_dx")
    dy_x = _mm(dp_x, w_xp, tb=True, out_dtype=BF16, name="proj_x_dx")

    dxq, dkn, dxv, dg_qn = _xa_bwd(zm, 7 * d, kv, q_norm_g, k_norm_g, dy_x, d, name="xa_bwd")

    def knorm_bwd_fn(kvv, dknv, dvv, g):
        dks, dgs = [], []
        for k in range(X_HEADS):
            sl = slice(k * dh, (k + 1) * dh)
            dk, dg = _rms_bwd(kvv[:, sl], g, dknv[:, sl])
            dks.append(dk)
            dgs.append(jnp.sum(dg, axis=0, keepdims=True))
        return jnp.concatenate(dks + [dvv], axis=1), dgs[0] + dgs[1] + dgs[2] + dgs[3]

    dkv, dg_kn = _rowwise(knorm_bwd_fn, [(kv, d, 0), dkn, dxv], [k_norm_g], [(2 * d, BF16)], [dh], name="xa_knorm_bwd")
    dw_kv = _mm(memn, dkv, ta=True, name="proj_kv_dw")
    dmemn = _mm(dkv, w_kv, tb=True, name="proj_kv_dx")

    def gmem_fn(mv, dv_, g):
        _, dg = _rms_bwd(mv, g, dv_)
        return (jnp.sum(dg, axis=0, keepdims=True),)

    (dg_mem,) = _rowwise(gmem_fn, [mem2, dmemn], [g_mem], [], [d], name="norm_mem_bwd")

    uncols = lambda a: a.reshape(a.shape[0], 4, a.shape[1] // 4).transpose(1, 0, 2)
    unrws = lambda a: a.reshape(4, a.shape[0] // 4, a.shape[1])
    to_parts = lambda q: q.astype(BF16).reshape(4, 2, q.shape[1] // 2, q.shape[2])
    early = [to_parts(q) for q in (uncols(dw_kv), unrws(dw_sbp), unrws(dw_mlp), unrws(dw_xp), unrws(dw_o),
                                   uncols(dw_f1), unrws(dw_f2))]
    ge_send, ge_recv, ge_src, ge_land, ge_token = _split_start(
        "grads", early, [(8,) + a.shape[2:] for a in early], dg_mem, name="exchange_early_start")

    dsq, dsk, dsv = _sb_bwd(zm, dy_sb, a_sb, ge_token, sbh, name="sb_bwd")

    def mlout_bwd_fn(dyv, hv, o, g):
        sg = _sigmoid(o)
        dn = dyv * sg
        dxs, dgs, ys = [], [], []
        for k in range(hh):
            sl = slice(k * dh, (k + 1) * dh)
            ys.append(_rms_fwd(hv[:, sl], g[:, sl]))
            dxk, dgk = _rms_bwd(hv[:, sl], g[:, sl], dn[:, sl])
            dxs.append(dxk)
            dgs.append(dgk)
        do = dyv * jnp.concatenate(ys, axis=1) * sg * (1.0 - sg)
        return jnp.concatenate(dxs, axis=1), do, jnp.sum(jnp.concatenate(dgs, axis=1), axis=0, keepdims=True)

    dhm, dmlo, dg_mln = _rowwise(mlout_bwd_fn, [dy_ml, hm, (zm, d, 6)], [ml_norm_g], [(d, F32), (d, BF16)], [d],
                                 name="ml_out_bwd")
    dmqk, dmlv, dgc, dgr = _ml_bwd(mqk, zm, 5 * d, gcol, grow, cst, nst, mst, dhm, d, name="ml_bwd")
    dmlqk, dconv_w, dconv_b = _conv_bwd(zm, 3 * d, 2 * d, conv_wf, conv_b, dmqk, name="conv_bwd")
    dgr_t = jnp.pad(dgr.transpose(1, 0, 2).reshape(8, s).T, ((0, 0), (0, LANES - 8)))

    def gate_bwd_fn(a, b, z, bias):
        tot = a + b
        rows_t = tot.shape[0]
        r = lax.broadcasted_iota(jnp.int32, (rows_t, rows_t), 0)
        c = lax.broadcasted_iota(jnp.int32, (rows_t, rows_t), 1)
        sh = CHUNK.bit_length() - 1
        same_chunk = jnp.right_shift(r, sh) == jnp.right_shift(c, sh)
        dlf = _u01dot(((c >= r) & same_chunk).astype(BF16), tot)
        lane = lax.broadcasted_iota(jnp.int32, tot.shape, 1)
        dz = jnp.where(lane < hh, tot, jnp.where(lane < 2 * hh, dlf * _sigmoid(-(z + bias)), 0.0))
        return dz, jnp.sum(dz, axis=0, keepdims=True)

    dzif, db_if_p = _rowwise(gate_bwd_fn, [dgc, dgr_t, zif], [b_if_p], [(LANES, BF16)], [LANES], name="ml_gates_bwd",
                             tr=8 * CHUNK)
    dzm = jnp.concatenate([dsq, dsk, dsv, dmlqk, dmlv, dmlo, dxq, dgate], axis=1)
    dw_main = _mm(hn, dzm, ta=True, out_dtype=BF16, name="proj_in_dw")
    dw_if = _mm(hn, dzif, ta=True, out_dtype=BF16, name="proj_if_dw")
    dw_in = jnp.concatenate([dw_main[:, :7 * d], dw_if[:, :2 * hh], dw_main[:, 7 * d:]], axis=1)
    late = [to_parts(uncols(dw_in))]
    gl_send, gl_recv, gl_src, gl_land, gl_token = _split_start(
        "grads", late, [(8,) + a.shape[2:] for a in late], dw_if, name="exchange_late_start")
    dhn = _mm(dzm, w_main, tb=True, after=gl_token, name="proj_in_dx")
    dhn = _mm(dzif, w_if, tb=True, tiles=[dhn], name="proj_if_dx")
    dx, dg_mix = _rowwise(norm_bwd_fn, [x2, dhn, dx1], [g_mix], [(d, F32)], [d], name="norm_in_bwd")

    own = lambda p: lax.dynamic_index_in_dim(lax.dynamic_index_in_dim(p, k4, 0, keepdims=False),
                                             lax.axis_index("c"), 0, keepdims=False)

    def finish(tag, send, recv, src, land, parts, after, ws, ms, vs):
        land = _split_wait("grads", send, recv, src, land, after, name=f"exchange_{tag}_wait")
        got = [lax.dynamic_update_index_in_dim(ld, own(p), me, 0) for ld, p in zip(land, parts)]
        halves = [_sum8(r, name=f"sum_grads_{tag}{i}") for i, r in enumerate(got)]
        both = _swap_halves(halves, name=f"swap_halves_{tag}")
        gs = [b.reshape(2 * b.shape[1], b.shape[2]) for b in both]
        return gs, [_adamw(w, g, m, v, name=f"adamw_{tag}{i}") for i, (w, g, m, v) in enumerate(zip(ws, gs, ms, vs))]

    first = lambda arrs: [a[0] for a in arrs]
    g_early, out_early = finish(
        "early", ge_send, ge_recv, ge_src, ge_land, early, [dx],
        first([w_mem_kv, w_sb_proj, w_ml_proj, w_x_proj, w_out, w_ff1, w_ff2]),
        first([m_w_mem_kv, m_w_sb_proj, m_w_ml_proj, m_w_x_proj, m_w_out, m_w_ff1, m_w_ff2]),
        first([v_w_mem_kv, v_w_sb_proj, v_w_ml_proj, v_w_x_proj, v_w_out, v_w_ff1, v_w_ff2]))
    g_late, out_late = finish(
        "late", gl_send, gl_recv, gl_src, gl_land, late, [o[0] for o in out_early],
        first([w_in]), first([m_w_in]), first([v_w_in]))
    g_big = [g[None] for g in g_late + g_early]
    big_out = [[o[None] for o in outs] for outs in out_late + out_early]

    small_g = [dg_mix, db_if_p[:, :2 * hh], db_gate, dconv_w, dconv_b, dg_mln, dg_mem, dg_qn, dg_kn, dg_mlp,
               jnp.sum(loss_cols).reshape(1, 1)]
    n_small = sum(a.size for a in small_g)
    rows = -(-n_small // (8 * LANES)) * 8
    g_small = _unpack(_allreduce_small(_pack(small_g, rows), out_late[0][0], name="allreduce_small"), small_g)
    loss = g_small[-1].reshape(())
    qw = conv_w.shape[2]
    g_conv_w = lax.dynamic_slice_in_dim(g_small[3], k4 * qw, qw, axis=1)
    g_small_w = [g_small[0], g_small[1], g_small[2], g_conv_w] + g_small[4:10]
    sm_w = [g_mix, b_if, b_gate, conv_w[0], conv_b, ml_norm_g, g_mem, q_norm_g, k_norm_g, g_mlp]
    sm_m = [m_g_mix, m_b_if, m_b_gate, m_conv_w[0], m_conv_b, m_ml_norm_g, m_g_mem, m_q_norm_g, m_k_norm_g, m_g_mlp]
    sm_v = [v_g_mix, v_b_if, v_b_gate, v_conv_w[0], v_conv_b, v_ml_norm_g, v_g_mem, v_q_norm_g, v_k_norm_g, v_g_mlp]
    n_sw = sum(a.size for a in sm_w)
    rows_w = -(-n_sw // (8 * LANES)) * 8
    sm_out = _adamw(_pack(sm_w, rows_w), _pack(g_small_w, rows_w), _pack(sm_m, rows_w), _pack(sm_v, rows_w),
                    name="adamw_small")
    sm_delta, sm_newm, sm_newv = [_unpack(p, sm_w) for p in sm_out]

    order = ["g_mix", "w_in", "b_if", "b_gate", "conv_w", "conv_b", "ml_norm_g", "g_mem", "w_mem_kv", "q_norm_g",
             "k_norm_g", "w_sb_proj", "w_ml_proj", "w_x_proj", "w_out", "g_mlp", "w_ff1", "w_ff2"]
    small_names = ["g_mix", "b_if", "b_gate", "conv_w", "conv_b", "ml_norm_g", "g_mem", "q_norm_g", "k_norm_g", "g_mlp"]
    big_names = ["w_in", "w_mem_kv", "w_sb_proj", "w_ml_proj", "w_x_proj", "w_out", "w_ff1", "w_ff2"]
    grads, deltas, new_m, new_v = {}, {}, {}, {}
    for i, nme in enumerate(small_names):
        shp = sm_w[i].shape if nme != "conv_w" else conv_w.shape
        grads[nme] = g_small_w[i].reshape(shp)
        deltas[nme], new_m[nme], new_v[nme] = (sm_delta[i].reshape(shp), sm_newm[i].reshape(shp),
                                               sm_newv[i].reshape(shp))
    for i, nme in enumerate(big_names):
        grads[nme] = g_big[i]
        deltas[nme], new_m[nme], new_v[nme] = big_out[i]
    return (loss, dx[None], *[grads[k] for k in order], *[deltas[k] for k in order], *[new_m[k] for k in order],
            *[new_v[k] for k in order])
```

```python
import functools

import jax
import jax.numpy as jnp
from jax import lax
from jax.experimental import pallas as pl
from jax.experimental.pallas import tpu as pltpu

F32 = jnp.float32
BF16 = jnp.bfloat16
MESH = pl.DeviceIdType.MESH

EPS = 1e-6
SB_HD = 128
SB_SLOTS = 8
ML_HEADS = 4
X_HEADS = 4
CHUNK = 64
CONV_W = 4
LANES = 128
ADAM_LR = 0.001
ADAM_B1 = 0.9
ADAM_B2 = 0.999
ADAM_EPS = 1e-08
ADAM_WD = 0.01
ADAM_STEP = 10
VMEM_CAP = 56 * 1024 * 1024
NEG = -1e30

NT = (((1,), (1,)), ((), ()))
NN = (((1,), (0,)), ((), ()))
TN = (((0,), (0,)), ((), ()))


def _dot(a, b, dn=NN):
    return lax.dot_general(a.astype(BF16), b.astype(BF16), dn, preferred_element_type=F32)


def _dot01(x, u, dn=NN):
    hi = x.astype(BF16)
    lo = (x - hi.astype(F32)).astype(BF16)
    return (lax.dot_general(hi, u, dn, preferred_element_type=F32)
            + lax.dot_general(lo, u, dn, preferred_element_type=F32))


def _u01dot(u, x):
    hi = x.astype(BF16)
    lo = (x - hi.astype(F32)).astype(BF16)
    return (lax.dot_general(u, hi, NN, preferred_element_type=F32)
            + lax.dot_general(u, lo, NN, preferred_element_type=F32))


def _pick(n, cands):
    for c in cands:
        if c <= n and n % c == 0:
            return c
    return n


def _nbytes(shape, dtype):
    n = 1
    for s in shape:
        n *= s
    return n * jnp.dtype(dtype).itemsize


def _params(vmem_bytes):
    return pltpu.CompilerParams(vmem_limit_bytes=int(min(VMEM_CAP, max(vmem_bytes, 16 * 1024 * 1024))))


def _hbm(a):
    return pltpu.with_memory_space_constraint(a, pltpu.HBM)


def _softplus(z):
    return jnp.maximum(z, 0.0) + jnp.log(1.0 + jnp.exp(-jnp.abs(z)))


def _sigmoid(z):
    return 1.0 / (1.0 + jnp.exp(-z))


def _rms_fwd(xv, g):
    r = lax.rsqrt(jnp.mean(xv * xv, axis=-1, keepdims=True) + EPS)
    return xv * r * g


def _rms_bwd(xv, g, dy):
    r = lax.rsqrt(jnp.mean(xv * xv, axis=-1, keepdims=True) + EPS)
    xh = xv * r
    dxh = dy * g
    dx = r * (dxh - xh * jnp.mean(dxh * xh, axis=-1, keepdims=True))
    return dx, dy * xh


def _mm(a, b, *, name, ta=False, tb=False, tiles=(), post=None, out_dtype=F32, bm=1024, bn=1024, bk=1024, after=None):
    m, k = (a.shape[1], a.shape[0]) if ta else a.shape
    n = b.shape[0] if tb else b.shape[1]
    tm = _pick(m, (bm, 512, 256, 128))
    tn = _pick(n, (bn, 512, 256, 128))
    tk = _pick(k, (bk, 512, 256, 128))
    nk = k // tk
    if (m // tm) * (n // tn) * nk < 8 and tm % 256 == 0:
        tm //= 2
    dn = (((0 if ta else 1,), (1 if tb else 0,)), ((), ()))
    dts = out_dtype if isinstance(out_dtype, tuple) else (out_dtype,)
    nt, no = len(tiles), len(dts)
    if post is None:
        post = lambda r, *ts: sum((t.astype(F32) for t in ts), r)

    def body(*refs):
        a_ref, b_ref = refs[:2]
        t_refs = refs[2:2 + nt]
        o_refs = refs[2 + nt + (after is not None):2 + nt + (after is not None) + no]
        part = lax.dot_general(a_ref[...].astype(BF16), b_ref[...].astype(BF16), dn, preferred_element_type=F32)

        def finish(r):
            res = post(r, *[t[...] for t in t_refs])
            res = res if isinstance(res, tuple) else (res,)
            for o, v in zip(o_refs, res):
                o[...] = v.astype(o.dtype)

        if nk == 1:
            finish(part)
        else:
            acc_ref = refs[-1]
            kk = pl.program_id(2)

            @pl.when(kk == 0)
            def _():
                acc_ref[...] = part

            @pl.when(kk > 0)
            def _():
                acc_ref[...] += part

            @pl.when(kk == nk - 1)
            def _():
                finish(acc_ref[...])

    a_spec = pl.BlockSpec((tk, tm), lambda i, j, q: (q, i)) if ta else pl.BlockSpec((tm, tk), lambda i, j, q: (i, q))
    b_spec = pl.BlockSpec((tn, tk), lambda i, j, q: (j, q)) if tb else pl.BlockSpec((tk, tn), lambda i, j, q: (q, j))
    o_spec = pl.BlockSpec((tm, tn), lambda i, j, q: (i, j))
    ins, specs = [_hbm(a), _hbm(b)] + [_hbm(t) for t in tiles], [a_spec, b_spec] + [o_spec] * nt
    vm = 2 * (_nbytes((tm, tk), a.dtype) + _nbytes((tk, tn), b.dtype)) + 3 * _nbytes((tm, tn), F32) \
        + _nbytes((tm, tk), BF16) + _nbytes((tk, tn), BF16) \
        + 2 * sum(_nbytes((tm, tn), t.dtype) for t in tiles) + 2 * sum(_nbytes((tm, tn), dt) for dt in dts)
    if after is not None:
        ins.append(after)
        specs.append(ANY)
    res = pl.pallas_call(
        body, name=name, grid=(m // tm, n // tn, nk), in_specs=specs, out_specs=[o_spec] * no,
        out_shape=[pltpu.HBM((m, n), dt) for dt in dts], scratch_shapes=[pltpu.VMEM((tm, tn), F32)] if nk > 1 else [],
        compiler_params=_params(vm + (4 << 20)),
    )(*ins)
    return res[0] if no == 1 else tuple(res)


def _rowwise(fn, rows, consts, outs, reds=(), *, name, tr=256, temps=6):
    rows = [r if isinstance(r, tuple) else (r, r.shape[1], 0) for r in rows]
    nrows = rows[0][0].shape[0]
    t = _pick(nrows, (tr, 128, 64, 32, 16, 8))
    nr, nc, no = len(rows), len(consts), len(outs)

    def body(*refs):
        rin, cin = refs[:nr], refs[nr:nr + nc]
        oref, rref = refs[nr + nc:nr + nc + no], refs[nr + nc + no:]
        res = fn(*[r[...] for r in rin], *[c[...] for c in cin])
        if not isinstance(res, (tuple, list)):
            res = (res,)
        for o, v in zip(oref, res[:no]):
            o[...] = v.astype(o.dtype)
        if rref:
            @pl.when(pl.program_id(0) == 0)
            def _():
                for r in rref:
                    r[...] = jnp.zeros_like(r)

            for r, v in zip(rref, res[no:]):
                r[...] += v

    in_specs = [pl.BlockSpec((t, w), functools.partial(lambda i, ci: (i, ci), ci=ci)) for (_, w, ci) in rows]
    in_specs += [pl.BlockSpec(c.shape, functools.partial(lambda i, nd: (0,) * nd, nd=c.ndim)) for c in consts]
    out_specs = [pl.BlockSpec((t, w), lambda i: (i, 0)) for (w, _) in outs]
    out_specs += [pl.BlockSpec((1, w), lambda i: (0, 0)) for w in reds]
    out_shape = [pltpu.HBM((nrows, w), dt) for (w, dt) in outs]
    out_shape += [jax.ShapeDtypeStruct((1, w), F32) for w in reds]
    widest = max([w for (_, w, _) in rows] + [w for (w, _) in outs])
    vm = 2 * sum(_nbytes((t, w), a.dtype) for (a, w, _) in rows) + 2 * sum(_nbytes((t, w), dt) for (w, dt) in outs)
    vm += temps * _nbytes((t, widest), F32) + (2 << 20)
    res = pl.pallas_call(
        body, name=name, grid=(nrows // t,), in_specs=in_specs, out_specs=out_specs, out_shape=out_shape,
        compiler_params=_params(vm),
    )(*[_hbm(a) for (a, _, _) in rows], *consts)
    return list(res)


def _sb_tiles(s, tq, tk):
    tq = _pick(s, (tq, 256, 128))
    tk = _pick(tq, (tk, 128))
    return tq, tk, tq // tk


def _sb_fwd(zm, heads, *, name, tq=512, tk=256):
    s = zm.shape[0]
    tq, tk, nd = _sb_tiles(s, tq, tk)
    scale = SB_HD ** -0.5

    def body(q_ref, k_ref, v_ref, o_ref, a_out, stage, sem):
        h, i = pl.program_id(0), pl.program_id(1)
        qb = (q_ref[...] * scale).astype(BF16)
        r = lax.broadcasted_iota(jnp.int32, (tq, tk), 0)
        c = lax.broadcasted_iota(jnp.int32, (tq, tk), 1)
        ur = lax.broadcasted_iota(jnp.int32, (tk, tk), 0)
        uc = lax.broadcasted_iota(jnp.int32, (tk, tk), 1)
        usuf = (ur > uc).astype(BF16)

        def out_copy(slot, j):
            return pltpu.make_async_copy(stage.at[slot], a_out.at[h, i, j], sem.at[slot])

        def tile(j, carry, causal, slot, reuse):
            acc, cl = carry
            if reuse is True:
                out_copy(slot, 0).wait()
            elif reuse is not None:
                @pl.when(reuse)
                def _():
                    out_copy(slot, 0).wait()
            rows = pl.ds(pl.multiple_of(j * tk, tk), tk)
            kb = k_ref[rows, :].astype(BF16)
            vb = v_ref[rows, :].astype(BF16)
            z = lax.dot_general(qb, kb, NT, preferred_element_type=F32)
            lsig = -_softplus(z)
            l = lsig if causal is None else jnp.where(causal, lsig, 0.0)
            loga = z + lsig + _dot01(l, usuf) + cl
            if causal is not None:
                loga = jnp.where(causal, loga, NEG)
            ab = jnp.exp(loga).astype(BF16)
            acc = acc + lax.dot_general(ab, vb, NN, preferred_element_type=F32)
            stage[slot] = ab
            out_copy(slot, j).start()
            return acc, cl + jnp.sum(l, axis=1, keepdims=True)

        carry = (jnp.zeros((tq, SB_HD), F32), jnp.zeros((tq, 1), F32))
        for n, dd in enumerate(range(nd - 1, -1, -1)):
            carry = tile(i * nd + dd, carry, c + dd * tk < r, n, None)

        def rest(n, cr):
            return tile(i * nd - 1 - n, cr, None, (nd + n) % SB_SLOTS, nd + n >= SB_SLOTS)

        acc, _ = lax.fori_loop(0, i * nd, rest, carry)
        total = (i + 1) * nd
        for back in range(1, SB_SLOTS + 1):
            @pl.when(total >= back)
            def _():
                out_copy((total - back) % SB_SLOTS, 0).wait()

        o_ref[...] = acc.astype(o_ref.dtype)

    assert nd <= SB_SLOTS
    blk = lambda off: pl.BlockSpec((s, SB_HD), functools.partial(lambda h, i, off: (0, off + h), off=off))
    return pl.pallas_call(
        body, name=name, grid=(heads, s // tq),
        in_specs=[pl.BlockSpec((tq, SB_HD), lambda h, i: (i, h)), blk(heads), blk(2 * heads)],
        out_specs=[pl.BlockSpec((tq, SB_HD), lambda h, i: (i, h)), ANY],
        out_shape=[pltpu.HBM((s, heads * SB_HD), BF16), pltpu.HBM((heads, s // tq, s // tk, tq, tk), BF16)],
        scratch_shapes=[pltpu.VMEM((SB_SLOTS, tq, tk), BF16), pltpu.SemaphoreType.DMA((SB_SLOTS,))],
        compiler_params=_params(8 * s * SB_HD * 4 + 24 * tq * tk * 4 + (8 << 20)),
    )(_hbm(zm), _hbm(zm), _hbm(zm))


def _sb_bwd(zm, dy, a_all, after, heads, *, name, tq=512, tk=256):
    s = zm.shape[0]
    tq, tk, nd = _sb_tiles(s, tq, tk)
    nq = s // tq
    scale = SB_HD ** -0.5

    def body(q_ref, k_ref, v_ref, do_ref, a_in, after_ref, dq_ref, dk_ref, dv_ref, dka, dva, abuf, sem):
        h, i = pl.program_id(0), pl.program_id(1)

        @pl.when(i == 0)
        def _():
            dka[...] = jnp.zeros_like(dka)
            dva[...] = jnp.zeros_like(dva)

        qb = (q_ref[...] * scale).astype(BF16)
        dob = do_ref[...].astype(BF16)
        r = lax.broadcasted_iota(jnp.int32, (tq, tk), 0)
        c = lax.broadcasted_iota(jnp.int32, (tq, tk), 1)
        ur = lax.broadcasted_iota(jnp.int32, (tk, tk), 0)
        uc = lax.broadcasted_iota(jnp.int32, (tk, tk), 1)
        uexcl = (ur < uc).astype(BF16)

        def fetch(j, slot):
            return pltpu.make_async_copy(a_in.at[h, i, j], abuf.at[slot], sem.at[slot])

        total = (i + 1) * nd
        ahead = SB_SLOTS - 1

        def tile(j, carry, causal):
            dq, cg = carry
            slot = j % SB_SLOTS
            fetch(j, slot).wait()

            @pl.when(j + ahead < total)
            def _():
                fetch(j + ahead, (j + ahead) % SB_SLOTS).start()

            rows = pl.ds(pl.multiple_of(j * tk, tk), tk)
            kb = k_ref[rows, :].astype(BF16)
            vb = v_ref[rows, :].astype(BF16)
            z = lax.dot_general(qb, kb, NT, preferred_element_type=F32)
            sig = 1.0 / (1.0 + jnp.exp(-z))
            ab = abuf[slot]
            g = ab.astype(F32) * lax.dot_general(dob, vb, NT, preferred_element_type=F32)
            p = cg + lax.dot_general(g.astype(BF16), uexcl, NN, preferred_element_type=F32)
            dz = g - sig * (g + p)
            if causal is not None:
                dz = jnp.where(causal, dz, 0.0)
            dzb = dz.astype(BF16)
            dva[rows, :] += lax.dot_general(ab, dob, TN, preferred_element_type=F32)
            dka[rows, :] += lax.dot_general(dzb, qb, TN, preferred_element_type=F32)
            dq = dq + lax.dot_general(dzb, kb, NN, preferred_element_type=F32)
            return dq, cg + jnp.sum(g, axis=1, keepdims=True)

        for first in range(ahead):
            @pl.when(first < total)
            def _():
                fetch(first, first).start()

        init = (jnp.zeros((tq, SB_HD), F32), jnp.zeros((tq, 1), F32))
        carry = lax.fori_loop(0, i * nd, lambda j, cr: tile(j, cr, None), init)
        for dd in range(nd):
            carry = tile(i * nd + dd, carry, c + dd * tk < r)
        dq_ref[...] = (carry[0] * scale).astype(dq_ref.dtype)

        @pl.when(i == nq - 1)
        def _():
            dk_ref[...] = dka[...].astype(dk_ref.dtype)
            dv_ref[...] = dva[...].astype(dv_ref.dtype)

    blk = lambda off: pl.BlockSpec((s, SB_HD), functools.partial(lambda h, i, off: (0, off + h), off=off))
    tile_spec = pl.BlockSpec((tq, SB_HD), lambda h, i: (i, h))
    full = pltpu.HBM((s, heads * SB_HD), BF16)
    return pl.pallas_call(
        body, name=name, grid=(heads, nq),
        in_specs=[tile_spec, blk(heads), blk(2 * heads), tile_spec, ANY, ANY],
        out_specs=[tile_spec, blk(0), blk(0)],
        out_shape=[full, full, full],
        scratch_shapes=[pltpu.VMEM((s, SB_HD), F32), pltpu.VMEM((s, SB_HD), F32),
                        pltpu.VMEM((SB_SLOTS, tq, tk), BF16), pltpu.SemaphoreType.DMA((SB_SLOTS,))],
        compiler_params=_params(12 * s * SB_HD * 4 + 32 * tq * tk * 4 + (8 << 20)),
    )(_hbm(zm), _hbm(zm), _hbm(zm), _hbm(dy), a_all, after)


def _conv_taps(u, w_ref, rows_i):
    taps = []
    for j in range(CONV_W):
        sh = CONV_W - 1 - j
        if sh == 0:
            taps.append(u)
        else:
            taps.append(jnp.where(rows_i >= sh, pltpu.roll(u, sh, 0), 0.0))
    return taps


def _conv_fwd(zm, col0, width, cw, cb, *, name):
    s = zm.shape[0]
    bw = _pick(width, (LANES,))
    off = col0 // bw

    def body(u_ref, w_ref, b_ref, o_ref):
        u = u_ref[...]
        rows_i = lax.broadcasted_iota(jnp.int32, u.shape, 0)
        acc = jnp.broadcast_to(b_ref[...], u.shape)
        for j, tp in enumerate(_conv_taps(u, w_ref, rows_i)):
            acc = acc + tp * w_ref[j:j + 1, :]
        o_ref[...] = acc * _sigmoid(acc)

    return pl.pallas_call(
        body, name=name, grid=(width // bw,),
        in_specs=[pl.BlockSpec((s, bw), lambda j: (0, off + j)), pl.BlockSpec((CONV_W, bw), lambda j: (0, j)),
                  pl.BlockSpec((1, bw), lambda j: (0, j))],
        out_specs=pl.BlockSpec((s, bw), lambda j: (0, j)),
        out_shape=pltpu.HBM((s, width), F32),
        compiler_params=_params(12 * s * bw * 4 + (4 << 20)),
    )(_hbm(zm), cw, cb)


def _conv_bwd(zm, col0, width, cw, cb, dqk, *, name):
    s = zm.shape[0]
    bw = _pick(width, (LANES,))
    off = col0 // bw

    def body(u_ref, w_ref, b_ref, d_ref, du_ref, dw_ref, db_ref):
        u = u_ref[...]
        rows_i = lax.broadcasted_iota(jnp.int32, u.shape, 0)
        taps = _conv_taps(u, w_ref, rows_i)
        acc = jnp.broadcast_to(b_ref[...], u.shape)
        for j, tp in enumerate(taps):
            acc = acc + tp * w_ref[j:j + 1, :]
        sg = _sigmoid(acc)
        dc = d_ref[...] * (sg * (1.0 + acc * (1.0 - sg)))
        du = jnp.zeros_like(u)
        for j in range(CONV_W):
            sh = CONV_W - 1 - j
            if sh == 0:
                du = du + dc * w_ref[j:j + 1, :]
            else:
                du = du + jnp.where(rows_i < s - sh, pltpu.roll(dc, s - sh, 0), 0.0) * w_ref[j:j + 1, :]
            dw_ref[j:j + 1, :] = jnp.sum(dc * taps[j], axis=0, keepdims=True)
        du_ref[...] = du.astype(du_ref.dtype)
        db_ref[...] = jnp.sum(dc, axis=0, keepdims=True)

    return pl.pallas_call(
        body, name=name, grid=(width // bw,),
        in_specs=[pl.BlockSpec((s, bw), lambda j: (0, off + j)), pl.BlockSpec((CONV_W, bw), lambda j: (0, j)),
                  pl.BlockSpec((1, bw), lambda j: (0, j)), pl.BlockSpec((s, bw), lambda j: (0, j))],
        out_specs=[pl.BlockSpec((s, bw), lambda j: (0, j)), pl.BlockSpec((CONV_W, bw), lambda j: (0, j)),
                   pl.BlockSpec((1, bw), lambda j: (0, j))],
        out_shape=[pltpu.HBM((s, width), BF16), pltpu.HBM((CONV_W, width), F32),
                   pltpu.HBM((1, width), F32)],
        compiler_params=_params(20 * s * bw * 4 + (4 << 20)),
    )(_hbm(zm), cw, cb, _hbm(dqk))


def _ml_gates(gcol_ref, grow_ref):
    l = CHUNK
    r = lax.broadcasted_iota(jnp.int32, (l, l), 0)
    c = lax.broadcasted_iota(jnp.int32, (l, l), 1)
    gcol = gcol_ref[...]
    grow = grow_ref[0]
    bcol = _u01dot((c <= r).astype(BF16), gcol)
    brow = _dot01(grow, (r <= c).astype(BF16))
    return gcol, grow, bcol, brow, r >= c


def _ml_chunk(h, dh, mq_ref, mk_ref, v_ref, gates, cp, n_prev, m_prev):
    gcol, grow, bcol, brow, tri = gates
    l = CHUNK
    sl = slice(h * dh, (h + 1) * dh)
    qc = mq_ref[:, sl]
    kc = mk_ref[:, sl] * (dh ** -0.5)
    vc = v_ref[:, sl]
    i_row = grow[h:h + 1, :]
    i_col = gcol[:, h:h + 1]
    b_col = bcol[:, ML_HEADS + h:ML_HEADS + h + 1]
    b_row = brow[ML_HEADS + h:ML_HEADS + h + 1, :]
    b_end = b_col[l - 1:l, :]
    d = jnp.where(tri, b_col - b_row + i_row, -jnp.inf)
    m_inter = b_col + m_prev
    m_t = jnp.maximum(m_inter, jnp.max(d, axis=1, keepdims=True))
    w = jnp.exp(d - m_t)
    s_inter = jnp.exp(m_inter - m_t)
    qb, kb, vb = qc.astype(BF16), kc.astype(BF16), vc.astype(BF16)
    cpb = cp.astype(BF16)
    a = lax.dot_general(qb, kb, NT, preferred_element_type=F32)
    sc = a * w
    qcp = lax.dot_general(qb, cpb, NT, preferred_element_type=F32)
    qn = jnp.sum(qc * n_prev, axis=1, keepdims=True)
    num = lax.dot_general(sc.astype(BF16), vb, NN, preferred_element_type=F32) + s_inter * qcp
    den = jnp.sum(sc, axis=1, keepdims=True) + s_inter * qn
    floor = jnp.exp(-m_t)
    dnm = jnp.maximum(jnp.abs(den), floor)
    g_col = b_end - b_col + i_col
    g_row = b_end - b_row + i_row
    m_new = jnp.maximum(b_end + m_prev, jnp.max(g_row, axis=1, keepdims=True))
    decay = jnp.exp(b_end + m_prev - m_new)
    wk = jnp.exp(g_col - m_new)
    return dict(qc=qc, kc=kc, vc=vc, qb=qb, kb=kb, vb=vb, cpb=cpb, w=w, s_inter=s_inter, a=a, sc=sc, qcp=qcp, qn=qn,
                num=num, den=den, floor=floor, dnm=dnm, m_new=m_new, decay=decay, wk=wk, sl=sl)


def _ml_fwd(mqk, zm, vcol, gcol, grow, d_model, *, name):
    s = zm.shape[0]
    nc = s // CHUNK
    dh = d_model // ML_HEADS
    hh = ML_HEADS

    def body(mq_ref, mk_ref, v_ref, gcol_ref, grow_ref, h_ref, cs_ref, ns_ref, ms_ref, c_s, n_s, m_s):
        @pl.when(pl.program_id(0) == 0)
        def _():
            c_s[...] = jnp.zeros_like(c_s)
            n_s[...] = jnp.zeros_like(n_s)
            m_s[...] = jnp.zeros_like(m_s)

        gates = _ml_gates(gcol_ref, grow_ref)
        for h in range(hh):
            cp, n_prev, m_prev = c_s[h], n_s[h], m_s[h][:, 0:1]
            cs_ref[0, h] = cp
            ns_ref[0, h] = n_prev
            ms_ref[0, h] = m_s[h]
            f = _ml_chunk(h, dh, mq_ref, mk_ref, v_ref, gates, cp, n_prev, m_prev)
            h_ref[:, f["sl"]] = f["num"] / f["dnm"]
            c_s[h] = f["decay"] * cp + lax.dot_general((f["vc"] * f["wk"]).astype(BF16), f["kb"], TN,
                                                       preferred_element_type=F32)
            n_s[h] = f["decay"] * n_prev + jnp.sum(f["wk"] * f["kc"], axis=0, keepdims=True)
            m_s[h] = jnp.broadcast_to(f["m_new"], (1, LANES))

    dblk = d_model
    return pl.pallas_call(
        body, name=name, grid=(nc,),
        in_specs=[pl.BlockSpec((CHUNK, dblk), lambda c: (c, 0)), pl.BlockSpec((CHUNK, dblk), lambda c: (c, 1)),
                  pl.BlockSpec((CHUNK, dblk), lambda c: (c, vcol // dblk)),
                  pl.BlockSpec((CHUNK, LANES), lambda c: (c, 0)), pl.BlockSpec((1, 8, CHUNK), lambda c: (c, 0, 0))],
        out_specs=[pl.BlockSpec((CHUNK, dblk), lambda c: (c, 0)),
                   pl.BlockSpec((1, hh, dh, dh), lambda c: (c, 0, 0, 0)),
                   pl.BlockSpec((1, hh, 1, dh), lambda c: (c, 0, 0, 0)),
                   pl.BlockSpec((1, hh, 1, LANES), lambda c: (c, 0, 0, 0))],
        out_shape=[pltpu.HBM((s, d_model), F32), pltpu.HBM((nc, hh, dh, dh), F32),
                   pltpu.HBM((nc, hh, 1, dh), F32), pltpu.HBM((nc, hh, 1, LANES), F32)],
        scratch_shapes=[pltpu.VMEM((hh, dh, dh), F32), pltpu.VMEM((hh, 1, dh), F32), pltpu.VMEM((hh, 1, LANES), F32)],
        compiler_params=_params(8 * hh * dh * dh * 4 + (16 << 20)),
    )(_hbm(mqk), _hbm(mqk), _hbm(zm), _hbm(gcol), _hbm(grow))


def _ml_bwd(mqk, zm, vcol, gcol, grow, cs, ns, ms, dhm, d_model, *, name):
    s = zm.shape[0]
    nc = s // CHUNK
    dh = d_model // ML_HEADS
    hh = ML_HEADS
    l = CHUNK

    def body(mq_ref, mk_ref, v_ref, gcol_ref, grow_ref, cs_ref, ns_ref, ms_ref, dh_ref,
             dqk_ref, dv_ref, dgc_ref, dgr_ref, dc_s, dn_s):
        @pl.when(pl.program_id(0) == 0)
        def _():
            dc_s[...] = jnp.zeros_like(dc_s)
            dn_s[...] = jnp.zeros_like(dn_s)

        gates = _ml_gates(gcol_ref, grow_ref)
        lane = lax.broadcasted_iota(jnp.int32, (l, LANES), 1)
        rowi = lax.broadcasted_iota(jnp.int32, (8, l), 0)
        lastrow = lax.broadcasted_iota(jnp.int32, (l, 1), 0) == l - 1
        dgc = jnp.zeros((l, LANES), F32)
        dgr = jnp.zeros((8, l), F32)
        for h in range(hh):
            cp, n_prev, m_prev = cs_ref[0, h], ns_ref[0, h], ms_ref[0, h][:, 0:1]
            f = _ml_chunk(h, dh, mq_ref, mk_ref, v_ref, gates, cp, n_prev, m_prev)
            dC, dn = dc_s[h], dn_s[h]
            dhv = dh_ref[:, f["sl"]]
            dnum = dhv / f["dnm"]
            hv = f["num"] / f["dnm"]
            ddnm = -jnp.sum(dhv * hv, axis=1, keepdims=True) / f["dnm"]
            dden = jnp.where(jnp.abs(f["den"]) >= f["floor"], ddnm * jnp.sign(f["den"]), 0.0)
            dnb = dnum.astype(BF16)
            dsc = lax.dot_general(dnb, f["vb"], NT, preferred_element_type=F32) + dden
            dvc = lax.dot_general(f["sc"].astype(BF16), dnb, TN, preferred_element_type=F32)
            ds_inter = jnp.sum(dnum * f["qcp"], axis=1, keepdims=True) + dden * f["qn"]
            sdn = (f["s_inter"] * dnum).astype(BF16)
            sdd = f["s_inter"] * dden
            da = dsc * f["w"]
            dab = da.astype(BF16)
            dqc = (lax.dot_general(dab, f["kb"], NN, preferred_element_type=F32)
                   + lax.dot_general(sdn, f["cpb"], NN, preferred_element_type=F32) + sdd * n_prev)
            dcp = f["decay"] * dC + lax.dot_general(sdn, f["qb"], TN, preferred_element_type=F32)
            dnp = f["decay"] * dn + jnp.sum(sdd * f["qc"], axis=0, keepdims=True)
            vw = (f["vc"] * f["wk"]).astype(BF16)
            dCb = dC.astype(BF16)
            dkc = (lax.dot_general(dab, f["qb"], TN, preferred_element_type=F32)
                   + lax.dot_general(vw, dCb, NN, preferred_element_type=F32) + f["wk"] * dn)
            e = lax.dot_general(f["kb"], dCb, NT, preferred_element_type=F32)
            dvc = dvc + e * f["wk"]
            dwk = jnp.sum(e * f["vc"], axis=1, keepdims=True) + jnp.sum(f["kc"] * dn, axis=1, keepdims=True)
            ddecay = jnp.sum(jnp.sum(dC * cp, axis=1, keepdims=True), axis=0, keepdims=True) \
                + jnp.sum(dn * n_prev, axis=1, keepdims=True)
            dd = dsc * f["sc"]
            dlw = dwk * f["wk"]
            db_end = jnp.sum(dlw, axis=0, keepdims=True) + ddecay * f["decay"]
            di_col = dlw
            db_col = jnp.sum(dd, axis=1, keepdims=True) + ds_inter * f["s_inter"] - dlw \
                + jnp.where(lastrow, db_end, 0.0)
            cs_dd = jnp.sum(dd, axis=0, keepdims=True)
            dgc = dgc + jnp.where(lane == h, di_col, 0.0) + jnp.where(lane == hh + h, db_col, 0.0)
            dgr = dgr + jnp.where(rowi == h, cs_dd, 0.0) - jnp.where(rowi == hh + h, cs_dd, 0.0)
            dqk_ref[:, f["sl"]] = dqc
            dqk_ref[:, d_model + h * dh:d_model + (h + 1) * dh] = dkc * (dh ** -0.5)
            dv_ref[:, f["sl"]] = dvc.astype(dv_ref.dtype)
            dc_s[h] = dcp
            dn_s[h] = dnp
        dgc_ref[...] = dgc
        dgr_ref[0] = dgr

    dblk = d_model
    rev = lambda c: nc - 1 - c
    return pl.pallas_call(
        body, name=name, grid=(nc,),
        in_specs=[pl.BlockSpec((l, dblk), lambda c: (rev(c), 0)), pl.BlockSpec((l, dblk), lambda c: (rev(c), 1)),
                  pl.BlockSpec((l, dblk), lambda c: (rev(c), vcol // dblk)),
                  pl.BlockSpec((l, LANES), lambda c: (rev(c), 0)), pl.BlockSpec((1, 8, l), lambda c: (rev(c), 0, 0)),
                  pl.BlockSpec((1, hh, dh, dh), lambda c: (rev(c), 0, 0, 0)),
                  pl.BlockSpec((1, hh, 1, dh), lambda c: (rev(c), 0, 0, 0)),
                  pl.BlockSpec((1, hh, 1, LANES), lambda c: (rev(c), 0, 0, 0)),
                  pl.BlockSpec((l, dblk), lambda c: (rev(c), 0))],
        out_specs=[pl.BlockSpec((l, 2 * dblk), lambda c: (rev(c), 0)),
                   pl.BlockSpec((l, dblk), lambda c: (rev(c), 0)), pl.BlockSpec((l, LANES), lambda c: (rev(c), 0)),
                   pl.BlockSpec((1, 8, l), lambda c: (rev(c), 0, 0))],
        out_shape=[pltpu.HBM((s, 2 * d_model), F32),
                   pltpu.HBM((s, d_model), BF16), pltpu.HBM((s, LANES), F32),
                   pltpu.HBM((nc, 8, l), F32)],
        scratch_shapes=[pltpu.VMEM((hh, dh, dh), F32), pltpu.VMEM((hh, 1, dh), F32)],
        compiler_params=_params(10 * hh * dh * dh * 4 + (16 << 20)),
    )(*[_hbm(a) for a in (mqk, mqk, zm, gcol, grow, cs, ns, ms, dhm)])


def _xa_fwd(zm, qcol, kv, gq, gk, d_model, *, name, tq=512):
    s = zm.shape[0]
    nm = kv.shape[0]
    dh = d_model // X_HEADS
    tq = _pick(s, (tq, 128, 64))
    scale = dh ** -0.5

    def body(q_ref, k_ref, v_ref, gq_ref, gk_ref, o_ref):
        qn = _rms_fwd(q_ref[...], gq_ref[...])
        kn = _rms_fwd(k_ref[...], gk_ref[...])
        lg = _dot(qn, kn, NT) * scale
        lg = lg - jnp.max(lg, axis=1, keepdims=True)
        p = jnp.exp(lg)
        p = p / jnp.sum(p, axis=1, keepdims=True)
        o_ref[...] = _dot(p, v_ref[...], NN).astype(o_ref.dtype)

    return pl.pallas_call(
        body, name=name, grid=(X_HEADS, s // tq),
        in_specs=[pl.BlockSpec((tq, dh), lambda h, i: (i, qcol // dh + h)), pl.BlockSpec((nm, dh), lambda h, i: (0, h)),
                  pl.BlockSpec((nm, dh), lambda h, i: (0, X_HEADS + h)),
                  pl.BlockSpec((1, dh), lambda h, i: (0, 0)), pl.BlockSpec((1, dh), lambda h, i: (0, 0))],
        out_specs=pl.BlockSpec((tq, dh), lambda h, i: (i, h)),
        out_shape=pltpu.HBM((s, d_model), BF16),
        compiler_params=_params(32 << 20),
    )(_hbm(zm), _hbm(kv), _hbm(kv), gq, gk)


def _xa_bwd(zm, qcol, kv, gq, gk, dy, d_model, *, name, tq=512):
    s = zm.shape[0]
    nm = kv.shape[0]
    dh = d_model // X_HEADS
    tq = _pick(s, (tq, 128, 64))
    nq = s // tq
    scale = dh ** -0.5

    def body(q_ref, k_ref, v_ref, gq_ref, gk_ref, do_ref, dq_ref, dkn_ref, dv_ref, dgq_ref):
        h, i = pl.program_id(0), pl.program_id(1)

        @pl.when(i == 0)
        def _():
            dkn_ref[...] = jnp.zeros_like(dkn_ref)
            dv_ref[...] = jnp.zeros_like(dv_ref)

        @pl.when((i == 0) & (h == 0))
        def _():
            dgq_ref[...] = jnp.zeros_like(dgq_ref)

        q = q_ref[...]
        qn = _rms_fwd(q, gq_ref[...])
        kn = _rms_fwd(k_ref[...], gk_ref[...])
        lg = _dot(qn, kn, NT) * scale
        lg = lg - jnp.max(lg, axis=1, keepdims=True)
        p = jnp.exp(lg)
        p = p / jnp.sum(p, axis=1, keepdims=True)
        do = do_ref[...]
        dv_ref[...] += _dot(p, do, TN)
        dp = _dot(do, v_ref[...], NT)
        dlg = p * (dp - jnp.sum(dp * p, axis=1, keepdims=True)) * scale
        dqn = _dot(dlg, kn, NN)
        dkn_ref[...] += _dot(dlg, qn, TN)
        dq, dgq = _rms_bwd(q, gq_ref[...], dqn)
        dq_ref[...] = dq.astype(dq_ref.dtype)
        dgq_ref[...] += jnp.sum(dgq, axis=0, keepdims=True)

    return pl.pallas_call(
        body, name=name, grid=(X_HEADS, nq),
        in_specs=[pl.BlockSpec((tq, dh), lambda h, i: (i, qcol // dh + h)), pl.BlockSpec((nm, dh), lambda h, i: (0, h)),
                  pl.BlockSpec((nm, dh), lambda h, i: (0, X_HEADS + h)),
                  pl.BlockSpec((1, dh), lambda h, i: (0, 0)), pl.BlockSpec((1, dh), lambda h, i: (0, 0)),
                  pl.BlockSpec((tq, dh), lambda h, i: (i, h))],
        out_specs=[pl.BlockSpec((tq, dh), lambda h, i: (i, h)), pl.BlockSpec((nm, dh), lambda h, i: (0, h)),
                   pl.BlockSpec((nm, dh), lambda h, i: (0, h)), pl.BlockSpec((1, dh), lambda h, i: (0, 0))],
        out_shape=[pltpu.HBM((s, d_model), BF16), pltpu.HBM((nm, d_model), F32),
                   pltpu.HBM((nm, d_model), F32), pltpu.HBM((1, dh), F32)],
        compiler_params=_params(32 << 20),
    )(_hbm(zm), _hbm(kv), _hbm(kv), gq, gk, _hbm(dy))


def _place():
    return lax.axis_index("x"), lax.axis_index("y"), lax.axis_index("c")


ANY = pl.BlockSpec(memory_space=pl.ANY)


def _allgather_two_level(big, small, *, name, chunk_rows=64):
    r, cc = big.shape
    half = r // 2
    nr = _pick(half, (chunk_rows, 32, 16))
    nq = half // nr

    def body(big_ref, small_ref, obig, osmall, land, passed, send, recv, fsend, frecv, out_a, out_b, ssend, srecv, loc):
        x, y, c = _place()
        k = 2 * x + y
        chips = [(1 - x, y), (x, 1 - y), (1 - x, 1 - y)]
        slots = [2 * px + py for px, py in chips]
        local = [pltpu.make_async_copy(big_ref, obig.at[k], loc.at[0]),
                 pltpu.make_async_copy(small_ref, osmall.at[k], loc.at[1])]
        for cp in local:
            cp.start()

        def rows(h, q):
            return pl.ds(pl.multiple_of(h * half + q * nr, nr), nr)

        def chunk(q):
            return pl.ds(q * nr, nr)

        def over_ici(j, q):
            return pltpu.make_async_remote_copy(
                src_ref=big_ref.at[rows(c, q)], dst_ref=land.at[j, chunk(q)], send_sem=send.at[nq * j + q],
                recv_sem=recv.at[nq * j + q], device_id=(chips[j][0], chips[j][1], c), device_id_type=MESH)

        def to_sibling(j, q):
            return pltpu.make_async_remote_copy(
                src_ref=land.at[j, chunk(q)], dst_ref=passed.at[j, chunk(q)], send_sem=fsend.at[nq * j + q],
                recv_sem=frecv.at[nq * j + q], device_id=(x, y, 1 - c), device_id_type=MESH)

        def small_copy(j, slot):
            return pltpu.make_async_remote_copy(
                src_ref=small_ref, dst_ref=osmall.at[slot], send_sem=ssend.at[j], recv_sem=srecv.at[j],
                device_id=(chips[j][0], chips[j][1], c), device_id_type=MESH)

        for q in range(nq):
            for j in range(3):
                over_ici(j, q).start()
        for j in range(3):
            small_copy(j, k).start()
        for q in range(nq):
            for j in range(3):
                over_ici(j, q).wait_recv()
                to_sibling(j, q).start()
                cp = pltpu.make_async_copy(land.at[j, chunk(q)], obig.at[slots[j], rows(c, q)], out_a.at[nq * j + q])
                cp.start()
                local.append(cp)
        for q in range(nq):
            for j in range(3):
                to_sibling(j, q).wait_recv()
                cp = pltpu.make_async_copy(passed.at[j, chunk(q)], obig.at[slots[j], rows(1 - c, q)],
                                           out_b.at[nq * j + q])
                cp.start()
                local.append(cp)
        for j in range(3):
            small_copy(j, slots[j]).wait_recv()
            small_copy(j, k).wait_send()
        for q in range(nq):
            for j in range(3):
                over_ici(j, q).wait_send()
                to_sibling(j, q).wait_send()
        for cp in local:
            cp.wait()

    stage = 2 * _nbytes((3, half, cc), big.dtype)
    return pl.pallas_call(
        body, name=name, in_specs=[ANY] * 2, out_specs=[ANY] * 2,
        out_shape=[pltpu.HBM((4,) + big.shape, big.dtype), pltpu.HBM((4,) + small.shape, small.dtype)],
        scratch_shapes=[pltpu.VMEM((3, half, cc), big.dtype), pltpu.VMEM((3, half, cc), big.dtype)]
        + [pltpu.SemaphoreType.DMA((3 * nq,))] * 6
        + [pltpu.SemaphoreType.DMA((3,)), pltpu.SemaphoreType.DMA((3,)), pltpu.SemaphoreType.DMA((2,))],
        compiler_params=_params(stage + stage // 8 + (4 << 20)),
    )(big, small)


HBM_SPEC = pl.BlockSpec(memory_space=pltpu.HBM)
SEM_SPEC = pl.BlockSpec(memory_space=pltpu.SEMAPHORE)
EFFECT = pltpu.SideEffectType.DATAFLOW_SIDE_EFFECTING


def _split_copies(kind, srcs, lands, send, recv):
    x, y, c = _place()
    if kind == "quarters":
        peers = [(1 - x, y, c), (x, 1 - y, c), (1 - x, 1 - y, c)]
    else:
        peers = [(x ^ ((j >> 2) & 1), y ^ ((j >> 1) & 1), c ^ (j & 1)) for j in range(1, 8)]
    npeer = len(peers)
    out = []
    for t in range(len(srcs)):
        for j, (px, py, pc) in enumerate(peers):
            if kind == "quarters":
                src, mine, theirs = srcs[t], 2 * x + y, 2 * px + py
            else:
                src, mine, theirs = srcs[t].at[2 * px + py, pc], 4 * x + 2 * y + c, 4 * px + 2 * py + pc
            mk = functools.partial(
                pltpu.make_async_remote_copy, src_ref=src, send_sem=send.at[npeer * t + j],
                recv_sem=recv.at[npeer * t + j], device_id=(px, py, pc), device_id_type=MESH)
            out.append((functools.partial(mk, dst_ref=lands[t].at[mine]),
                        functools.partial(mk, dst_ref=lands[t].at[theirs])))
    return out


def _split_start(kind, srcs, land_shapes, after, *, name):
    n = len(srcs)
    ncopies = n * (3 if kind == "quarters" else 7)

    def body(*refs):
        ins, lands = refs[:n], refs[n:2 * n]
        send, recv = refs[2 * n + 1], refs[2 * n + 2]
        token = refs[-1]
        for start, _ in _split_copies(kind, ins, lands, send, recv):
            start().start()
        token[...] = jnp.zeros_like(token)

    lands = [_hbm(lax.empty(shp, a.dtype)) for shp, a in zip(land_shapes, srcs)]
    res = pl.pallas_call(
        body, name=name, in_specs=[HBM_SPEC] * (2 * n) + [ANY],
        out_specs=[SEM_SPEC, SEM_SPEC] + [HBM_SPEC] * (2 * n) + [pl.BlockSpec(memory_space=pltpu.VMEM)],
        out_shape=[pltpu.SemaphoreType.DMA((ncopies,)), pltpu.SemaphoreType.DMA((ncopies,))]
        + [pltpu.HBM(a.shape, a.dtype) for a in srcs] + [pltpu.HBM(shp, a.dtype) for shp, a in zip(land_shapes, srcs)]
        + [jax.ShapeDtypeStruct((8, LANES), F32)],
        input_output_aliases={i: 2 + i for i in range(2 * n)},
        compiler_params=pltpu.CompilerParams(has_side_effects=EFFECT),
    )(*[_hbm(a) for a in srcs], *lands, after)
    return res[0], res[1], list(res[2:2 + n]), list(res[2 + n:2 + 2 * n]), res[-1]


def _split_wait(kind, send, recv, srcs, lands, after, *, name):
    n = len(srcs)

    def body(*refs):
        ins, lnd = refs[:n], refs[n:2 * n]
        snd, rcv = refs[2 * n], refs[2 * n + 1]
        for start, arrive in _split_copies(kind, ins, lnd, snd, rcv):
            start().wait_send()
            arrive().wait_recv()

    res = pl.pallas_call(
        body, name=name, in_specs=[HBM_SPEC] * (2 * n) + [SEM_SPEC, SEM_SPEC] + [ANY] * len(after),
        out_specs=[HBM_SPEC] * (2 * n),
        out_shape=[pltpu.HBM(a.shape, a.dtype) for a in srcs] + [pltpu.HBM(a.shape, a.dtype) for a in lands],
        input_output_aliases={i: i for i in range(2 * n)},
        compiler_params=pltpu.CompilerParams(has_side_effects=EFFECT),
    )(*srcs, *lands, send, recv, *after)
    return list(res[n:])


def _sum8(parts, *, name):
    _, r, c = parts.shape
    t = _pick(r, (128, 64, 32, 16, 8))

    def body(p_ref, o_ref):
        acc = p_ref[0].astype(F32)
        for k in range(1, 8):
            acc = acc + p_ref[k].astype(F32)
        o_ref[...] = acc

    return pl.pallas_call(
        body, name=name, grid=(r // t,), in_specs=[pl.BlockSpec((8, t, c), lambda i: (0, i, 0))],
        out_specs=pl.BlockSpec((t, c), lambda i: (i, 0)), out_shape=pltpu.HBM((r, c), F32),
        compiler_params=_params(2 * 8 * t * c * 2 + 6 * t * c * 4 + (4 << 20)),
    )(_hbm(parts))


def _swap_halves(halves, *, name, chunk_bytes=512 * 1024):
    n = len(halves)
    items = []
    for t, a in enumerate(halves):
        r = a.shape[0]
        k = 1
        while _nbytes(a.shape, a.dtype) // k > chunk_bytes and r % (2 * k) == 0 and (r // (2 * k)) % 8 == 0:
            k *= 2
        items += [(t, q * (r // k), r // k) for q in range(k)]
    m = len(items)

    def body(*refs):
        ins, outs = refs[:n], refs[n:2 * n]
        sbuf, rbuf = refs[2 * n:3 * n], refs[3 * n:4 * n]
        send, recv, loc_own, loc_in, loc_out = refs[4 * n:]
        x, y, c = _place()
        local, stage = [], []
        for t in range(n):
            cp = pltpu.make_async_copy(ins[t], outs[t].at[c], loc_own.at[t])
            cp.start()
            local.append(cp)
        for q, (t, r0, nr) in enumerate(items):
            cp = pltpu.make_async_copy(ins[t].at[pl.ds(r0, nr)], sbuf[t].at[pl.ds(r0, nr)], loc_in.at[q])
            cp.start()
            stage.append(cp)

        def copy(q):
            t, r0, nr = items[q]
            return pltpu.make_async_remote_copy(
                src_ref=sbuf[t].at[pl.ds(r0, nr)], dst_ref=rbuf[t].at[pl.ds(r0, nr)], send_sem=send.at[q],
                recv_sem=recv.at[q], device_id=(x, y, 1 - c), device_id_type=MESH)

        for q in range(m):
            stage[q].wait()
            copy(q).start()
        for q, (t, r0, nr) in enumerate(items):
            copy(q).wait_recv()
            cp = pltpu.make_async_copy(rbuf[t].at[pl.ds(r0, nr)], outs[t].at[1 - c, pl.ds(r0, nr)], loc_out.at[q])
            cp.start()
            local.append(cp)
        for q in range(m):
            copy(q).wait_send()
        for cp in local:
            cp.wait()

    stage_bytes = 2 * sum(_nbytes(a.shape, a.dtype) for a in halves)
    return pl.pallas_call(
        body, name=name, in_specs=[ANY] * n, out_specs=[ANY] * n,
        out_shape=[pltpu.HBM((2,) + a.shape, a.dtype) for a in halves],
        scratch_shapes=[pltpu.VMEM(a.shape, a.dtype) for a in halves] * 2
        + [pltpu.SemaphoreType.DMA((m,)), pltpu.SemaphoreType.DMA((m,)), pltpu.SemaphoreType.DMA((n,)),
           pltpu.SemaphoreType.DMA((m,)), pltpu.SemaphoreType.DMA((m,))],
        compiler_params=_params(stage_bytes + (4 << 20)),
    )(*halves)


def _allreduce_small(p, after, *, name):
    r = p.shape[0]

    def body(p_ref, after_ref, o_ref, buf, send, recv):
        x, y, c = _place()
        me = 4 * x + 2 * y + c
        peers = [(x ^ ((j >> 2) & 1), y ^ ((j >> 1) & 1), c ^ (j & 1)) for j in range(1, 8)]

        def copy(j, slot):
            return pltpu.make_async_remote_copy(
                src_ref=p_ref, dst_ref=buf.at[slot], send_sem=send.at[j], recv_sem=recv.at[j],
                device_id=peers[j], device_id_type=MESH)

        for j in range(7):
            copy(j, me).start()
        buf[me] = p_ref[...]
        for j in range(7):
            px, py, pc = peers[j]
            copy(j, 4 * px + 2 * py + pc).wait_recv()
        for j in range(7):
            copy(j, me).wait_send()
        acc = buf[0]
        for k in range(1, 8):
            acc = acc + buf[k]
        o_ref[...] = acc

    vspec = pl.BlockSpec(memory_space=pltpu.VMEM)
    return pl.pallas_call(
        body, name=name, in_specs=[vspec, ANY], out_specs=vspec, out_shape=jax.ShapeDtypeStruct((r, LANES), F32),
        scratch_shapes=[pltpu.VMEM((8, r, LANES), F32), pltpu.SemaphoreType.DMA((7,)), pltpu.SemaphoreType.DMA((7,))],
    )(p, after)


def _adamw_fn(w, g, m, v):
    m = ADAM_B1 * m + (1.0 - ADAM_B1) * g
    v = ADAM_B2 * v + (1.0 - ADAM_B2) * (g * g)
    m_hat = m / (1.0 - ADAM_B1 ** ADAM_STEP)
    v_hat = v / (1.0 - ADAM_B2 ** ADAM_STEP)
    delta = -ADAM_LR * (m_hat / (jnp.sqrt(v_hat) + ADAM_EPS) + ADAM_WD * w)
    return delta, m, v


def _adamw(w, g, m, v, *, name):
    c = w.shape[1]
    return _rowwise(_adamw_fn, [w, g, m, v], [], [(c, F32)] * 3, name=name, tr=128)


def _pack(vecs, rows):
    flat = jnp.concatenate([a.reshape(-1).astype(F32) for a in vecs])
    return jnp.pad(flat, (0, rows * LANES - flat.shape[0])).reshape(rows, LANES)


def _unpack(p, like):
    flat, out, o = p.reshape(-1), [], 0
    for a in like:
        out.append(flat[o:o + a.size].reshape(a.shape))
        o += a.size
    return out


def kernel(x, mem, g_mix, w_in, b_if, b_gate, conv_w, conv_b, ml_norm_g, g_mem, w_mem_kv, q_norm_g, k_norm_g, w_sb_proj, w_ml_proj, w_x_proj, w_out, g_mlp, w_ff1, w_ff2, loss_target, m_g_mix, m_w_in, m_b_if, m_b_gate, m_conv_w, m_conv_b, m_ml_norm_g, m_g_mem, m_w_mem_kv, m_q_norm_g, m_k_norm_g, m_w_sb_proj, m_w_ml_proj, m_w_x_proj, m_w_out, m_g_mlp, m_w_ff1, m_w_ff2, v_g_mix, v_w_in, v_b_if, v_b_gate, v_conv_w, v_conv_b, v_ml_norm_g, v_g_mem, v_w_mem_kv, v_q_norm_g, v_k_norm_g, v_w_sb_proj, v_w_ml_proj, v_w_x_proj, v_w_out, v_g_mlp, v_w_ff1, v_w_ff2):
    _, s, d = x.shape
    nm = mem.shape[1]
    n_in = 4 * w_in.shape[2]
    dff = 4 * w_ff1.shape[2]
    sbh = d // SB_HD
    hh = ML_HEADS
    dh = d // hh
    nc = s // CHUNK
    assert n_in == 11 * d + 2 * hh and d % (2 * LANES) == 0 and s % LANES == 0
    x2, mem2, tgt = x[0], mem[0], loss_target[0]

    k4 = 2 * lax.axis_index("x") + lax.axis_index("y")
    me = 2 * k4 + lax.axis_index("c")
    g_first = _allgather_two_level(w_in[0].astype(BF16), conv_w[0], name="gather_w_in")
    later = [a[0].astype(BF16) for a in (w_mem_kv, w_sb_proj, w_ml_proj, w_x_proj, w_out, w_ff1, w_ff2)]
    gw_send, gw_recv, gw_src, gw_land, gw_token = _split_start(
        "quarters", later, [(4,) + a.shape for a in later], g_first[0], name="gather_rest_start")
    cols = lambda a: a.transpose(1, 0, 2).reshape(a.shape[1], 4 * a.shape[2])
    rws = lambda a: a.reshape(4 * a.shape[1], a.shape[2])
    w_in_f = cols(g_first[0])
    w_main = jnp.concatenate([w_in_f[:, :7 * d], w_in_f[:, 7 * d + 2 * hh:]], axis=1)
    w_if = jnp.pad(w_in_f[:, 7 * d:7 * d + 2 * hh], ((0, 0), (0, LANES - 2 * hh)))
    conv_wf = cols(g_first[1])
    b_if_p = jnp.pad(b_if, ((0, 0), (0, LANES - 2 * hh)))

    (hn,) = _rowwise(_rms_fwd, [x2], [g_mix], [(d, BF16)], name="norm_in")
    zm = _mm(hn, w_main, after=gw_token, name="proj_in")
    zif = _mm(hn, w_if, name="proj_if")
    y_sb, a_sb = _sb_fwd(zm, sbh, name="sb_fwd")

    def gate_fn(z, b):
        pre = z + b
        lane = lax.broadcasted_iota(jnp.int32, pre.shape, 1)
        return jnp.where(lane < hh, pre, -_softplus(-pre))

    (gcol,) = _rowwise(gate_fn, [zif], [b_if_p], [(LANES, F32)], name="ml_gates")
    grow = gcol[:, :8].T.reshape(8, nc, CHUNK).transpose(1, 0, 2)
    mqk = _conv_fwd(zm, 3 * d, 2 * d, conv_wf, conv_b, name="conv_fwd")
    hm, cst, nst, mst = _ml_fwd(mqk, zm, 5 * d, gcol, grow, d, name="ml_fwd")

    def mlout_fn(hv, o, g):
        ys = [_rms_fwd(hv[:, k * dh:(k + 1) * dh], g[:, k * dh:(k + 1) * dh]) for k in range(hh)]
        return jnp.concatenate(ys, axis=1) * _sigmoid(o)

    (y_ml,) = _rowwise(mlout_fn, [hm, (zm, d, 6)], [ml_norm_g], [(d, BF16)], name="ml_out")
    gw_land = _split_wait("quarters", gw_send, gw_recv, gw_src, gw_land, [y_ml, y_sb], name="gather_rest_wait")
    gw = [lax.dynamic_update_index_in_dim(ld, a, k4, 0) for ld, a in zip(gw_land, later)]
    w_kv, w_sbp, w_mlp, w_xp, w_o, w_f1, w_f2 = (cols(gw[0]), rws(gw[1]), rws(gw[2]), rws(gw[3]), rws(gw[4]),
                                                 cols(gw[5]), rws(gw[6]))
    (memn,) = _rowwise(_rms_fwd, [mem2], [g_mem], [(d, BF16)], name="norm_mem")
    kv = _mm(memn, w_kv, name="proj_kv")
    y_x = _xa_fwd(zm, 7 * d, kv, q_norm_g, k_norm_g, d, name="xa_fwd")
    p_sb = _mm(y_sb, w_sbp, name="proj_sb")
    p_ml = _mm(y_ml, w_mlp, name="proj_ml")
    p_x = _mm(y_x, w_xp, name="proj_x")

    def merge_fn(a, b, c, g0, g1, g2, bg):
        return (_sigmoid(g0 + bg[:, :d]) * a + _sigmoid(g1 + bg[:, d:2 * d]) * b + _sigmoid(g2 + bg[:, 2 * d:]) * c)

    gate_cols = [(zm, d, 8), (zm, d, 9), (zm, d, 10)]
    (mixed,) = _rowwise(merge_fn, [p_sb, p_ml, p_x] + gate_cols, [b_gate], [(d, BF16)], name="merge")
    x1 = _mm(mixed, w_o, tiles=[x2], name="proj_out")
    (h2,) = _rowwise(_rms_fwd, [x1], [g_mlp], [(d, BF16)], name="norm_mlp")
    u, act = _mm(h2, w_f1, post=lambda r: (r, jnp.square(jnp.maximum(r, 0.0))), out_dtype=(F32, BF16), name="ff1")
    dy = _mm(act, w_f2, tiles=[x1, tgt], post=lambda r, xv, tv: (r + xv - tv) * (1.0 / d), name="ff2")
    (loss_cols,) = _rowwise(lambda g: (jnp.sum(g * g, axis=0, keepdims=True) * (0.5 * d),), [dy], [], [], [d],
                            name="loss")

    du = _mm(dy, w_f2, tb=True, tiles=[u], post=lambda r, uv: r * 2.0 * jnp.maximum(uv, 0.0), out_dtype=BF16,
             name="ff2_dx")
    dw_f2 = _mm(act, dy, ta=True, name="ff2_dw")
    dw_f1 = _mm(h2, du, ta=True, name="ff1_dw")
    dh2 = _mm(du, w_f1, tb=True, name="ff1_dx")

    def norm_bwd_fn(xv, dyv, res, g):
        dx, dg = _rms_bwd(xv, g, dyv)
        return dx + res, jnp.sum(dg, axis=0, keepdims=True)

    dx1, dg_mlp = _rowwise(norm_bwd_fn, [x1, dh2, dy], [g_mlp], [(d, F32)], [d], name="norm_mlp_bwd")
    dmixed = _mm(dx1, w_o, tb=True, name="proj_out_dx")
    dw_o = _mm(mixed, dx1, ta=True, name="proj_out_dw")

    def merge_bwd_fn(dm, a, b, c, g0, g1, g2, bg):
        outs, dgs = [], []
        for p, g, k in ((a, g0, 0), (b, g1, 1), (c, g2, 2)):
            sg = _sigmoid(g + bg[:, k * d:(k + 1) * d])
            outs.append(dm * sg)
            dgs.append(dm * p * sg * (1.0 - sg))
        dgate = jnp.concatenate(dgs, axis=1)
        return (*outs, dgate, jnp.sum(dgate, axis=0, keepdims=True))

    dp_sb, dp_ml, dp_x, dgate, db_gate = _rowwise(
        merge_bwd_fn, [dmixed, p_sb, p_ml, p_x] + gate_cols, [b_gate], [(d, BF16)] * 3 + [(3 * d, BF16)], [3 * d],
        name="merge_bwd", tr=128)
    dw_sbp = _mm(y_sb, dp_sb, ta=True, name="proj_sb_dw")
    dw_mlp = _mm(y_ml, dp_ml, ta=True, name="proj_ml_dw")
    dw_xp = _mm(y_x, dp_x, ta=True, name="proj_x_dw")
    dy_sb = _mm(dp_sb, w_sbp, tb=True, out_dtype=BF16, name="proj_sb_dx")
    dy_ml = _mm(dp_ml, w_mlp, tb=True, name="proj_ml_dx")
    dy_x = _mm(dp_x, w_xp, tb=True, out_dtype=BF16, name="proj_x_dx")

    dxq, dkn, dxv, dg_qn = _xa_bwd(zm, 7 * d, kv, q_norm_g, k_norm_g, dy_x, d, name="xa_bwd")

    def knorm_bwd_fn(kvv, dknv, dvv, g):
        dks, dgs = [], []
        for k in range(X_HEADS):
            sl = slice(k * dh, (k + 1) * dh)
            dk, dg = _rms_bwd(kvv[:, sl], g, dknv[:, sl])
            dks.append(dk)
            dgs.append(jnp.sum(dg, axis=0, keepdims=True))
        return jnp.concatenate(dks + [dvv], axis=1), dgs[0] + dgs[1] + dgs[2] + dgs[3]

    dkv, dg_kn = _rowwise(knorm_bwd_fn, [(kv, d, 0), dkn, dxv], [k_norm_g], [(2 * d, BF16)], [dh], name="xa_knorm_bwd")
    dw_kv = _mm(memn, dkv, ta=True, name="proj_kv_dw")
    dmemn = _mm(dkv, w_kv, tb=True, name="proj_kv_dx")

    def gmem_fn(mv, dv_, g):
        _, dg = _rms_bwd(mv, g, dv_)
        return (jnp.sum(dg, axis=0, keepdims=True),)

    (dg_mem,) = _rowwise(gmem_fn, [mem2, dmemn], [g_mem], [], [d], name="norm_mem_bwd")

    uncols = lambda a: a.reshape(a.shape[0], 4, a.shape[1] // 4).transpose(1, 0, 2)
    unrws = lambda a: a.reshape(4, a.shape[0] // 4, a.shape[1])
    to_parts = lambda q: q.astype(BF16).reshape(4, 2, q.shape[1] // 2, q.shape[2])
    early = [to_parts(q) for q in (uncols(dw_kv), unrws(dw_sbp), unrws(dw_mlp), unrws(dw_xp), unrws(dw_o),
                                   uncols(dw_f1), unrws(dw_f2))]
    ge_send, ge_recv, ge_src, ge_land, ge_token = _split_start(
        "grads", early, [(8,) + a.shape[2:] for a in early], dg_mem, name="exchange_early_start")

    dsq, dsk, dsv = _sb_bwd(zm, dy_sb, a_sb, ge_token, sbh, name="sb_bwd")

    def mlout_bwd_fn(dyv, hv, o, g):
        sg = _sigmoid(o)
        dn = dyv * sg
        dxs, dgs, ys = [], [], []
        for k in range(hh):
            sl = slice(k * dh, (k + 1) * dh)
            ys.append(_rms_fwd(hv[:, sl], g[:, sl]))
            dxk, dgk = _rms_bwd(hv[:, sl], g[:, sl], dn[:, sl])
            dxs.append(dxk)
            dgs.append(dgk)
        do = dyv * jnp.concatenate(ys, axis=1) * sg * (1.0 - sg)
        return jnp.concatenate(dxs, axis=1), do, jnp.sum(jnp.concatenate(dgs, axis=1), axis=0, keepdims=True)

    dhm, dmlo, dg_mln = _rowwise(mlout_bwd_fn, [dy_ml, hm, (zm, d, 6)], [ml_norm_g], [(d, F32), (d, BF16)], [d],
                                 name="ml_out_bwd")
    dmqk, dmlv, dgc, dgr = _ml_bwd(mqk, zm, 5 * d, gcol, grow, cst, nst, mst, dhm, d, name="ml_bwd")
    dmlqk, dconv_w, dconv_b = _conv_bwd(zm, 3 * d, 2 * d, conv_wf, conv_b, dmqk, name="conv_bwd")
    dgr_t = jnp.pad(dgr.transpose(1, 0, 2).reshape(8, s).T, ((0, 0), (0, LANES - 8)))

    def gate_bwd_fn(a, b, z, bias):
        tot = a + b
        rows_t = tot.shape[0]
        r = lax.broadcasted_iota(jnp.int32, (rows_t, rows_t), 0)
        c = lax.broadcasted_iota(jnp.int32, (rows_t, rows_t), 1)
        sh = CHUNK.bit_length() - 1
        same_chunk = jnp.right_shift(r, sh) == jnp.right_shift(c, sh)
        dlf = _u01dot(((c >= r) & same_chunk).astype(BF16), tot)
        lane = lax.broadcasted_iota(jnp.int32, tot.shape, 1)
        dz = jnp.where(lane < hh, tot, jnp.where(lane < 2 * hh, dlf * _sigmoid(-(z + bias)), 0.0))
        return dz, jnp.sum(dz, axis=0, keepdims=True)

    dzif, db_if_p = _rowwise(gate_bwd_fn, [dgc, dgr_t, zif], [b_if_p], [(LANES, BF16)], [LANES], name="ml_gates_bwd",
                             tr=8 * CHUNK)
    dzm = jnp.concatenate([dsq, dsk, dsv, dmlqk, dmlv, dmlo, dxq, dgate], axis=1)
    dw_main = _mm(hn, dzm, ta=True, out_dtype=BF16, name="proj_in_dw")
    dw_if = _mm(hn, dzif, ta=True, out_dtype=BF16, name="proj_if_dw")
    dw_in = jnp.concatenate([dw_main[:, :7 * d], dw_if[:, :2 * hh], dw_main[:, 7 * d:]], axis=1)
    late = [to_parts(uncols(dw_in))]
    gl_send, gl_recv, gl_src, gl_land, gl_token = _split_start(
        "grads", late, [(8,) + a.shape[2:] for a in late], dw_if, name="exchange_late_start")
    dhn = _mm(dzm, w_main, tb=True, after=gl_token, name="proj_in_dx")
    dhn = _mm(dzif, w_if, tb=True, tiles=[dhn], name="proj_if_dx")
    dx, dg_mix = _rowwise(norm_bwd_fn, [x2, dhn, dx1], [g_mix], [(d, F32)], [d], name="norm_in_bwd")

    own = lambda p: lax.dynamic_index_in_dim(lax.dynamic_index_in_dim(p, k4, 0, keepdims=False),
                                             lax.axis_index("c"), 0, keepdims=False)

    def finish(tag, send, recv, src, land, parts, after, ws, ms, vs):
        land = _split_wait("grads", send, recv, src, land, after, name=f"exchange_{tag}_wait")
        got = [lax.dynamic_update_index_in_dim(ld, own(p), me, 0) for ld, p in zip(land, parts)]
        halves = [_sum8(r, name=f"sum_grads_{tag}{i}") for i, r in enumerate(got)]
        both = _swap_halves(halves, name=f"swap_halves_{tag}")
        gs = [b.reshape(2 * b.shape[1], b.shape[2]) for b in both]
        return gs, [_adamw(w, g, m, v, name=f"adamw_{tag}{i}") for i, (w, g, m, v) in enumerate(zip(ws, gs, ms, vs))]

    first = lambda arrs: [a[0] for a in arrs]
    g_early, out_early = finish(
        "early", ge_send, ge_recv, ge_src, ge_land, early, [dx],
        first([w_mem_kv, w_sb_proj, w_ml_proj, w_x_proj, w_out, w_ff1, w_ff2]),
        first([m_w_mem_kv, m_w_sb_proj, m_w_ml_proj, m_w_x_proj, m_w_out, m_w_ff1, m_w_ff2]),
        first([v_w_mem_kv, v_w_sb_proj, v_w_ml_proj, v_w_x_proj, v_w_out, v_w_ff1, v_w_ff2]))
    g_late, out_late = finish(
        "late", gl_send, gl_recv, gl_src, gl_land, late, [o[0] for o in out_early],
        first([w_in]), first([m_w_in]), first([v_w_in]))
    g_big = [g[None] for g in g_late + g_early]
    big_out = [[o[None] for o in outs] for outs in out_late + out_early]

    small_g = [dg_mix, db_if_p[:, :2 * hh], db_gate, dconv_w, dconv_b, dg_mln, dg_mem, dg_qn, dg_kn, dg_mlp,
               jnp.sum(loss_cols).reshape(1, 1)]
    n_small = sum(a.size for a in small_g)
    rows = -(-n_small // (8 * LANES)) * 8
    g_small = _unpack(_allreduce_small(_pack(small_g, rows), out_late[0][0], name="allreduce_small"), small_g)
    loss = g_small[-1].reshape(())
    qw = conv_w.shape[2]
    g_conv_w = lax.dynamic_slice_in_dim(g_small[3], k4 * qw, qw, axis=1)
    g_small_w = [g_small[0], g_small[1], g_small[2], g_conv_w] + g_small[4:10]
    sm_w = [g_mix, b_if, b_gate, conv_w[0], conv_b, ml_norm_g, g_mem, q_norm_g, k_norm_g, g_mlp]
    sm_m = [m_g_mix, m_b_if, m_b_gate, m_conv_w[0], m_conv_b, m_ml_norm_g, m_g_mem, m_q_norm_g, m_k_norm_g, m_g_mlp]
    sm_v = [v_g_mix, v_b_if, v_b_gate, v_conv_w[0], v_conv_b, v_ml_norm_g, v_g_mem, v_q_norm_g, v_k_norm_g, v_g_mlp]
    n_sw = sum(a.size for a in sm_w)
    rows_w = -(-n_sw // (8 * LANES)) * 8
    sm_out = _adamw(_pack(sm_w, rows_w), _pack(g_small_w, rows_w), _pack(sm_m, rows_w), _pack(sm_v, rows_w),
                    name="adamw_small")
    sm_delta, sm_newm, sm_newv = [_unpack(p, sm_w) for p in sm_out]

    order = ["g_mix", "w_in", "b_if", "b_gate", "conv_w", "conv_b", "ml_norm_g", "g_mem", "w_mem_kv", "q_norm_g",
             "k_norm_g", "w_sb_proj", "w_ml_proj", "w_x_proj", "w_out", "g_mlp", "w_ff1", "w_ff2"]
    small_names = ["g_mix", "b_if", "b_gate", "conv_w", "conv_b", "ml_norm_g", "g_mem", "q_norm_g", "k_norm_g", "g_mlp"]
    big_names = ["w_in", "w_mem_kv", "w_sb_proj", "w_ml_proj", "w_x_proj", "w_out", "w_ff1", "w_ff2"]
    grads, deltas, new_m, new_v = {}, {}, {}, {}
    for i, nme in enumerate(small_names):
        shp = sm_w[i].shape if nme != "conv_w" else conv_w.shape
        grads[nme] = g_small_w[i].reshape(shp)
        deltas[nme], new_m[nme], new_v[nme] = (sm_delta[i].reshape(shp), sm_newm[i].reshape(shp),
                                               sm_newv[i].reshape(shp))
    for i, nme in enumerate(big_names):
        grads[nme] = g_big[i]
        deltas[nme], new_m[nme], new_v[nme] = big_out[i]
    return (loss, dx[None], *[grads[k] for k in order], *[deltas[k] for k in order], *[new_m[k] for k in order],
            *[new_v[k] for k in order])
```

```python
import functools

import jax
import jax.numpy as jnp
from jax import lax
from jax.experimental import pallas as pl
from jax.experimental.pallas import tpu as pltpu

F32 = jnp.float32
BF16 = jnp.bfloat16
MESH = pl.DeviceIdType.MESH

EPS = 1e-6
SB_HD = 128
SB_SLOTS = 8
ML_HEADS = 4
X_HEADS = 4
CHUNK = 64
CONV_W = 4
LANES = 128
ADAM_LR = 0.001
ADAM_B1 = 0.9
ADAM_B2 = 0.999
ADAM_EPS = 1e-08
ADAM_WD = 0.01
ADAM_STEP = 10
VMEM_CAP = 56 * 1024 * 1024
NEG = -1e30

NT = (((1,), (1,)), ((), ()))
NN = (((1,), (0,)), ((), ()))
TN = (((0,), (0,)), ((), ()))


def _dot(a, b, dn=NN):
    return lax.dot_general(a.astype(BF16), b.astype(BF16), dn, preferred_element_type=F32)


def _dot01(x, u, dn=NN):
    hi = x.astype(BF16)
    lo = (x - hi.astype(F32)).astype(BF16)
    return (lax.dot_general(hi, u, dn, preferred_element_type=F32)
            + lax.dot_general(lo, u, dn, preferred_element_type=F32))


def _u01dot(u, x):
    hi = x.astype(BF16)
    lo = (x - hi.astype(F32)).astype(BF16)
    return (lax.dot_general(u, hi, NN, preferred_element_type=F32)
            + lax.dot_general(u, lo, NN, preferred_element_type=F32))


def _pick(n, cands):
    for c in cands:
        if c <= n and n % c == 0:
            return c
    return n


def _nbytes(shape, dtype):
    n = 1
    for s in shape:
        n *= s
    return n * jnp.dtype(dtype).itemsize


def _params(vmem_bytes):
    return pltpu.CompilerParams(vmem_limit_bytes=int(min(VMEM_CAP, max(vmem_bytes, 16 * 1024 * 1024))))


def _hbm(a):
    return pltpu.with_memory_space_constraint(a, pltpu.HBM)


def _softplus(z):
    return jnp.maximum(z, 0.0) + jnp.log(1.0 + jnp.exp(-jnp.abs(z)))


def _sigmoid(z):
    return 1.0 / (1.0 + jnp.exp(-z))


def _rms_fwd(xv, g):
    r = lax.rsqrt(jnp.mean(xv * xv, axis=-1, keepdims=True) + EPS)
    return xv * r * g


def _rms_bwd(xv, g, dy):
    r = lax.rsqrt(jnp.mean(xv * xv, axis=-1, keepdims=True) + EPS)
    xh = xv * r
    dxh = dy * g
    dx = r * (dxh - xh * jnp.mean(dxh * xh, axis=-1, keepdims=True))
    return dx, dy * xh


def _mm(a, b, *, name, ta=False, tb=False, tiles=(), post=None, out_dtype=F32, bm=1024, bn=1024, bk=1024, after=None):
    m, k = (a.shape[1], a.shape[0]) if ta else a.shape
    n = b.shape[0] if tb else b.shape[1]
    tm = _pick(m, (bm, 512, 256, 128))
    tn = _pick(n, (bn, 512, 256, 128))
    tk = _pick(k, (bk, 512, 256, 128))
    nk = k // tk
    if (m // tm) * (n // tn) * nk < 8 and tm % 256 == 0:
        tm //= 2
    dn = (((0 if ta else 1,), (1 if tb else 0,)), ((), ()))
    dts = out_dtype if isinstance(out_dtype, tuple) else (out_dtype,)
    nt, no = len(tiles), len(dts)
    if post is None:
        post = lambda r, *ts: sum((t.astype(F32) for t in ts), r)

    def body(*refs):
        a_ref, b_ref = refs[:2]
        t_refs = refs[2:2 + nt]
        o_refs = refs[2 + nt + (after is not None):2 + nt + (after is not None) + no]
        part = lax.dot_general(a_ref[...].astype(BF16), b_ref[...].astype(BF16), dn, preferred_element_type=F32)

        def finish(r):
            res = post(r, *[t[...] for t in t_refs])
            res = res if isinstance(res, tuple) else (res,)
            for o, v in zip(o_refs, res):
                o[...] = v.astype(o.dtype)

        if nk == 1:
            finish(part)
        else:
            acc_ref = refs[-1]
            kk = pl.program_id(2)

            @pl.when(kk == 0)
            def _():
                acc_ref[...] = part

            @pl.when(kk > 0)
            def _():
                acc_ref[...] += part

            @pl.when(kk == nk - 1)
            def _():
                finish(acc_ref[...])

    a_spec = pl.BlockSpec((tk, tm), lambda i, j, q: (q, i)) if ta else pl.BlockSpec((tm, tk), lambda i, j, q: (i, q))
    b_spec = pl.BlockSpec((tn, tk), lambda i, j, q: (j, q)) if tb else pl.BlockSpec((tk, tn), lambda i, j, q: (q, j))
    o_spec = pl.BlockSpec((tm, tn), lambda i, j, q: (i, j))
    ins, specs = [_hbm(a), _hbm(b)] + [_hbm(t) for t in tiles], [a_spec, b_spec] + [o_spec] * nt
    vm = 2 * (_nbytes((tm, tk), a.dtype) + _nbytes((tk, tn), b.dtype)) + 3 * _nbytes((tm, tn), F32) \
        + _nbytes((tm, tk), BF16) + _nbytes((tk, tn), BF16) \
        + 2 * sum(_nbytes((tm, tn), t.dtype) for t in tiles) + 2 * sum(_nbytes((tm, tn), dt) for dt in dts)
    if after is not None:
        ins.append(after)
        specs.append(ANY)
    res = pl.pallas_call(
        body, name=name, grid=(m // tm, n // tn, nk), in_specs=specs, out_specs=[o_spec] * no,
        out_shape=[pltpu.HBM((m, n), dt) for dt in dts], scratch_shapes=[pltpu.VMEM((tm, tn), F32)] if nk > 1 else [],
        compiler_params=_params(vm + (4 << 20)),
    )(*ins)
    return res[0] if no == 1 else tuple(res)


def _rowwise(fn, rows, consts, outs, reds=(), *, name, tr=256, temps=6):
    rows = [r if isinstance(r, tuple) else (r, r.shape[1], 0) for r in rows]
    nrows = rows[0][0].shape[0]
    t = _pick(nrows, (tr, 128, 64, 32, 16, 8))
    nr, nc, no = len(rows), len(consts), len(outs)

    def body(*refs):
        rin, cin = refs[:nr], refs[nr:nr + nc]
        oref, rref = refs[nr + nc:nr + nc + no], refs[nr + nc + no:]
        res = fn(*[r[...] for r in rin], *[c[...] for c in cin])
        if not isinstance(res, (tuple, list)):
            res = (res,)
        for o, v in zip(oref, res[:no]):
            o[...] = v.astype(o.dtype)
        if rref:
            @pl.when(pl.program_id(0) == 0)
            def _():
                for r in rref:
                    r[...] = jnp.zeros_like(r)

            for r, v in zip(rref, res[no:]):
                r[...] += v

    in_specs = [pl.BlockSpec((t, w), functools.partial(lambda i, ci: (i, ci), ci=ci)) for (_, w, ci) in rows]
    in_specs += [pl.BlockSpec(c.shape, functools.partial(lambda i, nd: (0,) * nd, nd=c.ndim)) for c in consts]
    out_specs = [pl.BlockSpec((t, w), lambda i: (i, 0)) for (w, _) in outs]
    out_specs += [pl.BlockSpec((1, w), lambda i: (0, 0)) for w in reds]
    out_shape = [pltpu.HBM((nrows, w), dt) for (w, dt) in outs]
    out_shape += [jax.ShapeDtypeStruct((1, w), F32) for w in reds]
    widest = max([w for (_, w, _) in rows] + [w for (w, _) in outs])
    vm = 2 * sum(_nbytes((t, w), a.dtype) for (a, w, _) in rows) + 2 * sum(_nbytes((t, w), dt) for (w, dt) in outs)
    vm += temps * _nbytes((t, widest), F32) + (2 << 20)
    res = pl.pallas_call(
        body, name=name, grid=(nrows // t,), in_specs=in_specs, out_specs=out_specs, out_shape=out_shape,
        compiler_params=_params(vm),
    )(*[_hbm(a) for (a, _, _) in rows], *consts)
    return list(res)


def _sb_tiles(s, tq, tk):
    tq = _pick(s, (tq, 256, 128))
    tk = _pick(tq, (tk, 128))
    return tq, tk, tq // tk


def _sb_fwd(zm, heads, *, name, tq=512, tk=256):
    s = zm.shape[0]
    tq, tk, nd = _sb_tiles(s, tq, tk)
    scale = SB_HD ** -0.5

    def body(q_ref, k_ref, v_ref, o_ref, a_out, stage, sem):
        h, i = pl.program_id(0), pl.program_id(1)
        qb = (q_ref[...] * scale).astype(BF16)
        r = lax.broadcasted_iota(jnp.int32, (tq, tk), 0)
        c = lax.broadcasted_iota(jnp.int32, (tq, tk), 1)
        ur = lax.broadcasted_iota(jnp.int32, (tk, tk), 0)
        uc = lax.broadcasted_iota(jnp.int32, (tk, tk), 1)
        usuf = (ur > uc).astype(BF16)

        def out_copy(slot, j):
            return pltpu.make_async_copy(stage.at[slot], a_out.at[h, i, j], sem.at[slot])

        def tile(j, carry, causal, slot, reuse):
            acc, cl = carry
            if reuse is True:
                out_copy(slot, 0).wait()
            elif reuse is not None:
                @pl.when(reuse)
                def _():
                    out_copy(slot, 0).wait()
            rows = pl.ds(pl.multiple_of(j * tk, tk), tk)
            kb = k_ref[rows, :].astype(BF16)
            vb = v_ref[rows, :].astype(BF16)
            z = lax.dot_general(qb, kb, NT, preferred_element_type=F32)
            lsig = -_softplus(z)
            l = lsig if causal is None else jnp.where(causal, lsig, 0.0)
            loga = z + lsig + _dot01(l, usuf) + cl
            if causal is not None:
                loga = jnp.where(causal, loga, NEG)
            ab = jnp.exp(loga).astype(BF16)
            acc = acc + lax.dot_general(ab, vb, NN, preferred_element_type=F32)
            stage[slot] = ab
            out_copy(slot, j).start()
            return acc, cl + jnp.sum(l, axis=1, keepdims=True)

        carry = (jnp.zeros((tq, SB_HD), F32), jnp.zeros((tq, 1), F32))
        for n, dd in enumerate(range(nd - 1, -1, -1)):
            carry = tile(i * nd + dd, carry, c + dd * tk < r, n, None)

        def rest(n, cr):
            return tile(i * nd - 1 - n, cr, None, (nd + n) % SB_SLOTS, nd + n >= SB_SLOTS)

        acc, _ = lax.fori_loop(0, i * nd, rest, carry)
        total = (i + 1) * nd
        for back in range(1, SB_SLOTS + 1):
            @pl.when(total >= back)
            def _():
                out_copy((total - back) % SB_SLOTS, 0).wait()

        o_ref[...] = acc.astype(o_ref.dtype)

    assert nd <= SB_SLOTS
    blk = lambda off: pl.BlockSpec((s, SB_HD), functools.partial(lambda h, i, off: (0, off + h), off=off))
    return pl.pallas_call(
        body, name=name, grid=(heads, s // tq),
        in_specs=[pl.BlockSpec((tq, SB_HD), lambda h, i: (i, h)), blk(heads), blk(2 * heads)],
        out_specs=[pl.BlockSpec((tq, SB_HD), lambda h, i: (i, h)), ANY],
        out_shape=[pltpu.HBM((s, heads * SB_HD), BF16), pltpu.HBM((heads, s // tq, s // tk, tq, tk), BF16)],
        scratch_shapes=[pltpu.VMEM((SB_SLOTS, tq, tk), BF16), pltpu.SemaphoreType.DMA((SB_SLOTS,))],
        compiler_params=_params(8 * s * SB_HD * 4 + 24 * tq * tk * 4 + (8 << 20)),
    )(_hbm(zm), _hbm(zm), _hbm(zm))


def _sb_bwd(zm, dy, a_all, after, heads, *, name, tq=512, tk=256):
    s = zm.shape[0]
    tq, tk, nd = _sb_tiles(s, tq, tk)
    nq = s // tq
    scale = SB_HD ** -0.5

    def body(q_ref, k_ref, v_ref, do_ref, a_in, after_ref, dq_ref, dk_ref, dv_ref, dka, dva, abuf, sem):
        h, i = pl.program_id(0), pl.program_id(1)

        @pl.when(i == 0)
        def _():
            dka[...] = jnp.zeros_like(dka)
            dva[...] = jnp.zeros_like(dva)

        qb = (q_ref[...] * scale).astype(BF16)
        dob = do_ref[...].astype(BF16)
        r = lax.broadcasted_iota(jnp.int32, (tq, tk), 0)
        c = lax.broadcasted_iota(jnp.int32, (tq, tk), 1)
        ur = lax.broadcasted_iota(jnp.int32, (tk, tk), 0)
        uc = lax.broadcasted_iota(jnp.int32, (tk, tk), 1)
        uexcl = (ur < uc).astype(BF16)

        def fetch(j, slot):
            return pltpu.make_async_copy(a_in.at[h, i, j], abuf.at[slot], sem.at[slot])

        total = (i + 1) * nd
        ahead = SB_SLOTS - 1

        def tile(j, carry, causal):
            dq, cg = carry
            slot = j % SB_SLOTS
            fetch(j, slot).wait()

            @pl.when(j + ahead < total)
            def _():
                fetch(j + ahead, (j + ahead) % SB_SLOTS).start()

            rows = pl.ds(pl.multiple_of(j * tk, tk), tk)
            kb = k_ref[rows, :].astype(BF16)
            vb = v_ref[rows, :].astype(BF16)
            z = lax.dot_general(qb, kb, NT, preferred_element_type=F32)
            sig = 1.0 / (1.0 + jnp.exp(-z))
            ab = abuf[slot]
            g = ab.astype(F32) * lax.dot_general(dob, vb, NT, preferred_element_type=F32)
            p = cg + lax.dot_general(g.astype(BF16), uexcl, NN, preferred_element_type=F32)
            dz = g - sig * (g + p)
            if causal is not None:
                dz = jnp.where(causal, dz, 0.0)
            dzb = dz.astype(BF16)
            dva[rows, :] += lax.dot_general(ab, dob, TN, preferred_element_type=F32)
            dka[rows, :] += lax.dot_general(dzb, qb, TN, preferred_element_type=F32)
            dq = dq + lax.dot_general(dzb, kb, NN, preferred_element_type=F32)
            return dq, cg + jnp.sum(g, axis=1, keepdims=True)

        for first in range(ahead):
            @pl.when(first < total)
            def _():
                fetch(first, first).start()

        init = (jnp.zeros((tq, SB_HD), F32), jnp.zeros((tq, 1), F32))
        carry = lax.fori_loop(0, i * nd, lambda j, cr: tile(j, cr, None), init)
        for dd in range(nd):
            carry = tile(i * nd + dd, carry, c + dd * tk < r)
        dq_ref[...] = (carry[0] * scale).astype(dq_ref.dtype)

        @pl.when(i == nq - 1)
        def _():
            dk_ref[...] = dka[...].astype(dk_ref.dtype)
            dv_ref[...] = dva[...].astype(dv_ref.dtype)

    blk = lambda off: pl.BlockSpec((s, SB_HD), functools.partial(lambda h, i, off: (0, off + h), off=off))
    tile_spec = pl.BlockSpec((tq, SB_HD), lambda h, i: (i, h))
    full = pltpu.HBM((s, heads * SB_HD), BF16)
    return pl.pallas_call(
        body, name=name, grid=(heads, nq),
        in_specs=[tile_spec, blk(heads), blk(2 * heads), tile_spec, ANY, ANY],
        out_specs=[tile_spec, blk(0), blk(0)],
        out_shape=[full, full, full],
        scratch_shapes=[pltpu.VMEM((s, SB_HD), F32), pltpu.VMEM((s, SB_HD), F32),
                        pltpu.VMEM((SB_SLOTS, tq, tk), BF16), pltpu.SemaphoreType.DMA((SB_SLOTS,))],
        compiler_params=_params(12 * s * SB_HD * 4 + 32 * tq * tk * 4 + (8 << 20)),
    )(_hbm(zm), _hbm(zm), _hbm(zm), _hbm(dy), a_all, after)


def _conv_taps(u, w_ref, rows_i):
    taps = []
    for j in range(CONV_W):
        sh = CONV_W - 1 - j
        if sh == 0:
            taps.append(u)
        else:
            taps.append(jnp.where(rows_i >= sh, pltpu.roll(u, sh, 0), 0.0))
    return taps


def _conv_fwd(zm, col0, width, cw, cb, *, name):
    s = zm.shape[0]
    bw = _pick(width, (LANES,))
    off = col0 // bw

    def body(u_ref, w_ref, b_ref, o_ref):
        u = u_ref[...]
        rows_i = lax.broadcasted_iota(jnp.int32, u.shape, 0)
        acc = jnp.broadcast_to(b_ref[...], u.shape)
        for j, tp in enumerate(_conv_taps(u, w_ref, rows_i)):
            acc = acc + tp * w_ref[j:j + 1, :]
        o_ref[...] = acc * _sigmoid(acc)

    return pl.pallas_call(
        body, name=name, grid=(width // bw,),
        in_specs=[pl.BlockSpec((s, bw), lambda j: (0, off + j)), pl.BlockSpec((CONV_W, bw), lambda j: (0, j)),
                  pl.BlockSpec((1, bw), lambda j: (0, j))],
        out_specs=pl.BlockSpec((s, bw), lambda j: (0, j)),
        out_shape=pltpu.HBM((s, width), F32),
        compiler_params=_params(12 * s * bw * 4 + (4 << 20)),
    )(_hbm(zm), cw, cb)


def _conv_bwd(zm, col0, width, cw, cb, dqk, *, name):
    s = zm.shape[0]
    bw = _pick(width, (LANES,))
    off = col0 // bw

    def body(u_ref, w_ref, b_ref, d_ref, du_ref, dw_ref, db_ref):
        u = u_ref[...]
        rows_i = lax.broadcasted_iota(jnp.int32, u.shape, 0)
        taps = _conv_taps(u, w_ref, rows_i)
        acc = jnp.broadcast_to(b_ref[...], u.shape)
        for j, tp in enumerate(taps):
            acc = acc + tp * w_ref[j:j + 1, :]
        sg = _sigmoid(acc)
        dc = d_ref[...] * (sg * (1.0 + acc * (1.0 - sg)))
        du = jnp.zeros_like(u)
        for j in range(CONV_W):
            sh = CONV_W - 1 - j
            if sh == 0:
                du = du + dc * w_ref[j:j + 1, :]
            else:
                du = du + jnp.where(rows_i < s - sh, pltpu.roll(dc, s - sh, 0), 0.0) * w_ref[j:j + 1, :]
            dw_ref[j:j + 1, :] = jnp.sum(dc * taps[j], axis=0, keepdims=True)
        du_ref[...] = du.astype(du_ref.dtype)
        db_ref[...] = jnp.sum(dc, axis=0, keepdims=True)

    return pl.pallas_call(
        body, name=name, grid=(width // bw,),
        in_specs=[pl.BlockSpec((s, bw), lambda j: (0, off + j)), pl.BlockSpec((CONV_W, bw), lambda j: (0, j)),
                  pl.BlockSpec((1, bw), lambda j: (0, j)), pl.BlockSpec((s, bw), lambda j: (0, j))],
        out_specs=[pl.BlockSpec((s, bw), lambda j: (0, j)), pl.BlockSpec((CONV_W, bw), lambda j: (0, j)),
                   pl.BlockSpec((1, bw), lambda j: (0, j))],
        out_shape=[pltpu.HBM((s, width), BF16), pltpu.HBM((CONV_W, width), F32),
                   pltpu.HBM((1, width), F32)],
        compiler_params=_params(20 * s * bw * 4 + (4 << 20)),
    )(_hbm(zm), cw, cb, _hbm(dqk))


def _ml_gates(gcol_ref, grow_ref):
    l = CHUNK
    r = lax.broadcasted_iota(jnp.int32, (l, l), 0)
    c = lax.broadcasted_iota(jnp.int32, (l, l), 1)
    gcol = gcol_ref[...]
    grow = grow_ref[0]
    bcol = _u01dot((c <= r).astype(BF16), gcol)
    brow = _dot01(grow, (r <= c).astype(BF16))
    return gcol, grow, bcol, brow, r >= c


def _ml_chunk(h, dh, mq_ref, mk_ref, v_ref, gates, cp, n_prev, m_prev):
    gcol, grow, bcol, brow, tri = gates
    l = CHUNK
    sl = slice(h * dh, (h + 1) * dh)
    qc = mq_ref[:, sl]
    kc = mk_ref[:, sl] * (dh ** -0.5)
    vc = v_ref[:, sl]
    i_row = grow[h:h + 1, :]
    i_col = gcol[:, h:h + 1]
    b_col = bcol[:, ML_HEADS + h:ML_HEADS + h + 1]
    b_row = brow[ML_HEADS + h:ML_HEADS + h + 1, :]
    b_end = b_col[l - 1:l, :]
    d = jnp.where(tri, b_col - b_row + i_row, -jnp.inf)
    m_inter = b_col + m_prev
    m_t = jnp.maximum(m_inter, jnp.max(d, axis=1, keepdims=True))
    w = jnp.exp(d - m_t)
    s_inter = jnp.exp(m_inter - m_t)
    qb, kb, vb = qc.astype(BF16), kc.astype(BF16), vc.astype(BF16)
    cpb = cp.astype(BF16)
    a = lax.dot_general(qb, kb, NT, preferred_element_type=F32)
    sc = a * w
    qcp = lax.dot_general(qb, cpb, NT, preferred_element_type=F32)
    qn = jnp.sum(qc * n_prev, axis=1, keepdims=True)
    num = lax.dot_general(sc.astype(BF16), vb, NN, preferred_element_type=F32) + s_inter * qcp
    den = jnp.sum(sc, axis=1, keepdims=True) + s_inter * qn
    floor = jnp.exp(-m_t)
    dnm = jnp.maximum(jnp.abs(den), floor)
    g_col = b_end - b_col + i_col
    g_row = b_end - b_row + i_row
    m_new = jnp.maximum(b_end + m_prev, jnp.max(g_row, axis=1, keepdims=True))
    decay = jnp.exp(b_end + m_prev - m_new)
    wk = jnp.exp(g_col - m_new)
    return dict(qc=qc, kc=kc, vc=vc, qb=qb, kb=kb, vb=vb, cpb=cpb, w=w, s_inter=s_inter, a=a, sc=sc, qcp=qcp, qn=qn,
                num=num, den=den, floor=floor, dnm=dnm, m_new=m_new, decay=decay, wk=wk, sl=sl)


def _ml_fwd(mqk, zm, vcol, gcol, grow, d_model, *, name):
    s = zm.shape[0]
    nc = s // CHUNK
    dh = d_model // ML_HEADS
    hh = ML_HEADS

    def body(mq_ref, mk_ref, v_ref, gcol_ref, grow_ref, h_ref, cs_ref, ns_ref, ms_ref, c_s, n_s, m_s):
        @pl.when(pl.program_id(0) == 0)
        def _():
            c_s[...] = jnp.zeros_like(c_s)
            n_s[...] = jnp.zeros_like(n_s)
            m_s[...] = jnp.zeros_like(m_s)

        gates = _ml_gates(gcol_ref, grow_ref)
        for h in range(hh):
            cp, n_prev, m_prev = c_s[h], n_s[h], m_s[h][:, 0:1]
            cs_ref[0, h] = cp
            ns_ref[0, h] = n_prev
            ms_ref[0, h] = m_s[h]
            f = _ml_chunk(h, dh, mq_ref, mk_ref, v_ref, gates, cp, n_prev, m_prev)
            h_ref[:, f["sl"]] = f["num"] / f["dnm"]
            c_s[h] = f["decay"] * cp + lax.dot_general((f["vc"] * f["wk"]).astype(BF16), f["kb"], TN,
                                                       preferred_element_type=F32)
            n_s[h] = f["decay"] * n_prev + jnp.sum(f["wk"] * f["kc"], axis=0, keepdims=True)
            m_s[h] = jnp.broadcast_to(f["m_new"], (1, LANES))

    dblk = d_model
    return pl.pallas_call(
        body, name=name, grid=(nc,),
        in_specs=[pl.BlockSpec((CHUNK, dblk), lambda c: (c, 0)), pl.BlockSpec((CHUNK, dblk), lambda c: (c, 1)),
                  pl.BlockSpec((CHUNK, dblk), lambda c: (c, vcol // dblk)),
                  pl.BlockSpec((CHUNK, LANES), lambda c: (c, 0)), pl.BlockSpec((1, 8, CHUNK), lambda c: (c, 0, 0))],
        out_specs=[pl.BlockSpec((CHUNK, dblk), lambda c: (c, 0)),
                   pl.BlockSpec((1, hh, dh, dh), lambda c: (c, 0, 0, 0)),
                   pl.BlockSpec((1, hh, 1, dh), lambda c: (c, 0, 0, 0)),
                   pl.BlockSpec((1, hh, 1, LANES), lambda c: (c, 0, 0, 0))],
        out_shape=[pltpu.HBM((s, d_model), F32), pltpu.HBM((nc, hh, dh, dh), F32),
                   pltpu.HBM((nc, hh, 1, dh), F32), pltpu.HBM((nc, hh, 1, LANES), F32)],
        scratch_shapes=[pltpu.VMEM((hh, dh, dh), F32), pltpu.VMEM((hh, 1, dh), F32), pltpu.VMEM((hh, 1, LANES), F32)],
        compiler_params=_params(8 * hh * dh * dh * 4 + (16 << 20)),
    )(_hbm(mqk), _hbm(mqk), _hbm(zm), _hbm(gcol), _hbm(grow))


def _ml_bwd(mqk, zm, vcol, gcol, grow, cs, ns, ms, dhm, d_model, *, name):
    s = zm.shape[0]
    nc = s // CHUNK
    dh = d_model // ML_HEADS
    hh = ML_HEADS
    l = CHUNK

    def body(mq_ref, mk_ref, v_ref, gcol_ref, grow_ref, cs_ref, ns_ref, ms_ref, dh_ref,
             dqk_ref, dv_ref, dgc_ref, dgr_ref, dc_s, dn_s):
        @pl.when(pl.program_id(0) == 0)
        def _():
            dc_s[...] = jnp.zeros_like(dc_s)
            dn_s[...] = jnp.zeros_like(dn_s)

        gates = _ml_gates(gcol_ref, grow_ref)
        lane = lax.broadcasted_iota(jnp.int32, (l, LANES), 1)
        rowi = lax.broadcasted_iota(jnp.int32, (8, l), 0)
        lastrow = lax.broadcasted_iota(jnp.int32, (l, 1), 0) == l - 1
        dgc = jnp.zeros((l, LANES), F32)
        dgr = jnp.zeros((8, l), F32)
        for h in range(hh):
            cp, n_prev, m_prev = cs_ref[0, h], ns_ref[0, h], ms_ref[0, h][:, 0:1]
            f = _ml_chunk(h, dh, mq_ref, mk_ref, v_ref, gates, cp, n_prev, m_prev)
            dC, dn = dc_s[h], dn_s[h]
            dhv = dh_ref[:, f["sl"]]
            dnum = dhv / f["dnm"]
            hv = f["num"] / f["dnm"]
            ddnm = -jnp.sum(dhv * hv, axis=1, keepdims=True) / f["dnm"]
            dden = jnp.where(jnp.abs(f["den"]) >= f["floor"], ddnm * jnp.sign(f["den"]), 0.0)
            dnb = dnum.astype(BF16)
            dsc = lax.dot_general(dnb, f["vb"], NT, preferred_element_type=F32) + dden
            dvc = lax.dot_general(f["sc"].astype(BF16), dnb, TN, preferred_element_type=F32)
            ds_inter = jnp.sum(dnum * f["qcp"], axis=1, keepdims=True) + dden * f["qn"]
            sdn = (f["s_inter"] * dnum).astype(BF16)
            sdd = f["s_inter"] * dden
            da = dsc * f["w"]
            dab = da.astype(BF16)
            dqc = (lax.dot_general(dab, f["kb"], NN, preferred_element_type=F32)
                   + lax.dot_general(sdn, f["cpb"], NN, preferred_element_type=F32) + sdd * n_prev)
            dcp = f["decay"] * dC + lax.dot_general(sdn, f["qb"], TN, preferred_element_type=F32)
            dnp = f["decay"] * dn + jnp.sum(sdd * f["qc"], axis=0, keepdims=True)
            vw = (f["vc"] * f["wk"]).astype(BF16)
            dCb = dC.astype(BF16)
            dkc = (lax.dot_general(dab, f["qb"], TN, preferred_element_type=F32)
                   + lax.dot_general(vw, dCb, NN, preferred_element_type=F32) + f["wk"] * dn)
            e = lax.dot_general(f["kb"], dCb, NT, preferred_element_type=F32)
            dvc = dvc + e * f["wk"]
            dwk = jnp.sum(e * f["vc"], axis=1, keepdims=True) + jnp.sum(f["kc"] * dn, axis=1, keepdims=True)
            ddecay = jnp.sum(jnp.sum(dC * cp, axis=1, keepdims=True), axis=0, keepdims=True) \
                + jnp.sum(dn * n_prev, axis=1, keepdims=True)
            dd = dsc * f["sc"]
            dlw = dwk * f["wk"]
            db_end = jnp.sum(dlw, axis=0, keepdims=True) + ddecay * f["decay"]
            di_col = dlw
            db_col = jnp.sum(dd, axis=1, keepdims=True) + ds_inter * f["s_inter"] - dlw \
                + jnp.where(lastrow, db_end, 0.0)
            cs_dd = jnp.sum(dd, axis=0, keepdims=True)
            dgc = dgc + jnp.where(lane == h, di_col, 0.0) + jnp.where(lane == hh + h, db_col, 0.0)
            dgr = dgr + jnp.where(rowi == h, cs_dd, 0.0) - jnp.where(rowi == hh + h, cs_dd, 0.0)
            dqk_ref[:, f["sl"]] = dqc
            dqk_ref[:, d_model + h * dh:d_model + (h + 1) * dh] = dkc * (dh ** -0.5)
            dv_ref[:, f["sl"]] = dvc.astype(dv_ref.dtype)
            dc_s[h] = dcp
            dn_s[h] = dnp
        dgc_ref[...] = dgc
        dgr_ref[0] = dgr

    dblk = d_model
    rev = lambda c: nc - 1 - c
    return pl.pallas_call(
        body, name=name, grid=(nc,),
        in_specs=[pl.BlockSpec((l, dblk), lambda c: (rev(c), 0)), pl.BlockSpec((l, dblk), lambda c: (rev(c), 1)),
                  pl.BlockSpec((l, dblk), lambda c: (rev(c), vcol // dblk)),
                  pl.BlockSpec((l, LANES), lambda c: (rev(c), 0)), pl.BlockSpec((1, 8, l), lambda c: (rev(c), 0, 0)),
                  pl.BlockSpec((1, hh, dh, dh), lambda c: (rev(c), 0, 0, 0)),
                  pl.BlockSpec((1, hh, 1, dh), lambda c: (rev(c), 0, 0, 0)),
                  pl.BlockSpec((1, hh, 1, LANES), lambda c: (rev(c), 0, 0, 0)),
                  pl.BlockSpec((l, dblk), lambda c: (rev(c), 0))],
        out_specs=[pl.BlockSpec((l, 2 * dblk), lambda c: (rev(c), 0)),
                   pl.BlockSpec((l, dblk), lambda c: (rev(c), 0)), pl.BlockSpec((l, LANES), lambda c: (rev(c), 0)),
                   pl.BlockSpec((1, 8, l), lambda c: (rev(c), 0, 0))],
        out_shape=[pltpu.HBM((s, 2 * d_model), F32),
                   pltpu.HBM((s, d_model), BF16), pltpu.HBM((s, LANES), F32),
                   pltpu.HBM((nc, 8, l), F32)],
        scratch_shapes=[pltpu.VMEM((hh, dh, dh), F32), pltpu.VMEM((hh, 1, dh), F32)],
        compiler_params=_params(10 * hh * dh * dh * 4 + (16 << 20)),
    )(*[_hbm(a) for a in (mqk, mqk, zm, gcol, grow, cs, ns, ms, dhm)])


def _xa_fwd(zm, qcol, kv, gq, gk, d_model, *, name, tq=512):
    s = zm.shape[0]
    nm = kv.shape[0]
    dh = d_model // X_HEADS
    tq = _pick(s, (tq, 128, 64))
    scale = dh ** -0.5

    def body(q_ref, k_ref, v_ref, gq_ref, gk_ref, o_ref):
        qn = _rms_fwd(q_ref[...], gq_ref[...])
        kn = _rms_fwd(k_ref[...], gk_ref[...])
        lg = _dot(qn, kn, NT) * scale
        lg = lg - jnp.max(lg, axis=1, keepdims=True)
        p = jnp.exp(lg)
        p = p / jnp.sum(p, axis=1, keepdims=True)
        o_ref[...] = _dot(p, v_ref[...], NN).astype(o_ref.dtype)

    return pl.pallas_call(
        body, name=name, grid=(X_HEADS, s // tq),
        in_specs=[pl.BlockSpec((tq, dh), lambda h, i: (i, qcol // dh + h)), pl.BlockSpec((nm, dh), lambda h, i: (0, h)),
                  pl.BlockSpec((nm, dh), lambda h, i: (0, X_HEADS + h)),
                  pl.BlockSpec((1, dh), lambda h, i: (0, 0)), pl.BlockSpec((1, dh), lambda h, i: (0, 0))],
        out_specs=pl.BlockSpec((tq, dh), lambda h, i: (i, h)),
        out_shape=pltpu.HBM((s, d_model), BF16),
        compiler_params=_params(32 << 20),
    )(_hbm(zm), _hbm(kv), _hbm(kv), gq, gk)


def _xa_bwd(zm, qcol, kv, gq, gk, dy, d_model, *, name, tq=512):
    s = zm.shape[0]
    nm = kv.shape[0]
    dh = d_model // X_HEADS
    tq = _pick(s, (tq, 128, 64))
    nq = s // tq
    scale = dh ** -0.5

    def body(q_ref, k_ref, v_ref, gq_ref, gk_ref, do_ref, dq_ref, dkn_ref, dv_ref, dgq_ref):
        h, i = pl.program_id(0), pl.program_id(1)

        @pl.when(i == 0)
        def _():
            dkn_ref[...] = jnp.zeros_like(dkn_ref)
            dv_ref[...] = jnp.zeros_like(dv_ref)

        @pl.when((i == 0) & (h == 0))
        def _():
            dgq_ref[...] = jnp.zeros_like(dgq_ref)

        q = q_ref[...]
        qn = _rms_fwd(q, gq_ref[...])
        kn = _rms_fwd(k_ref[...], gk_ref[...])
        lg = _dot(qn, kn, NT) * scale
        lg = lg - jnp.max(lg, axis=1, keepdims=True)
        p = jnp.exp(lg)
        p = p / jnp.sum(p, axis=1, keepdims=True)
        do = do_ref[...]
        dv_ref[...] += _dot(p, do, TN)
        dp = _dot(do, v_ref[...], NT)
        dlg = p * (dp - jnp.sum(dp * p, axis=1, keepdims=True)) * scale
        dqn = _dot(dlg, kn, NN)
        dkn_ref[...] += _dot(dlg, qn, TN)
        dq, dgq = _rms_bwd(q, gq_ref[...], dqn)
        dq_ref[...] = dq.astype(dq_ref.dtype)
        dgq_ref[...] += jnp.sum(dgq, axis=0, keepdims=True)

    return pl.pallas_call(
        body, name=name, grid=(X_HEADS, nq),
        in_specs=[pl.BlockSpec((tq, dh), lambda h, i: (i, qcol // dh + h)), pl.BlockSpec((nm, dh), lambda h, i: (0, h)),
                  pl.BlockSpec((nm, dh), lambda h, i: (0, X_HEADS + h)),
                  pl.BlockSpec((1, dh), lambda h, i: (0, 0)), pl.BlockSpec((1, dh), lambda h, i: (0, 0)),
                  pl.BlockSpec((tq, dh), lambda h, i: (i, h))],
        out_specs=[pl.BlockSpec((tq, dh), lambda h, i: (i, h)), pl.BlockSpec((nm, dh), lambda h, i: (0, h)),
                   pl.BlockSpec((nm, dh), lambda h, i: (0, h)), pl.BlockSpec((1, dh), lambda h, i: (0, 0))],
        out_shape=[pltpu.HBM((s, d_model), BF16), pltpu.HBM((nm, d_model), F32),
                   pltpu.HBM((nm, d_model), F32), pltpu.HBM((1, dh), F32)],
        compiler_params=_params(32 << 20),
    )(_hbm(zm), _hbm(kv), _hbm(kv), gq, gk, _hbm(dy))


def _place():
    return lax.axis_index("x"), lax.axis_index("y"), lax.axis_index("c")


ANY = pl.BlockSpec(memory_space=pl.ANY)


def _allgather_two_level(big, small, *, name, chunk_rows=64):
    r, cc = big.shape
    half = r // 2
    nr = _pick(half, (chunk_rows, 32, 16))
    nq = half // nr

    def body(big_ref, small_ref, obig, osmall, land, passed, send, recv, fsend, frecv, out_a, out_b, ssend, srecv, loc):
        x, y, c = _place()
        k = 2 * x + y
        chips = [(1 - x, y), (x, 1 - y), (1 - x, 1 - y)]
        slots = [2 * px + py for px, py in chips]
        local = [pltpu.make_async_copy(big_ref, obig.at[k], loc.at[0]),
                 pltpu.make_async_copy(small_ref, osmall.at[k], loc.at[1])]
        for cp in local:
            cp.start()

        def rows(h, q):
            return pl.ds(pl.multiple_of(h * half + q * nr, nr), nr)

        def chunk(q):
            return pl.ds(q * nr, nr)

        def over_ici(j, q):
            return pltpu.make_async_remote_copy(
                src_ref=big_ref.at[rows(c, q)], dst_ref=land.at[j, chunk(q)], send_sem=send.at[nq * j + q],
                recv_sem=recv.at[nq * j + q], device_id=(chips[j][0], chips[j][1], c), device_id_type=MESH)

        def to_sibling(j, q):
            return pltpu.make_async_remote_copy(
                src_ref=land.at[j, chunk(q)], dst_ref=passed.at[j, chunk(q)], send_sem=fsend.at[nq * j + q],
                recv_sem=frecv.at[nq * j + q], device_id=(x, y, 1 - c), device_id_type=MESH)

        def small_copy(j, slot):
            return pltpu.make_async_remote_copy(
                src_ref=small_ref, dst_ref=osmall.at[slot], send_sem=ssend.at[j], recv_sem=srecv.at[j],
                device_id=(chips[j][0], chips[j][1], c), device_id_type=MESH)

        for q in range(nq):
            for j in range(3):
                over_ici(j, q).start()
        for j in range(3):
            small_copy(j, k).start()
        for q in range(nq):
            for j in range(3):
                over_ici(j, q).wait_recv()
                to_sibling(j, q).start()
                cp = pltpu.make_async_copy(land.at[j, chunk(q)], obig.at[slots[j], rows(c, q)], out_a.at[nq * j + q])
                cp.start()
                local.append(cp)
        for q in range(nq):
            for j in range(3):
                to_sibling(j, q).wait_recv()
                cp = pltpu.make_async_copy(passed.at[j, chunk(q)], obig.at[slots[j], rows(1 - c, q)],
                                           out_b.at[nq * j + q])
                cp.start()
                local.append(cp)
        for j in range(3):
            small_copy(j, slots[j]).wait_recv()
            small_copy(j, k).wait_send()
        for q in range(nq):
            for j in range(3):
                over_ici(j, q).wait_send()
                to_sibling(j, q).wait_send()
        for cp in local:
            cp.wait()

    stage = 2 * _nbytes((3, half, cc), big.dtype)
    return pl.pallas_call(
        body, name=name, in_specs=[ANY] * 2, out_specs=[ANY] * 2,
        out_shape=[pltpu.HBM((4,) + big.shape, big.dtype), pltpu.HBM((4,) + small.shape, small.dtype)],
        scratch_shapes=[pltpu.VMEM((3, half, cc), big.dtype), pltpu.VMEM((3, half, cc), big.dtype)]
        + [pltpu.SemaphoreType.DMA((3 * nq,))] * 6
        + [pltpu.SemaphoreType.DMA((3,)), pltpu.SemaphoreType.DMA((3,)), pltpu.SemaphoreType.DMA((2,))],
        compiler_params=_params(stage + stage // 8 + (4 << 20)),
    )(big, small)


HBM_SPEC = pl.BlockSpec(memory_space=pltpu.HBM)
SEM_SPEC = pl.BlockSpec(memory_space=pltpu.SEMAPHORE)
EFFECT = pltpu.SideEffectType.DATAFLOW_SIDE_EFFECTING


def _split_copies(kind, srcs, lands, send, recv):
    x, y, c = _place()
    if kind == "quarters":
        peers = [(1 - x, y, c), (x, 1 - y, c), (1 - x, 1 - y, c)]
    else:
        peers = [(x ^ ((j >> 2) & 1), y ^ ((j >> 1) & 1), c ^ (j & 1)) for j in range(1, 8)]
    npeer = len(peers)
    out = []
    for t in range(len(srcs)):
        for j, (px, py, pc) in enumerate(peers):
            if kind == "quarters":
                src, mine, theirs = srcs[t], 2 * x + y, 2 * px + py
            else:
                src, mine, theirs = srcs[t].at[2 * px + py, pc], 4 * x + 2 * y + c, 4 * px + 2 * py + pc
            mk = functools.partial(
                pltpu.make_async_remote_copy, src_ref=src, send_sem=send.at[npeer * t + j],
                recv_sem=recv.at[npeer * t + j], device_id=(px, py, pc), device_id_type=MESH)
            out.append((functools.partial(mk, dst_ref=lands[t].at[mine]),
                        functools.partial(mk, dst_ref=lands[t].at[theirs])))
    return out


def _split_start(kind, srcs, land_shapes, after, *, name):
    n = len(srcs)
    ncopies = n * (3 if kind == "quarters" else 7)

    def body(*refs):
        ins, lands = refs[:n], refs[n:2 * n]
        send, recv = refs[2 * n + 1], refs[2 * n + 2]
        token = refs[-1]
        for start, _ in _split_copies(kind, ins, lands, send, recv):
            start().start()
        token[...] = jnp.zeros_like(token)

    lands = [_hbm(lax.empty(shp, a.dtype)) for shp, a in zip(land_shapes, srcs)]
    res = pl.pallas_call(
        body, name=name, in_specs=[HBM_SPEC] * (2 * n) + [ANY],
        out_specs=[SEM_SPEC, SEM_SPEC] + [HBM_SPEC] * (2 * n) + [pl.BlockSpec(memory_space=pltpu.VMEM)],
        out_shape=[pltpu.SemaphoreType.DMA((ncopies,)), pltpu.SemaphoreType.DMA((ncopies,))]
        + [pltpu.HBM(a.shape, a.dtype) for a in srcs] + [pltpu.HBM(shp, a.dtype) for shp, a in zip(land_shapes, srcs)]
        + [jax.ShapeDtypeStruct((8, LANES), F32)],
        input_output_aliases={i: 2 + i for i in range(2 * n)},
        compiler_params=pltpu.CompilerParams(has_side_effects=EFFECT),
    )(*[_hbm(a) for a in srcs], *lands, after)
    return res[0], res[1], list(res[2:2 + n]), list(res[2 + n:2 + 2 * n]), res[-1]


def _split_wait(kind, send, recv, srcs, lands, after, *, name):
    n = len(srcs)

    def body(*refs):
        ins, lnd = refs[:n], refs[n:2 * n]
        snd, rcv = refs[2 * n], refs[2 * n + 1]
        for start, arrive in _split_copies(kind, ins, lnd, snd, rcv):
            start().wait_send()
            arrive().wait_recv()

    res = pl.pallas_call(
        body, name=name, in_specs=[HBM_SPEC] * (2 * n) + [SEM_SPEC, SEM_SPEC] + [ANY] * len(after),
        out_specs=[HBM_SPEC] * (2 * n),
        out_shape=[pltpu.HBM(a.shape, a.dtype) for a in srcs] + [pltpu.HBM(a.shape, a.dtype) for a in lands],
        input_output_aliases={i: i for i in range(2 * n)},
        compiler_params=pltpu.CompilerParams(has_side_effects=EFFECT),
    )(*srcs, *lands, send, recv, *after)
    return list(res[n:])


def _sum8(parts, *, name):
    _, r, c = parts.shape
    t = _pick(r, (128, 64, 32, 16, 8))

    def body(p_ref, o_ref):
        acc = p_ref[0].astype(F32)
        for k in range(1, 8):
            acc = acc + p_ref[k].astype(F32)
        o_ref[...] = acc

    return pl.pallas_call(
        body, name=name, grid=(r // t,), in_specs=[pl.BlockSpec((8, t, c), lambda i: (0, i, 0))],
        out_specs=pl.BlockSpec((t, c), lambda i: (i, 0)), out_shape=pltpu.HBM((r, c), F32),
        compiler_params=_params(2 * 8 * t * c * 2 + 6 * t * c * 4 + (4 << 20)),
    )(_hbm(parts))


def _swap_halves(halves, *, name, chunk_bytes=512 * 1024):
    n = len(halves)
    items = []
    for t, a in enumerate(halves):
        r = a.shape[0]
        k = 1
        while _nbytes(a.shape, a.dtype) // k > chunk_bytes and r % (2 * k) == 0 and (r // (2 * k)) % 8 == 0:
            k *= 2
        items += [(t, q * (r // k), r // k) for q in range(k)]
    m = len(items)

    def body(*refs):
        ins, outs = refs[:n], refs[n:2 * n]
        sbuf, rbuf = refs[2 * n:3 * n], refs[3 * n:4 * n]
        send, recv, loc_own, loc_in, loc_out = refs[4 * n:]
        x, y, c = _place()
        local, stage = [], []
        for t in range(n):
            cp = pltpu.make_async_copy(ins[t], outs[t].at[c], loc_own.at[t])
            cp.start()
            local.append(cp)
        for q, (t, r0, nr) in enumerate(items):
            cp = pltpu.make_async_copy(ins[t].at[pl.ds(r0, nr)], sbuf[t].at[pl.ds(r0, nr)], loc_in.at[q])
            cp.start()
            stage.append(cp)

        def copy(q):
            t, r0, nr = items[q]
            return pltpu.make_async_remote_copy(
                src_ref=sbuf[t].at[pl.ds(r0, nr)], dst_ref=rbuf[t].at[pl.ds(r0, nr)], send_sem=send.at[q],
                recv_sem=recv.at[q], device_id=(x, y, 1 - c), device_id_type=MESH)

        for q in range(m):
            stage[q].wait()
            copy(q).start()
        for q, (t, r0, nr) in enumerate(items):
            copy(q).wait_recv()
            cp = pltpu.make_async_copy(rbuf[t].at[pl.ds(r0, nr)], outs[t].at[1 - c, pl.ds(r0, nr)], loc_out.at[q])
            cp.start()
            local.append(cp)
        for q in range(m):
            copy(q).wait_send()
        for cp in local:
            cp.wait()

    stage_bytes = 2 * sum(_nbytes(a.shape, a.dtype) for a in halves)
    return pl.pallas_call(
        body, name=name, in_specs=[ANY] * n, out_specs=[ANY] * n,
        out_shape=[pltpu.HBM((2,) + a.shape, a.dtype) for a in halves],
        scratch_shapes=[pltpu.VMEM(a.shape, a.dtype) for a in halves] * 2
        + [pltpu.SemaphoreType.DMA((m,)), pltpu.SemaphoreType.DMA((m,)), pltpu.SemaphoreType.DMA((n,)),
           pltpu.SemaphoreType.DMA((m,)), pltpu.SemaphoreType.DMA((m,))],
        compiler_params=_params(stage_bytes + (4 << 20)),
    )(*halves)


def _allreduce_small(p, after, *, name):
    r = p.shape[0]

    def body(p_ref, after_ref, o_ref, buf, send, recv):
        x, y, c = _place()
        me = 4 * x + 2 * y + c
        peers = [(x ^ ((j >> 2) & 1), y ^ ((j >> 1) & 1), c ^ (j & 1)) for j in range(1, 8)]

        def copy(j, slot):
            return pltpu.make_async_remote_copy(
                src_ref=p_ref, dst_ref=buf.at[slot], send_sem=send.at[j], recv_sem=recv.at[j],
                device_id=peers[j], device_id_type=MESH)

        for j in range(7):
            copy(j, me).start()
        buf[me] = p_ref[...]
        for j in range(7):
            px, py, pc = peers[j]
            copy(j, 4 * px + 2 * py + pc).wait_recv()
        for j in range(7):
            copy(j, me).wait_send()
        acc = buf[0]
        for k in range(1, 8):
            acc = acc + buf[k]
        o_ref[...] = acc

    vspec = pl.BlockSpec(memory_space=pltpu.VMEM)
    return pl.pallas_call(
        body, name=name, in_specs=[vspec, ANY], out_specs=vspec, out_shape=jax.ShapeDtypeStruct((r, LANES), F32),
        scratch_shapes=[pltpu.VMEM((8, r, LANES), F32), pltpu.SemaphoreType.DMA((7,)), pltpu.SemaphoreType.DMA((7,))],
    )(p, after)


def _adamw_fn(w, g, m, v):
    m = ADAM_B1 * m + (1.0 - ADAM_B1) * g
    v = ADAM_B2 * v + (1.0 - ADAM_B2) * (g * g)
    m_hat = m / (1.0 - ADAM_B1 ** ADAM_STEP)
    v_hat = v / (1.0 - ADAM_B2 ** ADAM_STEP)
    delta = -ADAM_LR * (m_hat / (jnp.sqrt(v_hat) + ADAM_EPS) + ADAM_WD * w)
    return delta, m, v


def _adamw(w, g, m, v, *, name):
    c = w.shape[1]
    return _rowwise(_adamw_fn, [w, g, m, v], [], [(c, F32)] * 3, name=name, tr=128)


def _pack(vecs, rows):
    flat = jnp.concatenate([a.reshape(-1).astype(F32) for a in vecs])
    return jnp.pad(flat, (0, rows * LANES - flat.shape[0])).reshape(rows, LANES)


def _unpack(p, like):
    flat, out, o = p.reshape(-1), [], 0
    for a in like:
        out.append(flat[o:o + a.size].reshape(a.shape))
        o += a.size
    return out


def kernel(x, mem, g_mix, w_in, b_if, b_gate, conv_w, conv_b, ml_norm_g, g_mem, w_mem_kv, q_norm_g, k_norm_g, w_sb_proj, w_ml_proj, w_x_proj, w_out, g_mlp, w_ff1, w_ff2, loss_target, m_g_mix, m_w_in, m_b_if, m_b_gate, m_conv_w, m_conv_b, m_ml_norm_g, m_g_mem, m_w_mem_kv, m_q_norm_g, m_k_norm_g, m_w_sb_proj, m_w_ml_proj, m_w_x_proj, m_w_out, m_g_mlp, m_w_ff1, m_w_ff2, v_g_mix, v_w_in, v_b_if, v_b_gate, v_conv_w, v_conv_b, v_ml_norm_g, v_g_mem, v_w_mem_kv, v_q_norm_g, v_k_norm_g, v_w_sb_proj, v_w_ml_proj, v_w_x_proj, v_w_out, v_g_mlp, v_w_ff1, v_w_ff2):
    _, s, d = x.shape
    nm = mem.shape[1]
    n_in = 4 * w_in.shape[2]
    dff = 4 * w_ff1.shape[2]
    sbh = d // SB_HD
    hh = ML_HEADS
    dh = d // hh
    nc = s // CHUNK
    assert n_in == 11 * d + 2 * hh and d % (2 * LANES) == 0 and s % LANES == 0
    x2, mem2, tgt = x[0], mem[0], loss_target[0]

    k4 = 2 * lax.axis_index("x") + lax.axis_index("y")
    me = 2 * k4 + lax.axis_index("c")
    g_first = _allgather_two_level(w_in[0].astype(BF16), conv_w[0], name="gather_w_in")
    later = [a[0].astype(BF16) for a in (w_mem_kv, w_sb_proj, w_ml_proj, w_x_proj, w_out, w_ff1, w_ff2)]
    gw_send, gw_recv, gw_src, gw_land, gw_token = _split_start(
        "quarters", later, [(4,) + a.shape for a in later], g_first[0], name="gather_rest_start")
    cols = lambda a: a.transpose(1, 0, 2).reshape(a.shape[1], 4 * a.shape[2])
    rws = lambda a: a.reshape(4 * a.shape[1], a.shape[2])
    w_in_f = cols(g_first[0])
    w_main = jnp.concatenate([w_in_f[:, :7 * d], w_in_f[:, 7 * d + 2 * hh:]], axis=1)
    w_if = jnp.pad(w_in_f[:, 7 * d:7 * d + 2 * hh], ((0, 0), (0, LANES - 2 * hh)))
    conv_wf = cols(g_first[1])
    b_if_p = jnp.pad(b_if, ((0, 0), (0, LANES - 2 * hh)))

    (hn,) = _rowwise(_rms_fwd, [x2], [g_mix], [(d, BF16)], name="norm_in", tr=512)
    zm = _mm(hn, w_main, after=gw_token, name="proj_in")
    zif = _mm(hn, w_if, name="proj_if")
    y_sb, a_sb = _sb_fwd(zm, sbh, name="sb_fwd")

    def gate_fn(z, b):
        pre = z + b
        lane = lax.broadcasted_iota(jnp.int32, pre.shape, 1)
        return jnp.where(lane < hh, pre, -_softplus(-pre))

    (gcol,) = _rowwise(gate_fn, [zif], [b_if_p], [(LANES, F32)], name="ml_gates", tr=1024)
    grow = gcol[:, :8].T.reshape(8, nc, CHUNK).transpose(1, 0, 2)
    mqk = _conv_fwd(zm, 3 * d, 2 * d, conv_wf, conv_b, name="conv_fwd")
    hm, cst, nst, mst = _ml_fwd(mqk, zm, 5 * d, gcol, grow, d, name="ml_fwd")

    def mlout_fn(hv, o, g):
        ys = [_rms_fwd(hv[:, k * dh:(k + 1) * dh], g[:, k * dh:(k + 1) * dh]) for k in range(hh)]
        return jnp.concatenate(ys, axis=1) * _sigmoid(o)

    (y_ml,) = _rowwise(mlout_fn, [hm, (zm, d, 6)], [ml_norm_g], [(d, BF16)], name="ml_out", tr=512)
    gw_land = _split_wait("quarters", gw_send, gw_recv, gw_src, gw_land, [y_ml, y_sb], name="gather_rest_wait")
    gw = [lax.dynamic_update_index_in_dim(ld, a, k4, 0) for ld, a in zip(gw_land, later)]
    w_kv, w_sbp, w_mlp, w_xp, w_o, w_f1, w_f2 = (cols(gw[0]), rws(gw[1]), rws(gw[2]), rws(gw[3]), rws(gw[4]),
                                                 cols(gw[5]), rws(gw[6]))
    (memn,) = _rowwise(_rms_fwd, [mem2], [g_mem], [(d, BF16)], name="norm_mem")
    kv = _mm(memn, w_kv, name="proj_kv")
    y_x = _xa_fwd(zm, 7 * d, kv, q_norm_g, k_norm_g, d, name="xa_fwd")
    p_sb = _mm(y_sb, w_sbp, name="proj_sb")
    p_ml = _mm(y_ml, w_mlp, name="proj_ml")
    p_x = _mm(y_x, w_xp, name="proj_x")

    def merge_fn(a, b, c, g0, g1, g2, bg):
        return (_sigmoid(g0 + bg[:, :d]) * a + _sigmoid(g1 + bg[:, d:2 * d]) * b + _sigmoid(g2 + bg[:, 2 * d:]) * c)

    gate_cols = [(zm, d, 8), (zm, d, 9), (zm, d, 10)]
    (mixed,) = _rowwise(merge_fn, [p_sb, p_ml, p_x] + gate_cols, [b_gate], [(d, BF16)], name="merge")
    x1 = _mm(mixed, w_o, tiles=[x2], name="proj_out")
    (h2,) = _rowwise(_rms_fwd, [x1], [g_mlp], [(d, BF16)], name="norm_mlp", tr=512)
    u, act = _mm(h2, w_f1, post=lambda r: (r, jnp.square(jnp.maximum(r, 0.0))), out_dtype=(F32, BF16), name="ff1")
    dy = _mm(act, w_f2, tiles=[x1, tgt], post=lambda r, xv, tv: (r + xv - tv) * (1.0 / d), name="ff2")
    (loss_cols,) = _rowwise(lambda g: (jnp.sum(g * g, axis=0, keepdims=True) * (0.5 * d),), [dy], [], [], [d],
                            name="loss", tr=1024)

    du = _mm(dy, w_f2, tb=True, tiles=[u], post=lambda r, uv: r * 2.0 * jnp.maximum(uv, 0.0), out_dtype=BF16,
             name="ff2_dx")
    dw_f2 = _mm(act, dy, ta=True, name="ff2_dw")
    dw_f1 = _mm(h2, du, ta=True, name="ff1_dw")
    dh2 = _mm(du, w_f1, tb=True, name="ff1_dx")

    def norm_bwd_fn(xv, dyv, res, g):
        dx, dg = _rms_bwd(xv, g, dyv)
        return dx + res, jnp.sum(dg, axis=0, keepdims=True)

    dx1, dg_mlp = _rowwise(norm_bwd_fn, [x1, dh2, dy], [g_mlp], [(d, F32)], [d], name="norm_mlp_bwd", tr=512)
    dmixed = _mm(dx1, w_o, tb=True, name="proj_out_dx")
    dw_o = _mm(mixed, dx1, ta=True, name="proj_out_dw")

    def merge_bwd_fn(dm, a, b, c, g0, g1, g2, bg):
        outs, dgs = [], []
        for p, g, k in ((a, g0, 0), (b, g1, 1), (c, g2, 2)):
            sg = _sigmoid(g + bg[:, k * d:(k + 1) * d])
            outs.append(dm * sg)
            dgs.append(dm * p * sg * (1.0 - sg))
        dgate = jnp.concatenate(dgs, axis=1)
        return (*outs, dgate, jnp.sum(dgate, axis=0, keepdims=True))

    dp_sb, dp_ml, dp_x, dgate, db_gate = _rowwise(
        merge_bwd_fn, [dmixed, p_sb, p_ml, p_x] + gate_cols, [b_gate], [(d, BF16)] * 3 + [(3 * d, BF16)], [3 * d],
        name="merge_bwd", tr=256)
    dw_sbp = _mm(y_sb, dp_sb, ta=True, name="proj_sb_dw")
    dw_mlp = _mm(y_ml, dp_ml, ta=True, name="proj_ml_dw")
    dw_xp = _mm(y_x, dp_x, ta=True, name="proj_x_dw")
    dy_sb = _mm(dp_sb, w_sbp, tb=True, out_dtype=BF16, name="proj_sb_dx")
    dy_ml = _mm(dp_ml, w_mlp, tb=True, name="proj_ml_dx")
    dy_x = _mm(dp_x, w_xp, tb=True, out_dtype=BF16, name="proj_x_dx")

    dxq, dkn, dxv, dg_qn = _xa_bwd(zm, 7 * d, kv, q_norm_g, k_norm_g, dy_x, d, name="xa_bwd")

    def knorm_bwd_fn(kvv, dknv, dvv, g):
        dks, dgs = [], []
        for k in range(X_HEADS):
            sl = slice(k * dh, (k + 1) * dh)
            dk, dg = _rms_bwd(kvv[:, sl], g, dknv[:, sl])
            dks.append(dk)
            dgs.append(jnp.sum(dg, axis=0, keepdims=True))
        return jnp.concatenate(dks + [dvv], axis=1), dgs[0] + dgs[1] + dgs[2] + dgs[3]

    dkv, dg_kn = _rowwise(knorm_bwd_fn, [(kv, d, 0), dkn, dxv], [k_norm_g], [(2 * d, BF16)], [dh], name="xa_knorm_bwd")
    dw_kv = _mm(memn, dkv, ta=True, name="proj_kv_dw")
    dmemn = _mm(dkv, w_kv, tb=True, name="proj_kv_dx")

    def gmem_fn(mv, dv_, g):
        _, dg = _rms_bwd(mv, g, dv_)
        return (jnp.sum(dg, axis=0, keepdims=True),)

    (dg_mem,) = _rowwise(gmem_fn, [mem2, dmemn], [g_mem], [], [d], name="norm_mem_bwd")

    uncols = lambda a: a.reshape(a.shape[0], 4, a.shape[1] // 4).transpose(1, 0, 2)
    unrws = lambda a: a.reshape(4, a.shape[0] // 4, a.shape[1])
    to_parts = lambda q: q.astype(BF16).reshape(4, 2, q.shape[1] // 2, q.shape[2])
    early = [to_parts(q) for q in (uncols(dw_kv), unrws(dw_sbp), unrws(dw_mlp), unrws(dw_xp), unrws(dw_o),
                                   uncols(dw_f1), unrws(dw_f2))]
    ge_send, ge_recv, ge_src, ge_land, ge_token = _split_start(
        "grads", early, [(8,) + a.shape[2:] for a in early], dg_mem, name="exchange_early_start")

    dsq, dsk, dsv = _sb_bwd(zm, dy_sb, a_sb, ge_token, sbh, name="sb_bwd")

    def mlout_bwd_fn(dyv, hv, o, g):
        sg = _sigmoid(o)
        dn = dyv * sg
        dxs, dgs, ys = [], [], []
        for k in range(hh):
            sl = slice(k * dh, (k + 1) * dh)
            ys.append(_rms_fwd(hv[:, sl], g[:, sl]))
            dxk, dgk = _rms_bwd(hv[:, sl], g[:, sl], dn[:, sl])
            dxs.append(dxk)
            dgs.append(dgk)
        do = dyv * jnp.concatenate(ys, axis=1) * sg * (1.0 - sg)
        return jnp.concatenate(dxs, axis=1), do, jnp.sum(jnp.concatenate(dgs, axis=1), axis=0, keepdims=True)

    dhm, dmlo, dg_mln = _rowwise(mlout_bwd_fn, [dy_ml, hm, (zm, d, 6)], [ml_norm_g], [(d, F32), (d, BF16)], [d],
                                 name="ml_out_bwd", tr=512)
    dmqk, dmlv, dgc, dgr = _ml_bwd(mqk, zm, 5 * d, gcol, grow, cst, nst, mst, dhm, d, name="ml_bwd")
    dmlqk, dconv_w, dconv_b = _conv_bwd(zm, 3 * d, 2 * d, conv_wf, conv_b, dmqk, name="conv_bwd")
    dgr_t = jnp.pad(dgr.transpose(1, 0, 2).reshape(8, s).T, ((0, 0), (0, LANES - 8)))

    def gate_bwd_fn(a, b, z, bias):
        tot = a + b
        rows_t = tot.shape[0]
        r = lax.broadcasted_iota(jnp.int32, (rows_t, rows_t), 0)
        c = lax.broadcasted_iota(jnp.int32, (rows_t, rows_t), 1)
        sh = CHUNK.bit_length() - 1
        same_chunk = jnp.right_shift(r, sh) == jnp.right_shift(c, sh)
        dlf = _u01dot(((c >= r) & same_chunk).astype(BF16), tot)
        lane = lax.broadcasted_iota(jnp.int32, tot.shape, 1)
        dz = jnp.where(lane < hh, tot, jnp.where(lane < 2 * hh, dlf * _sigmoid(-(z + bias)), 0.0))
        return dz, jnp.sum(dz, axis=0, keepdims=True)

    dzif, db_if_p = _rowwise(gate_bwd_fn, [dgc, dgr_t, zif], [b_if_p], [(LANES, BF16)], [LANES], name="ml_gates_bwd",
                             tr=8 * CHUNK)
    dzm = jnp.concatenate([dsq, dsk, dsv, dmlqk, dmlv, dmlo, dxq, dgate], axis=1)
    dw_main = _mm(hn, dzm, ta=True, out_dtype=BF16, name="proj_in_dw")
    dw_if = _mm(hn, dzif, ta=True, out_dtype=BF16, name="proj_if_dw")
    dw_in = jnp.concatenate([dw_main[:, :7 * d], dw_if[:, :2 * hh], dw_main[:, 7 * d:]], axis=1)
    late = [to_parts(uncols(dw_in))]
    gl_send, gl_recv, gl_src, gl_land, gl_token = _split_start(
        "grads", late, [(8,) + a.shape[2:] for a in late], dw_if, name="exchange_late_start")
    dhn = _mm(dzm, w_main, tb=True, after=gl_token, name="proj_in_dx")
    dhn = _mm(dzif, w_if, tb=True, tiles=[dhn], name="proj_if_dx")
    dx, dg_mix = _rowwise(norm_bwd_fn, [x2, dhn, dx1], [g_mix], [(d, F32)], [d], name="norm_in_bwd", tr=512)

    own = lambda p: lax.dynamic_index_in_dim(lax.dynamic_index_in_dim(p, k4, 0, keepdims=False),
                                             lax.axis_index("c"), 0, keepdims=False)

    def finish(tag, send, recv, src, land, parts, after, ws, ms, vs):
        land = _split_wait("grads", send, recv, src, land, after, name=f"exchange_{tag}_wait")
        got = [lax.dynamic_update_index_in_dim(ld, own(p), me, 0) for ld, p in zip(land, parts)]
        halves = [_sum8(r, name=f"sum_grads_{tag}{i}") for i, r in enumerate(got)]
        both = _swap_halves(halves, name=f"swap_halves_{tag}")
        gs = [b.reshape(2 * b.shape[1], b.shape[2]) for b in both]
        return gs, [_adamw(w, g, m, v, name=f"adamw_{tag}{i}") for i, (w, g, m, v) in enumerate(zip(ws, gs, ms, vs))]

    first = lambda arrs: [a[0] for a in arrs]
    g_early, out_early = finish(
        "early", ge_send, ge_recv, ge_src, ge_land, early, [dx],
        first([w_mem_kv, w_sb_proj, w_ml_proj, w_x_proj, w_out, w_ff1, w_ff2]),
        first([m_w_mem_kv, m_w_sb_proj, m_w_ml_proj, m_w_x_proj, m_w_out, m_w_ff1, m_w_ff2]),
        first([v_w_mem_kv, v_w_sb_proj, v_w_ml_proj, v_w_x_proj, v_w_out, v_w_ff1, v_w_ff2]))
    g_late, out_late = finish(
        "late", gl_send, gl_recv, gl_src, gl_land, late, [o[0] for o in out_early],
        first([w_in]), first([m_w_in]), first([v_w_in]))
    g_big = [g[None] for g in g_late + g_early]
    big_out = [[o[None] for o in outs] for outs in out_late + out_early]

    small_g = [dg_mix, db_if_p[:, :2 * hh], db_gate, dconv_w, dconv_b, dg_mln, dg_mem, dg_qn, dg_kn, dg_mlp,
               jnp.sum(loss_cols).reshape(1, 1)]
    n_small = sum(a.size for a in small_g)
    rows = -(-n_small // (8 * LANES)) * 8
    g_small = _unpack(_allreduce_small(_pack(small_g, rows), out_late[0][0], name="allreduce_small"), small_g)
    loss = g_small[-1].reshape(())
    qw = conv_w.shape[2]
    g_conv_w = lax.dynamic_slice_in_dim(g_small[3], k4 * qw, qw, axis=1)
    g_small_w = [g_small[0], g_small[1], g_small[2], g_conv_w] + g_small[4:10]
    sm_w = [g_mix, b_if, b_gate, conv_w[0], conv_b, ml_norm_g, g_mem, q_norm_g, k_norm_g, g_mlp]
    sm_m = [m_g_mix, m_b_if, m_b_gate, m_conv_w[0], m_conv_b, m_ml_norm_g, m_g_mem, m_q_norm_g, m_k_norm_g, m_g_mlp]
    sm_v = [v_g_mix, v_b_if, v_b_gate, v_conv_w[0], v_conv_b, v_ml_norm_g, v_g_mem, v_q_norm_g, v_k_norm_g, v_g_mlp]
    n_sw = sum(a.size for a in sm_w)
    rows_w = -(-n_sw // (8 * LANES)) * 8
    sm_out = _adamw(_pack(sm_w, rows_w), _pack(g_small_w, rows_w), _pack(sm_m, rows_w), _pack(sm_v, rows_w),
                    name="adamw_small")
    sm_delta, sm_newm, sm_newv = [_unpack(p, sm_w) for p in sm_out]

    order = ["g_mix", "w_in", "b_if", "b_gate", "conv_w", "conv_b", "ml_norm_g", "g_mem", "w_mem_kv", "q_norm_g",
             "k_norm_g", "w_sb_proj", "w_ml_proj", "w_x_proj", "w_out", "g_mlp", "w_ff1", "w_ff2"]
    small_names = ["g_mix", "b_if", "b_gate", "conv_w", "conv_b", "ml_norm_g", "g_mem", "q_norm_g", "k_norm_g", "g_mlp"]
    big_names = ["w_in", "w_mem_kv", "w_sb_proj", "w_ml_proj", "w_x_proj", "w_out", "w_ff1", "w_ff2"]
    grads, deltas, new_m, new_v = {}, {}, {}, {}
    for i, nme in enumerate(small_names):
        shp = sm_w[i].shape if nme != "conv_w" else conv_w.shape
        grads[nme] = g_small_w[i].reshape(shp)
        deltas[nme], new_m[nme], new_v[nme] = (sm_delta[i].reshape(shp), sm_newm[i].reshape(shp),
                                               sm_newv[i].reshape(shp))
    for i, nme in enumerate(big_names):
        grads[nme] = g_big[i]
        deltas[nme], new_m[nme], new_v[nme] = big_out[i]
    return (loss, dx[None], *[grads[k] for k in order], *[deltas[k] for k in order], *[new_m[k] for k in order],
            *[new_v[k] for k in order])
```

```python
import functools

import jax
import jax.numpy as jnp
from jax import lax
from jax.experimental import pallas as pl
from jax.experimental.pallas import tpu as pltpu

F32 = jnp.float32
BF16 = jnp.bfloat16
MESH = pl.DeviceIdType.MESH

EPS = 1e-6
SB_HD = 128
SB_SLOTS = 8
ML_HEADS = 4
X_HEADS = 4
CHUNK = 64
CONV_W = 4
LANES = 128
ADAM_LR = 0.001
ADAM_B1 = 0.9
ADAM_B2 = 0.999
ADAM_EPS = 1e-08
ADAM_WD = 0.01
ADAM_STEP = 10
VMEM_CAP = 56 * 1024 * 1024
NEG = -1e30

NT = (((1,), (1,)), ((), ()))
NN = (((1,), (0,)), ((), ()))
TN = (((0,), (0,)), ((), ()))


def _dot(a, b, dn=NN):
    return lax.dot_general(a.astype(BF16), b.astype(BF16), dn, preferred_element_type=F32)


def _dot01(x, u, dn=NN):
    hi = x.astype(BF16)
    lo = (x - hi.astype(F32)).astype(BF16)
    return (lax.dot_general(hi, u, dn, preferred_element_type=F32)
            + lax.dot_general(lo, u, dn, preferred_element_type=F32))


def _u01dot(u, x):
    hi = x.astype(BF16)
    lo = (x - hi.astype(F32)).astype(BF16)
    return (lax.dot_general(u, hi, NN, preferred_element_type=F32)
            + lax.dot_general(u, lo, NN, preferred_element_type=F32))


def _pick(n, cands):
    for c in cands:
        if c <= n and n % c == 0:
            return c
    return n


def _nbytes(shape, dtype):
    n = 1
    for s in shape:
        n *= s
    return n * jnp.dtype(dtype).itemsize


def _params(vmem_bytes):
    return pltpu.CompilerParams(vmem_limit_bytes=int(min(VMEM_CAP, max(vmem_bytes, 16 * 1024 * 1024))))


def _hbm(a):
    return pltpu.with_memory_space_constraint(a, pltpu.HBM)


def _softplus(z):
    return jnp.maximum(z, 0.0) + jnp.log(1.0 + jnp.exp(-jnp.abs(z)))


def _sigmoid(z):
    return 1.0 / (1.0 + jnp.exp(-z))


def _rms_fwd(xv, g):
    r = lax.rsqrt(jnp.mean(xv * xv, axis=-1, keepdims=True) + EPS)
    return xv * r * g


def _rms_bwd(xv, g, dy):
    r = lax.rsqrt(jnp.mean(xv * xv, axis=-1, keepdims=True) + EPS)
    xh = xv * r
    dxh = dy * g
    dx = r * (dxh - xh * jnp.mean(dxh * xh, axis=-1, keepdims=True))
    return dx, dy * xh


def _mm(a, b, *, name, ta=False, tb=False, tiles=(), post=None, out_dtype=F32, bm=1024, bn=1024, bk=1024, after=None):
    m, k = (a.shape[1], a.shape[0]) if ta else a.shape
    n = b.shape[0] if tb else b.shape[1]
    tm = _pick(m, (bm, 512, 256, 128))
    tn = _pick(n, (bn, 512, 256, 128))
    tk = _pick(k, (bk, 512, 256, 128))
    nk = k // tk
    if (m // tm) * (n // tn) * nk < 8 and tm % 256 == 0:
        tm //= 2
    dn = (((0 if ta else 1,), (1 if tb else 0,)), ((), ()))
    dts = out_dtype if isinstance(out_dtype, tuple) else (out_dtype,)
    nt, no = len(tiles), len(dts)
    if post is None:
        post = lambda r, *ts: sum((t.astype(F32) for t in ts), r)

    def body(*refs):
        a_ref, b_ref = refs[:2]
        t_refs = refs[2:2 + nt]
        o_refs = refs[2 + nt + (after is not None):2 + nt + (after is not None) + no]
        part = lax.dot_general(a_ref[...].astype(BF16), b_ref[...].astype(BF16), dn, preferred_element_type=F32)

        def finish(r):
            res = post(r, *[t[...] for t in t_refs])
            res = res if isinstance(res, tuple) else (res,)
            for o, v in zip(o_refs, res):
                o[...] = v.astype(o.dtype)

        if nk == 1:
            finish(part)
        else:
            acc_ref = refs[-1]
            kk = pl.program_id(2)

            @pl.when(kk == 0)
            def _():
                acc_ref[...] = part

            @pl.when(kk > 0)
            def _():
                acc_ref[...] += part

            @pl.when(kk == nk - 1)
            def _():
                finish(acc_ref[...])

    a_spec = pl.BlockSpec((tk, tm), lambda i, j, q: (q, i)) if ta else pl.BlockSpec((tm, tk), lambda i, j, q: (i, q))
    b_spec = pl.BlockSpec((tn, tk), lambda i, j, q: (j, q)) if tb else pl.BlockSpec((tk, tn), lambda i, j, q: (q, j))
    o_spec = pl.BlockSpec((tm, tn), lambda i, j, q: (i, j))
    ins, specs = [_hbm(a), _hbm(b)] + [_hbm(t) for t in tiles], [a_spec, b_spec] + [o_spec] * nt
    vm = 2 * (_nbytes((tm, tk), a.dtype) + _nbytes((tk, tn), b.dtype)) + 3 * _nbytes((tm, tn), F32) \
        + _nbytes((tm, tk), BF16) + _nbytes((tk, tn), BF16) \
        + 2 * sum(_nbytes((tm, tn), t.dtype) for t in tiles) + 2 * sum(_nbytes((tm, tn), dt) for dt in dts)
    if after is not None:
        ins.append(after)
        specs.append(ANY)
    res = pl.pallas_call(
        body, name=name, grid=(m // tm, n // tn, nk), in_specs=specs, out_specs=[o_spec] * no,
        out_shape=[pltpu.HBM((m, n), dt) for dt in dts], scratch_shapes=[pltpu.VMEM((tm, tn), F32)] if nk > 1 else [],
        compiler_params=_params(vm + (4 << 20)),
    )(*ins)
    return res[0] if no == 1 else tuple(res)


def _rowwise(fn, rows, consts, outs, reds=(), *, name, tr=256, temps=6):
    rows = [r if isinstance(r, tuple) else (r, r.shape[1], 0) for r in rows]
    nrows = rows[0][0].shape[0]
    t = _pick(nrows, (tr, 128, 64, 32, 16, 8))
    nr, nc, no = len(rows), len(consts), len(outs)

    def body(*refs):
        rin, cin = refs[:nr], refs[nr:nr + nc]
        oref, rref = refs[nr + nc:nr + nc + no], refs[nr + nc + no:]
        res = fn(*[r[...] for r in rin], *[c[...] for c in cin])
        if not isinstance(res, (tuple, list)):
            res = (res,)
        for o, v in zip(oref, res[:no]):
            o[...] = v.astype(o.dtype)
        if rref:
            @pl.when(pl.program_id(0) == 0)
            def _():
                for r in rref:
                    r[...] = jnp.zeros_like(r)

            for r, v in zip(rref, res[no:]):
                r[...] += v

    in_specs = [pl.BlockSpec((t, w), functools.partial(lambda i, ci: (i, ci), ci=ci)) for (_, w, ci) in rows]
    in_specs += [pl.BlockSpec(c.shape, functools.partial(lambda i, nd: (0,) * nd, nd=c.ndim)) for c in consts]
    out_specs = [pl.BlockSpec((t, w), lambda i: (i, 0)) for (w, _) in outs]
    out_specs += [pl.BlockSpec((1, w), lambda i: (0, 0)) for w in reds]
    out_shape = [pltpu.HBM((nrows, w), dt) for (w, dt) in outs]
    out_shape += [jax.ShapeDtypeStruct((1, w), F32) for w in reds]
    widest = max([w for (_, w, _) in rows] + [w for (w, _) in outs])
    vm = 2 * sum(_nbytes((t, w), a.dtype) for (a, w, _) in rows) + 2 * sum(_nbytes((t, w), dt) for (w, dt) in outs)
    vm += temps * _nbytes((t, widest), F32) + (2 << 20)
    res = pl.pallas_call(
        body, name=name, grid=(nrows // t,), in_specs=in_specs, out_specs=out_specs, out_shape=out_shape,
        compiler_params=_params(vm),
    )(*[_hbm(a) for (a, _, _) in rows], *consts)
    return list(res)


def _sb_tiles(s, tq, tk):
    tq = _pick(s, (tq, 256, 128))
    tk = _pick(tq, (tk, 128))
    return tq, tk, tq // tk


def _sb_fwd(zm, heads, *, name, tq=512, tk=256):
    s = zm.shape[0]
    tq, tk, nd = _sb_tiles(s, tq, tk)
    scale = SB_HD ** -0.5

    def body(q_ref, k_ref, v_ref, o_ref, a_out, stage, sem):
        h, i = pl.program_id(0), pl.program_id(1)
        qb = (q_ref[...] * scale).astype(BF16)
        r = lax.broadcasted_iota(jnp.int32, (tq, tk), 0)
        c = lax.broadcasted_iota(jnp.int32, (tq, tk), 1)
        ur = lax.broadcasted_iota(jnp.int32, (tk, tk), 0)
        uc = lax.broadcasted_iota(jnp.int32, (tk, tk), 1)
        usuf = (ur > uc).astype(BF16)

        def out_copy(slot, j):
            return pltpu.make_async_copy(stage.at[slot], a_out.at[h, i, j], sem.at[slot])

        def tile(j, carry, causal, slot, reuse):
            acc, cl = carry
            if reuse is True:
                out_copy(slot, 0).wait()
            elif reuse is not None:
                @pl.when(reuse)
                def _():
                    out_copy(slot, 0).wait()
            rows = pl.ds(pl.multiple_of(j * tk, tk), tk)
            kb = k_ref[rows, :].astype(BF16)
            vb = v_ref[rows, :].astype(BF16)
            z = lax.dot_general(qb, kb, NT, preferred_element_type=F32)
            lsig = -_softplus(z)
            l = lsig if causal is None else jnp.where(causal, lsig, 0.0)
            loga = z + lsig + _dot01(l, usuf) + cl
            if causal is not None:
                loga = jnp.where(causal, loga, NEG)
            ab = jnp.exp(loga).astype(BF16)
            acc = acc + lax.dot_general(ab, vb, NN, preferred_element_type=F32)
            stage[slot] = ab
            out_copy(slot, j).start()
            return acc, cl + jnp.sum(l, axis=1, keepdims=True)

        carry = (jnp.zeros((tq, SB_HD), F32), jnp.zeros((tq, 1), F32))
        for n, dd in enumerate(range(nd - 1, -1, -1)):
            carry = tile(i * nd + dd, carry, c + dd * tk < r, n, None)

        def rest(n, cr):
            return tile(i * nd - 1 - n, cr, None, (nd + n) % SB_SLOTS, nd + n >= SB_SLOTS)

        acc, _ = lax.fori_loop(0, i * nd, rest, carry)
        total = (i + 1) * nd
        for back in range(1, SB_SLOTS + 1):
            @pl.when(total >= back)
            def _():
                out_copy((total - back) % SB_SLOTS, 0).wait()

        o_ref[...] = acc.astype(o_ref.dtype)

    assert nd <= SB_SLOTS
    blk = lambda off: pl.BlockSpec((s, SB_HD), functools.partial(lambda h, i, off: (0, off + h), off=off))
    return pl.pallas_call(
        body, name=name, grid=(heads, s // tq),
        in_specs=[pl.BlockSpec((tq, SB_HD), lambda h, i: (i, h)), blk(heads), blk(2 * heads)],
        out_specs=[pl.BlockSpec((tq, SB_HD), lambda h, i: (i, h)), ANY],
        out_shape=[pltpu.HBM((s, heads * SB_HD), BF16), pltpu.HBM((heads, s // tq, s // tk, tq, tk), BF16)],
        scratch_shapes=[pltpu.VMEM((SB_SLOTS, tq, tk), BF16), pltpu.SemaphoreType.DMA((SB_SLOTS,))],
        compiler_params=_params(8 * s * SB_HD * 4 + 24 * tq * tk * 4 + (8 << 20)),
    )(_hbm(zm), _hbm(zm), _hbm(zm))


def _sb_bwd(zm, dy, a_all, after, heads, *, name, tq=512, tk=256):
    s = zm.shape[0]
    tq, tk, nd = _sb_tiles(s, tq, tk)
    nq = s // tq
    scale = SB_HD ** -0.5

    def body(q_ref, k_ref, v_ref, do_ref, a_in, after_ref, dq_ref, dk_ref, dv_ref, dka, dva, abuf, sem):
        h, i = pl.program_id(0), pl.program_id(1)

        @pl.when(i == 0)
        def _():
            dka[...] = jnp.zeros_like(dka)
            dva[...] = jnp.zeros_like(dva)

        qb = (q_ref[...] * scale).astype(BF16)
        dob = do_ref[...].astype(BF16)
        r = lax.broadcasted_iota(jnp.int32, (tq, tk), 0)
        c = lax.broadcasted_iota(jnp.int32, (tq, tk), 1)
        ur = lax.broadcasted_iota(jnp.int32, (tk, tk), 0)
        uc = lax.broadcasted_iota(jnp.int32, (tk, tk), 1)
        uexcl = (ur < uc).astype(BF16)

        def fetch(j, slot):
            return pltpu.make_async_copy(a_in.at[h, i, j], abuf.at[slot], sem.at[slot])

        total = (i + 1) * nd
        ahead = SB_SLOTS - 1

        def tile(j, carry, causal):
            dq, cg = carry
            slot = j % SB_SLOTS
            fetch(j, slot).wait()

            @pl.when(j + ahead < total)
            def _():
                fetch(j + ahead, (j + ahead) % SB_SLOTS).start()

            rows = pl.ds(pl.multiple_of(j * tk, tk), tk)
            kb = k_ref[rows, :].astype(BF16)
            vb = v_ref[rows, :].astype(BF16)
            z = lax.dot_general(qb, kb, NT, preferred_element_type=F32)
            sig = 1.0 / (1.0 + jnp.exp(-z))
            ab = abuf[slot]
            g = ab.astype(F32) * lax.dot_general(dob, vb, NT, preferred_element_type=F32)
            p = cg + lax.dot_general(g.astype(BF16), uexcl, NN, preferred_element_type=F32)
            dz = g - sig * (g + p)
            if causal is not None:
                dz = jnp.where(causal, dz, 0.0)
            dzb = dz.astype(BF16)
            dva[rows, :] += lax.dot_general(ab, dob, TN, preferred_element_type=F32)
            dka[rows, :] += lax.dot_general(dzb, qb, TN, preferred_element_type=F32)
            dq = dq + lax.dot_general(dzb, kb, NN, preferred_element_type=F32)
            return dq, cg + jnp.sum(g, axis=1, keepdims=True)

        for first in range(ahead):
            @pl.when(first < total)
            def _():
                fetch(first, first).start()

        init = (jnp.zeros((tq, SB_HD), F32), jnp.zeros((tq, 1), F32))
        carry = lax.fori_loop(0, i * nd, lambda j, cr: tile(j, cr, None), init)
        for dd in range(nd):
            carry = tile(i * nd + dd, carry, c + dd * tk < r)
        dq_ref[...] = (carry[0] * scale).astype(dq_ref.dtype)

        @pl.when(i == nq - 1)
        def _():
            dk_ref[...] = dka[...].astype(dk_ref.dtype)
            dv_ref[...] = dva[...].astype(dv_ref.dtype)

    blk = lambda off: pl.BlockSpec((s, SB_HD), functools.partial(lambda h, i, off: (0, off + h), off=off))
    tile_spec = pl.BlockSpec((tq, SB_HD), lambda h, i: (i, h))
    full = pltpu.HBM((s, heads * SB_HD), BF16)
    return pl.pallas_call(
        body, name=name, grid=(heads, nq),
        in_specs=[tile_spec, blk(heads), blk(2 * heads), tile_spec, ANY, ANY],
        out_specs=[tile_spec, blk(0), blk(0)],
        out_shape=[full, full, full],
        scratch_shapes=[pltpu.VMEM((s, SB_HD), F32), pltpu.VMEM((s, SB_HD), F32),
                        pltpu.VMEM((SB_SLOTS, tq, tk), BF16), pltpu.SemaphoreType.DMA((SB_SLOTS,))],
        compiler_params=_params(12 * s * SB_HD * 4 + 32 * tq * tk * 4 + (8 << 20)),
    )(_hbm(zm), _hbm(zm), _hbm(zm), _hbm(dy), a_all, after)


def _conv_taps(u, w_ref, rows_i):
    taps = []
    for j in range(CONV_W):
        sh = CONV_W - 1 - j
        if sh == 0:
            taps.append(u)
        else:
            taps.append(jnp.where(rows_i >= sh, pltpu.roll(u, sh, 0), 0.0))
    return taps


def _conv_fwd(zm, col0, width, cw, cb, *, name):
    s = zm.shape[0]
    bw = _pick(width, (LANES,))
    off = col0 // bw

    def body(u_ref, w_ref, b_ref, o_ref):
        u = u_ref[...]
        rows_i = lax.broadcasted_iota(jnp.int32, u.shape, 0)
        acc = jnp.broadcast_to(b_ref[...], u.shape)
        for j, tp in enumerate(_conv_taps(u, w_ref, rows_i)):
            acc = acc + tp * w_ref[j:j + 1, :]
        o_ref[...] = acc * _sigmoid(acc)

    return pl.pallas_call(
        body, name=name, grid=(width // bw,),
        in_specs=[pl.BlockSpec((s, bw), lambda j: (0, off + j)), pl.BlockSpec((CONV_W, bw), lambda j: (0, j)),
                  pl.BlockSpec((1, bw), lambda j: (0, j))],
        out_specs=pl.BlockSpec((s, bw), lambda j: (0, j)),
        out_shape=pltpu.HBM((s, width), F32),
        compiler_params=_params(12 * s * bw * 4 + (4 << 20)),
    )(_hbm(zm), cw, cb)


def _conv_bwd(zm, col0, width, cw, cb, dqk, *, name):
    s = zm.shape[0]
    bw = _pick(width, (LANES,))
    off = col0 // bw

    def body(u_ref, w_ref, b_ref, d_ref, du_ref, dw_ref, db_ref):
        u = u_ref[...]
        rows_i = lax.broadcasted_iota(jnp.int32, u.shape, 0)
        taps = _conv_taps(u, w_ref, rows_i)
        acc = jnp.broadcast_to(b_ref[...], u.shape)
        for j, tp in enumerate(taps):
            acc = acc + tp * w_ref[j:j + 1, :]
        sg = _sigmoid(acc)
        dc = d_ref[...] * (sg * (1.0 + acc * (1.0 - sg)))
        du = jnp.zeros_like(u)
        for j in range(CONV_W):
            sh = CONV_W - 1 - j
            if sh == 0:
                du = du + dc * w_ref[j:j + 1, :]
            else:
                du = du + jnp.where(rows_i < s - sh, pltpu.roll(dc, s - sh, 0), 0.0) * w_ref[j:j + 1, :]
            dw_ref[j:j + 1, :] = jnp.sum(dc * taps[j], axis=0, keepdims=True)
        du_ref[...] = du.astype(du_ref.dtype)
        db_ref[...] = jnp.sum(dc, axis=0, keepdims=True)

    return pl.pallas_call(
        body, name=name, grid=(width // bw,),
        in_specs=[pl.BlockSpec((s, bw), lambda j: (0, off + j)), pl.BlockSpec((CONV_W, bw), lambda j: (0, j)),
                  pl.BlockSpec((1, bw), lambda j: (0, j)), pl.BlockSpec((s, bw), lambda j: (0, j))],
        out_specs=[pl.BlockSpec((s, bw), lambda j: (0, j)), pl.BlockSpec((CONV_W, bw), lambda j: (0, j)),
                   pl.BlockSpec((1, bw), lambda j: (0, j))],
        out_shape=[pltpu.HBM((s, width), BF16), pltpu.HBM((CONV_W, width), F32),
                   pltpu.HBM((1, width), F32)],
        compiler_params=_params(20 * s * bw * 4 + (4 << 20)),
    )(_hbm(zm), cw, cb, _hbm(dqk))


def _ml_gates(gcol_ref, grow_ref):
    l = CHUNK
    r = lax.broadcasted_iota(jnp.int32, (l, l), 0)
    c = lax.broadcasted_iota(jnp.int32, (l, l), 1)
    gcol = gcol_ref[...]
    grow = grow_ref[0]
    bcol = _u01dot((c <= r).astype(BF16), gcol)
    brow = _dot01(grow, (r <= c).astype(BF16))
    return gcol, grow, bcol, brow, r >= c


def _ml_chunk(h, dh, mq_ref, mk_ref, v_ref, gates, cp, n_prev, m_prev):
    gcol, grow, bcol, brow, tri = gates
    l = CHUNK
    sl = slice(h * dh, (h + 1) * dh)
    qc = mq_ref[:, sl]
    kc = mk_ref[:, sl] * (dh ** -0.5)
    vc = v_ref[:, sl]
    i_row = grow[h:h + 1, :]
    i_col = gcol[:, h:h + 1]
    b_col = bcol[:, ML_HEADS + h:ML_HEADS + h + 1]
    b_row = brow[ML_HEADS + h:ML_HEADS + h + 1, :]
    b_end = b_col[l - 1:l, :]
    d = jnp.where(tri, b_col - b_row + i_row, -jnp.inf)
    m_inter = b_col + m_prev
    m_t = jnp.maximum(m_inter, jnp.max(d, axis=1, keepdims=True))
    w = jnp.exp(d - m_t)
    s_inter = jnp.exp(m_inter - m_t)
    qb, kb, vb = qc.astype(BF16), kc.astype(BF16), vc.astype(BF16)
    cpb = cp.astype(BF16)
    a = lax.dot_general(qb, kb, NT, preferred_element_type=F32)
    sc = a * w
    qcp = lax.dot_general(qb, cpb, NT, preferred_element_type=F32)
    qn = jnp.sum(qc * n_prev, axis=1, keepdims=True)
    num = lax.dot_general(sc.astype(BF16), vb, NN, preferred_element_type=F32) + s_inter * qcp
    den = jnp.sum(sc, axis=1, keepdims=True) + s_inter * qn
    floor = jnp.exp(-m_t)
    dnm = jnp.maximum(jnp.abs(den), floor)
    g_col = b_end - b_col + i_col
    g_row = b_end - b_row + i_row
    m_new = jnp.maximum(b_end + m_prev, jnp.max(g_row, axis=1, keepdims=True))
    decay = jnp.exp(b_end + m_prev - m_new)
    wk = jnp.exp(g_col - m_new)
    return dict(qc=qc, kc=kc, vc=vc, qb=qb, kb=kb, vb=vb, cpb=cpb, w=w, s_inter=s_inter, a=a, sc=sc, qcp=qcp, qn=qn,
                num=num, den=den, floor=floor, dnm=dnm, m_new=m_new, decay=decay, wk=wk, sl=sl)


def _ml_fwd(mqk, zm, vcol, gcol, grow, d_model, *, name):
    s = zm.shape[0]
    nc = s // CHUNK
    dh = d_model // ML_HEADS
    hh = ML_HEADS

    def body(mq_ref, mk_ref, v_ref, gcol_ref, grow_ref, h_ref, cs_ref, ns_ref, ms_ref, c_s, n_s, m_s):
        @pl.when(pl.program_id(0) == 0)
        def _():
            c_s[...] = jnp.zeros_like(c_s)
            n_s[...] = jnp.zeros_like(n_s)
            m_s[...] = jnp.zeros_like(m_s)

        gates = _ml_gates(gcol_ref, grow_ref)
        for h in range(hh):
            cp, n_prev, m_prev = c_s[h], n_s[h], m_s[h][:, 0:1]
            cs_ref[0, h] = cp
            ns_ref[0, h] = n_prev
            ms_ref[0, h] = m_s[h]
            f = _ml_chunk(h, dh, mq_ref, mk_ref, v_ref, gates, cp, n_prev, m_prev)
            h_ref[:, f["sl"]] = f["num"] / f["dnm"]
            c_s[h] = f["decay"] * cp + lax.dot_general((f["vc"] * f["wk"]).astype(BF16), f["kb"], TN,
                                                       preferred_element_type=F32)
            n_s[h] = f["decay"] * n_prev + jnp.sum(f["wk"] * f["kc"], axis=0, keepdims=True)
            m_s[h] = jnp.broadcast_to(f["m_new"], (1, LANES))

    dblk = d_model
    return pl.pallas_call(
        body, name=name, grid=(nc,),
        in_specs=[pl.BlockSpec((CHUNK, dblk), lambda c: (c, 0)), pl.BlockSpec((CHUNK, dblk), lambda c: (c, 1)),
                  pl.BlockSpec((CHUNK, dblk), lambda c: (c, vcol // dblk)),
                  pl.BlockSpec((CHUNK, LANES), lambda c: (c, 0)), pl.BlockSpec((1, 8, CHUNK), lambda c: (c, 0, 0))],
        out_specs=[pl.BlockSpec((CHUNK, dblk), lambda c: (c, 0)),
                   pl.BlockSpec((1, hh, dh, dh), lambda c: (c, 0, 0, 0)),
                   pl.BlockSpec((1, hh, 1, dh), lambda c: (c, 0, 0, 0)),
                   pl.BlockSpec((1, hh, 1, LANES), lambda c: (c, 0, 0, 0))],
        out_shape=[pltpu.HBM((s, d_model), F32), pltpu.HBM((nc, hh, dh, dh), F32),
                   pltpu.HBM((nc, hh, 1, dh), F32), pltpu.HBM((nc, hh, 1, LANES), F32)],
        scratch_shapes=[pltpu.VMEM((hh, dh, dh), F32), pltpu.VMEM((hh, 1, dh), F32), pltpu.VMEM((hh, 1, LANES), F32)],
        compiler_params=_params(8 * hh * dh * dh * 4 + (16 << 20)),
    )(_hbm(mqk), _hbm(mqk), _hbm(zm), _hbm(gcol), _hbm(grow))


def _ml_bwd(mqk, zm, vcol, gcol, grow, cs, ns, ms, dhm, d_model, *, name):
    s = zm.shape[0]
    nc = s // CHUNK
    dh = d_model // ML_HEADS
    hh = ML_HEADS
    l = CHUNK

    def body(mq_ref, mk_ref, v_ref, gcol_ref, grow_ref, cs_ref, ns_ref, ms_ref, dh_ref,
             dqk_ref, dv_ref, dgc_ref, dgr_ref, dc_s, dn_s):
        @pl.when(pl.program_id(0) == 0)
        def _():
            dc_s[...] = jnp.zeros_like(dc_s)
            dn_s[...] = jnp.zeros_like(dn_s)

        gates = _ml_gates(gcol_ref, grow_ref)
        lane = lax.broadcasted_iota(jnp.int32, (l, LANES), 1)
        rowi = lax.broadcasted_iota(jnp.int32, (8, l), 0)
        lastrow = lax.broadcasted_iota(jnp.int32, (l, 1), 0) == l - 1
        dgc = jnp.zeros((l, LANES), F32)
        dgr = jnp.zeros((8, l), F32)
        for h in range(hh):
            cp, n_prev, m_prev = cs_ref[0, h], ns_ref[0, h], ms_ref[0, h][:, 0:1]
            f = _ml_chunk(h, dh, mq_ref, mk_ref, v_ref, gates, cp, n_prev, m_prev)
            dC, dn = dc_s[h], dn_s[h]
            dhv = dh_ref[:, f["sl"]]
            dnum = dhv / f["dnm"]
            hv = f["num"] / f["dnm"]
            ddnm = -jnp.sum(dhv * hv, axis=1, keepdims=True) / f["dnm"]
            dden = jnp.where(jnp.abs(f["den"]) >= f["floor"], ddnm * jnp.sign(f["den"]), 0.0)
            dnb = dnum.astype(BF16)
            dsc = lax.dot_general(dnb, f["vb"], NT, preferred_element_type=F32) + dden
            dvc = lax.dot_general(f["sc"].astype(BF16), dnb, TN, preferred_element_type=F32)
            ds_inter = jnp.sum(dnum * f["qcp"], axis=1, keepdims=True) + dden * f["qn"]
            sdn = (f["s_inter"] * dnum).astype(BF16)
            sdd = f["s_inter"] * dden
            da = dsc * f["w"]
            dab = da.astype(BF16)
            dqc = (lax.dot_general(dab, f["kb"], NN, preferred_element_type=F32)
                   + lax.dot_general(sdn, f["cpb"], NN, preferred_element_type=F32) + sdd * n_prev)
            dcp = f["decay"] * dC + lax.dot_general(sdn, f["qb"], TN, preferred_element_type=F32)
            dnp = f["decay"] * dn + jnp.sum(sdd * f["qc"], axis=0, keepdims=True)
            vw = (f["vc"] * f["wk"]).astype(BF16)
            dCb = dC.astype(BF16)
            dkc = (lax.dot_general(dab, f["qb"], TN, preferred_element_type=F32)
                   + lax.dot_general(vw, dCb, NN, preferred_element_type=F32) + f["wk"] * dn)
            e = lax.dot_general(f["kb"], dCb, NT, preferred_element_type=F32)
            dvc = dvc + e * f["wk"]
            dwk = jnp.sum(e * f["vc"], axis=1, keepdims=True) + jnp.sum(f["kc"] * dn, axis=1, keepdims=True)
            ddecay = jnp.sum(jnp.sum(dC * cp, axis=1, keepdims=True), axis=0, keepdims=True) \
                + jnp.sum(dn * n_prev, axis=1, keepdims=True)
            dd = dsc * f["sc"]
            dlw = dwk * f["wk"]
            db_end = jnp.sum(dlw, axis=0, keepdims=True) + ddecay * f["decay"]
            di_col = dlw
            db_col = jnp.sum(dd, axis=1, keepdims=True) + ds_inter * f["s_inter"] - dlw \
                + jnp.where(lastrow, db_end, 0.0)
            cs_dd = jnp.sum(dd, axis=0, keepdims=True)
            dgc = dgc + jnp.where(lane == h, di_col, 0.0) + jnp.where(lane == hh + h, db_col, 0.0)
            dgr = dgr + jnp.where(rowi == h, cs_dd, 0.0) - jnp.where(rowi == hh + h, cs_dd, 0.0)
            dqk_ref[:, f["sl"]] = dqc
            dqk_ref[:, d_model + h * dh:d_model + (h + 1) * dh] = dkc * (dh ** -0.5)
            dv_ref[:, f["sl"]] = dvc.astype(dv_ref.dtype)
            dc_s[h] = dcp
            dn_s[h] = dnp
        dgc_ref[...] = dgc
        dgr_ref[0] = dgr

    dblk = d_model
    rev = lambda c: nc - 1 - c
    return pl.pallas_call(
        body, name=name, grid=(nc,),
        in_specs=[pl.BlockSpec((l, dblk), lambda c: (rev(c), 0)), pl.BlockSpec((l, dblk), lambda c: (rev(c), 1)),
                  pl.BlockSpec((l, dblk), lambda c: (rev(c), vcol // dblk)),
                  pl.BlockSpec((l, LANES), lambda c: (rev(c), 0)), pl.BlockSpec((1, 8, l), lambda c: (rev(c), 0, 0)),
                  pl.BlockSpec((1, hh, dh, dh), lambda c: (rev(c), 0, 0, 0)),
                  pl.BlockSpec((1, hh, 1, dh), lambda c: (rev(c), 0, 0, 0)),
                  pl.BlockSpec((1, hh, 1, LANES), lambda c: (rev(c), 0, 0, 0)),
                  pl.BlockSpec((l, dblk), lambda c: (rev(c), 0))],
        out_specs=[pl.BlockSpec((l, 2 * dblk), lambda c: (rev(c), 0)),
                   pl.BlockSpec((l, dblk), lambda c: (rev(c), 0)), pl.BlockSpec((l, LANES), lambda c: (rev(c), 0)),
                   pl.BlockSpec((1, 8, l), lambda c: (rev(c), 0, 0))],
        out_shape=[pltpu.HBM((s, 2 * d_model), F32),
                   pltpu.HBM((s, d_model), BF16), pltpu.HBM((s, LANES), F32),
                   pltpu.HBM((nc, 8, l), F32)],
        scratch_shapes=[pltpu.VMEM((hh, dh, dh), F32), pltpu.VMEM((hh, 1, dh), F32)],
        compiler_params=_params(10 * hh * dh * dh * 4 + (16 << 20)),
    )(*[_hbm(a) for a in (mqk, mqk, zm, gcol, grow, cs, ns, ms, dhm)])


def _xa_fwd(zm, qcol, kv, gq, gk, d_model, *, name, tq=512):
    s = zm.shape[0]
    nm = kv.shape[0]
    dh = d_model // X_HEADS
    tq = _pick(s, (tq, 128, 64))
    scale = dh ** -0.5

    def body(q_ref, k_ref, v_ref, gq_ref, gk_ref, o_ref):
        qn = _rms_fwd(q_ref[...], gq_ref[...])
        kn = _rms_fwd(k_ref[...], gk_ref[...])
        lg = _dot(qn, kn, NT) * scale
        lg = lg - jnp.max(lg, axis=1, keepdims=True)
        p = jnp.exp(lg)
        p = p / jnp.sum(p, axis=1, keepdims=True)
        o_ref[...] = _dot(p, v_ref[...], NN).astype(o_ref.dtype)

    return pl.pallas_call(
        body, name=name, grid=(X_HEADS, s // tq),
        in_specs=[pl.BlockSpec((tq, dh), lambda h, i: (i, qcol // dh + h)), pl.BlockSpec((nm, dh), lambda h, i: (0, h)),
                  pl.BlockSpec((nm, dh), lambda h, i: (0, X_HEADS + h)),
                  pl.BlockSpec((1, dh), lambda h, i: (0, 0)), pl.BlockSpec((1, dh), lambda h, i: (0, 0))],
        out_specs=pl.BlockSpec((tq, dh), lambda h, i: (i, h)),
        out_shape=pltpu.HBM((s, d_model), BF16),
        compiler_params=_params(32 << 20),
    )(_hbm(zm), _hbm(kv), _hbm(kv), gq, gk)


def _xa_bwd(zm, qcol, kv, gq, gk, dy, d_model, *, name, tq=512):
    s = zm.shape[0]
    nm = kv.shape[0]
    dh = d_model // X_HEADS
    tq = _pick(s, (tq, 128, 64))
    nq = s // tq
    scale = dh ** -0.5

    def body(q_ref, k_ref, v_ref, gq_ref, gk_ref, do_ref, dq_ref, dkn_ref, dv_ref, dgq_ref):
        h, i = pl.program_id(0), pl.program_id(1)

        @pl.when(i == 0)
        def _():
            dkn_ref[...] = jnp.zeros_like(dkn_ref)
            dv_ref[...] = jnp.zeros_like(dv_ref)

        @pl.when((i == 0) & (h == 0))
        def _():
            dgq_ref[...] = jnp.zeros_like(dgq_ref)

        q = q_ref[...]
        qn = _rms_fwd(q, gq_ref[...])
        kn = _rms_fwd(k_ref[...], gk_ref[...])
        lg = _dot(qn, kn, NT) * scale
        lg = lg - jnp.max(lg, axis=1, keepdims=True)
        p = jnp.exp(lg)
        p = p / jnp.sum(p, axis=1, keepdims=True)
        do = do_ref[...]
        dv_ref[...] += _dot(p, do, TN)
        dp = _dot(do, v_ref[...], NT)
        dlg = p * (dp - jnp.sum(dp * p, axis=1, keepdims=True)) * scale
        dqn = _dot(dlg, kn, NN)
        dkn_ref[...] += _dot(dlg, qn, TN)
        dq, dgq = _rms_bwd(q, gq_ref[...], dqn)
        dq_ref[...] = dq.astype(dq_ref.dtype)
        dgq_ref[...] += jnp.sum(dgq, axis=0, keepdims=True)

    return pl.pallas_call(
        body, name=name, grid=(X_HEADS, nq),
        in_specs=[pl.BlockSpec((tq, dh), lambda h, i: (i, qcol // dh + h)), pl.BlockSpec((nm, dh), lambda h, i: (0, h)),
                  pl.BlockSpec((nm, dh), lambda h, i: (0, X_HEADS + h)),
                  pl.BlockSpec((1, dh), lambda h, i: (0, 0)), pl.BlockSpec((1, dh), lambda h, i: (0, 0)),
                  pl.BlockSpec((tq, dh), lambda h, i: (i, h))],
        out_specs=[pl.BlockSpec((tq, dh), lambda h, i: (i, h)), pl.BlockSpec((nm, dh), lambda h, i: (0, h)),
                   pl.BlockSpec((nm, dh), lambda h, i: (0, h)), pl.BlockSpec((1, dh), lambda h, i: (0, 0))],
        out_shape=[pltpu.HBM((s, d_model), BF16), pltpu.HBM((nm, d_model), F32),
                   pltpu.HBM((nm, d_model), F32), pltpu.HBM((1, dh), F32)],
        compiler_params=_params(32 << 20),
    )(_hbm(zm), _hbm(kv), _hbm(kv), gq, gk, _hbm(dy))


def _place():
    return lax.axis_index("x"), lax.axis_index("y"), lax.axis_index("c")


ANY = pl.BlockSpec(memory_space=pl.ANY)


def _allgather_two_level(big, small, *, name, chunk_rows=64):
    r, cc = big.shape
    half = r // 2
    nr = _pick(half, (chunk_rows, 32, 16))
    nq = half // nr

    def body(big_ref, small_ref, obig, osmall, land, passed, send, recv, fsend, frecv, out_a, out_b, ssend, srecv, loc):
        x, y, c = _place()
        k = 2 * x + y
        chips = [(1 - x, y), (x, 1 - y), (1 - x, 1 - y)]
        slots = [2 * px + py for px, py in chips]
        local = [pltpu.make_async_copy(big_ref, obig.at[k], loc.at[0]),
                 pltpu.make_async_copy(small_ref, osmall.at[k], loc.at[1])]
        for cp in local:
            cp.start()

        def rows(h, q):
            return pl.ds(pl.multiple_of(h * half + q * nr, nr), nr)

        def chunk(q):
            return pl.ds(q * nr, nr)

        def over_ici(j, q):
            return pltpu.make_async_remote_copy(
                src_ref=big_ref.at[rows(c, q)], dst_ref=land.at[j, chunk(q)], send_sem=send.at[nq * j + q],
                recv_sem=recv.at[nq * j + q], device_id=(chips[j][0], chips[j][1], c), device_id_type=MESH)

        def to_sibling(j, q):
            return pltpu.make_async_remote_copy(
                src_ref=land.at[j, chunk(q)], dst_ref=passed.at[j, chunk(q)], send_sem=fsend.at[nq * j + q],
                recv_sem=frecv.at[nq * j + q], device_id=(x, y, 1 - c), device_id_type=MESH)

        def small_copy(j, slot):
            return pltpu.make_async_remote_copy(
                src_ref=small_ref, dst_ref=osmall.at[slot], send_sem=ssend.at[j], recv_sem=srecv.at[j],
                device_id=(chips[j][0], chips[j][1], c), device_id_type=MESH)

        for q in range(nq):
            for j in range(3):
                over_ici(j, q).start()
        for j in range(3):
            small_copy(j, k).start()
        for q in range(nq):
            for j in range(3):
                over_ici(j, q).wait_recv()
                to_sibling(j, q).start()
                cp = pltpu.make_async_copy(land.at[j, chunk(q)], obig.at[slots[j], rows(c, q)], out_a.at[nq * j + q])
                cp.start()
                local.append(cp)
        for q in range(nq):
            for j in range(3):
                to_sibling(j, q).wait_recv()
                cp = pltpu.make_async_copy(passed.at[j, chunk(q)], obig.at[slots[j], rows(1 - c, q)],
                                           out_b.at[nq * j + q])
                cp.start()
                local.append(cp)
        for j in range(3):
            small_copy(j, slots[j]).wait_recv()
            small_copy(j, k).wait_send()
        for q in range(nq):
            for j in range(3):
                over_ici(j, q).wait_send()
                to_sibling(j, q).wait_send()
        for cp in local:
            cp.wait()

    stage = 2 * _nbytes((3, half, cc), big.dtype)
    return pl.pallas_call(
        body, name=name, in_specs=[ANY] * 2, out_specs=[ANY] * 2,
        out_shape=[pltpu.HBM((4,) + big.shape, big.dtype), pltpu.HBM((4,) + small.shape, small.dtype)],
        scratch_shapes=[pltpu.VMEM((3, half, cc), big.dtype), pltpu.VMEM((3, half, cc), big.dtype)]
        + [pltpu.SemaphoreType.DMA((3 * nq,))] * 6
        + [pltpu.SemaphoreType.DMA((3,)), pltpu.SemaphoreType.DMA((3,)), pltpu.SemaphoreType.DMA((2,))],
        compiler_params=_params(stage + stage // 8 + (4 << 20)),
    )(big, small)


HBM_SPEC = pl.BlockSpec(memory_space=pltpu.HBM)
SEM_SPEC = pl.BlockSpec(memory_space=pltpu.SEMAPHORE)
EFFECT = pltpu.SideEffectType.DATAFLOW_SIDE_EFFECTING


def _split_copies(kind, srcs, lands, send, recv):
    x, y, c = _place()
    if kind == "quarters":
        peers = [(1 - x, y, c), (x, 1 - y, c), (1 - x, 1 - y, c)]
    else:
        peers = [(x ^ ((j >> 2) & 1), y ^ ((j >> 1) & 1), c ^ (j & 1)) for j in range(1, 8)]
    npeer = len(peers)
    out = []
    for t in range(len(srcs)):
        for j, (px, py, pc) in enumerate(peers):
            if kind == "quarters":
                src, mine, theirs = srcs[t], 2 * x + y, 2 * px + py
            else:
                src, mine, theirs = srcs[t].at[2 * px + py, pc], 4 * x + 2 * y + c, 4 * px + 2 * py + pc
            mk = functools.partial(
                pltpu.make_async_remote_copy, src_ref=src, send_sem=send.at[npeer * t + j],
                recv_sem=recv.at[npeer * t + j], device_id=(px, py, pc), device_id_type=MESH)
            out.append((functools.partial(mk, dst_ref=lands[t].at[mine]),
                        functools.partial(mk, dst_ref=lands[t].at[theirs])))
    return out


def _split_start(kind, srcs, land_shapes, after, *, name):
    n = len(srcs)
    ncopies = n * (3 if kind == "quarters" else 7)

    def body(*refs):
        ins, lands = refs[:n], refs[n:2 * n]
        send, recv = refs[2 * n + 1], refs[2 * n + 2]
        token = refs[-1]
        for start, _ in _split_copies(kind, ins, lands, send, recv):
            start().start()
        token[...] = jnp.zeros_like(token)

    lands = [_hbm(lax.empty(shp, a.dtype)) for shp, a in zip(land_shapes, srcs)]
    res = pl.pallas_call(
        body, name=name, in_specs=[HBM_SPEC] * (2 * n) + [ANY],
        out_specs=[SEM_SPEC, SEM_SPEC] + [HBM_SPEC] * (2 * n) + [pl.BlockSpec(memory_space=pltpu.VMEM)],
        out_shape=[pltpu.SemaphoreType.DMA((ncopies,)), pltpu.SemaphoreType.DMA((ncopies,))]
        + [pltpu.HBM(a.shape, a.dtype) for a in srcs] + [pltpu.HBM(shp, a.dtype) for shp, a in zip(land_shapes, srcs)]
        + [jax.ShapeDtypeStruct((8, LANES), F32)],
        input_output_aliases={i: 2 + i for i in range(2 * n)},
        compiler_params=pltpu.CompilerParams(has_side_effects=EFFECT),
    )(*[_hbm(a) for a in srcs], *lands, after)
    return res[0], res[1], list(res[2:2 + n]), list(res[2 + n:2 + 2 * n]), res[-1]


def _split_wait(kind, send, recv, srcs, lands, after, *, name):
    n = len(srcs)

    def body(*refs):
        ins, lnd = refs[:n], refs[n:2 * n]
        snd, rcv = refs[2 * n], refs[2 * n + 1]
        for start, arrive in _split_copies(kind, ins, lnd, snd, rcv):
            start().wait_send()
            arrive().wait_recv()

    res = pl.pallas_call(
        body, name=name, in_specs=[HBM_SPEC] * (2 * n) + [SEM_SPEC, SEM_SPEC] + [ANY] * len(after),
        out_specs=[HBM_SPEC] * (2 * n),
        out_shape=[pltpu.HBM(a.shape, a.dtype) for a in srcs] + [pltpu.HBM(a.shape, a.dtype) for a in lands],
        input_output_aliases={i: i for i in range(2 * n)},
        compiler_params=pltpu.CompilerParams(has_side_effects=EFFECT),
    )(*srcs, *lands, send, recv, *after)
    return list(res[n:])


def _sum8(parts, *, name):
    _, r, c = parts.shape
    t = _pick(r, (128, 64, 32, 16, 8))

    def body(p_ref, o_ref):
        acc = p_ref[0].astype(F32)
        for k in range(1, 8):
            acc = acc + p_ref[k].astype(F32)
        o_ref[...] = acc

    return pl.pallas_call(
        body, name=name, grid=(r // t,), in_specs=[pl.BlockSpec((8, t, c), lambda i: (0, i, 0))],
        out_specs=pl.BlockSpec((t, c), lambda i: (i, 0)), out_shape=pltpu.HBM((r, c), F32),
        compiler_params=_params(2 * 8 * t * c * 2 + 6 * t * c * 4 + (4 << 20)),
    )(_hbm(parts))


def _swap_halves(halves, *, name, chunk_bytes=512 * 1024):
    n = len(halves)
    items = []
    for t, a in enumerate(halves):
        r = a.shape[0]
        k = 1
        while _nbytes(a.shape, a.dtype) // k > chunk_bytes and r % (2 * k) == 0 and (r // (2 * k)) % 8 == 0:
            k *= 2
        items += [(t, q * (r // k), r // k) for q in range(k)]
    m = len(items)

    def body(*refs):
        ins, outs = refs[:n], refs[n:2 * n]
        sbuf, rbuf = refs[2 * n:3 * n], refs[3 * n:4 * n]
        send, recv, loc_own, loc_in, loc_out = refs[4 * n:]
        x, y, c = _place()
        local, stage = [], []
        for t in range(n):
            cp = pltpu.make_async_copy(ins[t], outs[t].at[c], loc_own.at[t])
            cp.start()
            local.append(cp)
        for q, (t, r0, nr) in enumerate(items):
            cp = pltpu.make_async_copy(ins[t].at[pl.ds(r0, nr)], sbuf[t].at[pl.ds(r0, nr)], loc_in.at[q])
            cp.start()
            stage.append(cp)

        def copy(q):
            t, r0, nr = items[q]
            return pltpu.make_async_remote_copy(
                src_ref=sbuf[t].at[pl.ds(r0, nr)], dst_ref=rbuf[t].at[pl.ds(r0, nr)], send_sem=send.at[q],
                recv_sem=recv.at[q], device_id=(x, y, 1 - c), device_id_type=MESH)

        for q in range(m):
            stage[q].wait()
            copy(q).start()
        for q, (t, r0, nr) in enumerate(items):
            copy(q).wait_recv()
            cp = pltpu.make_async_copy(rbuf[t].at[pl.ds(r0, nr)], outs[t].at[1 - c, pl.ds(r0, nr)], loc_out.at[q])
            cp.start()
            local.append(cp)
        for q in range(m):
            copy(q).wait_send()
        for cp in local:
            cp.wait()

    stage_bytes = 2 * sum(_nbytes(a.shape, a.dtype) for a in halves)
    return pl.pallas_call(
        body, name=name, in_specs=[ANY] * n, out_specs=[ANY] * n,
        out_shape=[pltpu.HBM((2,) + a.shape, a.dtype) for a in halves],
        scratch_shapes=[pltpu.VMEM(a.shape, a.dtype) for a in halves] * 2
        + [pltpu.SemaphoreType.DMA((m,)), pltpu.SemaphoreType.DMA((m,)), pltpu.SemaphoreType.DMA((n,)),
           pltpu.SemaphoreType.DMA((m,)), pltpu.SemaphoreType.DMA((m,))],
        compiler_params=_params(stage_bytes + (4 << 20)),
    )(*halves)


def _allreduce_small(p, after, *, name):
    r = p.shape[0]

    def body(p_ref, after_ref, o_ref, buf, send, recv):
        x, y, c = _place()
        me = 4 * x + 2 * y + c
        peers = [(x ^ ((j >> 2) & 1), y ^ ((j >> 1) & 1), c ^ (j & 1)) for j in range(1, 8)]

        def copy(j, slot):
            return pltpu.make_async_remote_copy(
                src_ref=p_ref, dst_ref=buf.at[slot], send_sem=send.at[j], recv_sem=recv.at[j],
                device_id=peers[j], device_id_type=MESH)

        for j in range(7):
            copy(j, me).start()
        buf[me] = p_ref[...]
        for j in range(7):
            px, py, pc = peers[j]
            copy(j, 4 * px + 2 * py + pc).wait_recv()
        for j in range(7):
            copy(j, me).wait_send()
        acc = buf[0]
        for k in range(1, 8):
            acc = acc + buf[k]
        o_ref[...] = acc

    vspec = pl.BlockSpec(memory_space=pltpu.VMEM)
    return pl.pallas_call(
        body, name=name, in_specs=[vspec, ANY], out_specs=vspec, out_shape=jax.ShapeDtypeStruct((r, LANES), F32),
        scratch_shapes=[pltpu.VMEM((8, r, LANES), F32), pltpu.SemaphoreType.DMA((7,)), pltpu.SemaphoreType.DMA((7,))],
    )(p, after)


def _adamw_fn(w, g, m, v):
    m = ADAM_B1 * m + (1.0 - ADAM_B1) * g
    v = ADAM_B2 * v + (1.0 - ADAM_B2) * (g * g)
    m_hat = m / (1.0 - ADAM_B1 ** ADAM_STEP)
    v_hat = v / (1.0 - ADAM_B2 ** ADAM_STEP)
    delta = -ADAM_LR * (m_hat / (jnp.sqrt(v_hat) + ADAM_EPS) + ADAM_WD * w)
    return delta, m, v


def _adamw(w, g, m, v, *, name):
    c = w.shape[1]
    return _rowwise(_adamw_fn, [w, g, m, v], [], [(c, F32)] * 3, name=name, tr=128)


def _pack(vecs, rows):
    flat = jnp.concatenate([a.reshape(-1).astype(F32) for a in vecs])
    return jnp.pad(flat, (0, rows * LANES - flat.shape[0])).reshape(rows, LANES)


def _unpack(p, like):
    flat, out, o = p.reshape(-1), [], 0
    for a in like:
        out.append(flat[o:o + a.size].reshape(a.shape))
        o += a.size
    return out


def kernel(x, mem, g_mix, w_in, b_if, b_gate, conv_w, conv_b, ml_norm_g, g_mem, w_mem_kv, q_norm_g, k_norm_g, w_sb_proj, w_ml_proj, w_x_proj, w_out, g_mlp, w_ff1, w_ff2, loss_target, m_g_mix, m_w_in, m_b_if, m_b_gate, m_conv_w, m_conv_b, m_ml_norm_g, m_g_mem, m_w_mem_kv, m_q_norm_g, m_k_norm_g, m_w_sb_proj, m_w_ml_proj, m_w_x_proj, m_w_out, m_g_mlp, m_w_ff1, m_w_ff2, v_g_mix, v_w_in, v_b_if, v_b_gate, v_conv_w, v_conv_b, v_ml_norm_g, v_g_mem, v_w_mem_kv, v_q_norm_g, v_k_norm_g, v_w_sb_proj, v_w_ml_proj, v_w_x_proj, v_w_out, v_g_mlp, v_w_ff1, v_w_ff2):
    _, s, d = x.shape
    nm = mem.shape[1]
    n_in = 4 * w_in.shape[2]
    dff = 4 * w_ff1.shape[2]
    sbh = d // SB_HD
    hh = ML_HEADS
    dh = d // hh
    nc = s // CHUNK
    assert n_in == 11 * d + 2 * hh and d % (2 * LANES) == 0 and s % LANES == 0
    x2, mem2, tgt = x[0], mem[0], loss_target[0]

    k4 = 2 * lax.axis_index("x") + lax.axis_index("y")
    me = 2 * k4 + lax.axis_index("c")
    g_first = _allgather_two_level(w_in[0].astype(BF16), conv_w[0], name="gather_w_in")
    later = [a[0].astype(BF16) for a in (w_mem_kv, w_sb_proj, w_ml_proj, w_x_proj, w_out, w_ff1, w_ff2)]
    gw_send, gw_recv, gw_src, gw_land, gw_token = _split_start(
        "quarters", later, [(4,) + a.shape for a in later], g_first[0], name="gather_rest_start")
    cols = lambda a: a.transpose(1, 0, 2).reshape(a.shape[1], 4 * a.shape[2])
    rws = lambda a: a.reshape(4 * a.shape[1], a.shape[2])
    qn = n_in // 4
    if_lo, if_hi = 7 * d, 7 * d + 2 * hh

    def cut(lo, hi):
        ks = [(k, max(lo, k * qn), min(hi, (k + 1) * qn)) for k in range(4)]
        return [g_first[0][k, :, a - k * qn:b - k * qn] for k, a, b in ks if a < b]

    w_main = jnp.concatenate(cut(0, if_lo) + cut(if_hi, n_in), axis=1)
    w_if = jnp.pad(jnp.concatenate(cut(if_lo, if_hi), axis=1), ((0, 0), (0, LANES - 2 * hh)))
    conv_wf = cols(g_first[1])
    b_if_p = jnp.pad(b_if, ((0, 0), (0, LANES - 2 * hh)))

    (hn,) = _rowwise(_rms_fwd, [x2], [g_mix], [(d, BF16)], name="norm_in", tr=512)
    zm = _mm(hn, w_main, after=gw_token, name="proj_in")
    zif = _mm(hn, w_if, name="proj_if")
    y_sb, a_sb = _sb_fwd(zm, sbh, name="sb_fwd")

    def gate_fn(z, b):
        pre = z + b
        lane = lax.broadcasted_iota(jnp.int32, pre.shape, 1)
        return jnp.where(lane < hh, pre, -_softplus(-pre))

    (gcol,) = _rowwise(gate_fn, [zif], [b_if_p], [(LANES, F32)], name="ml_gates", tr=1024)
    grow = gcol[:, :8].T.reshape(8, nc, CHUNK).transpose(1, 0, 2)
    mqk = _conv_fwd(zm, 3 * d, 2 * d, conv_wf, conv_b, name="conv_fwd")
    hm, cst, nst, mst = _ml_fwd(mqk, zm, 5 * d, gcol, grow, d, name="ml_fwd")

    def mlout_fn(hv, o, g):
        ys = [_rms_fwd(hv[:, k * dh:(k + 1) * dh], g[:, k * dh:(k + 1) * dh]) for k in range(hh)]
        return jnp.concatenate(ys, axis=1) * _sigmoid(o)

    (y_ml,) = _rowwise(mlout_fn, [hm, (zm, d, 6)], [ml_norm_g], [(d, BF16)], name="ml_out", tr=512)
    gw_land = _split_wait("quarters", gw_send, gw_recv, gw_src, gw_land, [y_ml, y_sb], name="gather_rest_wait")
    gw = [lax.dynamic_update_index_in_dim(ld, a, k4, 0) for ld, a in zip(gw_land, later)]
    w_kv, w_sbp, w_mlp, w_xp, w_o, w_f1, w_f2 = (cols(gw[0]), rws(gw[1]), rws(gw[2]), rws(gw[3]), rws(gw[4]),
                                                 cols(gw[5]), rws(gw[6]))
    (memn,) = _rowwise(_rms_fwd, [mem2], [g_mem], [(d, BF16)], name="norm_mem")
    kv = _mm(memn, w_kv, name="proj_kv")
    y_x = _xa_fwd(zm, 7 * d, kv, q_norm_g, k_norm_g, d, name="xa_fwd")
    p_sb = _mm(y_sb, w_sbp, name="proj_sb")
    p_ml = _mm(y_ml, w_mlp, name="proj_ml")
    p_x = _mm(y_x, w_xp, name="proj_x")

    def merge_fn(a, b, c, g0, g1, g2, bg):
        return (_sigmoid(g0 + bg[:, :d]) * a + _sigmoid(g1 + bg[:, d:2 * d]) * b + _sigmoid(g2 + bg[:, 2 * d:]) * c)

    gate_cols = [(zm, d, 8), (zm, d, 9), (zm, d, 10)]
    (mixed,) = _rowwise(merge_fn, [p_sb, p_ml, p_x] + gate_cols, [b_gate], [(d, BF16)], name="merge")
    x1 = _mm(mixed, w_o, tiles=[x2], name="proj_out")
    (h2,) = _rowwise(_rms_fwd, [x1], [g_mlp], [(d, BF16)], name="norm_mlp", tr=512)
    u, act = _mm(h2, w_f1, post=lambda r: (r, jnp.square(jnp.maximum(r, 0.0))), out_dtype=(F32, BF16), name="ff1")
    dy = _mm(act, w_f2, tiles=[x1, tgt], post=lambda r, xv, tv: (r + xv - tv) * (1.0 / d), name="ff2")
    (loss_cols,) = _rowwise(lambda g: (jnp.sum(g * g, axis=0, keepdims=True) * (0.5 * d),), [dy], [], [], [d],
                            name="loss", tr=1024)

    du = _mm(dy, w_f2, tb=True, tiles=[u], post=lambda r, uv: r * 2.0 * jnp.maximum(uv, 0.0), out_dtype=BF16,
             name="ff2_dx")
    dw_f2 = _mm(act, dy, ta=True, name="ff2_dw")
    dw_f1 = _mm(h2, du, ta=True, name="ff1_dw")
    dh2 = _mm(du, w_f1, tb=True, name="ff1_dx")

    def norm_bwd_fn(xv, dyv, res, g):
        dx, dg = _rms_bwd(xv, g, dyv)
        return dx + res, jnp.sum(dg, axis=0, keepdims=True)

    dx1, dg_mlp = _rowwise(norm_bwd_fn, [x1, dh2, dy], [g_mlp], [(d, F32)], [d], name="norm_mlp_bwd", tr=512)
    dmixed = _mm(dx1, w_o, tb=True, name="proj_out_dx")
    dw_o = _mm(mixed, dx1, ta=True, name="proj_out_dw")

    def merge_bwd_fn(dm, a, b, c, g0, g1, g2, bg):
        outs, dgs = [], []
        for p, g, k in ((a, g0, 0), (b, g1, 1), (c, g2, 2)):
            sg = _sigmoid(g + bg[:, k * d:(k + 1) * d])
            outs.append(dm * sg)
            dgs.append(dm * p * sg * (1.0 - sg))
        dgate = jnp.concatenate(dgs, axis=1)
        return (*outs, dgate, jnp.sum(dgate, axis=0, keepdims=True))

    dp_sb, dp_ml, dp_x, dgate, db_gate = _rowwise(
        merge_bwd_fn, [dmixed, p_sb, p_ml, p_x] + gate_cols, [b_gate], [(d, BF16)] * 3 + [(3 * d, BF16)], [3 * d],
        name="merge_bwd", tr=256)
    dw_sbp = _mm(y_sb, dp_sb, ta=True, name="proj_sb_dw")
    dw_mlp = _mm(y_ml, dp_ml, ta=True, name="proj_ml_dw")
    dw_xp = _mm(y_x, dp_x, ta=True, name="proj_x_dw")
    dy_sb = _mm(dp_sb, w_sbp, tb=True, out_dtype=BF16, name="proj_sb_dx")
    dy_ml = _mm(dp_ml, w_mlp, tb=True, name="proj_ml_dx")
    dy_x = _mm(dp_x, w_xp, tb=True, out_dtype=BF16, name="proj_x_dx")

    dxq, dkn, dxv, dg_qn = _xa_bwd(zm, 7 * d, kv, q_norm_g, k_norm_g, dy_x, d, name="xa_bwd")

    def knorm_bwd_fn(kvv, dknv, dvv, g):
        dks, dgs = [], []
        for k in range(X_HEADS):
            sl = slice(k * dh, (k + 1) * dh)
            dk, dg = _rms_bwd(kvv[:, sl], g, dknv[:, sl])
            dks.append(dk)
            dgs.append(jnp.sum(dg, axis=0, keepdims=True))
        return jnp.concatenate(dks + [dvv], axis=1), dgs[0] + dgs[1] + dgs[2] + dgs[3]

    dkv, dg_kn = _rowwise(knorm_bwd_fn, [(kv, d, 0), dkn, dxv], [k_norm_g], [(2 * d, BF16)], [dh], name="xa_knorm_bwd")
    dw_kv = _mm(memn, dkv, ta=True, name="proj_kv_dw")
    dmemn = _mm(dkv, w_kv, tb=True, name="proj_kv_dx")

    def gmem_fn(mv, dv_, g):
        _, dg = _rms_bwd(mv, g, dv_)
        return (jnp.sum(dg, axis=0, keepdims=True),)

    (dg_mem,) = _rowwise(gmem_fn, [mem2, dmemn], [g_mem], [], [d], name="norm_mem_bwd")

    uncols = lambda a: a.reshape(a.shape[0], 4, a.shape[1] // 4).transpose(1, 0, 2)
    unrws = lambda a: a.reshape(4, a.shape[0] // 4, a.shape[1])
    to_parts = lambda q: q.astype(BF16).reshape(4, 2, q.shape[1] // 2, q.shape[2])
    early = [to_parts(q) for q in (uncols(dw_kv), unrws(dw_sbp), unrws(dw_mlp), unrws(dw_xp), unrws(dw_o),
                                   uncols(dw_f1), unrws(dw_f2))]
    ge_send, ge_recv, ge_src, ge_land, ge_token = _split_start(
        "grads", early, [(8,) + a.shape[2:] for a in early], dg_mem, name="exchange_early_start")

    dsq, dsk, dsv = _sb_bwd(zm, dy_sb, a_sb, ge_token, sbh, name="sb_bwd")

    def mlout_bwd_fn(dyv, hv, o, g):
        sg = _sigmoid(o)
        dn = dyv * sg
        dxs, dgs, ys = [], [], []
        for k in range(hh):
            sl = slice(k * dh, (k + 1) * dh)
            ys.append(_rms_fwd(hv[:, sl], g[:, sl]))
            dxk, dgk = _rms_bwd(hv[:, sl], g[:, sl], dn[:, sl])
            dxs.append(dxk)
            dgs.append(dgk)
        do = dyv * jnp.concatenate(ys, axis=1) * sg * (1.0 - sg)
        return jnp.concatenate(dxs, axis=1), do, jnp.sum(jnp.concatenate(dgs, axis=1), axis=0, keepdims=True)

    dhm, dmlo, dg_mln = _rowwise(mlout_bwd_fn, [dy_ml, hm, (zm, d, 6)], [ml_norm_g], [(d, F32), (d, BF16)], [d],
                                 name="ml_out_bwd", tr=512)
    dmqk, dmlv, dgc, dgr = _ml_bwd(mqk, zm, 5 * d, gcol, grow, cst, nst, mst, dhm, d, name="ml_bwd")
    dmlqk, dconv_w, dconv_b = _conv_bwd(zm, 3 * d, 2 * d, conv_wf, conv_b, dmqk, name="conv_bwd")
    dgr_t = jnp.pad(dgr.transpose(1, 0, 2).reshape(8, s).T, ((0, 0), (0, LANES - 8)))

    def gate_bwd_fn(a, b, z, bias):
        tot = a + b
        rows_t = tot.shape[0]
        r = lax.broadcasted_iota(jnp.int32, (rows_t, rows_t), 0)
        c = lax.broadcasted_iota(jnp.int32, (rows_t, rows_t), 1)
        sh = CHUNK.bit_length() - 1
        same_chunk = jnp.right_shift(r, sh) == jnp.right_shift(c, sh)
        dlf = _u01dot(((c >= r) & same_chunk).astype(BF16), tot)
        lane = lax.broadcasted_iota(jnp.int32, tot.shape, 1)
        dz = jnp.where(lane < hh, tot, jnp.where(lane < 2 * hh, dlf * _sigmoid(-(z + bias)), 0.0))
        return dz, jnp.sum(dz, axis=0, keepdims=True)

    dzif, db_if_p = _rowwise(gate_bwd_fn, [dgc, dgr_t, zif], [b_if_p], [(LANES, BF16)], [LANES], name="ml_gates_bwd",
                             tr=8 * CHUNK)
    dzm = jnp.concatenate([dsq, dsk, dsv, dmlqk, dmlv, dmlo, dxq, dgate], axis=1)
    dw_main = _mm(hn, dzm, ta=True, out_dtype=BF16, name="proj_in_dw")
    dw_if = _mm(hn, dzif, ta=True, out_dtype=BF16, name="proj_if_dw")

    def dw_quarter(k):
        lo, hi = k * qn, (k + 1) * qn
        segs = [(dw_main, 0, if_lo, 0), (dw_if, if_lo, if_hi, if_lo), (dw_main, if_hi, n_in, 2 * hh)]
        got = [src[:, max(lo, a) - off:min(hi, b) - off] for src, a, b, off in segs if max(lo, a) < min(hi, b)]
        return jnp.concatenate(got, axis=1)

    late = [to_parts(jnp.stack([dw_quarter(k) for k in range(4)]))]
    gl_send, gl_recv, gl_src, gl_land, gl_token = _split_start(
        "grads", late, [(8,) + a.shape[2:] for a in late], dw_if, name="exchange_late_start")
    dhn = _mm(dzm, w_main, tb=True, after=gl_token, name="proj_in_dx")
    dhn = _mm(dzif, w_if, tb=True, tiles=[dhn], name="proj_if_dx")
    dx, dg_mix = _rowwise(norm_bwd_fn, [x2, dhn, dx1], [g_mix], [(d, F32)], [d], name="norm_in_bwd", tr=512)

    own = lambda p: lax.dynamic_index_in_dim(lax.dynamic_index_in_dim(p, k4, 0, keepdims=False),
                                             lax.axis_index("c"), 0, keepdims=False)

    def finish(tag, send, recv, src, land, parts, after, ws, ms, vs):
        land = _split_wait("grads", send, recv, src, land, after, name=f"exchange_{tag}_wait")
        got = [lax.dynamic_update_index_in_dim(ld, own(p), me, 0) for ld, p in zip(land, parts)]
        halves = [_sum8(r, name=f"sum_grads_{tag}{i}") for i, r in enumerate(got)]
        both = _swap_halves(halves, name=f"swap_halves_{tag}")
        gs = [b.reshape(2 * b.shape[1], b.shape[2]) for b in both]
        return gs, [_adamw(w, g, m, v, name=f"adamw_{tag}{i}") for i, (w, g, m, v) in enumerate(zip(ws, gs, ms, vs))]

    first = lambda arrs: [a[0] for a in arrs]
    g_early, out_early = finish(
        "early", ge_send, ge_recv, ge_src, ge_land, early, [dx],
        first([w_mem_kv, w_sb_proj, w_ml_proj, w_x_proj, w_out, w_ff1, w_ff2]),
        first([m_w_mem_kv, m_w_sb_proj, m_w_ml_proj, m_w_x_proj, m_w_out, m_w_ff1, m_w_ff2]),
        first([v_w_mem_kv, v_w_sb_proj, v_w_ml_proj, v_w_x_proj, v_w_out, v_w_ff1, v_w_ff2]))
    g_late, out_late = finish(
        "late", gl_send, gl_recv, gl_src, gl_land, late, [o[0] for o in out_early],
        first([w_in]), first([m_w_in]), first([v_w_in]))
    g_big = [g[None] for g in g_late + g_early]
    big_out = [[o[None] for o in outs] for outs in out_late + out_early]

    small_g = [dg_mix, db_if_p[:, :2 * hh], db_gate, dconv_w, dconv_b, dg_mln, dg_mem, dg_qn, dg_kn, dg_mlp,
               jnp.sum(loss_cols).reshape(1, 1)]
    n_small = sum(a.size for a in small_g)
    rows = -(-n_small // (8 * LANES)) * 8
    g_small = _unpack(_allreduce_small(_pack(small_g, rows), out_late[0][0], name="allreduce_small"), small_g)
    loss = g_small[-1].reshape(())
    qw = conv_w.shape[2]
    g_conv_w = lax.dynamic_slice_in_dim(g_small[3], k4 * qw, qw, axis=1)
    g_small_w = [g_small[0], g_small[1], g_small[2], g_conv_w] + g_small[4:10]
    sm_w = [g_mix, b_if, b_gate, conv_w[0], conv_b, ml_norm_g, g_mem, q_norm_g, k_norm_g, g_mlp]
    sm_m = [m_g_mix, m_b_if, m_b_gate, m_conv_w[0], m_conv_b, m_ml_norm_g, m_g_mem, m_q_norm_g, m_k_norm_g, m_g_mlp]
    sm_v = [v_g_mix, v_b_if, v_b_gate, v_conv_w[0], v_conv_b, v_ml_norm_g, v_g_mem, v_q_norm_g, v_k_norm_g, v_g_mlp]
    n_sw = sum(a.size for a in sm_w)
    rows_w = -(-n_sw // (8 * LANES)) * 8
    sm_out = _adamw(_pack(sm_w, rows_w), _pack(g_small_w, rows_w), _pack(sm_m, rows_w), _pack(sm_v, rows_w),
                    name="adamw_small")
    sm_delta, sm_newm, sm_newv = [_unpack(p, sm_w) for p in sm_out]

    order = ["g_mix", "w_in", "b_if", "b_gate", "conv_w", "conv_b", "ml_norm_g", "g_mem", "w_mem_kv", "q_norm_g",
             "k_norm_g", "w_sb_proj", "w_ml_proj", "w_x_proj", "w_out", "g_mlp", "w_ff1", "w_ff2"]
    small_names = ["g_mix", "b_if", "b_gate", "conv_w", "conv_b", "ml_norm_g", "g_mem", "q_norm_g", "k_norm_g", "g_mlp"]
    big_names = ["w_in", "w_mem_kv", "w_sb_proj", "w_ml_proj", "w_x_proj", "w_out", "w_ff1", "w_ff2"]
    grads, deltas, new_m, new_v = {}, {}, {}, {}
    for i, nme in enumerate(small_names):
        shp = sm_w[i].shape if nme != "conv_w" else conv_w.shape
        grads[nme] = g_small_w[i].reshape(shp)
        deltas[nme], new_m[nme], new_v[nme] = (sm_delta[i].reshape(shp), sm_newm[i].reshape(shp),
                                               sm_newv[i].reshape(shp))
    for i, nme in enumerate(big_names):
        grads[nme] = g_big[i]
        deltas[nme], new_m[nme], new_v[nme] = big_out[i]
    return (loss, dx[None], *[grads[k] for k in order], *[deltas[k] for k in order], *[new_m[k] for k in order],
            *[new_v[k] for k in order])
```

```python
import functools

import jax
import jax.numpy as jnp
from jax import lax
from jax.experimental import pallas as pl
from jax.experimental.pallas import tpu as pltpu

F32 = jnp.float32
BF16 = jnp.bfloat16
MESH = pl.DeviceIdType.MESH

EPS = 1e-6
SB_HD = 128
SB_SLOTS = 8
ML_HEADS = 4
X_HEADS = 4
CHUNK = 64
CONV_W = 4
LANES = 128
ADAM_LR = 0.001
ADAM_B1 = 0.9
ADAM_B2 = 0.999
ADAM_EPS = 1e-08
ADAM_WD = 0.01
ADAM_STEP = 10
VMEM_CAP = 56 * 1024 * 1024
NEG = -1e30

NT = (((1,), (1,)), ((), ()))
NN = (((1,), (0,)), ((), ()))
TN = (((0,), (0,)), ((), ()))


def _dot(a, b, dn=NN):
    return lax.dot_general(a.astype(BF16), b.astype(BF16), dn, preferred_element_type=F32)


def _dot01(x, u, dn=NN):
    hi = x.astype(BF16)
    lo = (x - hi.astype(F32)).astype(BF16)
    return (lax.dot_general(hi, u, dn, preferred_element_type=F32)
            + lax.dot_general(lo, u, dn, preferred_element_type=F32))


def _u01dot(u, x):
    hi = x.astype(BF16)
    lo = (x - hi.astype(F32)).astype(BF16)
    return (lax.dot_general(u, hi, NN, preferred_element_type=F32)
            + lax.dot_general(u, lo, NN, preferred_element_type=F32))


def _pick(n, cands):
    for c in cands:
        if c <= n and n % c == 0:
            return c
    return n


def _nbytes(shape, dtype):
    n = 1
    for s in shape:
        n *= s
    return n * jnp.dtype(dtype).itemsize


def _params(vmem_bytes):
    return pltpu.CompilerParams(vmem_limit_bytes=int(min(VMEM_CAP, max(vmem_bytes, 16 * 1024 * 1024))))


def _hbm(a):
    return pltpu.with_memory_space_constraint(a, pltpu.HBM)


def _softplus(z):
    return jnp.maximum(z, 0.0) + jnp.log(1.0 + jnp.exp(-jnp.abs(z)))


def _sigmoid(z):
    return 1.0 / (1.0 + jnp.exp(-z))


def _rms_fwd(xv, g):
    r = lax.rsqrt(jnp.mean(xv * xv, axis=-1, keepdims=True) + EPS)
    return xv * r * g


def _rms_bwd(xv, g, dy):
    r = lax.rsqrt(jnp.mean(xv * xv, axis=-1, keepdims=True) + EPS)
    xh = xv * r
    dxh = dy * g
    dx = r * (dxh - xh * jnp.mean(dxh * xh, axis=-1, keepdims=True))
    return dx, dy * xh


def _mm(a, b, *, name, ta=False, tb=False, tiles=(), post=None, out_dtype=F32, bm=1024, bn=1024, bk=1024, after=None):
    m, k = (a.shape[1], a.shape[0]) if ta else a.shape
    n = b.shape[0] if tb else b.shape[1]
    tm = _pick(m, (bm, 512, 256, 128))
    tn = _pick(n, (bn, 512, 256, 128))
    tk = _pick(k, (bk, 512, 256, 128))
    nk = k // tk
    if (m // tm) * (n // tn) * nk < 8 and tm % 256 == 0:
        tm //= 2
    dn = (((0 if ta else 1,), (1 if tb else 0,)), ((), ()))
    dts = out_dtype if isinstance(out_dtype, tuple) else (out_dtype,)
    nt, no = len(tiles), len(dts)
    if post is None:
        post = lambda r, *ts: sum((t.astype(F32) for t in ts), r)

    def body(*refs):
        a_ref, b_ref = refs[:2]
        t_refs = refs[2:2 + nt]
        o_refs = refs[2 + nt + (after is not None):2 + nt + (after is not None) + no]
        part = lax.dot_general(a_ref[...].astype(BF16), b_ref[...].astype(BF16), dn, preferred_element_type=F32)

        def finish(r):
            res = post(r, *[t[...] for t in t_refs])
            res = res if isinstance(res, tuple) else (res,)
            for o, v in zip(o_refs, res):
                o[...] = v.astype(o.dtype)

        if nk == 1:
            finish(part)
        else:
            acc_ref = refs[-1]
            kk = pl.program_id(2)

            @pl.when(kk == 0)
            def _():
                acc_ref[...] = part

            @pl.when(kk > 0)
            def _():
                acc_ref[...] += part

            @pl.when(kk == nk - 1)
            def _():
                finish(acc_ref[...])

    a_spec = pl.BlockSpec((tk, tm), lambda i, j, q: (q, i)) if ta else pl.BlockSpec((tm, tk), lambda i, j, q: (i, q))
    b_spec = pl.BlockSpec((tn, tk), lambda i, j, q: (j, q)) if tb else pl.BlockSpec((tk, tn), lambda i, j, q: (q, j))
    o_spec = pl.BlockSpec((tm, tn), lambda i, j, q: (i, j))
    ins, specs = [_hbm(a), _hbm(b)] + [_hbm(t) for t in tiles], [a_spec, b_spec] + [o_spec] * nt
    vm = 2 * (_nbytes((tm, tk), a.dtype) + _nbytes((tk, tn), b.dtype)) + 3 * _nbytes((tm, tn), F32) \
        + _nbytes((tm, tk), BF16) + _nbytes((tk, tn), BF16) \
        + 2 * sum(_nbytes((tm, tn), t.dtype) for t in tiles) + 2 * sum(_nbytes((tm, tn), dt) for dt in dts)
    if after is not None:
        ins.append(after)
        specs.append(ANY)
    res = pl.pallas_call(
        body, name=name, grid=(m // tm, n // tn, nk), in_specs=specs, out_specs=[o_spec] * no,
        out_shape=[pltpu.HBM((m, n), dt) for dt in dts], scratch_shapes=[pltpu.VMEM((tm, tn), F32)] if nk > 1 else [],
        compiler_params=_params(vm + (4 << 20)),
    )(*ins)
    return res[0] if no == 1 else tuple(res)


def _rowwise(fn, rows, consts, outs, reds=(), *, name, tr=256, temps=6):
    rows = [r if isinstance(r, tuple) else (r, r.shape[1], 0) for r in rows]
    nrows = rows[0][0].shape[0]
    t = _pick(nrows, (tr, 128, 64, 32, 16, 8))
    nr, nc, no = len(rows), len(consts), len(outs)

    def body(*refs):
        rin, cin = refs[:nr], refs[nr:nr + nc]
        oref, rref = refs[nr + nc:nr + nc + no], refs[nr + nc + no:]
        res = fn(*[r[...] for r in rin], *[c[...] for c in cin])
        if not isinstance(res, (tuple, list)):
            res = (res,)
        for o, v in zip(oref, res[:no]):
            o[...] = v.astype(o.dtype)
        if rref:
            @pl.when(pl.program_id(0) == 0)
            def _():
                for r in rref:
                    r[...] = jnp.zeros_like(r)

            for r, v in zip(rref, res[no:]):
                r[...] += v

    in_specs = [pl.BlockSpec((t, w), functools.partial(lambda i, ci: (i, ci), ci=ci)) for (_, w, ci) in rows]
    in_specs += [pl.BlockSpec(c.shape, functools.partial(lambda i, nd: (0,) * nd, nd=c.ndim)) for c in consts]
    out_specs = [pl.BlockSpec((t, w), lambda i: (i, 0)) for (w, _) in outs]
    out_specs += [pl.BlockSpec((1, w), lambda i: (0, 0)) for w in reds]
    out_shape = [pltpu.HBM((nrows, w), dt) for (w, dt) in outs]
    out_shape += [jax.ShapeDtypeStruct((1, w), F32) for w in reds]
    widest = max([w for (_, w, _) in rows] + [w for (w, _) in outs])
    vm = 2 * sum(_nbytes((t, w), a.dtype) for (a, w, _) in rows) + 2 * sum(_nbytes((t, w), dt) for (w, dt) in outs)
    vm += temps * _nbytes((t, widest), F32) + (2 << 20)
    res = pl.pallas_call(
        body, name=name, grid=(nrows // t,), in_specs=in_specs, out_specs=out_specs, out_shape=out_shape,
        compiler_params=_params(vm),
    )(*[_hbm(a) for (a, _, _) in rows], *consts)
    return list(res)


def _sb_tiles(s, tq, tk):
    tq = _pick(s, (tq, 256, 128))
    tk = _pick(tq, (tk, 128))
    return tq, tk, tq // tk


def _sb_fwd(zm, heads, *, name, tq=512, tk=256):
    s = zm.shape[0]
    tq, tk, nd = _sb_tiles(s, tq, tk)
    scale = SB_HD ** -0.5

    def body(q_ref, k_ref, v_ref, o_ref, a_out, stage, sem):
        h, i = pl.program_id(0), pl.program_id(1)
        qb = (q_ref[...] * scale).astype(BF16)
        r = lax.broadcasted_iota(jnp.int32, (tq, tk), 0)
        c = lax.broadcasted_iota(jnp.int32, (tq, tk), 1)
        ur = lax.broadcasted_iota(jnp.int32, (tk, tk), 0)
        uc = lax.broadcasted_iota(jnp.int32, (tk, tk), 1)
        usuf = (ur > uc).astype(BF16)

        def out_copy(slot, j):
            return pltpu.make_async_copy(stage.at[slot], a_out.at[h, i, j], sem.at[slot])

        def tile(j, carry, causal, slot, reuse):
            acc, cl = carry
            if reuse is True:
                out_copy(slot, 0).wait()
            elif reuse is not None:
                @pl.when(reuse)
                def _():
                    out_copy(slot, 0).wait()
            rows = pl.ds(pl.multiple_of(j * tk, tk), tk)
            kb = k_ref[rows, :].astype(BF16)
            vb = v_ref[rows, :].astype(BF16)
            z = lax.dot_general(qb, kb, NT, preferred_element_type=F32)
            lsig = -_softplus(z)
            l = lsig if causal is None else jnp.where(causal, lsig, 0.0)
            loga = z + lsig + _dot01(l, usuf) + cl
            if causal is not None:
                loga = jnp.where(causal, loga, NEG)
            ab = jnp.exp(loga).astype(BF16)
            acc = acc + lax.dot_general(ab, vb, NN, preferred_element_type=F32)
            stage[slot] = ab
            out_copy(slot, j).start()
            return acc, cl + jnp.sum(l, axis=1, keepdims=True)

        carry = (jnp.zeros((tq, SB_HD), F32), jnp.zeros((tq, 1), F32))
        for n, dd in enumerate(range(nd - 1, -1, -1)):
            carry = tile(i * nd + dd, carry, c + dd * tk < r, n, None)

        def rest(n, cr):
            return tile(i * nd - 1 - n, cr, None, (nd + n) % SB_SLOTS, nd + n >= SB_SLOTS)

        acc, _ = lax.fori_loop(0, i * nd, rest, carry)
        total = (i + 1) * nd
        for back in range(1, SB_SLOTS + 1):
            @pl.when(total >= back)
            def _():
                out_copy((total - back) % SB_SLOTS, 0).wait()

        o_ref[...] = acc.astype(o_ref.dtype)

    assert nd <= SB_SLOTS
    blk = lambda off: pl.BlockSpec((s, SB_HD), functools.partial(lambda h, i, off: (0, off + h), off=off))
    return pl.pallas_call(
        body, name=name, grid=(heads, s // tq),
        in_specs=[pl.BlockSpec((tq, SB_HD), lambda h, i: (i, h)), blk(heads), blk(2 * heads)],
        out_specs=[pl.BlockSpec((tq, SB_HD), lambda h, i: (i, h)), ANY],
        out_shape=[pltpu.HBM((s, heads * SB_HD), BF16), pltpu.HBM((heads, s // tq, s // tk, tq, tk), BF16)],
        scratch_shapes=[pltpu.VMEM((SB_SLOTS, tq, tk), BF16), pltpu.SemaphoreType.DMA((SB_SLOTS,))],
        compiler_params=_params(8 * s * SB_HD * 4 + 24 * tq * tk * 4 + (8 << 20)),
    )(_hbm(zm), _hbm(zm), _hbm(zm))


def _sb_bwd(zm, dy, a_all, after, heads, *, name, tq=512, tk=256):
    s = zm.shape[0]
    tq, tk, nd = _sb_tiles(s, tq, tk)
    nq = s // tq
    scale = SB_HD ** -0.5

    def body(q_ref, k_ref, v_ref, do_ref, a_in, after_ref, dq_ref, dk_ref, dv_ref, dka, dva, abuf, sem):
        h, i = pl.program_id(0), pl.program_id(1)

        @pl.when(i == 0)
        def _():
            dka[...] = jnp.zeros_like(dka)
            dva[...] = jnp.zeros_like(dva)

        qb = (q_ref[...] * scale).astype(BF16)
        dob = do_ref[...].astype(BF16)
        qb_t = (q_ref[...] * scale).T.astype(BF16)
        dob_t = do_ref[...].astype(F32).T.astype(BF16)
        r = lax.broadcasted_iota(jnp.int32, (tq, tk), 0)
        c = lax.broadcasted_iota(jnp.int32, (tq, tk), 1)
        ur = lax.broadcasted_iota(jnp.int32, (tk, tk), 0)
        uc = lax.broadcasted_iota(jnp.int32, (tk, tk), 1)
        uexcl = (ur < uc).astype(BF16)

        def fetch(j, slot):
            return pltpu.make_async_copy(a_in.at[h, i, j], abuf.at[slot], sem.at[slot])

        total = (i + 1) * nd
        ahead = SB_SLOTS - 1

        def tile(j, carry, causal):
            dq, cg = carry
            slot = j % SB_SLOTS
            fetch(j, slot).wait()

            @pl.when(j + ahead < total)
            def _():
                fetch(j + ahead, (j + ahead) % SB_SLOTS).start()

            rows = pl.ds(pl.multiple_of(j * tk, tk), tk)
            kb = k_ref[rows, :].astype(BF16)
            vb = v_ref[rows, :].astype(BF16)
            z = lax.dot_general(qb, kb, NT, preferred_element_type=F32)
            sig = 1.0 / (1.0 + jnp.exp(-z))
            ab = abuf[slot]
            g = ab.astype(F32) * lax.dot_general(dob, vb, NT, preferred_element_type=F32)
            p = cg + lax.dot_general(g.astype(BF16), uexcl, NN, preferred_element_type=F32)
            dz = g - sig * (g + p)
            if causal is not None:
                dz = jnp.where(causal, dz, 0.0)
            dzb = dz.astype(BF16)
            dva[j] += lax.dot_general(dob_t, ab, NN, preferred_element_type=F32)
            dka[j] += lax.dot_general(qb_t, dzb, NN, preferred_element_type=F32)
            dq = dq + lax.dot_general(dzb, kb, NN, preferred_element_type=F32)
            return dq, cg + jnp.sum(g, axis=1, keepdims=True)

        for first in range(ahead):
            @pl.when(first < total)
            def _():
                fetch(first, first).start()

        init = (jnp.zeros((tq, SB_HD), F32), jnp.zeros((tq, 1), F32))
        carry = lax.fori_loop(0, i * nd, lambda j, cr: tile(j, cr, None), init)
        for dd in range(nd):
            carry = tile(i * nd + dd, carry, c + dd * tk < r)
        dq_ref[...] = (carry[0] * scale).astype(dq_ref.dtype)

        @pl.when(i == nq - 1)
        def _():
            for jj in range(s // tk):
                dk_ref[jj * tk:(jj + 1) * tk, :] = dka[jj].T.astype(dk_ref.dtype)
                dv_ref[jj * tk:(jj + 1) * tk, :] = dva[jj].T.astype(dv_ref.dtype)

    blk = lambda off: pl.BlockSpec((s, SB_HD), functools.partial(lambda h, i, off: (0, off + h), off=off))
    tile_spec = pl.BlockSpec((tq, SB_HD), lambda h, i: (i, h))
    full = pltpu.HBM((s, heads * SB_HD), BF16)
    return pl.pallas_call(
        body, name=name, grid=(heads, nq),
        in_specs=[tile_spec, blk(heads), blk(2 * heads), tile_spec, ANY, ANY],
        out_specs=[tile_spec, blk(0), blk(0)],
        out_shape=[full, full, full],
        scratch_shapes=[pltpu.VMEM((s // tk, SB_HD, tk), F32), pltpu.VMEM((s // tk, SB_HD, tk), F32),
                        pltpu.VMEM((SB_SLOTS, tq, tk), BF16), pltpu.SemaphoreType.DMA((SB_SLOTS,))],
        compiler_params=_params(12 * s * SB_HD * 4 + 32 * tq * tk * 4 + (8 << 20)),
    )(_hbm(zm), _hbm(zm), _hbm(zm), _hbm(dy), a_all, after)


def _conv_taps(u, w_ref, rows_i):
    taps = []
    for j in range(CONV_W):
        sh = CONV_W - 1 - j
        if sh == 0:
            taps.append(u)
        else:
            taps.append(jnp.where(rows_i >= sh, pltpu.roll(u, sh, 0), 0.0))
    return taps


def _conv_fwd(zm, col0, width, cw, cb, *, name):
    s = zm.shape[0]
    bw = _pick(width, (LANES,))
    off = col0 // bw

    def body(u_ref, w_ref, b_ref, o_ref):
        u = u_ref[...]
        rows_i = lax.broadcasted_iota(jnp.int32, u.shape, 0)
        acc = jnp.broadcast_to(b_ref[...], u.shape)
        for j, tp in enumerate(_conv_taps(u, w_ref, rows_i)):
            acc = acc + tp * w_ref[j:j + 1, :]
        o_ref[...] = acc * _sigmoid(acc)

    return pl.pallas_call(
        body, name=name, grid=(width // bw,),
        in_specs=[pl.BlockSpec((s, bw), lambda j: (0, off + j)), pl.BlockSpec((CONV_W, bw), lambda j: (0, j)),
                  pl.BlockSpec((1, bw), lambda j: (0, j))],
        out_specs=pl.BlockSpec((s, bw), lambda j: (0, j)),
        out_shape=pltpu.HBM((s, width), F32),
        compiler_params=_params(12 * s * bw * 4 + (4 << 20)),
    )(_hbm(zm), cw, cb)


def _conv_bwd(zm, col0, width, cw, cb, dqk, *, name):
    s = zm.shape[0]
    bw = _pick(width, (LANES,))
    off = col0 // bw

    def body(u_ref, w_ref, b_ref, d_ref, du_ref, dw_ref, db_ref):
        u = u_ref[...]
        rows_i = lax.broadcasted_iota(jnp.int32, u.shape, 0)
        taps = _conv_taps(u, w_ref, rows_i)
        acc = jnp.broadcast_to(b_ref[...], u.shape)
        for j, tp in enumerate(taps):
            acc = acc + tp * w_ref[j:j + 1, :]
        sg = _sigmoid(acc)
        dc = d_ref[...] * (sg * (1.0 + acc * (1.0 - sg)))
        du = jnp.zeros_like(u)
        for j in range(CONV_W):
            sh = CONV_W - 1 - j
            if sh == 0:
                du = du + dc * w_ref[j:j + 1, :]
            else:
                du = du + jnp.where(rows_i < s - sh, pltpu.roll(dc, s - sh, 0), 0.0) * w_ref[j:j + 1, :]
            dw_ref[j:j + 1, :] = jnp.sum(dc * taps[j], axis=0, keepdims=True)
        du_ref[...] = du.astype(du_ref.dtype)
        db_ref[...] = jnp.sum(dc, axis=0, keepdims=True)

    return pl.pallas_call(
        body, name=name, grid=(width // bw,),
        in_specs=[pl.BlockSpec((s, bw), lambda j: (0, off + j)), pl.BlockSpec((CONV_W, bw), lambda j: (0, j)),
                  pl.BlockSpec((1, bw), lambda j: (0, j)), pl.BlockSpec((s, bw), lambda j: (0, j))],
        out_specs=[pl.BlockSpec((s, bw), lambda j: (0, j)), pl.BlockSpec((CONV_W, bw), lambda j: (0, j)),
                   pl.BlockSpec((1, bw), lambda j: (0, j))],
        out_shape=[pltpu.HBM((s, width), BF16), pltpu.HBM((CONV_W, width), F32),
                   pltpu.HBM((1, width), F32)],
        compiler_params=_params(20 * s * bw * 4 + (4 << 20)),
    )(_hbm(zm), cw, cb, _hbm(dqk))


def _ml_gates(gcol_ref, grow_ref):
    l = CHUNK
    r = lax.broadcasted_iota(jnp.int32, (l, l), 0)
    c = lax.broadcasted_iota(jnp.int32, (l, l), 1)
    gcol = gcol_ref[...]
    grow = grow_ref[0]
    bcol = _u01dot((c <= r).astype(BF16), gcol)
    brow = _dot01(grow, (r <= c).astype(BF16))
    return gcol, grow, bcol, brow, r >= c


def _ml_chunk(h, dh, mq_ref, mk_ref, v_ref, gates, cp, n_prev, m_prev):
    gcol, grow, bcol, brow, tri = gates
    l = CHUNK
    sl = slice(h * dh, (h + 1) * dh)
    qc = mq_ref[:, sl]
    kc = mk_ref[:, sl] * (dh ** -0.5)
    vc = v_ref[:, sl]
    i_row = grow[h:h + 1, :]
    i_col = gcol[:, h:h + 1]
    b_col = bcol[:, ML_HEADS + h:ML_HEADS + h + 1]
    b_row = brow[ML_HEADS + h:ML_HEADS + h + 1, :]
    b_end = b_col[l - 1:l, :]
    d = jnp.where(tri, b_col - b_row + i_row, -jnp.inf)
    m_inter = b_col + m_prev
    m_t = jnp.maximum(m_inter, jnp.max(d, axis=1, keepdims=True))
    w = jnp.exp(d - m_t)
    s_inter = jnp.exp(m_inter - m_t)
    qb, kb, vb = qc.astype(BF16), kc.astype(BF16), vc.astype(BF16)
    cpb = cp.astype(BF16)
    a = lax.dot_general(qb, kb, NT, preferred_element_type=F32)
    sc = a * w
    qcp = lax.dot_general(qb, cpb, NT, preferred_element_type=F32)
    qn = jnp.sum(qc * n_prev, axis=1, keepdims=True)
    num = lax.dot_general(sc.astype(BF16), vb, NN, preferred_element_type=F32) + s_inter * qcp
    den = jnp.sum(sc, axis=1, keepdims=True) + s_inter * qn
    floor = jnp.exp(-m_t)
    dnm = jnp.maximum(jnp.abs(den), floor)
    g_col = b_end - b_col + i_col
    g_row = b_end - b_row + i_row
    m_new = jnp.maximum(b_end + m_prev, jnp.max(g_row, axis=1, keepdims=True))
    decay = jnp.exp(b_end + m_prev - m_new)
    wk = jnp.exp(g_col - m_new)
    return dict(qc=qc, kc=kc, vc=vc, qb=qb, kb=kb, vb=vb, cpb=cpb, w=w, s_inter=s_inter, a=a, sc=sc, qcp=qcp, qn=qn,
                num=num, den=den, floor=floor, dnm=dnm, m_new=m_new, decay=decay, wk=wk, sl=sl)


def _ml_fwd(mqk, zm, vcol, gcol, grow, d_model, *, name):
    s = zm.shape[0]
    nc = s // CHUNK
    dh = d_model // ML_HEADS
    hh = ML_HEADS

    def body(mq_ref, mk_ref, v_ref, gcol_ref, grow_ref, h_ref, cs_ref, ns_ref, ms_ref, c_s, n_s, m_s):
        @pl.when(pl.program_id(0) == 0)
        def _():
            c_s[...] = jnp.zeros_like(c_s)
            n_s[...] = jnp.zeros_like(n_s)
            m_s[...] = jnp.zeros_like(m_s)

        gates = _ml_gates(gcol_ref, grow_ref)
        for h in range(hh):
            cp, n_prev, m_prev = c_s[h], n_s[h], m_s[h][:, 0:1]
            cs_ref[0, h] = cp
            ns_ref[0, h] = n_prev
            ms_ref[0, h] = m_s[h]
            f = _ml_chunk(h, dh, mq_ref, mk_ref, v_ref, gates, cp, n_prev, m_prev)
            h_ref[:, f["sl"]] = f["num"] / f["dnm"]
            c_s[h] = f["decay"] * cp + lax.dot_general((f["vc"] * f["wk"]).astype(BF16), f["kb"], TN,
                                                       preferred_element_type=F32)
            n_s[h] = f["decay"] * n_prev + jnp.sum(f["wk"] * f["kc"], axis=0, keepdims=True)
            m_s[h] = jnp.broadcast_to(f["m_new"], (1, LANES))

    dblk = d_model
    return pl.pallas_call(
        body, name=name, grid=(nc,),
        in_specs=[pl.BlockSpec((CHUNK, dblk), lambda c: (c, 0)), pl.BlockSpec((CHUNK, dblk), lambda c: (c, 1)),
                  pl.BlockSpec((CHUNK, dblk), lambda c: (c, vcol // dblk)),
                  pl.BlockSpec((CHUNK, LANES), lambda c: (c, 0)), pl.BlockSpec((1, 8, CHUNK), lambda c: (c, 0, 0))],
        out_specs=[pl.BlockSpec((CHUNK, dblk), lambda c: (c, 0)),
                   pl.BlockSpec((1, hh, dh, dh), lambda c: (c, 0, 0, 0)),
                   pl.BlockSpec((1, hh, 1, dh), lambda c: (c, 0, 0, 0)),
                   pl.BlockSpec((1, hh, 1, LANES), lambda c: (c, 0, 0, 0))],
        out_shape=[pltpu.HBM((s, d_model), F32), pltpu.HBM((nc, hh, dh, dh), F32),
                   pltpu.HBM((nc, hh, 1, dh), F32), pltpu.HBM((nc, hh, 1, LANES), F32)],
        scratch_shapes=[pltpu.VMEM((hh, dh, dh), F32), pltpu.VMEM((hh, 1, dh), F32), pltpu.VMEM((hh, 1, LANES), F32)],
        compiler_params=_params(8 * hh * dh * dh * 4 + (16 << 20)),
    )(_hbm(mqk), _hbm(mqk), _hbm(zm), _hbm(gcol), _hbm(grow))


def _ml_bwd(mqk, zm, vcol, gcol, grow, cs, ns, ms, dhm, d_model, *, name):
    s = zm.shape[0]
    nc = s // CHUNK
    dh = d_model // ML_HEADS
    hh = ML_HEADS
    l = CHUNK

    def body(mq_ref, mk_ref, v_ref, gcol_ref, grow_ref, cs_ref, ns_ref, ms_ref, dh_ref,
             dqk_ref, dv_ref, dgc_ref, dgr_ref, dc_s, dn_s):
        @pl.when(pl.program_id(0) == 0)
        def _():
            dc_s[...] = jnp.zeros_like(dc_s)
            dn_s[...] = jnp.zeros_like(dn_s)

        gates = _ml_gates(gcol_ref, grow_ref)
        lane = lax.broadcasted_iota(jnp.int32, (l, LANES), 1)
        rowi = lax.broadcasted_iota(jnp.int32, (8, l), 0)
        lastrow = lax.broadcasted_iota(jnp.int32, (l, 1), 0) == l - 1
        dgc = jnp.zeros((l, LANES), F32)
        dgr = jnp.zeros((8, l), F32)
        for h in range(hh):
            cp, n_prev, m_prev = cs_ref[0, h], ns_ref[0, h], ms_ref[0, h][:, 0:1]
            f = _ml_chunk(h, dh, mq_ref, mk_ref, v_ref, gates, cp, n_prev, m_prev)
            dC, dn = dc_s[h], dn_s[h]
            dhv = dh_ref[:, f["sl"]]
            dnum = dhv / f["dnm"]
            hv = f["num"] / f["dnm"]
            ddnm = -jnp.sum(dhv * hv, axis=1, keepdims=True) / f["dnm"]
            dden = jnp.where(jnp.abs(f["den"]) >= f["floor"], ddnm * jnp.sign(f["den"]), 0.0)
            dnb = dnum.astype(BF16)
            dsc = lax.dot_general(dnb, f["vb"], NT, preferred_element_type=F32) + dden
            dvc = lax.dot_general(f["sc"].astype(BF16), dnb, TN, preferred_element_type=F32)
            ds_inter = jnp.sum(dnum * f["qcp"], axis=1, keepdims=True) + dden * f["qn"]
            sdn = (f["s_inter"] * dnum).astype(BF16)
            sdd = f["s_inter"] * dden
            da = dsc * f["w"]
            dab = da.astype(BF16)
            dqc = (lax.dot_general(dab, f["kb"], NN, preferred_element_type=F32)
                   + lax.dot_general(sdn, f["cpb"], NN, preferred_element_type=F32) + sdd * n_prev)
            dcp = f["decay"] * dC + lax.dot_general(sdn, f["qb"], TN, preferred_element_type=F32)
            dnp = f["decay"] * dn + jnp.sum(sdd * f["qc"], axis=0, keepdims=True)
            vw = (f["vc"] * f["wk"]).astype(BF16)
            dCb = dC.astype(BF16)
            dkc = (lax.dot_general(dab, f["qb"], TN, preferred_element_type=F32)
                   + lax.dot_general(vw, dCb, NN, preferred_element_type=F32) + f["wk"] * dn)
            e = lax.dot_general(f["kb"], dCb, NT, preferred_element_type=F32)
            dvc = dvc + e * f["wk"]
            dwk = jnp.sum(e * f["vc"], axis=1, keepdims=True) + jnp.sum(f["kc"] * dn, axis=1, keepdims=True)
            ddecay = jnp.sum(jnp.sum(dC * cp, axis=1, keepdims=True), axis=0, keepdims=True) \
                + jnp.sum(dn * n_prev, axis=1, keepdims=True)
            dd = dsc * f["sc"]
            dlw = dwk * f["wk"]
            db_end = jnp.sum(dlw, axis=0, keepdims=True) + ddecay * f["decay"]
            di_col = dlw
            db_col = jnp.sum(dd, axis=1, keepdims=True) + ds_inter * f["s_inter"] - dlw \
                + jnp.where(lastrow, db_end, 0.0)
            cs_dd = jnp.sum(dd, axis=0, keepdims=True)
            dgc = dgc + jnp.where(lane == h, di_col, 0.0) + jnp.where(lane == hh + h, db_col, 0.0)
            dgr = dgr + jnp.where(rowi == h, cs_dd, 0.0) - jnp.where(rowi == hh + h, cs_dd, 0.0)
            dqk_ref[:, f["sl"]] = dqc
            dqk_ref[:, d_model + h * dh:d_model + (h + 1) * dh] = dkc * (dh ** -0.5)
            dv_ref[:, f["sl"]] = dvc.astype(dv_ref.dtype)
            dc_s[h] = dcp
            dn_s[h] = dnp
        dgc_ref[...] = dgc
        dgr_ref[0] = dgr

    dblk = d_model
    rev = lambda c: nc - 1 - c
    return pl.pallas_call(
        body, name=name, grid=(nc,),
        in_specs=[pl.BlockSpec((l, dblk), lambda c: (rev(c), 0)), pl.BlockSpec((l, dblk), lambda c: (rev(c), 1)),
                  pl.BlockSpec((l, dblk), lambda c: (rev(c), vcol // dblk)),
                  pl.BlockSpec((l, LANES), lambda c: (rev(c), 0)), pl.BlockSpec((1, 8, l), lambda c: (rev(c), 0, 0)),
                  pl.BlockSpec((1, hh, dh, dh), lambda c: (rev(c), 0, 0, 0)),
                  pl.BlockSpec((1, hh, 1, dh), lambda c: (rev(c), 0, 0, 0)),
                  pl.BlockSpec((1, hh, 1, LANES), lambda c: (rev(c), 0, 0, 0)),
                  pl.BlockSpec((l, dblk), lambda c: (rev(c), 0))],
        out_specs=[pl.BlockSpec((l, 2 * dblk), lambda c: (rev(c), 0)),
                   pl.BlockSpec((l, dblk), lambda c: (rev(c), 0)), pl.BlockSpec((l, LANES), lambda c: (rev(c), 0)),
                   pl.BlockSpec((1, 8, l), lambda c: (rev(c), 0, 0))],
        out_shape=[pltpu.HBM((s, 2 * d_model), F32),
                   pltpu.HBM((s, d_model), BF16), pltpu.HBM((s, LANES), F32),
                   pltpu.HBM((nc, 8, l), F32)],
        scratch_shapes=[pltpu.VMEM((hh, dh, dh), F32), pltpu.VMEM((hh, 1, dh), F32)],
        compiler_params=_params(10 * hh * dh * dh * 4 + (16 << 20)),
    )(*[_hbm(a) for a in (mqk, mqk, zm, gcol, grow, cs, ns, ms, dhm)])


def _xa_fwd(zm, qcol, kv, gq, gk, d_model, *, name, tq=512):
    s = zm.shape[0]
    nm = kv.shape[0]
    dh = d_model // X_HEADS
    tq = _pick(s, (tq, 128, 64))
    scale = dh ** -0.5

    def body(q_ref, k_ref, v_ref, gq_ref, gk_ref, o_ref):
        qn = _rms_fwd(q_ref[...], gq_ref[...])
        kn = _rms_fwd(k_ref[...], gk_ref[...])
        lg = _dot(qn, kn, NT) * scale
        lg = lg - jnp.max(lg, axis=1, keepdims=True)
        p = jnp.exp(lg)
        p = p / jnp.sum(p, axis=1, keepdims=True)
        o_ref[...] = _dot(p, v_ref[...], NN).astype(o_ref.dtype)

    return pl.pallas_call(
        body, name=name, grid=(X_HEADS, s // tq),
        in_specs=[pl.BlockSpec((tq, dh), lambda h, i: (i, qcol // dh + h)), pl.BlockSpec((nm, dh), lambda h, i: (0, h)),
                  pl.BlockSpec((nm, dh), lambda h, i: (0, X_HEADS + h)),
                  pl.BlockSpec((1, dh), lambda h, i: (0, 0)), pl.BlockSpec((1, dh), lambda h, i: (0, 0))],
        out_specs=pl.BlockSpec((tq, dh), lambda h, i: (i, h)),
        out_shape=pltpu.HBM((s, d_model), BF16),
        compiler_params=_params(32 << 20),
    )(_hbm(zm), _hbm(kv), _hbm(kv), gq, gk)


def _xa_bwd(zm, qcol, kv, gq, gk, dy, d_model, *, name, tq=512):
    s = zm.shape[0]
    nm = kv.shape[0]
    dh = d_model // X_HEADS
    tq = _pick(s, (tq, 128, 64))
    nq = s // tq
    scale = dh ** -0.5

    def body(q_ref, k_ref, v_ref, gq_ref, gk_ref, do_ref, dq_ref, dkn_ref, dv_ref, dgq_ref):
        h, i = pl.program_id(0), pl.program_id(1)

        @pl.when(i == 0)
        def _():
            dkn_ref[...] = jnp.zeros_like(dkn_ref)
            dv_ref[...] = jnp.zeros_like(dv_ref)

        @pl.when((i == 0) & (h == 0))
        def _():
            dgq_ref[...] = jnp.zeros_like(dgq_ref)

        q = q_ref[...]
        qn = _rms_fwd(q, gq_ref[...])
        kn = _rms_fwd(k_ref[...], gk_ref[...])
        lg = _dot(qn, kn, NT) * scale
        lg = lg - jnp.max(lg, axis=1, keepdims=True)
        p = jnp.exp(lg)
        p = p / jnp.sum(p, axis=1, keepdims=True)
        do = do_ref[...]
        dv_ref[...] += _dot(p, do, TN)
        dp = _dot(do, v_ref[...], NT)
        dlg = p * (dp - jnp.sum(dp * p, axis=1, keepdims=True)) * scale
        dqn = _dot(dlg, kn, NN)
        dkn_ref[...] += _dot(dlg, qn, TN)
        dq, dgq = _rms_bwd(q, gq_ref[...], dqn)
        dq_ref[...] = dq.astype(dq_ref.dtype)
        dgq_ref[...] += jnp.sum(dgq, axis=0, keepdims=True)

    return pl.pallas_call(
        body, name=name, grid=(X_HEADS, nq),
        in_specs=[pl.BlockSpec((tq, dh), lambda h, i: (i, qcol // dh + h)), pl.BlockSpec((nm, dh), lambda h, i: (0, h)),
                  pl.BlockSpec((nm, dh), lambda h, i: (0, X_HEADS + h)),
                  pl.BlockSpec((1, dh), lambda h, i: (0, 0)), pl.BlockSpec((1, dh), lambda h, i: (0, 0)),
                  pl.BlockSpec((tq, dh), lambda h, i: (i, h))],
        out_specs=[pl.BlockSpec((tq, dh), lambda h, i: (i, h)), pl.BlockSpec((nm, dh), lambda h, i: (0, h)),
                   pl.BlockSpec((nm, dh), lambda h, i: (0, h)), pl.BlockSpec((1, dh), lambda h, i: (0, 0))],
        out_shape=[pltpu.HBM((s, d_model), BF16), pltpu.HBM((nm, d_model), F32),
                   pltpu.HBM((nm, d_model), F32), pltpu.HBM((1, dh), F32)],
        compiler_params=_params(32 << 20),
    )(_hbm(zm), _hbm(kv), _hbm(kv), gq, gk, _hbm(dy))


def _place():
    return lax.axis_index("x"), lax.axis_index("y"), lax.axis_index("c")


ANY = pl.BlockSpec(memory_space=pl.ANY)


def _allgather_two_level(big, small, *, name, chunk_rows=64):
    r, cc = big.shape
    half = r // 2
    nr = _pick(half, (chunk_rows, 32, 16))
    nq = half // nr

    def body(big_ref, small_ref, obig, osmall, land, passed, send, recv, fsend, frecv, out_a, out_b, ssend, srecv, loc):
        x, y, c = _place()
        k = 2 * x + y
        chips = [(1 - x, y), (x, 1 - y), (1 - x, 1 - y)]
        slots = [2 * px + py for px, py in chips]
        local = [pltpu.make_async_copy(big_ref, obig.at[k], loc.at[0]),
                 pltpu.make_async_copy(small_ref, osmall.at[k], loc.at[1])]
        for cp in local:
            cp.start()

        def rows(h, q):
            return pl.ds(pl.multiple_of(h * half + q * nr, nr), nr)

        def chunk(q):
            return pl.ds(q * nr, nr)

        def over_ici(j, q):
            return pltpu.make_async_remote_copy(
                src_ref=big_ref.at[rows(c, q)], dst_ref=land.at[j, chunk(q)], send_sem=send.at[nq * j + q],
                recv_sem=recv.at[nq * j + q], device_id=(chips[j][0], chips[j][1], c), device_id_type=MESH)

        def to_sibling(j, q):
            return pltpu.make_async_remote_copy(
                src_ref=land.at[j, chunk(q)], dst_ref=passed.at[j, chunk(q)], send_sem=fsend.at[nq * j + q],
                recv_sem=frecv.at[nq * j + q], device_id=(x, y, 1 - c), device_id_type=MESH)

        def small_copy(j, slot):
            return pltpu.make_async_remote_copy(
                src_ref=small_ref, dst_ref=osmall.at[slot], send_sem=ssend.at[j], recv_sem=srecv.at[j],
                device_id=(chips[j][0], chips[j][1], c), device_id_type=MESH)

        for q in range(nq):
            for j in range(3):
                over_ici(j, q).start()
        for j in range(3):
            small_copy(j, k).start()
        for q in range(nq):
            for j in range(3):
                over_ici(j, q).wait_recv()
                to_sibling(j, q).start()
                cp = pltpu.make_async_copy(land.at[j, chunk(q)], obig.at[slots[j], rows(c, q)], out_a.at[nq * j + q])
                cp.start()
                local.append(cp)
        for q in range(nq):
            for j in range(3):
                to_sibling(j, q).wait_recv()
                cp = pltpu.make_async_copy(passed.at[j, chunk(q)], obig.at[slots[j], rows(1 - c, q)],
                                           out_b.at[nq * j + q])
                cp.start()
                local.append(cp)
        for j in range(3):
            small_copy(j, slots[j]).wait_recv()
            small_copy(j, k).wait_send()
        for q in range(nq):
            for j in range(3):
                over_ici(j, q).wait_send()
                to_sibling(j, q).wait_send()
        for cp in local:
            cp.wait()

    stage = 2 * _nbytes((3, half, cc), big.dtype)
    return pl.pallas_call(
        body, name=name, in_specs=[ANY] * 2, out_specs=[ANY] * 2,
        out_shape=[pltpu.HBM((4,) + big.shape, big.dtype), pltpu.HBM((4,) + small.shape, small.dtype)],
        scratch_shapes=[pltpu.VMEM((3, half, cc), big.dtype), pltpu.VMEM((3, half, cc), big.dtype)]
        + [pltpu.SemaphoreType.DMA((3 * nq,))] * 6
        + [pltpu.SemaphoreType.DMA((3,)), pltpu.SemaphoreType.DMA((3,)), pltpu.SemaphoreType.DMA((2,))],
        compiler_params=_params(stage + stage // 8 + (4 << 20)),
    )(big, small)


HBM_SPEC = pl.BlockSpec(memory_space=pltpu.HBM)
SEM_SPEC = pl.BlockSpec(memory_space=pltpu.SEMAPHORE)
EFFECT = pltpu.SideEffectType.DATAFLOW_SIDE_EFFECTING


def _split_copies(kind, srcs, lands, send, recv):
    x, y, c = _place()
    if kind == "quarters":
        peers = [(1 - x, y, c), (x, 1 - y, c), (1 - x, 1 - y, c)]
    else:
        peers = [(x ^ ((j >> 2) & 1), y ^ ((j >> 1) & 1), c ^ (j & 1)) for j in range(1, 8)]
    npeer = len(peers)
    out = []
    for t in range(len(srcs)):
        for j, (px, py, pc) in enumerate(peers):
            if kind == "quarters":
                src, mine, theirs = srcs[t], 2 * x + y, 2 * px + py
            else:
                src, mine, theirs = srcs[t].at[2 * px + py, pc], 4 * x + 2 * y + c, 4 * px + 2 * py + pc
            mk = functools.partial(
                pltpu.make_async_remote_copy, src_ref=src, send_sem=send.at[npeer * t + j],
                recv_sem=recv.at[npeer * t + j], device_id=(px, py, pc), device_id_type=MESH)
            out.append((functools.partial(mk, dst_ref=lands[t].at[mine]),
                        functools.partial(mk, dst_ref=lands[t].at[theirs])))
    return out


def _split_start(kind, srcs, land_shapes, after, *, name):
    n = len(srcs)
    ncopies = n * (3 if kind == "quarters" else 7)

    def body(*refs):
        ins, lands = refs[:n], refs[n:2 * n]
        send, recv = refs[2 * n + 1], refs[2 * n + 2]
        token = refs[-1]
        for start, _ in _split_copies(kind, ins, lands, send, recv):
            start().start()
        token[...] = jnp.zeros_like(token)

    lands = [_hbm(lax.empty(shp, a.dtype)) for shp, a in zip(land_shapes, srcs)]
    res = pl.pallas_call(
        body, name=name, in_specs=[HBM_SPEC] * (2 * n) + [ANY],
        out_specs=[SEM_SPEC, SEM_SPEC] + [HBM_SPEC] * (2 * n) + [pl.BlockSpec(memory_space=pltpu.VMEM)],
        out_shape=[pltpu.SemaphoreType.DMA((ncopies,)), pltpu.SemaphoreType.DMA((ncopies,))]
        + [pltpu.HBM(a.shape, a.dtype) for a in srcs] + [pltpu.HBM(shp, a.dtype) for shp, a in zip(land_shapes, srcs)]
        + [jax.ShapeDtypeStruct((8, LANES), F32)],
        input_output_aliases={i: 2 + i for i in range(2 * n)},
        compiler_params=pltpu.CompilerParams(has_side_effects=EFFECT),
    )(*[_hbm(a) for a in srcs], *lands, after)
    return res[0], res[1], list(res[2:2 + n]), list(res[2 + n:2 + 2 * n]), res[-1]


def _split_wait(kind, send, recv, srcs, lands, after, *, name):
    n = len(srcs)

    def body(*refs):
        ins, lnd = refs[:n], refs[n:2 * n]
        snd, rcv = refs[2 * n], refs[2 * n + 1]
        for start, arrive in _split_copies(kind, ins, lnd, snd, rcv):
            start().wait_send()
            arrive().wait_recv()

    res = pl.pallas_call(
        body, name=name, in_specs=[HBM_SPEC] * (2 * n) + [SEM_SPEC, SEM_SPEC] + [ANY] * len(after),
        out_specs=[HBM_SPEC] * (2 * n),
        out_shape=[pltpu.HBM(a.shape, a.dtype) for a in srcs] + [pltpu.HBM(a.shape, a.dtype) for a in lands],
        input_output_aliases={i: i for i in range(2 * n)},
        compiler_params=pltpu.CompilerParams(has_side_effects=EFFECT),
    )(*srcs, *lands, send, recv, *after)
    return list(res[n:])


def _sum8(parts, *, name):
    _, r, c = parts.shape
    t = _pick(r, (128, 64, 32, 16, 8))

    def body(p_ref, o_ref):
        acc = p_ref[0].astype(F32)
        for k in range(1, 8):
            acc = acc + p_ref[k].astype(F32)
        o_ref[...] = acc

    return pl.pallas_call(
        body, name=name, grid=(r // t,), in_specs=[pl.BlockSpec((8, t, c), lambda i: (0, i, 0))],
        out_specs=pl.BlockSpec((t, c), lambda i: (i, 0)), out_shape=pltpu.HBM((r, c), F32),
        compiler_params=_params(2 * 8 * t * c * 2 + 6 * t * c * 4 + (4 << 20)),
    )(_hbm(parts))


def _swap_halves(halves, *, name, chunk_bytes=512 * 1024):
    n = len(halves)
    items = []
    for t, a in enumerate(halves):
        r = a.shape[0]
        k = 1
        while _nbytes(a.shape, a.dtype) // k > chunk_bytes and r % (2 * k) == 0 and (r // (2 * k)) % 8 == 0:
            k *= 2
        items += [(t, q * (r // k), r // k) for q in range(k)]
    m = len(items)

    def body(*refs):
        ins, outs = refs[:n], refs[n:2 * n]
        sbuf, rbuf = refs[2 * n:3 * n], refs[3 * n:4 * n]
        send, recv, loc_own, loc_in, loc_out = refs[4 * n:]
        x, y, c = _place()
        local, stage = [], []
        for t in range(n):
            cp = pltpu.make_async_copy(ins[t], outs[t].at[c], loc_own.at[t])
            cp.start()
            local.append(cp)
        for q, (t, r0, nr) in enumerate(items):
            cp = pltpu.make_async_copy(ins[t].at[pl.ds(r0, nr)], sbuf[t].at[pl.ds(r0, nr)], loc_in.at[q])
            cp.start()
            stage.append(cp)

        def copy(q):
            t, r0, nr = items[q]
            return pltpu.make_async_remote_copy(
                src_ref=sbuf[t].at[pl.ds(r0, nr)], dst_ref=rbuf[t].at[pl.ds(r0, nr)], send_sem=send.at[q],
                recv_sem=recv.at[q], device_id=(x, y, 1 - c), device_id_type=MESH)

        for q in range(m):
            stage[q].wait()
            copy(q).start()
        for q, (t, r0, nr) in enumerate(items):
            copy(q).wait_recv()
            cp = pltpu.make_async_copy(rbuf[t].at[pl.ds(r0, nr)], outs[t].at[1 - c, pl.ds(r0, nr)], loc_out.at[q])
            cp.start()
            local.append(cp)
        for q in range(m):
            copy(q).wait_send()
        for cp in local:
            cp.wait()

    stage_bytes = 2 * sum(_nbytes(a.shape, a.dtype) for a in halves)
    return pl.pallas_call(
        body, name=name, in_specs=[ANY] * n, out_specs=[ANY] * n,
        out_shape=[pltpu.HBM((2,) + a.shape, a.dtype) for a in halves],
        scratch_shapes=[pltpu.VMEM(a.shape, a.dtype) for a in halves] * 2
        + [pltpu.SemaphoreType.DMA((m,)), pltpu.SemaphoreType.DMA((m,)), pltpu.SemaphoreType.DMA((n,)),
           pltpu.SemaphoreType.DMA((m,)), pltpu.SemaphoreType.DMA((m,))],
        compiler_params=_params(stage_bytes + (4 << 20)),
    )(*halves)


def _allreduce_small(p, after, *, name):
    r = p.shape[0]

    def body(p_ref, after_ref, o_ref, buf, send, recv):
        x, y, c = _place()
        me = 4 * x + 2 * y + c
        peers = [(x ^ ((j >> 2) & 1), y ^ ((j >> 1) & 1), c ^ (j & 1)) for j in range(1, 8)]

        def copy(j, slot):
            return pltpu.make_async_remote_copy(
                src_ref=p_ref, dst_ref=buf.at[slot], send_sem=send.at[j], recv_sem=recv.at[j],
                device_id=peers[j], device_id_type=MESH)

        for j in range(7):
            copy(j, me).start()
        buf[me] = p_ref[...]
        for j in range(7):
            px, py, pc = peers[j]
            copy(j, 4 * px + 2 * py + pc).wait_recv()
        for j in range(7):
            copy(j, me).wait_send()
        acc = buf[0]
        for k in range(1, 8):
            acc = acc + buf[k]
        o_ref[...] = acc

    vspec = pl.BlockSpec(memory_space=pltpu.VMEM)
    return pl.pallas_call(
        body, name=name, in_specs=[vspec, ANY], out_specs=vspec, out_shape=jax.ShapeDtypeStruct((r, LANES), F32),
        scratch_shapes=[pltpu.VMEM((8, r, LANES), F32), pltpu.SemaphoreType.DMA((7,)), pltpu.SemaphoreType.DMA((7,))],
    )(p, after)


def _adamw_fn(w, g, m, v):
    m = ADAM_B1 * m + (1.0 - ADAM_B1) * g
    v = ADAM_B2 * v + (1.0 - ADAM_B2) * (g * g)
    m_hat = m / (1.0 - ADAM_B1 ** ADAM_STEP)
    v_hat = v / (1.0 - ADAM_B2 ** ADAM_STEP)
    delta = -ADAM_LR * (m_hat / (jnp.sqrt(v_hat) + ADAM_EPS) + ADAM_WD * w)
    return delta, m, v


def _adamw(w, g, m, v, *, name):
    c = w.shape[1]
    return _rowwise(_adamw_fn, [w, g, m, v], [], [(c, F32)] * 3, name=name, tr=128)


def _pack(vecs, rows):
    flat = jnp.concatenate([a.reshape(-1).astype(F32) for a in vecs])
    return jnp.pad(flat, (0, rows * LANES - flat.shape[0])).reshape(rows, LANES)


def _unpack(p, like):
    flat, out, o = p.reshape(-1), [], 0
    for a in like:
        out.append(flat[o:o + a.size].reshape(a.shape))
        o += a.size
    return out


def kernel(x, mem, g_mix, w_in, b_if, b_gate, conv_w, conv_b, ml_norm_g, g_mem, w_mem_kv, q_norm_g, k_norm_g, w_sb_proj, w_ml_proj, w_x_proj, w_out, g_mlp, w_ff1, w_ff2, loss_target, m_g_mix, m_w_in, m_b_if, m_b_gate, m_conv_w, m_conv_b, m_ml_norm_g, m_g_mem, m_w_mem_kv, m_q_norm_g, m_k_norm_g, m_w_sb_proj, m_w_ml_proj, m_w_x_proj, m_w_out, m_g_mlp, m_w_ff1, m_w_ff2, v_g_mix, v_w_in, v_b_if, v_b_gate, v_conv_w, v_conv_b, v_ml_norm_g, v_g_mem, v_w_mem_kv, v_q_norm_g, v_k_norm_g, v_w_sb_proj, v_w_ml_proj, v_w_x_proj, v_w_out, v_g_mlp, v_w_ff1, v_w_ff2):
    _, s, d = x.shape
    nm = mem.shape[1]
    n_in = 4 * w_in.shape[2]
    dff = 4 * w_ff1.shape[2]
    sbh = d // SB_HD
    hh = ML_HEADS
    dh = d // hh
    nc = s // CHUNK
    assert n_in == 11 * d + 2 * hh and d % (2 * LANES) == 0 and s % LANES == 0
    x2, mem2, tgt = x[0], mem[0], loss_target[0]

    k4 = 2 * lax.axis_index("x") + lax.axis_index("y")
    me = 2 * k4 + lax.axis_index("c")
    g_first = _allgather_two_level(w_in[0].astype(BF16), conv_w[0], name="gather_w_in")
    later = [a[0].astype(BF16) for a in (w_mem_kv, w_sb_proj, w_ml_proj, w_x_proj, w_out, w_ff1, w_ff2)]
    gw_send, gw_recv, gw_src, gw_land, gw_token = _split_start(
        "quarters", later, [(4,) + a.shape for a in later], g_first[0], name="gather_rest_start")
    cols = lambda a: a.transpose(1, 0, 2).reshape(a.shape[1], 4 * a.shape[2])
    rws = lambda a: a.reshape(4 * a.shape[1], a.shape[2])
    qn = n_in // 4
    if_lo, if_hi = 7 * d, 7 * d + 2 * hh

    def cut(lo, hi):
        ks = [(k, max(lo, k * qn), min(hi, (k + 1) * qn)) for k in range(4)]
        return [g_first[0][k, :, a - k * qn:b - k * qn] for k, a, b in ks if a < b]

    w_main = jnp.concatenate(cut(0, if_lo) + cut(if_hi, n_in), axis=1)
    w_if = jnp.pad(jnp.concatenate(cut(if_lo, if_hi), axis=1), ((0, 0), (0, LANES - 2 * hh)))
    conv_wf = cols(g_first[1])
    b_if_p = jnp.pad(b_if, ((0, 0), (0, LANES - 2 * hh)))

    (hn,) = _rowwise(_rms_fwd, [x2], [g_mix], [(d, BF16)], name="norm_in", tr=512)
    zm = _mm(hn, w_main, after=gw_token, name="proj_in")
    zif = _mm(hn, w_if, name="proj_if")
    y_sb, a_sb = _sb_fwd(zm, sbh, name="sb_fwd")

    def gate_fn(z, b):
        pre = z + b
        lane = lax.broadcasted_iota(jnp.int32, pre.shape, 1)
        return jnp.where(lane < hh, pre, -_softplus(-pre))

    (gcol,) = _rowwise(gate_fn, [zif], [b_if_p], [(LANES, F32)], name="ml_gates", tr=1024)
    grow = gcol[:, :8].T.reshape(8, nc, CHUNK).transpose(1, 0, 2)
    mqk = _conv_fwd(zm, 3 * d, 2 * d, conv_wf, conv_b, name="conv_fwd")
    hm, cst, nst, mst = _ml_fwd(mqk, zm, 5 * d, gcol, grow, d, name="ml_fwd")

    def mlout_fn(hv, o, g):
        ys = [_rms_fwd(hv[:, k * dh:(k + 1) * dh], g[:, k * dh:(k + 1) * dh]) for k in range(hh)]
        return jnp.concatenate(ys, axis=1) * _sigmoid(o)

    (y_ml,) = _rowwise(mlout_fn, [hm, (zm, d, 6)], [ml_norm_g], [(d, BF16)], name="ml_out", tr=512)
    gw_land = _split_wait("quarters", gw_send, gw_recv, gw_src, gw_land, [y_ml, y_sb], name="gather_rest_wait")
    gw = [lax.dynamic_update_index_in_dim(ld, a, k4, 0) for ld, a in zip(gw_land, later)]
    w_kv, w_sbp, w_mlp, w_xp, w_o, w_f1, w_f2 = (cols(gw[0]), rws(gw[1]), rws(gw[2]), rws(gw[3]), rws(gw[4]),
                                                 cols(gw[5]), rws(gw[6]))
    (memn,) = _rowwise(_rms_fwd, [mem2], [g_mem], [(d, BF16)], name="norm_mem")
    kv = _mm(memn, w_kv, name="proj_kv")
    y_x = _xa_fwd(zm, 7 * d, kv, q_norm_g, k_norm_g, d, name="xa_fwd")
    p_sb = _mm(y_sb, w_sbp, name="proj_sb")
    p_ml = _mm(y_ml, w_mlp, name="proj_ml")
    p_x = _mm(y_x, w_xp, name="proj_x")

    def merge_fn(a, b, c, g0, g1, g2, bg):
        return (_sigmoid(g0 + bg[:, :d]) * a + _sigmoid(g1 + bg[:, d:2 * d]) * b + _sigmoid(g2 + bg[:, 2 * d:]) * c)

    gate_cols = [(zm, d, 8), (zm, d, 9), (zm, d, 10)]
    (mixed,) = _rowwise(merge_fn, [p_sb, p_ml, p_x] + gate_cols, [b_gate], [(d, BF16)], name="merge")
    x1 = _mm(mixed, w_o, tiles=[x2], name="proj_out")
    (h2,) = _rowwise(_rms_fwd, [x1], [g_mlp], [(d, BF16)], name="norm_mlp", tr=512)
    u, act = _mm(h2, w_f1, post=lambda r: (r, jnp.square(jnp.maximum(r, 0.0))), out_dtype=(F32, BF16), name="ff1")
    dy = _mm(act, w_f2, tiles=[x1, tgt], post=lambda r, xv, tv: (r + xv - tv) * (1.0 / d), name="ff2")
    (loss_cols,) = _rowwise(lambda g: (jnp.sum(g * g, axis=0, keepdims=True) * (0.5 * d),), [dy], [], [], [d],
                            name="loss", tr=1024)

    du = _mm(dy, w_f2, tb=True, tiles=[u], post=lambda r, uv: r * 2.0 * jnp.maximum(uv, 0.0), out_dtype=BF16,
             name="ff2_dx")
    dw_f2 = _mm(act, dy, ta=True, name="ff2_dw")
    dw_f1 = _mm(h2, du, ta=True, name="ff1_dw")
    dh2 = _mm(du, w_f1, tb=True, name="ff1_dx")

    def norm_bwd_fn(xv, dyv, res, g):
        dx, dg = _rms_bwd(xv, g, dyv)
        return dx + res, jnp.sum(dg, axis=0, keepdims=True)

    dx1, dg_mlp = _rowwise(norm_bwd_fn, [x1, dh2, dy], [g_mlp], [(d, F32)], [d], name="norm_mlp_bwd", tr=512)
    dmixed = _mm(dx1, w_o, tb=True, name="proj_out_dx")
    dw_o = _mm(mixed, dx1, ta=True, name="proj_out_dw")

    def merge_bwd_fn(dm, a, b, c, g0, g1, g2, bg):
        outs, dgs = [], []
        for p, g, k in ((a, g0, 0), (b, g1, 1), (c, g2, 2)):
            sg = _sigmoid(g + bg[:, k * d:(k + 1) * d])
            outs.append(dm * sg)
            dgs.append(dm * p * sg * (1.0 - sg))
        dgate = jnp.concatenate(dgs, axis=1)
        return (*outs, dgate, jnp.sum(dgate, axis=0, keepdims=True))

    dp_sb, dp_ml, dp_x, dgate, db_gate = _rowwise(
        merge_bwd_fn, [dmixed, p_sb, p_ml, p_x] + gate_cols, [b_gate], [(d, BF16)] * 3 + [(3 * d, BF16)], [3 * d],
        name="merge_bwd", tr=256)
    dw_sbp = _mm(y_sb, dp_sb, ta=True, name="proj_sb_dw")
    dw_mlp = _mm(y_ml, dp_ml, ta=True, name="proj_ml_dw")
    dw_xp = _mm(y_x, dp_x, ta=True, name="proj_x_dw")
    dy_sb = _mm(dp_sb, w_sbp, tb=True, out_dtype=BF16, name="proj_sb_dx")
    dy_ml = _mm(dp_ml, w_mlp, tb=True, name="proj_ml_dx")
    dy_x = _mm(dp_x, w_xp, tb=True, out_dtype=BF16, name="proj_x_dx")

    dxq, dkn, dxv, dg_qn = _xa_bwd(zm, 7 * d, kv, q_norm_g, k_norm_g, dy_x, d, name="xa_bwd")

    def knorm_bwd_fn(kvv, dknv, dvv, g):
        dks, dgs = [], []
        for k in range(X_HEADS):
            sl = slice(k * dh, (k + 1) * dh)
            dk, dg = _rms_bwd(kvv[:, sl], g, dknv[:, sl])
            dks.append(dk)
            dgs.append(jnp.sum(dg, axis=0, keepdims=True))
        return jnp.concatenate(dks + [dvv], axis=1), dgs[0] + dgs[1] + dgs[2] + dgs[3]

    dkv, dg_kn = _rowwise(knorm_bwd_fn, [(kv, d, 0), dkn, dxv], [k_norm_g], [(2 * d, BF16)], [dh], name="xa_knorm_bwd")
    dw_kv = _mm(memn, dkv, ta=True, name="proj_kv_dw")
    dmemn = _mm(dkv, w_kv, tb=True, name="proj_kv_dx")

    def gmem_fn(mv, dv_, g):
        _, dg = _rms_bwd(mv, g, dv_)
        return (jnp.sum(dg, axis=0, keepdims=True),)

    (dg_mem,) = _rowwise(gmem_fn, [mem2, dmemn], [g_mem], [], [d], name="norm_mem_bwd")

    uncols = lambda a: a.reshape(a.shape[0], 4, a.shape[1] // 4).transpose(1, 0, 2)
    unrws = lambda a: a.reshape(4, a.shape[0] // 4, a.shape[1])
    to_parts = lambda q: q.astype(BF16).reshape(4, 2, q.shape[1] // 2, q.shape[2])
    early = [to_parts(q) for q in (uncols(dw_kv), unrws(dw_sbp), unrws(dw_mlp), unrws(dw_xp), unrws(dw_o),
                                   uncols(dw_f1), unrws(dw_f2))]
    ge_send, ge_recv, ge_src, ge_land, ge_token = _split_start(
        "grads", early, [(8,) + a.shape[2:] for a in early], dg_mem, name="exchange_early_start")

    dsq, dsk, dsv = _sb_bwd(zm, dy_sb, a_sb, ge_token, sbh, name="sb_bwd")

    def mlout_bwd_fn(dyv, hv, o, g):
        sg = _sigmoid(o)
        dn = dyv * sg
        dxs, dgs, ys = [], [], []
        for k in range(hh):
            sl = slice(k * dh, (k + 1) * dh)
            ys.append(_rms_fwd(hv[:, sl], g[:, sl]))
            dxk, dgk = _rms_bwd(hv[:, sl], g[:, sl], dn[:, sl])
            dxs.append(dxk)
            dgs.append(dgk)
        do = dyv * jnp.concatenate(ys, axis=1) * sg * (1.0 - sg)
        return jnp.concatenate(dxs, axis=1), do, jnp.sum(jnp.concatenate(dgs, axis=1), axis=0, keepdims=True)

    dhm, dmlo, dg_mln = _rowwise(mlout_bwd_fn, [dy_ml, hm, (zm, d, 6)], [ml_norm_g], [(d, F32), (d, BF16)], [d],
                                 name="ml_out_bwd", tr=512)
    dmqk, dmlv, dgc, dgr = _ml_bwd(mqk, zm, 5 * d, gcol, grow, cst, nst, mst, dhm, d, name="ml_bwd")
    dmlqk, dconv_w, dconv_b = _conv_bwd(zm, 3 * d, 2 * d, conv_wf, conv_b, dmqk, name="conv_bwd")
    dgr_t = jnp.pad(dgr.transpose(1, 0, 2).reshape(8, s).T, ((0, 0), (0, LANES - 8)))

    def gate_bwd_fn(a, b, z, bias):
        tot = a + b
        rows_t = tot.shape[0]
        r = lax.broadcasted_iota(jnp.int32, (rows_t, rows_t), 0)
        c = lax.broadcasted_iota(jnp.int32, (rows_t, rows_t), 1)
        sh = CHUNK.bit_length() - 1
        same_chunk = jnp.right_shift(r, sh) == jnp.right_shift(c, sh)
        dlf = _u01dot(((c >= r) & same_chunk).astype(BF16), tot)
        lane = lax.broadcasted_iota(jnp.int32, tot.shape, 1)
        dz = jnp.where(lane < hh, tot, jnp.where(lane < 2 * hh, dlf * _sigmoid(-(z + bias)), 0.0))
        return dz, jnp.sum(dz, axis=0, keepdims=True)

    dzif, db_if_p = _rowwise(gate_bwd_fn, [dgc, dgr_t, zif], [b_if_p], [(LANES, BF16)], [LANES], name="ml_gates_bwd",
                             tr=8 * CHUNK)
    dzm = jnp.concatenate([dsq, dsk, dsv, dmlqk, dmlv, dmlo, dxq, dgate], axis=1)
    dw_main = _mm(hn, dzm, ta=True, out_dtype=BF16, name="proj_in_dw")
    dw_if = _mm(hn, dzif, ta=True, out_dtype=BF16, name="proj_if_dw")

    def dw_quarter(k):
        lo, hi = k * qn, (k + 1) * qn
        segs = [(dw_main, 0, if_lo, 0), (dw_if, if_lo, if_hi, if_lo), (dw_main, if_hi, n_in, 2 * hh)]
        got = [src[:, max(lo, a) - off:min(hi, b) - off] for src, a, b, off in segs if max(lo, a) < min(hi, b)]
        return jnp.concatenate(got, axis=1)

    late = [to_parts(jnp.stack([dw_quarter(k) for k in range(4)]))]
    gl_send, gl_recv, gl_src, gl_land, gl_token = _split_start(
        "grads", late, [(8,) + a.shape[2:] for a in late], dw_if, name="exchange_late_start")
    dhn = _mm(dzm, w_main, tb=True, after=gl_token, name="proj_in_dx")
    dhn = _mm(dzif, w_if, tb=True, tiles=[dhn], name="proj_if_dx")
    dx, dg_mix = _rowwise(norm_bwd_fn, [x2, dhn, dx1], [g_mix], [(d, F32)], [d], name="norm_in_bwd", tr=512)

    own = lambda p: lax.dynamic_index_in_dim(lax.dynamic_index_in_dim(p, k4, 0, keepdims=False),
                                             lax.axis_index("c"), 0, keepdims=False)

    def finish(tag, send, recv, src, land, parts, after, ws, ms, vs):
        land = _split_wait("grads", send, recv, src, land, after, name=f"exchange_{tag}_wait")
        got = [lax.dynamic_update_index_in_dim(ld, own(p), me, 0) for ld, p in zip(land, parts)]
        halves = [_sum8(r, name=f"sum_grads_{tag}{i}") for i, r in enumerate(got)]
        both = _swap_halves(halves, name=f"swap_halves_{tag}")
        gs = [b.reshape(2 * b.shape[1], b.shape[2]) for b in both]
        return gs, [_adamw(w, g, m, v, name=f"adamw_{tag}{i}") for i, (w, g, m, v) in enumerate(zip(ws, gs, ms, vs))]

    first = lambda arrs: [a[0] for a in arrs]
    g_early, out_early = finish(
        "early", ge_send, ge_recv, ge_src, ge_land, early, [dx],
        first([w_mem_kv, w_sb_proj, w_ml_proj, w_x_proj, w_out, w_ff1, w_ff2]),
        first([m_w_mem_kv, m_w_sb_proj, m_w_ml_proj, m_w_x_proj, m_w_out, m_w_ff1, m_w_ff2]),
        first([v_w_mem_kv, v_w_sb_proj, v_w_ml_proj, v_w_x_proj, v_w_out, v_w_ff1, v_w_ff2]))
    g_late, out_late = finish(
        "late", gl_send, gl_recv, gl_src, gl_land, late, [o[0] for o in out_early],
        first([w_in]), first([m_w_in]), first([v_w_in]))
    g_big = [g[None] for g in g_late + g_early]
    big_out = [[o[None] for o in outs] for outs in out_late + out_early]

    small_g = [dg_mix, db_if_p[:, :2 * hh], db_gate, dconv_w, dconv_b, dg_mln, dg_mem, dg_qn, dg_kn, dg_mlp,
               jnp.sum(loss_cols).reshape(1, 1)]
    n_small = sum(a.size for a in small_g)
    rows = -(-n_small // (8 * LANES)) * 8
    g_small = _unpack(_allreduce_small(_pack(small_g, rows), out_late[0][0], name="allreduce_small"), small_g)
    loss = g_small[-1].reshape(())
    qw = conv_w.shape[2]
    g_conv_w = lax.dynamic_slice_in_dim(g_small[3], k4 * qw, qw, axis=1)
    g_small_w = [g_small[0], g_small[1], g_small[2], g_conv_w] + g_small[4:10]
    sm_w = [g_mix, b_if, b_gate, conv_w[0], conv_b, ml_norm_g, g_mem, q_norm_g, k_norm_g, g_mlp]
    sm_m = [m_g_mix, m_b_if, m_b_gate, m_conv_w[0], m_conv_b, m_ml_norm_g, m_g_mem, m_q_norm_g, m_k_norm_g, m_g_mlp]
    sm_v = [v_g_mix, v_b_if, v_b_gate, v_conv_w[0], v_conv_b, v_ml_norm_g, v_g_mem, v_q_norm_g, v_k_norm_g, v_g_mlp]
    n_sw = sum(a.size for a in sm_w)
    rows_w = -(-n_sw // (8 * LANES)) * 8
    sm_out = _adamw(_pack(sm_w, rows_w), _pack(g_small_w, rows_w), _pack(sm_m, rows_w), _pack(sm_v, rows_w),
                    name="adamw_small")
    sm_delta, sm_newm, sm_newv = [_unpack(p, sm_w) for p in sm_out]

    order = ["g_mix", "w_in", "b_if", "b_gate", "conv_w", "conv_b", "ml_norm_g", "g_mem", "w_mem_kv", "q_norm_g",
             "k_norm_g", "w_sb_proj", "w_ml_proj", "w_x_proj", "w_out", "g_mlp", "w_ff1", "w_ff2"]
    small_names = ["g_mix", "b_if", "b_gate", "conv_w", "conv_b", "ml_norm_g", "g_mem", "q_norm_g", "k_norm_g", "g_mlp"]
    big_names = ["w_in", "w_mem_kv", "w_sb_proj", "w_ml_proj", "w_x_proj", "w_out", "w_ff1", "w_ff2"]
    grads, deltas, new_m, new_v = {}, {}, {}, {}
    for i, nme in enumerate(small_names):
        shp = sm_w[i].shape if nme != "conv_w" else conv_w.shape
        grads[nme] = g_small_w[i].reshape(shp)
        deltas[nme], new_m[nme], new_v[nme] = (sm_delta[i].reshape(shp), sm_newm[i].reshape(shp),
                                               sm_newv[i].reshape(shp))
    for i, nme in enumerate(big_names):
        grads[nme] = g_big[i]
        deltas[nme], new_m[nme], new_v[nme] = big_out[i]
    return (loss, dx[None], *[grads[k] for k in order], *[deltas[k] for k in order], *[new_m[k] for k in order],
            *[new_v[k] for k in order])
```

```python
import functools

import jax
import jax.numpy as jnp
from jax import lax
from jax.experimental import pallas as pl
from jax.experimental.pallas import tpu as pltpu

F32 = jnp.float32
BF16 = jnp.bfloat16
MESH = pl.DeviceIdType.MESH

EPS = 1e-6
SB_HD = 128
SB_SLOTS = 8
ML_HEADS = 4
X_HEADS = 4
CHUNK = 64
CONV_W = 4
LANES = 128
ADAM_LR = 0.001
ADAM_B1 = 0.9
ADAM_B2 = 0.999
ADAM_EPS = 1e-08
ADAM_WD = 0.01
ADAM_STEP = 10
VMEM_CAP = 56 * 1024 * 1024
NEG = -1e30

NT = (((1,), (1,)), ((), ()))
NN = (((1,), (0,)), ((), ()))
TN = (((0,), (0,)), ((), ()))


def _dot(a, b, dn=NN):
    return lax.dot_general(a.astype(BF16), b.astype(BF16), dn, preferred_element_type=F32)


def _dot01(x, u, dn=NN):
    hi = x.astype(BF16)
    lo = (x - hi.astype(F32)).astype(BF16)
    return (lax.dot_general(hi, u, dn, preferred_element_type=F32)
            + lax.dot_general(lo, u, dn, preferred_element_type=F32))


def _u01dot(u, x):
    hi = x.astype(BF16)
    lo = (x - hi.astype(F32)).astype(BF16)
    return (lax.dot_general(u, hi, NN, preferred_element_type=F32)
            + lax.dot_general(u, lo, NN, preferred_element_type=F32))


def _pick(n, cands):
    for c in cands:
        if c <= n and n % c == 0:
            return c
    return n


def _nbytes(shape, dtype):
    n = 1
    for s in shape:
        n *= s
    return n * jnp.dtype(dtype).itemsize


def _params(vmem_bytes):
    return pltpu.CompilerParams(vmem_limit_bytes=int(min(VMEM_CAP, max(vmem_bytes, 16 * 1024 * 1024))))


def _hbm(a):
    return pltpu.with_memory_space_constraint(a, pltpu.HBM)


def _softplus(z):
    return jnp.maximum(z, 0.0) + jnp.log(1.0 + jnp.exp(-jnp.abs(z)))


def _sigmoid(z):
    return 1.0 / (1.0 + jnp.exp(-z))


def _rms_fwd(xv, g):
    r = lax.rsqrt(jnp.mean(xv * xv, axis=-1, keepdims=True) + EPS)
    return xv * r * g


def _rms_bwd(xv, g, dy):
    r = lax.rsqrt(jnp.mean(xv * xv, axis=-1, keepdims=True) + EPS)
    xh = xv * r
    dxh = dy * g
    dx = r * (dxh - xh * jnp.mean(dxh * xh, axis=-1, keepdims=True))
    return dx, dy * xh


def _mm(a, b, *, name, ta=False, tb=False, tiles=(), post=None, out_dtype=F32, bm=1024, bn=1024, bk=1024, after=None):
    m, k = (a.shape[1], a.shape[0]) if ta else a.shape
    n = b.shape[0] if tb else b.shape[1]
    tm = _pick(m, (bm, 512, 256, 128))
    tn = _pick(n, (bn, 512, 256, 128))
    tk = _pick(k, (bk, 512, 256, 128))
    nk = k // tk
    if (m // tm) * (n // tn) * nk < 8 and tm % 256 == 0:
        tm //= 2
    dn = (((0 if ta else 1,), (1 if tb else 0,)), ((), ()))
    dts = out_dtype if isinstance(out_dtype, tuple) else (out_dtype,)
    nt, no = len(tiles), len(dts)
    if post is None:
        post = lambda r, *ts: sum((t.astype(F32) for t in ts), r)

    def body(*refs):
        a_ref, b_ref = refs[:2]
        t_refs = refs[2:2 + nt]
        o_refs = refs[2 + nt + (after is not None):2 + nt + (after is not None) + no]
        part = lax.dot_general(a_ref[...].astype(BF16), b_ref[...].astype(BF16), dn, preferred_element_type=F32)

        def finish(r):
            res = post(r, *[t[...] for t in t_refs])
            res = res if isinstance(res, tuple) else (res,)
            for o, v in zip(o_refs, res):
                o[...] = v.astype(o.dtype)

        if nk == 1:
            finish(part)
        else:
            acc_ref = refs[-1]
            kk = pl.program_id(2)

            @pl.when(kk == 0)
            def _():
                acc_ref[...] = part

            @pl.when(kk > 0)
            def _():
                acc_ref[...] += part

            @pl.when(kk == nk - 1)
            def _():
                finish(acc_ref[...])

    a_spec = pl.BlockSpec((tk, tm), lambda i, j, q: (q, i)) if ta else pl.BlockSpec((tm, tk), lambda i, j, q: (i, q))
    b_spec = pl.BlockSpec((tn, tk), lambda i, j, q: (j, q)) if tb else pl.BlockSpec((tk, tn), lambda i, j, q: (q, j))
    o_spec = pl.BlockSpec((tm, tn), lambda i, j, q: (i, j))
    ins, specs = [_hbm(a), _hbm(b)] + [_hbm(t) for t in tiles], [a_spec, b_spec] + [o_spec] * nt
    vm = 2 * (_nbytes((tm, tk), a.dtype) + _nbytes((tk, tn), b.dtype)) + 3 * _nbytes((tm, tn), F32) \
        + _nbytes((tm, tk), BF16) + _nbytes((tk, tn), BF16) \
        + 2 * sum(_nbytes((tm, tn), t.dtype) for t in tiles) + 2 * sum(_nbytes((tm, tn), dt) for dt in dts)
    if after is not None:
        ins.append(after)
        specs.append(ANY)
    res = pl.pallas_call(
        body, name=name, grid=(m // tm, n // tn, nk), in_specs=specs, out_specs=[o_spec] * no,
        out_shape=[pltpu.HBM((m, n), dt) for dt in dts], scratch_shapes=[pltpu.VMEM((tm, tn), F32)] if nk > 1 else [],
        compiler_params=_params(vm + (4 << 20)),
    )(*ins)
    return res[0] if no == 1 else tuple(res)


def _rowwise(fn, rows, consts, outs, reds=(), *, name, tr=256, temps=6, into=None):
    rows = [r if isinstance(r, tuple) else (r, r.shape[1], 0) for r in rows]
    nrows = rows[0][0].shape[0]
    t = _pick(nrows, (tr, 128, 64, 32, 16, 8))
    nr, nc, no = len(rows), len(consts), len(outs)
    nb = 0 if into is None else 1

    def body(*refs):
        rin, cin = refs[:nr], refs[nr:nr + nc]
        oref, rref = refs[nr + nc + nb:nr + nc + nb + no], refs[nr + nc + nb + no:]
        res = fn(*[r[...] for r in rin], *[c[...] for c in cin])
        if not isinstance(res, (tuple, list)):
            res = (res,)
        for o, v in zip(oref, res[:no]):
            o[...] = v.astype(o.dtype)
        if rref:
            @pl.when(pl.program_id(0) == 0)
            def _():
                for r in rref:
                    r[...] = jnp.zeros_like(r)

            for r, v in zip(rref, res[no:]):
                r[...] += v

    in_specs = [pl.BlockSpec((t, w), functools.partial(lambda i, ci: (i, ci), ci=ci)) for (_, w, ci) in rows]
    in_specs += [pl.BlockSpec(c.shape, functools.partial(lambda i, nd: (0,) * nd, nd=c.ndim)) for c in consts]
    out_specs = [pl.BlockSpec((t, w), lambda i: (i, 0)) for (w, _) in outs]
    out_specs += [pl.BlockSpec((1, w), lambda i: (0, 0)) for w in reds]
    out_shape = [pltpu.HBM((nrows, w), dt) for (w, dt) in outs]
    out_shape += [jax.ShapeDtypeStruct((1, w), F32) for w in reds]
    widest = max([w for (_, w, _) in rows] + [w for (w, _) in outs])
    vm = 2 * sum(_nbytes((t, w), a.dtype) for (a, w, _) in rows) + 2 * sum(_nbytes((t, w), dt) for (w, dt) in outs)
    vm += temps * _nbytes((t, widest), F32) + (2 << 20)
    extra, aliases = [], {}
    if into is not None:
        buf, oi, cb = into
        out_specs[oi] = pl.BlockSpec((t, outs[oi][0]), lambda i: (i, cb))
        out_shape[oi] = pltpu.HBM(buf.shape, buf.dtype)
        in_specs.append(ANY)
        extra, aliases = [buf], {nr + nc: oi}
    res = pl.pallas_call(
        body, name=name, grid=(nrows // t,), in_specs=in_specs, out_specs=out_specs, out_shape=out_shape,
        input_output_aliases=aliases, compiler_params=_params(vm),
    )(*[_hbm(a) for (a, _, _) in rows], *consts, *extra)
    return list(res)


def _sb_tiles(s, tq, tk):
    tq = _pick(s, (tq, 256, 128))
    tk = _pick(tq, (tk, 128))
    return tq, tk, tq // tk


def _sb_fwd(zm, heads, *, name, tq=512, tk=256):
    s = zm.shape[0]
    tq, tk, nd = _sb_tiles(s, tq, tk)
    scale = SB_HD ** -0.5

    def body(q_ref, k_ref, v_ref, o_ref, a_out, stage, sem):
        h, i = pl.program_id(0), pl.program_id(1)
        qb = (q_ref[...] * scale).astype(BF16)
        r = lax.broadcasted_iota(jnp.int32, (tq, tk), 0)
        c = lax.broadcasted_iota(jnp.int32, (tq, tk), 1)
        ur = lax.broadcasted_iota(jnp.int32, (tk, tk), 0)
        uc = lax.broadcasted_iota(jnp.int32, (tk, tk), 1)
        usuf = (ur > uc).astype(BF16)

        def out_copy(slot, j):
            return pltpu.make_async_copy(stage.at[slot], a_out.at[h, i, j], sem.at[slot])

        def tile(j, carry, causal, slot, reuse):
            acc, cl = carry
            if reuse is True:
                out_copy(slot, 0).wait()
            elif reuse is not None:
                @pl.when(reuse)
                def _():
                    out_copy(slot, 0).wait()
            rows = pl.ds(pl.multiple_of(j * tk, tk), tk)
            kb = k_ref[rows, :].astype(BF16)
            vb = v_ref[rows, :].astype(BF16)
            z = lax.dot_general(qb, kb, NT, preferred_element_type=F32)
            lsig = -_softplus(z)
            l = lsig if causal is None else jnp.where(causal, lsig, 0.0)
            loga = z + lsig + _dot01(l, usuf) + cl
            if causal is not None:
                loga = jnp.where(causal, loga, NEG)
            ab = jnp.exp(loga).astype(BF16)
            acc = acc + lax.dot_general(ab, vb, NN, preferred_element_type=F32)
            stage[slot] = ab
            out_copy(slot, j).start()
            return acc, cl + jnp.sum(l, axis=1, keepdims=True)

        carry = (jnp.zeros((tq, SB_HD), F32), jnp.zeros((tq, 1), F32))
        for n, dd in enumerate(range(nd - 1, -1, -1)):
            carry = tile(i * nd + dd, carry, c + dd * tk < r, n, None)

        def rest(n, cr):
            return tile(i * nd - 1 - n, cr, None, (nd + n) % SB_SLOTS, nd + n >= SB_SLOTS)

        acc, _ = lax.fori_loop(0, i * nd, rest, carry)
        total = (i + 1) * nd
        for back in range(1, SB_SLOTS + 1):
            @pl.when(total >= back)
            def _():
                out_copy((total - back) % SB_SLOTS, 0).wait()

        o_ref[...] = acc.astype(o_ref.dtype)

    assert nd <= SB_SLOTS
    blk = lambda off: pl.BlockSpec((s, SB_HD), functools.partial(lambda h, i, off: (0, off + h), off=off))
    return pl.pallas_call(
        body, name=name, grid=(heads, s // tq),
        in_specs=[pl.BlockSpec((tq, SB_HD), lambda h, i: (i, h)), blk(heads), blk(2 * heads)],
        out_specs=[pl.BlockSpec((tq, SB_HD), lambda h, i: (i, h)), ANY],
        out_shape=[pltpu.HBM((s, heads * SB_HD), BF16), pltpu.HBM((heads, s // tq, s // tk, tq, tk), BF16)],
        scratch_shapes=[pltpu.VMEM((SB_SLOTS, tq, tk), BF16), pltpu.SemaphoreType.DMA((SB_SLOTS,))],
        compiler_params=_params(8 * s * SB_HD * 4 + 24 * tq * tk * 4 + (8 << 20)),
    )(_hbm(zm), _hbm(zm), _hbm(zm))


def _sb_bwd(zm, dy, a_all, dz, after, heads, *, name, tq=512, tk=256):
    s = zm.shape[0]
    tq, tk, nd = _sb_tiles(s, tq, tk)
    nq = s // tq
    scale = SB_HD ** -0.5

    def body(q_ref, k_ref, v_ref, do_ref, a_in, dz_ref, after_ref, dq_ref, dk_ref, dv_ref, dka, dva, abuf, sem):
        h, i = pl.program_id(0), pl.program_id(1)

        @pl.when(i == 0)
        def _():
            dka[...] = jnp.zeros_like(dka)
            dva[...] = jnp.zeros_like(dva)

        qb = (q_ref[...] * scale).astype(BF16)
        dob = do_ref[...].astype(BF16)
        qb_t = (q_ref[...] * scale).T.astype(BF16)
        dob_t = do_ref[...].astype(F32).T.astype(BF16)
        r = lax.broadcasted_iota(jnp.int32, (tq, tk), 0)
        c = lax.broadcasted_iota(jnp.int32, (tq, tk), 1)
        ur = lax.broadcasted_iota(jnp.int32, (tk, tk), 0)
        uc = lax.broadcasted_iota(jnp.int32, (tk, tk), 1)
        uexcl = (ur < uc).astype(BF16)

        def fetch(j, slot):
            return pltpu.make_async_copy(a_in.at[h, i, j], abuf.at[slot], sem.at[slot])

        total = (i + 1) * nd
        ahead = SB_SLOTS - 1

        def tile(j, carry, causal):
            dq, cg = carry
            slot = j % SB_SLOTS
            fetch(j, slot).wait()

            @pl.when(j + ahead < total)
            def _():
                fetch(j + ahead, (j + ahead) % SB_SLOTS).start()

            rows = pl.ds(pl.multiple_of(j * tk, tk), tk)
            kb = k_ref[rows, :].astype(BF16)
            vb = v_ref[rows, :].astype(BF16)
            z = lax.dot_general(qb, kb, NT, preferred_element_type=F32)
            sig = 1.0 / (1.0 + jnp.exp(-z))
            ab = abuf[slot]
            g = ab.astype(F32) * lax.dot_general(dob, vb, NT, preferred_element_type=F32)
            p = cg + lax.dot_general(g.astype(BF16), uexcl, NN, preferred_element_type=F32)
            dz = g - sig * (g + p)
            if causal is not None:
                dz = jnp.where(causal, dz, 0.0)
            dzb = dz.astype(BF16)
            dva[j] += lax.dot_general(dob_t, ab, NN, preferred_element_type=F32)
            dka[j] += lax.dot_general(qb_t, dzb, NN, preferred_element_type=F32)
            dq = dq + lax.dot_general(dzb, kb, NN, preferred_element_type=F32)
            return dq, cg + jnp.sum(g, axis=1, keepdims=True)

        for first in range(ahead):
            @pl.when(first < total)
            def _():
                fetch(first, first).start()

        init = (jnp.zeros((tq, SB_HD), F32), jnp.zeros((tq, 1), F32))
        carry = lax.fori_loop(0, i * nd, lambda j, cr: tile(j, cr, None), init)
        for dd in range(nd):
            carry = tile(i * nd + dd, carry, c + dd * tk < r)
        dq_ref[...] = (carry[0] * scale).astype(dq_ref.dtype)

        @pl.when(i == nq - 1)
        def _():
            for jj in range(s // tk):
                dk_ref[jj * tk:(jj + 1) * tk, :] = dka[jj].T.astype(dk_ref.dtype)
                dv_ref[jj * tk:(jj + 1) * tk, :] = dva[jj].T.astype(dv_ref.dtype)

    blk = lambda off: pl.BlockSpec((s, SB_HD), functools.partial(lambda h, i, off: (0, off + h), off=off))
    tile_spec = pl.BlockSpec((tq, SB_HD), lambda h, i: (i, h))
    full = pltpu.HBM((s, heads * SB_HD), BF16)
    return pl.pallas_call(
        body, name=name, grid=(heads, nq),
        in_specs=[tile_spec, blk(heads), blk(2 * heads), tile_spec, ANY, ANY, ANY],
        out_specs=[tile_spec, blk(0), blk(0)],
        out_shape=[pltpu.HBM(dz.shape, dz.dtype), full, full],
        input_output_aliases={5: 0},
        scratch_shapes=[pltpu.VMEM((s // tk, SB_HD, tk), F32), pltpu.VMEM((s // tk, SB_HD, tk), F32),
                        pltpu.VMEM((SB_SLOTS, tq, tk), BF16), pltpu.SemaphoreType.DMA((SB_SLOTS,))],
        compiler_params=_params(12 * s * SB_HD * 4 + 32 * tq * tk * 4 + (8 << 20)),
    )(_hbm(zm), _hbm(zm), _hbm(zm), _hbm(dy), a_all, dz, after)


def _conv_taps(u, w_ref, rows_i):
    taps = []
    for j in range(CONV_W):
        sh = CONV_W - 1 - j
        if sh == 0:
            taps.append(u)
        else:
            taps.append(jnp.where(rows_i >= sh, pltpu.roll(u, sh, 0), 0.0))
    return taps


def _conv_fwd(zm, col0, width, cw, cb, *, name):
    s = zm.shape[0]
    bw = _pick(width, (LANES,))
    off = col0 // bw

    def body(u_ref, w_ref, b_ref, o_ref):
        u = u_ref[...]
        rows_i = lax.broadcasted_iota(jnp.int32, u.shape, 0)
        acc = jnp.broadcast_to(b_ref[...], u.shape)
        for j, tp in enumerate(_conv_taps(u, w_ref, rows_i)):
            acc = acc + tp * w_ref[j:j + 1, :]
        o_ref[...] = acc * _sigmoid(acc)

    return pl.pallas_call(
        body, name=name, grid=(width // bw,),
        in_specs=[pl.BlockSpec((s, bw), lambda j: (0, off + j)), pl.BlockSpec((CONV_W, bw), lambda j: (0, j)),
                  pl.BlockSpec((1, bw), lambda j: (0, j))],
        out_specs=pl.BlockSpec((s, bw), lambda j: (0, j)),
        out_shape=pltpu.HBM((s, width), F32),
        compiler_params=_params(12 * s * bw * 4 + (4 << 20)),
    )(_hbm(zm), cw, cb)


def _conv_bwd(zm, col0, width, cw, cb, dqk, dz, *, name):
    s = zm.shape[0]
    bw = _pick(width, (LANES,))
    off = col0 // bw

    def body(u_ref, w_ref, b_ref, d_ref, dz_ref, du_ref, dw_ref, db_ref):
        u = u_ref[...]
        rows_i = lax.broadcasted_iota(jnp.int32, u.shape, 0)
        taps = _conv_taps(u, w_ref, rows_i)
        acc = jnp.broadcast_to(b_ref[...], u.shape)
        for j, tp in enumerate(taps):
            acc = acc + tp * w_ref[j:j + 1, :]
        sg = _sigmoid(acc)
        dc = d_ref[...] * (sg * (1.0 + acc * (1.0 - sg)))
        du = jnp.zeros_like(u)
        for j in range(CONV_W):
            sh = CONV_W - 1 - j
            if sh == 0:
                du = du + dc * w_ref[j:j + 1, :]
            else:
                du = du + jnp.where(rows_i < s - sh, pltpu.roll(dc, s - sh, 0), 0.0) * w_ref[j:j + 1, :]
            dw_ref[j:j + 1, :] = jnp.sum(dc * taps[j], axis=0, keepdims=True)
        du_ref[...] = du.astype(du_ref.dtype)
        db_ref[...] = jnp.sum(dc, axis=0, keepdims=True)

    return pl.pallas_call(
        body, name=name, grid=(width // bw,),
        in_specs=[pl.BlockSpec((s, bw), lambda j: (0, off + j)), pl.BlockSpec((CONV_W, bw), lambda j: (0, j)),
                  pl.BlockSpec((1, bw), lambda j: (0, j)), pl.BlockSpec((s, bw), lambda j: (0, j)), ANY],
        out_specs=[pl.BlockSpec((s, bw), lambda j: (0, off + j)), pl.BlockSpec((CONV_W, bw), lambda j: (0, j)),
                   pl.BlockSpec((1, bw), lambda j: (0, j))],
        out_shape=[pltpu.HBM(dz.shape, dz.dtype), pltpu.HBM((CONV_W, width), F32),
                   pltpu.HBM((1, width), F32)],
        input_output_aliases={4: 0},
        compiler_params=_params(20 * s * bw * 4 + (4 << 20)),
    )(_hbm(zm), cw, cb, _hbm(dqk), dz)


def _ml_gates(gcol_ref, grow_ref):
    l = CHUNK
    r = lax.broadcasted_iota(jnp.int32, (l, l), 0)
    c = lax.broadcasted_iota(jnp.int32, (l, l), 1)
    gcol = gcol_ref[...]
    grow = grow_ref[0]
    bcol = _u01dot((c <= r).astype(BF16), gcol)
    brow = _dot01(grow, (r <= c).astype(BF16))
    return gcol, grow, bcol, brow, r >= c


def _ml_chunk(h, dh, mq_ref, mk_ref, v_ref, gates, cp, n_prev, m_prev):
    gcol, grow, bcol, brow, tri = gates
    l = CHUNK
    sl = slice(h * dh, (h + 1) * dh)
    qc = mq_ref[:, sl]
    kc = mk_ref[:, sl] * (dh ** -0.5)
    vc = v_ref[:, sl]
    i_row = grow[h:h + 1, :]
    i_col = gcol[:, h:h + 1]
    b_col = bcol[:, ML_HEADS + h:ML_HEADS + h + 1]
    b_row = brow[ML_HEADS + h:ML_HEADS + h + 1, :]
    b_end = b_col[l - 1:l, :]
    d = jnp.where(tri, b_col - b_row + i_row, -jnp.inf)
    m_inter = b_col + m_prev
    m_t = jnp.maximum(m_inter, jnp.max(d, axis=1, keepdims=True))
    w = jnp.exp(d - m_t)
    s_inter = jnp.exp(m_inter - m_t)
    qb, kb, vb = qc.astype(BF16), kc.astype(BF16), vc.astype(BF16)
    cpb = cp.astype(BF16)
    a = lax.dot_general(qb, kb, NT, preferred_element_type=F32)
    sc = a * w
    qcp = lax.dot_general(qb, cpb, NT, preferred_element_type=F32)
    qn = jnp.sum(qc * n_prev, axis=1, keepdims=True)
    num = lax.dot_general(sc.astype(BF16), vb, NN, preferred_element_type=F32) + s_inter * qcp
    den = jnp.sum(sc, axis=1, keepdims=True) + s_inter * qn
    floor = jnp.exp(-m_t)
    dnm = jnp.maximum(jnp.abs(den), floor)
    g_col = b_end - b_col + i_col
    g_row = b_end - b_row + i_row
    m_new = jnp.maximum(b_end + m_prev, jnp.max(g_row, axis=1, keepdims=True))
    decay = jnp.exp(b_end + m_prev - m_new)
    wk = jnp.exp(g_col - m_new)
    return dict(qc=qc, kc=kc, vc=vc, qb=qb, kb=kb, vb=vb, cpb=cpb, w=w, s_inter=s_inter, a=a, sc=sc, qcp=qcp, qn=qn,
                num=num, den=den, floor=floor, dnm=dnm, m_new=m_new, decay=decay, wk=wk, sl=sl)


def _ml_fwd(mqk, zm, vcol, gcol, grow, d_model, *, name):
    s = zm.shape[0]
    nc = s // CHUNK
    dh = d_model // ML_HEADS
    hh = ML_HEADS

    def body(mq_ref, mk_ref, v_ref, gcol_ref, grow_ref, h_ref, cs_ref, ns_ref, ms_ref, c_s, n_s, m_s):
        @pl.when(pl.program_id(0) == 0)
        def _():
            c_s[...] = jnp.zeros_like(c_s)
            n_s[...] = jnp.zeros_like(n_s)
            m_s[...] = jnp.zeros_like(m_s)

        gates = _ml_gates(gcol_ref, grow_ref)
        for h in range(hh):
            cp, n_prev, m_prev = c_s[h], n_s[h], m_s[h][:, 0:1]
            cs_ref[0, h] = cp
            ns_ref[0, h] = n_prev
            ms_ref[0, h] = m_s[h]
            f = _ml_chunk(h, dh, mq_ref, mk_ref, v_ref, gates, cp, n_prev, m_prev)
            h_ref[:, f["sl"]] = f["num"] / f["dnm"]
            c_s[h] = f["decay"] * cp + lax.dot_general((f["vc"] * f["wk"]).astype(BF16), f["kb"], TN,
                                                       preferred_element_type=F32)
            n_s[h] = f["decay"] * n_prev + jnp.sum(f["wk"] * f["kc"], axis=0, keepdims=True)
            m_s[h] = jnp.broadcast_to(f["m_new"], (1, LANES))

    dblk = d_model
    return pl.pallas_call(
        body, name=name, grid=(nc,),
        in_specs=[pl.BlockSpec((CHUNK, dblk), lambda c: (c, 0)), pl.BlockSpec((CHUNK, dblk), lambda c: (c, 1)),
                  pl.BlockSpec((CHUNK, dblk), lambda c: (c, vcol // dblk)),
                  pl.BlockSpec((CHUNK, LANES), lambda c: (c, 0)), pl.BlockSpec((1, 8, CHUNK), lambda c: (c, 0, 0))],
        out_specs=[pl.BlockSpec((CHUNK, dblk), lambda c: (c, 0)),
                   pl.BlockSpec((1, hh, dh, dh), lambda c: (c, 0, 0, 0)),
                   pl.BlockSpec((1, hh, 1, dh), lambda c: (c, 0, 0, 0)),
                   pl.BlockSpec((1, hh, 1, LANES), lambda c: (c, 0, 0, 0))],
        out_shape=[pltpu.HBM((s, d_model), F32), pltpu.HBM((nc, hh, dh, dh), F32),
                   pltpu.HBM((nc, hh, 1, dh), F32), pltpu.HBM((nc, hh, 1, LANES), F32)],
        scratch_shapes=[pltpu.VMEM((hh, dh, dh), F32), pltpu.VMEM((hh, 1, dh), F32), pltpu.VMEM((hh, 1, LANES), F32)],
        compiler_params=_params(8 * hh * dh * dh * 4 + (16 << 20)),
    )(_hbm(mqk), _hbm(mqk), _hbm(zm), _hbm(gcol), _hbm(grow))


def _ml_bwd(mqk, zm, vcol, gcol, grow, cs, ns, ms, dhm, dz, d_model, *, name):
    s = zm.shape[0]
    nc = s // CHUNK
    dh = d_model // ML_HEADS
    hh = ML_HEADS
    l = CHUNK

    def body(mq_ref, mk_ref, v_ref, gcol_ref, grow_ref, cs_ref, ns_ref, ms_ref, dh_ref, dz_ref,
             dqk_ref, dv_ref, dgc_ref, dgr_ref, dc_s, dn_s):
        @pl.when(pl.program_id(0) == 0)
        def _():
            dc_s[...] = jnp.zeros_like(dc_s)
            dn_s[...] = jnp.zeros_like(dn_s)

        gates = _ml_gates(gcol_ref, grow_ref)
        lane = lax.broadcasted_iota(jnp.int32, (l, LANES), 1)
        rowi = lax.broadcasted_iota(jnp.int32, (8, l), 0)
        lastrow = lax.broadcasted_iota(jnp.int32, (l, 1), 0) == l - 1
        dgc = jnp.zeros((l, LANES), F32)
        dgr = jnp.zeros((8, l), F32)
        for h in range(hh):
            cp, n_prev, m_prev = cs_ref[0, h], ns_ref[0, h], ms_ref[0, h][:, 0:1]
            f = _ml_chunk(h, dh, mq_ref, mk_ref, v_ref, gates, cp, n_prev, m_prev)
            dC, dn = dc_s[h], dn_s[h]
            dhv = dh_ref[:, f["sl"]]
            dnum = dhv / f["dnm"]
            hv = f["num"] / f["dnm"]
            ddnm = -jnp.sum(dhv * hv, axis=1, keepdims=True) / f["dnm"]
            dden = jnp.where(jnp.abs(f["den"]) >= f["floor"], ddnm * jnp.sign(f["den"]), 0.0)
            dnb = dnum.astype(BF16)
            dsc = lax.dot_general(dnb, f["vb"], NT, preferred_element_type=F32) + dden
            dvc = lax.dot_general(f["sc"].astype(BF16), dnb, TN, preferred_element_type=F32)
            ds_inter = jnp.sum(dnum * f["qcp"], axis=1, keepdims=True) + dden * f["qn"]
            sdn = (f["s_inter"] * dnum).astype(BF16)
            sdd = f["s_inter"] * dden
            da = dsc * f["w"]
            dab = da.astype(BF16)
            dqc = (lax.dot_general(dab, f["kb"], NN, preferred_element_type=F32)
                   + lax.dot_general(sdn, f["cpb"], NN, preferred_element_type=F32) + sdd * n_prev)
            dcp = f["decay"] * dC + lax.dot_general(sdn, f["qb"], TN, preferred_element_type=F32)
            dnp = f["decay"] * dn + jnp.sum(sdd * f["qc"], axis=0, keepdims=True)
            vw = (f["vc"] * f["wk"]).astype(BF16)
            dCb = dC.astype(BF16)
            dkc = (lax.dot_general(dab, f["qb"], TN, preferred_element_type=F32)
                   + lax.dot_general(vw, dCb, NN, preferred_element_type=F32) + f["wk"] * dn)
            e = lax.dot_general(f["kb"], dCb, NT, preferred_element_type=F32)
            dvc = dvc + e * f["wk"]
            dwk = jnp.sum(e * f["vc"], axis=1, keepdims=True) + jnp.sum(f["kc"] * dn, axis=1, keepdims=True)
            ddecay = jnp.sum(jnp.sum(dC * cp, axis=1, keepdims=True), axis=0, keepdims=True) \
                + jnp.sum(dn * n_prev, axis=1, keepdims=True)
            dd = dsc * f["sc"]
            dlw = dwk * f["wk"]
            db_end = jnp.sum(dlw, axis=0, keepdims=True) + ddecay * f["decay"]
            di_col = dlw
            db_col = jnp.sum(dd, axis=1, keepdims=True) + ds_inter * f["s_inter"] - dlw \
                + jnp.where(lastrow, db_end, 0.0)
            cs_dd = jnp.sum(dd, axis=0, keepdims=True)
            dgc = dgc + jnp.where(lane == h, di_col, 0.0) + jnp.where(lane == hh + h, db_col, 0.0)
            dgr = dgr + jnp.where(rowi == h, cs_dd, 0.0) - jnp.where(rowi == hh + h, cs_dd, 0.0)
            dqk_ref[:, f["sl"]] = dqc
            dqk_ref[:, d_model + h * dh:d_model + (h + 1) * dh] = dkc * (dh ** -0.5)
            dv_ref[:, f["sl"]] = dvc.astype(dv_ref.dtype)
            dc_s[h] = dcp
            dn_s[h] = dnp
        dgc_ref[...] = dgc
        dgr_ref[0] = dgr

    dblk = d_model
    rev = lambda c: nc - 1 - c
    return pl.pallas_call(
        body, name=name, grid=(nc,),
        in_specs=[pl.BlockSpec((l, dblk), lambda c: (rev(c), 0)), pl.BlockSpec((l, dblk), lambda c: (rev(c), 1)),
                  pl.BlockSpec((l, dblk), lambda c: (rev(c), vcol // dblk)),
                  pl.BlockSpec((l, LANES), lambda c: (rev(c), 0)), pl.BlockSpec((1, 8, l), lambda c: (rev(c), 0, 0)),
                  pl.BlockSpec((1, hh, dh, dh), lambda c: (rev(c), 0, 0, 0)),
                  pl.BlockSpec((1, hh, 1, dh), lambda c: (rev(c), 0, 0, 0)),
                  pl.BlockSpec((1, hh, 1, LANES), lambda c: (rev(c), 0, 0, 0)),
                  pl.BlockSpec((l, dblk), lambda c: (rev(c), 0)), ANY],
        out_specs=[pl.BlockSpec((l, 2 * dblk), lambda c: (rev(c), 0)),
                   pl.BlockSpec((l, dblk), lambda c: (rev(c), vcol // dblk)),
                   pl.BlockSpec((l, LANES), lambda c: (rev(c), 0)),
                   pl.BlockSpec((1, 8, l), lambda c: (rev(c), 0, 0))],
        out_shape=[pltpu.HBM((s, 2 * d_model), F32),
                   pltpu.HBM(dz.shape, dz.dtype), pltpu.HBM((s, LANES), F32),
                   pltpu.HBM((nc, 8, l), F32)],
        input_output_aliases={9: 1},
        scratch_shapes=[pltpu.VMEM((hh, dh, dh), F32), pltpu.VMEM((hh, 1, dh), F32)],
        compiler_params=_params(10 * hh * dh * dh * 4 + (16 << 20)),
    )(*[_hbm(a) for a in (mqk, mqk, zm, gcol, grow, cs, ns, ms, dhm)], dz)


def _xa_fwd(zm, qcol, kv, gq, gk, d_model, *, name, tq=512):
    s = zm.shape[0]
    nm = kv.shape[0]
    dh = d_model // X_HEADS
    tq = _pick(s, (tq, 128, 64))
    scale = dh ** -0.5

    def body(q_ref, k_ref, v_ref, gq_ref, gk_ref, o_ref):
        qn = _rms_fwd(q_ref[...], gq_ref[...])
        kn = _rms_fwd(k_ref[...], gk_ref[...])
        lg = _dot(qn, kn, NT) * scale
        lg = lg - jnp.max(lg, axis=1, keepdims=True)
        p = jnp.exp(lg)
        p = p / jnp.sum(p, axis=1, keepdims=True)
        o_ref[...] = _dot(p, v_ref[...], NN).astype(o_ref.dtype)

    return pl.pallas_call(
        body, name=name, grid=(X_HEADS, s // tq),
        in_specs=[pl.BlockSpec((tq, dh), lambda h, i: (i, qcol // dh + h)), pl.BlockSpec((nm, dh), lambda h, i: (0, h)),
                  pl.BlockSpec((nm, dh), lambda h, i: (0, X_HEADS + h)),
                  pl.BlockSpec((1, dh), lambda h, i: (0, 0)), pl.BlockSpec((1, dh), lambda h, i: (0, 0))],
        out_specs=pl.BlockSpec((tq, dh), lambda h, i: (i, h)),
        out_shape=pltpu.HBM((s, d_model), BF16),
        compiler_params=_params(32 << 20),
    )(_hbm(zm), _hbm(kv), _hbm(kv), gq, gk)


def _xa_bwd(zm, qcol, kv, gq, gk, dy, dz, d_model, *, name, tq=512):
    s = zm.shape[0]
    nm = kv.shape[0]
    dh = d_model // X_HEADS
    tq = _pick(s, (tq, 128, 64))
    nq = s // tq
    scale = dh ** -0.5

    def body(q_ref, k_ref, v_ref, gq_ref, gk_ref, do_ref, dz_ref, dq_ref, dkn_ref, dv_ref, dgq_ref):
        h, i = pl.program_id(0), pl.program_id(1)

        @pl.when(i == 0)
        def _():
            dkn_ref[...] = jnp.zeros_like(dkn_ref)
            dv_ref[...] = jnp.zeros_like(dv_ref)

        @pl.when((i == 0) & (h == 0))
        def _():
            dgq_ref[...] = jnp.zeros_like(dgq_ref)

        q = q_ref[...]
        qn = _rms_fwd(q, gq_ref[...])
        kn = _rms_fwd(k_ref[...], gk_ref[...])
        lg = _dot(qn, kn, NT) * scale
        lg = lg - jnp.max(lg, axis=1, keepdims=True)
        p = jnp.exp(lg)
        p = p / jnp.sum(p, axis=1, keepdims=True)
        do = do_ref[...]
        dv_ref[...] += _dot(p, do, TN)
        dp = _dot(do, v_ref[...], NT)
        dlg = p * (dp - jnp.sum(dp * p, axis=1, keepdims=True)) * scale
        dqn = _dot(dlg, kn, NN)
        dkn_ref[...] += _dot(dlg, qn, TN)
        dq, dgq = _rms_bwd(q, gq_ref[...], dqn)
        dq_ref[...] = dq.astype(dq_ref.dtype)
        dgq_ref[...] += jnp.sum(dgq, axis=0, keepdims=True)

    return pl.pallas_call(
        body, name=name, grid=(X_HEADS, nq),
        in_specs=[pl.BlockSpec((tq, dh), lambda h, i: (i, qcol // dh + h)), pl.BlockSpec((nm, dh), lambda h, i: (0, h)),
                  pl.BlockSpec((nm, dh), lambda h, i: (0, X_HEADS + h)),
                  pl.BlockSpec((1, dh), lambda h, i: (0, 0)), pl.BlockSpec((1, dh), lambda h, i: (0, 0)),
                  pl.BlockSpec((tq, dh), lambda h, i: (i, h)), ANY],
        out_specs=[pl.BlockSpec((tq, dh), lambda h, i: (i, qcol // dh + h)),
                   pl.BlockSpec((nm, dh), lambda h, i: (0, h)),
                   pl.BlockSpec((nm, dh), lambda h, i: (0, h)), pl.BlockSpec((1, dh), lambda h, i: (0, 0))],
        out_shape=[pltpu.HBM(dz.shape, dz.dtype), pltpu.HBM((nm, d_model), F32),
                   pltpu.HBM((nm, d_model), F32), pltpu.HBM((1, dh), F32)],
        input_output_aliases={6: 0},
        compiler_params=_params(32 << 20),
    )(_hbm(zm), _hbm(kv), _hbm(kv), gq, gk, _hbm(dy), dz)


def _place():
    return lax.axis_index("x"), lax.axis_index("y"), lax.axis_index("c")


ANY = pl.BlockSpec(memory_space=pl.ANY)


def _allgather_two_level(big, small, *, name, chunk_rows=64):
    r, cc = big.shape
    half = r // 2
    nr = _pick(half, (chunk_rows, 32, 16))
    nq = half // nr

    def body(big_ref, small_ref, obig, osmall, land, passed, send, recv, fsend, frecv, out_a, out_b, ssend, srecv, loc):
        x, y, c = _place()
        k = 2 * x + y
        chips = [(1 - x, y), (x, 1 - y), (1 - x, 1 - y)]
        slots = [2 * px + py for px, py in chips]
        local = [pltpu.make_async_copy(big_ref, obig.at[k], loc.at[0]),
                 pltpu.make_async_copy(small_ref, osmall.at[k], loc.at[1])]
        for cp in local:
            cp.start()

        def rows(h, q):
            return pl.ds(pl.multiple_of(h * half + q * nr, nr), nr)

        def chunk(q):
            return pl.ds(q * nr, nr)

        def over_ici(j, q):
            return pltpu.make_async_remote_copy(
                src_ref=big_ref.at[rows(c, q)], dst_ref=land.at[j, chunk(q)], send_sem=send.at[nq * j + q],
                recv_sem=recv.at[nq * j + q], device_id=(chips[j][0], chips[j][1], c), device_id_type=MESH)

        def to_sibling(j, q):
            return pltpu.make_async_remote_copy(
                src_ref=land.at[j, chunk(q)], dst_ref=passed.at[j, chunk(q)], send_sem=fsend.at[nq * j + q],
                recv_sem=frecv.at[nq * j + q], device_id=(x, y, 1 - c), device_id_type=MESH)

        def small_copy(j, slot):
            return pltpu.make_async_remote_copy(
                src_ref=small_ref, dst_ref=osmall.at[slot], send_sem=ssend.at[j], recv_sem=srecv.at[j],
                device_id=(chips[j][0], chips[j][1], c), device_id_type=MESH)

        for q in range(nq):
            for j in range(3):
                over_ici(j, q).start()
        for j in range(3):
            small_copy(j, k).start()
        for q in range(nq):
            for j in range(3):
                over_ici(j, q).wait_recv()
                to_sibling(j, q).start()
                cp = pltpu.make_async_copy(land.at[j, chunk(q)], obig.at[slots[j], rows(c, q)], out_a.at[nq * j + q])
                cp.start()
                local.append(cp)
        for q in range(nq):
            for j in range(3):
                to_sibling(j, q).wait_recv()
                cp = pltpu.make_async_copy(passed.at[j, chunk(q)], obig.at[slots[j], rows(1 - c, q)],
                                           out_b.at[nq * j + q])
                cp.start()
                local.append(cp)
        for j in range(3):
            small_copy(j, slots[j]).wait_recv()
            small_copy(j, k).wait_send()
        for q in range(nq):
            for j in range(3):
                over_ici(j, q).wait_send()
                to_sibling(j, q).wait_send()
        for cp in local:
            cp.wait()

    stage = 2 * _nbytes((3, half, cc), big.dtype)
    return pl.pallas_call(
        body, name=name, in_specs=[ANY] * 2, out_specs=[ANY] * 2,
        out_shape=[pltpu.HBM((4,) + big.shape, big.dtype), pltpu.HBM((4,) + small.shape, small.dtype)],
        scratch_shapes=[pltpu.VMEM((3, half, cc), big.dtype), pltpu.VMEM((3, half, cc), big.dtype)]
        + [pltpu.SemaphoreType.DMA((3 * nq,))] * 6
        + [pltpu.SemaphoreType.DMA((3,)), pltpu.SemaphoreType.DMA((3,)), pltpu.SemaphoreType.DMA((2,))],
        compiler_params=_params(stage + stage // 8 + (4 << 20)),
    )(big, small)


HBM_SPEC = pl.BlockSpec(memory_space=pltpu.HBM)
SEM_SPEC = pl.BlockSpec(memory_space=pltpu.SEMAPHORE)
EFFECT = pltpu.SideEffectType.DATAFLOW_SIDE_EFFECTING


def _split_copies(kind, srcs, lands, send, recv):
    x, y, c = _place()
    if kind == "quarters":
        peers = [(1 - x, y, c), (x, 1 - y, c), (1 - x, 1 - y, c)]
    else:
        peers = [(x ^ ((j >> 2) & 1), y ^ ((j >> 1) & 1), c ^ (j & 1)) for j in range(1, 8)]
    npeer = len(peers)
    out = []
    for t in range(len(srcs)):
        for j, (px, py, pc) in enumerate(peers):
            if kind == "quarters":
                src, mine, theirs = srcs[t], 2 * x + y, 2 * px + py
            else:
                src, mine, theirs = srcs[t].at[2 * px + py, pc], 4 * x + 2 * y + c, 4 * px + 2 * py + pc
            mk = functools.partial(
                pltpu.make_async_remote_copy, src_ref=src, send_sem=send.at[npeer * t + j],
                recv_sem=recv.at[npeer * t + j], device_id=(px, py, pc), device_id_type=MESH)
            out.append((functools.partial(mk, dst_ref=lands[t].at[mine]),
                        functools.partial(mk, dst_ref=lands[t].at[theirs])))
    return out


def _split_start(kind, srcs, land_shapes, after, *, name):
    n = len(srcs)
    ncopies = n * (3 if kind == "quarters" else 7)

    def body(*refs):
        ins, lands = refs[:n], refs[n:2 * n]
        send, recv = refs[2 * n + 1], refs[2 * n + 2]
        token = refs[-1]
        for start, _ in _split_copies(kind, ins, lands, send, recv):
            start().start()
        token[...] = jnp.zeros_like(token)

    lands = [_hbm(lax.empty(shp, a.dtype)) for shp, a in zip(land_shapes, srcs)]
    res = pl.pallas_call(
        body, name=name, in_specs=[HBM_SPEC] * (2 * n) + [ANY],
        out_specs=[SEM_SPEC, SEM_SPEC] + [HBM_SPEC] * (2 * n) + [pl.BlockSpec(memory_space=pltpu.VMEM)],
        out_shape=[pltpu.SemaphoreType.DMA((ncopies,)), pltpu.SemaphoreType.DMA((ncopies,))]
        + [pltpu.HBM(a.shape, a.dtype) for a in srcs] + [pltpu.HBM(shp, a.dtype) for shp, a in zip(land_shapes, srcs)]
        + [jax.ShapeDtypeStruct((8, LANES), F32)],
        input_output_aliases={i: 2 + i for i in range(2 * n)},
        compiler_params=pltpu.CompilerParams(has_side_effects=EFFECT),
    )(*[_hbm(a) for a in srcs], *lands, after)
    return res[0], res[1], list(res[2:2 + n]), list(res[2 + n:2 + 2 * n]), res[-1]


def _split_wait(kind, send, recv, srcs, lands, after, *, name):
    n = len(srcs)

    def body(*refs):
        ins, lnd = refs[:n], refs[n:2 * n]
        snd, rcv = refs[2 * n], refs[2 * n + 1]
        for start, arrive in _split_copies(kind, ins, lnd, snd, rcv):
            start().wait_send()
            arrive().wait_recv()

    res = pl.pallas_call(
        body, name=name, in_specs=[HBM_SPEC] * (2 * n) + [SEM_SPEC, SEM_SPEC] + [ANY] * len(after),
        out_specs=[HBM_SPEC] * (2 * n),
        out_shape=[pltpu.HBM(a.shape, a.dtype) for a in srcs] + [pltpu.HBM(a.shape, a.dtype) for a in lands],
        input_output_aliases={i: i for i in range(2 * n)},
        compiler_params=pltpu.CompilerParams(has_side_effects=EFFECT),
    )(*srcs, *lands, send, recv, *after)
    return list(res[n:])


def _sum8(parts, *, name):
    _, r, c = parts.shape
    t = _pick(r, (128, 64, 32, 16, 8))

    def body(p_ref, o_ref):
        acc = p_ref[0].astype(F32)
        for k in range(1, 8):
            acc = acc + p_ref[k].astype(F32)
        o_ref[...] = acc

    return pl.pallas_call(
        body, name=name, grid=(r // t,), in_specs=[pl.BlockSpec((8, t, c), lambda i: (0, i, 0))],
        out_specs=pl.BlockSpec((t, c), lambda i: (i, 0)), out_shape=pltpu.HBM((r, c), F32),
        compiler_params=_params(2 * 8 * t * c * 2 + 6 * t * c * 4 + (4 << 20)),
    )(_hbm(parts))


def _swap_halves(halves, *, name, chunk_bytes=512 * 1024):
    n = len(halves)
    items = []
    for t, a in enumerate(halves):
        r = a.shape[0]
        k = 1
        while _nbytes(a.shape, a.dtype) // k > chunk_bytes and r % (2 * k) == 0 and (r // (2 * k)) % 8 == 0:
            k *= 2
        items += [(t, q * (r // k), r // k) for q in range(k)]
    m = len(items)

    def body(*refs):
        ins, outs = refs[:n], refs[n:2 * n]
        sbuf, rbuf = refs[2 * n:3 * n], refs[3 * n:4 * n]
        send, recv, loc_own, loc_in, loc_out = refs[4 * n:]
        x, y, c = _place()
        local, stage = [], []
        for t in range(n):
            cp = pltpu.make_async_copy(ins[t], outs[t].at[c], loc_own.at[t])
            cp.start()
            local.append(cp)
        for q, (t, r0, nr) in enumerate(items):
            cp = pltpu.make_async_copy(ins[t].at[pl.ds(r0, nr)], sbuf[t].at[pl.ds(r0, nr)], loc_in.at[q])
            cp.start()
            stage.append(cp)

        def copy(q):
            t, r0, nr = items[q]
            return pltpu.make_async_remote_copy(
                src_ref=sbuf[t].at[pl.ds(r0, nr)], dst_ref=rbuf[t].at[pl.ds(r0, nr)], send_sem=send.at[q],
                recv_sem=recv.at[q], device_id=(x, y, 1 - c), device_id_type=MESH)

        for q in range(m):
            stage[q].wait()
            copy(q).start()
        for q, (t, r0, nr) in enumerate(items):
            copy(q).wait_recv()
            cp = pltpu.make_async_copy(rbuf[t].at[pl.ds(r0, nr)], outs[t].at[1 - c, pl.ds(r0, nr)], loc_out.at[q])
            cp.start()
            local.append(cp)
        for q in range(m):
            copy(q).wait_send()
        for cp in local:
            cp.wait()

    stage_bytes = 2 * sum(_nbytes(a.shape, a.dtype) for a in halves)
    return pl.pallas_call(
        body, name=name, in_specs=[ANY] * n, out_specs=[ANY] * n,
        out_shape=[pltpu.HBM((2,) + a.shape, a.dtype) for a in halves],
        scratch_shapes=[pltpu.VMEM(a.shape, a.dtype) for a in halves] * 2
        + [pltpu.SemaphoreType.DMA((m,)), pltpu.SemaphoreType.DMA((m,)), pltpu.SemaphoreType.DMA((n,)),
           pltpu.SemaphoreType.DMA((m,)), pltpu.SemaphoreType.DMA((m,))],
        compiler_params=_params(stage_bytes + (4 << 20)),
    )(*halves)


def _allreduce_small(p, after, *, name):
    r = p.shape[0]

    def body(p_ref, after_ref, o_ref, buf, send, recv):
        x, y, c = _place()
        me = 4 * x + 2 * y + c
        peers = [(x ^ ((j >> 2) & 1), y ^ ((j >> 1) & 1), c ^ (j & 1)) for j in range(1, 8)]

        def copy(j, slot):
            return pltpu.make_async_remote_copy(
                src_ref=p_ref, dst_ref=buf.at[slot], send_sem=send.at[j], recv_sem=recv.at[j],
                device_id=peers[j], device_id_type=MESH)

        for j in range(7):
            copy(j, me).start()
        buf[me] = p_ref[...]
        for j in range(7):
            px, py, pc = peers[j]
            copy(j, 4 * px + 2 * py + pc).wait_recv()
        for j in range(7):
            copy(j, me).wait_send()
        acc = buf[0]
        for k in range(1, 8):
            acc = acc + buf[k]
        o_ref[...] = acc

    vspec = pl.BlockSpec(memory_space=pltpu.VMEM)
    return pl.pallas_call(
        body, name=name, in_specs=[vspec, ANY], out_specs=vspec, out_shape=jax.ShapeDtypeStruct((r, LANES), F32),
        scratch_shapes=[pltpu.VMEM((8, r, LANES), F32), pltpu.SemaphoreType.DMA((7,)), pltpu.SemaphoreType.DMA((7,))],
    )(p, after)


def _adamw_fn(w, g, m, v):
    m = ADAM_B1 * m + (1.0 - ADAM_B1) * g
    v = ADAM_B2 * v + (1.0 - ADAM_B2) * (g * g)
    m_hat = m / (1.0 - ADAM_B1 ** ADAM_STEP)
    v_hat = v / (1.0 - ADAM_B2 ** ADAM_STEP)
    delta = -ADAM_LR * (m_hat / (jnp.sqrt(v_hat) + ADAM_EPS) + ADAM_WD * w)
    return delta, m, v


def _adamw(w, g, m, v, *, name):
    c = w.shape[1]
    return _rowwise(_adamw_fn, [w, g, m, v], [], [(c, F32)] * 3, name=name, tr=128)


def _pack(vecs, rows):
    flat = jnp.concatenate([a.reshape(-1).astype(F32) for a in vecs])
    return jnp.pad(flat, (0, rows * LANES - flat.shape[0])).reshape(rows, LANES)


def _unpack(p, like):
    flat, out, o = p.reshape(-1), [], 0
    for a in like:
        out.append(flat[o:o + a.size].reshape(a.shape))
        o += a.size
    return out


def kernel(x, mem, g_mix, w_in, b_if, b_gate, conv_w, conv_b, ml_norm_g, g_mem, w_mem_kv, q_norm_g, k_norm_g, w_sb_proj, w_ml_proj, w_x_proj, w_out, g_mlp, w_ff1, w_ff2, loss_target, m_g_mix, m_w_in, m_b_if, m_b_gate, m_conv_w, m_conv_b, m_ml_norm_g, m_g_mem, m_w_mem_kv, m_q_norm_g, m_k_norm_g, m_w_sb_proj, m_w_ml_proj, m_w_x_proj, m_w_out, m_g_mlp, m_w_ff1, m_w_ff2, v_g_mix, v_w_in, v_b_if, v_b_gate, v_conv_w, v_conv_b, v_ml_norm_g, v_g_mem, v_w_mem_kv, v_q_norm_g, v_k_norm_g, v_w_sb_proj, v_w_ml_proj, v_w_x_proj, v_w_out, v_g_mlp, v_w_ff1, v_w_ff2):
    _, s, d = x.shape
    nm = mem.shape[1]
    n_in = 4 * w_in.shape[2]
    dff = 4 * w_ff1.shape[2]
    sbh = d // SB_HD
    hh = ML_HEADS
    dh = d // hh
    nc = s // CHUNK
    assert n_in == 11 * d + 2 * hh and d % (2 * LANES) == 0 and s % LANES == 0
    x2, mem2, tgt = x[0], mem[0], loss_target[0]

    k4 = 2 * lax.axis_index("x") + lax.axis_index("y")
    me = 2 * k4 + lax.axis_index("c")
    g_first = _allgather_two_level(w_in[0].astype(BF16), conv_w[0], name="gather_w_in")
    later = [a[0].astype(BF16) for a in (w_mem_kv, w_sb_proj, w_ml_proj, w_x_proj, w_out, w_ff1, w_ff2)]
    gw_send, gw_recv, gw_src, gw_land, gw_token = _split_start(
        "quarters", later, [(4,) + a.shape for a in later], g_first[0], name="gather_rest_start")
    cols = lambda a: a.transpose(1, 0, 2).reshape(a.shape[1], 4 * a.shape[2])
    rws = lambda a: a.reshape(4 * a.shape[1], a.shape[2])
    qn = n_in // 4
    if_lo, if_hi = 7 * d, 7 * d + 2 * hh

    def cut(lo, hi):
        ks = [(k, max(lo, k * qn), min(hi, (k + 1) * qn)) for k in range(4)]
        return [g_first[0][k, :, a - k * qn:b - k * qn] for k, a, b in ks if a < b]

    w_main = jnp.concatenate(cut(0, if_lo) + cut(if_hi, n_in), axis=1)
    w_if = jnp.pad(jnp.concatenate(cut(if_lo, if_hi), axis=1), ((0, 0), (0, LANES - 2 * hh)))
    conv_wf = cols(g_first[1])
    b_if_p = jnp.pad(b_if, ((0, 0), (0, LANES - 2 * hh)))

    (hn,) = _rowwise(_rms_fwd, [x2], [g_mix], [(d, BF16)], name="norm_in", tr=512)
    zm = _mm(hn, w_main, after=gw_token, name="proj_in")
    zif = _mm(hn, w_if, name="proj_if")
    y_sb, a_sb = _sb_fwd(zm, sbh, name="sb_fwd")

    def gate_fn(z, b):
        pre = z + b
        lane = lax.broadcasted_iota(jnp.int32, pre.shape, 1)
        return jnp.where(lane < hh, pre, -_softplus(-pre))

    (gcol,) = _rowwise(gate_fn, [zif], [b_if_p], [(LANES, F32)], name="ml_gates", tr=1024)
    grow = gcol[:, :8].T.reshape(8, nc, CHUNK).transpose(1, 0, 2)
    mqk = _conv_fwd(zm, 3 * d, 2 * d, conv_wf, conv_b, name="conv_fwd")
    hm, cst, nst, mst = _ml_fwd(mqk, zm, 5 * d, gcol, grow, d, name="ml_fwd")

    def mlout_fn(hv, o, g):
        ys = [_rms_fwd(hv[:, k * dh:(k + 1) * dh], g[:, k * dh:(k + 1) * dh]) for k in range(hh)]
        return jnp.concatenate(ys, axis=1) * _sigmoid(o)

    (y_ml,) = _rowwise(mlout_fn, [hm, (zm, d, 6)], [ml_norm_g], [(d, BF16)], name="ml_out", tr=512)
    gw_land = _split_wait("quarters", gw_send, gw_recv, gw_src, gw_land, [y_ml, y_sb], name="gather_rest_wait")
    gw = [lax.dynamic_update_index_in_dim(ld, a, k4, 0) for ld, a in zip(gw_land, later)]
    w_kv, w_sbp, w_mlp, w_xp, w_o, w_f1, w_f2 = (cols(gw[0]), rws(gw[1]), rws(gw[2]), rws(gw[3]), rws(gw[4]),
                                                 cols(gw[5]), rws(gw[6]))
    (memn,) = _rowwise(_rms_fwd, [mem2], [g_mem], [(d, BF16)], name="norm_mem")
    kv = _mm(memn, w_kv, name="proj_kv")
    y_x = _xa_fwd(zm, 7 * d, kv, q_norm_g, k_norm_g, d, name="xa_fwd")
    p_sb = _mm(y_sb, w_sbp, name="proj_sb")
    p_ml = _mm(y_ml, w_mlp, name="proj_ml")
    p_x = _mm(y_x, w_xp, name="proj_x")

    def merge_fn(a, b, c, g0, g1, g2, bg):
        return (_sigmoid(g0 + bg[:, :d]) * a + _sigmoid(g1 + bg[:, d:2 * d]) * b + _sigmoid(g2 + bg[:, 2 * d:]) * c)

    gate_cols = [(zm, d, 8), (zm, d, 9), (zm, d, 10)]
    (mixed,) = _rowwise(merge_fn, [p_sb, p_ml, p_x] + gate_cols, [b_gate], [(d, BF16)], name="merge")
    x1 = _mm(mixed, w_o, tiles=[x2], name="proj_out")
    (h2,) = _rowwise(_rms_fwd, [x1], [g_mlp], [(d, BF16)], name="norm_mlp", tr=512)
    u, act = _mm(h2, w_f1, post=lambda r: (r, jnp.square(jnp.maximum(r, 0.0))), out_dtype=(F32, BF16), name="ff1")
    dy = _mm(act, w_f2, tiles=[x1, tgt], post=lambda r, xv, tv: (r + xv - tv) * (1.0 / d), name="ff2")
    (loss_cols,) = _rowwise(lambda g: (jnp.sum(g * g, axis=0, keepdims=True) * (0.5 * d),), [dy], [], [], [d],
                            name="loss", tr=1024)

    du = _mm(dy, w_f2, tb=True, tiles=[u], post=lambda r, uv: r * 2.0 * jnp.maximum(uv, 0.0), out_dtype=BF16,
             name="ff2_dx")
    dw_f2 = _mm(act, dy, ta=True, name="ff2_dw")
    dw_f1 = _mm(h2, du, ta=True, name="ff1_dw")
    dh2 = _mm(du, w_f1, tb=True, name="ff1_dx")

    def norm_bwd_fn(xv, dyv, res, g):
        dx, dg = _rms_bwd(xv, g, dyv)
        return dx + res, jnp.sum(dg, axis=0, keepdims=True)

    dx1, dg_mlp = _rowwise(norm_bwd_fn, [x1, dh2, dy], [g_mlp], [(d, F32)], [d], name="norm_mlp_bwd", tr=512)
    dmixed = _mm(dx1, w_o, tb=True, name="proj_out_dx")
    dw_o = _mm(mixed, dx1, ta=True, name="proj_out_dw")

    def merge_bwd_fn(dm, a, b, c, g0, g1, g2, bg):
        outs, dgs = [], []
        for p, g, k in ((a, g0, 0), (b, g1, 1), (c, g2, 2)):
            sg = _sigmoid(g + bg[:, k * d:(k + 1) * d])
            outs.append(dm * sg)
            dgs.append(dm * p * sg * (1.0 - sg))
        dgate = jnp.concatenate(dgs, axis=1)
        return (*outs, dgate, jnp.sum(dgate, axis=0, keepdims=True))

    dp_sb, dp_ml, dp_x, dgate, db_gate = _rowwise(
        merge_bwd_fn, [dmixed, p_sb, p_ml, p_x] + gate_cols, [b_gate], [(d, BF16)] * 3 + [(3 * d, BF16)], [3 * d],
        name="merge_bwd", tr=256)
    dw_sbp = _mm(y_sb, dp_sb, ta=True, name="proj_sb_dw")
    dw_mlp = _mm(y_ml, dp_ml, ta=True, name="proj_ml_dw")
    dw_xp = _mm(y_x, dp_x, ta=True, name="proj_x_dw")
    dy_sb = _mm(dp_sb, w_sbp, tb=True, out_dtype=BF16, name="proj_sb_dx")
    dy_ml = _mm(dp_ml, w_mlp, tb=True, name="proj_ml_dx")
    dy_x = _mm(dp_x, w_xp, tb=True, out_dtype=BF16, name="proj_x_dx")

    dzm = _hbm(lax.empty((s, 11 * d), BF16))
    dzm, dkn, dxv, dg_qn = _xa_bwd(zm, 7 * d, kv, q_norm_g, k_norm_g, dy_x, dzm, d, name="xa_bwd")

    def knorm_bwd_fn(kvv, dknv, dvv, g):
        dks, dgs = [], []
        for k in range(X_HEADS):
            sl = slice(k * dh, (k + 1) * dh)
            dk, dg = _rms_bwd(kvv[:, sl], g, dknv[:, sl])
            dks.append(dk)
            dgs.append(jnp.sum(dg, axis=0, keepdims=True))
        return jnp.concatenate(dks + [dvv], axis=1), dgs[0] + dgs[1] + dgs[2] + dgs[3]

    dkv, dg_kn = _rowwise(knorm_bwd_fn, [(kv, d, 0), dkn, dxv], [k_norm_g], [(2 * d, BF16)], [dh], name="xa_knorm_bwd")
    dw_kv = _mm(memn, dkv, ta=True, name="proj_kv_dw")
    dmemn = _mm(dkv, w_kv, tb=True, name="proj_kv_dx")

    def gmem_fn(mv, dv_, g):
        _, dg = _rms_bwd(mv, g, dv_)
        return (jnp.sum(dg, axis=0, keepdims=True),)

    (dg_mem,) = _rowwise(gmem_fn, [mem2, dmemn], [g_mem], [], [d], name="norm_mem_bwd")

    uncols = lambda a: a.reshape(a.shape[0], 4, a.shape[1] // 4).transpose(1, 0, 2)
    unrws = lambda a: a.reshape(4, a.shape[0] // 4, a.shape[1])
    to_parts = lambda q: q.astype(BF16).reshape(4, 2, q.shape[1] // 2, q.shape[2])
    early = [to_parts(q) for q in (uncols(dw_kv), unrws(dw_sbp), unrws(dw_mlp), unrws(dw_xp), unrws(dw_o),
                                   uncols(dw_f1), unrws(dw_f2))]
    ge_send, ge_recv, ge_src, ge_land, ge_token = _split_start(
        "grads", early, [(8,) + a.shape[2:] for a in early], dg_mem, name="exchange_early_start")

    def mlout_bwd_fn(dyv, hv, o, g):
        sg = _sigmoid(o)
        dn = dyv * sg
        dxs, dgs, ys = [], [], []
        for k in range(hh):
            sl = slice(k * dh, (k + 1) * dh)
            ys.append(_rms_fwd(hv[:, sl], g[:, sl]))
            dxk, dgk = _rms_bwd(hv[:, sl], g[:, sl], dn[:, sl])
            dxs.append(dxk)
            dgs.append(dgk)
        do = dyv * jnp.concatenate(ys, axis=1) * sg * (1.0 - sg)
        return jnp.concatenate(dxs, axis=1), do, jnp.sum(jnp.concatenate(dgs, axis=1), axis=0, keepdims=True)

    dhm, dzm, dg_mln = _rowwise(mlout_bwd_fn, [dy_ml, hm, (zm, d, 6)], [ml_norm_g], [(d, F32), (d, BF16)], [d],
                                name="ml_out_bwd", tr=512, into=(dzm, 1, 6))
    dmqk, dzm, dgc, dgr = _ml_bwd(mqk, zm, 5 * d, gcol, grow, cst, nst, mst, dhm, dzm, d, name="ml_bwd")
    dzm, dconv_w, dconv_b = _conv_bwd(zm, 3 * d, 2 * d, conv_wf, conv_b, dmqk, dzm, name="conv_bwd")
    dzm, dsk, dsv = _sb_bwd(zm, dy_sb, a_sb, dzm, ge_token, sbh, name="sb_bwd")
    dgr_t = jnp.pad(dgr.transpose(1, 0, 2).reshape(8, s).T, ((0, 0), (0, LANES - 8)))

    def gate_bwd_fn(a, b, z, bias):
        tot = a + b
        rows_t = tot.shape[0]
        r = lax.broadcasted_iota(jnp.int32, (rows_t, rows_t), 0)
        c = lax.broadcasted_iota(jnp.int32, (rows_t, rows_t), 1)
        sh = CHUNK.bit_length() - 1
        same_chunk = jnp.right_shift(r, sh) == jnp.right_shift(c, sh)
        dlf = _u01dot(((c >= r) & same_chunk).astype(BF16), tot)
        lane = lax.broadcasted_iota(jnp.int32, tot.shape, 1)
        dz = jnp.where(lane < hh, tot, jnp.where(lane < 2 * hh, dlf * _sigmoid(-(z + bias)), 0.0))
        return dz, jnp.sum(dz, axis=0, keepdims=True)

    dzif, db_if_p = _rowwise(gate_bwd_fn, [dgc, dgr_t, zif], [b_if_p], [(LANES, BF16)], [LANES], name="ml_gates_bwd",
                             tr=8 * CHUNK)
    for part, col in ((dsk, d), (dsv, 2 * d), (dgate, 8 * d)):
        dzm = lax.dynamic_update_slice(dzm, part, (0, col))
    dw_main = _mm(hn, dzm, ta=True, out_dtype=BF16, name="proj_in_dw")
    dw_if = _mm(hn, dzif, ta=True, out_dtype=BF16, name="proj_if_dw")

    def dw_quarter(k):
        lo, hi = k * qn, (k + 1) * qn
        segs = [(dw_main, 0, if_lo, 0), (dw_if, if_lo, if_hi, if_lo), (dw_main, if_hi, n_in, 2 * hh)]
        got = [src[:, max(lo, a) - off:min(hi, b) - off] for src, a, b, off in segs if max(lo, a) < min(hi, b)]
        return jnp.concatenate(got, axis=1)

    late = [to_parts(jnp.stack([dw_quarter(k) for k in range(4)]))]
    gl_send, gl_recv, gl_src, gl_land, gl_token = _split_start(
        "grads", late, [(8,) + a.shape[2:] for a in late], dw_if, name="exchange_late_start")
    dhn = _mm(dzm, w_main, tb=True, after=gl_token, name="proj_in_dx")
    dhn = _mm(dzif, w_if, tb=True, tiles=[dhn], name="proj_if_dx")
    dx, dg_mix = _rowwise(norm_bwd_fn, [x2, dhn, dx1], [g_mix], [(d, F32)], [d], name="norm_in_bwd", tr=512)

    own = lambda p: lax.dynamic_index_in_dim(lax.dynamic_index_in_dim(p, k4, 0, keepdims=False),
                                             lax.axis_index("c"), 0, keepdims=False)

    def finish(tag, send, recv, src, land, parts, after, ws, ms, vs):
        land = _split_wait("grads", send, recv, src, land, after, name=f"exchange_{tag}_wait")
        got = [lax.dynamic_update_index_in_dim(ld, own(p), me, 0) for ld, p in zip(land, parts)]
        halves = [_sum8(r, name=f"sum_grads_{tag}{i}") for i, r in enumerate(got)]
        both = _swap_halves(halves, name=f"swap_halves_{tag}")
        gs = [b.reshape(2 * b.shape[1], b.shape[2]) for b in both]
        return gs, [_adamw(w, g, m, v, name=f"adamw_{tag}{i}") for i, (w, g, m, v) in enumerate(zip(ws, gs, ms, vs))]

    first = lambda arrs: [a[0] for a in arrs]
    g_early, out_early = finish(
        "early", ge_send, ge_recv, ge_src, ge_land, early, [dx],
        first([w_mem_kv, w_sb_proj, w_ml_proj, w_x_proj, w_out, w_ff1, w_ff2]),
        first([m_w_mem_kv, m_w_sb_proj, m_w_ml_proj, m_w_x_proj, m_w_out, m_w_ff1, m_w_ff2]),
        first([v_w_mem_kv, v_w_sb_proj, v_w_ml_proj, v_w_x_proj, v_w_out, v_w_ff1, v_w_ff2]))
    g_late, out_late = finish(
        "late", gl_send, gl_recv, gl_src, gl_land, late, [o[0] for o in out_early],
        first([w_in]), first([m_w_in]), first([v_w_in]))
    g_big = [g[None] for g in g_late + g_early]
    big_out = [[o[None] for o in outs] for outs in out_late + out_early]

    small_g = [dg_mix, db_if_p[:, :2 * hh], db_gate, dconv_w, dconv_b, dg_mln, dg_mem, dg_qn, dg_kn, dg_mlp,
               jnp.sum(loss_cols).reshape(1, 1)]
    n_small = sum(a.size for a in small_g)
    rows = -(-n_small // (8 * LANES)) * 8
    g_small = _unpack(_allreduce_small(_pack(small_g, rows), out_late[0][0], name="allreduce_small"), small_g)
    loss = g_small[-1].reshape(())
    qw = conv_w.shape[2]
    g_conv_w = lax.dynamic_slice_in_dim(g_small[3], k4 * qw, qw, axis=1)
    g_small_w = [g_small[0], g_small[1], g_small[2], g_conv_w] + g_small[4:10]
    sm_w = [g_mix, b_if, b_gate, conv_w[0], conv_b, ml_norm_g, g_mem, q_norm_g, k_norm_g, g_mlp]
    sm_m = [m_g_mix, m_b_if, m_b_gate, m_conv_w[0], m_conv_b, m_ml_norm_g, m_g_mem, m_q_norm_g, m_k_norm_g, m_g_mlp]
    sm_v = [v_g_mix, v_b_if, v_b_gate, v_conv_w[0], v_conv_b, v_ml_norm_g, v_g_mem, v_q_norm_g, v_k_norm_g, v_g_mlp]
    n_sw = sum(a.size for a in sm_w)
    rows_w = -(-n_sw // (8 * LANES)) * 8
    sm_out = _adamw(_pack(sm_w, rows_w), _pack(g_small_w, rows_w), _pack(sm_m, rows_w), _pack(sm_v, rows_w),
                    name="adamw_small")
    sm_delta, sm_newm, sm_newv = [_unpack(p, sm_w) for p in sm_out]

    order = ["g_mix", "w_in", "b_if", "b_gate", "conv_w", "conv_b", "ml_norm_g", "g_mem", "w_mem_kv", "q_norm_g",
             "k_norm_g", "w_sb_proj", "w_ml_proj", "w_x_proj", "w_out", "g_mlp", "w_ff1", "w_ff2"]
    small_names = ["g_mix", "b_if", "b_gate", "conv_w", "conv_b", "ml_norm_g", "g_mem", "q_norm_g", "k_norm_g", "g_mlp"]
    big_names = ["w_in", "w_mem_kv", "w_sb_proj", "w_ml_proj", "w_x_proj", "w_out", "w_ff1", "w_ff2"]
    grads, deltas, new_m, new_v = {}, {}, {}, {}
    for i, nme in enumerate(small_names):
        shp = sm_w[i].shape if nme != "conv_w" else conv_w.shape
        grads[nme] = g_small_w[i].reshape(shp)
        deltas[nme], new_m[nme], new_v[nme] = (sm_delta[i].reshape(shp), sm_newm[i].reshape(shp),
                                               sm_newv[i].reshape(shp))
    for i, nme in enumerate(big_names):
        grads[nme] = g_big[i]
        deltas[nme], new_m[nme], new_v[nme] = big_out[i]
    return (loss, dx[None], *[grads[k] for k in order], *[deltas[k] for k in order], *[new_m[k] for k in order],
            *[new_v[k] for k in order])
```

```python
import functools

import jax
import jax.numpy as jnp
from jax import lax
from jax.experimental import pallas as pl
from jax.experimental.pallas import tpu as pltpu

F32 = jnp.float32
BF16 = jnp.bfloat16
MESH = pl.DeviceIdType.MESH

EPS = 1e-6
SB_HD = 128
SB_SLOTS = 8
ML_HEADS = 4
X_HEADS = 4
CHUNK = 64
CONV_W = 4
LANES = 128
ADAM_LR = 0.001
ADAM_B1 = 0.9
ADAM_B2 = 0.999
ADAM_EPS = 1e-08
ADAM_WD = 0.01
ADAM_STEP = 10
VMEM_CAP = 56 * 1024 * 1024
NEG = -1e30

NT = (((1,), (1,)), ((), ()))
NN = (((1,), (0,)), ((), ()))
TN = (((0,), (0,)), ((), ()))


def _dot(a, b, dn=NN):
    return lax.dot_general(a.astype(BF16), b.astype(BF16), dn, preferred_element_type=F32)


def _dot01(x, u, dn=NN):
    hi = x.astype(BF16)
    lo = (x - hi.astype(F32)).astype(BF16)
    return (lax.dot_general(hi, u, dn, preferred_element_type=F32)
            + lax.dot_general(lo, u, dn, preferred_element_type=F32))


def _u01dot(u, x):
    hi = x.astype(BF16)
    lo = (x - hi.astype(F32)).astype(BF16)
    return (lax.dot_general(u, hi, NN, preferred_element_type=F32)
            + lax.dot_general(u, lo, NN, preferred_element_type=F32))


def _pick(n, cands):
    for c in cands:
        if c <= n and n % c == 0:
            return c
    return n


def _nbytes(shape, dtype):
    n = 1
    for s in shape:
        n *= s
    return n * jnp.dtype(dtype).itemsize


def _params(vmem_bytes):
    return pltpu.CompilerParams(vmem_limit_bytes=int(min(VMEM_CAP, max(vmem_bytes, 16 * 1024 * 1024))))


def _hbm(a):
    return pltpu.with_memory_space_constraint(a, pltpu.HBM)


def _softplus(z):
    return jnp.maximum(z, 0.0) + jnp.log(1.0 + jnp.exp(-jnp.abs(z)))


def _sigmoid(z):
    return 1.0 / (1.0 + jnp.exp(-z))


def _rms_fwd(xv, g):
    r = lax.rsqrt(jnp.mean(xv * xv, axis=-1, keepdims=True) + EPS)
    return xv * r * g


def _rms_bwd(xv, g, dy):
    r = lax.rsqrt(jnp.mean(xv * xv, axis=-1, keepdims=True) + EPS)
    xh = xv * r
    dxh = dy * g
    dx = r * (dxh - xh * jnp.mean(dxh * xh, axis=-1, keepdims=True))
    return dx, dy * xh


def _mm(a, b, *, name, ta=False, tb=False, tiles=(), post=None, out_dtype=F32, bm=1024, bn=1024, bk=1024, after=None):
    m, k = (a.shape[1], a.shape[0]) if ta else a.shape
    n = b.shape[0] if tb else b.shape[1]
    tm = _pick(m, (bm, 512, 256, 128))
    tn = _pick(n, (bn, 512, 256, 128))
    tk = _pick(k, (bk, 512, 256, 128))
    nk = k // tk
    if (m // tm) * (n // tn) * nk < 8 and tm % 256 == 0:
        tm //= 2
    dn = (((0 if ta else 1,), (1 if tb else 0,)), ((), ()))
    dts = out_dtype if isinstance(out_dtype, tuple) else (out_dtype,)
    nt, no = len(tiles), len(dts)
    if post is None:
        post = lambda r, *ts: sum((t.astype(F32) for t in ts), r)

    def body(*refs):
        a_ref, b_ref = refs[:2]
        t_refs = refs[2:2 + nt]
        o_refs = refs[2 + nt + (after is not None):2 + nt + (after is not None) + no]
        part = lax.dot_general(a_ref[...].astype(BF16), b_ref[...].astype(BF16), dn, preferred_element_type=F32)

        def finish(r):
            res = post(r, *[t[...] for t in t_refs])
            res = res if isinstance(res, tuple) else (res,)
            for o, v in zip(o_refs, res):
                o[...] = v.astype(o.dtype)

        if nk == 1:
            finish(part)
        else:
            acc_ref = refs[-1]
            kk = pl.program_id(2)

            @pl.when(kk == 0)
            def _():
                acc_ref[...] = part

            @pl.when(kk > 0)
            def _():
                acc_ref[...] += part

            @pl.when(kk == nk - 1)
            def _():
                finish(acc_ref[...])

    a_spec = pl.BlockSpec((tk, tm), lambda i, j, q: (q, i)) if ta else pl.BlockSpec((tm, tk), lambda i, j, q: (i, q))
    b_spec = pl.BlockSpec((tn, tk), lambda i, j, q: (j, q)) if tb else pl.BlockSpec((tk, tn), lambda i, j, q: (q, j))
    o_spec = pl.BlockSpec((tm, tn), lambda i, j, q: (i, j))
    ins, specs = [_hbm(a), _hbm(b)] + [_hbm(t) for t in tiles], [a_spec, b_spec] + [o_spec] * nt
    vm = 2 * (_nbytes((tm, tk), a.dtype) + _nbytes((tk, tn), b.dtype)) + 3 * _nbytes((tm, tn), F32) \
        + _nbytes((tm, tk), BF16) + _nbytes((tk, tn), BF16) \
        + 2 * sum(_nbytes((tm, tn), t.dtype) for t in tiles) + 2 * sum(_nbytes((tm, tn), dt) for dt in dts)
    if after is not None:
        ins.append(after)
        specs.append(ANY)
    res = pl.pallas_call(
        body, name=name, grid=(m // tm, n // tn, nk), in_specs=specs, out_specs=[o_spec] * no,
        out_shape=[pltpu.HBM((m, n), dt) for dt in dts], scratch_shapes=[pltpu.VMEM((tm, tn), F32)] if nk > 1 else [],
        compiler_params=_params(vm + (4 << 20)),
    )(*ins)
    return res[0] if no == 1 else tuple(res)


def _rowwise(fn, rows, consts, outs, reds=(), *, name, tr=256, temps=6, into=None):
    rows = [r if isinstance(r, tuple) else (r, r.shape[1], 0) for r in rows]
    nrows = rows[0][0].shape[0]
    t = _pick(nrows, (tr, 128, 64, 32, 16, 8))
    nr, nc, no = len(rows), len(consts), len(outs)
    nb = 0 if into is None else 1

    def body(*refs):
        rin, cin = refs[:nr], refs[nr:nr + nc]
        oref, rref = refs[nr + nc + nb:nr + nc + nb + no], refs[nr + nc + nb + no:]
        res = fn(*[r[...] for r in rin], *[c[...] for c in cin])
        if not isinstance(res, (tuple, list)):
            res = (res,)
        for o, v in zip(oref, res[:no]):
            o[...] = v.astype(o.dtype)
        if rref:
            @pl.when(pl.program_id(0) == 0)
            def _():
                for r in rref:
                    r[...] = jnp.zeros_like(r)

            for r, v in zip(rref, res[no:]):
                r[...] += v

    in_specs = [pl.BlockSpec((t, w), functools.partial(lambda i, ci: (i, ci), ci=ci)) for (_, w, ci) in rows]
    in_specs += [pl.BlockSpec(c.shape, functools.partial(lambda i, nd: (0,) * nd, nd=c.ndim)) for c in consts]
    out_specs = [pl.BlockSpec((t, w), lambda i: (i, 0)) for (w, _) in outs]
    out_specs += [pl.BlockSpec((1, w), lambda i: (0, 0)) for w in reds]
    out_shape = [pltpu.HBM((nrows, w), dt) for (w, dt) in outs]
    out_shape += [jax.ShapeDtypeStruct((1, w), F32) for w in reds]
    widest = max([w for (_, w, _) in rows] + [w for (w, _) in outs])
    vm = 2 * sum(_nbytes((t, w), a.dtype) for (a, w, _) in rows) + 2 * sum(_nbytes((t, w), dt) for (w, dt) in outs)
    vm += temps * _nbytes((t, widest), F32) + (2 << 20)
    extra, aliases = [], {}
    if into is not None:
        buf, oi, cb = into
        out_specs[oi] = pl.BlockSpec((t, outs[oi][0]), lambda i: (i, cb))
        out_shape[oi] = pltpu.HBM(buf.shape, buf.dtype)
        in_specs.append(ANY)
        extra, aliases = [buf], {nr + nc: oi}
    res = pl.pallas_call(
        body, name=name, grid=(nrows // t,), in_specs=in_specs, out_specs=out_specs, out_shape=out_shape,
        input_output_aliases=aliases, compiler_params=_params(vm),
    )(*[_hbm(a) for (a, _, _) in rows], *consts, *extra)
    return list(res)


def _sb_tiles(s, tq, tk):
    tq = _pick(s, (tq, 256, 128))
    tk = _pick(tq, (tk, 128))
    return tq, tk, tq // tk


def _sb_fwd(zm, heads, *, name, tq=512, tk=256):
    s = zm.shape[0]
    tq, tk, nd = _sb_tiles(s, tq, tk)
    scale = SB_HD ** -0.5

    def body(q_ref, k_ref, v_ref, o_ref, a_out, stage, sem):
        h, i = pl.program_id(0), pl.program_id(1)
        qb = (q_ref[...] * scale).astype(BF16)
        r = lax.broadcasted_iota(jnp.int32, (tq, tk), 0)
        c = lax.broadcasted_iota(jnp.int32, (tq, tk), 1)
        ur = lax.broadcasted_iota(jnp.int32, (tk, tk), 0)
        uc = lax.broadcasted_iota(jnp.int32, (tk, tk), 1)
        usuf = (ur > uc).astype(BF16)

        def out_copy(slot, j):
            return pltpu.make_async_copy(stage.at[slot], a_out.at[h, i, j], sem.at[slot])

        def tile(j, carry, causal, slot, reuse):
            acc, cl = carry
            if reuse is True:
                out_copy(slot, 0).wait()
            elif reuse is not None:
                @pl.when(reuse)
                def _():
                    out_copy(slot, 0).wait()
            rows = pl.ds(pl.multiple_of(j * tk, tk), tk)
            kb = k_ref[rows, :].astype(BF16)
            vb = v_ref[rows, :].astype(BF16)
            z = lax.dot_general(qb, kb, NT, preferred_element_type=F32)
            lsig = -_softplus(z)
            l = lsig if causal is None else jnp.where(causal, lsig, 0.0)
            loga = z + lsig + _dot01(l, usuf) + cl
            if causal is not None:
                loga = jnp.where(causal, loga, NEG)
            ab = jnp.exp(loga).astype(BF16)
            acc = acc + lax.dot_general(ab, vb, NN, preferred_element_type=F32)
            stage[slot] = ab
            out_copy(slot, j).start()
            return acc, cl + jnp.sum(l, axis=1, keepdims=True)

        carry = (jnp.zeros((tq, SB_HD), F32), jnp.zeros((tq, 1), F32))
        for n, dd in enumerate(range(nd - 1, -1, -1)):
            carry = tile(i * nd + dd, carry, c + dd * tk < r, n, None)

        if nd == 2:
            slots = 2

            def pair(n, cr):
                out_copy(0, 0).wait()
                out_copy(1, 0).wait()
                return tile(i * nd - 2 - 2 * n, tile(i * nd - 1 - 2 * n, cr, None, 0, None), None, 1, None)

            acc, _ = lax.fori_loop(0, i, pair, carry)
        else:
            slots = SB_SLOTS

            def rest(n, cr):
                return tile(i * nd - 1 - n, cr, None, (nd + n) % SB_SLOTS, nd + n >= SB_SLOTS)

            acc, _ = lax.fori_loop(0, i * nd, rest, carry)
        total = (i + 1) * nd
        for back in range(1, slots + 1):
            @pl.when(total >= back)
            def _():
                out_copy((total - back) % slots, 0).wait()

        o_ref[...] = acc.astype(o_ref.dtype)

    assert nd <= SB_SLOTS
    blk = lambda off: pl.BlockSpec((s, SB_HD), functools.partial(lambda h, i, off: (0, off + h), off=off))
    return pl.pallas_call(
        body, name=name, grid=(heads, s // tq),
        in_specs=[pl.BlockSpec((tq, SB_HD), lambda h, i: (i, h)), blk(heads), blk(2 * heads)],
        out_specs=[pl.BlockSpec((tq, SB_HD), lambda h, i: (i, h)), ANY],
        out_shape=[pltpu.HBM((s, heads * SB_HD), BF16), pltpu.HBM((heads, s // tq, s // tk, tq, tk), BF16)],
        scratch_shapes=[pltpu.VMEM((SB_SLOTS, tq, tk), BF16), pltpu.SemaphoreType.DMA((SB_SLOTS,))],
        compiler_params=_params(8 * s * SB_HD * 4 + 24 * tq * tk * 4 + (8 << 20)),
    )(_hbm(zm), _hbm(zm), _hbm(zm))


def _sb_bwd(zm, dy, a_all, dz, after, heads, *, name, tq=512, tk=256):
    s = zm.shape[0]
    tq, tk, nd = _sb_tiles(s, tq, tk)
    nq = s // tq
    scale = SB_HD ** -0.5

    def body(q_ref, k_ref, v_ref, do_ref, a_in, dz_ref, after_ref, dq_ref, dk_ref, dv_ref, dka, dva, abuf, sem):
        h, i = pl.program_id(0), pl.program_id(1)

        @pl.when(i == 0)
        def _():
            dka[...] = jnp.zeros_like(dka)
            dva[...] = jnp.zeros_like(dva)

        qb = (q_ref[...] * scale).astype(BF16)
        dob = do_ref[...].astype(BF16)
        qb_t = (q_ref[...] * scale).T.astype(BF16)
        dob_t = do_ref[...].astype(F32).T.astype(BF16)
        r = lax.broadcasted_iota(jnp.int32, (tq, tk), 0)
        c = lax.broadcasted_iota(jnp.int32, (tq, tk), 1)
        ur = lax.broadcasted_iota(jnp.int32, (tk, tk), 0)
        uc = lax.broadcasted_iota(jnp.int32, (tk, tk), 1)
        uexcl = (ur < uc).astype(BF16)

        def fetch(j, slot):
            return pltpu.make_async_copy(a_in.at[h, i, j], abuf.at[slot], sem.at[slot])

        total = (i + 1) * nd
        ahead = SB_SLOTS - 1 - (nd == 2)

        def arrive(j):
            fetch(j, j % SB_SLOTS).wait()

            @pl.when(j + ahead < total)
            def _():
                fetch(j + ahead, (j + ahead) % SB_SLOTS).start()

        def tile(j, carry, causal, sync=True):
            dq, cg = carry
            slot = j % SB_SLOTS
            if sync:
                arrive(j)
            rows = pl.ds(pl.multiple_of(j * tk, tk), tk)
            kb = k_ref[rows, :].astype(BF16)
            vb = v_ref[rows, :].astype(BF16)
            z = lax.dot_general(qb, kb, NT, preferred_element_type=F32)
            sig = 1.0 / (1.0 + jnp.exp(-z))
            ab = abuf[slot]
            g = ab.astype(F32) * lax.dot_general(dob, vb, NT, preferred_element_type=F32)
            p = cg + lax.dot_general(g.astype(BF16), uexcl, NN, preferred_element_type=F32)
            dz = g - sig * (g + p)
            if causal is not None:
                dz = jnp.where(causal, dz, 0.0)
            dzb = dz.astype(BF16)
            dva[j] += lax.dot_general(dob_t, ab, NN, preferred_element_type=F32)
            dka[j] += lax.dot_general(qb_t, dzb, NN, preferred_element_type=F32)
            dq = dq + lax.dot_general(dzb, kb, NN, preferred_element_type=F32)
            return dq, cg + jnp.sum(g, axis=1, keepdims=True)

        for first in range(ahead):
            @pl.when(first < total)
            def _():
                fetch(first, first).start()

        init = (jnp.zeros((tq, SB_HD), F32), jnp.zeros((tq, 1), F32))
        if nd == 2:
            def pair(n, cr):
                arrive(2 * n)
                arrive(2 * n + 1)
                return tile(2 * n + 1, tile(2 * n, cr, None, False), None, False)

            carry = lax.fori_loop(0, i, pair, init)
        else:
            carry = lax.fori_loop(0, i * nd, lambda j, cr: tile(j, cr, None), init)
        for dd in range(nd):
            carry = tile(i * nd + dd, carry, c + dd * tk < r)
        dq_ref[...] = (carry[0] * scale).astype(dq_ref.dtype)

        @pl.when(i == nq - 1)
        def _():
            for jj in range(s // tk):
                dk_ref[jj * tk:(jj + 1) * tk, :] = dka[jj].T.astype(dk_ref.dtype)
                dv_ref[jj * tk:(jj + 1) * tk, :] = dva[jj].T.astype(dv_ref.dtype)

    blk = lambda off: pl.BlockSpec((s, SB_HD), functools.partial(lambda h, i, off: (0, off + h), off=off))
    tile_spec = pl.BlockSpec((tq, SB_HD), lambda h, i: (i, h))
    full = pltpu.HBM((s, heads * SB_HD), BF16)
    return pl.pallas_call(
        body, name=name, grid=(heads, nq),
        in_specs=[tile_spec, blk(heads), blk(2 * heads), tile_spec, ANY, ANY, ANY],
        out_specs=[tile_spec, blk(0), blk(0)],
        out_shape=[pltpu.HBM(dz.shape, dz.dtype), full, full],
        input_output_aliases={5: 0},
        scratch_shapes=[pltpu.VMEM((s // tk, SB_HD, tk), F32), pltpu.VMEM((s // tk, SB_HD, tk), F32),
                        pltpu.VMEM((SB_SLOTS, tq, tk), BF16), pltpu.SemaphoreType.DMA((SB_SLOTS,))],
        compiler_params=_params(12 * s * SB_HD * 4 + 32 * tq * tk * 4 + (8 << 20)),
    )(_hbm(zm), _hbm(zm), _hbm(zm), _hbm(dy), a_all, dz, after)


def _conv_taps(u, w_ref, rows_i):
    taps = []
    for j in range(CONV_W):
        sh = CONV_W - 1 - j
        if sh == 0:
            taps.append(u)
        else:
            taps.append(jnp.where(rows_i >= sh, pltpu.roll(u, sh, 0), 0.0))
    return taps


def _conv_fwd(zm, col0, width, cw, cb, *, name):
    s = zm.shape[0]
    bw = _pick(width, (LANES,))
    off = col0 // bw

    def body(u_ref, w_ref, b_ref, o_ref):
        u = u_ref[...]
        rows_i = lax.broadcasted_iota(jnp.int32, u.shape, 0)
        acc = jnp.broadcast_to(b_ref[...], u.shape)
        for j, tp in enumerate(_conv_taps(u, w_ref, rows_i)):
            acc = acc + tp * w_ref[j:j + 1, :]
        o_ref[...] = acc * _sigmoid(acc)

    return pl.pallas_call(
        body, name=name, grid=(width // bw,),
        in_specs=[pl.BlockSpec((s, bw), lambda j: (0, off + j)), pl.BlockSpec((CONV_W, bw), lambda j: (0, j)),
                  pl.BlockSpec((1, bw), lambda j: (0, j))],
        out_specs=pl.BlockSpec((s, bw), lambda j: (0, j)),
        out_shape=pltpu.HBM((s, width), F32),
        compiler_params=_params(12 * s * bw * 4 + (4 << 20)),
    )(_hbm(zm), cw, cb)


def _conv_bwd(zm, col0, width, cw, cb, dqk, dz, *, name):
    s = zm.shape[0]
    bw = _pick(width, (LANES,))
    off = col0 // bw

    def body(u_ref, w_ref, b_ref, d_ref, dz_ref, du_ref, dw_ref, db_ref):
        u = u_ref[...]
        rows_i = lax.broadcasted_iota(jnp.int32, u.shape, 0)
        taps = _conv_taps(u, w_ref, rows_i)
        acc = jnp.broadcast_to(b_ref[...], u.shape)
        for j, tp in enumerate(taps):
            acc = acc + tp * w_ref[j:j + 1, :]
        sg = _sigmoid(acc)
        dc = d_ref[...] * (sg * (1.0 + acc * (1.0 - sg)))
        du = jnp.zeros_like(u)
        for j in range(CONV_W):
            sh = CONV_W - 1 - j
            if sh == 0:
                du = du + dc * w_ref[j:j + 1, :]
            else:
                du = du + jnp.where(rows_i < s - sh, pltpu.roll(dc, s - sh, 0), 0.0) * w_ref[j:j + 1, :]
            dw_ref[j:j + 1, :] = jnp.sum(dc * taps[j], axis=0, keepdims=True)
        du_ref[...] = du.astype(du_ref.dtype)
        db_ref[...] = jnp.sum(dc, axis=0, keepdims=True)

    return pl.pallas_call(
        body, name=name, grid=(width // bw,),
        in_specs=[pl.BlockSpec((s, bw), lambda j: (0, off + j)), pl.BlockSpec((CONV_W, bw), lambda j: (0, j)),
                  pl.BlockSpec((1, bw), lambda j: (0, j)), pl.BlockSpec((s, bw), lambda j: (0, j)), ANY],
        out_specs=[pl.BlockSpec((s, bw), lambda j: (0, off + j)), pl.BlockSpec((CONV_W, bw), lambda j: (0, j)),
                   pl.BlockSpec((1, bw), lambda j: (0, j))],
        out_shape=[pltpu.HBM(dz.shape, dz.dtype), pltpu.HBM((CONV_W, width), F32),
                   pltpu.HBM((1, width), F32)],
        input_output_aliases={4: 0},
        compiler_params=_params(20 * s * bw * 4 + (4 << 20)),
    )(_hbm(zm), cw, cb, _hbm(dqk), dz)


def _ml_gates(gcol_ref, grow_ref):
    l = CHUNK
    r = lax.broadcasted_iota(jnp.int32, (l, l), 0)
    c = lax.broadcasted_iota(jnp.int32, (l, l), 1)
    gcol = gcol_ref[...]
    grow = grow_ref[0]
    bcol = _u01dot((c <= r).astype(BF16), gcol)
    brow = _dot01(grow, (r <= c).astype(BF16))
    return gcol, grow, bcol, brow, r >= c


def _ml_chunk(h, dh, mq_ref, mk_ref, v_ref, gates, cp, n_prev, m_prev):
    gcol, grow, bcol, brow, tri = gates
    l = CHUNK
    sl = slice(h * dh, (h + 1) * dh)
    qc = mq_ref[:, sl]
    kc = mk_ref[:, sl] * (dh ** -0.5)
    vc = v_ref[:, sl]
    i_row = grow[h:h + 1, :]
    i_col = gcol[:, h:h + 1]
    b_col = bcol[:, ML_HEADS + h:ML_HEADS + h + 1]
    b_row = brow[ML_HEADS + h:ML_HEADS + h + 1, :]
    b_end = b_col[l - 1:l, :]
    d = jnp.where(tri, b_col - b_row + i_row, -jnp.inf)
    m_inter = b_col + m_prev
    m_t = jnp.maximum(m_inter, jnp.max(d, axis=1, keepdims=True))
    w = jnp.exp(d - m_t)
    s_inter = jnp.exp(m_inter - m_t)
    qb, kb, vb = qc.astype(BF16), kc.astype(BF16), vc.astype(BF16)
    cpb = cp.astype(BF16)
    a = lax.dot_general(qb, kb, NT, preferred_element_type=F32)
    sc = a * w
    qcp = lax.dot_general(qb, cpb, NT, preferred_element_type=F32)
    qn = jnp.sum(qc * n_prev, axis=1, keepdims=True)
    num = lax.dot_general(sc.astype(BF16), vb, NN, preferred_element_type=F32) + s_inter * qcp
    den = jnp.sum(sc, axis=1, keepdims=True) + s_inter * qn
    floor = jnp.exp(-m_t)
    dnm = jnp.maximum(jnp.abs(den), floor)
    g_col = b_end - b_col + i_col
    g_row = b_end - b_row + i_row
    m_new = jnp.maximum(b_end + m_prev, jnp.max(g_row, axis=1, keepdims=True))
    decay = jnp.exp(b_end + m_prev - m_new)
    wk = jnp.exp(g_col - m_new)
    return dict(qc=qc, kc=kc, vc=vc, qb=qb, kb=kb, vb=vb, cpb=cpb, w=w, s_inter=s_inter, a=a, sc=sc, qcp=qcp, qn=qn,
                num=num, den=den, floor=floor, dnm=dnm, m_new=m_new, decay=decay, wk=wk, sl=sl)


def _ml_fwd(mqk, zm, vcol, gcol, grow, d_model, *, name):
    s = zm.shape[0]
    nc = s // CHUNK
    dh = d_model // ML_HEADS
    hh = ML_HEADS

    def body(mq_ref, mk_ref, v_ref, gcol_ref, grow_ref, h_ref, cs_ref, ns_ref, ms_ref, c_s, n_s, m_s):
        @pl.when(pl.program_id(0) == 0)
        def _():
            c_s[...] = jnp.zeros_like(c_s)
            n_s[...] = jnp.zeros_like(n_s)
            m_s[...] = jnp.zeros_like(m_s)

        gates = _ml_gates(gcol_ref, grow_ref)
        for h in range(hh):
            cp, n_prev, m_prev = c_s[h], n_s[h], m_s[h][:, 0:1]
            cs_ref[0, h] = cp
            ns_ref[0, h] = n_prev
            ms_ref[0, h] = m_s[h]
            f = _ml_chunk(h, dh, mq_ref, mk_ref, v_ref, gates, cp, n_prev, m_prev)
            h_ref[:, f["sl"]] = f["num"] / f["dnm"]
            c_s[h] = f["decay"] * cp + lax.dot_general((f["vc"] * f["wk"]).astype(BF16), f["kb"], TN,
                                                       preferred_element_type=F32)
            n_s[h] = f["decay"] * n_prev + jnp.sum(f["wk"] * f["kc"], axis=0, keepdims=True)
            m_s[h] = jnp.broadcast_to(f["m_new"], (1, LANES))

    dblk = d_model
    return pl.pallas_call(
        body, name=name, grid=(nc,),
        in_specs=[pl.BlockSpec((CHUNK, dblk), lambda c: (c, 0)), pl.BlockSpec((CHUNK, dblk), lambda c: (c, 1)),
                  pl.BlockSpec((CHUNK, dblk), lambda c: (c, vcol // dblk)),
                  pl.BlockSpec((CHUNK, LANES), lambda c: (c, 0)), pl.BlockSpec((1, 8, CHUNK), lambda c: (c, 0, 0))],
        out_specs=[pl.BlockSpec((CHUNK, dblk), lambda c: (c, 0)),
                   pl.BlockSpec((1, hh, dh, dh), lambda c: (c, 0, 0, 0)),
                   pl.BlockSpec((1, hh, 1, dh), lambda c: (c, 0, 0, 0)),
                   pl.BlockSpec((1, hh, 1, LANES), lambda c: (c, 0, 0, 0))],
        out_shape=[pltpu.HBM((s, d_model), F32), pltpu.HBM((nc, hh, dh, dh), F32),
                   pltpu.HBM((nc, hh, 1, dh), F32), pltpu.HBM((nc, hh, 1, LANES), F32)],
        scratch_shapes=[pltpu.VMEM((hh, dh, dh), F32), pltpu.VMEM((hh, 1, dh), F32), pltpu.VMEM((hh, 1, LANES), F32)],
        compiler_params=_params(8 * hh * dh * dh * 4 + (16 << 20)),
    )(_hbm(mqk), _hbm(mqk), _hbm(zm), _hbm(gcol), _hbm(grow))


def _ml_bwd(mqk, zm, vcol, gcol, grow, cs, ns, ms, dhm, dz, d_model, *, name):
    s = zm.shape[0]
    nc = s // CHUNK
    dh = d_model // ML_HEADS
    hh = ML_HEADS
    l = CHUNK

    def body(mq_ref, mk_ref, v_ref, gcol_ref, grow_ref, cs_ref, ns_ref, ms_ref, dh_ref, dz_ref,
             dqk_ref, dv_ref, dgc_ref, dgr_ref, dc_s, dn_s):
        @pl.when(pl.program_id(0) == 0)
        def _():
            dc_s[...] = jnp.zeros_like(dc_s)
            dn_s[...] = jnp.zeros_like(dn_s)

        gates = _ml_gates(gcol_ref, grow_ref)
        lane = lax.broadcasted_iota(jnp.int32, (l, LANES), 1)
        rowi = lax.broadcasted_iota(jnp.int32, (8, l), 0)
        lastrow = lax.broadcasted_iota(jnp.int32, (l, 1), 0) == l - 1
        dgc = jnp.zeros((l, LANES), F32)
        dgr = jnp.zeros((8, l), F32)
        for h in range(hh):
            cp, n_prev, m_prev = cs_ref[0, h], ns_ref[0, h], ms_ref[0, h][:, 0:1]
            f = _ml_chunk(h, dh, mq_ref, mk_ref, v_ref, gates, cp, n_prev, m_prev)
            dC, dn = dc_s[h], dn_s[h]
            dhv = dh_ref[:, f["sl"]]
            dnum = dhv / f["dnm"]
            hv = f["num"] / f["dnm"]
            ddnm = -jnp.sum(dhv * hv, axis=1, keepdims=True) / f["dnm"]
            dden = jnp.where(jnp.abs(f["den"]) >= f["floor"], ddnm * jnp.sign(f["den"]), 0.0)
            dnb = dnum.astype(BF16)
            dsc = lax.dot_general(dnb, f["vb"], NT, preferred_element_type=F32) + dden
            dvc = lax.dot_general(f["sc"].astype(BF16), dnb, TN, preferred_element_type=F32)
            ds_inter = jnp.sum(dnum * f["qcp"], axis=1, keepdims=True) + dden * f["qn"]
            sdn = (f["s_inter"] * dnum).astype(BF16)
            sdd = f["s_inter"] * dden
            da = dsc * f["w"]
            dab = da.astype(BF16)
            dqc = (lax.dot_general(dab, f["kb"], NN, preferred_element_type=F32)
                   + lax.dot_general(sdn, f["cpb"], NN, preferred_element_type=F32) + sdd * n_prev)
            dcp = f["decay"] * dC + lax.dot_general(sdn, f["qb"], TN, preferred_element_type=F32)
            dnp = f["decay"] * dn + jnp.sum(sdd * f["qc"], axis=0, keepdims=True)
            vw = (f["vc"] * f["wk"]).astype(BF16)
            dCb = dC.astype(BF16)
            dkc = (lax.dot_general(dab, f["qb"], TN, preferred_element_type=F32)
                   + lax.dot_general(vw, dCb, NN, preferred_element_type=F32) + f["wk"] * dn)
            e = lax.dot_general(f["kb"], dCb, NT, preferred_element_type=F32)
            dvc = dvc + e * f["wk"]
            dwk = jnp.sum(e * f["vc"], axis=1, keepdims=True) + jnp.sum(f["kc"] * dn, axis=1, keepdims=True)
            ddecay = jnp.sum(jnp.sum(dC * cp, axis=1, keepdims=True), axis=0, keepdims=True) \
                + jnp.sum(dn * n_prev, axis=1, keepdims=True)
            dd = dsc * f["sc"]
            dlw = dwk * f["wk"]
            db_end = jnp.sum(dlw, axis=0, keepdims=True) + ddecay * f["decay"]
            di_col = dlw
            db_col = jnp.sum(dd, axis=1, keepdims=True) + ds_inter * f["s_inter"] - dlw \
                + jnp.where(lastrow, db_end, 0.0)
            cs_dd = jnp.sum(dd, axis=0, keepdims=True)
            dgc = dgc + jnp.where(lane == h, di_col, 0.0) + jnp.where(lane == hh + h, db_col, 0.0)
            dgr = dgr + jnp.where(rowi == h, cs_dd, 0.0) - jnp.where(rowi == hh + h, cs_dd, 0.0)
            dqk_ref[:, f["sl"]] = dqc
            dqk_ref[:, d_model + h * dh:d_model + (h + 1) * dh] = dkc * (dh ** -0.5)
            dv_ref[:, f["sl"]] = dvc.astype(dv_ref.dtype)
            dc_s[h] = dcp
            dn_s[h] = dnp
        dgc_ref[...] = dgc
        dgr_ref[0] = dgr

    dblk = d_model
    rev = lambda c: nc - 1 - c
    return pl.pallas_call(
        body, name=name, grid=(nc,),
        in_specs=[pl.BlockSpec((l, dblk), lambda c: (rev(c), 0)), pl.BlockSpec((l, dblk), lambda c: (rev(c), 1)),
                  pl.BlockSpec((l, dblk), lambda c: (rev(c), vcol // dblk)),
                  pl.BlockSpec((l, LANES), lambda c: (rev(c), 0)), pl.BlockSpec((1, 8, l), lambda c: (rev(c), 0, 0)),
                  pl.BlockSpec((1, hh, dh, dh), lambda c: (rev(c), 0, 0, 0)),
                  pl.BlockSpec((1, hh, 1, dh), lambda c: (rev(c), 0, 0, 0)),
                  pl.BlockSpec((1, hh, 1, LANES), lambda c: (rev(c), 0, 0, 0)),
                  pl.BlockSpec((l, dblk), lambda c: (rev(c), 0)), ANY],
        out_specs=[pl.BlockSpec((l, 2 * dblk), lambda c: (rev(c), 0)),
                   pl.BlockSpec((l, dblk), lambda c: (rev(c), vcol // dblk)),
                   pl.BlockSpec((l, LANES), lambda c: (rev(c), 0)),
                   pl.BlockSpec((1, 8, l), lambda c: (rev(c), 0, 0))],
        out_shape=[pltpu.HBM((s, 2 * d_model), F32),
                   pltpu.HBM(dz.shape, dz.dtype), pltpu.HBM((s, LANES), F32),
                   pltpu.HBM((nc, 8, l), F32)],
        input_output_aliases={9: 1},
        scratch_shapes=[pltpu.VMEM((hh, dh, dh), F32), pltpu.VMEM((hh, 1, dh), F32)],
        compiler_params=_params(10 * hh * dh * dh * 4 + (16 << 20)),
    )(*[_hbm(a) for a in (mqk, mqk, zm, gcol, grow, cs, ns, ms, dhm)], dz)


def _xa_fwd(zm, qcol, kv, gq, gk, d_model, *, name, tq=512):
    s = zm.shape[0]
    nm = kv.shape[0]
    dh = d_model // X_HEADS
    tq = _pick(s, (tq, 128, 64))
    scale = dh ** -0.5

    def body(q_ref, k_ref, v_ref, gq_ref, gk_ref, o_ref):
        qn = _rms_fwd(q_ref[...], gq_ref[...])
        kn = _rms_fwd(k_ref[...], gk_ref[...])
        lg = _dot(qn, kn, NT) * scale
        lg = lg - jnp.max(lg, axis=1, keepdims=True)
        p = jnp.exp(lg)
        p = p / jnp.sum(p, axis=1, keepdims=True)
        o_ref[...] = _dot(p, v_ref[...], NN).astype(o_ref.dtype)

    return pl.pallas_call(
        body, name=name, grid=(X_HEADS, s // tq),
        in_specs=[pl.BlockSpec((tq, dh), lambda h, i: (i, qcol // dh + h)), pl.BlockSpec((nm, dh), lambda h, i: (0, h)),
                  pl.BlockSpec((nm, dh), lambda h, i: (0, X_HEADS + h)),
                  pl.BlockSpec((1, dh), lambda h, i: (0, 0)), pl.BlockSpec((1, dh), lambda h, i: (0, 0))],
        out_specs=pl.BlockSpec((tq, dh), lambda h, i: (i, h)),
        out_shape=pltpu.HBM((s, d_model), BF16),
        compiler_params=_params(32 << 20),
    )(_hbm(zm), _hbm(kv), _hbm(kv), gq, gk)


def _xa_bwd(zm, qcol, kv, gq, gk, dy, dz, d_model, *, name, tq=512):
    s = zm.shape[0]
    nm = kv.shape[0]
    dh = d_model // X_HEADS
    tq = _pick(s, (tq, 128, 64))
    nq = s // tq
    scale = dh ** -0.5

    def body(q_ref, k_ref, v_ref, gq_ref, gk_ref, do_ref, dz_ref, dq_ref, dkn_ref, dv_ref, dgq_ref):
        h, i = pl.program_id(0), pl.program_id(1)

        @pl.when(i == 0)
        def _():
            dkn_ref[...] = jnp.zeros_like(dkn_ref)
            dv_ref[...] = jnp.zeros_like(dv_ref)

        @pl.when((i == 0) & (h == 0))
        def _():
            dgq_ref[...] = jnp.zeros_like(dgq_ref)

        q = q_ref[...]
        qn = _rms_fwd(q, gq_ref[...])
        kn = _rms_fwd(k_ref[...], gk_ref[...])
        lg = _dot(qn, kn, NT) * scale
        lg = lg - jnp.max(lg, axis=1, keepdims=True)
        p = jnp.exp(lg)
        p = p / jnp.sum(p, axis=1, keepdims=True)
        do = do_ref[...]
        dv_ref[...] += _dot(p, do, TN)
        dp = _dot(do, v_ref[...], NT)
        dlg = p * (dp - jnp.sum(dp * p, axis=1, keepdims=True)) * scale
        dqn = _dot(dlg, kn, NN)
        dkn_ref[...] += _dot(dlg, qn, TN)
        dq, dgq = _rms_bwd(q, gq_ref[...], dqn)
        dq_ref[...] = dq.astype(dq_ref.dtype)
        dgq_ref[...] += jnp.sum(dgq, axis=0, keepdims=True)

    return pl.pallas_call(
        body, name=name, grid=(X_HEADS, nq),
        in_specs=[pl.BlockSpec((tq, dh), lambda h, i: (i, qcol // dh + h)), pl.BlockSpec((nm, dh), lambda h, i: (0, h)),
                  pl.BlockSpec((nm, dh), lambda h, i: (0, X_HEADS + h)),
                  pl.BlockSpec((1, dh), lambda h, i: (0, 0)), pl.BlockSpec((1, dh), lambda h, i: (0, 0)),
                  pl.BlockSpec((tq, dh), lambda h, i: (i, h)), ANY],
        out_specs=[pl.BlockSpec((tq, dh), lambda h, i: (i, qcol // dh + h)),
                   pl.BlockSpec((nm, dh), lambda h, i: (0, h)),
                   pl.BlockSpec((nm, dh), lambda h, i: (0, h)), pl.BlockSpec((1, dh), lambda h, i: (0, 0))],
        out_shape=[pltpu.HBM(dz.shape, dz.dtype), pltpu.HBM((nm, d_model), F32),
                   pltpu.HBM((nm, d_model), F32), pltpu.HBM((1, dh), F32)],
        input_output_aliases={6: 0},
        compiler_params=_params(32 << 20),
    )(_hbm(zm), _hbm(kv), _hbm(kv), gq, gk, _hbm(dy), dz)


def _place():
    return lax.axis_index("x"), lax.axis_index("y"), lax.axis_index("c")


ANY = pl.BlockSpec(memory_space=pl.ANY)


def _allgather_two_level(big, small, *, name, chunk_rows=64):
    r, cc = big.shape
    half = r // 2
    nr = _pick(half, (chunk_rows, 32, 16))
    nq = half // nr

    def body(big_ref, small_ref, obig, osmall, land, passed, send, recv, fsend, frecv, out_a, out_b, ssend, srecv, loc):
        x, y, c = _place()
        k = 2 * x + y
        chips = [(1 - x, y), (x, 1 - y), (1 - x, 1 - y)]
        slots = [2 * px + py for px, py in chips]
        local = [pltpu.make_async_copy(big_ref, obig.at[k], loc.at[0]),
                 pltpu.make_async_copy(small_ref, osmall.at[k], loc.at[1])]
        for cp in local:
            cp.start()

        def rows(h, q):
            return pl.ds(pl.multiple_of(h * half + q * nr, nr), nr)

        def chunk(q):
            return pl.ds(q * nr, nr)

        def over_ici(j, q):
            return pltpu.make_async_remote_copy(
                src_ref=big_ref.at[rows(c, q)], dst_ref=land.at[j, chunk(q)], send_sem=send.at[nq * j + q],
                recv_sem=recv.at[nq * j + q], device_id=(chips[j][0], chips[j][1], c), device_id_type=MESH)

        def to_sibling(j, q):
            return pltpu.make_async_remote_copy(
                src_ref=land.at[j, chunk(q)], dst_ref=passed.at[j, chunk(q)], send_sem=fsend.at[nq * j + q],
                recv_sem=frecv.at[nq * j + q], device_id=(x, y, 1 - c), device_id_type=MESH)

        def small_copy(j, slot):
            return pltpu.make_async_remote_copy(
                src_ref=small_ref, dst_ref=osmall.at[slot], send_sem=ssend.at[j], recv_sem=srecv.at[j],
                device_id=(chips[j][0], chips[j][1], c), device_id_type=MESH)

        for q in range(nq):
            for j in range(3):
                over_ici(j, q).start()
        for j in range(3):
            small_copy(j, k).start()
        for q in range(nq):
            for j in range(3):
                over_ici(j, q).wait_recv()
                to_sibling(j, q).start()
                cp = pltpu.make_async_copy(land.at[j, chunk(q)], obig.at[slots[j], rows(c, q)], out_a.at[nq * j + q])
                cp.start()
                local.append(cp)
        for q in range(nq):
            for j in range(3):
                to_sibling(j, q).wait_recv()
                cp = pltpu.make_async_copy(passed.at[j, chunk(q)], obig.at[slots[j], rows(1 - c, q)],
                                           out_b.at[nq * j + q])
                cp.start()
                local.append(cp)
        for j in range(3):
            small_copy(j, slots[j]).wait_recv()
            small_copy(j, k).wait_send()
        for q in range(nq):
            for j in range(3):
                over_ici(j, q).wait_send()
                to_sibling(j, q).wait_send()
        for cp in local:
            cp.wait()

    stage = 2 * _nbytes((3, half, cc), big.dtype)
    return pl.pallas_call(
        body, name=name, in_specs=[ANY] * 2, out_specs=[ANY] * 2,
        out_shape=[pltpu.HBM((4,) + big.shape, big.dtype), pltpu.HBM((4,) + small.shape, small.dtype)],
        scratch_shapes=[pltpu.VMEM((3, half, cc), big.dtype), pltpu.VMEM((3, half, cc), big.dtype)]
        + [pltpu.SemaphoreType.DMA((3 * nq,))] * 6
        + [pltpu.SemaphoreType.DMA((3,)), pltpu.SemaphoreType.DMA((3,)), pltpu.SemaphoreType.DMA((2,))],
        compiler_params=_params(stage + stage // 8 + (4 << 20)),
    )(big, small)


HBM_SPEC = pl.BlockSpec(memory_space=pltpu.HBM)
SEM_SPEC = pl.BlockSpec(memory_space=pltpu.SEMAPHORE)
EFFECT = pltpu.SideEffectType.DATAFLOW_SIDE_EFFECTING


def _split_copies(kind, srcs, lands, send, recv):
    x, y, c = _place()
    if kind == "quarters":
        peers = [(1 - x, y, c), (x, 1 - y, c), (1 - x, 1 - y, c)]
    else:
        peers = [(x ^ ((j >> 2) & 1), y ^ ((j >> 1) & 1), c ^ (j & 1)) for j in range(1, 8)]
    npeer = len(peers)
    out = []
    for t in range(len(srcs)):
        for j, (px, py, pc) in enumerate(peers):
            if kind == "quarters":
                src, mine, theirs = srcs[t], 2 * x + y, 2 * px + py
            else:
                src, mine, theirs = srcs[t].at[2 * px + py, pc], 4 * x + 2 * y + c, 4 * px + 2 * py + pc
            mk = functools.partial(
                pltpu.make_async_remote_copy, src_ref=src, send_sem=send.at[npeer * t + j],
                recv_sem=recv.at[npeer * t + j], device_id=(px, py, pc), device_id_type=MESH)
            out.append((functools.partial(mk, dst_ref=lands[t].at[mine]),
                        functools.partial(mk, dst_ref=lands[t].at[theirs])))
    return out


def _split_start(kind, srcs, land_shapes, after, *, name):
    n = len(srcs)
    ncopies = n * (3 if kind == "quarters" else 7)

    def body(*refs):
        ins, lands = refs[:n], refs[n:2 * n]
        send, recv = refs[2 * n + 1], refs[2 * n + 2]
        token = refs[-1]
        for start, _ in _split_copies(kind, ins, lands, send, recv):
            start().start()
        token[...] = jnp.zeros_like(token)

    lands = [_hbm(lax.empty(shp, a.dtype)) for shp, a in zip(land_shapes, srcs)]
    res = pl.pallas_call(
        body, name=name, in_specs=[HBM_SPEC] * (2 * n) + [ANY],
        out_specs=[SEM_SPEC, SEM_SPEC] + [HBM_SPEC] * (2 * n) + [pl.BlockSpec(memory_space=pltpu.VMEM)],
        out_shape=[pltpu.SemaphoreType.DMA((ncopies,)), pltpu.SemaphoreType.DMA((ncopies,))]
        + [pltpu.HBM(a.shape, a.dtype) for a in srcs] + [pltpu.HBM(shp, a.dtype) for shp, a in zip(land_shapes, srcs)]
        + [jax.ShapeDtypeStruct((8, LANES), F32)],
        input_output_aliases={i: 2 + i for i in range(2 * n)},
        compiler_params=pltpu.CompilerParams(has_side_effects=EFFECT),
    )(*[_hbm(a) for a in srcs], *lands, after)
    return res[0], res[1], list(res[2:2 + n]), list(res[2 + n:2 + 2 * n]), res[-1]


def _split_wait(kind, send, recv, srcs, lands, after, *, name):
    n = len(srcs)

    def body(*refs):
        ins, lnd = refs[:n], refs[n:2 * n]
        snd, rcv = refs[2 * n], refs[2 * n + 1]
        for start, arrive in _split_copies(kind, ins, lnd, snd, rcv):
            start().wait_send()
            arrive().wait_recv()

    res = pl.pallas_call(
        body, name=name, in_specs=[HBM_SPEC] * (2 * n) + [SEM_SPEC, SEM_SPEC] + [ANY] * len(after),
        out_specs=[HBM_SPEC] * (2 * n),
        out_shape=[pltpu.HBM(a.shape, a.dtype) for a in srcs] + [pltpu.HBM(a.shape, a.dtype) for a in lands],
        input_output_aliases={i: i for i in range(2 * n)},
        compiler_params=pltpu.CompilerParams(has_side_effects=EFFECT),
    )(*srcs, *lands, send, recv, *after)
    return list(res[n:])


def _sum8(parts, *, name):
    _, r, c = parts.shape
    t = _pick(r, (128, 64, 32, 16, 8))

    def body(p_ref, o_ref):
        acc = p_ref[0].astype(F32)
        for k in range(1, 8):
            acc = acc + p_ref[k].astype(F32)
        o_ref[...] = acc

    return pl.pallas_call(
        body, name=name, grid=(r // t,), in_specs=[pl.BlockSpec((8, t, c), lambda i: (0, i, 0))],
        out_specs=pl.BlockSpec((t, c), lambda i: (i, 0)), out_shape=pltpu.HBM((r, c), F32),
        compiler_params=_params(2 * 8 * t * c * 2 + 6 * t * c * 4 + (4 << 20)),
    )(_hbm(parts))


def _swap_halves(halves, *, name, chunk_bytes=512 * 1024):
    n = len(halves)
    items = []
    for t, a in enumerate(halves):
        r = a.shape[0]
        k = 1
        while _nbytes(a.shape, a.dtype) // k > chunk_bytes and r % (2 * k) == 0 and (r // (2 * k)) % 8 == 0:
            k *= 2
        items += [(t, q * (r // k), r // k) for q in range(k)]
    m = len(items)

    def body(*refs):
        ins, outs = refs[:n], refs[n:2 * n]
        sbuf, rbuf = refs[2 * n:3 * n], refs[3 * n:4 * n]
        send, recv, loc_own, loc_in, loc_out = refs[4 * n:]
        x, y, c = _place()
        local, stage = [], []
        for t in range(n):
            cp = pltpu.make_async_copy(ins[t], outs[t].at[c], loc_own.at[t])
            cp.start()
            local.append(cp)
        for q, (t, r0, nr) in enumerate(items):
            cp = pltpu.make_async_copy(ins[t].at[pl.ds(r0, nr)], sbuf[t].at[pl.ds(r0, nr)], loc_in.at[q])
            cp.start()
            stage.append(cp)

        def copy(q):
            t, r0, nr = items[q]
            return pltpu.make_async_remote_copy(
                src_ref=sbuf[t].at[pl.ds(r0, nr)], dst_ref=rbuf[t].at[pl.ds(r0, nr)], send_sem=send.at[q],
                recv_sem=recv.at[q], device_id=(x, y, 1 - c), device_id_type=MESH)

        for q in range(m):
            stage[q].wait()
            copy(q).start()
        for q, (t, r0, nr) in enumerate(items):
            copy(q).wait_recv()
            cp = pltpu.make_async_copy(rbuf[t].at[pl.ds(r0, nr)], outs[t].at[1 - c, pl.ds(r0, nr)], loc_out.at[q])
            cp.start()
            local.append(cp)
        for q in range(m):
            copy(q).wait_send()
        for cp in local:
            cp.wait()

    stage_bytes = 2 * sum(_nbytes(a.shape, a.dtype) for a in halves)
    return pl.pallas_call(
        body, name=name, in_specs=[ANY] * n, out_specs=[ANY] * n,
        out_shape=[pltpu.HBM((2,) + a.shape, a.dtype) for a in halves],
        scratch_shapes=[pltpu.VMEM(a.shape, a.dtype) for a in halves] * 2
        + [pltpu.SemaphoreType.DMA((m,)), pltpu.SemaphoreType.DMA((m,)), pltpu.SemaphoreType.DMA((n,)),
           pltpu.SemaphoreType.DMA((m,)), pltpu.SemaphoreType.DMA((m,))],
        compiler_params=_params(stage_bytes + (4 << 20)),
    )(*halves)


def _allreduce_small(p, after, *, name):
    r = p.shape[0]

    def body(p_ref, after_ref, o_ref, buf, send, recv):
        x, y, c = _place()
        me = 4 * x + 2 * y + c
        peers = [(x ^ ((j >> 2) & 1), y ^ ((j >> 1) & 1), c ^ (j & 1)) for j in range(1, 8)]

        def copy(j, slot):
            return pltpu.make_async_remote_copy(
                src_ref=p_ref, dst_ref=buf.at[slot], send_sem=send.at[j], recv_sem=recv.at[j],
                device_id=peers[j], device_id_type=MESH)

        for j in range(7):
            copy(j, me).start()
        buf[me] = p_ref[...]
        for j in range(7):
            px, py, pc = peers[j]
            copy(j, 4 * px + 2 * py + pc).wait_recv()
        for j in range(7):
            copy(j, me).wait_send()
        acc = buf[0]
        for k in range(1, 8):
            acc = acc + buf[k]
        o_ref[...] = acc

    vspec = pl.BlockSpec(memory_space=pltpu.VMEM)
    return pl.pallas_call(
        body, name=name, in_specs=[vspec, ANY], out_specs=vspec, out_shape=jax.ShapeDtypeStruct((r, LANES), F32),
        scratch_shapes=[pltpu.VMEM((8, r, LANES), F32), pltpu.SemaphoreType.DMA((7,)), pltpu.SemaphoreType.DMA((7,))],
    )(p, after)


def _adamw_fn(w, g, m, v):
    m = ADAM_B1 * m + (1.0 - ADAM_B1) * g
    v = ADAM_B2 * v + (1.0 - ADAM_B2) * (g * g)
    m_hat = m / (1.0 - ADAM_B1 ** ADAM_STEP)
    v_hat = v / (1.0 - ADAM_B2 ** ADAM_STEP)
    delta = -ADAM_LR * (m_hat / (jnp.sqrt(v_hat) + ADAM_EPS) + ADAM_WD * w)
    return delta, m, v


def _adamw(w, g, m, v, *, name):
    c = w.shape[1]
    return _rowwise(_adamw_fn, [w, g, m, v], [], [(c, F32)] * 3, name=name, tr=128)


def _pack(vecs, rows):
    flat = jnp.concatenate([a.reshape(-1).astype(F32) for a in vecs])
    return jnp.pad(flat, (0, rows * LANES - flat.shape[0])).reshape(rows, LANES)


def _unpack(p, like):
    flat, out, o = p.reshape(-1), [], 0
    for a in like:
        out.append(flat[o:o + a.size].reshape(a.shape))
        o += a.size
    return out


def kernel(x, mem, g_mix, w_in, b_if, b_gate, conv_w, conv_b, ml_norm_g, g_mem, w_mem_kv, q_norm_g, k_norm_g, w_sb_proj, w_ml_proj, w_x_proj, w_out, g_mlp, w_ff1, w_ff2, loss_target, m_g_mix, m_w_in, m_b_if, m_b_gate, m_conv_w, m_conv_b, m_ml_norm_g, m_g_mem, m_w_mem_kv, m_q_norm_g, m_k_norm_g, m_w_sb_proj, m_w_ml_proj, m_w_x_proj, m_w_out, m_g_mlp, m_w_ff1, m_w_ff2, v_g_mix, v_w_in, v_b_if, v_b_gate, v_conv_w, v_conv_b, v_ml_norm_g, v_g_mem, v_w_mem_kv, v_q_norm_g, v_k_norm_g, v_w_sb_proj, v_w_ml_proj, v_w_x_proj, v_w_out, v_g_mlp, v_w_ff1, v_w_ff2):
    _, s, d = x.shape
    nm = mem.shape[1]
    n_in = 4 * w_in.shape[2]
    dff = 4 * w_ff1.shape[2]
    sbh = d // SB_HD
    hh = ML_HEADS
    dh = d // hh
    nc = s // CHUNK
    assert n_in == 11 * d + 2 * hh and d % (2 * LANES) == 0 and s % LANES == 0
    x2, mem2, tgt = x[0], mem[0], loss_target[0]

    k4 = 2 * lax.axis_index("x") + lax.axis_index("y")
    me = 2 * k4 + lax.axis_index("c")
    g_first = _allgather_two_level(w_in[0].astype(BF16), conv_w[0], name="gather_w_in")
    later = [a[0].astype(BF16) for a in (w_mem_kv, w_sb_proj, w_ml_proj, w_x_proj, w_out, w_ff1, w_ff2)]
    gw_send, gw_recv, gw_src, gw_land, gw_token = _split_start(
        "quarters", later, [(4,) + a.shape for a in later], g_first[0], name="gather_rest_start")
    cols = lambda a: a.transpose(1, 0, 2).reshape(a.shape[1], 4 * a.shape[2])
    rws = lambda a: a.reshape(4 * a.shape[1], a.shape[2])
    qn = n_in // 4
    if_lo, if_hi = 7 * d, 7 * d + 2 * hh

    def cut(lo, hi):
        ks = [(k, max(lo, k * qn), min(hi, (k + 1) * qn)) for k in range(4)]
        return [g_first[0][k, :, a - k * qn:b - k * qn] for k, a, b in ks if a < b]

    w_main = jnp.concatenate(cut(0, if_lo) + cut(if_hi, n_in), axis=1)
    w_if = jnp.pad(jnp.concatenate(cut(if_lo, if_hi), axis=1), ((0, 0), (0, LANES - 2 * hh)))
    conv_wf = cols(g_first[1])
    b_if_p = jnp.pad(b_if, ((0, 0), (0, LANES - 2 * hh)))

    (hn,) = _rowwise(_rms_fwd, [x2], [g_mix], [(d, BF16)], name="norm_in", tr=512)
    zm = _mm(hn, w_main, after=gw_token, name="proj_in")
    zif = _mm(hn, w_if, name="proj_if")
    y_sb, a_sb = _sb_fwd(zm, sbh, name="sb_fwd")

    def gate_fn(z, b):
        pre = z + b
        lane = lax.broadcasted_iota(jnp.int32, pre.shape, 1)
        return jnp.where(lane < hh, pre, -_softplus(-pre))

    (gcol,) = _rowwise(gate_fn, [zif], [b_if_p], [(LANES, F32)], name="ml_gates", tr=1024)
    grow = gcol[:, :8].T.reshape(8, nc, CHUNK).transpose(1, 0, 2)
    mqk = _conv_fwd(zm, 3 * d, 2 * d, conv_wf, conv_b, name="conv_fwd")
    hm, cst, nst, mst = _ml_fwd(mqk, zm, 5 * d, gcol, grow, d, name="ml_fwd")

    def mlout_fn(hv, o, g):
        ys = [_rms_fwd(hv[:, k * dh:(k + 1) * dh], g[:, k * dh:(k + 1) * dh]) for k in range(hh)]
        return jnp.concatenate(ys, axis=1) * _sigmoid(o)

    (y_ml,) = _rowwise(mlout_fn, [hm, (zm, d, 6)], [ml_norm_g], [(d, BF16)], name="ml_out", tr=512)
    gw_land = _split_wait("quarters", gw_send, gw_recv, gw_src, gw_land, [y_ml, y_sb], name="gather_rest_wait")
    gw = [lax.dynamic_update_index_in_dim(ld, a, k4, 0) for ld, a in zip(gw_land, later)]
    w_kv, w_sbp, w_mlp, w_xp, w_o, w_f1, w_f2 = (cols(gw[0]), rws(gw[1]), rws(gw[2]), rws(gw[3]), rws(gw[4]),
                                                 cols(gw[5]), rws(gw[6]))
    (memn,) = _rowwise(_rms_fwd, [mem2], [g_mem], [(d, BF16)], name="norm_mem")
    kv = _mm(memn, w_kv, name="proj_kv")
    y_x = _xa_fwd(zm, 7 * d, kv, q_norm_g, k_norm_g, d, name="xa_fwd")
    p_sb = _mm(y_sb, w_sbp, name="proj_sb")
    p_ml = _mm(y_ml, w_mlp, name="proj_ml")
    p_x = _mm(y_x, w_xp, name="proj_x")

    def merge_fn(a, b, c, g0, g1, g2, bg):
        return (_sigmoid(g0 + bg[:, :d]) * a + _sigmoid(g1 + bg[:, d:2 * d]) * b + _sigmoid(g2 + bg[:, 2 * d:]) * c)

    gate_cols = [(zm, d, 8), (zm, d, 9), (zm, d, 10)]
    (mixed,) = _rowwise(merge_fn, [p_sb, p_ml, p_x] + gate_cols, [b_gate], [(d, BF16)], name="merge")
    x1 = _mm(mixed, w_o, tiles=[x2], name="proj_out")
    (h2,) = _rowwise(_rms_fwd, [x1], [g_mlp], [(d, BF16)], name="norm_mlp", tr=512)
    u, act = _mm(h2, w_f1, post=lambda r: (r, jnp.square(jnp.maximum(r, 0.0))), out_dtype=(F32, BF16), name="ff1")
    dy = _mm(act, w_f2, tiles=[x1, tgt], post=lambda r, xv, tv: (r + xv - tv) * (1.0 / d), name="ff2")
    (loss_cols,) = _rowwise(lambda g: (jnp.sum(g * g, axis=0, keepdims=True) * (0.5 * d),), [dy], [], [], [d],
                            name="loss", tr=1024)

    du = _mm(dy, w_f2, tb=True, tiles=[u], post=lambda r, uv: r * 2.0 * jnp.maximum(uv, 0.0), out_dtype=BF16,
             name="ff2_dx")
    dw_f2 = _mm(act, dy, ta=True, name="ff2_dw")
    dw_f1 = _mm(h2, du, ta=True, name="ff1_dw")
    dh2 = _mm(du, w_f1, tb=True, name="ff1_dx")

    def norm_bwd_fn(xv, dyv, res, g):
        dx, dg = _rms_bwd(xv, g, dyv)
        return dx + res, jnp.sum(dg, axis=0, keepdims=True)

    dx1, dg_mlp = _rowwise(norm_bwd_fn, [x1, dh2, dy], [g_mlp], [(d, F32)], [d], name="norm_mlp_bwd", tr=512)
    dmixed = _mm(dx1, w_o, tb=True, name="proj_out_dx")
    dw_o = _mm(mixed, dx1, ta=True, name="proj_out_dw")

    def merge_bwd_fn(dm, a, b, c, g0, g1, g2, bg):
        outs, dgs = [], []
        for p, g, k in ((a, g0, 0), (b, g1, 1), (c, g2, 2)):
            sg = _sigmoid(g + bg[:, k * d:(k + 1) * d])
            outs.append(dm * sg)
            dgs.append(dm * p * sg * (1.0 - sg))
        dgate = jnp.concatenate(dgs, axis=1)
        return (*outs, dgate, jnp.sum(dgate, axis=0, keepdims=True))

    dp_sb, dp_ml, dp_x, dgate, db_gate = _rowwise(
        merge_bwd_fn, [dmixed, p_sb, p_ml, p_x] + gate_cols, [b_gate], [(d, BF16)] * 3 + [(3 * d, BF16)], [3 * d],
        name="merge_bwd", tr=256)
    dw_sbp = _mm(y_sb, dp_sb, ta=True, name="proj_sb_dw")
    dw_mlp = _mm(y_ml, dp_ml, ta=True, name="proj_ml_dw")
    dw_xp = _mm(y_x, dp_x, ta=True, name="proj_x_dw")
    dy_sb = _mm(dp_sb, w_sbp, tb=True, out_dtype=BF16, name="proj_sb_dx")
    dy_ml = _mm(dp_ml, w_mlp, tb=True, name="proj_ml_dx")
    dy_x = _mm(dp_x, w_xp, tb=True, out_dtype=BF16, name="proj_x_dx")

    dzm = _hbm(lax.empty((s, 11 * d), BF16))
    dzm, dkn, dxv, dg_qn = _xa_bwd(zm, 7 * d, kv, q_norm_g, k_norm_g, dy_x, dzm, d, name="xa_bwd")

    def knorm_bwd_fn(kvv, dknv, dvv, g):
        dks, dgs = [], []
        for k in range(X_HEADS):
            sl = slice(k * dh, (k + 1) * dh)
            dk, dg = _rms_bwd(kvv[:, sl], g, dknv[:, sl])
            dks.append(dk)
            dgs.append(jnp.sum(dg, axis=0, keepdims=True))
        return jnp.concatenate(dks + [dvv], axis=1), dgs[0] + dgs[1] + dgs[2] + dgs[3]

    dkv, dg_kn = _rowwise(knorm_bwd_fn, [(kv, d, 0), dkn, dxv], [k_norm_g], [(2 * d, BF16)], [dh], name="xa_knorm_bwd")
    dw_kv = _mm(memn, dkv, ta=True, name="proj_kv_dw")
    dmemn = _mm(dkv, w_kv, tb=True, name="proj_kv_dx")

    def gmem_fn(mv, dv_, g):
        _, dg = _rms_bwd(mv, g, dv_)
        return (jnp.sum(dg, axis=0, keepdims=True),)

    (dg_mem,) = _rowwise(gmem_fn, [mem2, dmemn], [g_mem], [], [d], name="norm_mem_bwd")

    uncols = lambda a: a.reshape(a.shape[0], 4, a.shape[1] // 4).transpose(1, 0, 2)
    unrws = lambda a: a.reshape(4, a.shape[0] // 4, a.shape[1])
    to_parts = lambda q: q.astype(BF16).reshape(4, 2, q.shape[1] // 2, q.shape[2])
    early = [to_parts(q) for q in (uncols(dw_kv), unrws(dw_sbp), unrws(dw_mlp), unrws(dw_xp), unrws(dw_o),
                                   uncols(dw_f1), unrws(dw_f2))]
    ge_send, ge_recv, ge_src, ge_land, ge_token = _split_start(
        "grads", early, [(8,) + a.shape[2:] for a in early], dg_mem, name="exchange_early_start")

    def mlout_bwd_fn(dyv, hv, o, g):
        sg = _sigmoid(o)
        dn = dyv * sg
        dxs, dgs, ys = [], [], []
        for k in range(hh):
            sl = slice(k * dh, (k + 1) * dh)
            ys.append(_rms_fwd(hv[:, sl], g[:, sl]))
            dxk, dgk = _rms_bwd(hv[:, sl], g[:, sl], dn[:, sl])
            dxs.append(dxk)
            dgs.append(dgk)
        do = dyv * jnp.concatenate(ys, axis=1) * sg * (1.0 - sg)
        return jnp.concatenate(dxs, axis=1), do, jnp.sum(jnp.concatenate(dgs, axis=1), axis=0, keepdims=True)

    dhm, dzm, dg_mln = _rowwise(mlout_bwd_fn, [dy_ml, hm, (zm, d, 6)], [ml_norm_g], [(d, F32), (d, BF16)], [d],
                                name="ml_out_bwd", tr=512, into=(dzm, 1, 6))
    dmqk, dzm, dgc, dgr = _ml_bwd(mqk, zm, 5 * d, gcol, grow, cst, nst, mst, dhm, dzm, d, name="ml_bwd")
    dzm, dconv_w, dconv_b = _conv_bwd(zm, 3 * d, 2 * d, conv_wf, conv_b, dmqk, dzm, name="conv_bwd")
    dzm, dsk, dsv = _sb_bwd(zm, dy_sb, a_sb, dzm, ge_token, sbh, name="sb_bwd")
    dgr_t = jnp.pad(dgr.transpose(1, 0, 2).reshape(8, s).T, ((0, 0), (0, LANES - 8)))

    def gate_bwd_fn(a, b, z, bias):
        tot = a + b
        rows_t = tot.shape[0]
        r = lax.broadcasted_iota(jnp.int32, (rows_t, rows_t), 0)
        c = lax.broadcasted_iota(jnp.int32, (rows_t, rows_t), 1)
        sh = CHUNK.bit_length() - 1
        same_chunk = jnp.right_shift(r, sh) == jnp.right_shift(c, sh)
        dlf = _u01dot(((c >= r) & same_chunk).astype(BF16), tot)
        lane = lax.broadcasted_iota(jnp.int32, tot.shape, 1)
        dz = jnp.where(lane < hh, tot, jnp.where(lane < 2 * hh, dlf * _sigmoid(-(z + bias)), 0.0))
        return dz, jnp.sum(dz, axis=0, keepdims=True)

    dzif, db_if_p = _rowwise(gate_bwd_fn, [dgc, dgr_t, zif], [b_if_p], [(LANES, BF16)], [LANES], name="ml_gates_bwd",
                             tr=8 * CHUNK)
    for part, col in ((dsk, d), (dsv, 2 * d), (dgate, 8 * d)):
        dzm = lax.dynamic_update_slice(dzm, part, (0, col))
    dw_main = _mm(hn, dzm, ta=True, out_dtype=BF16, name="proj_in_dw")
    dw_if = _mm(hn, dzif, ta=True, out_dtype=BF16, name="proj_if_dw")

    def dw_quarter(k):
        lo, hi = k * qn, (k + 1) * qn
        segs = [(dw_main, 0, if_lo, 0), (dw_if, if_lo, if_hi, if_lo), (dw_main, if_hi, n_in, 2 * hh)]
        got = [src[:, max(lo, a) - off:min(hi, b) - off] for src, a, b, off in segs if max(lo, a) < min(hi, b)]
        return jnp.concatenate(got, axis=1)

    late = [to_parts(jnp.stack([dw_quarter(k) for k in range(4)]))]
    gl_send, gl_recv, gl_src, gl_land, gl_token = _split_start(
        "grads", late, [(8,) + a.shape[2:] for a in late], dw_if, name="exchange_late_start")
    dhn = _mm(dzm, w_main, tb=True, after=gl_token, name="proj_in_dx")
    dhn = _mm(dzif, w_if, tb=True, tiles=[dhn], name="proj_if_dx")
    dx, dg_mix = _rowwise(norm_bwd_fn, [x2, dhn, dx1], [g_mix], [(d, F32)], [d], name="norm_in_bwd", tr=512)

    own = lambda p: lax.dynamic_index_in_dim(lax.dynamic_index_in_dim(p, k4, 0, keepdims=False),
                                             lax.axis_index("c"), 0, keepdims=False)

    def finish(tag, send, recv, src, land, parts, after, ws, ms, vs):
        land = _split_wait("grads", send, recv, src, land, after, name=f"exchange_{tag}_wait")
        got = [lax.dynamic_update_index_in_dim(ld, own(p), me, 0) for ld, p in zip(land, parts)]
        halves = [_sum8(r, name=f"sum_grads_{tag}{i}") for i, r in enumerate(got)]
        both = _swap_halves(halves, name=f"swap_halves_{tag}")
        gs = [b.reshape(2 * b.shape[1], b.shape[2]) for b in both]
        return gs, [_adamw(w, g, m, v, name=f"adamw_{tag}{i}") for i, (w, g, m, v) in enumerate(zip(ws, gs, ms, vs))]

    first = lambda arrs: [a[0] for a in arrs]
    g_early, out_early = finish(
        "early", ge_send, ge_recv, ge_src, ge_land, early, [dx],
        first([w_mem_kv, w_sb_proj, w_ml_proj, w_x_proj, w_out, w_ff1, w_ff2]),
        first([m_w_mem_kv, m_w_sb_proj, m_w_ml_proj, m_w_x_proj, m_w_out, m_w_ff1, m_w_ff2]),
        first([v_w_mem_kv, v_w_sb_proj, v_w_ml_proj, v_w_x_proj, v_w_out, v_w_ff1, v_w_ff2]))
    g_late, out_late = finish(
        "late", gl_send, gl_recv, gl_src, gl_land, late, [o[0] for o in out_early],
        first([w_in]), first([m_w_in]), first([v_w_in]))
    g_big = [g[None] for g in g_late + g_early]
    big_out = [[o[None] for o in outs] for outs in out_late + out_early]

    small_g = [dg_mix, db_if_p[:, :2 * hh], db_gate, dconv_w, dconv_b, dg_mln, dg_mem, dg_qn, dg_kn, dg_mlp,
               jnp.sum(loss_cols).reshape(1, 1)]
    n_small = sum(a.size for a in small_g)
    rows = -(-n_small // (8 * LANES)) * 8
    g_small = _unpack(_allreduce_small(_pack(small_g, rows), out_late[0][0], name="allreduce_small"), small_g)
    loss = g_small[-1].reshape(())
    qw = conv_w.shape[2]
    g_conv_w = lax.dynamic_slice_in_dim(g_small[3], k4 * qw, qw, axis=1)
    g_small_w = [g_small[0], g_small[1], g_small[2], g_conv_w] + g_small[4:10]
    sm_w = [g_mix, b_if, b_gate, conv_w[0], conv_b, ml_norm_g, g_mem, q_norm_g, k_norm_g, g_mlp]
    sm_m = [m_g_mix, m_b_if, m_b_gate, m_conv_w[0], m_conv_b, m_ml_norm_g, m_g_mem, m_q_norm_g, m_k_norm_g, m_g_mlp]
    sm_v = [v_g_mix, v_b_if, v_b_gate, v_conv_w[0], v_conv_b, v_ml_norm_g, v_g_mem, v_q_norm_g, v_k_norm_g, v_g_mlp]
    n_sw = sum(a.size for a in sm_w)
    rows_w = -(-n_sw // (8 * LANES)) * 8
    sm_out = _adamw(_pack(sm_w, rows_w), _pack(g_small_w, rows_w), _pack(sm_m, rows_w), _pack(sm_v, rows_w),
                    name="adamw_small")
    sm_delta, sm_newm, sm_newv = [_unpack(p, sm_w) for p in sm_out]

    order = ["g_mix", "w_in", "b_if", "b_gate", "conv_w", "conv_b", "ml_norm_g", "g_mem", "w_mem_kv", "q_norm_g",
             "k_norm_g", "w_sb_proj", "w_ml_proj", "w_x_proj", "w_out", "g_mlp", "w_ff1", "w_ff2"]
    small_names = ["g_mix", "b_if", "b_gate", "conv_w", "conv_b", "ml_norm_g", "g_mem", "q_norm_g", "k_norm_g", "g_mlp"]
    big_names = ["w_in", "w_mem_kv", "w_sb_proj", "w_ml_proj", "w_x_proj", "w_out", "w_ff1", "w_ff2"]
    grads, deltas, new_m, new_v = {}, {}, {}, {}
    for i, nme in enumerate(small_names):
        shp = sm_w[i].shape if nme != "conv_w" else conv_w.shape
        grads[nme] = g_small_w[i].reshape(shp)
        deltas[nme], new_m[nme], new_v[nme] = (sm_delta[i].reshape(shp), sm_newm[i].reshape(shp),
                                               sm_newv[i].reshape(shp))
    for i, nme in enumerate(big_names):
        grads[nme] = g_big[i]
        deltas[nme], new_m[nme], new_v[nme] = big_out[i]
    return (loss, dx[None], *[grads[k] for k in order], *[deltas[k] for k in order], *[new_m[k] for k in order],
            *[new_v[k] for k in order])
```

```python
import functools

import jax
import jax.numpy as jnp
from jax import lax
from jax.experimental import pallas as pl
from jax.experimental.pallas import tpu as pltpu

F32 = jnp.float32
BF16 = jnp.bfloat16
MESH = pl.DeviceIdType.MESH

EPS = 1e-6
SB_HD = 128
SB_SLOTS = 8
ML_HEADS = 4
X_HEADS = 4
CHUNK = 64
CONV_W = 4
LANES = 128
ADAM_LR = 0.001
ADAM_B1 = 0.9
ADAM_B2 = 0.999
ADAM_EPS = 1e-08
ADAM_WD = 0.01
ADAM_STEP = 10
VMEM_CAP = 56 * 1024 * 1024
NEG = -1e30

NT = (((1,), (1,)), ((), ()))
NN = (((1,), (0,)), ((), ()))
TN = (((0,), (0,)), ((), ()))


def _dot(a, b, dn=NN):
    return lax.dot_general(a.astype(BF16), b.astype(BF16), dn, preferred_element_type=F32)


def _dot01(x, u, dn=NN):
    hi = x.astype(BF16)
    lo = (x - hi.astype(F32)).astype(BF16)
    return (lax.dot_general(hi, u, dn, preferred_element_type=F32)
            + lax.dot_general(lo, u, dn, preferred_element_type=F32))


def _u01dot(u, x):
    hi = x.astype(BF16)
    lo = (x - hi.astype(F32)).astype(BF16)
    return (lax.dot_general(u, hi, NN, preferred_element_type=F32)
            + lax.dot_general(u, lo, NN, preferred_element_type=F32))


def _pick(n, cands):
    for c in cands:
        if c <= n and n % c == 0:
            return c
    return n


def _nbytes(shape, dtype):
    n = 1
    for s in shape:
        n *= s
    return n * jnp.dtype(dtype).itemsize


def _params(vmem_bytes):
    return pltpu.CompilerParams(vmem_limit_bytes=int(min(VMEM_CAP, max(vmem_bytes, 16 * 1024 * 1024))))


def _hbm(a):
    return pltpu.with_memory_space_constraint(a, pltpu.HBM)


def _softplus(z):
    return jnp.maximum(z, 0.0) + jnp.log(1.0 + jnp.exp(-jnp.abs(z)))


def _sigmoid(z):
    return 1.0 / (1.0 + jnp.exp(-z))


def _rms_fwd(xv, g):
    r = lax.rsqrt(jnp.mean(xv * xv, axis=-1, keepdims=True) + EPS)
    return xv * r * g


def _rms_bwd(xv, g, dy):
    r = lax.rsqrt(jnp.mean(xv * xv, axis=-1, keepdims=True) + EPS)
    xh = xv * r
    dxh = dy * g
    dx = r * (dxh - xh * jnp.mean(dxh * xh, axis=-1, keepdims=True))
    return dx, dy * xh


def _mm(a, b, *, name, ta=False, tb=False, tiles=(), post=None, out_dtype=F32, bm=1024, bn=1024, bk=1024, after=None):
    m, k = (a.shape[1], a.shape[0]) if ta else a.shape
    n = b.shape[0] if tb else b.shape[1]
    tm = _pick(m, (bm, 512, 256, 128))
    tn = _pick(n, (bn, 512, 256, 128))
    tk = _pick(k, (bk, 512, 256, 128))
    nk = k // tk
    if (m // tm) * (n // tn) * nk < 8 and tm % 256 == 0:
        tm //= 2
    dn = (((0 if ta else 1,), (1 if tb else 0,)), ((), ()))
    dts = out_dtype if isinstance(out_dtype, tuple) else (out_dtype,)
    nt, no = len(tiles), len(dts)
    if post is None:
        post = lambda r, *ts: sum((t.astype(F32) for t in ts), r)

    def body(*refs):
        a_ref, b_ref = refs[:2]
        t_refs = refs[2:2 + nt]
        o_refs = refs[2 + nt + (after is not None):2 + nt + (after is not None) + no]
        part = lax.dot_general(a_ref[...].astype(BF16), b_ref[...].astype(BF16), dn, preferred_element_type=F32)

        def finish(r):
            res = post(r, *[t[...] for t in t_refs])
            res = res if isinstance(res, tuple) else (res,)
            for o, v in zip(o_refs, res):
                o[...] = v.astype(o.dtype)

        if nk == 1:
            finish(part)
        else:
            acc_ref = refs[-1]
            kk = pl.program_id(2)

            @pl.when(kk == 0)
            def _():
                acc_ref[...] = part

            @pl.when(kk > 0)
            def _():
                acc_ref[...] += part

            @pl.when(kk == nk - 1)
            def _():
                finish(acc_ref[...])

    a_spec = pl.BlockSpec((tk, tm), lambda i, j, q: (q, i)) if ta else pl.BlockSpec((tm, tk), lambda i, j, q: (i, q))
    b_spec = pl.BlockSpec((tn, tk), lambda i, j, q: (j, q)) if tb else pl.BlockSpec((tk, tn), lambda i, j, q: (q, j))
    o_spec = pl.BlockSpec((tm, tn), lambda i, j, q: (i, j))
    ins, specs = [_hbm(a), _hbm(b)] + [_hbm(t) for t in tiles], [a_spec, b_spec] + [o_spec] * nt
    vm = 2 * (_nbytes((tm, tk), a.dtype) + _nbytes((tk, tn), b.dtype)) + 3 * _nbytes((tm, tn), F32) \
        + _nbytes((tm, tk), BF16) + _nbytes((tk, tn), BF16) \
        + 2 * sum(_nbytes((tm, tn), t.dtype) for t in tiles) + 2 * sum(_nbytes((tm, tn), dt) for dt in dts)
    if after is not None:
        ins.append(after)
        specs.append(ANY)
    res = pl.pallas_call(
        body, name=name, grid=(m // tm, n // tn, nk), in_specs=specs, out_specs=[o_spec] * no,
        out_shape=[pltpu.HBM((m, n), dt) for dt in dts], scratch_shapes=[pltpu.VMEM((tm, tn), F32)] if nk > 1 else [],
        compiler_params=_params(vm + (4 << 20)),
    )(*ins)
    return res[0] if no == 1 else tuple(res)


def _rowwise(fn, rows, consts, outs, reds=(), *, name, tr=256, temps=6, into=None):
    rows = [r if isinstance(r, tuple) else (r, r.shape[1], 0) for r in rows]
    nrows = rows[0][0].shape[0]
    t = _pick(nrows, (tr, 128, 64, 32, 16, 8))
    nr, nc, no = len(rows), len(consts), len(outs)
    nb = 0 if into is None else 1

    def body(*refs):
        rin, cin = refs[:nr], refs[nr:nr + nc]
        oref, rref = refs[nr + nc + nb:nr + nc + nb + no], refs[nr + nc + nb + no:]
        res = fn(*[r[...] for r in rin], *[c[...] for c in cin])
        if not isinstance(res, (tuple, list)):
            res = (res,)
        for o, v in zip(oref, res[:no]):
            o[...] = v.astype(o.dtype)
        if rref:
            @pl.when(pl.program_id(0) == 0)
            def _():
                for r in rref:
                    r[...] = jnp.zeros_like(r)

            for r, v in zip(rref, res[no:]):
                r[...] += v

    in_specs = [pl.BlockSpec((t, w), functools.partial(lambda i, ci: (i, ci), ci=ci)) for (_, w, ci) in rows]
    in_specs += [pl.BlockSpec(c.shape, functools.partial(lambda i, nd: (0,) * nd, nd=c.ndim)) for c in consts]
    out_specs = [pl.BlockSpec((t, w), lambda i: (i, 0)) for (w, _) in outs]
    out_specs += [pl.BlockSpec((1, w), lambda i: (0, 0)) for w in reds]
    out_shape = [pltpu.HBM((nrows, w), dt) for (w, dt) in outs]
    out_shape += [jax.ShapeDtypeStruct((1, w), F32) for w in reds]
    widest = max([w for (_, w, _) in rows] + [w for (w, _) in outs])
    vm = 2 * sum(_nbytes((t, w), a.dtype) for (a, w, _) in rows) + 2 * sum(_nbytes((t, w), dt) for (w, dt) in outs)
    vm += temps * _nbytes((t, widest), F32) + (2 << 20)
    extra, aliases = [], {}
    if into is not None:
        buf, oi, cb = into
        out_specs[oi] = pl.BlockSpec((t, outs[oi][0]), lambda i: (i, cb))
        out_shape[oi] = pltpu.HBM(buf.shape, buf.dtype)
        in_specs.append(ANY)
        extra, aliases = [buf], {nr + nc: oi}
    res = pl.pallas_call(
        body, name=name, grid=(nrows // t,), in_specs=in_specs, out_specs=out_specs, out_shape=out_shape,
        input_output_aliases=aliases, compiler_params=_params(vm),
    )(*[_hbm(a) for (a, _, _) in rows], *consts, *extra)
    return list(res)


def _sb_tiles(s, tq, tk):
    tq = _pick(s, (tq, 256, 128))
    tk = _pick(tq, (tk, 128))
    return tq, tk, tq // tk


def _sb_fwd(zm, heads, *, name, tq=512, tk=256):
    s = zm.shape[0]
    tq, tk, nd = _sb_tiles(s, tq, tk)
    scale = SB_HD ** -0.5

    def body(q_ref, k_ref, v_ref, o_ref, a_out, stage, sem):
        h, i = pl.program_id(0), pl.program_id(1)
        qb = (q_ref[...] * scale).astype(BF16)
        r = lax.broadcasted_iota(jnp.int32, (tq, tk), 0)
        c = lax.broadcasted_iota(jnp.int32, (tq, tk), 1)
        ur = lax.broadcasted_iota(jnp.int32, (tk, tk), 0)
        uc = lax.broadcasted_iota(jnp.int32, (tk, tk), 1)
        usuf = (ur > uc).astype(BF16)

        def out_copy(slot, j):
            return pltpu.make_async_copy(stage.at[slot], a_out.at[h, i, j], sem.at[slot])

        def tile(j, carry, causal, slot, reuse):
            acc, cl = carry
            if reuse is True:
                out_copy(slot, 0).wait()
            elif reuse is not None:
                @pl.when(reuse)
                def _():
                    out_copy(slot, 0).wait()
            rows = pl.ds(pl.multiple_of(j * tk, tk), tk)
            kb = k_ref[rows, :].astype(BF16)
            vb = v_ref[rows, :].astype(BF16)
            z = lax.dot_general(qb, kb, NT, preferred_element_type=F32)
            lsig = -_softplus(z)
            l = lsig if causal is None else jnp.where(causal, lsig, 0.0)
            loga = z + lsig + _dot01(l, usuf) + cl
            if causal is not None:
                loga = jnp.where(causal, loga, NEG)
            ab = jnp.exp(loga).astype(BF16)
            acc = acc + lax.dot_general(ab, vb, NN, preferred_element_type=F32)
            stage[slot] = ab
            out_copy(slot, j).start()
            return acc, cl + jnp.sum(l, axis=1, keepdims=True)

        carry = (jnp.zeros((tq, SB_HD), F32), jnp.zeros((tq, 1), F32))
        for n, dd in enumerate(range(nd - 1, -1, -1)):
            carry = tile(i * nd + dd, carry, c + dd * tk < r, n, None)

        if nd == 2:
            slots = 4

            def pair(n, cr):
                s0 = (2 + 2 * n) % slots

                @pl.when(n >= 1)
                def _():
                    out_copy(s0, 0).wait()
                    out_copy(s0 + 1, 0).wait()

                return tile(i * nd - 2 - 2 * n, tile(i * nd - 1 - 2 * n, cr, None, s0, None), None, s0 + 1, None)

            acc, _ = lax.fori_loop(0, i, pair, carry)
        else:
            slots = SB_SLOTS

            def rest(n, cr):
                return tile(i * nd - 1 - n, cr, None, (nd + n) % SB_SLOTS, nd + n >= SB_SLOTS)

            acc, _ = lax.fori_loop(0, i * nd, rest, carry)
        total = (i + 1) * nd
        for back in range(1, slots + 1):
            @pl.when(total >= back)
            def _():
                out_copy((total - back) % slots, 0).wait()

        o_ref[...] = acc.astype(o_ref.dtype)

    assert nd <= SB_SLOTS
    blk = lambda off: pl.BlockSpec((s, SB_HD), functools.partial(lambda h, i, off: (0, off + h), off=off))
    return pl.pallas_call(
        body, name=name, grid=(heads, s // tq),
        in_specs=[pl.BlockSpec((tq, SB_HD), lambda h, i: (i, h)), blk(heads), blk(2 * heads)],
        out_specs=[pl.BlockSpec((tq, SB_HD), lambda h, i: (i, h)), ANY],
        out_shape=[pltpu.HBM((s, heads * SB_HD), BF16), pltpu.HBM((heads, s // tq, s // tk, tq, tk), BF16)],
        scratch_shapes=[pltpu.VMEM((SB_SLOTS, tq, tk), BF16), pltpu.SemaphoreType.DMA((SB_SLOTS,))],
        compiler_params=_params(8 * s * SB_HD * 4 + 24 * tq * tk * 4 + (8 << 20)),
    )(_hbm(zm), _hbm(zm), _hbm(zm))


def _sb_bwd(zm, dy, a_all, dz, after, heads, *, name, tq=512, tk=256):
    s = zm.shape[0]
    tq, tk, nd = _sb_tiles(s, tq, tk)
    nq = s // tq
    scale = SB_HD ** -0.5

    def body(q_ref, k_ref, v_ref, do_ref, a_in, dz_ref, after_ref, dq_ref, dk_ref, dv_ref, dka, dva, abuf, sem):
        h, i = pl.program_id(0), pl.program_id(1)

        @pl.when(i == 0)
        def _():
            dka[...] = jnp.zeros_like(dka)
            dva[...] = jnp.zeros_like(dva)

        qb = (q_ref[...] * scale).astype(BF16)
        dob = do_ref[...].astype(BF16)
        qb_t = (q_ref[...] * scale).T.astype(BF16)
        dob_t = do_ref[...].astype(F32).T.astype(BF16)
        r = lax.broadcasted_iota(jnp.int32, (tq, tk), 0)
        c = lax.broadcasted_iota(jnp.int32, (tq, tk), 1)
        ur = lax.broadcasted_iota(jnp.int32, (tk, tk), 0)
        uc = lax.broadcasted_iota(jnp.int32, (tk, tk), 1)
        uexcl = (ur < uc).astype(BF16)

        def fetch(j, slot):
            return pltpu.make_async_copy(a_in.at[h, i, j], abuf.at[slot], sem.at[slot])

        total = (i + 1) * nd
        ahead = SB_SLOTS - 1 - (nd == 2)

        def arrive(j):
            fetch(j, j % SB_SLOTS).wait()

            @pl.when(j + ahead < total)
            def _():
                fetch(j + ahead, (j + ahead) % SB_SLOTS).start()

        def tile(j, carry, causal, sync=True):
            dq, cg = carry
            slot = j % SB_SLOTS
            if sync:
                arrive(j)
            rows = pl.ds(pl.multiple_of(j * tk, tk), tk)
            kb = k_ref[rows, :].astype(BF16)
            vb = v_ref[rows, :].astype(BF16)
            z = lax.dot_general(qb, kb, NT, preferred_element_type=F32)
            sig = 1.0 / (1.0 + jnp.exp(-z))
            ab = abuf[slot]
            g = ab.astype(F32) * lax.dot_general(dob, vb, NT, preferred_element_type=F32)
            p = cg + lax.dot_general(g.astype(BF16), uexcl, NN, preferred_element_type=F32)
            dz = g - sig * (g + p)
            if causal is not None:
                dz = jnp.where(causal, dz, 0.0)
            dzb = dz.astype(BF16)
            dva[j] += lax.dot_general(dob_t, ab, NN, preferred_element_type=F32)
            dka[j] += lax.dot_general(qb_t, dzb, NN, preferred_element_type=F32)
            dq = dq + lax.dot_general(dzb, kb, NN, preferred_element_type=F32)
            return dq, cg + jnp.sum(g, axis=1, keepdims=True)

        for first in range(ahead):
            @pl.when(first < total)
            def _():
                fetch(first, first).start()

        init = (jnp.zeros((tq, SB_HD), F32), jnp.zeros((tq, 1), F32))
        if nd == 2:
            def pair(n, cr):
                arrive(2 * n)
                arrive(2 * n + 1)
                return tile(2 * n + 1, tile(2 * n, cr, None, False), None, False)

            carry = lax.fori_loop(0, i, pair, init)
        else:
            carry = lax.fori_loop(0, i * nd, lambda j, cr: tile(j, cr, None), init)
        for dd in range(nd):
            carry = tile(i * nd + dd, carry, c + dd * tk < r)
        dq_ref[...] = (carry[0] * scale).astype(dq_ref.dtype)

        @pl.when(i == nq - 1)
        def _():
            for jj in range(s // tk):
                dk_ref[jj * tk:(jj + 1) * tk, :] = dka[jj].T.astype(dk_ref.dtype)
                dv_ref[jj * tk:(jj + 1) * tk, :] = dva[jj].T.astype(dv_ref.dtype)

    blk = lambda off: pl.BlockSpec((s, SB_HD), functools.partial(lambda h, i, off: (0, off + h), off=off))
    tile_spec = pl.BlockSpec((tq, SB_HD), lambda h, i: (i, h))
    full = pltpu.HBM((s, heads * SB_HD), BF16)
    return pl.pallas_call(
        body, name=name, grid=(heads, nq),
        in_specs=[tile_spec, blk(heads), blk(2 * heads), tile_spec, ANY, ANY, ANY],
        out_specs=[tile_spec, blk(0), blk(0)],
        out_shape=[pltpu.HBM(dz.shape, dz.dtype), full, full],
        input_output_aliases={5: 0},
        scratch_shapes=[pltpu.VMEM((s // tk, SB_HD, tk), F32), pltpu.VMEM((s // tk, SB_HD, tk), F32),
                        pltpu.VMEM((SB_SLOTS, tq, tk), BF16), pltpu.SemaphoreType.DMA((SB_SLOTS,))],
        compiler_params=_params(12 * s * SB_HD * 4 + 32 * tq * tk * 4 + (8 << 20)),
    )(_hbm(zm), _hbm(zm), _hbm(zm), _hbm(dy), a_all, dz, after)


def _conv_taps(u, w_ref, rows_i):
    taps = []
    for j in range(CONV_W):
        sh = CONV_W - 1 - j
        if sh == 0:
            taps.append(u)
        else:
            taps.append(jnp.where(rows_i >= sh, pltpu.roll(u, sh, 0), 0.0))
    return taps


def _conv_fwd(zm, col0, width, cw, cb, *, name):
    s = zm.shape[0]
    bw = _pick(width, (LANES,))
    off = col0 // bw

    def body(u_ref, w_ref, b_ref, o_ref):
        u = u_ref[...]
        rows_i = lax.broadcasted_iota(jnp.int32, u.shape, 0)
        acc = jnp.broadcast_to(b_ref[...], u.shape)
        for j, tp in enumerate(_conv_taps(u, w_ref, rows_i)):
            acc = acc + tp * w_ref[j:j + 1, :]
        o_ref[...] = acc * _sigmoid(acc)

    return pl.pallas_call(
        body, name=name, grid=(width // bw,),
        in_specs=[pl.BlockSpec((s, bw), lambda j: (0, off + j)), pl.BlockSpec((CONV_W, bw), lambda j: (0, j)),
                  pl.BlockSpec((1, bw), lambda j: (0, j))],
        out_specs=pl.BlockSpec((s, bw), lambda j: (0, j)),
        out_shape=pltpu.HBM((s, width), F32),
        compiler_params=_params(12 * s * bw * 4 + (4 << 20)),
    )(_hbm(zm), cw, cb)


def _conv_bwd(zm, col0, width, cw, cb, dqk, dz, *, name):
    s = zm.shape[0]
    bw = _pick(width, (LANES,))
    off = col0 // bw

    def body(u_ref, w_ref, b_ref, d_ref, dz_ref, du_ref, dw_ref, db_ref):
        u = u_ref[...]
        rows_i = lax.broadcasted_iota(jnp.int32, u.shape, 0)
        taps = _conv_taps(u, w_ref, rows_i)
        acc = jnp.broadcast_to(b_ref[...], u.shape)
        for j, tp in enumerate(taps):
            acc = acc + tp * w_ref[j:j + 1, :]
        sg = _sigmoid(acc)
        dc = d_ref[...] * (sg * (1.0 + acc * (1.0 - sg)))
        du = jnp.zeros_like(u)
        for j in range(CONV_W):
            sh = CONV_W - 1 - j
            if sh == 0:
                du = du + dc * w_ref[j:j + 1, :]
            else:
                du = du + jnp.where(rows_i < s - sh, pltpu.roll(dc, s - sh, 0), 0.0) * w_ref[j:j + 1, :]
            dw_ref[j:j + 1, :] = jnp.sum(dc * taps[j], axis=0, keepdims=True)
        du_ref[...] = du.astype(du_ref.dtype)
        db_ref[...] = jnp.sum(dc, axis=0, keepdims=True)

    return pl.pallas_call(
        body, name=name, grid=(width // bw,),
        in_specs=[pl.BlockSpec((s, bw), lambda j: (0, off + j)), pl.BlockSpec((CONV_W, bw), lambda j: (0, j)),
                  pl.BlockSpec((1, bw), lambda j: (0, j)), pl.BlockSpec((s, bw), lambda j: (0, j)), ANY],
        out_specs=[pl.BlockSpec((s, bw), lambda j: (0, off + j)), pl.BlockSpec((CONV_W, bw), lambda j: (0, j)),
                   pl.BlockSpec((1, bw), lambda j: (0, j))],
        out_shape=[pltpu.HBM(dz.shape, dz.dtype), pltpu.HBM((CONV_W, width), F32),
                   pltpu.HBM((1, width), F32)],
        input_output_aliases={4: 0},
        compiler_params=_params(20 * s * bw * 4 + (4 << 20)),
    )(_hbm(zm), cw, cb, _hbm(dqk), dz)


def _ml_gates(gcol_ref, grow_ref):
    l = CHUNK
    r = lax.broadcasted_iota(jnp.int32, (l, l), 0)
    c = lax.broadcasted_iota(jnp.int32, (l, l), 1)
    gcol = gcol_ref[...]
    grow = grow_ref[0]
    bcol = _u01dot((c <= r).astype(BF16), gcol)
    brow = _dot01(grow, (r <= c).astype(BF16))
    return gcol, grow, bcol, brow, r >= c


def _ml_chunk(h, dh, mq_ref, mk_ref, v_ref, gates, cp, n_prev, m_prev):
    gcol, grow, bcol, brow, tri = gates
    l = CHUNK
    sl = slice(h * dh, (h + 1) * dh)
    qc = mq_ref[:, sl]
    kc = mk_ref[:, sl] * (dh ** -0.5)
    vc = v_ref[:, sl]
    i_row = grow[h:h + 1, :]
    i_col = gcol[:, h:h + 1]
    b_col = bcol[:, ML_HEADS + h:ML_HEADS + h + 1]
    b_row = brow[ML_HEADS + h:ML_HEADS + h + 1, :]
    b_end = b_col[l - 1:l, :]
    d = jnp.where(tri, b_col - b_row + i_row, -jnp.inf)
    m_inter = b_col + m_prev
    m_t = jnp.maximum(m_inter, jnp.max(d, axis=1, keepdims=True))
    w = jnp.exp(d - m_t)
    s_inter = jnp.exp(m_inter - m_t)
    qb, kb, vb = qc.astype(BF16), kc.astype(BF16), vc.astype(BF16)
    cpb = cp.astype(BF16)
    a = lax.dot_general(qb, kb, NT, preferred_element_type=F32)
    sc = a * w
    qcp = lax.dot_general(qb, cpb, NT, preferred_element_type=F32)
    qn = jnp.sum(qc * n_prev, axis=1, keepdims=True)
    num = lax.dot_general(sc.astype(BF16), vb, NN, preferred_element_type=F32) + s_inter * qcp
    den = jnp.sum(sc, axis=1, keepdims=True) + s_inter * qn
    floor = jnp.exp(-m_t)
    dnm = jnp.maximum(jnp.abs(den), floor)
    g_col = b_end - b_col + i_col
    g_row = b_end - b_row + i_row
    m_new = jnp.maximum(b_end + m_prev, jnp.max(g_row, axis=1, keepdims=True))
    decay = jnp.exp(b_end + m_prev - m_new)
    wk = jnp.exp(g_col - m_new)
    return dict(qc=qc, kc=kc, vc=vc, qb=qb, kb=kb, vb=vb, cpb=cpb, w=w, s_inter=s_inter, a=a, sc=sc, qcp=qcp, qn=qn,
                num=num, den=den, floor=floor, dnm=dnm, m_new=m_new, decay=decay, wk=wk, sl=sl)


def _ml_fwd(mqk, zm, vcol, gcol, grow, d_model, *, name):
    s = zm.shape[0]
    nc = s // CHUNK
    dh = d_model // ML_HEADS
    hh = ML_HEADS

    def body(mq_ref, mk_ref, v_ref, gcol_ref, grow_ref, h_ref, cs_ref, ns_ref, ms_ref, c_s, n_s, m_s):
        @pl.when(pl.program_id(0) == 0)
        def _():
            c_s[...] = jnp.zeros_like(c_s)
            n_s[...] = jnp.zeros_like(n_s)
            m_s[...] = jnp.zeros_like(m_s)

        gates = _ml_gates(gcol_ref, grow_ref)
        for h in range(hh):
            cp, n_prev, m_prev = c_s[h], n_s[h], m_s[h][:, 0:1]
            cs_ref[0, h] = cp
            ns_ref[0, h] = n_prev
            ms_ref[0, h] = m_s[h]
            f = _ml_chunk(h, dh, mq_ref, mk_ref, v_ref, gates, cp, n_prev, m_prev)
            h_ref[:, f["sl"]] = f["num"] / f["dnm"]
            c_s[h] = f["decay"] * cp + lax.dot_general((f["vc"] * f["wk"]).astype(BF16), f["kb"], TN,
                                                       preferred_element_type=F32)
            n_s[h] = f["decay"] * n_prev + jnp.sum(f["wk"] * f["kc"], axis=0, keepdims=True)
            m_s[h] = jnp.broadcast_to(f["m_new"], (1, LANES))

    dblk = d_model
    return pl.pallas_call(
        body, name=name, grid=(nc,),
        in_specs=[pl.BlockSpec((CHUNK, dblk), lambda c: (c, 0)), pl.BlockSpec((CHUNK, dblk), lambda c: (c, 1)),
                  pl.BlockSpec((CHUNK, dblk), lambda c: (c, vcol // dblk)),
                  pl.BlockSpec((CHUNK, LANES), lambda c: (c, 0)), pl.BlockSpec((1, 8, CHUNK), lambda c: (c, 0, 0))],
        out_specs=[pl.BlockSpec((CHUNK, dblk), lambda c: (c, 0)),
                   pl.BlockSpec((1, hh, dh, dh), lambda c: (c, 0, 0, 0)),
                   pl.BlockSpec((1, hh, 1, dh), lambda c: (c, 0, 0, 0)),
                   pl.BlockSpec((1, hh, 1, LANES), lambda c: (c, 0, 0, 0))],
        out_shape=[pltpu.HBM((s, d_model), F32), pltpu.HBM((nc, hh, dh, dh), F32),
                   pltpu.HBM((nc, hh, 1, dh), F32), pltpu.HBM((nc, hh, 1, LANES), F32)],
        scratch_shapes=[pltpu.VMEM((hh, dh, dh), F32), pltpu.VMEM((hh, 1, dh), F32), pltpu.VMEM((hh, 1, LANES), F32)],
        compiler_params=_params(8 * hh * dh * dh * 4 + (16 << 20)),
    )(_hbm(mqk), _hbm(mqk), _hbm(zm), _hbm(gcol), _hbm(grow))


def _ml_bwd(mqk, zm, vcol, gcol, grow, cs, ns, ms, dhm, dz, d_model, *, name):
    s = zm.shape[0]
    nc = s // CHUNK
    dh = d_model // ML_HEADS
    hh = ML_HEADS
    l = CHUNK

    def body(mq_ref, mk_ref, v_ref, gcol_ref, grow_ref, cs_ref, ns_ref, ms_ref, dh_ref, dz_ref,
             dqk_ref, dv_ref, dgc_ref, dgr_ref, dc_s, dn_s):
        @pl.when(pl.program_id(0) == 0)
        def _():
            dc_s[...] = jnp.zeros_like(dc_s)
            dn_s[...] = jnp.zeros_like(dn_s)

        gates = _ml_gates(gcol_ref, grow_ref)
        lane = lax.broadcasted_iota(jnp.int32, (l, LANES), 1)
        rowi = lax.broadcasted_iota(jnp.int32, (8, l), 0)
        lastrow = lax.broadcasted_iota(jnp.int32, (l, 1), 0) == l - 1
        dgc = jnp.zeros((l, LANES), F32)
        dgr = jnp.zeros((8, l), F32)
        for h in range(hh):
            cp, n_prev, m_prev = cs_ref[0, h], ns_ref[0, h], ms_ref[0, h][:, 0:1]
            f = _ml_chunk(h, dh, mq_ref, mk_ref, v_ref, gates, cp, n_prev, m_prev)
            dC, dn = dc_s[h], dn_s[h]
            dhv = dh_ref[:, f["sl"]]
            dnum = dhv / f["dnm"]
            hv = f["num"] / f["dnm"]
            ddnm = -jnp.sum(dhv * hv, axis=1, keepdims=True) / f["dnm"]
            dden = jnp.where(jnp.abs(f["den"]) >= f["floor"], ddnm * jnp.sign(f["den"]), 0.0)
            dnb = dnum.astype(BF16)
            dsc = lax.dot_general(dnb, f["vb"], NT, preferred_element_type=F32) + dden
            dvc = lax.dot_general(f["sc"].astype(BF16), dnb, TN, preferred_element_type=F32)
            ds_inter = jnp.sum(dnum * f["qcp"], axis=1, keepdims=True) + dden * f["qn"]
            sdn = (f["s_inter"] * dnum).astype(BF16)
            sdd = f["s_inter"] * dden
            da = dsc * f["w"]
            dab = da.astype(BF16)
            dqc = (lax.dot_general(dab, f["kb"], NN, preferred_element_type=F32)
                   + lax.dot_general(sdn, f["cpb"], NN, preferred_element_type=F32) + sdd * n_prev)
            dcp = f["decay"] * dC + lax.dot_general(sdn, f["qb"], TN, preferred_element_type=F32)
            dnp = f["decay"] * dn + jnp.sum(sdd * f["qc"], axis=0, keepdims=True)
            vw = (f["vc"] * f["wk"]).astype(BF16)
            dCb = dC.astype(BF16)
            dkc = (lax.dot_general(dab, f["qb"], TN, preferred_element_type=F32)
                   + lax.dot_general(vw, dCb, NN, preferred_element_type=F32) + f["wk"] * dn)
            e = lax.dot_general(f["kb"], dCb, NT, preferred_element_type=F32)
            dvc = dvc + e * f["wk"]
            dwk = jnp.sum(e * f["vc"], axis=1, keepdims=True) + jnp.sum(f["kc"] * dn, axis=1, keepdims=True)
            ddecay = jnp.sum(jnp.sum(dC * cp, axis=1, keepdims=True), axis=0, keepdims=True) \
                + jnp.sum(dn * n_prev, axis=1, keepdims=True)
            dd = dsc * f["sc"]
            dlw = dwk * f["wk"]
            db_end = jnp.sum(dlw, axis=0, keepdims=True) + ddecay * f["decay"]
            di_col = dlw
            db_col = jnp.sum(dd, axis=1, keepdims=True) + ds_inter * f["s_inter"] - dlw \
                + jnp.where(lastrow, db_end, 0.0)
            cs_dd = jnp.sum(dd, axis=0, keepdims=True)
            dgc = dgc + jnp.where(lane == h, di_col, 0.0) + jnp.where(lane == hh + h, db_col, 0.0)
            dgr = dgr + jnp.where(rowi == h, cs_dd, 0.0) - jnp.where(rowi == hh + h, cs_dd, 0.0)
            dqk_ref[:, f["sl"]] = dqc
            dqk_ref[:, d_model + h * dh:d_model + (h + 1) * dh] = dkc * (dh ** -0.5)
            dv_ref[:, f["sl"]] = dvc.astype(dv_ref.dtype)
            dc_s[h] = dcp
            dn_s[h] = dnp
        dgc_ref[...] = dgc
        dgr_ref[0] = dgr

    dblk = d_model
    rev = lambda c: nc - 1 - c
    return pl.pallas_call(
        body, name=name, grid=(nc,),
        in_specs=[pl.BlockSpec((l, dblk), lambda c: (rev(c), 0)), pl.BlockSpec((l, dblk), lambda c: (rev(c), 1)),
                  pl.BlockSpec((l, dblk), lambda c: (rev(c), vcol // dblk)),
                  pl.BlockSpec((l, LANES), lambda c: (rev(c), 0)), pl.BlockSpec((1, 8, l), lambda c: (rev(c), 0, 0)),
                  pl.BlockSpec((1, hh, dh, dh), lambda c: (rev(c), 0, 0, 0)),
                  pl.BlockSpec((1, hh, 1, dh), lambda c: (rev(c), 0, 0, 0)),
                  pl.BlockSpec((1, hh, 1, LANES), lambda c: (rev(c), 0, 0, 0)),
                  pl.BlockSpec((l, dblk), lambda c: (rev(c), 0)), ANY],
        out_specs=[pl.BlockSpec((l, 2 * dblk), lambda c: (rev(c), 0)),
                   pl.BlockSpec((l, dblk), lambda c: (rev(c), vcol // dblk)),
                   pl.BlockSpec((l, LANES), lambda c: (rev(c), 0)),
                   pl.BlockSpec((1, 8, l), lambda c: (rev(c), 0, 0))],
        out_shape=[pltpu.HBM((s, 2 * d_model), F32),
                   pltpu.HBM(dz.shape, dz.dtype), pltpu.HBM((s, LANES), F32),
                   pltpu.HBM((nc, 8, l), F32)],
        input_output_aliases={9: 1},
        scratch_shapes=[pltpu.VMEM((hh, dh, dh), F32), pltpu.VMEM((hh, 1, dh), F32)],
        compiler_params=_params(10 * hh * dh * dh * 4 + (16 << 20)),
    )(*[_hbm(a) for a in (mqk, mqk, zm, gcol, grow, cs, ns, ms, dhm)], dz)


def _xa_fwd(zm, qcol, kv, gq, gk, d_model, *, name, tq=512):
    s = zm.shape[0]
    nm = kv.shape[0]
    dh = d_model // X_HEADS
    tq = _pick(s, (tq, 128, 64))
    scale = dh ** -0.5

    def body(q_ref, k_ref, v_ref, gq_ref, gk_ref, o_ref):
        qn = _rms_fwd(q_ref[...], gq_ref[...])
        kn = _rms_fwd(k_ref[...], gk_ref[...])
        lg = _dot(qn, kn, NT) * scale
        lg = lg - jnp.max(lg, axis=1, keepdims=True)
        p = jnp.exp(lg)
        p = p / jnp.sum(p, axis=1, keepdims=True)
        o_ref[...] = _dot(p, v_ref[...], NN).astype(o_ref.dtype)

    return pl.pallas_call(
        body, name=name, grid=(X_HEADS, s // tq),
        in_specs=[pl.BlockSpec((tq, dh), lambda h, i: (i, qcol // dh + h)), pl.BlockSpec((nm, dh), lambda h, i: (0, h)),
                  pl.BlockSpec((nm, dh), lambda h, i: (0, X_HEADS + h)),
                  pl.BlockSpec((1, dh), lambda h, i: (0, 0)), pl.BlockSpec((1, dh), lambda h, i: (0, 0))],
        out_specs=pl.BlockSpec((tq, dh), lambda h, i: (i, h)),
        out_shape=pltpu.HBM((s, d_model), BF16),
        compiler_params=_params(32 << 20),
    )(_hbm(zm), _hbm(kv), _hbm(kv), gq, gk)


def _xa_bwd(zm, qcol, kv, gq, gk, dy, dz, d_model, *, name, tq=512):
    s = zm.shape[0]
    nm = kv.shape[0]
    dh = d_model // X_HEADS
    tq = _pick(s, (tq, 128, 64))
    nq = s // tq
    scale = dh ** -0.5

    def body(q_ref, k_ref, v_ref, gq_ref, gk_ref, do_ref, dz_ref, dq_ref, dkn_ref, dv_ref, dgq_ref):
        h, i = pl.program_id(0), pl.program_id(1)

        @pl.when(i == 0)
        def _():
            dkn_ref[...] = jnp.zeros_like(dkn_ref)
            dv_ref[...] = jnp.zeros_like(dv_ref)

        @pl.when((i == 0) & (h == 0))
        def _():
            dgq_ref[...] = jnp.zeros_like(dgq_ref)

        q = q_ref[...]
        qn = _rms_fwd(q, gq_ref[...])
        kn = _rms_fwd(k_ref[...], gk_ref[...])
        lg = _dot(qn, kn, NT) * scale
        lg = lg - jnp.max(lg, axis=1, keepdims=True)
        p = jnp.exp(lg)
        p = p / jnp.sum(p, axis=1, keepdims=True)
        do = do_ref[...]
        dv_ref[...] += _dot(p, do, TN)
        dp = _dot(do, v_ref[...], NT)
        dlg = p * (dp - jnp.sum(dp * p, axis=1, keepdims=True)) * scale
        dqn = _dot(dlg, kn, NN)
        dkn_ref[...] += _dot(dlg, qn, TN)
        dq, dgq = _rms_bwd(q, gq_ref[...], dqn)
        dq_ref[...] = dq.astype(dq_ref.dtype)
        dgq_ref[...] += jnp.sum(dgq, axis=0, keepdims=True)

    return pl.pallas_call(
        body, name=name, grid=(X_HEADS, nq),
        in_specs=[pl.BlockSpec((tq, dh), lambda h, i: (i, qcol // dh + h)), pl.BlockSpec((nm, dh), lambda h, i: (0, h)),
                  pl.BlockSpec((nm, dh), lambda h, i: (0, X_HEADS + h)),
                  pl.BlockSpec((1, dh), lambda h, i: (0, 0)), pl.BlockSpec((1, dh), lambda h, i: (0, 0)),
                  pl.BlockSpec((tq, dh), lambda h, i: (i, h)), ANY],
        out_specs=[pl.BlockSpec((tq, dh), lambda h, i: (i, qcol // dh + h)),
                   pl.BlockSpec((nm, dh), lambda h, i: (0, h)),
                   pl.BlockSpec((nm, dh), lambda h, i: (0, h)), pl.BlockSpec((1, dh), lambda h, i: (0, 0))],
        out_shape=[pltpu.HBM(dz.shape, dz.dtype), pltpu.HBM((nm, d_model), F32),
                   pltpu.HBM((nm, d_model), F32), pltpu.HBM((1, dh), F32)],
        input_output_aliases={6: 0},
        compiler_params=_params(32 << 20),
    )(_hbm(zm), _hbm(kv), _hbm(kv), gq, gk, _hbm(dy), dz)


def _place():
    return lax.axis_index("x"), lax.axis_index("y"), lax.axis_index("c")


ANY = pl.BlockSpec(memory_space=pl.ANY)


def _allgather_two_level(big, small, *, name, chunk_rows=64):
    r, cc = big.shape
    half = r // 2
    nr = _pick(half, (chunk_rows, 32, 16))
    nq = half // nr

    def body(big_ref, small_ref, obig, osmall, land, passed, send, recv, fsend, frecv, out_a, out_b, ssend, srecv, loc):
        x, y, c = _place()
        k = 2 * x + y
        chips = [(1 - x, y), (x, 1 - y), (1 - x, 1 - y)]
        slots = [2 * px + py for px, py in chips]
        local = [pltpu.make_async_copy(big_ref, obig.at[k], loc.at[0]),
                 pltpu.make_async_copy(small_ref, osmall.at[k], loc.at[1])]
        for cp in local:
            cp.start()

        def rows(h, q):
            return pl.ds(pl.multiple_of(h * half + q * nr, nr), nr)

        def chunk(q):
            return pl.ds(q * nr, nr)

        def over_ici(j, q):
            return pltpu.make_async_remote_copy(
                src_ref=big_ref.at[rows(c, q)], dst_ref=land.at[j, chunk(q)], send_sem=send.at[nq * j + q],
                recv_sem=recv.at[nq * j + q], device_id=(chips[j][0], chips[j][1], c), device_id_type=MESH)

        def to_sibling(j, q):
            return pltpu.make_async_remote_copy(
                src_ref=land.at[j, chunk(q)], dst_ref=passed.at[j, chunk(q)], send_sem=fsend.at[nq * j + q],
                recv_sem=frecv.at[nq * j + q], device_id=(x, y, 1 - c), device_id_type=MESH)

        def small_copy(j, slot):
            return pltpu.make_async_remote_copy(
                src_ref=small_ref, dst_ref=osmall.at[slot], send_sem=ssend.at[j], recv_sem=srecv.at[j],
                device_id=(chips[j][0], chips[j][1], c), device_id_type=MESH)

        for q in range(nq):
            for j in range(3):
                over_ici(j, q).start()
        for j in range(3):
            small_copy(j, k).start()
        for q in range(nq):
            for j in range(3):
                over_ici(j, q).wait_recv()
                to_sibling(j, q).start()
                cp = pltpu.make_async_copy(land.at[j, chunk(q)], obig.at[slots[j], rows(c, q)], out_a.at[nq * j + q])
                cp.start()
                local.append(cp)
        for q in range(nq):
            for j in range(3):
                to_sibling(j, q).wait_recv()
                cp = pltpu.make_async_copy(passed.at[j, chunk(q)], obig.at[slots[j], rows(1 - c, q)],
                                           out_b.at[nq * j + q])
                cp.start()
                local.append(cp)
        for j in range(3):
            small_copy(j, slots[j]).wait_recv()
            small_copy(j, k).wait_send()
        for q in range(nq):
            for j in range(3):
                over_ici(j, q).wait_send()
                to_sibling(j, q).wait_send()
        for cp in local:
            cp.wait()

    stage = 2 * _nbytes((3, half, cc), big.dtype)
    return pl.pallas_call(
        body, name=name, in_specs=[ANY] * 2, out_specs=[ANY] * 2,
        out_shape=[pltpu.HBM((4,) + big.shape, big.dtype), pltpu.HBM((4,) + small.shape, small.dtype)],
        scratch_shapes=[pltpu.VMEM((3, half, cc), big.dtype), pltpu.VMEM((3, half, cc), big.dtype)]
        + [pltpu.SemaphoreType.DMA((3 * nq,))] * 6
        + [pltpu.SemaphoreType.DMA((3,)), pltpu.SemaphoreType.DMA((3,)), pltpu.SemaphoreType.DMA((2,))],
        compiler_params=_params(stage + stage // 8 + (4 << 20)),
    )(big, small)


HBM_SPEC = pl.BlockSpec(memory_space=pltpu.HBM)
SEM_SPEC = pl.BlockSpec(memory_space=pltpu.SEMAPHORE)
EFFECT = pltpu.SideEffectType.DATAFLOW_SIDE_EFFECTING


def _split_copies(kind, srcs, lands, send, recv):
    x, y, c = _place()
    if kind == "quarters":
        peers = [(1 - x, y, c), (x, 1 - y, c), (1 - x, 1 - y, c)]
    else:
        peers = [(x ^ ((j >> 2) & 1), y ^ ((j >> 1) & 1), c ^ (j & 1)) for j in range(1, 8)]
    npeer = len(peers)
    out = []
    for t in range(len(srcs)):
        for j, (px, py, pc) in enumerate(peers):
            if kind == "quarters":
                src, mine, theirs = srcs[t], 2 * x + y, 2 * px + py
            else:
                src, mine, theirs = srcs[t].at[2 * px + py, pc], 4 * x + 2 * y + c, 4 * px + 2 * py + pc
            mk = functools.partial(
                pltpu.make_async_remote_copy, src_ref=src, send_sem=send.at[npeer * t + j],
                recv_sem=recv.at[npeer * t + j], device_id=(px, py, pc), device_id_type=MESH)
            out.append((functools.partial(mk, dst_ref=lands[t].at[mine]),
                        functools.partial(mk, dst_ref=lands[t].at[theirs])))
    return out


def _split_start(kind, srcs, land_shapes, after, *, name):
    n = len(srcs)
    ncopies = n * (3 if kind == "quarters" else 7)

    def body(*refs):
        ins, lands = refs[:n], refs[n:2 * n]
        send, recv = refs[2 * n + 1], refs[2 * n + 2]
        token = refs[-1]
        for start, _ in _split_copies(kind, ins, lands, send, recv):
            start().start()
        token[...] = jnp.zeros_like(token)

    lands = [_hbm(lax.empty(shp, a.dtype)) for shp, a in zip(land_shapes, srcs)]
    res = pl.pallas_call(
        body, name=name, in_specs=[HBM_SPEC] * (2 * n) + [ANY],
        out_specs=[SEM_SPEC, SEM_SPEC] + [HBM_SPEC] * (2 * n) + [pl.BlockSpec(memory_space=pltpu.VMEM)],
        out_shape=[pltpu.SemaphoreType.DMA((ncopies,)), pltpu.SemaphoreType.DMA((ncopies,))]
        + [pltpu.HBM(a.shape, a.dtype) for a in srcs] + [pltpu.HBM(shp, a.dtype) for shp, a in zip(land_shapes, srcs)]
        + [jax.ShapeDtypeStruct((8, LANES), F32)],
        input_output_aliases={i: 2 + i for i in range(2 * n)},
        compiler_params=pltpu.CompilerParams(has_side_effects=EFFECT),
    )(*[_hbm(a) for a in srcs], *lands, after)
    return res[0], res[1], list(res[2:2 + n]), list(res[2 + n:2 + 2 * n]), res[-1]


def _split_wait(kind, send, recv, srcs, lands, after, *, name):
    n = len(srcs)

    def body(*refs):
        ins, lnd = refs[:n], refs[n:2 * n]
        snd, rcv = refs[2 * n], refs[2 * n + 1]
        for start, arrive in _split_copies(kind, ins, lnd, snd, rcv):
            start().wait_send()
            arrive().wait_recv()

    res = pl.pallas_call(
        body, name=name, in_specs=[HBM_SPEC] * (2 * n) + [SEM_SPEC, SEM_SPEC] + [ANY] * len(after),
        out_specs=[HBM_SPEC] * (2 * n),
        out_shape=[pltpu.HBM(a.shape, a.dtype) for a in srcs] + [pltpu.HBM(a.shape, a.dtype) for a in lands],
        input_output_aliases={i: i for i in range(2 * n)},
        compiler_params=pltpu.CompilerParams(has_side_effects=EFFECT),
    )(*srcs, *lands, send, recv, *after)
    return list(res[n:])


def _sum8(parts, *, name):
    _, r, c = parts.shape
    t = _pick(r, (128, 64, 32, 16, 8))

    def body(p_ref, o_ref):
        acc = p_ref[0].astype(F32)
        for k in range(1, 8):
            acc = acc + p_ref[k].astype(F32)
        o_ref[...] = acc

    return pl.pallas_call(
        body, name=name, grid=(r // t,), in_specs=[pl.BlockSpec((8, t, c), lambda i: (0, i, 0))],
        out_specs=pl.BlockSpec((t, c), lambda i: (i, 0)), out_shape=pltpu.HBM((r, c), F32),
        compiler_params=_params(2 * 8 * t * c * 2 + 6 * t * c * 4 + (4 << 20)),
    )(_hbm(parts))


def _swap_halves(halves, *, name, chunk_bytes=512 * 1024):
    n = len(halves)
    items = []
    for t, a in enumerate(halves):
        r = a.shape[0]
        k = 1
        while _nbytes(a.shape, a.dtype) // k > chunk_bytes and r % (2 * k) == 0 and (r // (2 * k)) % 8 == 0:
            k *= 2
        items += [(t, q * (r // k), r // k) for q in range(k)]
    m = len(items)

    def body(*refs):
        ins, outs = refs[:n], refs[n:2 * n]
        sbuf, rbuf = refs[2 * n:3 * n], refs[3 * n:4 * n]
        send, recv, loc_own, loc_in, loc_out = refs[4 * n:]
        x, y, c = _place()
        local, stage = [], []
        for t in range(n):
            cp = pltpu.make_async_copy(ins[t], outs[t].at[c], loc_own.at[t])
            cp.start()
            local.append(cp)
        for q, (t, r0, nr) in enumerate(items):
            cp = pltpu.make_async_copy(ins[t].at[pl.ds(r0, nr)], sbuf[t].at[pl.ds(r0, nr)], loc_in.at[q])
            cp.start()
            stage.append(cp)

        def copy(q):
            t, r0, nr = items[q]
            return pltpu.make_async_remote_copy(
                src_ref=sbuf[t].at[pl.ds(r0, nr)], dst_ref=rbuf[t].at[pl.ds(r0, nr)], send_sem=send.at[q],
                recv_sem=recv.at[q], device_id=(x, y, 1 - c), device_id_type=MESH)

        for q in range(m):
            stage[q].wait()
            copy(q).start()
        for q, (t, r0, nr) in enumerate(items):
            copy(q).wait_recv()
            cp = pltpu.make_async_copy(rbuf[t].at[pl.ds(r0, nr)], outs[t].at[1 - c, pl.ds(r0, nr)], loc_out.at[q])
            cp.start()
            local.append(cp)
        for q in range(m):
            copy(q).wait_send()
        for cp in local:
            cp.wait()

    stage_bytes = 2 * sum(_nbytes(a.shape, a.dtype) for a in halves)
    return pl.pallas_call(
        body, name=name, in_specs=[ANY] * n, out_specs=[ANY] * n,
        out_shape=[pltpu.HBM((2,) + a.shape, a.dtype) for a in halves],
        scratch_shapes=[pltpu.VMEM(a.shape, a.dtype) for a in halves] * 2
        + [pltpu.SemaphoreType.DMA((m,)), pltpu.SemaphoreType.DMA((m,)), pltpu.SemaphoreType.DMA((n,)),
           pltpu.SemaphoreType.DMA((m,)), pltpu.SemaphoreType.DMA((m,))],
        compiler_params=_params(stage_bytes + (4 << 20)),
    )(*halves)


def _allreduce_small(p, after, *, name):
    r = p.shape[0]

    def body(p_ref, after_ref, o_ref, buf, send, recv):
        x, y, c = _place()
        me = 4 * x + 2 * y + c
        peers = [(x ^ ((j >> 2) & 1), y ^ ((j >> 1) & 1), c ^ (j & 1)) for j in range(1, 8)]

        def copy(j, slot):
            return pltpu.make_async_remote_copy(
                src_ref=p_ref, dst_ref=buf.at[slot], send_sem=send.at[j], recv_sem=recv.at[j],
                device_id=peers[j], device_id_type=MESH)

        for j in range(7):
            copy(j, me).start()
        buf[me] = p_ref[...]
        for j in range(7):
            px, py, pc = peers[j]
            copy(j, 4 * px + 2 * py + pc).wait_recv()
        for j in range(7):
            copy(j, me).wait_send()
        acc = buf[0]
        for k in range(1, 8):
            acc = acc + buf[k]
        o_ref[...] = acc

    vspec = pl.BlockSpec(memory_space=pltpu.VMEM)
    return pl.pallas_call(
        body, name=name, in_specs=[vspec, ANY], out_specs=vspec, out_shape=jax.ShapeDtypeStruct((r, LANES), F32),
        scratch_shapes=[pltpu.VMEM((8, r, LANES), F32), pltpu.SemaphoreType.DMA((7,)), pltpu.SemaphoreType.DMA((7,))],
    )(p, after)


def _adamw_fn(w, g, m, v):
    m = ADAM_B1 * m + (1.0 - ADAM_B1) * g
    v = ADAM_B2 * v + (1.0 - ADAM_B2) * (g * g)
    m_hat = m / (1.0 - ADAM_B1 ** ADAM_STEP)
    v_hat = v / (1.0 - ADAM_B2 ** ADAM_STEP)
    delta = -ADAM_LR * (m_hat / (jnp.sqrt(v_hat) + ADAM_EPS) + ADAM_WD * w)
    return delta, m, v


def _adamw(w, g, m, v, *, name):
    c = w.shape[1]
    return _rowwise(_adamw_fn, [w, g, m, v], [], [(c, F32)] * 3, name=name, tr=128)


def _pack(vecs, rows):
    flat = jnp.concatenate([a.reshape(-1).astype(F32) for a in vecs])
    return jnp.pad(flat, (0, rows * LANES - flat.shape[0])).reshape(rows, LANES)


def _unpack(p, like):
    flat, out, o = p.reshape(-1), [], 0
    for a in like:
        out.append(flat[o:o + a.size].reshape(a.shape))
        o += a.size
    return out


def kernel(x, mem, g_mix, w_in, b_if, b_gate, conv_w, conv_b, ml_norm_g, g_mem, w_mem_kv, q_norm_g, k_norm_g, w_sb_proj, w_ml_proj, w_x_proj, w_out, g_mlp, w_ff1, w_ff2, loss_target, m_g_mix, m_w_in, m_b_if, m_b_gate, m_conv_w, m_conv_b, m_ml_norm_g, m_g_mem, m_w_mem_kv, m_q_norm_g, m_k_norm_g, m_w_sb_proj, m_w_ml_proj, m_w_x_proj, m_w_out, m_g_mlp, m_w_ff1, m_w_ff2, v_g_mix, v_w_in, v_b_if, v_b_gate, v_conv_w, v_conv_b, v_ml_norm_g, v_g_mem, v_w_mem_kv, v_q_norm_g, v_k_norm_g, v_w_sb_proj, v_w_ml_proj, v_w_x_proj, v_w_out, v_g_mlp, v_w_ff1, v_w_ff2):
    _, s, d = x.shape
    nm = mem.shape[1]
    n_in = 4 * w_in.shape[2]
    dff = 4 * w_ff1.shape[2]
    sbh = d // SB_HD
    hh = ML_HEADS
    dh = d // hh
    nc = s // CHUNK
    assert n_in == 11 * d + 2 * hh and d % (2 * LANES) == 0 and s % LANES == 0
    x2, mem2, tgt = x[0], mem[0], loss_target[0]

    k4 = 2 * lax.axis_index("x") + lax.axis_index("y")
    me = 2 * k4 + lax.axis_index("c")
    g_first = _allgather_two_level(w_in[0].astype(BF16), conv_w[0], name="gather_w_in")
    later = [a[0].astype(BF16) for a in (w_mem_kv, w_sb_proj, w_ml_proj, w_x_proj, w_out, w_ff1, w_ff2)]
    gw_send, gw_recv, gw_src, gw_land, gw_token = _split_start(
        "quarters", later, [(4,) + a.shape for a in later], g_first[0], name="gather_rest_start")
    cols = lambda a: a.transpose(1, 0, 2).reshape(a.shape[1], 4 * a.shape[2])
    rws = lambda a: a.reshape(4 * a.shape[1], a.shape[2])
    qn = n_in // 4
    if_lo, if_hi = 7 * d, 7 * d + 2 * hh

    def cut(lo, hi):
        ks = [(k, max(lo, k * qn), min(hi, (k + 1) * qn)) for k in range(4)]
        return [g_first[0][k, :, a - k * qn:b - k * qn] for k, a, b in ks if a < b]

    w_main = jnp.concatenate(cut(0, if_lo) + cut(if_hi, n_in), axis=1)
    w_if = jnp.pad(jnp.concatenate(cut(if_lo, if_hi), axis=1), ((0, 0), (0, LANES - 2 * hh)))
    conv_wf = cols(g_first[1])
    b_if_p = jnp.pad(b_if, ((0, 0), (0, LANES - 2 * hh)))

    (hn,) = _rowwise(_rms_fwd, [x2], [g_mix], [(d, BF16)], name="norm_in", tr=512)
    zm = _mm(hn, w_main, after=gw_token, name="proj_in")
    zif = _mm(hn, w_if, name="proj_if")
    y_sb, a_sb = _sb_fwd(zm, sbh, name="sb_fwd")

    def gate_fn(z, b):
        pre = z + b
        lane = lax.broadcasted_iota(jnp.int32, pre.shape, 1)
        return jnp.where(lane < hh, pre, -_softplus(-pre))

    (gcol,) = _rowwise(gate_fn, [zif], [b_if_p], [(LANES, F32)], name="ml_gates", tr=1024)
    grow = gcol[:, :8].T.reshape(8, nc, CHUNK).transpose(1, 0, 2)
    mqk = _conv_fwd(zm, 3 * d, 2 * d, conv_wf, conv_b, name="conv_fwd")
    hm, cst, nst, mst = _ml_fwd(mqk, zm, 5 * d, gcol, grow, d, name="ml_fwd")

    def mlout_fn(hv, o, g):
        ys = [_rms_fwd(hv[:, k * dh:(k + 1) * dh], g[:, k * dh:(k + 1) * dh]) for k in range(hh)]
        return jnp.concatenate(ys, axis=1) * _sigmoid(o)

    (y_ml,) = _rowwise(mlout_fn, [hm, (zm, d, 6)], [ml_norm_g], [(d, BF16)], name="ml_out", tr=512)
    gw_land = _split_wait("quarters", gw_send, gw_recv, gw_src, gw_land, [y_ml, y_sb], name="gather_rest_wait")
    gw = [lax.dynamic_update_index_in_dim(ld, a, k4, 0) for ld, a in zip(gw_land, later)]
    w_kv, w_sbp, w_mlp, w_xp, w_o, w_f1, w_f2 = (cols(gw[0]), rws(gw[1]), rws(gw[2]), rws(gw[3]), rws(gw[4]),
                                                 cols(gw[5]), rws(gw[6]))
    (memn,) = _rowwise(_rms_fwd, [mem2], [g_mem], [(d, BF16)], name="norm_mem")
    kv = _mm(memn, w_kv, name="proj_kv")
    y_x = _xa_fwd(zm, 7 * d, kv, q_norm_g, k_norm_g, d, name="xa_fwd")
    p_sb = _mm(y_sb, w_sbp, name="proj_sb")
    p_ml = _mm(y_ml, w_mlp, name="proj_ml")
    p_x = _mm(y_x, w_xp, name="proj_x")

    def merge_fn(a, b, c, g0, g1, g2, bg):
        return (_sigmoid(g0 + bg[:, :d]) * a + _sigmoid(g1 + bg[:, d:2 * d]) * b + _sigmoid(g2 + bg[:, 2 * d:]) * c)

    gate_cols = [(zm, d, 8), (zm, d, 9), (zm, d, 10)]
    (mixed,) = _rowwise(merge_fn, [p_sb, p_ml, p_x] + gate_cols, [b_gate], [(d, BF16)], name="merge")
    x1 = _mm(mixed, w_o, tiles=[x2], name="proj_out")
    (h2,) = _rowwise(_rms_fwd, [x1], [g_mlp], [(d, BF16)], name="norm_mlp", tr=512)
    u, act = _mm(h2, w_f1, post=lambda r: (r, jnp.square(jnp.maximum(r, 0.0))), out_dtype=(F32, BF16), name="ff1")
    dy = _mm(act, w_f2, tiles=[x1, tgt], post=lambda r, xv, tv: (r + xv - tv) * (1.0 / d), name="ff2")
    (loss_cols,) = _rowwise(lambda g: (jnp.sum(g * g, axis=0, keepdims=True) * (0.5 * d),), [dy], [], [], [d],
                            name="loss", tr=1024)

    du = _mm(dy, w_f2, tb=True, tiles=[u], post=lambda r, uv: r * 2.0 * jnp.maximum(uv, 0.0), out_dtype=BF16,
             name="ff2_dx")
    dw_f2 = _mm(act, dy, ta=True, name="ff2_dw")
    dw_f1 = _mm(h2, du, ta=True, name="ff1_dw")
    dh2 = _mm(du, w_f1, tb=True, name="ff1_dx")

    def norm_bwd_fn(xv, dyv, res, g):
        dx, dg = _rms_bwd(xv, g, dyv)
        return dx + res, jnp.sum(dg, axis=0, keepdims=True)

    dx1, dg_mlp = _rowwise(norm_bwd_fn, [x1, dh2, dy], [g_mlp], [(d, F32)], [d], name="norm_mlp_bwd", tr=512)
    dmixed = _mm(dx1, w_o, tb=True, name="proj_out_dx")
    dw_o = _mm(mixed, dx1, ta=True, name="proj_out_dw")

    def merge_bwd_fn(dm, a, b, c, g0, g1, g2, bg):
        outs, dgs = [], []
        for p, g, k in ((a, g0, 0), (b, g1, 1), (c, g2, 2)):
            sg = _sigmoid(g + bg[:, k * d:(k + 1) * d])
            outs.append(dm * sg)
            dgs.append(dm * p * sg * (1.0 - sg))
        dgate = jnp.concatenate(dgs, axis=1)
        return (*outs, dgate, jnp.sum(dgate, axis=0, keepdims=True))

    dp_sb, dp_ml, dp_x, dgate, db_gate = _rowwise(
        merge_bwd_fn, [dmixed, p_sb, p_ml, p_x] + gate_cols, [b_gate], [(d, BF16)] * 3 + [(3 * d, BF16)], [3 * d],
        name="merge_bwd", tr=256)
    dw_sbp = _mm(y_sb, dp_sb, ta=True, name="proj_sb_dw")
    dw_mlp = _mm(y_ml, dp_ml, ta=True, name="proj_ml_dw")
    dw_xp = _mm(y_x, dp_x, ta=True, name="proj_x_dw")
    dy_sb = _mm(dp_sb, w_sbp, tb=True, out_dtype=BF16, name="proj_sb_dx")
    dy_ml = _mm(dp_ml, w_mlp, tb=True, name="proj_ml_dx")
    dy_x = _mm(dp_x, w_xp, tb=True, out_dtype=BF16, name="proj_x_dx")

    dzm = _hbm(lax.empty((s, 11 * d), BF16))
    dzm, dkn, dxv, dg_qn = _xa_bwd(zm, 7 * d, kv, q_norm_g, k_norm_g, dy_x, dzm, d, name="xa_bwd")

    def knorm_bwd_fn(kvv, dknv, dvv, g):
        dks, dgs = [], []
        for k in range(X_HEADS):
            sl = slice(k * dh, (k + 1) * dh)
            dk, dg = _rms_bwd(kvv[:, sl], g, dknv[:, sl])
            dks.append(dk)
            dgs.append(jnp.sum(dg, axis=0, keepdims=True))
        return jnp.concatenate(dks + [dvv], axis=1), dgs[0] + dgs[1] + dgs[2] + dgs[3]

    dkv, dg_kn = _rowwise(knorm_bwd_fn, [(kv, d, 0), dkn, dxv], [k_norm_g], [(2 * d, BF16)], [dh], name="xa_knorm_bwd")
    dw_kv = _mm(memn, dkv, ta=True, name="proj_kv_dw")
    dmemn = _mm(dkv, w_kv, tb=True, name="proj_kv_dx")

    def gmem_fn(mv, dv_, g):
        _, dg = _rms_bwd(mv, g, dv_)
        return (jnp.sum(dg, axis=0, keepdims=True),)

    (dg_mem,) = _rowwise(gmem_fn, [mem2, dmemn], [g_mem], [], [d], name="norm_mem_bwd")

    uncols = lambda a: a.reshape(a.shape[0], 4, a.shape[1] // 4).transpose(1, 0, 2)
    unrws = lambda a: a.reshape(4, a.shape[0] // 4, a.shape[1])
    to_parts = lambda q: q.astype(BF16).reshape(4, 2, q.shape[1] // 2, q.shape[2])
    early = [to_parts(q) for q in (uncols(dw_kv), unrws(dw_sbp), unrws(dw_mlp), unrws(dw_xp), unrws(dw_o),
                                   uncols(dw_f1), unrws(dw_f2))]
    ge_send, ge_recv, ge_src, ge_land, ge_token = _split_start(
        "grads", early, [(8,) + a.shape[2:] for a in early], dg_mem, name="exchange_early_start")

    def mlout_bwd_fn(dyv, hv, o, g):
        sg = _sigmoid(o)
        dn = dyv * sg
        dxs, dgs, ys = [], [], []
        for k in range(hh):
            sl = slice(k * dh, (k + 1) * dh)
            ys.append(_rms_fwd(hv[:, sl], g[:, sl]))
            dxk, dgk = _rms_bwd(hv[:, sl], g[:, sl], dn[:, sl])
            dxs.append(dxk)
            dgs.append(dgk)
        do = dyv * jnp.concatenate(ys, axis=1) * sg * (1.0 - sg)
        return jnp.concatenate(dxs, axis=1), do, jnp.sum(jnp.concatenate(dgs, axis=1), axis=0, keepdims=True)

    dhm, dzm, dg_mln = _rowwise(mlout_bwd_fn, [dy_ml, hm, (zm, d, 6)], [ml_norm_g], [(d, F32), (d, BF16)], [d],
                                name="ml_out_bwd", tr=512, into=(dzm, 1, 6))
    dmqk, dzm, dgc, dgr = _ml_bwd(mqk, zm, 5 * d, gcol, grow, cst, nst, mst, dhm, dzm, d, name="ml_bwd")
    dzm, dconv_w, dconv_b = _conv_bwd(zm, 3 * d, 2 * d, conv_wf, conv_b, dmqk, dzm, name="conv_bwd")
    dzm, dsk, dsv = _sb_bwd(zm, dy_sb, a_sb, dzm, ge_token, sbh, name="sb_bwd")
    dgr_t = jnp.pad(dgr.transpose(1, 0, 2).reshape(8, s).T, ((0, 0), (0, LANES - 8)))

    def gate_bwd_fn(a, b, z, bias):
        tot = a + b
        rows_t = tot.shape[0]
        r = lax.broadcasted_iota(jnp.int32, (rows_t, rows_t), 0)
        c = lax.broadcasted_iota(jnp.int32, (rows_t, rows_t), 1)
        sh = CHUNK.bit_length() - 1
        same_chunk = jnp.right_shift(r, sh) == jnp.right_shift(c, sh)
        dlf = _u01dot(((c >= r) & same_chunk).astype(BF16), tot)
        lane = lax.broadcasted_iota(jnp.int32, tot.shape, 1)
        dz = jnp.where(lane < hh, tot, jnp.where(lane < 2 * hh, dlf * _sigmoid(-(z + bias)), 0.0))
        return dz, jnp.sum(dz, axis=0, keepdims=True)

    dzif, db_if_p = _rowwise(gate_bwd_fn, [dgc, dgr_t, zif], [b_if_p], [(LANES, BF16)], [LANES], name="ml_gates_bwd",
                             tr=8 * CHUNK)
    for part, col in ((dsk, d), (dsv, 2 * d), (dgate, 8 * d)):
        dzm = lax.dynamic_update_slice(dzm, part, (0, col))
    dw_main = _mm(hn, dzm, ta=True, out_dtype=BF16, name="proj_in_dw")
    dw_if = _mm(hn, dzif, ta=True, out_dtype=BF16, name="proj_if_dw")

    def dw_quarter(k):
        lo, hi = k * qn, (k + 1) * qn
        segs = [(dw_main, 0, if_lo, 0), (dw_if, if_lo, if_hi, if_lo), (dw_main, if_hi, n_in, 2 * hh)]
        got = [src[:, max(lo, a) - off:min(hi, b) - off] for src, a, b, off in segs if max(lo, a) < min(hi, b)]
        return jnp.concatenate(got, axis=1)

    late = [to_parts(jnp.stack([dw_quarter(k) for k in range(4)]))]
    gl_send, gl_recv, gl_src, gl_land, gl_token = _split_start(
        "grads", late, [(8,) + a.shape[2:] for a in late], dw_if, name="exchange_late_start")
    dhn = _mm(dzm, w_main, tb=True, after=gl_token, name="proj_in_dx")
    dhn = _mm(dzif, w_if, tb=True, tiles=[dhn], name="proj_if_dx")
    dx, dg_mix = _rowwise(norm_bwd_fn, [x2, dhn, dx1], [g_mix], [(d, F32)], [d], name="norm_in_bwd", tr=512)

    own = lambda p: lax.dynamic_index_in_dim(lax.dynamic_index_in_dim(p, k4, 0, keepdims=False),
                                             lax.axis_index("c"), 0, keepdims=False)

    def finish(tag, send, recv, src, land, parts, after, ws, ms, vs):
        land = _split_wait("grads", send, recv, src, land, after, name=f"exchange_{tag}_wait")
        got = [lax.dynamic_update_index_in_dim(ld, own(p), me, 0) for ld, p in zip(land, parts)]
        halves = [_sum8(r, name=f"sum_grads_{tag}{i}") for i, r in enumerate(got)]
        both = _swap_halves(halves, name=f"swap_halves_{tag}")
        gs = [b.reshape(2 * b.shape[1], b.shape[2]) for b in both]
        return gs, [_adamw(w, g, m, v, name=f"adamw_{tag}{i}") for i, (w, g, m, v) in enumerate(zip(ws, gs, ms, vs))]

    first = lambda arrs: [a[0] for a in arrs]
    g_early, out_early = finish(
        "early", ge_send, ge_recv, ge_src, ge_land, early, [dx],
        first([w_mem_kv, w_sb_proj, w_ml_proj, w_x_proj, w_out, w_ff1, w_ff2]),
        first([m_w_mem_kv, m_w_sb_proj, m_w_ml_proj, m_w_x_proj, m_w_out, m_w_ff1, m_w_ff2]),
        first([v_w_mem_kv, v_w_sb_proj, v_w_ml_proj, v_w_x_proj, v_w_out, v_w_ff1, v_w_ff2]))
    g_late, out_late = finish(
        "late", gl_send, gl_recv, gl_src, gl_land, late, [o[0] for o in out_early],
        first([w_in]), first([m_w_in]), first([v_w_in]))
    g_big = [g[None] for g in g_late + g_early]
    big_out = [[o[None] for o in outs] for outs in out_late + out_early]

    small_g = [dg_mix, db_if_p[:, :2 * hh], db_gate, dconv_w, dconv_b, dg_mln, dg_mem, dg_qn, dg_kn, dg_mlp,
               jnp.sum(loss_cols).reshape(1, 1)]
    n_small = sum(a.size for a in small_g)
    rows = -(-n_small // (8 * LANES)) * 8
    g_small = _unpack(_allreduce_small(_pack(small_g, rows), out_late[0][0], name="allreduce_small"), small_g)
    loss = g_small[-1].reshape(())
    qw = conv_w.shape[2]
    g_conv_w = lax.dynamic_slice_in_dim(g_small[3], k4 * qw, qw, axis=1)
    g_small_w = [g_small[0], g_small[1], g_small[2], g_conv_w] + g_small[4:10]
    sm_w = [g_mix, b_if, b_gate, conv_w[0], conv_b, ml_norm_g, g_mem, q_norm_g, k_norm_g, g_mlp]
    sm_m = [m_g_mix, m_b_if, m_b_gate, m_conv_w[0], m_conv_b, m_ml_norm_g, m_g_mem, m_q_norm_g, m_k_norm_g, m_g_mlp]
    sm_v = [v_g_mix, v_b_if, v_b_gate, v_conv_w[0], v_conv_b, v_ml_norm_g, v_g_mem, v_q_norm_g, v_k_norm_g, v_g_mlp]
    n_sw = sum(a.size for a in sm_w)
    rows_w = -(-n_sw // (8 * LANES)) * 8
    sm_out = _adamw(_pack(sm_w, rows_w), _pack(g_small_w, rows_w), _pack(sm_m, rows_w), _pack(sm_v, rows_w),
                    name="adamw_small")
    sm_delta, sm_newm, sm_newv = [_unpack(p, sm_w) for p in sm_out]

    order = ["g_mix", "w_in", "b_if", "b_gate", "conv_w", "conv_b", "ml_norm_g", "g_mem", "w_mem_kv", "q_norm_g",
             "k_norm_g", "w_sb_proj", "w_ml_proj", "w_x_proj", "w_out", "g_mlp", "w_ff1", "w_ff2"]
    small_names = ["g_mix", "b_if", "b_gate", "conv_w", "conv_b", "ml_norm_g", "g_mem", "q_norm_g", "k_norm_g", "g_mlp"]
    big_names = ["w_in", "w_mem_kv", "w_sb_proj", "w_ml_proj", "w_x_proj", "w_out", "w_ff1", "w_ff2"]
    grads, deltas, new_m, new_v = {}, {}, {}, {}
    for i, nme in enumerate(small_names):
        shp = sm_w[i].shape if nme != "conv_w" else conv_w.shape
        grads[nme] = g_small_w[i].reshape(shp)
        deltas[nme], new_m[nme], new_v[nme] = (sm_delta[i].reshape(shp), sm_newm[i].reshape(shp),
                                               sm_newv[i].reshape(shp))
    for i, nme in enumerate(big_names):
        grads[nme] = g_big[i]
        deltas[nme], new_m[nme], new_v[nme] = big_out[i]
    return (loss, dx[None], *[grads[k] for k in order], *[deltas[k] for k in order], *[new_m[k] for k in order],
            *[new_v[k] for k in order])
```

```python
import functools

import jax
import jax.numpy as jnp
from jax import lax
from jax.experimental import pallas as pl
from jax.experimental.pallas import tpu as pltpu

F32 = jnp.float32
BF16 = jnp.bfloat16
MESH = pl.DeviceIdType.MESH

EPS = 1e-6
SB_HD = 128
SB_SLOTS = 8
ML_HEADS = 4
X_HEADS = 4
CHUNK = 64
CONV_W = 4
LANES = 128
ADAM_LR = 0.001
ADAM_B1 = 0.9
ADAM_B2 = 0.999
ADAM_EPS = 1e-08
ADAM_WD = 0.01
ADAM_STEP = 10
VMEM_CAP = 56 * 1024 * 1024
NEG = -1e30

NT = (((1,), (1,)), ((), ()))
NN = (((1,), (0,)), ((), ()))
TN = (((0,), (0,)), ((), ()))


def _dot(a, b, dn=NN):
    return lax.dot_general(a.astype(BF16), b.astype(BF16), dn, preferred_element_type=F32)


def _dot01(x, u, dn=NN):
    hi = x.astype(BF16)
    lo = (x - hi.astype(F32)).astype(BF16)
    return (lax.dot_general(hi, u, dn, preferred_element_type=F32)
            + lax.dot_general(lo, u, dn, preferred_element_type=F32))


def _u01dot(u, x):
    hi = x.astype(BF16)
    lo = (x - hi.astype(F32)).astype(BF16)
    return (lax.dot_general(u, hi, NN, preferred_element_type=F32)
            + lax.dot_general(u, lo, NN, preferred_element_type=F32))


def _pick(n, cands):
    for c in cands:
        if c <= n and n % c == 0:
            return c
    return n


def _nbytes(shape, dtype):
    n = 1
    for s in shape:
        n *= s
    return n * jnp.dtype(dtype).itemsize


def _params(vmem_bytes):
    return pltpu.CompilerParams(vmem_limit_bytes=int(min(VMEM_CAP, max(vmem_bytes, 16 * 1024 * 1024))))


def _hbm(a):
    return pltpu.with_memory_space_constraint(a, pltpu.HBM)


def _softplus(z):
    return jnp.maximum(z, 0.0) + jnp.log(1.0 + jnp.exp(-jnp.abs(z)))


def _sigmoid(z):
    return 1.0 / (1.0 + jnp.exp(-z))


def _rms_fwd(xv, g):
    r = lax.rsqrt(jnp.mean(xv * xv, axis=-1, keepdims=True) + EPS)
    return xv * r * g


def _rms_bwd(xv, g, dy):
    r = lax.rsqrt(jnp.mean(xv * xv, axis=-1, keepdims=True) + EPS)
    xh = xv * r
    dxh = dy * g
    dx = r * (dxh - xh * jnp.mean(dxh * xh, axis=-1, keepdims=True))
    return dx, dy * xh


def _mm(a, b, *, name, ta=False, tb=False, tiles=(), post=None, out_dtype=F32, bm=1024, bn=1024, bk=1024, after=None):
    m, k = (a.shape[1], a.shape[0]) if ta else a.shape
    n = b.shape[0] if tb else b.shape[1]
    tm = _pick(m, (bm, 512, 256, 128))
    tn = _pick(n, (bn, 512, 256, 128))
    tk = _pick(k, (bk, 512, 256, 128))
    nk = k // tk
    if (m // tm) * (n // tn) * nk < 8 and tm % 256 == 0:
        tm //= 2
    dn = (((0 if ta else 1,), (1 if tb else 0,)), ((), ()))
    dts = out_dtype if isinstance(out_dtype, tuple) else (out_dtype,)
    nt, no = len(tiles), len(dts)
    if post is None:
        post = lambda r, *ts: sum((t.astype(F32) for t in ts), r)

    def body(*refs):
        a_ref, b_ref = refs[:2]
        t_refs = refs[2:2 + nt]
        o_refs = refs[2 + nt + (after is not None):2 + nt + (after is not None) + no]
        part = lax.dot_general(a_ref[...].astype(BF16), b_ref[...].astype(BF16), dn, preferred_element_type=F32)

        def finish(r):
            res = post(r, *[t[...] for t in t_refs])
            res = res if isinstance(res, tuple) else (res,)
            for o, v in zip(o_refs, res):
                o[...] = v.astype(o.dtype)

        if nk == 1:
            finish(part)
        else:
            acc_ref = refs[-1]
            kk = pl.program_id(2)

            @pl.when(kk == 0)
            def _():
                acc_ref[...] = part

            @pl.when(kk > 0)
            def _():
                acc_ref[...] += part

            @pl.when(kk == nk - 1)
            def _():
                finish(acc_ref[...])

    a_spec = pl.BlockSpec((tk, tm), lambda i, j, q: (q, i)) if ta else pl.BlockSpec((tm, tk), lambda i, j, q: (i, q))
    b_spec = pl.BlockSpec((tn, tk), lambda i, j, q: (j, q)) if tb else pl.BlockSpec((tk, tn), lambda i, j, q: (q, j))
    o_spec = pl.BlockSpec((tm, tn), lambda i, j, q: (i, j))
    ins, specs = [_hbm(a), _hbm(b)] + [_hbm(t) for t in tiles], [a_spec, b_spec] + [o_spec] * nt
    vm = 2 * (_nbytes((tm, tk), a.dtype) + _nbytes((tk, tn), b.dtype)) + 3 * _nbytes((tm, tn), F32) \
        + _nbytes((tm, tk), BF16) + _nbytes((tk, tn), BF16) \
        + 2 * sum(_nbytes((tm, tn), t.dtype) for t in tiles) + 2 * sum(_nbytes((tm, tn), dt) for dt in dts)
    if after is not None:
        ins.append(after)
        specs.append(ANY)
    res = pl.pallas_call(
        body, name=name, grid=(m // tm, n // tn, nk), in_specs=specs, out_specs=[o_spec] * no,
        out_shape=[pltpu.HBM((m, n), dt) for dt in dts], scratch_shapes=[pltpu.VMEM((tm, tn), F32)] if nk > 1 else [],
        compiler_params=_params(vm + (4 << 20)),
    )(*ins)
    return res[0] if no == 1 else tuple(res)


def _rowwise(fn, rows, consts, outs, reds=(), *, name, tr=256, temps=6, into=None):
    rows = [r if isinstance(r, tuple) else (r, r.shape[1], 0) for r in rows]
    nrows = rows[0][0].shape[0]
    t = _pick(nrows, (tr, 128, 64, 32, 16, 8))
    nr, nc, no = len(rows), len(consts), len(outs)
    nb = 0 if into is None else 1

    def body(*refs):
        rin, cin = refs[:nr], refs[nr:nr + nc]
        oref, rref = refs[nr + nc + nb:nr + nc + nb + no], refs[nr + nc + nb + no:]
        res = fn(*[r[...] for r in rin], *[c[...] for c in cin])
        if not isinstance(res, (tuple, list)):
            res = (res,)
        for o, v in zip(oref, res[:no]):
            o[...] = v.astype(o.dtype)
        if rref:
            @pl.when(pl.program_id(0) == 0)
            def _():
                for r in rref:
                    r[...] = jnp.zeros_like(r)

            for r, v in zip(rref, res[no:]):
                r[...] += v

    in_specs = [pl.BlockSpec((t, w), functools.partial(lambda i, ci: (i, ci), ci=ci)) for (_, w, ci) in rows]
    in_specs += [pl.BlockSpec(c.shape, functools.partial(lambda i, nd: (0,) * nd, nd=c.ndim)) for c in consts]
    out_specs = [pl.BlockSpec((t, w), lambda i: (i, 0)) for (w, _) in outs]
    out_specs += [pl.BlockSpec((1, w), lambda i: (0, 0)) for w in reds]
    out_shape = [pltpu.HBM((nrows, w), dt) for (w, dt) in outs]
    out_shape += [jax.ShapeDtypeStruct((1, w), F32) for w in reds]
    widest = max([w for (_, w, _) in rows] + [w for (w, _) in outs])
    vm = 2 * sum(_nbytes((t, w), a.dtype) for (a, w, _) in rows) + 2 * sum(_nbytes((t, w), dt) for (w, dt) in outs)
    vm += temps * _nbytes((t, widest), F32) + (2 << 20)
    extra, aliases = [], {}
    if into is not None:
        buf, oi, cb = into
        out_specs[oi] = pl.BlockSpec((t, outs[oi][0]), lambda i: (i, cb))
        out_shape[oi] = pltpu.HBM(buf.shape, buf.dtype)
        in_specs.append(ANY)
        extra, aliases = [buf], {nr + nc: oi}
    res = pl.pallas_call(
        body, name=name, grid=(nrows // t,), in_specs=in_specs, out_specs=out_specs, out_shape=out_shape,
        input_output_aliases=aliases, compiler_params=_params(vm),
    )(*[_hbm(a) for (a, _, _) in rows], *consts, *extra)
    return list(res)


def _sb_tiles(s, tq, tk):
    tq = _pick(s, (tq, 256, 128))
    tk = _pick(tq, (tk, 128))
    return tq, tk, tq // tk


def _sb_fwd(zm, heads, *, name, tq=512, tk=256):
    s = zm.shape[0]
    tq, tk, nd = _sb_tiles(s, tq, tk)
    scale = SB_HD ** -0.5

    def body(q_ref, k_ref, v_ref, o_ref, a_out, stage, sem):
        h, i = pl.program_id(0), pl.program_id(1)
        qb = (q_ref[...] * scale).astype(BF16)
        r = lax.broadcasted_iota(jnp.int32, (tq, tk), 0)
        c = lax.broadcasted_iota(jnp.int32, (tq, tk), 1)
        ur = lax.broadcasted_iota(jnp.int32, (tk, tk), 0)
        uc = lax.broadcasted_iota(jnp.int32, (tk, tk), 1)
        usuf = (ur > uc).astype(BF16)

        def out_copy(slot, j):
            return pltpu.make_async_copy(stage.at[slot], a_out.at[h, i, j], sem.at[slot])

        def tile(j, carry, causal, slot, reuse):
            acc, cl = carry
            if reuse is True:
                out_copy(slot, 0).wait()
            elif reuse is not None:
                @pl.when(reuse)
                def _():
                    out_copy(slot, 0).wait()
            rows = pl.ds(pl.multiple_of(j * tk, tk), tk)
            kb = k_ref[rows, :].astype(BF16)
            vb = v_ref[rows, :].astype(BF16)
            z = lax.dot_general(qb, kb, NT, preferred_element_type=F32)
            lsig = -_softplus(z)
            l = lsig if causal is None else jnp.where(causal, lsig, 0.0)
            loga = z + lsig + _dot01(l, usuf) + cl
            if causal is not None:
                loga = jnp.where(causal, loga, NEG)
            ab = jnp.exp(loga).astype(BF16)
            acc = acc + lax.dot_general(ab, vb, NN, preferred_element_type=F32)
            stage[slot] = ab
            out_copy(slot, j).start()
            return acc, cl + jnp.sum(l, axis=1, keepdims=True)

        carry = (jnp.zeros((tq, SB_HD), F32), jnp.zeros((tq, 1), F32))
        for n, dd in enumerate(range(nd - 1, -1, -1)):
            carry = tile(i * nd + dd, carry, c + dd * tk < r, n, None)

        if nd == 2:
            slots = 4

            def pair(n, cr):
                s0 = (2 + 2 * n) % slots

                @pl.when(n >= 1)
                def _():
                    out_copy(s0, 0).wait()
                    out_copy(s0 + 1, 0).wait()

                return tile(i * nd - 2 - 2 * n, tile(i * nd - 1 - 2 * n, cr, None, s0, None), None, s0 + 1, None)

            acc, _ = lax.fori_loop(0, i, pair, carry)
        else:
            slots = SB_SLOTS

            def rest(n, cr):
                return tile(i * nd - 1 - n, cr, None, (nd + n) % SB_SLOTS, nd + n >= SB_SLOTS)

            acc, _ = lax.fori_loop(0, i * nd, rest, carry)
        total = (i + 1) * nd
        for back in range(1, slots + 1):
            @pl.when(total >= back)
            def _():
                out_copy((total - back) % slots, 0).wait()

        o_ref[...] = acc.astype(o_ref.dtype)

    assert nd <= SB_SLOTS
    blk = lambda off: pl.BlockSpec((s, SB_HD), functools.partial(lambda h, i, off: (0, off + h), off=off))
    return pl.pallas_call(
        body, name=name, grid=(heads, s // tq),
        in_specs=[pl.BlockSpec((tq, SB_HD), lambda h, i: (i, h)), blk(heads), blk(2 * heads)],
        out_specs=[pl.BlockSpec((tq, SB_HD), lambda h, i: (i, h)), ANY],
        out_shape=[pltpu.HBM((s, heads * SB_HD), BF16), pltpu.HBM((heads, s // tq, s // tk, tq, tk), BF16)],
        scratch_shapes=[pltpu.VMEM((SB_SLOTS, tq, tk), BF16), pltpu.SemaphoreType.DMA((SB_SLOTS,))],
        compiler_params=_params(8 * s * SB_HD * 4 + 24 * tq * tk * 4 + (8 << 20)),
    )(_hbm(zm), _hbm(zm), _hbm(zm))


def _sb_bwd(zm, dy, a_all, dz, after, heads, *, name, tq=512, tk=256):
    s = zm.shape[0]
    tq, tk, nd = _sb_tiles(s, tq, tk)
    nq = s // tq
    scale = SB_HD ** -0.5

    def body(q_ref, k_ref, v_ref, do_ref, a_in, dz_ref, after_ref, dq_ref, dk_ref, dv_ref, dka, dva, abuf, sem):
        h, i = pl.program_id(0), pl.program_id(1)

        @pl.when(i == 0)
        def _():
            dka[...] = jnp.zeros_like(dka)
            dva[...] = jnp.zeros_like(dva)

        qb = (q_ref[...] * scale).astype(BF16)
        dob = do_ref[...].astype(BF16)
        qb_t = (q_ref[...] * scale).T.astype(BF16)
        dob_t = do_ref[...].astype(F32).T.astype(BF16)
        r = lax.broadcasted_iota(jnp.int32, (tq, tk), 0)
        c = lax.broadcasted_iota(jnp.int32, (tq, tk), 1)
        ur = lax.broadcasted_iota(jnp.int32, (tk, tk), 0)
        uc = lax.broadcasted_iota(jnp.int32, (tk, tk), 1)
        uexcl = (ur < uc).astype(BF16)

        def fetch(j, slot):
            return pltpu.make_async_copy(a_in.at[h, i, j], abuf.at[slot], sem.at[slot])

        total = (i + 1) * nd
        ahead = SB_SLOTS - 1 - (nd == 2)

        def arrive(j):
            fetch(j, j % SB_SLOTS).wait()

            @pl.when(j + ahead < total)
            def _():
                fetch(j + ahead, (j + ahead) % SB_SLOTS).start()

        def tile(j, carry, causal, sync=True):
            dq, cg = carry
            slot = j % SB_SLOTS
            if sync:
                arrive(j)
            rows = pl.ds(pl.multiple_of(j * tk, tk), tk)
            kb = k_ref[rows, :].astype(BF16)
            vb = v_ref[rows, :].astype(BF16)
            z = lax.dot_general(qb, kb, NT, preferred_element_type=F32)
            sig = 1.0 / (1.0 + jnp.exp(-z))
            ab = abuf[slot]
            g = ab.astype(F32) * lax.dot_general(dob, vb, NT, preferred_element_type=F32)
            p = cg + lax.dot_general(g.astype(BF16), uexcl, NN, preferred_element_type=F32)
            dz = g - sig * (g + p)
            if causal is not None:
                dz = jnp.where(causal, dz, 0.0)
            dzb = dz.astype(BF16)
            dva[j] += lax.dot_general(dob_t, ab, NN, preferred_element_type=F32)
            dka[j] += lax.dot_general(qb_t, dzb, NN, preferred_element_type=F32)
            dq = dq + lax.dot_general(dzb, kb, NN, preferred_element_type=F32)
            return dq, cg + jnp.sum(g, axis=1, keepdims=True)

        for first in range(ahead):
            @pl.when(first < total)
            def _():
                fetch(first, first).start()

        init = (jnp.zeros((tq, SB_HD), F32), jnp.zeros((tq, 1), F32))
        if nd == 2:
            def pair(n, cr):
                arrive(2 * n)
                arrive(2 * n + 1)
                return tile(2 * n + 1, tile(2 * n, cr, None, False), None, False)

            carry = lax.fori_loop(0, i, pair, init)
        else:
            carry = lax.fori_loop(0, i * nd, lambda j, cr: tile(j, cr, None), init)
        if nd == 2:
            for dd in range(nd):
                arrive(i * nd + dd)
        for dd in range(nd):
            carry = tile(i * nd + dd, carry, c + dd * tk < r, nd != 2)
        dq_ref[...] = (carry[0] * scale).astype(dq_ref.dtype)

        @pl.when(i == nq - 1)
        def _():
            for jj in range(s // tk):
                dk_ref[jj * tk:(jj + 1) * tk, :] = dka[jj].T.astype(dk_ref.dtype)
                dv_ref[jj * tk:(jj + 1) * tk, :] = dva[jj].T.astype(dv_ref.dtype)

    blk = lambda off: pl.BlockSpec((s, SB_HD), functools.partial(lambda h, i, off: (0, off + h), off=off))
    tile_spec = pl.BlockSpec((tq, SB_HD), lambda h, i: (i, h))
    full = pltpu.HBM((s, heads * SB_HD), BF16)
    return pl.pallas_call(
        body, name=name, grid=(heads, nq),
        in_specs=[tile_spec, blk(heads), blk(2 * heads), tile_spec, ANY, ANY, ANY],
        out_specs=[tile_spec, blk(0), blk(0)],
        out_shape=[pltpu.HBM(dz.shape, dz.dtype), full, full],
        input_output_aliases={5: 0},
        scratch_shapes=[pltpu.VMEM((s // tk, SB_HD, tk), F32), pltpu.VMEM((s // tk, SB_HD, tk), F32),
                        pltpu.VMEM((SB_SLOTS, tq, tk), BF16), pltpu.SemaphoreType.DMA((SB_SLOTS,))],
        compiler_params=_params(12 * s * SB_HD * 4 + 32 * tq * tk * 4 + (8 << 20)),
    )(_hbm(zm), _hbm(zm), _hbm(zm), _hbm(dy), a_all, dz, after)


def _conv_taps(u, w_ref, rows_i):
    taps = []
    for j in range(CONV_W):
        sh = CONV_W - 1 - j
        if sh == 0:
            taps.append(u)
        else:
            taps.append(jnp.where(rows_i >= sh, pltpu.roll(u, sh, 0), 0.0))
    return taps


def _conv_fwd(zm, col0, width, cw, cb, *, name):
    s = zm.shape[0]
    bw = _pick(width, (LANES,))
    off = col0 // bw

    def body(u_ref, w_ref, b_ref, o_ref):
        u = u_ref[...]
        rows_i = lax.broadcasted_iota(jnp.int32, u.shape, 0)
        acc = jnp.broadcast_to(b_ref[...], u.shape)
        for j, tp in enumerate(_conv_taps(u, w_ref, rows_i)):
            acc = acc + tp * w_ref[j:j + 1, :]
        o_ref[...] = acc * _sigmoid(acc)

    return pl.pallas_call(
        body, name=name, grid=(width // bw,),
        in_specs=[pl.BlockSpec((s, bw), lambda j: (0, off + j)), pl.BlockSpec((CONV_W, bw), lambda j: (0, j)),
                  pl.BlockSpec((1, bw), lambda j: (0, j))],
        out_specs=pl.BlockSpec((s, bw), lambda j: (0, j)),
        out_shape=pltpu.HBM((s, width), F32),
        compiler_params=_params(12 * s * bw * 4 + (4 << 20)),
    )(_hbm(zm), cw, cb)


def _conv_bwd(zm, col0, width, cw, cb, dqk, dz, *, name):
    s = zm.shape[0]
    bw = _pick(width, (LANES,))
    off = col0 // bw

    def body(u_ref, w_ref, b_ref, d_ref, dz_ref, du_ref, dw_ref, db_ref):
        u = u_ref[...]
        rows_i = lax.broadcasted_iota(jnp.int32, u.shape, 0)
        taps = _conv_taps(u, w_ref, rows_i)
        acc = jnp.broadcast_to(b_ref[...], u.shape)
        for j, tp in enumerate(taps):
            acc = acc + tp * w_ref[j:j + 1, :]
        sg = _sigmoid(acc)
        dc = d_ref[...] * (sg * (1.0 + acc * (1.0 - sg)))
        du = jnp.zeros_like(u)
        for j in range(CONV_W):
            sh = CONV_W - 1 - j
            if sh == 0:
                du = du + dc * w_ref[j:j + 1, :]
            else:
                du = du + jnp.where(rows_i < s - sh, pltpu.roll(dc, s - sh, 0), 0.0) * w_ref[j:j + 1, :]
            dw_ref[j:j + 1, :] = jnp.sum(dc * taps[j], axis=0, keepdims=True)
        du_ref[...] = du.astype(du_ref.dtype)
        db_ref[...] = jnp.sum(dc, axis=0, keepdims=True)

    return pl.pallas_call(
        body, name=name, grid=(width // bw,),
        in_specs=[pl.BlockSpec((s, bw), lambda j: (0, off + j)), pl.BlockSpec((CONV_W, bw), lambda j: (0, j)),
                  pl.BlockSpec((1, bw), lambda j: (0, j)), pl.BlockSpec((s, bw), lambda j: (0, j)), ANY],
        out_specs=[pl.BlockSpec((s, bw), lambda j: (0, off + j)), pl.BlockSpec((CONV_W, bw), lambda j: (0, j)),
                   pl.BlockSpec((1, bw), lambda j: (0, j))],
        out_shape=[pltpu.HBM(dz.shape, dz.dtype), pltpu.HBM((CONV_W, width), F32),
                   pltpu.HBM((1, width), F32)],
        input_output_aliases={4: 0},
        compiler_params=_params(20 * s * bw * 4 + (4 << 20)),
    )(_hbm(zm), cw, cb, _hbm(dqk), dz)


def _ml_gates(gcol_ref, grow_ref):
    l = CHUNK
    r = lax.broadcasted_iota(jnp.int32, (l, l), 0)
    c = lax.broadcasted_iota(jnp.int32, (l, l), 1)
    gcol = gcol_ref[...]
    grow = grow_ref[0]
    bcol = _u01dot((c <= r).astype(BF16), gcol)
    brow = _dot01(grow, (r <= c).astype(BF16))
    return gcol, grow, bcol, brow, r >= c


def _ml_chunk(h, dh, mq_ref, mk_ref, v_ref, gates, cp, n_prev, m_prev):
    gcol, grow, bcol, brow, tri = gates
    l = CHUNK
    sl = slice(h * dh, (h + 1) * dh)
    qc = mq_ref[:, sl]
    kc = mk_ref[:, sl] * (dh ** -0.5)
    vc = v_ref[:, sl]
    i_row = grow[h:h + 1, :]
    i_col = gcol[:, h:h + 1]
    b_col = bcol[:, ML_HEADS + h:ML_HEADS + h + 1]
    b_row = brow[ML_HEADS + h:ML_HEADS + h + 1, :]
    b_end = b_col[l - 1:l, :]
    d = jnp.where(tri, b_col - b_row + i_row, -jnp.inf)
    m_inter = b_col + m_prev
    m_t = jnp.maximum(m_inter, jnp.max(d, axis=1, keepdims=True))
    w = jnp.exp(d - m_t)
    s_inter = jnp.exp(m_inter - m_t)
    qb, kb, vb = qc.astype(BF16), kc.astype(BF16), vc.astype(BF16)
    cpb = cp.astype(BF16)
    a = lax.dot_general(qb, kb, NT, preferred_element_type=F32)
    sc = a * w
    qcp = lax.dot_general(qb, cpb, NT, preferred_element_type=F32)
    qn = jnp.sum(qc * n_prev, axis=1, keepdims=True)
    num = lax.dot_general(sc.astype(BF16), vb, NN, preferred_element_type=F32) + s_inter * qcp
    den = jnp.sum(sc, axis=1, keepdims=True) + s_inter * qn
    floor = jnp.exp(-m_t)
    dnm = jnp.maximum(jnp.abs(den), floor)
    g_col = b_end - b_col + i_col
    g_row = b_end - b_row + i_row
    m_new = jnp.maximum(b_end + m_prev, jnp.max(g_row, axis=1, keepdims=True))
    decay = jnp.exp(b_end + m_prev - m_new)
    wk = jnp.exp(g_col - m_new)
    return dict(qc=qc, kc=kc, vc=vc, qb=qb, kb=kb, vb=vb, cpb=cpb, w=w, s_inter=s_inter, a=a, sc=sc, qcp=qcp, qn=qn,
                num=num, den=den, floor=floor, dnm=dnm, m_new=m_new, decay=decay, wk=wk, sl=sl)


def _ml_fwd(mqk, zm, vcol, gcol, grow, d_model, *, name):
    s = zm.shape[0]
    nc = s // CHUNK
    dh = d_model // ML_HEADS
    hh = ML_HEADS

    def body(mq_ref, mk_ref, v_ref, gcol_ref, grow_ref, h_ref, cs_ref, ns_ref, ms_ref, c_s, n_s, m_s):
        @pl.when(pl.program_id(0) == 0)
        def _():
            c_s[...] = jnp.zeros_like(c_s)
            n_s[...] = jnp.zeros_like(n_s)
            m_s[...] = jnp.zeros_like(m_s)

        gates = _ml_gates(gcol_ref, grow_ref)
        for h in range(hh):
            cp, n_prev, m_prev = c_s[h], n_s[h], m_s[h][:, 0:1]
            cs_ref[0, h] = cp
            ns_ref[0, h] = n_prev
            ms_ref[0, h] = m_s[h]
            f = _ml_chunk(h, dh, mq_ref, mk_ref, v_ref, gates, cp, n_prev, m_prev)
            h_ref[:, f["sl"]] = f["num"] / f["dnm"]
            c_s[h] = f["decay"] * cp + lax.dot_general((f["vc"] * f["wk"]).astype(BF16), f["kb"], TN,
                                                       preferred_element_type=F32)
            n_s[h] = f["decay"] * n_prev + jnp.sum(f["wk"] * f["kc"], axis=0, keepdims=True)
            m_s[h] = jnp.broadcast_to(f["m_new"], (1, LANES))

    dblk = d_model
    return pl.pallas_call(
        body, name=name, grid=(nc,),
        in_specs=[pl.BlockSpec((CHUNK, dblk), lambda c: (c, 0)), pl.BlockSpec((CHUNK, dblk), lambda c: (c, 1)),
                  pl.BlockSpec((CHUNK, dblk), lambda c: (c, vcol // dblk)),
                  pl.BlockSpec((CHUNK, LANES), lambda c: (c, 0)), pl.BlockSpec((1, 8, CHUNK), lambda c: (c, 0, 0))],
        out_specs=[pl.BlockSpec((CHUNK, dblk), lambda c: (c, 0)),
                   pl.BlockSpec((1, hh, dh, dh), lambda c: (c, 0, 0, 0)),
                   pl.BlockSpec((1, hh, 1, dh), lambda c: (c, 0, 0, 0)),
                   pl.BlockSpec((1, hh, 1, LANES), lambda c: (c, 0, 0, 0))],
        out_shape=[pltpu.HBM((s, d_model), F32), pltpu.HBM((nc, hh, dh, dh), F32),
                   pltpu.HBM((nc, hh, 1, dh), F32), pltpu.HBM((nc, hh, 1, LANES), F32)],
        scratch_shapes=[pltpu.VMEM((hh, dh, dh), F32), pltpu.VMEM((hh, 1, dh), F32), pltpu.VMEM((hh, 1, LANES), F32)],
        compiler_params=_params(8 * hh * dh * dh * 4 + (16 << 20)),
    )(_hbm(mqk), _hbm(mqk), _hbm(zm), _hbm(gcol), _hbm(grow))


def _ml_bwd(mqk, zm, vcol, gcol, grow, cs, ns, ms, dhm, dz, d_model, *, name):
    s = zm.shape[0]
    nc = s // CHUNK
    dh = d_model // ML_HEADS
    hh = ML_HEADS
    l = CHUNK

    def body(mq_ref, mk_ref, v_ref, gcol_ref, grow_ref, cs_ref, ns_ref, ms_ref, dh_ref, dz_ref,
             dqk_ref, dv_ref, dgc_ref, dgr_ref, dc_s, dn_s):
        @pl.when(pl.program_id(0) == 0)
        def _():
            dc_s[...] = jnp.zeros_like(dc_s)
            dn_s[...] = jnp.zeros_like(dn_s)

        gates = _ml_gates(gcol_ref, grow_ref)
        lane = lax.broadcasted_iota(jnp.int32, (l, LANES), 1)
        rowi = lax.broadcasted_iota(jnp.int32, (8, l), 0)
        lastrow = lax.broadcasted_iota(jnp.int32, (l, 1), 0) == l - 1
        dgc = jnp.zeros((l, LANES), F32)
        dgr = jnp.zeros((8, l), F32)
        for h in range(hh):
            cp, n_prev, m_prev = cs_ref[0, h], ns_ref[0, h], ms_ref[0, h][:, 0:1]
            f = _ml_chunk(h, dh, mq_ref, mk_ref, v_ref, gates, cp, n_prev, m_prev)
            dC, dn = dc_s[h], dn_s[h]
            dhv = dh_ref[:, f["sl"]]
            dnum = dhv / f["dnm"]
            hv = f["num"] / f["dnm"]
            ddnm = -jnp.sum(dhv * hv, axis=1, keepdims=True) / f["dnm"]
            dden = jnp.where(jnp.abs(f["den"]) >= f["floor"], ddnm * jnp.sign(f["den"]), 0.0)
            dnb = dnum.astype(BF16)
            dsc = lax.dot_general(dnb, f["vb"], NT, preferred_element_type=F32) + dden
            dvc = lax.dot_general(f["sc"].astype(BF16), dnb, TN, preferred_element_type=F32)
            ds_inter = jnp.sum(dnum * f["qcp"], axis=1, keepdims=True) + dden * f["qn"]
            sdn = (f["s_inter"] * dnum).astype(BF16)
            sdd = f["s_inter"] * dden
            da = dsc * f["w"]
            dab = da.astype(BF16)
            dqc = (lax.dot_general(dab, f["kb"], NN, preferred_element_type=F32)
                   + lax.dot_general(sdn, f["cpb"], NN, preferred_element_type=F32) + sdd * n_prev)
            dcp = f["decay"] * dC + lax.dot_general(sdn, f["qb"], TN, preferred_element_type=F32)
            dnp = f["decay"] * dn + jnp.sum(sdd * f["qc"], axis=0, keepdims=True)
            vw = (f["vc"] * f["wk"]).astype(BF16)
            dCb = dC.astype(BF16)
            dkc = (lax.dot_general(dab, f["qb"], TN, preferred_element_type=F32)
                   + lax.dot_general(vw, dCb, NN, preferred_element_type=F32) + f["wk"] * dn)
            e = lax.dot_general(f["kb"], dCb, NT, preferred_element_type=F32)
            dvc = dvc + e * f["wk"]
            dwk = jnp.sum(e * f["vc"], axis=1, keepdims=True) + jnp.sum(f["kc"] * dn, axis=1, keepdims=True)
            ddecay = jnp.sum(jnp.sum(dC * cp, axis=1, keepdims=True), axis=0, keepdims=True) \
                + jnp.sum(dn * n_prev, axis=1, keepdims=True)
            dd = dsc * f["sc"]
            dlw = dwk * f["wk"]
            db_end = jnp.sum(dlw, axis=0, keepdims=True) + ddecay * f["decay"]
            di_col = dlw
            db_col = jnp.sum(dd, axis=1, keepdims=True) + ds_inter * f["s_inter"] - dlw \
                + jnp.where(lastrow, db_end, 0.0)
            cs_dd = jnp.sum(dd, axis=0, keepdims=True)
            dgc = dgc + jnp.where(lane == h, di_col, 0.0) + jnp.where(lane == hh + h, db_col, 0.0)
            dgr = dgr + jnp.where(rowi == h, cs_dd, 0.0) - jnp.where(rowi == hh + h, cs_dd, 0.0)
            dqk_ref[:, f["sl"]] = dqc
            dqk_ref[:, d_model + h * dh:d_model + (h + 1) * dh] = dkc * (dh ** -0.5)
            dv_ref[:, f["sl"]] = dvc.astype(dv_ref.dtype)
            dc_s[h] = dcp
            dn_s[h] = dnp
        dgc_ref[...] = dgc
        dgr_ref[0] = dgr

    dblk = d_model
    rev = lambda c: nc - 1 - c
    return pl.pallas_call(
        body, name=name, grid=(nc,),
        in_specs=[pl.BlockSpec((l, dblk), lambda c: (rev(c), 0)), pl.BlockSpec((l, dblk), lambda c: (rev(c), 1)),
                  pl.BlockSpec((l, dblk), lambda c: (rev(c), vcol // dblk)),
                  pl.BlockSpec((l, LANES), lambda c: (rev(c), 0)), pl.BlockSpec((1, 8, l), lambda c: (rev(c), 0, 0)),
                  pl.BlockSpec((1, hh, dh, dh), lambda c: (rev(c), 0, 0, 0)),
                  pl.BlockSpec((1, hh, 1, dh), lambda c: (rev(c), 0, 0, 0)),
                  pl.BlockSpec((1, hh, 1, LANES), lambda c: (rev(c), 0, 0, 0)),
                  pl.BlockSpec((l, dblk), lambda c: (rev(c), 0)), ANY],
        out_specs=[pl.BlockSpec((l, 2 * dblk), lambda c: (rev(c), 0)),
                   pl.BlockSpec((l, dblk), lambda c: (rev(c), vcol // dblk)),
                   pl.BlockSpec((l, LANES), lambda c: (rev(c), 0)),
                   pl.BlockSpec((1, 8, l), lambda c: (rev(c), 0, 0))],
        out_shape=[pltpu.HBM((s, 2 * d_model), F32),
                   pltpu.HBM(dz.shape, dz.dtype), pltpu.HBM((s, LANES), F32),
                   pltpu.HBM((nc, 8, l), F32)],
        input_output_aliases={9: 1},
        scratch_shapes=[pltpu.VMEM((hh, dh, dh), F32), pltpu.VMEM((hh, 1, dh), F32)],
        compiler_params=_params(10 * hh * dh * dh * 4 + (16 << 20)),
    )(*[_hbm(a) for a in (mqk, mqk, zm, gcol, grow, cs, ns, ms, dhm)], dz)


def _xa_fwd(zm, qcol, kv, gq, gk, d_model, *, name, tq=512):
    s = zm.shape[0]
    nm = kv.shape[0]
    dh = d_model // X_HEADS
    tq = _pick(s, (tq, 128, 64))
    scale = dh ** -0.5

    def body(q_ref, k_ref, v_ref, gq_ref, gk_ref, o_ref):
        qn = _rms_fwd(q_ref[...], gq_ref[...])
        kn = _rms_fwd(k_ref[...], gk_ref[...])
        lg = _dot(qn, kn, NT) * scale
        lg = lg - jnp.max(lg, axis=1, keepdims=True)
        p = jnp.exp(lg)
        p = p / jnp.sum(p, axis=1, keepdims=True)
        o_ref[...] = _dot(p, v_ref[...], NN).astype(o_ref.dtype)

    return pl.pallas_call(
        body, name=name, grid=(X_HEADS, s // tq),
        in_specs=[pl.BlockSpec((tq, dh), lambda h, i: (i, qcol // dh + h)), pl.BlockSpec((nm, dh), lambda h, i: (0, h)),
                  pl.BlockSpec((nm, dh), lambda h, i: (0, X_HEADS + h)),
                  pl.BlockSpec((1, dh), lambda h, i: (0, 0)), pl.BlockSpec((1, dh), lambda h, i: (0, 0))],
        out_specs=pl.BlockSpec((tq, dh), lambda h, i: (i, h)),
        out_shape=pltpu.HBM((s, d_model), BF16),
        compiler_params=_params(32 << 20),
    )(_hbm(zm), _hbm(kv), _hbm(kv), gq, gk)


def _xa_bwd(zm, qcol, kv, gq, gk, dy, dz, d_model, *, name, tq=512):
    s = zm.shape[0]
    nm = kv.shape[0]
    dh = d_model // X_HEADS
    tq = _pick(s, (tq, 128, 64))
    nq = s // tq
    scale = dh ** -0.5

    def body(q_ref, k_ref, v_ref, gq_ref, gk_ref, do_ref, dz_ref, dq_ref, dkn_ref, dv_ref, dgq_ref):
        h, i = pl.program_id(0), pl.program_id(1)

        @pl.when(i == 0)
        def _():
            dkn_ref[...] = jnp.zeros_like(dkn_ref)
            dv_ref[...] = jnp.zeros_like(dv_ref)

        @pl.when((i == 0) & (h == 0))
        def _():
            dgq_ref[...] = jnp.zeros_like(dgq_ref)

        q = q_ref[...]
        qn = _rms_fwd(q, gq_ref[...])
        kn = _rms_fwd(k_ref[...], gk_ref[...])
        lg = _dot(qn, kn, NT) * scale
        lg = lg - jnp.max(lg, axis=1, keepdims=True)
        p = jnp.exp(lg)
        p = p / jnp.sum(p, axis=1, keepdims=True)
        do = do_ref[...]
        dv_ref[...] += _dot(p, do, TN)
        dp = _dot(do, v_ref[...], NT)
        dlg = p * (dp - jnp.sum(dp * p, axis=1, keepdims=True)) * scale
        dqn = _dot(dlg, kn, NN)
        dkn_ref[...] += _dot(dlg, qn, TN)
        dq, dgq = _rms_bwd(q, gq_ref[...], dqn)
        dq_ref[...] = dq.astype(dq_ref.dtype)
        dgq_ref[...] += jnp.sum(dgq, axis=0, keepdims=True)

    return pl.pallas_call(
        body, name=name, grid=(X_HEADS, nq),
        in_specs=[pl.BlockSpec((tq, dh), lambda h, i: (i, qcol // dh + h)), pl.BlockSpec((nm, dh), lambda h, i: (0, h)),
                  pl.BlockSpec((nm, dh), lambda h, i: (0, X_HEADS + h)),
                  pl.BlockSpec((1, dh), lambda h, i: (0, 0)), pl.BlockSpec((1, dh), lambda h, i: (0, 0)),
                  pl.BlockSpec((tq, dh), lambda h, i: (i, h)), ANY],
        out_specs=[pl.BlockSpec((tq, dh), lambda h, i: (i, qcol // dh + h)),
                   pl.BlockSpec((nm, dh), lambda h, i: (0, h)),
                   pl.BlockSpec((nm, dh), lambda h, i: (0, h)), pl.BlockSpec((1, dh), lambda h, i: (0, 0))],
        out_shape=[pltpu.HBM(dz.shape, dz.dtype), pltpu.HBM((nm, d_model), F32),
                   pltpu.HBM((nm, d_model), F32), pltpu.HBM((1, dh), F32)],
        input_output_aliases={6: 0},
        compiler_params=_params(32 << 20),
    )(_hbm(zm), _hbm(kv), _hbm(kv), gq, gk, _hbm(dy), dz)


def _place():
    return lax.axis_index("x"), lax.axis_index("y"), lax.axis_index("c")


ANY = pl.BlockSpec(memory_space=pl.ANY)


def _allgather_two_level(big, small, *, name, chunk_rows=64):
    r, cc = big.shape
    half = r // 2
    nr = _pick(half, (chunk_rows, 32, 16))
    nq = half // nr

    def body(big_ref, small_ref, obig, osmall, land, passed, send, recv, fsend, frecv, out_a, out_b, ssend, srecv, loc):
        x, y, c = _place()
        k = 2 * x + y
        chips = [(1 - x, y), (x, 1 - y), (1 - x, 1 - y)]
        slots = [2 * px + py for px, py in chips]
        local = [pltpu.make_async_copy(big_ref, obig.at[k], loc.at[0]),
                 pltpu.make_async_copy(small_ref, osmall.at[k], loc.at[1])]
        for cp in local:
            cp.start()

        def rows(h, q):
            return pl.ds(pl.multiple_of(h * half + q * nr, nr), nr)

        def chunk(q):
            return pl.ds(q * nr, nr)

        def over_ici(j, q):
            return pltpu.make_async_remote_copy(
                src_ref=big_ref.at[rows(c, q)], dst_ref=land.at[j, chunk(q)], send_sem=send.at[nq * j + q],
                recv_sem=recv.at[nq * j + q], device_id=(chips[j][0], chips[j][1], c), device_id_type=MESH)

        def to_sibling(j, q):
            return pltpu.make_async_remote_copy(
                src_ref=land.at[j, chunk(q)], dst_ref=passed.at[j, chunk(q)], send_sem=fsend.at[nq * j + q],
                recv_sem=frecv.at[nq * j + q], device_id=(x, y, 1 - c), device_id_type=MESH)

        def small_copy(j, slot):
            return pltpu.make_async_remote_copy(
                src_ref=small_ref, dst_ref=osmall.at[slot], send_sem=ssend.at[j], recv_sem=srecv.at[j],
                device_id=(chips[j][0], chips[j][1], c), device_id_type=MESH)

        for q in range(nq):
            for j in range(3):
                over_ici(j, q).start()
        for j in range(3):
            small_copy(j, k).start()
        for q in range(nq):
            for j in range(3):
                over_ici(j, q).wait_recv()
                to_sibling(j, q).start()
                cp = pltpu.make_async_copy(land.at[j, chunk(q)], obig.at[slots[j], rows(c, q)], out_a.at[nq * j + q])
                cp.start()
                local.append(cp)
        for q in range(nq):
            for j in range(3):
                to_sibling(j, q).wait_recv()
                cp = pltpu.make_async_copy(passed.at[j, chunk(q)], obig.at[slots[j], rows(1 - c, q)],
                                           out_b.at[nq * j + q])
                cp.start()
                local.append(cp)
        for j in range(3):
            small_copy(j, slots[j]).wait_recv()
            small_copy(j, k).wait_send()
        for q in range(nq):
            for j in range(3):
                over_ici(j, q).wait_send()
                to_sibling(j, q).wait_send()
        for cp in local:
            cp.wait()

    stage = 2 * _nbytes((3, half, cc), big.dtype)
    return pl.pallas_call(
        body, name=name, in_specs=[ANY] * 2, out_specs=[ANY] * 2,
        out_shape=[pltpu.HBM((4,) + big.shape, big.dtype), pltpu.HBM((4,) + small.shape, small.dtype)],
        scratch_shapes=[pltpu.VMEM((3, half, cc), big.dtype), pltpu.VMEM((3, half, cc), big.dtype)]
        + [pltpu.SemaphoreType.DMA((3 * nq,))] * 6
        + [pltpu.SemaphoreType.DMA((3,)), pltpu.SemaphoreType.DMA((3,)), pltpu.SemaphoreType.DMA((2,))],
        compiler_params=_params(stage + stage // 8 + (4 << 20)),
    )(big, small)


HBM_SPEC = pl.BlockSpec(memory_space=pltpu.HBM)
SEM_SPEC = pl.BlockSpec(memory_space=pltpu.SEMAPHORE)
EFFECT = pltpu.SideEffectType.DATAFLOW_SIDE_EFFECTING


def _split_copies(kind, srcs, lands, send, recv):
    x, y, c = _place()
    if kind == "quarters":
        peers = [(1 - x, y, c), (x, 1 - y, c), (1 - x, 1 - y, c)]
    else:
        peers = [(x ^ ((j >> 2) & 1), y ^ ((j >> 1) & 1), c ^ (j & 1)) for j in range(1, 8)]
    npeer = len(peers)
    out = []
    for t in range(len(srcs)):
        for j, (px, py, pc) in enumerate(peers):
            if kind == "quarters":
                src, mine, theirs = srcs[t], 2 * x + y, 2 * px + py
            else:
                src, mine, theirs = srcs[t].at[2 * px + py, pc], 4 * x + 2 * y + c, 4 * px + 2 * py + pc
            mk = functools.partial(
                pltpu.make_async_remote_copy, src_ref=src, send_sem=send.at[npeer * t + j],
                recv_sem=recv.at[npeer * t + j], device_id=(px, py, pc), device_id_type=MESH)
            out.append((functools.partial(mk, dst_ref=lands[t].at[mine]),
                        functools.partial(mk, dst_ref=lands[t].at[theirs])))
    return out


def _split_start(kind, srcs, land_shapes, after, *, name):
    n = len(srcs)
    ncopies = n * (3 if kind == "quarters" else 7)

    def body(*refs):
        ins, lands = refs[:n], refs[n:2 * n]
        send, recv = refs[2 * n + 1], refs[2 * n + 2]
        token = refs[-1]
        for start, _ in _split_copies(kind, ins, lands, send, recv):
            start().start()
        token[...] = jnp.zeros_like(token)

    lands = [_hbm(lax.empty(shp, a.dtype)) for shp, a in zip(land_shapes, srcs)]
    res = pl.pallas_call(
        body, name=name, in_specs=[HBM_SPEC] * (2 * n) + [ANY],
        out_specs=[SEM_SPEC, SEM_SPEC] + [HBM_SPEC] * (2 * n) + [pl.BlockSpec(memory_space=pltpu.VMEM)],
        out_shape=[pltpu.SemaphoreType.DMA((ncopies,)), pltpu.SemaphoreType.DMA((ncopies,))]
        + [pltpu.HBM(a.shape, a.dtype) for a in srcs] + [pltpu.HBM(shp, a.dtype) for shp, a in zip(land_shapes, srcs)]
        + [jax.ShapeDtypeStruct((8, LANES), F32)],
        input_output_aliases={i: 2 + i for i in range(2 * n)},
        compiler_params=pltpu.CompilerParams(has_side_effects=EFFECT),
    )(*[_hbm(a) for a in srcs], *lands, after)
    return res[0], res[1], list(res[2:2 + n]), list(res[2 + n:2 + 2 * n]), res[-1]


def _split_wait(kind, send, recv, srcs, lands, after, *, name):
    n = len(srcs)

    def body(*refs):
        ins, lnd = refs[:n], refs[n:2 * n]
        snd, rcv = refs[2 * n], refs[2 * n + 1]
        for start, arrive in _split_copies(kind, ins, lnd, snd, rcv):
            start().wait_send()
            arrive().wait_recv()

    res = pl.pallas_call(
        body, name=name, in_specs=[HBM_SPEC] * (2 * n) + [SEM_SPEC, SEM_SPEC] + [ANY] * len(after),
        out_specs=[HBM_SPEC] * (2 * n),
        out_shape=[pltpu.HBM(a.shape, a.dtype) for a in srcs] + [pltpu.HBM(a.shape, a.dtype) for a in lands],
        input_output_aliases={i: i for i in range(2 * n)},
        compiler_params=pltpu.CompilerParams(has_side_effects=EFFECT),
    )(*srcs, *lands, send, recv, *after)
    return list(res[n:])


def _sum8(parts, *, name):
    _, r, c = parts.shape
    t = _pick(r, (128, 64, 32, 16, 8))

    def body(p_ref, o_ref):
        acc = p_ref[0].astype(F32)
        for k in range(1, 8):
            acc = acc + p_ref[k].astype(F32)
        o_ref[...] = acc

    return pl.pallas_call(
        body, name=name, grid=(r // t,), in_specs=[pl.BlockSpec((8, t, c), lambda i: (0, i, 0))],
        out_specs=pl.BlockSpec((t, c), lambda i: (i, 0)), out_shape=pltpu.HBM((r, c), F32),
        compiler_params=_params(2 * 8 * t * c * 2 + 6 * t * c * 4 + (4 << 20)),
    )(_hbm(parts))


def _swap_halves(halves, *, name, chunk_bytes=512 * 1024):
    n = len(halves)
    items = []
    for t, a in enumerate(halves):
        r = a.shape[0]
        k = 1
        while _nbytes(a.shape, a.dtype) // k > chunk_bytes and r % (2 * k) == 0 and (r // (2 * k)) % 8 == 0:
            k *= 2
        items += [(t, q * (r // k), r // k) for q in range(k)]
    m = len(items)

    def body(*refs):
        ins, outs = refs[:n], refs[n:2 * n]
        sbuf, rbuf = refs[2 * n:3 * n], refs[3 * n:4 * n]
        send, recv, loc_own, loc_in, loc_out = refs[4 * n:]
        x, y, c = _place()
        local, stage = [], []
        for t in range(n):
            cp = pltpu.make_async_copy(ins[t], outs[t].at[c], loc_own.at[t])
            cp.start()
            local.append(cp)
        for q, (t, r0, nr) in enumerate(items):
            cp = pltpu.make_async_copy(ins[t].at[pl.ds(r0, nr)], sbuf[t].at[pl.ds(r0, nr)], loc_in.at[q])
            cp.start()
            stage.append(cp)

        def copy(q):
            t, r0, nr = items[q]
            return pltpu.make_async_remote_copy(
                src_ref=sbuf[t].at[pl.ds(r0, nr)], dst_ref=rbuf[t].at[pl.ds(r0, nr)], send_sem=send.at[q],
                recv_sem=recv.at[q], device_id=(x, y, 1 - c), device_id_type=MESH)

        for q in range(m):
            stage[q].wait()
            copy(q).start()
        for q, (t, r0, nr) in enumerate(items):
            copy(q).wait_recv()
            cp = pltpu.make_async_copy(rbuf[t].at[pl.ds(r0, nr)], outs[t].at[1 - c, pl.ds(r0, nr)], loc_out.at[q])
            cp.start()
            local.append(cp)
        for q in range(m):
            copy(q).wait_send()
        for cp in local:
            cp.wait()

    stage_bytes = 2 * sum(_nbytes(a.shape, a.dtype) for a in halves)
    return pl.pallas_call(
        body, name=name, in_specs=[ANY] * n, out_specs=[ANY] * n,
        out_shape=[pltpu.HBM((2,) + a.shape, a.dtype) for a in halves],
        scratch_shapes=[pltpu.VMEM(a.shape, a.dtype) for a in halves] * 2
        + [pltpu.SemaphoreType.DMA((m,)), pltpu.SemaphoreType.DMA((m,)), pltpu.SemaphoreType.DMA((n,)),
           pltpu.SemaphoreType.DMA((m,)), pltpu.SemaphoreType.DMA((m,))],
        compiler_params=_params(stage_bytes + (4 << 20)),
    )(*halves)


def _allreduce_small(p, after, *, name):
    r = p.shape[0]

    def body(p_ref, after_ref, o_ref, buf, send, recv):
        x, y, c = _place()
        me = 4 * x + 2 * y + c
        peers = [(x ^ ((j >> 2) & 1), y ^ ((j >> 1) & 1), c ^ (j & 1)) for j in range(1, 8)]

        def copy(j, slot):
            return pltpu.make_async_remote_copy(
                src_ref=p_ref, dst_ref=buf.at[slot], send_sem=send.at[j], recv_sem=recv.at[j],
                device_id=peers[j], device_id_type=MESH)

        for j in range(7):
            copy(j, me).start()
        buf[me] = p_ref[...]
        for j in range(7):
            px, py, pc = peers[j]
            copy(j, 4 * px + 2 * py + pc).wait_recv()
        for j in range(7):
            copy(j, me).wait_send()
        acc = buf[0]
        for k in range(1, 8):
            acc = acc + buf[k]
        o_ref[...] = acc

    vspec = pl.BlockSpec(memory_space=pltpu.VMEM)
    return pl.pallas_call(
        body, name=name, in_specs=[vspec, ANY], out_specs=vspec, out_shape=jax.ShapeDtypeStruct((r, LANES), F32),
        scratch_shapes=[pltpu.VMEM((8, r, LANES), F32), pltpu.SemaphoreType.DMA((7,)), pltpu.SemaphoreType.DMA((7,))],
    )(p, after)


def _adamw_fn(w, g, m, v):
    m = ADAM_B1 * m + (1.0 - ADAM_B1) * g
    v = ADAM_B2 * v + (1.0 - ADAM_B2) * (g * g)
    m_hat = m / (1.0 - ADAM_B1 ** ADAM_STEP)
    v_hat = v / (1.0 - ADAM_B2 ** ADAM_STEP)
    delta = -ADAM_LR * (m_hat / (jnp.sqrt(v_hat) + ADAM_EPS) + ADAM_WD * w)
    return delta, m, v


def _adamw(w, g, m, v, *, name):
    c = w.shape[1]
    return _rowwise(_adamw_fn, [w, g, m, v], [], [(c, F32)] * 3, name=name, tr=128)


def _pack(vecs, rows):
    flat = jnp.concatenate([a.reshape(-1).astype(F32) for a in vecs])
    return jnp.pad(flat, (0, rows * LANES - flat.shape[0])).reshape(rows, LANES)


def _unpack(p, like):
    flat, out, o = p.reshape(-1), [], 0
    for a in like:
        out.append(flat[o:o + a.size].reshape(a.shape))
        o += a.size
    return out


def kernel(x, mem, g_mix, w_in, b_if, b_gate, conv_w, conv_b, ml_norm_g, g_mem, w_mem_kv, q_norm_g, k_norm_g, w_sb_proj, w_ml_proj, w_x_proj, w_out, g_mlp, w_ff1, w_ff2, loss_target, m_g_mix, m_w_in, m_b_if, m_b_gate, m_conv_w, m_conv_b, m_ml_norm_g, m_g_mem, m_w_mem_kv, m_q_norm_g, m_k_norm_g, m_w_sb_proj, m_w_ml_proj, m_w_x_proj, m_w_out, m_g_mlp, m_w_ff1, m_w_ff2, v_g_mix, v_w_in, v_b_if, v_b_gate, v_conv_w, v_conv_b, v_ml_norm_g, v_g_mem, v_w_mem_kv, v_q_norm_g, v_k_norm_g, v_w_sb_proj, v_w_ml_proj, v_w_x_proj, v_w_out, v_g_mlp, v_w_ff1, v_w_ff2):
    _, s, d = x.shape
    nm = mem.shape[1]
    n_in = 4 * w_in.shape[2]
    dff = 4 * w_ff1.shape[2]
    sbh = d // SB_HD
    hh = ML_HEADS
    dh = d // hh
    nc = s // CHUNK
    assert n_in == 11 * d + 2 * hh and d % (2 * LANES) == 0 and s % LANES == 0
    x2, mem2, tgt = x[0], mem[0], loss_target[0]

    k4 = 2 * lax.axis_index("x") + lax.axis_index("y")
    me = 2 * k4 + lax.axis_index("c")
    g_first = _allgather_two_level(w_in[0].astype(BF16), conv_w[0], name="gather_w_in")
    later = [a[0].astype(BF16) for a in (w_mem_kv, w_sb_proj, w_ml_proj, w_x_proj, w_out, w_ff1, w_ff2)]
    gw_send, gw_recv, gw_src, gw_land, gw_token = _split_start(
        "quarters", later, [(4,) + a.shape for a in later], g_first[0], name="gather_rest_start")
    cols = lambda a: a.transpose(1, 0, 2).reshape(a.shape[1], 4 * a.shape[2])
    rws = lambda a: a.reshape(4 * a.shape[1], a.shape[2])
    qn = n_in // 4
    if_lo, if_hi = 7 * d, 7 * d + 2 * hh

    def cut(lo, hi):
        ks = [(k, max(lo, k * qn), min(hi, (k + 1) * qn)) for k in range(4)]
        return [g_first[0][k, :, a - k * qn:b - k * qn] for k, a, b in ks if a < b]

    w_main = jnp.concatenate(cut(0, if_lo) + cut(if_hi, n_in), axis=1)
    w_if = jnp.pad(jnp.concatenate(cut(if_lo, if_hi), axis=1), ((0, 0), (0, LANES - 2 * hh)))
    conv_wf = cols(g_first[1])
    b_if_p = jnp.pad(b_if, ((0, 0), (0, LANES - 2 * hh)))

    (hn,) = _rowwise(_rms_fwd, [x2], [g_mix], [(d, BF16)], name="norm_in", tr=512)
    zm = _mm(hn, w_main, after=gw_token, name="proj_in")
    zif = _mm(hn, w_if, name="proj_if")
    y_sb, a_sb = _sb_fwd(zm, sbh, name="sb_fwd")

    def gate_fn(z, b):
        pre = z + b
        lane = lax.broadcasted_iota(jnp.int32, pre.shape, 1)
        return jnp.where(lane < hh, pre, -_softplus(-pre))

    (gcol,) = _rowwise(gate_fn, [zif], [b_if_p], [(LANES, F32)], name="ml_gates", tr=1024)
    grow = gcol[:, :8].T.reshape(8, nc, CHUNK).transpose(1, 0, 2)
    mqk = _conv_fwd(zm, 3 * d, 2 * d, conv_wf, conv_b, name="conv_fwd")
    hm, cst, nst, mst = _ml_fwd(mqk, zm, 5 * d, gcol, grow, d, name="ml_fwd")

    def mlout_fn(hv, o, g):
        ys = [_rms_fwd(hv[:, k * dh:(k + 1) * dh], g[:, k * dh:(k + 1) * dh]) for k in range(hh)]
        return jnp.concatenate(ys, axis=1) * _sigmoid(o)

    (y_ml,) = _rowwise(mlout_fn, [hm, (zm, d, 6)], [ml_norm_g], [(d, BF16)], name="ml_out", tr=512)
    gw_land = _split_wait("quarters", gw_send, gw_recv, gw_src, gw_land, [y_ml, y_sb], name="gather_rest_wait")
    gw = [lax.dynamic_update_index_in_dim(ld, a, k4, 0) for ld, a in zip(gw_land, later)]
    w_kv, w_sbp, w_mlp, w_xp, w_o, w_f1, w_f2 = (cols(gw[0]), rws(gw[1]), rws(gw[2]), rws(gw[3]), rws(gw[4]),
                                                 cols(gw[5]), rws(gw[6]))
    (memn,) = _rowwise(_rms_fwd, [mem2], [g_mem], [(d, BF16)], name="norm_mem")
    kv = _mm(memn, w_kv, name="proj_kv")
    y_x = _xa_fwd(zm, 7 * d, kv, q_norm_g, k_norm_g, d, name="xa_fwd")
    p_sb = _mm(y_sb, w_sbp, name="proj_sb")
    p_ml = _mm(y_ml, w_mlp, name="proj_ml")
    p_x = _mm(y_x, w_xp, name="proj_x")

    def merge_fn(a, b, c, g0, g1, g2, bg):
        return (_sigmoid(g0 + bg[:, :d]) * a + _sigmoid(g1 + bg[:, d:2 * d]) * b + _sigmoid(g2 + bg[:, 2 * d:]) * c)

    gate_cols = [(zm, d, 8), (zm, d, 9), (zm, d, 10)]
    (mixed,) = _rowwise(merge_fn, [p_sb, p_ml, p_x] + gate_cols, [b_gate], [(d, BF16)], name="merge")
    x1 = _mm(mixed, w_o, tiles=[x2], name="proj_out")
    (h2,) = _rowwise(_rms_fwd, [x1], [g_mlp], [(d, BF16)], name="norm_mlp", tr=512)
    u, act = _mm(h2, w_f1, post=lambda r: (r, jnp.square(jnp.maximum(r, 0.0))), out_dtype=(F32, BF16), name="ff1")
    dy = _mm(act, w_f2, tiles=[x1, tgt], post=lambda r, xv, tv: (r + xv - tv) * (1.0 / d), name="ff2")
    (loss_cols,) = _rowwise(lambda g: (jnp.sum(g * g, axis=0, keepdims=True) * (0.5 * d),), [dy], [], [], [d],
                            name="loss", tr=1024)

    du = _mm(dy, w_f2, tb=True, tiles=[u], post=lambda r, uv: r * 2.0 * jnp.maximum(uv, 0.0), out_dtype=BF16,
             name="ff2_dx")
    dw_f2 = _mm(act, dy, ta=True, name="ff2_dw")
    dw_f1 = _mm(h2, du, ta=True, name="ff1_dw")
    dh2 = _mm(du, w_f1, tb=True, name="ff1_dx")

    def norm_bwd_fn(xv, dyv, res, g):
        dx, dg = _rms_bwd(xv, g, dyv)
        return dx + res, jnp.sum(dg, axis=0, keepdims=True)

    dx1, dg_mlp = _rowwise(norm_bwd_fn, [x1, dh2, dy], [g_mlp], [(d, F32)], [d], name="norm_mlp_bwd", tr=512)
    dmixed = _mm(dx1, w_o, tb=True, name="proj_out_dx")
    dw_o = _mm(mixed, dx1, ta=True, name="proj_out_dw")

    def merge_bwd_fn(dm, a, b, c, g0, g1, g2, bg):
        outs, dgs = [], []
        for p, g, k in ((a, g0, 0), (b, g1, 1), (c, g2, 2)):
            sg = _sigmoid(g + bg[:, k * d:(k + 1) * d])
            outs.append(dm * sg)
            dgs.append(dm * p * sg * (1.0 - sg))
        dgate = jnp.concatenate(dgs, axis=1)
        return (*outs, dgate, jnp.sum(dgate, axis=0, keepdims=True))

    dp_sb, dp_ml, dp_x, dgate, db_gate = _rowwise(
        merge_bwd_fn, [dmixed, p_sb, p_ml, p_x] + gate_cols, [b_gate], [(d, BF16)] * 3 + [(3 * d, BF16)], [3 * d],
        name="merge_bwd", tr=256)
    dw_sbp = _mm(y_sb, dp_sb, ta=True, name="proj_sb_dw")
    dw_mlp = _mm(y_ml, dp_ml, ta=True, name="proj_ml_dw")
    dw_xp = _mm(y_x, dp_x, ta=True, name="proj_x_dw")
    dy_sb = _mm(dp_sb, w_sbp, tb=True, out_dtype=BF16, name="proj_sb_dx")
    dy_ml = _mm(dp_ml, w_mlp, tb=True, name="proj_ml_dx")
    dy_x = _mm(dp_x, w_xp, tb=True, out_dtype=BF16, name="proj_x_dx")

    dzm = _hbm(lax.empty((s, 11 * d), BF16))
    dzm, dkn, dxv, dg_qn = _xa_bwd(zm, 7 * d, kv, q_norm_g, k_norm_g, dy_x, dzm, d, name="xa_bwd")

    def knorm_bwd_fn(kvv, dknv, dvv, g):
        dks, dgs = [], []
        for k in range(X_HEADS):
            sl = slice(k * dh, (k + 1) * dh)
            dk, dg = _rms_bwd(kvv[:, sl], g, dknv[:, sl])
            dks.append(dk)
            dgs.append(jnp.sum(dg, axis=0, keepdims=True))
        return jnp.concatenate(dks + [dvv], axis=1), dgs[0] + dgs[1] + dgs[2] + dgs[3]

    dkv, dg_kn = _rowwise(knorm_bwd_fn, [(kv, d, 0), dkn, dxv], [k_norm_g], [(2 * d, BF16)], [dh], name="xa_knorm_bwd")
    dw_kv = _mm(memn, dkv, ta=True, name="proj_kv_dw")
    dmemn = _mm(dkv, w_kv, tb=True, name="proj_kv_dx")

    def gmem_fn(mv, dv_, g):
        _, dg = _rms_bwd(mv, g, dv_)
        return (jnp.sum(dg, axis=0, keepdims=True),)

    (dg_mem,) = _rowwise(gmem_fn, [mem2, dmemn], [g_mem], [], [d], name="norm_mem_bwd")

    uncols = lambda a: a.reshape(a.shape[0], 4, a.shape[1] // 4).transpose(1, 0, 2)
    unrws = lambda a: a.reshape(4, a.shape[0] // 4, a.shape[1])
    to_parts = lambda q: q.astype(BF16).reshape(4, 2, q.shape[1] // 2, q.shape[2])
    early = [to_parts(q) for q in (uncols(dw_kv), unrws(dw_sbp), unrws(dw_mlp), unrws(dw_xp), unrws(dw_o),
                                   uncols(dw_f1), unrws(dw_f2))]
    ge_send, ge_recv, ge_src, ge_land, ge_token = _split_start(
        "grads", early, [(8,) + a.shape[2:] for a in early], dg_mem, name="exchange_early_start")

    def mlout_bwd_fn(dyv, hv, o, g):
        sg = _sigmoid(o)
        dn = dyv * sg
        dxs, dgs, ys = [], [], []
        for k in range(hh):
            sl = slice(k * dh, (k + 1) * dh)
            ys.append(_rms_fwd(hv[:, sl], g[:, sl]))
            dxk, dgk = _rms_bwd(hv[:, sl], g[:, sl], dn[:, sl])
            dxs.append(dxk)
            dgs.append(dgk)
        do = dyv * jnp.concatenate(ys, axis=1) * sg * (1.0 - sg)
        return jnp.concatenate(dxs, axis=1), do, jnp.sum(jnp.concatenate(dgs, axis=1), axis=0, keepdims=True)

    dhm, dzm, dg_mln = _rowwise(mlout_bwd_fn, [dy_ml, hm, (zm, d, 6)], [ml_norm_g], [(d, F32), (d, BF16)], [d],
                                name="ml_out_bwd", tr=512, into=(dzm, 1, 6))
    dmqk, dzm, dgc, dgr = _ml_bwd(mqk, zm, 5 * d, gcol, grow, cst, nst, mst, dhm, dzm, d, name="ml_bwd")
    dzm, dconv_w, dconv_b = _conv_bwd(zm, 3 * d, 2 * d, conv_wf, conv_b, dmqk, dzm, name="conv_bwd")
    dzm, dsk, dsv = _sb_bwd(zm, dy_sb, a_sb, dzm, ge_token, sbh, name="sb_bwd")
    dgr_t = jnp.pad(dgr.transpose(1, 0, 2).reshape(8, s).T, ((0, 0), (0, LANES - 8)))

    def gate_bwd_fn(a, b, z, bias):
        tot = a + b
        rows_t = tot.shape[0]
        r = lax.broadcasted_iota(jnp.int32, (rows_t, rows_t), 0)
        c = lax.broadcasted_iota(jnp.int32, (rows_t, rows_t), 1)
        sh = CHUNK.bit_length() - 1
        same_chunk = jnp.right_shift(r, sh) == jnp.right_shift(c, sh)
        dlf = _u01dot(((c >= r) & same_chunk).astype(BF16), tot)
        lane = lax.broadcasted_iota(jnp.int32, tot.shape, 1)
        dz = jnp.where(lane < hh, tot, jnp.where(lane < 2 * hh, dlf * _sigmoid(-(z + bias)), 0.0))
        return dz, jnp.sum(dz, axis=0, keepdims=True)

    dzif, db_if_p = _rowwise(gate_bwd_fn, [dgc, dgr_t, zif], [b_if_p], [(LANES, BF16)], [LANES], name="ml_gates_bwd",
                             tr=8 * CHUNK)
    for part, col in ((dsk, d), (dsv, 2 * d), (dgate, 8 * d)):
        dzm = lax.dynamic_update_slice(dzm, part, (0, col))
    dw_main = _mm(hn, dzm, ta=True, out_dtype=BF16, name="proj_in_dw")
    dw_if = _mm(hn, dzif, ta=True, out_dtype=BF16, name="proj_if_dw")

    def dw_quarter(k):
        lo, hi = k * qn, (k + 1) * qn
        segs = [(dw_main, 0, if_lo, 0), (dw_if, if_lo, if_hi, if_lo), (dw_main, if_hi, n_in, 2 * hh)]
        got = [src[:, max(lo, a) - off:min(hi, b) - off] for src, a, b, off in segs if max(lo, a) < min(hi, b)]
        return jnp.concatenate(got, axis=1)

    late = [to_parts(jnp.stack([dw_quarter(k) for k in range(4)]))]
    gl_send, gl_recv, gl_src, gl_land, gl_token = _split_start(
        "grads", late, [(8,) + a.shape[2:] for a in late], dw_if, name="exchange_late_start")
    dhn = _mm(dzm, w_main, tb=True, after=gl_token, name="proj_in_dx")
    dhn = _mm(dzif, w_if, tb=True, tiles=[dhn], name="proj_if_dx")
    dx, dg_mix = _rowwise(norm_bwd_fn, [x2, dhn, dx1], [g_mix], [(d, F32)], [d], name="norm_in_bwd", tr=512)

    own = lambda p: lax.dynamic_index_in_dim(lax.dynamic_index_in_dim(p, k4, 0, keepdims=False),
                                             lax.axis_index("c"), 0, keepdims=False)

    def finish(tag, send, recv, src, land, parts, after, ws, ms, vs):
        land = _split_wait("grads", send, recv, src, land, after, name=f"exchange_{tag}_wait")
        got = [lax.dynamic_update_index_in_dim(ld, own(p), me, 0) for ld, p in zip(land, parts)]
        halves = [_sum8(r, name=f"sum_grads_{tag}{i}") for i, r in enumerate(got)]
        both = _swap_halves(halves, name=f"swap_halves_{tag}")
        gs = [b.reshape(2 * b.shape[1], b.shape[2]) for b in both]
        return gs, [_adamw(w, g, m, v, name=f"adamw_{tag}{i}") for i, (w, g, m, v) in enumerate(zip(ws, gs, ms, vs))]

    first = lambda arrs: [a[0] for a in arrs]
    g_early, out_early = finish(
        "early", ge_send, ge_recv, ge_src, ge_land, early, [dx],
        first([w_mem_kv, w_sb_proj, w_ml_proj, w_x_proj, w_out, w_ff1, w_ff2]),
        first([m_w_mem_kv, m_w_sb_proj, m_w_ml_proj, m_w_x_proj, m_w_out, m_w_ff1, m_w_ff2]),
        first([v_w_mem_kv, v_w_sb_proj, v_w_ml_proj, v_w_x_proj, v_w_out, v_w_ff1, v_w_ff2]))
    g_late, out_late = finish(
        "late", gl_send, gl_recv, gl_src, gl_land, late, [o[0] for o in out_early],
        first([w_in]), first([m_w_in]), first([v_w_in]))
    g_big = [g[None] for g in g_late + g_early]
    big_out = [[o[None] for o in outs] for outs in out_late + out_early]

    small_g = [dg_mix, db_if_p[:, :2 * hh], db_gate, dconv_w, dconv_b, dg_mln, dg_mem, dg_qn, dg_kn, dg_mlp,
               jnp.sum(loss_cols).reshape(1, 1)]
    n_small = sum(a.size for a in small_g)
    rows = -(-n_small // (8 * LANES)) * 8
    g_small = _unpack(_allreduce_small(_pack(small_g, rows), out_late[0][0], name="allreduce_small"), small_g)
    loss = g_small[-1].reshape(())
    qw = conv_w.shape[2]
    g_conv_w = lax.dynamic_slice_in_dim(g_small[3], k4 * qw, qw, axis=1)
    g_small_w = [g_small[0], g_small[1], g_small[2], g_conv_w] + g_small[4:10]
    sm_w = [g_mix, b_if, b_gate, conv_w[0], conv_b, ml_norm_g, g_mem, q_norm_g, k_norm_g, g_mlp]
    sm_m = [m_g_mix, m_b_if, m_b_gate, m_conv_w[0], m_conv_b, m_ml_norm_g, m_g_mem, m_q_norm_g, m_k_norm_g, m_g_mlp]
    sm_v = [v_g_mix, v_b_if, v_b_gate, v_conv_w[0], v_conv_b, v_ml_norm_g, v_g_mem, v_q_norm_g, v_k_norm_g, v_g_mlp]
    n_sw = sum(a.size for a in sm_w)
    rows_w = -(-n_sw // (8 * LANES)) * 8
    sm_out = _adamw(_pack(sm_w, rows_w), _pack(g_small_w, rows_w), _pack(sm_m, rows_w), _pack(sm_v, rows_w),
                    name="adamw_small")
    sm_delta, sm_newm, sm_newv = [_unpack(p, sm_w) for p in sm_out]

    order = ["g_mix", "w_in", "b_if", "b_gate", "conv_w", "conv_b", "ml_norm_g", "g_mem", "w_mem_kv", "q_norm_g",
             "k_norm_g", "w_sb_proj", "w_ml_proj", "w_x_proj", "w_out", "g_mlp", "w_ff1", "w_ff2"]
    small_names = ["g_mix", "b_if", "b_gate", "conv_w", "conv_b", "ml_norm_g", "g_mem", "q_norm_g", "k_norm_g", "g_mlp"]
    big_names = ["w_in", "w_mem_kv", "w_sb_proj", "w_ml_proj", "w_x_proj", "w_out", "w_ff1", "w_ff2"]
    grads, deltas, new_m, new_v = {}, {}, {}, {}
    for i, nme in enumerate(small_names):
        shp = sm_w[i].shape if nme != "conv_w" else conv_w.shape
        grads[nme] = g_small_w[i].reshape(shp)
        deltas[nme], new_m[nme], new_v[nme] = (sm_delta[i].reshape(shp), sm_newm[i].reshape(shp),
                                               sm_newv[i].reshape(shp))
    for i, nme in enumerate(big_names):
        grads[nme] = g_big[i]
        deltas[nme], new_m[nme], new_v[nme] = big_out[i]
    return (loss, dx[None], *[grads[k] for k in order], *[deltas[k] for k in order], *[new_m[k] for k in order],
            *[new_v[k] for k in order])
```

```python
import functools

import jax
import jax.numpy as jnp
from jax import lax
from jax.experimental import pallas as pl
from jax.experimental.pallas import tpu as pltpu

F32 = jnp.float32
BF16 = jnp.bfloat16
MESH = pl.DeviceIdType.MESH

EPS = 1e-6
SB_HD = 128
SB_SLOTS = 8
ML_HEADS = 4
X_HEADS = 4
CHUNK = 64
CONV_W = 4
LANES = 128
ADAM_LR = 0.001
ADAM_B1 = 0.9
ADAM_B2 = 0.999
ADAM_EPS = 1e-08
ADAM_WD = 0.01
ADAM_STEP = 10
VMEM_CAP = 56 * 1024 * 1024
NEG = -1e30

NT = (((1,), (1,)), ((), ()))
NN = (((1,), (0,)), ((), ()))
TN = (((0,), (0,)), ((), ()))


def _dot(a, b, dn=NN):
    return lax.dot_general(a.astype(BF16), b.astype(BF16), dn, preferred_element_type=F32)


def _dot01(x, u, dn=NN):
    hi = x.astype(BF16)
    lo = (x - hi.astype(F32)).astype(BF16)
    return (lax.dot_general(hi, u, dn, preferred_element_type=F32)
            + lax.dot_general(lo, u, dn, preferred_element_type=F32))


def _u01dot(u, x):
    hi = x.astype(BF16)
    lo = (x - hi.astype(F32)).astype(BF16)
    return (lax.dot_general(u, hi, NN, preferred_element_type=F32)
            + lax.dot_general(u, lo, NN, preferred_element_type=F32))


def _pick(n, cands):
    for c in cands:
        if c <= n and n % c == 0:
            return c
    return n


def _nbytes(shape, dtype):
    n = 1
    for s in shape:
        n *= s
    return n * jnp.dtype(dtype).itemsize


def _params(vmem_bytes):
    return pltpu.CompilerParams(vmem_limit_bytes=int(min(VMEM_CAP, max(vmem_bytes, 16 * 1024 * 1024))))


def _hbm(a):
    return pltpu.with_memory_space_constraint(a, pltpu.HBM)


def _softplus(z):
    return jnp.maximum(z, 0.0) + jnp.log(1.0 + jnp.exp(-jnp.abs(z)))


def _sigmoid(z):
    return 1.0 / (1.0 + jnp.exp(-z))


def _rms_fwd(xv, g):
    r = lax.rsqrt(jnp.mean(xv * xv, axis=-1, keepdims=True) + EPS)
    return xv * r * g


def _rms_bwd(xv, g, dy):
    r = lax.rsqrt(jnp.mean(xv * xv, axis=-1, keepdims=True) + EPS)
    xh = xv * r
    dxh = dy * g
    dx = r * (dxh - xh * jnp.mean(dxh * xh, axis=-1, keepdims=True))
    return dx, dy * xh


def _mm(a, b, *, name, ta=False, tb=False, tiles=(), post=None, out_dtype=F32, bm=1024, bn=1024, bk=1024, after=None):
    m, k = (a.shape[1], a.shape[0]) if ta else a.shape
    n = b.shape[0] if tb else b.shape[1]
    tm = _pick(m, (bm, 512, 256, 128))
    tn = _pick(n, (bn, 512, 256, 128))
    tk = _pick(k, (bk, 512, 256, 128))
    nk = k // tk
    if (m // tm) * (n // tn) * nk < 8 and tm % 256 == 0:
        tm //= 2
    dn = (((0 if ta else 1,), (1 if tb else 0,)), ((), ()))
    dts = out_dtype if isinstance(out_dtype, tuple) else (out_dtype,)
    nt, no = len(tiles), len(dts)
    if post is None:
        post = lambda r, *ts: sum((t.astype(F32) for t in ts), r)

    def body(*refs):
        a_ref, b_ref = refs[:2]
        t_refs = refs[2:2 + nt]
        o_refs = refs[2 + nt + (after is not None):2 + nt + (after is not None) + no]
        part = lax.dot_general(a_ref[...].astype(BF16), b_ref[...].astype(BF16), dn, preferred_element_type=F32)

        def finish(r):
            res = post(r, *[t[...] for t in t_refs])
            res = res if isinstance(res, tuple) else (res,)
            for o, v in zip(o_refs, res):
                o[...] = v.astype(o.dtype)

        if nk == 1:
            finish(part)
        else:
            acc_ref = refs[-1]
            kk = pl.program_id(2)

            @pl.when(kk == 0)
            def _():
                acc_ref[...] = part

            @pl.when(kk > 0)
            def _():
                acc_ref[...] += part

            @pl.when(kk == nk - 1)
            def _():
                finish(acc_ref[...])

    a_spec = pl.BlockSpec((tk, tm), lambda i, j, q: (q, i)) if ta else pl.BlockSpec((tm, tk), lambda i, j, q: (i, q))
    b_spec = pl.BlockSpec((tn, tk), lambda i, j, q: (j, q)) if tb else pl.BlockSpec((tk, tn), lambda i, j, q: (q, j))
    o_spec = pl.BlockSpec((tm, tn), lambda i, j, q: (i, j))
    ins, specs = [_hbm(a), _hbm(b)] + [_hbm(t) for t in tiles], [a_spec, b_spec] + [o_spec] * nt
    vm = 2 * (_nbytes((tm, tk), a.dtype) + _nbytes((tk, tn), b.dtype)) + 3 * _nbytes((tm, tn), F32) \
        + _nbytes((tm, tk), BF16) + _nbytes((tk, tn), BF16) \
        + 2 * sum(_nbytes((tm, tn), t.dtype) for t in tiles) + 2 * sum(_nbytes((tm, tn), dt) for dt in dts)
    if after is not None:
        ins.append(after)
        specs.append(ANY)
    res = pl.pallas_call(
        body, name=name, grid=(m // tm, n // tn, nk), in_specs=specs, out_specs=[o_spec] * no,
        out_shape=[pltpu.HBM((m, n), dt) for dt in dts], scratch_shapes=[pltpu.VMEM((tm, tn), F32)] if nk > 1 else [],
        compiler_params=_params(vm + (4 << 20)),
    )(*ins)
    return res[0] if no == 1 else tuple(res)


def _rowwise(fn, rows, consts, outs, reds=(), *, name, tr=256, temps=6, into=None):
    rows = [r if isinstance(r, tuple) else (r, r.shape[1], 0) for r in rows]
    nrows = rows[0][0].shape[0]
    t = _pick(nrows, (tr, 128, 64, 32, 16, 8))
    nr, nc, no = len(rows), len(consts), len(outs)
    nb = 0 if into is None else 1

    def body(*refs):
        rin, cin = refs[:nr], refs[nr:nr + nc]
        oref, rref = refs[nr + nc + nb:nr + nc + nb + no], refs[nr + nc + nb + no:]
        res = fn(*[r[...] for r in rin], *[c[...] for c in cin])
        if not isinstance(res, (tuple, list)):
            res = (res,)
        for o, v in zip(oref, res[:no]):
            o[...] = v.astype(o.dtype)
        if rref:
            @pl.when(pl.program_id(0) == 0)
            def _():
                for r in rref:
                    r[...] = jnp.zeros_like(r)

            for r, v in zip(rref, res[no:]):
                r[...] += v

    in_specs = [pl.BlockSpec((t, w), functools.partial(lambda i, ci: (i, ci), ci=ci)) for (_, w, ci) in rows]
    in_specs += [pl.BlockSpec(c.shape, functools.partial(lambda i, nd: (0,) * nd, nd=c.ndim)) for c in consts]
    out_specs = [pl.BlockSpec((t, w), lambda i: (i, 0)) for (w, _) in outs]
    out_specs += [pl.BlockSpec((1, w), lambda i: (0, 0)) for w in reds]
    out_shape = [pltpu.HBM((nrows, w), dt) for (w, dt) in outs]
    out_shape += [jax.ShapeDtypeStruct((1, w), F32) for w in reds]
    widest = max([w for (_, w, _) in rows] + [w for (w, _) in outs])
    vm = 2 * sum(_nbytes((t, w), a.dtype) for (a, w, _) in rows) + 2 * sum(_nbytes((t, w), dt) for (w, dt) in outs)
    vm += temps * _nbytes((t, widest), F32) + (2 << 20)
    extra, aliases = [], {}
    if into is not None:
        buf, oi, cb = into
        out_specs[oi] = pl.BlockSpec((t, outs[oi][0]), lambda i: (i, cb))
        out_shape[oi] = pltpu.HBM(buf.shape, buf.dtype)
        in_specs.append(ANY)
        extra, aliases = [buf], {nr + nc: oi}
    res = pl.pallas_call(
        body, name=name, grid=(nrows // t,), in_specs=in_specs, out_specs=out_specs, out_shape=out_shape,
        input_output_aliases=aliases, compiler_params=_params(vm),
    )(*[_hbm(a) for (a, _, _) in rows], *consts, *extra)
    return list(res)


def _sb_tiles(s, tq, tk):
    tq = _pick(s, (tq, 256, 128))
    tk = _pick(tq, (tk, 128))
    return tq, tk, tq // tk


def _sb_fwd(zm, heads, *, name, tq=512, tk=256):
    s = zm.shape[0]
    tq, tk, nd = _sb_tiles(s, tq, tk)
    scale = SB_HD ** -0.5

    def body(q_ref, k_ref, v_ref, o_ref, a_out, stage, sem):
        h, i = pl.program_id(0), pl.program_id(1)
        qb = (q_ref[...] * scale).astype(BF16)
        r = lax.broadcasted_iota(jnp.int32, (tq, tk), 0)
        c = lax.broadcasted_iota(jnp.int32, (tq, tk), 1)
        ur = lax.broadcasted_iota(jnp.int32, (tk, tk), 0)
        uc = lax.broadcasted_iota(jnp.int32, (tk, tk), 1)
        usuf = (ur > uc).astype(BF16)

        def out_copy(slot, j):
            return pltpu.make_async_copy(stage.at[slot], a_out.at[h, i, j], sem.at[slot])

        def tile(j, carry, causal, slot, reuse):
            acc, cl = carry
            if reuse is True:
                out_copy(slot, 0).wait()
            elif reuse is not None:
                @pl.when(reuse)
                def _():
                    out_copy(slot, 0).wait()
            rows = pl.ds(pl.multiple_of(j * tk, tk), tk)
            kb = k_ref[rows, :].astype(BF16)
            vb = v_ref[rows, :].astype(BF16)
            z = lax.dot_general(qb, kb, NT, preferred_element_type=F32)
            lsig = -_softplus(z)
            l = lsig if causal is None else jnp.where(causal, lsig, 0.0)
            loga = z + lsig + _dot01(l, usuf) + cl
            if causal is not None:
                loga = jnp.where(causal, loga, NEG)
            ab = jnp.exp(loga).astype(BF16)
            acc = acc + lax.dot_general(ab, vb, NN, preferred_element_type=F32)
            stage[slot] = ab
            out_copy(slot, j).start()
            return acc, cl + jnp.sum(l, axis=1, keepdims=True)

        carry = (jnp.zeros((tq, SB_HD), F32), jnp.zeros((tq, 1), F32))
        for n, dd in enumerate(range(nd - 1, -1, -1)):
            carry = tile(i * nd + dd, carry, c + dd * tk < r, n, None)

        if nd == 2:
            slots = 4

            def pair(n, cr):
                s0 = (2 + 2 * n) % slots

                @pl.when(n >= 1)
                def _():
                    out_copy(s0, 0).wait()
                    out_copy(s0 + 1, 0).wait()

                return tile(i * nd - 2 - 2 * n, tile(i * nd - 1 - 2 * n, cr, None, s0, None), None, s0 + 1, None)

            acc, _ = lax.fori_loop(0, i, pair, carry)
        else:
            slots = SB_SLOTS

            def rest(n, cr):
                return tile(i * nd - 1 - n, cr, None, (nd + n) % SB_SLOTS, nd + n >= SB_SLOTS)

            acc, _ = lax.fori_loop(0, i * nd, rest, carry)
        total = (i + 1) * nd
        for back in range(1, slots + 1):
            @pl.when(total >= back)
            def _():
                out_copy((total - back) % slots, 0).wait()

        o_ref[...] = acc.astype(o_ref.dtype)

    assert nd <= SB_SLOTS
    blk = lambda off: pl.BlockSpec((s, SB_HD), functools.partial(lambda h, i, off: (0, off + h), off=off))
    return pl.pallas_call(
        body, name=name, grid=(heads, s // tq),
        in_specs=[pl.BlockSpec((tq, SB_HD), lambda h, i: (i, h)), blk(heads), blk(2 * heads)],
        out_specs=[pl.BlockSpec((tq, SB_HD), lambda h, i: (i, h)), ANY],
        out_shape=[pltpu.HBM((s, heads * SB_HD), BF16), pltpu.HBM((heads, s // tq, s // tk, tq, tk), BF16)],
        scratch_shapes=[pltpu.VMEM((SB_SLOTS, tq, tk), BF16), pltpu.SemaphoreType.DMA((SB_SLOTS,))],
        compiler_params=_params(8 * s * SB_HD * 4 + 24 * tq * tk * 4 + (8 << 20)),
    )(_hbm(zm), _hbm(zm), _hbm(zm))


def _sb_bwd(zm, dy, a_all, dz, after, heads, *, name, tq=512, tk=256):
    s = zm.shape[0]
    tq, tk, nd = _sb_tiles(s, tq, tk)
    nq = s // tq
    scale = SB_HD ** -0.5

    def body(q_ref, k_ref, v_ref, do_ref, a_in, dz_ref, after_ref, dq_ref, dk_ref, dv_ref, dka, dva, abuf, sem):
        h, i = pl.program_id(0), pl.program_id(1)

        @pl.when(i == 0)
        def _():
            dka[...] = jnp.zeros_like(dka)
            dva[...] = jnp.zeros_like(dva)

        qb = (q_ref[...] * scale).astype(BF16)
        dob = do_ref[...].astype(BF16)
        qb_t = (q_ref[...] * scale).T.astype(BF16)
        dob_t = do_ref[...].astype(F32).T.astype(BF16)
        r = lax.broadcasted_iota(jnp.int32, (tq, tk), 0)
        c = lax.broadcasted_iota(jnp.int32, (tq, tk), 1)
        ur = lax.broadcasted_iota(jnp.int32, (tk, tk), 0)
        uc = lax.broadcasted_iota(jnp.int32, (tk, tk), 1)
        uexcl = (ur < uc).astype(BF16)

        def fetch(j, slot):
            return pltpu.make_async_copy(a_in.at[h, i, j], abuf.at[slot], sem.at[slot])

        total = (i + 1) * nd
        ahead = SB_SLOTS - 1 - (nd == 2)

        def arrive(j):
            fetch(j, j % SB_SLOTS).wait()

            @pl.when(j + ahead < total)
            def _():
                fetch(j + ahead, (j + ahead) % SB_SLOTS).start()

        def tile(j, carry, causal, sync=True):
            dq, cg = carry
            slot = j % SB_SLOTS
            if sync:
                arrive(j)
            rows = pl.ds(pl.multiple_of(j * tk, tk), tk)
            kb = k_ref[rows, :].astype(BF16)
            vb = v_ref[rows, :].astype(BF16)
            z = lax.dot_general(qb, kb, NT, preferred_element_type=F32)
            sig = 1.0 / (1.0 + jnp.exp(-z))
            ab = abuf[slot]
            g = ab.astype(F32) * lax.dot_general(dob, vb, NT, preferred_element_type=F32)
            p = cg + lax.dot_general(g.astype(BF16), uexcl, NN, preferred_element_type=F32)
            dz = g - sig * (g + p)
            if causal is not None:
                dz = jnp.where(causal, dz, 0.0)
            dzb = dz.astype(BF16)
            dva[j] += lax.dot_general(dob_t, ab, NN, preferred_element_type=F32)
            dka[j] += lax.dot_general(qb_t, dzb, NN, preferred_element_type=F32)
            dq = dq + lax.dot_general(dzb, kb, NN, preferred_element_type=F32)
            return dq, cg + jnp.sum(g, axis=1, keepdims=True)

        for first in range(ahead):
            @pl.when(first < total)
            def _():
                fetch(first, first).start()

        init = (jnp.zeros((tq, SB_HD), F32), jnp.zeros((tq, 1), F32))
        if nd == 2:
            def pair(n, cr):
                arrive(2 * n)
                arrive(2 * n + 1)
                return tile(2 * n + 1, tile(2 * n, cr, None, False), None, False)

            carry = lax.fori_loop(0, i, pair, init)
        else:
            carry = lax.fori_loop(0, i * nd, lambda j, cr: tile(j, cr, None), init)
        for dd in range(nd):
            carry = tile(i * nd + dd, carry, c + dd * tk < r)
        dq_ref[...] = (carry[0] * scale).astype(dq_ref.dtype)

        @pl.when(i == nq - 1)
        def _():
            for jj in range(s // tk):
                dk_ref[jj * tk:(jj + 1) * tk, :] = dka[jj].T.astype(dk_ref.dtype)
                dv_ref[jj * tk:(jj + 1) * tk, :] = dva[jj].T.astype(dv_ref.dtype)

    blk = lambda off: pl.BlockSpec((s, SB_HD), functools.partial(lambda h, i, off: (0, off + h), off=off))
    tile_spec = pl.BlockSpec((tq, SB_HD), lambda h, i: (i, h))
    full = pltpu.HBM((s, heads * SB_HD), BF16)
    return pl.pallas_call(
        body, name=name, grid=(heads, nq),
        in_specs=[tile_spec, blk(heads), blk(2 * heads), tile_spec, ANY, ANY, ANY],
        out_specs=[tile_spec, blk(0), blk(0)],
        out_shape=[pltpu.HBM(dz.shape, dz.dtype), full, full],
        input_output_aliases={5: 0},
        scratch_shapes=[pltpu.VMEM((s // tk, SB_HD, tk), F32), pltpu.VMEM((s // tk, SB_HD, tk), F32),
                        pltpu.VMEM((SB_SLOTS, tq, tk), BF16), pltpu.SemaphoreType.DMA((SB_SLOTS,))],
        compiler_params=_params(12 * s * SB_HD * 4 + 32 * tq * tk * 4 + (8 << 20)),
    )(_hbm(zm), _hbm(zm), _hbm(zm), _hbm(dy), a_all, dz, after)


def _conv_taps(u, w_ref, rows_i):
    taps = []
    for j in range(CONV_W):
        sh = CONV_W - 1 - j
        if sh == 0:
            taps.append(u)
        else:
            taps.append(jnp.where(rows_i >= sh, pltpu.roll(u, sh, 0), 0.0))
    return taps


def _conv_fwd(zm, col0, width, cw, cb, *, name):
    s = zm.shape[0]
    bw = _pick(width, (LANES,))
    off = col0 // bw

    def body(u_ref, w_ref, b_ref, o_ref):
        u = u_ref[...]
        rows_i = lax.broadcasted_iota(jnp.int32, u.shape, 0)
        acc = jnp.broadcast_to(b_ref[...], u.shape)
        for j, tp in enumerate(_conv_taps(u, w_ref, rows_i)):
            acc = acc + tp * w_ref[j:j + 1, :]
        o_ref[...] = acc * _sigmoid(acc)

    return pl.pallas_call(
        body, name=name, grid=(width // bw,),
        in_specs=[pl.BlockSpec((s, bw), lambda j: (0, off + j)), pl.BlockSpec((CONV_W, bw), lambda j: (0, j)),
                  pl.BlockSpec((1, bw), lambda j: (0, j))],
        out_specs=pl.BlockSpec((s, bw), lambda j: (0, j)),
        out_shape=pltpu.HBM((s, width), F32),
        compiler_params=_params(12 * s * bw * 4 + (4 << 20)),
    )(_hbm(zm), cw, cb)


def _conv_bwd(zm, col0, width, cw, cb, dqk, dz, *, name):
    s = zm.shape[0]
    bw = _pick(width, (LANES,))
    off = col0 // bw

    def body(u_ref, w_ref, b_ref, d_ref, dz_ref, du_ref, dw_ref, db_ref):
        u = u_ref[...]
        rows_i = lax.broadcasted_iota(jnp.int32, u.shape, 0)
        taps = _conv_taps(u, w_ref, rows_i)
        acc = jnp.broadcast_to(b_ref[...], u.shape)
        for j, tp in enumerate(taps):
            acc = acc + tp * w_ref[j:j + 1, :]
        sg = _sigmoid(acc)
        dc = d_ref[...] * (sg * (1.0 + acc * (1.0 - sg)))
        du = jnp.zeros_like(u)
        for j in range(CONV_W):
            sh = CONV_W - 1 - j
            if sh == 0:
                du = du + dc * w_ref[j:j + 1, :]
            else:
                du = du + jnp.where(rows_i < s - sh, pltpu.roll(dc, s - sh, 0), 0.0) * w_ref[j:j + 1, :]
            dw_ref[j:j + 1, :] = jnp.sum(dc * taps[j], axis=0, keepdims=True)
        du_ref[...] = du.astype(du_ref.dtype)
        db_ref[...] = jnp.sum(dc, axis=0, keepdims=True)

    return pl.pallas_call(
        body, name=name, grid=(width // bw,),
        in_specs=[pl.BlockSpec((s, bw), lambda j: (0, off + j)), pl.BlockSpec((CONV_W, bw), lambda j: (0, j)),
                  pl.BlockSpec((1, bw), lambda j: (0, j)), pl.BlockSpec((s, bw), lambda j: (0, j)), ANY],
        out_specs=[pl.BlockSpec((s, bw), lambda j: (0, off + j)), pl.BlockSpec((CONV_W, bw), lambda j: (0, j)),
                   pl.BlockSpec((1, bw), lambda j: (0, j))],
        out_shape=[pltpu.HBM(dz.shape, dz.dtype), pltpu.HBM((CONV_W, width), F32),
                   pltpu.HBM((1, width), F32)],
        input_output_aliases={4: 0},
        compiler_params=_params(20 * s * bw * 4 + (4 << 20)),
    )(_hbm(zm), cw, cb, _hbm(dqk), dz)


def _ml_gates(gcol_ref, grow_ref):
    l = CHUNK
    r = lax.broadcasted_iota(jnp.int32, (l, l), 0)
    c = lax.broadcasted_iota(jnp.int32, (l, l), 1)
    gcol = gcol_ref[...]
    grow = grow_ref[0]
    bcol = _u01dot((c <= r).astype(BF16), gcol)
    brow = _dot01(grow, (r <= c).astype(BF16))
    return gcol, grow, bcol, brow, r >= c


def _ml_chunk(h, dh, mq_ref, mk_ref, v_ref, gates, cp, n_prev, m_prev):
    gcol, grow, bcol, brow, tri = gates
    l = CHUNK
    sl = slice(h * dh, (h + 1) * dh)
    qc = mq_ref[:, sl]
    kc = mk_ref[:, sl] * (dh ** -0.5)
    vc = v_ref[:, sl]
    i_row = grow[h:h + 1, :]
    i_col = gcol[:, h:h + 1]
    b_col = bcol[:, ML_HEADS + h:ML_HEADS + h + 1]
    b_row = brow[ML_HEADS + h:ML_HEADS + h + 1, :]
    b_end = b_col[l - 1:l, :]
    d = jnp.where(tri, b_col - b_row + i_row, -jnp.inf)
    m_inter = b_col + m_prev
    m_t = jnp.maximum(m_inter, jnp.max(d, axis=1, keepdims=True))
    w = jnp.exp(d - m_t)
    s_inter = jnp.exp(m_inter - m_t)
    qb, kb, vb = qc.astype(BF16), kc.astype(BF16), vc.astype(BF16)
    cpb = cp.astype(BF16)
    a = lax.dot_general(qb, kb, NT, preferred_element_type=F32)
    sc = a * w
    qcp = lax.dot_general(qb, cpb, NT, preferred_element_type=F32)
    qn = jnp.sum(qc * n_prev, axis=1, keepdims=True)
    num = lax.dot_general(sc.astype(BF16), vb, NN, preferred_element_type=F32) + s_inter * qcp
    den = jnp.sum(sc, axis=1, keepdims=True) + s_inter * qn
    floor = jnp.exp(-m_t)
    dnm = jnp.maximum(jnp.abs(den), floor)
    g_col = b_end - b_col + i_col
    g_row = b_end - b_row + i_row
    m_new = jnp.maximum(b_end + m_prev, jnp.max(g_row, axis=1, keepdims=True))
    decay = jnp.exp(b_end + m_prev - m_new)
    wk = jnp.exp(g_col - m_new)
    return dict(qc=qc, kc=kc, vc=vc, qb=qb, kb=kb, vb=vb, cpb=cpb, w=w, s_inter=s_inter, a=a, sc=sc, qcp=qcp, qn=qn,
                num=num, den=den, floor=floor, dnm=dnm, m_new=m_new, decay=decay, wk=wk, sl=sl)


def _ml_fwd(mqk, zm, vcol, gcol, grow, d_model, *, name):
    s = zm.shape[0]
    nc = s // CHUNK
    dh = d_model // ML_HEADS
    hh = ML_HEADS

    def body(mq_ref, mk_ref, v_ref, gcol_ref, grow_ref, h_ref, cs_ref, ns_ref, ms_ref, c_s, n_s, m_s):
        @pl.when(pl.program_id(0) == 0)
        def _():
            c_s[...] = jnp.zeros_like(c_s)
            n_s[...] = jnp.zeros_like(n_s)
            m_s[...] = jnp.zeros_like(m_s)

        gates = _ml_gates(gcol_ref, grow_ref)
        for h in range(hh):
            cp, n_prev, m_prev = c_s[h], n_s[h], m_s[h][:, 0:1]
            cs_ref[0, h] = cp
            ns_ref[0, h] = n_prev
            ms_ref[0, h] = m_s[h]
            f = _ml_chunk(h, dh, mq_ref, mk_ref, v_ref, gates, cp, n_prev, m_prev)
            h_ref[:, f["sl"]] = f["num"] / f["dnm"]
            c_s[h] = f["decay"] * cp + lax.dot_general((f["vc"] * f["wk"]).astype(BF16), f["kb"], TN,
                                                       preferred_element_type=F32)
            n_s[h] = f["decay"] * n_prev + jnp.sum(f["wk"] * f["kc"], axis=0, keepdims=True)
            m_s[h] = jnp.broadcast_to(f["m_new"], (1, LANES))

    dblk = d_model
    return pl.pallas_call(
        body, name=name, grid=(nc,),
        in_specs=[pl.BlockSpec((CHUNK, dblk), lambda c: (c, 0)), pl.BlockSpec((CHUNK, dblk), lambda c: (c, 1)),
                  pl.BlockSpec((CHUNK, dblk), lambda c: (c, vcol // dblk)),
                  pl.BlockSpec((CHUNK, LANES), lambda c: (c, 0)), pl.BlockSpec((1, 8, CHUNK), lambda c: (c, 0, 0))],
        out_specs=[pl.BlockSpec((CHUNK, dblk), lambda c: (c, 0)),
                   pl.BlockSpec((1, hh, dh, dh), lambda c: (c, 0, 0, 0)),
                   pl.BlockSpec((1, hh, 1, dh), lambda c: (c, 0, 0, 0)),
                   pl.BlockSpec((1, hh, 1, LANES), lambda c: (c, 0, 0, 0))],
        out_shape=[pltpu.HBM((s, d_model), F32), pltpu.HBM((nc, hh, dh, dh), F32),
                   pltpu.HBM((nc, hh, 1, dh), F32), pltpu.HBM((nc, hh, 1, LANES), F32)],
        scratch_shapes=[pltpu.VMEM((hh, dh, dh), F32), pltpu.VMEM((hh, 1, dh), F32), pltpu.VMEM((hh, 1, LANES), F32)],
        compiler_params=_params(8 * hh * dh * dh * 4 + (16 << 20)),
    )(_hbm(mqk), _hbm(mqk), _hbm(zm), _hbm(gcol), _hbm(grow))


def _ml_bwd(mqk, zm, vcol, gcol, grow, cs, ns, ms, dhm, dz, d_model, *, name):
    s = zm.shape[0]
    nc = s // CHUNK
    dh = d_model // ML_HEADS
    hh = ML_HEADS
    l = CHUNK

    def body(mq_ref, mk_ref, v_ref, gcol_ref, grow_ref, cs_ref, ns_ref, ms_ref, dh_ref, dz_ref,
             dqk_ref, dv_ref, dgc_ref, dgr_ref, dc_s, dn_s):
        @pl.when(pl.program_id(0) == 0)
        def _():
            dc_s[...] = jnp.zeros_like(dc_s)
            dn_s[...] = jnp.zeros_like(dn_s)

        gates = _ml_gates(gcol_ref, grow_ref)
        lane = lax.broadcasted_iota(jnp.int32, (l, LANES), 1)
        rowi = lax.broadcasted_iota(jnp.int32, (8, l), 0)
        lastrow = lax.broadcasted_iota(jnp.int32, (l, 1), 0) == l - 1
        dgc = jnp.zeros((l, LANES), F32)
        dgr = jnp.zeros((8, l), F32)
        for h in range(hh):
            cp, n_prev, m_prev = cs_ref[0, h], ns_ref[0, h], ms_ref[0, h][:, 0:1]
            f = _ml_chunk(h, dh, mq_ref, mk_ref, v_ref, gates, cp, n_prev, m_prev)
            dC, dn = dc_s[h], dn_s[h]
            dhv = dh_ref[:, f["sl"]]
            dnum = dhv / f["dnm"]
            hv = f["num"] / f["dnm"]
            ddnm = -jnp.sum(dhv * hv, axis=1, keepdims=True) / f["dnm"]
            dden = jnp.where(jnp.abs(f["den"]) >= f["floor"], ddnm * jnp.sign(f["den"]), 0.0)
            dnb = dnum.astype(BF16)
            dsc = lax.dot_general(dnb, f["vb"], NT, preferred_element_type=F32) + dden
            dvc = lax.dot_general(f["sc"].astype(BF16), dnb, TN, preferred_element_type=F32)
            ds_inter = jnp.sum(dnum * f["qcp"], axis=1, keepdims=True) + dden * f["qn"]
            sdn = (f["s_inter"] * dnum).astype(BF16)
            sdd = f["s_inter"] * dden
            da = dsc * f["w"]
            dab = da.astype(BF16)
            dqc = (lax.dot_general(dab, f["kb"], NN, preferred_element_type=F32)
                   + lax.dot_general(sdn, f["cpb"], NN, preferred_element_type=F32) + sdd * n_prev)
            dcp = f["decay"] * dC + lax.dot_general(sdn, f["qb"], TN, preferred_element_type=F32)
            dnp = f["decay"] * dn + jnp.sum(sdd * f["qc"], axis=0, keepdims=True)
            vw = (f["vc"] * f["wk"]).astype(BF16)
            dCb = dC.astype(BF16)
            dkc = (lax.dot_general(dab, f["qb"], TN, preferred_element_type=F32)
                   + lax.dot_general(vw, dCb, NN, preferred_element_type=F32) + f["wk"] * dn)
            e = lax.dot_general(f["kb"], dCb, NT, preferred_element_type=F32)
            dvc = dvc + e * f["wk"]
            dwk = jnp.sum(e * f["vc"], axis=1, keepdims=True) + jnp.sum(f["kc"] * dn, axis=1, keepdims=True)
            ddecay = jnp.sum(jnp.sum(dC * cp, axis=1, keepdims=True), axis=0, keepdims=True) \
                + jnp.sum(dn * n_prev, axis=1, keepdims=True)
            dd = dsc * f["sc"]
            dlw = dwk * f["wk"]
            db_end = jnp.sum(dlw, axis=0, keepdims=True) + ddecay * f["decay"]
            di_col = dlw
            db_col = jnp.sum(dd, axis=1, keepdims=True) + ds_inter * f["s_inter"] - dlw \
                + jnp.where(lastrow, db_end, 0.0)
            cs_dd = jnp.sum(dd, axis=0, keepdims=True)
            dgc = dgc + jnp.where(lane == h, di_col, 0.0) + jnp.where(lane == hh + h, db_col, 0.0)
            dgr = dgr + jnp.where(rowi == h, cs_dd, 0.0) - jnp.where(rowi == hh + h, cs_dd, 0.0)
            dqk_ref[:, f["sl"]] = dqc
            dqk_ref[:, d_model + h * dh:d_model + (h + 1) * dh] = dkc * (dh ** -0.5)
            dv_ref[:, f["sl"]] = dvc.astype(dv_ref.dtype)
            dc_s[h] = dcp
            dn_s[h] = dnp
        dgc_ref[...] = dgc
        dgr_ref[0] = dgr

    dblk = d_model
    rev = lambda c: nc - 1 - c
    return pl.pallas_call(
        body, name=name, grid=(nc,),
        in_specs=[pl.BlockSpec((l, dblk), lambda c: (rev(c), 0)), pl.BlockSpec((l, dblk), lambda c: (rev(c), 1)),
                  pl.BlockSpec((l, dblk), lambda c: (rev(c), vcol // dblk)),
                  pl.BlockSpec((l, LANES), lambda c: (rev(c), 0)), pl.BlockSpec((1, 8, l), lambda c: (rev(c), 0, 0)),
                  pl.BlockSpec((1, hh, dh, dh), lambda c: (rev(c), 0, 0, 0)),
                  pl.BlockSpec((1, hh, 1, dh), lambda c: (rev(c), 0, 0, 0)),
                  pl.BlockSpec((1, hh, 1, LANES), lambda c: (rev(c), 0, 0, 0)),
                  pl.BlockSpec((l, dblk), lambda c: (rev(c), 0)), ANY],
        out_specs=[pl.BlockSpec((l, 2 * dblk), lambda c: (rev(c), 0)),
                   pl.BlockSpec((l, dblk), lambda c: (rev(c), vcol // dblk)),
                   pl.BlockSpec((l, LANES), lambda c: (rev(c), 0)),
                   pl.BlockSpec((1, 8, l), lambda c: (rev(c), 0, 0))],
        out_shape=[pltpu.HBM((s, 2 * d_model), F32),
                   pltpu.HBM(dz.shape, dz.dtype), pltpu.HBM((s, LANES), F32),
                   pltpu.HBM((nc, 8, l), F32)],
        input_output_aliases={9: 1},
        scratch_shapes=[pltpu.VMEM((hh, dh, dh), F32), pltpu.VMEM((hh, 1, dh), F32)],
        compiler_params=_params(10 * hh * dh * dh * 4 + (16 << 20)),
    )(*[_hbm(a) for a in (mqk, mqk, zm, gcol, grow, cs, ns, ms, dhm)], dz)


def _xa_fwd(zm, qcol, kv, gq, gk, d_model, *, name, tq=1024):
    s = zm.shape[0]
    nm = kv.shape[0]
    dh = d_model // X_HEADS
    tq = _pick(s, (tq, 128, 64))
    scale = dh ** -0.5

    def body(q_ref, k_ref, v_ref, gq_ref, gk_ref, o_ref):
        qn = _rms_fwd(q_ref[...], gq_ref[...])
        kn = _rms_fwd(k_ref[...], gk_ref[...])
        lg = _dot(qn, kn, NT) * scale
        lg = lg - jnp.max(lg, axis=1, keepdims=True)
        p = jnp.exp(lg)
        p = p / jnp.sum(p, axis=1, keepdims=True)
        o_ref[...] = _dot(p, v_ref[...], NN).astype(o_ref.dtype)

    return pl.pallas_call(
        body, name=name, grid=(X_HEADS, s // tq),
        in_specs=[pl.BlockSpec((tq, dh), lambda h, i: (i, qcol // dh + h)), pl.BlockSpec((nm, dh), lambda h, i: (0, h)),
                  pl.BlockSpec((nm, dh), lambda h, i: (0, X_HEADS + h)),
                  pl.BlockSpec((1, dh), lambda h, i: (0, 0)), pl.BlockSpec((1, dh), lambda h, i: (0, 0))],
        out_specs=pl.BlockSpec((tq, dh), lambda h, i: (i, h)),
        out_shape=pltpu.HBM((s, d_model), BF16),
        compiler_params=_params(32 << 20),
    )(_hbm(zm), _hbm(kv), _hbm(kv), gq, gk)


def _xa_bwd(zm, qcol, kv, gq, gk, dy, dz, d_model, *, name, tq=1024):
    s = zm.shape[0]
    nm = kv.shape[0]
    dh = d_model // X_HEADS
    tq = _pick(s, (tq, 128, 64))
    nq = s // tq
    scale = dh ** -0.5

    def body(q_ref, k_ref, v_ref, gq_ref, gk_ref, do_ref, dz_ref, dq_ref, dkn_ref, dv_ref, dgq_ref):
        h, i = pl.program_id(0), pl.program_id(1)

        @pl.when(i == 0)
        def _():
            dkn_ref[...] = jnp.zeros_like(dkn_ref)
            dv_ref[...] = jnp.zeros_like(dv_ref)

        @pl.when((i == 0) & (h == 0))
        def _():
            dgq_ref[...] = jnp.zeros_like(dgq_ref)

        q = q_ref[...]
        qn = _rms_fwd(q, gq_ref[...])
        kn = _rms_fwd(k_ref[...], gk_ref[...])
        lg = _dot(qn, kn, NT) * scale
        lg = lg - jnp.max(lg, axis=1, keepdims=True)
        p = jnp.exp(lg)
        p = p / jnp.sum(p, axis=1, keepdims=True)
        do = do_ref[...]
        dv_ref[...] += _dot(p, do, TN)
        dp = _dot(do, v_ref[...], NT)
        dlg = p * (dp - jnp.sum(dp * p, axis=1, keepdims=True)) * scale
        dqn = _dot(dlg, kn, NN)
        dkn_ref[...] += _dot(dlg, qn, TN)
        dq, dgq = _rms_bwd(q, gq_ref[...], dqn)
        dq_ref[...] = dq.astype(dq_ref.dtype)
        dgq_ref[...] += jnp.sum(dgq, axis=0, keepdims=True)

    return pl.pallas_call(
        body, name=name, grid=(X_HEADS, nq),
        in_specs=[pl.BlockSpec((tq, dh), lambda h, i: (i, qcol // dh + h)), pl.BlockSpec((nm, dh), lambda h, i: (0, h)),
                  pl.BlockSpec((nm, dh), lambda h, i: (0, X_HEADS + h)),
                  pl.BlockSpec((1, dh), lambda h, i: (0, 0)), pl.BlockSpec((1, dh), lambda h, i: (0, 0)),
                  pl.BlockSpec((tq, dh), lambda h, i: (i, h)), ANY],
        out_specs=[pl.BlockSpec((tq, dh), lambda h, i: (i, qcol // dh + h)),
                   pl.BlockSpec((nm, dh), lambda h, i: (0, h)),
                   pl.BlockSpec((nm, dh), lambda h, i: (0, h)), pl.BlockSpec((1, dh), lambda h, i: (0, 0))],
        out_shape=[pltpu.HBM(dz.shape, dz.dtype), pltpu.HBM((nm, d_model), F32),
                   pltpu.HBM((nm, d_model), F32), pltpu.HBM((1, dh), F32)],
        input_output_aliases={6: 0},
        compiler_params=_params(32 << 20),
    )(_hbm(zm), _hbm(kv), _hbm(kv), gq, gk, _hbm(dy), dz)


def _place():
    return lax.axis_index("x"), lax.axis_index("y"), lax.axis_index("c")


ANY = pl.BlockSpec(memory_space=pl.ANY)


def _allgather_two_level(big, small, *, name, chunk_rows=64):
    r, cc = big.shape
    half = r // 2
    nr = _pick(half, (chunk_rows, 32, 16))
    nq = half // nr

    def body(big_ref, small_ref, obig, osmall, land, passed, send, recv, fsend, frecv, out_a, out_b, ssend, srecv, loc):
        x, y, c = _place()
        k = 2 * x + y
        chips = [(1 - x, y), (x, 1 - y), (1 - x, 1 - y)]
        slots = [2 * px + py for px, py in chips]
        local = [pltpu.make_async_copy(big_ref, obig.at[k], loc.at[0]),
                 pltpu.make_async_copy(small_ref, osmall.at[k], loc.at[1])]
        for cp in local:
            cp.start()

        def rows(h, q):
            return pl.ds(pl.multiple_of(h * half + q * nr, nr), nr)

        def chunk(q):
            return pl.ds(q * nr, nr)

        def over_ici(j, q):
            return pltpu.make_async_remote_copy(
                src_ref=big_ref.at[rows(c, q)], dst_ref=land.at[j, chunk(q)], send_sem=send.at[nq * j + q],
                recv_sem=recv.at[nq * j + q], device_id=(chips[j][0], chips[j][1], c), device_id_type=MESH)

        def to_sibling(j, q):
            return pltpu.make_async_remote_copy(
                src_ref=land.at[j, chunk(q)], dst_ref=passed.at[j, chunk(q)], send_sem=fsend.at[nq * j + q],
                recv_sem=frecv.at[nq * j + q], device_id=(x, y, 1 - c), device_id_type=MESH)

        def small_copy(j, slot):
            return pltpu.make_async_remote_copy(
                src_ref=small_ref, dst_ref=osmall.at[slot], send_sem=ssend.at[j], recv_sem=srecv.at[j],
                device_id=(chips[j][0], chips[j][1], c), device_id_type=MESH)

        for q in range(nq):
            for j in range(3):
                over_ici(j, q).start()
        for j in range(3):
            small_copy(j, k).start()
        for q in range(nq):
            for j in range(3):
                over_ici(j, q).wait_recv()
                to_sibling(j, q).start()
                cp = pltpu.make_async_copy(land.at[j, chunk(q)], obig.at[slots[j], rows(c, q)], out_a.at[nq * j + q])
                cp.start()
                local.append(cp)
        for q in range(nq):
            for j in range(3):
                to_sibling(j, q).wait_recv()
                cp = pltpu.make_async_copy(passed.at[j, chunk(q)], obig.at[slots[j], rows(1 - c, q)],
                                           out_b.at[nq * j + q])
                cp.start()
                local.append(cp)
        for j in range(3):
            small_copy(j, slots[j]).wait_recv()
            small_copy(j, k).wait_send()
        for q in range(nq):
            for j in range(3):
                over_ici(j, q).wait_send()
                to_sibling(j, q).wait_send()
        for cp in local:
            cp.wait()

    stage = 2 * _nbytes((3, half, cc), big.dtype)
    return pl.pallas_call(
        body, name=name, in_specs=[ANY] * 2, out_specs=[ANY] * 2,
        out_shape=[pltpu.HBM((4,) + big.shape, big.dtype), pltpu.HBM((4,) + small.shape, small.dtype)],
        scratch_shapes=[pltpu.VMEM((3, half, cc), big.dtype), pltpu.VMEM((3, half, cc), big.dtype)]
        + [pltpu.SemaphoreType.DMA((3 * nq,))] * 6
        + [pltpu.SemaphoreType.DMA((3,)), pltpu.SemaphoreType.DMA((3,)), pltpu.SemaphoreType.DMA((2,))],
        compiler_params=_params(stage + stage // 8 + (4 << 20)),
    )(big, small)


HBM_SPEC = pl.BlockSpec(memory_space=pltpu.HBM)
SEM_SPEC = pl.BlockSpec(memory_space=pltpu.SEMAPHORE)
EFFECT = pltpu.SideEffectType.DATAFLOW_SIDE_EFFECTING


def _split_copies(kind, srcs, lands, send, recv):
    x, y, c = _place()
    if kind == "quarters":
        peers = [(1 - x, y, c), (x, 1 - y, c), (1 - x, 1 - y, c)]
    else:
        peers = [(x ^ ((j >> 2) & 1), y ^ ((j >> 1) & 1), c ^ (j & 1)) for j in range(1, 8)]
    npeer = len(peers)
    out = []
    for t in range(len(srcs)):
        for j, (px, py, pc) in enumerate(peers):
            if kind == "quarters":
                src, mine, theirs = srcs[t], 2 * x + y, 2 * px + py
            else:
                src, mine, theirs = srcs[t].at[2 * px + py, pc], 4 * x + 2 * y + c, 4 * px + 2 * py + pc
            mk = functools.partial(
                pltpu.make_async_remote_copy, src_ref=src, send_sem=send.at[npeer * t + j],
                recv_sem=recv.at[npeer * t + j], device_id=(px, py, pc), device_id_type=MESH)
            out.append((functools.partial(mk, dst_ref=lands[t].at[mine]),
                        functools.partial(mk, dst_ref=lands[t].at[theirs])))
    return out


def _split_start(kind, srcs, land_shapes, after, *, name):
    n = len(srcs)
    ncopies = n * (3 if kind == "quarters" else 7)

    def body(*refs):
        ins, lands = refs[:n], refs[n:2 * n]
        send, recv = refs[2 * n + 1], refs[2 * n + 2]
        token = refs[-1]
        for start, _ in _split_copies(kind, ins, lands, send, recv):
            start().start()
        token[...] = jnp.zeros_like(token)

    lands = [_hbm(lax.empty(shp, a.dtype)) for shp, a in zip(land_shapes, srcs)]
    res = pl.pallas_call(
        body, name=name, in_specs=[HBM_SPEC] * (2 * n) + [ANY],
        out_specs=[SEM_SPEC, SEM_SPEC] + [HBM_SPEC] * (2 * n) + [pl.BlockSpec(memory_space=pltpu.VMEM)],
        out_shape=[pltpu.SemaphoreType.DMA((ncopies,)), pltpu.SemaphoreType.DMA((ncopies,))]
        + [pltpu.HBM(a.shape, a.dtype) for a in srcs] + [pltpu.HBM(shp, a.dtype) for shp, a in zip(land_shapes, srcs)]
        + [jax.ShapeDtypeStruct((8, LANES), F32)],
        input_output_aliases={i: 2 + i for i in range(2 * n)},
        compiler_params=pltpu.CompilerParams(has_side_effects=EFFECT),
    )(*[_hbm(a) for a in srcs], *lands, after)
    return res[0], res[1], list(res[2:2 + n]), list(res[2 + n:2 + 2 * n]), res[-1]


def _split_wait(kind, send, recv, srcs, lands, after, *, name):
    n = len(srcs)

    def body(*refs):
        ins, lnd = refs[:n], refs[n:2 * n]
        snd, rcv = refs[2 * n], refs[2 * n + 1]
        for start, arrive in _split_copies(kind, ins, lnd, snd, rcv):
            start().wait_send()
            arrive().wait_recv()

    res = pl.pallas_call(
        body, name=name, in_specs=[HBM_SPEC] * (2 * n) + [SEM_SPEC, SEM_SPEC] + [ANY] * len(after),
        out_specs=[HBM_SPEC] * (2 * n),
        out_shape=[pltpu.HBM(a.shape, a.dtype) for a in srcs] + [pltpu.HBM(a.shape, a.dtype) for a in lands],
        input_output_aliases={i: i for i in range(2 * n)},
        compiler_params=pltpu.CompilerParams(has_side_effects=EFFECT),
    )(*srcs, *lands, send, recv, *after)
    return list(res[n:])


def _sum8(parts, *, name):
    _, r, c = parts.shape
    t = _pick(r, (128, 64, 32, 16, 8))

    def body(p_ref, o_ref):
        acc = p_ref[0].astype(F32)
        for k in range(1, 8):
            acc = acc + p_ref[k].astype(F32)
        o_ref[...] = acc

    return pl.pallas_call(
        body, name=name, grid=(r // t,), in_specs=[pl.BlockSpec((8, t, c), lambda i: (0, i, 0))],
        out_specs=pl.BlockSpec((t, c), lambda i: (i, 0)), out_shape=pltpu.HBM((r, c), F32),
        compiler_params=_params(2 * 8 * t * c * 2 + 6 * t * c * 4 + (4 << 20)),
    )(_hbm(parts))


def _swap_halves(halves, *, name, chunk_bytes=512 * 1024):
    n = len(halves)
    items = []
    for t, a in enumerate(halves):
        r = a.shape[0]
        k = 1
        while _nbytes(a.shape, a.dtype) // k > chunk_bytes and r % (2 * k) == 0 and (r // (2 * k)) % 8 == 0:
            k *= 2
        items += [(t, q * (r // k), r // k) for q in range(k)]
    m = len(items)

    def body(*refs):
        ins, outs = refs[:n], refs[n:2 * n]
        sbuf, rbuf = refs[2 * n:3 * n], refs[3 * n:4 * n]
        send, recv, loc_own, loc_in, loc_out = refs[4 * n:]
        x, y, c = _place()
        local, stage = [], []
        for t in range(n):
            cp = pltpu.make_async_copy(ins[t], outs[t].at[c], loc_own.at[t])
            cp.start()
            local.append(cp)
        for q, (t, r0, nr) in enumerate(items):
            cp = pltpu.make_async_copy(ins[t].at[pl.ds(r0, nr)], sbuf[t].at[pl.ds(r0, nr)], loc_in.at[q])
            cp.start()
            stage.append(cp)

        def copy(q):
            t, r0, nr = items[q]
            return pltpu.make_async_remote_copy(
                src_ref=sbuf[t].at[pl.ds(r0, nr)], dst_ref=rbuf[t].at[pl.ds(r0, nr)], send_sem=send.at[q],
                recv_sem=recv.at[q], device_id=(x, y, 1 - c), device_id_type=MESH)

        for q in range(m):
            stage[q].wait()
            copy(q).start()
        for q, (t, r0, nr) in enumerate(items):
            copy(q).wait_recv()
            cp = pltpu.make_async_copy(rbuf[t].at[pl.ds(r0, nr)], outs[t].at[1 - c, pl.ds(r0, nr)], loc_out.at[q])
            cp.start()
            local.append(cp)
        for q in range(m):
            copy(q).wait_send()
        for cp in local:
            cp.wait()

    stage_bytes = 2 * sum(_nbytes(a.shape, a.dtype) for a in halves)
    return pl.pallas_call(
        body, name=name, in_specs=[ANY] * n, out_specs=[ANY] * n,
        out_shape=[pltpu.HBM((2,) + a.shape, a.dtype) for a in halves],
        scratch_shapes=[pltpu.VMEM(a.shape, a.dtype) for a in halves] * 2
        + [pltpu.SemaphoreType.DMA((m,)), pltpu.SemaphoreType.DMA((m,)), pltpu.SemaphoreType.DMA((n,)),
           pltpu.SemaphoreType.DMA((m,)), pltpu.SemaphoreType.DMA((m,))],
        compiler_params=_params(stage_bytes + (4 << 20)),
    )(*halves)


def _allreduce_small(p, after, *, name):
    r = p.shape[0]

    def body(p_ref, after_ref, o_ref, buf, send, recv):
        x, y, c = _place()
        me = 4 * x + 2 * y + c
        peers = [(x ^ ((j >> 2) & 1), y ^ ((j >> 1) & 1), c ^ (j & 1)) for j in range(1, 8)]

        def copy(j, slot):
            return pltpu.make_async_remote_copy(
                src_ref=p_ref, dst_ref=buf.at[slot], send_sem=send.at[j], recv_sem=recv.at[j],
                device_id=peers[j], device_id_type=MESH)

        for j in range(7):
            copy(j, me).start()
        buf[me] = p_ref[...]
        for j in range(7):
            px, py, pc = peers[j]
            copy(j, 4 * px + 2 * py + pc).wait_recv()
        for j in range(7):
            copy(j, me).wait_send()
        acc = buf[0]
        for k in range(1, 8):
            acc = acc + buf[k]
        o_ref[...] = acc

    vspec = pl.BlockSpec(memory_space=pltpu.VMEM)
    return pl.pallas_call(
        body, name=name, in_specs=[vspec, ANY], out_specs=vspec, out_shape=jax.ShapeDtypeStruct((r, LANES), F32),
        scratch_shapes=[pltpu.VMEM((8, r, LANES), F32), pltpu.SemaphoreType.DMA((7,)), pltpu.SemaphoreType.DMA((7,))],
    )(p, after)


def _adamw_fn(w, g, m, v):
    m = ADAM_B1 * m + (1.0 - ADAM_B1) * g
    v = ADAM_B2 * v + (1.0 - ADAM_B2) * (g * g)
    m_hat = m / (1.0 - ADAM_B1 ** ADAM_STEP)
    v_hat = v / (1.0 - ADAM_B2 ** ADAM_STEP)
    delta = -ADAM_LR * (m_hat / (jnp.sqrt(v_hat) + ADAM_EPS) + ADAM_WD * w)
    return delta, m, v


def _adamw(w, g, m, v, *, name):
    c = w.shape[1]
    return _rowwise(_adamw_fn, [w, g, m, v], [], [(c, F32)] * 3, name=name, tr=128)


def _pack(vecs, rows):
    flat = jnp.concatenate([a.reshape(-1).astype(F32) for a in vecs])
    return jnp.pad(flat, (0, rows * LANES - flat.shape[0])).reshape(rows, LANES)


def _unpack(p, like):
    flat, out, o = p.reshape(-1), [], 0
    for a in like:
        out.append(flat[o:o + a.size].reshape(a.shape))
        o += a.size
    return out


def kernel(x, mem, g_mix, w_in, b_if, b_gate, conv_w, conv_b, ml_norm_g, g_mem, w_mem_kv, q_norm_g, k_norm_g, w_sb_proj, w_ml_proj, w_x_proj, w_out, g_mlp, w_ff1, w_ff2, loss_target, m_g_mix, m_w_in, m_b_if, m_b_gate, m_conv_w, m_conv_b, m_ml_norm_g, m_g_mem, m_w_mem_kv, m_q_norm_g, m_k_norm_g, m_w_sb_proj, m_w_ml_proj, m_w_x_proj, m_w_out, m_g_mlp, m_w_ff1, m_w_ff2, v_g_mix, v_w_in, v_b_if, v_b_gate, v_conv_w, v_conv_b, v_ml_norm_g, v_g_mem, v_w_mem_kv, v_q_norm_g, v_k_norm_g, v_w_sb_proj, v_w_ml_proj, v_w_x_proj, v_w_out, v_g_mlp, v_w_ff1, v_w_ff2):
    _, s, d = x.shape
    nm = mem.shape[1]
    n_in = 4 * w_in.shape[2]
    dff = 4 * w_ff1.shape[2]
    sbh = d // SB_HD
    hh = ML_HEADS
    dh = d // hh
    nc = s // CHUNK
    assert n_in == 11 * d + 2 * hh and d % (2 * LANES) == 0 and s % LANES == 0
    x2, mem2, tgt = x[0], mem[0], loss_target[0]

    k4 = 2 * lax.axis_index("x") + lax.axis_index("y")
    me = 2 * k4 + lax.axis_index("c")
    g_first = _allgather_two_level(w_in[0].astype(BF16), conv_w[0], name="gather_w_in")
    later = [a[0].astype(BF16) for a in (w_mem_kv, w_sb_proj, w_ml_proj, w_x_proj, w_out, w_ff1, w_ff2)]
    gw_send, gw_recv, gw_src, gw_land, gw_token = _split_start(
        "quarters", later, [(4,) + a.shape for a in later], g_first[0], name="gather_rest_start")
    cols = lambda a: a.transpose(1, 0, 2).reshape(a.shape[1], 4 * a.shape[2])
    rws = lambda a: a.reshape(4 * a.shape[1], a.shape[2])
    qn = n_in // 4
    if_lo, if_hi = 7 * d, 7 * d + 2 * hh

    def cut(lo, hi):
        ks = [(k, max(lo, k * qn), min(hi, (k + 1) * qn)) for k in range(4)]
        return [g_first[0][k, :, a - k * qn:b - k * qn] for k, a, b in ks if a < b]

    w_main = jnp.concatenate(cut(0, if_lo) + cut(if_hi, n_in), axis=1)
    w_if = jnp.pad(jnp.concatenate(cut(if_lo, if_hi), axis=1), ((0, 0), (0, LANES - 2 * hh)))
    conv_wf = cols(g_first[1])
    b_if_p = jnp.pad(b_if, ((0, 0), (0, LANES - 2 * hh)))

    (hn,) = _rowwise(_rms_fwd, [x2], [g_mix], [(d, BF16)], name="norm_in", tr=512)
    zm = _mm(hn, w_main, after=gw_token, name="proj_in")
    zif = _mm(hn, w_if, name="proj_if")
    y_sb, a_sb = _sb_fwd(zm, sbh, name="sb_fwd")

    def gate_fn(z, b):
        pre = z + b
        lane = lax.broadcasted_iota(jnp.int32, pre.shape, 1)
        return jnp.where(lane < hh, pre, -_softplus(-pre))

    (gcol,) = _rowwise(gate_fn, [zif], [b_if_p], [(LANES, F32)], name="ml_gates", tr=1024)
    grow = gcol[:, :8].T.reshape(8, nc, CHUNK).transpose(1, 0, 2)
    mqk = _conv_fwd(zm, 3 * d, 2 * d, conv_wf, conv_b, name="conv_fwd")
    hm, cst, nst, mst = _ml_fwd(mqk, zm, 5 * d, gcol, grow, d, name="ml_fwd")

    def mlout_fn(hv, o, g):
        ys = [_rms_fwd(hv[:, k * dh:(k + 1) * dh], g[:, k * dh:(k + 1) * dh]) for k in range(hh)]
        return jnp.concatenate(ys, axis=1) * _sigmoid(o)

    (y_ml,) = _rowwise(mlout_fn, [hm, (zm, d, 6)], [ml_norm_g], [(d, BF16)], name="ml_out", tr=512)
    gw_land = _split_wait("quarters", gw_send, gw_recv, gw_src, gw_land, [y_ml, y_sb], name="gather_rest_wait")
    gw = [lax.dynamic_update_index_in_dim(ld, a, k4, 0) for ld, a in zip(gw_land, later)]
    w_kv, w_sbp, w_mlp, w_xp, w_o, w_f1, w_f2 = (cols(gw[0]), rws(gw[1]), rws(gw[2]), rws(gw[3]), rws(gw[4]),
                                                 cols(gw[5]), rws(gw[6]))
    (memn,) = _rowwise(_rms_fwd, [mem2], [g_mem], [(d, BF16)], name="norm_mem")
    kv = _mm(memn, w_kv, name="proj_kv")
    y_x = _xa_fwd(zm, 7 * d, kv, q_norm_g, k_norm_g, d, name="xa_fwd")
    p_sb = _mm(y_sb, w_sbp, name="proj_sb")
    p_ml = _mm(y_ml, w_mlp, name="proj_ml")
    p_x = _mm(y_x, w_xp, name="proj_x")

    def merge_fn(a, b, c, g0, g1, g2, bg):
        return (_sigmoid(g0 + bg[:, :d]) * a + _sigmoid(g1 + bg[:, d:2 * d]) * b + _sigmoid(g2 + bg[:, 2 * d:]) * c)

    gate_cols = [(zm, d, 8), (zm, d, 9), (zm, d, 10)]
    (mixed,) = _rowwise(merge_fn, [p_sb, p_ml, p_x] + gate_cols, [b_gate], [(d, BF16)], name="merge", tr=512)
    x1 = _mm(mixed, w_o, tiles=[x2], name="proj_out")
    (h2,) = _rowwise(_rms_fwd, [x1], [g_mlp], [(d, BF16)], name="norm_mlp", tr=512)
    u, act = _mm(h2, w_f1, post=lambda r: (r, jnp.square(jnp.maximum(r, 0.0))), out_dtype=(F32, BF16), name="ff1")
    dy = _mm(act, w_f2, tiles=[x1, tgt], post=lambda r, xv, tv: (r + xv - tv) * (1.0 / d), name="ff2")
    (loss_cols,) = _rowwise(lambda g: (jnp.sum(g * g, axis=0, keepdims=True) * (0.5 * d),), [dy], [], [], [d],
                            name="loss", tr=1024)

    du = _mm(dy, w_f2, tb=True, tiles=[u], post=lambda r, uv: r * 2.0 * jnp.maximum(uv, 0.0), out_dtype=BF16,
             name="ff2_dx")
    dw_f2 = _mm(act, dy, ta=True, name="ff2_dw")
    dw_f1 = _mm(h2, du, ta=True, name="ff1_dw")
    dh2 = _mm(du, w_f1, tb=True, name="ff1_dx")

    def norm_bwd_fn(xv, dyv, res, g):
        dx, dg = _rms_bwd(xv, g, dyv)
        return dx + res, jnp.sum(dg, axis=0, keepdims=True)

    dx1, dg_mlp = _rowwise(norm_bwd_fn, [x1, dh2, dy], [g_mlp], [(d, F32)], [d], name="norm_mlp_bwd", tr=512)
    dmixed = _mm(dx1, w_o, tb=True, name="proj_out_dx")
    dw_o = _mm(mixed, dx1, ta=True, name="proj_out_dw")

    def merge_bwd_fn(dm, a, b, c, g0, g1, g2, bg):
        outs, dgs = [], []
        for p, g, k in ((a, g0, 0), (b, g1, 1), (c, g2, 2)):
            sg = _sigmoid(g + bg[:, k * d:(k + 1) * d])
            outs.append(dm * sg)
            dgs.append(dm * p * sg * (1.0 - sg))
        dgate = jnp.concatenate(dgs, axis=1)
        return (*outs, dgate, jnp.sum(dgate, axis=0, keepdims=True))

    dp_sb, dp_ml, dp_x, dgate, db_gate = _rowwise(
        merge_bwd_fn, [dmixed, p_sb, p_ml, p_x] + gate_cols, [b_gate], [(d, BF16)] * 3 + [(3 * d, BF16)], [3 * d],
        name="merge_bwd", tr=256)
    dw_sbp = _mm(y_sb, dp_sb, ta=True, name="proj_sb_dw")
    dw_mlp = _mm(y_ml, dp_ml, ta=True, name="proj_ml_dw")
    dw_xp = _mm(y_x, dp_x, ta=True, name="proj_x_dw")
    dy_sb = _mm(dp_sb, w_sbp, tb=True, out_dtype=BF16, name="proj_sb_dx")
    dy_ml = _mm(dp_ml, w_mlp, tb=True, name="proj_ml_dx")
    dy_x = _mm(dp_x, w_xp, tb=True, out_dtype=BF16, name="proj_x_dx")

    dzm = _hbm(lax.empty((s, 11 * d), BF16))
    dzm, dkn, dxv, dg_qn = _xa_bwd(zm, 7 * d, kv, q_norm_g, k_norm_g, dy_x, dzm, d, name="xa_bwd")

    def knorm_bwd_fn(kvv, dknv, dvv, g):
        dks, dgs = [], []
        for k in range(X_HEADS):
            sl = slice(k * dh, (k + 1) * dh)
            dk, dg = _rms_bwd(kvv[:, sl], g, dknv[:, sl])
            dks.append(dk)
            dgs.append(jnp.sum(dg, axis=0, keepdims=True))
        return jnp.concatenate(dks + [dvv], axis=1), dgs[0] + dgs[1] + dgs[2] + dgs[3]

    dkv, dg_kn = _rowwise(knorm_bwd_fn, [(kv, d, 0), dkn, dxv], [k_norm_g], [(2 * d, BF16)], [dh], name="xa_knorm_bwd")
    dw_kv = _mm(memn, dkv, ta=True, name="proj_kv_dw")
    dmemn = _mm(dkv, w_kv, tb=True, name="proj_kv_dx")

    def gmem_fn(mv, dv_, g):
        _, dg = _rms_bwd(mv, g, dv_)
        return (jnp.sum(dg, axis=0, keepdims=True),)

    (dg_mem,) = _rowwise(gmem_fn, [mem2, dmemn], [g_mem], [], [d], name="norm_mem_bwd")

    uncols = lambda a: a.reshape(a.shape[0], 4, a.shape[1] // 4).transpose(1, 0, 2)
    unrws = lambda a: a.reshape(4, a.shape[0] // 4, a.shape[1])
    to_parts = lambda q: q.astype(BF16).reshape(4, 2, q.shape[1] // 2, q.shape[2])
    early = [to_parts(q) for q in (uncols(dw_kv), unrws(dw_sbp), unrws(dw_mlp), unrws(dw_xp), unrws(dw_o),
                                   uncols(dw_f1), unrws(dw_f2))]
    ge_send, ge_recv, ge_src, ge_land, ge_token = _split_start(
        "grads", early, [(8,) + a.shape[2:] for a in early], dg_mem, name="exchange_early_start")

    def mlout_bwd_fn(dyv, hv, o, g):
        sg = _sigmoid(o)
        dn = dyv * sg
        dxs, dgs, ys = [], [], []
        for k in range(hh):
            sl = slice(k * dh, (k + 1) * dh)
            ys.append(_rms_fwd(hv[:, sl], g[:, sl]))
            dxk, dgk = _rms_bwd(hv[:, sl], g[:, sl], dn[:, sl])
            dxs.append(dxk)
            dgs.append(dgk)
        do = dyv * jnp.concatenate(ys, axis=1) * sg * (1.0 - sg)
        return jnp.concatenate(dxs, axis=1), do, jnp.sum(jnp.concatenate(dgs, axis=1), axis=0, keepdims=True)

    dhm, dzm, dg_mln = _rowwise(mlout_bwd_fn, [dy_ml, hm, (zm, d, 6)], [ml_norm_g], [(d, F32), (d, BF16)], [d],
                                name="ml_out_bwd", tr=512, into=(dzm, 1, 6))
    dmqk, dzm, dgc, dgr = _ml_bwd(mqk, zm, 5 * d, gcol, grow, cst, nst, mst, dhm, dzm, d, name="ml_bwd")
    dzm, dconv_w, dconv_b = _conv_bwd(zm, 3 * d, 2 * d, conv_wf, conv_b, dmqk, dzm, name="conv_bwd")
    dzm, dsk, dsv = _sb_bwd(zm, dy_sb, a_sb, dzm, ge_token, sbh, name="sb_bwd")
    dgr_t = jnp.pad(dgr.transpose(1, 0, 2).reshape(8, s).T, ((0, 0), (0, LANES - 8)))

    def gate_bwd_fn(a, b, z, bias):
        tot = a + b
        rows_t = tot.shape[0]
        r = lax.broadcasted_iota(jnp.int32, (rows_t, rows_t), 0)
        c = lax.broadcasted_iota(jnp.int32, (rows_t, rows_t), 1)
        sh = CHUNK.bit_length() - 1
        same_chunk = jnp.right_shift(r, sh) == jnp.right_shift(c, sh)
        dlf = _u01dot(((c >= r) & same_chunk).astype(BF16), tot)
        lane = lax.broadcasted_iota(jnp.int32, tot.shape, 1)
        dz = jnp.where(lane < hh, tot, jnp.where(lane < 2 * hh, dlf * _sigmoid(-(z + bias)), 0.0))
        return dz, jnp.sum(dz, axis=0, keepdims=True)

    dzif, db_if_p = _rowwise(gate_bwd_fn, [dgc, dgr_t, zif], [b_if_p], [(LANES, BF16)], [LANES], name="ml_gates_bwd",
                             tr=8 * CHUNK)
    for part, col in ((dsk, d), (dsv, 2 * d), (dgate, 8 * d)):
        dzm = lax.dynamic_update_slice(dzm, part, (0, col))
    dw_main = _mm(hn, dzm, ta=True, out_dtype=BF16, name="proj_in_dw")
    dw_if = _mm(hn, dzif, ta=True, out_dtype=BF16, name="proj_if_dw")

    def dw_quarter(k):
        lo, hi = k * qn, (k + 1) * qn
        segs = [(dw_main, 0, if_lo, 0), (dw_if, if_lo, if_hi, if_lo), (dw_main, if_hi, n_in, 2 * hh)]
        got = [src[:, max(lo, a) - off:min(hi, b) - off] for src, a, b, off in segs if max(lo, a) < min(hi, b)]
        return jnp.concatenate(got, axis=1)

    late = [to_parts(jnp.stack([dw_quarter(k) for k in range(4)]))]
    gl_send, gl_recv, gl_src, gl_land, gl_token = _split_start(
        "grads", late, [(8,) + a.shape[2:] for a in late], dw_if, name="exchange_late_start")
    dhn = _mm(dzm, w_main, tb=True, after=gl_token, name="proj_in_dx")
    dhn = _mm(dzif, w_if, tb=True, tiles=[dhn], name="proj_if_dx")
    dx, dg_mix = _rowwise(norm_bwd_fn, [x2, dhn, dx1], [g_mix], [(d, F32)], [d], name="norm_in_bwd", tr=512)

    own = lambda p: lax.dynamic_index_in_dim(lax.dynamic_index_in_dim(p, k4, 0, keepdims=False),
                                             lax.axis_index("c"), 0, keepdims=False)

    def finish(tag, send, recv, src, land, parts, after, ws, ms, vs):
        land = _split_wait("grads", send, recv, src, land, after, name=f"exchange_{tag}_wait")
        got = [lax.dynamic_update_index_in_dim(ld, own(p), me, 0) for ld, p in zip(land, parts)]
        halves = [_sum8(r, name=f"sum_grads_{tag}{i}") for i, r in enumerate(got)]
        both = _swap_halves(halves, name=f"swap_halves_{tag}")
        gs = [b.reshape(2 * b.shape[1], b.shape[2]) for b in both]
        return gs, [_adamw(w, g, m, v, name=f"adamw_{tag}{i}") for i, (w, g, m, v) in enumerate(zip(ws, gs, ms, vs))]

    first = lambda arrs: [a[0] for a in arrs]
    g_early, out_early = finish(
        "early", ge_send, ge_recv, ge_src, ge_land, early, [dx],
        first([w_mem_kv, w_sb_proj, w_ml_proj, w_x_proj, w_out, w_ff1, w_ff2]),
        first([m_w_mem_kv, m_w_sb_proj, m_w_ml_proj, m_w_x_proj, m_w_out, m_w_ff1, m_w_ff2]),
        first([v_w_mem_kv, v_w_sb_proj, v_w_ml_proj, v_w_x_proj, v_w_out, v_w_ff1, v_w_ff2]))
    g_late, out_late = finish(
        "late", gl_send, gl_recv, gl_src, gl_land, late, [o[0] for o in out_early],
        first([w_in]), first([m_w_in]), first([v_w_in]))
    g_big = [g[None] for g in g_late + g_early]
    big_out = [[o[None] for o in outs] for outs in out_late + out_early]

    small_g = [dg_mix, db_if_p[:, :2 * hh], db_gate, dconv_w, dconv_b, dg_mln, dg_mem, dg_qn, dg_kn, dg_mlp,
               jnp.sum(loss_cols).reshape(1, 1)]
    n_small = sum(a.size for a in small_g)
    rows = -(-n_small // (8 * LANES)) * 8
    g_small = _unpack(_allreduce_small(_pack(small_g, rows), out_late[0][0], name="allreduce_small"), small_g)
    loss = g_small[-1].reshape(())
    qw = conv_w.shape[2]
    g_conv_w = lax.dynamic_slice_in_dim(g_small[3], k4 * qw, qw, axis=1)
    g_small_w = [g_small[0], g_small[1], g_small[2], g_conv_w] + g_small[4:10]
    sm_w = [g_mix, b_if, b_gate, conv_w[0], conv_b, ml_norm_g, g_mem, q_norm_g, k_norm_g, g_mlp]
    sm_m = [m_g_mix, m_b_if, m_b_gate, m_conv_w[0], m_conv_b, m_ml_norm_g, m_g_mem, m_q_norm_g, m_k_norm_g, m_g_mlp]
    sm_v = [v_g_mix, v_b_if, v_b_gate, v_conv_w[0], v_conv_b, v_ml_norm_g, v_g_mem, v_q_norm_g, v_k_norm_g, v_g_mlp]
    n_sw = sum(a.size for a in sm_w)
    rows_w = -(-n_sw // (8 * LANES)) * 8
    sm_out = _adamw(_pack(sm_w, rows_w), _pack(g_small_w, rows_w), _pack(sm_m, rows_w), _pack(sm_v, rows_w),
                    name="adamw_small")
    sm_delta, sm_newm, sm_newv = [_unpack(p, sm_w) for p in sm_out]

    order = ["g_mix", "w_in", "b_if", "b_gate", "conv_w", "conv_b", "ml_norm_g", "g_mem", "w_mem_kv", "q_norm_g",
             "k_norm_g", "w_sb_proj", "w_ml_proj", "w_x_proj", "w_out", "g_mlp", "w_ff1", "w_ff2"]
    small_names = ["g_mix", "b_if", "b_gate", "conv_w", "conv_b", "ml_norm_g", "g_mem", "q_norm_g", "k_norm_g", "g_mlp"]
    big_names = ["w_in", "w_mem_kv", "w_sb_proj", "w_ml_proj", "w_x_proj", "w_out", "w_ff1", "w_ff2"]
    grads, deltas, new_m, new_v = {}, {}, {}, {}
    for i, nme in enumerate(small_names):
        shp = sm_w[i].shape if nme != "conv_w" else conv_w.shape
        grads[nme] = g_small_w[i].reshape(shp)
        deltas[nme], new_m[nme], new_v[nme] = (sm_delta[i].reshape(shp), sm_newm[i].reshape(shp),
                                               sm_newv[i].reshape(shp))
    for i, nme in enumerate(big_names):
        grads[nme] = g_big[i]
        deltas[nme], new_m[nme], new_v[nme] = big_out[i]
    return (loss, dx[None], *[grads[k] for k in order], *[deltas[k] for k in order], *[new_m[k] for k in order],
            *[new_v[k] for k in order])
```

```python
import functools

import jax
import jax.numpy as jnp
from jax import lax
from jax.experimental import pallas as pl
from jax.experimental.pallas import tpu as pltpu

F32 = jnp.float32
BF16 = jnp.bfloat16
MESH = pl.DeviceIdType.MESH

EPS = 1e-6
SB_HD = 128
SB_SLOTS = 8
ML_HEADS = 4
X_HEADS = 4
CHUNK = 64
CONV_W = 4
LANES = 128
ADAM_LR = 0.001
ADAM_B1 = 0.9
ADAM_B2 = 0.999
ADAM_EPS = 1e-08
ADAM_WD = 0.01
ADAM_STEP = 10
VMEM_CAP = 56 * 1024 * 1024
NEG = -1e30

NT = (((1,), (1,)), ((), ()))
NN = (((1,), (0,)), ((), ()))
TN = (((0,), (0,)), ((), ()))


def _dot(a, b, dn=NN):
    return lax.dot_general(a.astype(BF16), b.astype(BF16), dn, preferred_element_type=F32)


def _dot01(x, u, dn=NN):
    hi = x.astype(BF16)
    lo = (x - hi.astype(F32)).astype(BF16)
    return (lax.dot_general(hi, u, dn, preferred_element_type=F32)
            + lax.dot_general(lo, u, dn, preferred_element_type=F32))


def _u01dot(u, x):
    hi = x.astype(BF16)
    lo = (x - hi.astype(F32)).astype(BF16)
    return (lax.dot_general(u, hi, NN, preferred_element_type=F32)
            + lax.dot_general(u, lo, NN, preferred_element_type=F32))


def _pick(n, cands):
    for c in cands:
        if c <= n and n % c == 0:
            return c
    return n


def _nbytes(shape, dtype):
    n = 1
    for s in shape:
        n *= s
    return n * jnp.dtype(dtype).itemsize


def _params(vmem_bytes):
    return pltpu.CompilerParams(vmem_limit_bytes=int(min(VMEM_CAP, max(vmem_bytes, 16 * 1024 * 1024))))


def _hbm(a):
    return pltpu.with_memory_space_constraint(a, pltpu.HBM)


def _softplus(z):
    return jnp.maximum(z, 0.0) + jnp.log(1.0 + jnp.exp(-jnp.abs(z)))


def _sigmoid(z):
    return 1.0 / (1.0 + jnp.exp(-z))


def _rms_fwd(xv, g):
    r = lax.rsqrt(jnp.mean(xv * xv, axis=-1, keepdims=True) + EPS)
    return xv * r * g


def _rms_bwd(xv, g, dy):
    r = lax.rsqrt(jnp.mean(xv * xv, axis=-1, keepdims=True) + EPS)
    xh = xv * r
    dxh = dy * g
    dx = r * (dxh - xh * jnp.mean(dxh * xh, axis=-1, keepdims=True))
    return dx, dy * xh


def _mm(a, b, *, name, ta=False, tb=False, tiles=(), post=None, out_dtype=F32, bm=1024, bn=1024, bk=1024, after=None):
    m, k = (a.shape[1], a.shape[0]) if ta else a.shape
    n = b.shape[0] if tb else b.shape[1]
    tm = _pick(m, (bm, 512, 256, 128))
    tn = _pick(n, (bn, 512, 256, 128))
    tk = _pick(k, (bk, 512, 256, 128))
    nk = k // tk
    if (m // tm) * (n // tn) * nk < 8 and tm % 256 == 0:
        tm //= 2
    dn = (((0 if ta else 1,), (1 if tb else 0,)), ((), ()))
    dts = out_dtype if isinstance(out_dtype, tuple) else (out_dtype,)
    nt, no = len(tiles), len(dts)
    if post is None:
        post = lambda r, *ts: sum((t.astype(F32) for t in ts), r)

    def body(*refs):
        a_ref, b_ref = refs[:2]
        t_refs = refs[2:2 + nt]
        o_refs = refs[2 + nt + (after is not None):2 + nt + (after is not None) + no]
        part = lax.dot_general(a_ref[...].astype(BF16), b_ref[...].astype(BF16), dn, preferred_element_type=F32)

        def finish(r):
            res = post(r, *[t[...] for t in t_refs])
            res = res if isinstance(res, tuple) else (res,)
            for o, v in zip(o_refs, res):
                o[...] = v.astype(o.dtype)

        if nk == 1:
            finish(part)
        else:
            acc_ref = refs[-1]
            kk = pl.program_id(2)

            @pl.when(kk == 0)
            def _():
                acc_ref[...] = part

            @pl.when(kk > 0)
            def _():
                acc_ref[...] += part

            @pl.when(kk == nk - 1)
            def _():
                finish(acc_ref[...])

    a_spec = pl.BlockSpec((tk, tm), lambda i, j, q: (q, i)) if ta else pl.BlockSpec((tm, tk), lambda i, j, q: (i, q))
    b_spec = pl.BlockSpec((tn, tk), lambda i, j, q: (j, q)) if tb else pl.BlockSpec((tk, tn), lambda i, j, q: (q, j))
    o_spec = pl.BlockSpec((tm, tn), lambda i, j, q: (i, j))
    ins, specs = [_hbm(a), _hbm(b)] + [_hbm(t) for t in tiles], [a_spec, b_spec] + [o_spec] * nt
    vm = 2 * (_nbytes((tm, tk), a.dtype) + _nbytes((tk, tn), b.dtype)) + 3 * _nbytes((tm, tn), F32) \
        + _nbytes((tm, tk), BF16) + _nbytes((tk, tn), BF16) \
        + 2 * sum(_nbytes((tm, tn), t.dtype) for t in tiles) + 2 * sum(_nbytes((tm, tn), dt) for dt in dts)
    if after is not None:
        ins.append(after)
        specs.append(ANY)
    res = pl.pallas_call(
        body, name=name, grid=(m // tm, n // tn, nk), in_specs=specs, out_specs=[o_spec] * no,
        out_shape=[pltpu.HBM((m, n), dt) for dt in dts], scratch_shapes=[pltpu.VMEM((tm, tn), F32)] if nk > 1 else [],
        compiler_params=_params(vm + (4 << 20)),
    )(*ins)
    return res[0] if no == 1 else tuple(res)


def _rowwise(fn, rows, consts, outs, reds=(), *, name, tr=256, temps=6, into=None):
    rows = [r if isinstance(r, tuple) else (r, r.shape[1], 0) for r in rows]
    nrows = rows[0][0].shape[0]
    t = _pick(nrows, (tr, 128, 64, 32, 16, 8))
    nr, nc, no = len(rows), len(consts), len(outs)
    nb = 0 if into is None else 1

    def body(*refs):
        rin, cin = refs[:nr], refs[nr:nr + nc]
        oref, rref = refs[nr + nc + nb:nr + nc + nb + no], refs[nr + nc + nb + no:]
        res = fn(*[r[...] for r in rin], *[c[...] for c in cin])
        if not isinstance(res, (tuple, list)):
            res = (res,)
        for o, v in zip(oref, res[:no]):
            o[...] = v.astype(o.dtype)
        if rref:
            @pl.when(pl.program_id(0) == 0)
            def _():
                for r in rref:
                    r[...] = jnp.zeros_like(r)

            for r, v in zip(rref, res[no:]):
                r[...] += v

    in_specs = [pl.BlockSpec((t, w), functools.partial(lambda i, ci: (i, ci), ci=ci)) for (_, w, ci) in rows]
    in_specs += [pl.BlockSpec(c.shape, functools.partial(lambda i, nd: (0,) * nd, nd=c.ndim)) for c in consts]
    out_specs = [pl.BlockSpec((t, w), lambda i: (i, 0)) for (w, _) in outs]
    out_specs += [pl.BlockSpec((1, w), lambda i: (0, 0)) for w in reds]
    out_shape = [pltpu.HBM((nrows, w), dt) for (w, dt) in outs]
    out_shape += [jax.ShapeDtypeStruct((1, w), F32) for w in reds]
    widest = max([w for (_, w, _) in rows] + [w for (w, _) in outs])
    vm = 2 * sum(_nbytes((t, w), a.dtype) for (a, w, _) in rows) + 2 * sum(_nbytes((t, w), dt) for (w, dt) in outs)
    vm += temps * _nbytes((t, widest), F32) + (2 << 20)
    extra, aliases = [], {}
    if into is not None:
        buf, oi, cb = into
        out_specs[oi] = pl.BlockSpec((t, outs[oi][0]), lambda i: (i, cb))
        out_shape[oi] = pltpu.HBM(buf.shape, buf.dtype)
        in_specs.append(ANY)
        extra, aliases = [buf], {nr + nc: oi}
    res = pl.pallas_call(
        body, name=name, grid=(nrows // t,), in_specs=in_specs, out_specs=out_specs, out_shape=out_shape,
        input_output_aliases=aliases, compiler_params=_params(vm),
    )(*[_hbm(a) for (a, _, _) in rows], *consts, *extra)
    return list(res)


def _norm_stream(xv, g, *, name, t=256, nbuf=3):
    r, c = xv.shape
    t = _pick(r, (t, 128, 64, 32, 16, 8))
    n = r // t

    def body(x_any, g_ref, o_ref, buf, sem):
        i = pl.program_id(0)

        def fetch(step, slot):
            return pltpu.make_async_copy(x_any.at[pl.ds(pl.multiple_of(step * t, t), t)], buf.at[slot], sem.at[slot])

        @pl.when(i == 0)
        def _():
            for k in range(min(nbuf - 1, n)):
                fetch(k, k).start()

        @pl.when(i + nbuf - 1 < n)
        def _():
            fetch(i + nbuf - 1, (i + nbuf - 1) % nbuf).start()

        fetch(i, i % nbuf).wait()
        o_ref[...] = _rms_fwd(buf[i % nbuf], g_ref[...]).astype(o_ref.dtype)

    return pl.pallas_call(
        body, name=name, grid=(n,), in_specs=[ANY, pl.BlockSpec((1, c), lambda i: (0, 0))],
        out_specs=pl.BlockSpec((t, c), lambda i: (i, 0)), out_shape=pltpu.HBM((r, c), BF16),
        scratch_shapes=[pltpu.VMEM((nbuf, t, c), xv.dtype), pltpu.SemaphoreType.DMA((nbuf,))],
        compiler_params=_params((nbuf + 6) * _nbytes((t, c), F32) + (2 << 20)),
    )(_hbm(xv), g)


def _sb_tiles(s, tq, tk):
    tq = _pick(s, (tq, 256, 128))
    tk = _pick(tq, (tk, 128))
    return tq, tk, tq // tk


def _sb_fwd(zm, heads, *, name, tq=512, tk=256):
    s = zm.shape[0]
    tq, tk, nd = _sb_tiles(s, tq, tk)
    scale = SB_HD ** -0.5

    def body(q_ref, k_ref, v_ref, o_ref, a_out, stage, sem):
        h, i = pl.program_id(0), pl.program_id(1)
        qb = (q_ref[...] * scale).astype(BF16)
        r = lax.broadcasted_iota(jnp.int32, (tq, tk), 0)
        c = lax.broadcasted_iota(jnp.int32, (tq, tk), 1)
        ur = lax.broadcasted_iota(jnp.int32, (tk, tk), 0)
        uc = lax.broadcasted_iota(jnp.int32, (tk, tk), 1)
        usuf = (ur > uc).astype(BF16)

        def out_copy(slot, j):
            return pltpu.make_async_copy(stage.at[slot], a_out.at[h, i, j], sem.at[slot])

        def tile(j, carry, causal, slot, reuse):
            acc, cl = carry
            if reuse is True:
                out_copy(slot, 0).wait()
            elif reuse is not None:
                @pl.when(reuse)
                def _():
                    out_copy(slot, 0).wait()
            rows = pl.ds(pl.multiple_of(j * tk, tk), tk)
            kb = k_ref[rows, :].astype(BF16)
            vb = v_ref[rows, :].astype(BF16)
            z = lax.dot_general(qb, kb, NT, preferred_element_type=F32)
            lsig = -_softplus(z)
            l = lsig if causal is None else jnp.where(causal, lsig, 0.0)
            loga = z + lsig + _dot01(l, usuf) + cl
            if causal is not None:
                loga = jnp.where(causal, loga, NEG)
            ab = jnp.exp(loga).astype(BF16)
            acc = acc + lax.dot_general(ab, vb, NN, preferred_element_type=F32)
            stage[slot] = ab
            out_copy(slot, j).start()
            return acc, cl + jnp.sum(l, axis=1, keepdims=True)

        carry = (jnp.zeros((tq, SB_HD), F32), jnp.zeros((tq, 1), F32))
        for n, dd in enumerate(range(nd - 1, -1, -1)):
            carry = tile(i * nd + dd, carry, c + dd * tk < r, n, None)

        if nd == 2:
            slots = 4

            def pair(n, cr):
                s0 = (2 + 2 * n) % slots

                @pl.when(n >= 1)
                def _():
                    out_copy(s0, 0).wait()
                    out_copy(s0 + 1, 0).wait()

                return tile(i * nd - 2 - 2 * n, tile(i * nd - 1 - 2 * n, cr, None, s0, None), None, s0 + 1, None)

            acc, _ = lax.fori_loop(0, i, pair, carry)
        else:
            slots = SB_SLOTS

            def rest(n, cr):
                return tile(i * nd - 1 - n, cr, None, (nd + n) % SB_SLOTS, nd + n >= SB_SLOTS)

            acc, _ = lax.fori_loop(0, i * nd, rest, carry)
        total = (i + 1) * nd
        for back in range(1, slots + 1):
            @pl.when(total >= back)
            def _():
                out_copy((total - back) % slots, 0).wait()

        o_ref[...] = acc.astype(o_ref.dtype)

    assert nd <= SB_SLOTS
    blk = lambda off: pl.BlockSpec((s, SB_HD), functools.partial(lambda h, i, off: (0, off + h), off=off))
    return pl.pallas_call(
        body, name=name, grid=(heads, s // tq),
        in_specs=[pl.BlockSpec((tq, SB_HD), lambda h, i: (i, h)), blk(heads), blk(2 * heads)],
        out_specs=[pl.BlockSpec((tq, SB_HD), lambda h, i: (i, h)), ANY],
        out_shape=[pltpu.HBM((s, heads * SB_HD), BF16), pltpu.HBM((heads, s // tq, s // tk, tq, tk), BF16)],
        scratch_shapes=[pltpu.VMEM((SB_SLOTS, tq, tk), BF16), pltpu.SemaphoreType.DMA((SB_SLOTS,))],
        compiler_params=_params(8 * s * SB_HD * 4 + 24 * tq * tk * 4 + (8 << 20)),
    )(_hbm(zm), _hbm(zm), _hbm(zm))


def _sb_bwd(zm, dy, a_all, dz, after, heads, *, name, tq=512, tk=256):
    s = zm.shape[0]
    tq, tk, nd = _sb_tiles(s, tq, tk)
    nq = s // tq
    scale = SB_HD ** -0.5

    def body(q_ref, k_ref, v_ref, do_ref, a_in, dz_ref, after_ref, dq_ref, dk_ref, dv_ref, dka, dva, abuf, sem):
        h, i = pl.program_id(0), pl.program_id(1)

        @pl.when(i == 0)
        def _():
            dka[...] = jnp.zeros_like(dka)
            dva[...] = jnp.zeros_like(dva)

        qb = (q_ref[...] * scale).astype(BF16)
        dob = do_ref[...].astype(BF16)
        qb_t = (q_ref[...] * scale).T.astype(BF16)
        dob_t = do_ref[...].astype(F32).T.astype(BF16)
        r = lax.broadcasted_iota(jnp.int32, (tq, tk), 0)
        c = lax.broadcasted_iota(jnp.int32, (tq, tk), 1)
        ur = lax.broadcasted_iota(jnp.int32, (tk, tk), 0)
        uc = lax.broadcasted_iota(jnp.int32, (tk, tk), 1)
        uexcl = (ur < uc).astype(BF16)

        def fetch(j, slot):
            return pltpu.make_async_copy(a_in.at[h, i, j], abuf.at[slot], sem.at[slot])

        total = (i + 1) * nd
        ahead = SB_SLOTS - 1 - (nd == 2)

        def arrive(j):
            fetch(j, j % SB_SLOTS).wait()

            @pl.when(j + ahead < total)
            def _():
                fetch(j + ahead, (j + ahead) % SB_SLOTS).start()

        def tile(j, carry, causal, sync=True):
            dq, cg = carry
            slot = j % SB_SLOTS
            if sync:
                arrive(j)
            rows = pl.ds(pl.multiple_of(j * tk, tk), tk)
            kb = k_ref[rows, :].astype(BF16)
            vb = v_ref[rows, :].astype(BF16)
            z = lax.dot_general(qb, kb, NT, preferred_element_type=F32)
            sig = 1.0 / (1.0 + jnp.exp(-z))
            ab = abuf[slot]
            g = ab.astype(F32) * lax.dot_general(dob, vb, NT, preferred_element_type=F32)
            p = cg + lax.dot_general(g.astype(BF16), uexcl, NN, preferred_element_type=F32)
            dz = g - sig * (g + p)
            if causal is not None:
                dz = jnp.where(causal, dz, 0.0)
            dzb = dz.astype(BF16)
            dva[j] += lax.dot_general(dob_t, ab, NN, preferred_element_type=F32)
            dka[j] += lax.dot_general(qb_t, dzb, NN, preferred_element_type=F32)
            dq = dq + lax.dot_general(dzb, kb, NN, preferred_element_type=F32)
            return dq, cg + jnp.sum(g, axis=1, keepdims=True)

        for first in range(ahead):
            @pl.when(first < total)
            def _():
                fetch(first, first).start()

        init = (jnp.zeros((tq, SB_HD), F32), jnp.zeros((tq, 1), F32))
        if nd == 2:
            def pair(n, cr):
                arrive(2 * n)
                arrive(2 * n + 1)
                return tile(2 * n + 1, tile(2 * n, cr, None, False), None, False)

            carry = lax.fori_loop(0, i, pair, init)
        else:
            carry = lax.fori_loop(0, i * nd, lambda j, cr: tile(j, cr, None), init)
        for dd in range(nd):
            carry = tile(i * nd + dd, carry, c + dd * tk < r)
        dq_ref[...] = (carry[0] * scale).astype(dq_ref.dtype)

        @pl.when(i == nq - 1)
        def _():
            for jj in range(s // tk):
                dk_ref[jj * tk:(jj + 1) * tk, :] = dka[jj].T.astype(dk_ref.dtype)
                dv_ref[jj * tk:(jj + 1) * tk, :] = dva[jj].T.astype(dv_ref.dtype)

    blk = lambda off: pl.BlockSpec((s, SB_HD), functools.partial(lambda h, i, off: (0, off + h), off=off))
    tile_spec = pl.BlockSpec((tq, SB_HD), lambda h, i: (i, h))
    full = pltpu.HBM((s, heads * SB_HD), BF16)
    return pl.pallas_call(
        body, name=name, grid=(heads, nq),
        in_specs=[tile_spec, blk(heads), blk(2 * heads), tile_spec, ANY, ANY, ANY],
        out_specs=[tile_spec, blk(0), blk(0)],
        out_shape=[pltpu.HBM(dz.shape, dz.dtype), full, full],
        input_output_aliases={5: 0},
        scratch_shapes=[pltpu.VMEM((s // tk, SB_HD, tk), F32), pltpu.VMEM((s // tk, SB_HD, tk), F32),
                        pltpu.VMEM((SB_SLOTS, tq, tk), BF16), pltpu.SemaphoreType.DMA((SB_SLOTS,))],
        compiler_params=_params(12 * s * SB_HD * 4 + 32 * tq * tk * 4 + (8 << 20)),
    )(_hbm(zm), _hbm(zm), _hbm(zm), _hbm(dy), a_all, dz, after)


def _conv_taps(u, w_ref, rows_i):
    taps = []
    for j in range(CONV_W):
        sh = CONV_W - 1 - j
        if sh == 0:
            taps.append(u)
        else:
            taps.append(jnp.where(rows_i >= sh, pltpu.roll(u, sh, 0), 0.0))
    return taps


def _conv_fwd(zm, col0, width, cw, cb, *, name):
    s = zm.shape[0]
    bw = _pick(width, (LANES,))
    off = col0 // bw

    def body(u_ref, w_ref, b_ref, o_ref):
        u = u_ref[...]
        rows_i = lax.broadcasted_iota(jnp.int32, u.shape, 0)
        acc = jnp.broadcast_to(b_ref[...], u.shape)
        for j, tp in enumerate(_conv_taps(u, w_ref, rows_i)):
            acc = acc + tp * w_ref[j:j + 1, :]
        o_ref[...] = acc * _sigmoid(acc)

    return pl.pallas_call(
        body, name=name, grid=(width // bw,),
        in_specs=[pl.BlockSpec((s, bw), lambda j: (0, off + j)), pl.BlockSpec((CONV_W, bw), lambda j: (0, j)),
                  pl.BlockSpec((1, bw), lambda j: (0, j))],
        out_specs=pl.BlockSpec((s, bw), lambda j: (0, j)),
        out_shape=pltpu.HBM((s, width), F32),
        compiler_params=_params(12 * s * bw * 4 + (4 << 20)),
    )(_hbm(zm), cw, cb)


def _conv_bwd(zm, col0, width, cw, cb, dqk, dz, *, name):
    s = zm.shape[0]
    bw = _pick(width, (LANES,))
    off = col0 // bw

    def body(u_ref, w_ref, b_ref, d_ref, dz_ref, du_ref, dw_ref, db_ref):
        u = u_ref[...]
        rows_i = lax.broadcasted_iota(jnp.int32, u.shape, 0)
        taps = _conv_taps(u, w_ref, rows_i)
        acc = jnp.broadcast_to(b_ref[...], u.shape)
        for j, tp in enumerate(taps):
            acc = acc + tp * w_ref[j:j + 1, :]
        sg = _sigmoid(acc)
        dc = d_ref[...] * (sg * (1.0 + acc * (1.0 - sg)))
        du = jnp.zeros_like(u)
        for j in range(CONV_W):
            sh = CONV_W - 1 - j
            if sh == 0:
                du = du + dc * w_ref[j:j + 1, :]
            else:
                du = du + jnp.where(rows_i < s - sh, pltpu.roll(dc, s - sh, 0), 0.0) * w_ref[j:j + 1, :]
            dw_ref[j:j + 1, :] = jnp.sum(dc * taps[j], axis=0, keepdims=True)
        du_ref[...] = du.astype(du_ref.dtype)
        db_ref[...] = jnp.sum(dc, axis=0, keepdims=True)

    return pl.pallas_call(
        body, name=name, grid=(width // bw,),
        in_specs=[pl.BlockSpec((s, bw), lambda j: (0, off + j)), pl.BlockSpec((CONV_W, bw), lambda j: (0, j)),
                  pl.BlockSpec((1, bw), lambda j: (0, j)), pl.BlockSpec((s, bw), lambda j: (0, j)), ANY],
        out_specs=[pl.BlockSpec((s, bw), lambda j: (0, off + j)), pl.BlockSpec((CONV_W, bw), lambda j: (0, j)),
                   pl.BlockSpec((1, bw), lambda j: (0, j))],
        out_shape=[pltpu.HBM(dz.shape, dz.dtype), pltpu.HBM((CONV_W, width), F32),
                   pltpu.HBM((1, width), F32)],
        input_output_aliases={4: 0},
        compiler_params=_params(20 * s * bw * 4 + (4 << 20)),
    )(_hbm(zm), cw, cb, _hbm(dqk), dz)


def _ml_gates(gcol_ref, grow_ref):
    l = CHUNK
    r = lax.broadcasted_iota(jnp.int32, (l, l), 0)
    c = lax.broadcasted_iota(jnp.int32, (l, l), 1)
    gcol = gcol_ref[...]
    grow = grow_ref[0]
    bcol = _u01dot((c <= r).astype(BF16), gcol)
    brow = _dot01(grow, (r <= c).astype(BF16))
    return gcol, grow, bcol, brow, r >= c


def _ml_chunk(h, dh, mq_ref, mk_ref, v_ref, gates, cp, n_prev, m_prev):
    gcol, grow, bcol, brow, tri = gates
    l = CHUNK
    sl = slice(h * dh, (h + 1) * dh)
    qc = mq_ref[:, sl]
    kc = mk_ref[:, sl] * (dh ** -0.5)
    vc = v_ref[:, sl]
    i_row = grow[h:h + 1, :]
    i_col = gcol[:, h:h + 1]
    b_col = bcol[:, ML_HEADS + h:ML_HEADS + h + 1]
    b_row = brow[ML_HEADS + h:ML_HEADS + h + 1, :]
    b_end = b_col[l - 1:l, :]
    d = jnp.where(tri, b_col - b_row + i_row, -jnp.inf)
    m_inter = b_col + m_prev
    m_t = jnp.maximum(m_inter, jnp.max(d, axis=1, keepdims=True))
    w = jnp.exp(d - m_t)
    s_inter = jnp.exp(m_inter - m_t)
    qb, kb, vb = qc.astype(BF16), kc.astype(BF16), vc.astype(BF16)
    cpb = cp.astype(BF16)
    a = lax.dot_general(qb, kb, NT, preferred_element_type=F32)
    sc = a * w
    qcp = lax.dot_general(qb, cpb, NT, preferred_element_type=F32)
    qn = jnp.sum(qc * n_prev, axis=1, keepdims=True)
    num = lax.dot_general(sc.astype(BF16), vb, NN, preferred_element_type=F32) + s_inter * qcp
    den = jnp.sum(sc, axis=1, keepdims=True) + s_inter * qn
    floor = jnp.exp(-m_t)
    dnm = jnp.maximum(jnp.abs(den), floor)
    g_col = b_end - b_col + i_col
    g_row = b_end - b_row + i_row
    m_new = jnp.maximum(b_end + m_prev, jnp.max(g_row, axis=1, keepdims=True))
    decay = jnp.exp(b_end + m_prev - m_new)
    wk = jnp.exp(g_col - m_new)
    return dict(qc=qc, kc=kc, vc=vc, qb=qb, kb=kb, vb=vb, cpb=cpb, w=w, s_inter=s_inter, a=a, sc=sc, qcp=qcp, qn=qn,
                num=num, den=den, floor=floor, dnm=dnm, m_new=m_new, decay=decay, wk=wk, sl=sl)


def _ml_fwd(mqk, zm, vcol, gcol, grow, d_model, *, name):
    s = zm.shape[0]
    nc = s // CHUNK
    dh = d_model // ML_HEADS
    hh = ML_HEADS

    def body(mq_ref, mk_ref, v_ref, gcol_ref, grow_ref, h_ref, cs_ref, ns_ref, ms_ref, c_s, n_s, m_s):
        @pl.when(pl.program_id(0) == 0)
        def _():
            c_s[...] = jnp.zeros_like(c_s)
            n_s[...] = jnp.zeros_like(n_s)
            m_s[...] = jnp.zeros_like(m_s)

        gates = _ml_gates(gcol_ref, grow_ref)
        for h in range(hh):
            cp, n_prev, m_prev = c_s[h], n_s[h], m_s[h][:, 0:1]
            cs_ref[0, h] = cp
            ns_ref[0, h] = n_prev
            ms_ref[0, h] = m_s[h]
            f = _ml_chunk(h, dh, mq_ref, mk_ref, v_ref, gates, cp, n_prev, m_prev)
            h_ref[:, f["sl"]] = f["num"] / f["dnm"]
            c_s[h] = f["decay"] * cp + lax.dot_general((f["vc"] * f["wk"]).astype(BF16), f["kb"], TN,
                                                       preferred_element_type=F32)
            n_s[h] = f["decay"] * n_prev + jnp.sum(f["wk"] * f["kc"], axis=0, keepdims=True)
            m_s[h] = jnp.broadcast_to(f["m_new"], (1, LANES))

    dblk = d_model
    return pl.pallas_call(
        body, name=name, grid=(nc,),
        in_specs=[pl.BlockSpec((CHUNK, dblk), lambda c: (c, 0)), pl.BlockSpec((CHUNK, dblk), lambda c: (c, 1)),
                  pl.BlockSpec((CHUNK, dblk), lambda c: (c, vcol // dblk)),
                  pl.BlockSpec((CHUNK, LANES), lambda c: (c, 0)), pl.BlockSpec((1, 8, CHUNK), lambda c: (c, 0, 0))],
        out_specs=[pl.BlockSpec((CHUNK, dblk), lambda c: (c, 0)),
                   pl.BlockSpec((1, hh, dh, dh), lambda c: (c, 0, 0, 0)),
                   pl.BlockSpec((1, hh, 1, dh), lambda c: (c, 0, 0, 0)),
                   pl.BlockSpec((1, hh, 1, LANES), lambda c: (c, 0, 0, 0))],
        out_shape=[pltpu.HBM((s, d_model), F32), pltpu.HBM((nc, hh, dh, dh), F32),
                   pltpu.HBM((nc, hh, 1, dh), F32), pltpu.HBM((nc, hh, 1, LANES), F32)],
        scratch_shapes=[pltpu.VMEM((hh, dh, dh), F32), pltpu.VMEM((hh, 1, dh), F32), pltpu.VMEM((hh, 1, LANES), F32)],
        compiler_params=_params(8 * hh * dh * dh * 4 + (16 << 20)),
    )(_hbm(mqk), _hbm(mqk), _hbm(zm), _hbm(gcol), _hbm(grow))


def _ml_bwd(mqk, zm, vcol, gcol, grow, cs, ns, ms, dhm, dz, d_model, *, name):
    s = zm.shape[0]
    nc = s // CHUNK
    dh = d_model // ML_HEADS
    hh = ML_HEADS
    l = CHUNK

    def body(mq_ref, mk_ref, v_ref, gcol_ref, grow_ref, cs_ref, ns_ref, ms_ref, dh_ref, dz_ref,
             dqk_ref, dv_ref, dgc_ref, dgr_ref, dc_s, dn_s):
        @pl.when(pl.program_id(0) == 0)
        def _():
            dc_s[...] = jnp.zeros_like(dc_s)
            dn_s[...] = jnp.zeros_like(dn_s)

        gates = _ml_gates(gcol_ref, grow_ref)
        lane = lax.broadcasted_iota(jnp.int32, (l, LANES), 1)
        rowi = lax.broadcasted_iota(jnp.int32, (8, l), 0)
        lastrow = lax.broadcasted_iota(jnp.int32, (l, 1), 0) == l - 1
        dgc = jnp.zeros((l, LANES), F32)
        dgr = jnp.zeros((8, l), F32)
        for h in range(hh):
            cp, n_prev, m_prev = cs_ref[0, h], ns_ref[0, h], ms_ref[0, h][:, 0:1]
            f = _ml_chunk(h, dh, mq_ref, mk_ref, v_ref, gates, cp, n_prev, m_prev)
            dC, dn = dc_s[h], dn_s[h]
            dhv = dh_ref[:, f["sl"]]
            dnum = dhv / f["dnm"]
            hv = f["num"] / f["dnm"]
            ddnm = -jnp.sum(dhv * hv, axis=1, keepdims=True) / f["dnm"]
            dden = jnp.where(jnp.abs(f["den"]) >= f["floor"], ddnm * jnp.sign(f["den"]), 0.0)
            dnb = dnum.astype(BF16)
            dsc = lax.dot_general(dnb, f["vb"], NT, preferred_element_type=F32) + dden
            dvc = lax.dot_general(f["sc"].astype(BF16), dnb, TN, preferred_element_type=F32)
            ds_inter = jnp.sum(dnum * f["qcp"], axis=1, keepdims=True) + dden * f["qn"]
            sdn = (f["s_inter"] * dnum).astype(BF16)
            sdd = f["s_inter"] * dden
            da = dsc * f["w"]
            dab = da.astype(BF16)
            dqc = (lax.dot_general(dab, f["kb"], NN, preferred_element_type=F32)
                   + lax.dot_general(sdn, f["cpb"], NN, preferred_element_type=F32) + sdd * n_prev)
            dcp = f["decay"] * dC + lax.dot_general(sdn, f["qb"], TN, preferred_element_type=F32)
            dnp = f["decay"] * dn + jnp.sum(sdd * f["qc"], axis=0, keepdims=True)
            vw = (f["vc"] * f["wk"]).astype(BF16)
            dCb = dC.astype(BF16)
            dkc = (lax.dot_general(dab, f["qb"], TN, preferred_element_type=F32)
                   + lax.dot_general(vw, dCb, NN, preferred_element_type=F32) + f["wk"] * dn)
            e = lax.dot_general(f["kb"], dCb, NT, preferred_element_type=F32)
            dvc = dvc + e * f["wk"]
            dwk = jnp.sum(e * f["vc"], axis=1, keepdims=True) + jnp.sum(f["kc"] * dn, axis=1, keepdims=True)
            ddecay = jnp.sum(jnp.sum(dC * cp, axis=1, keepdims=True), axis=0, keepdims=True) \
                + jnp.sum(dn * n_prev, axis=1, keepdims=True)
            dd = dsc * f["sc"]
            dlw = dwk * f["wk"]
            db_end = jnp.sum(dlw, axis=0, keepdims=True) + ddecay * f["decay"]
            di_col = dlw
            db_col = jnp.sum(dd, axis=1, keepdims=True) + ds_inter * f["s_inter"] - dlw \
                + jnp.where(lastrow, db_end, 0.0)
            cs_dd = jnp.sum(dd, axis=0, keepdims=True)
            dgc = dgc + jnp.where(lane == h, di_col, 0.0) + jnp.where(lane == hh + h, db_col, 0.0)
            dgr = dgr + jnp.where(rowi == h, cs_dd, 0.0) - jnp.where(rowi == hh + h, cs_dd, 0.0)
            dqk_ref[:, f["sl"]] = dqc
            dqk_ref[:, d_model + h * dh:d_model + (h + 1) * dh] = dkc * (dh ** -0.5)
            dv_ref[:, f["sl"]] = dvc.astype(dv_ref.dtype)
            dc_s[h] = dcp
            dn_s[h] = dnp
        dgc_ref[...] = dgc
        dgr_ref[0] = dgr

    dblk = d_model
    rev = lambda c: nc - 1 - c
    return pl.pallas_call(
        body, name=name, grid=(nc,),
        in_specs=[pl.BlockSpec((l, dblk), lambda c: (rev(c), 0)), pl.BlockSpec((l, dblk), lambda c: (rev(c), 1)),
                  pl.BlockSpec((l, dblk), lambda c: (rev(c), vcol // dblk)),
                  pl.BlockSpec((l, LANES), lambda c: (rev(c), 0)), pl.BlockSpec((1, 8, l), lambda c: (rev(c), 0, 0)),
                  pl.BlockSpec((1, hh, dh, dh), lambda c: (rev(c), 0, 0, 0)),
                  pl.BlockSpec((1, hh, 1, dh), lambda c: (rev(c), 0, 0, 0)),
                  pl.BlockSpec((1, hh, 1, LANES), lambda c: (rev(c), 0, 0, 0)),
                  pl.BlockSpec((l, dblk), lambda c: (rev(c), 0)), ANY],
        out_specs=[pl.BlockSpec((l, 2 * dblk), lambda c: (rev(c), 0)),
                   pl.BlockSpec((l, dblk), lambda c: (rev(c), vcol // dblk)),
                   pl.BlockSpec((l, LANES), lambda c: (rev(c), 0)),
                   pl.BlockSpec((1, 8, l), lambda c: (rev(c), 0, 0))],
        out_shape=[pltpu.HBM((s, 2 * d_model), F32),
                   pltpu.HBM(dz.shape, dz.dtype), pltpu.HBM((s, LANES), F32),
                   pltpu.HBM((nc, 8, l), F32)],
        input_output_aliases={9: 1},
        scratch_shapes=[pltpu.VMEM((hh, dh, dh), F32), pltpu.VMEM((hh, 1, dh), F32)],
        compiler_params=_params(10 * hh * dh * dh * 4 + (16 << 20)),
    )(*[_hbm(a) for a in (mqk, mqk, zm, gcol, grow, cs, ns, ms, dhm)], dz)


def _xa_fwd(zm, qcol, kv, gq, gk, d_model, *, name, tq=1024):
    s = zm.shape[0]
    nm = kv.shape[0]
    dh = d_model // X_HEADS
    tq = _pick(s, (tq, 128, 64))
    scale = dh ** -0.5

    def body(q_ref, k_ref, v_ref, gq_ref, gk_ref, o_ref):
        qn = _rms_fwd(q_ref[...], gq_ref[...])
        kn = _rms_fwd(k_ref[...], gk_ref[...])
        lg = _dot(qn, kn, NT) * scale
        lg = lg - jnp.max(lg, axis=1, keepdims=True)
        p = jnp.exp(lg)
        p = p / jnp.sum(p, axis=1, keepdims=True)
        o_ref[...] = _dot(p, v_ref[...], NN).astype(o_ref.dtype)

    return pl.pallas_call(
        body, name=name, grid=(X_HEADS, s // tq),
        in_specs=[pl.BlockSpec((tq, dh), lambda h, i: (i, qcol // dh + h)), pl.BlockSpec((nm, dh), lambda h, i: (0, h)),
                  pl.BlockSpec((nm, dh), lambda h, i: (0, X_HEADS + h)),
                  pl.BlockSpec((1, dh), lambda h, i: (0, 0)), pl.BlockSpec((1, dh), lambda h, i: (0, 0))],
        out_specs=pl.BlockSpec((tq, dh), lambda h, i: (i, h)),
        out_shape=pltpu.HBM((s, d_model), BF16),
        compiler_params=_params(32 << 20),
    )(_hbm(zm), _hbm(kv), _hbm(kv), gq, gk)


def _xa_bwd(zm, qcol, kv, gq, gk, dy, dz, d_model, *, name, tq=1024):
    s = zm.shape[0]
    nm = kv.shape[0]
    dh = d_model // X_HEADS
    tq = _pick(s, (tq, 128, 64))
    nq = s // tq
    scale = dh ** -0.5

    def body(q_ref, k_ref, v_ref, gq_ref, gk_ref, do_ref, dz_ref, dq_ref, dkn_ref, dv_ref, dgq_ref):
        h, i = pl.program_id(0), pl.program_id(1)

        @pl.when(i == 0)
        def _():
            dkn_ref[...] = jnp.zeros_like(dkn_ref)
            dv_ref[...] = jnp.zeros_like(dv_ref)

        @pl.when((i == 0) & (h == 0))
        def _():
            dgq_ref[...] = jnp.zeros_like(dgq_ref)

        q = q_ref[...]
        qn = _rms_fwd(q, gq_ref[...])
        kn = _rms_fwd(k_ref[...], gk_ref[...])
        lg = _dot(qn, kn, NT) * scale
        lg = lg - jnp.max(lg, axis=1, keepdims=True)
        p = jnp.exp(lg)
        p = p / jnp.sum(p, axis=1, keepdims=True)
        do = do_ref[...]
        dv_ref[...] += _dot(p, do, TN)
        dp = _dot(do, v_ref[...], NT)
        dlg = p * (dp - jnp.sum(dp * p, axis=1, keepdims=True)) * scale
        dqn = _dot(dlg, kn, NN)
        dkn_ref[...] += _dot(dlg, qn, TN)
        dq, dgq = _rms_bwd(q, gq_ref[...], dqn)
        dq_ref[...] = dq.astype(dq_ref.dtype)
        dgq_ref[...] += jnp.sum(dgq, axis=0, keepdims=True)

    return pl.pallas_call(
        body, name=name, grid=(X_HEADS, nq),
        in_specs=[pl.BlockSpec((tq, dh), lambda h, i: (i, qcol // dh + h)), pl.BlockSpec((nm, dh), lambda h, i: (0, h)),
                  pl.BlockSpec((nm, dh), lambda h, i: (0, X_HEADS + h)),
                  pl.BlockSpec((1, dh), lambda h, i: (0, 0)), pl.BlockSpec((1, dh), lambda h, i: (0, 0)),
                  pl.BlockSpec((tq, dh), lambda h, i: (i, h)), ANY],
        out_specs=[pl.BlockSpec((tq, dh), lambda h, i: (i, qcol // dh + h)),
                   pl.BlockSpec((nm, dh), lambda h, i: (0, h)),
                   pl.BlockSpec((nm, dh), lambda h, i: (0, h)), pl.BlockSpec((1, dh), lambda h, i: (0, 0))],
        out_shape=[pltpu.HBM(dz.shape, dz.dtype), pltpu.HBM((nm, d_model), F32),
                   pltpu.HBM((nm, d_model), F32), pltpu.HBM((1, dh), F32)],
        input_output_aliases={6: 0},
        compiler_params=_params(32 << 20),
    )(_hbm(zm), _hbm(kv), _hbm(kv), gq, gk, _hbm(dy), dz)


def _place():
    return lax.axis_index("x"), lax.axis_index("y"), lax.axis_index("c")


ANY = pl.BlockSpec(memory_space=pl.ANY)


def _allgather_two_level(big, small, *, name, chunk_rows=64):
    r, cc = big.shape
    half = r // 2
    nr = _pick(half, (chunk_rows, 32, 16))
    nq = half // nr

    def body(big_ref, small_ref, obig, osmall, land, passed, send, recv, fsend, frecv, out_a, out_b, ssend, srecv, loc):
        x, y, c = _place()
        k = 2 * x + y
        chips = [(1 - x, y), (x, 1 - y), (1 - x, 1 - y)]
        slots = [2 * px + py for px, py in chips]
        local = [pltpu.make_async_copy(big_ref, obig.at[k], loc.at[0]),
                 pltpu.make_async_copy(small_ref, osmall.at[k], loc.at[1])]
        for cp in local:
            cp.start()

        def rows(h, q):
            return pl.ds(pl.multiple_of(h * half + q * nr, nr), nr)

        def chunk(q):
            return pl.ds(q * nr, nr)

        def over_ici(j, q):
            return pltpu.make_async_remote_copy(
                src_ref=big_ref.at[rows(c, q)], dst_ref=land.at[j, chunk(q)], send_sem=send.at[nq * j + q],
                recv_sem=recv.at[nq * j + q], device_id=(chips[j][0], chips[j][1], c), device_id_type=MESH)

        def to_sibling(j, q):
            return pltpu.make_async_remote_copy(
                src_ref=land.at[j, chunk(q)], dst_ref=passed.at[j, chunk(q)], send_sem=fsend.at[nq * j + q],
                recv_sem=frecv.at[nq * j + q], device_id=(x, y, 1 - c), device_id_type=MESH)

        def small_copy(j, slot):
            return pltpu.make_async_remote_copy(
                src_ref=small_ref, dst_ref=osmall.at[slot], send_sem=ssend.at[j], recv_sem=srecv.at[j],
                device_id=(chips[j][0], chips[j][1], c), device_id_type=MESH)

        for q in range(nq):
            for j in range(3):
                over_ici(j, q).start()
        for j in range(3):
            small_copy(j, k).start()
        for q in range(nq):
            for j in range(3):
                over_ici(j, q).wait_recv()
                to_sibling(j, q).start()
                cp = pltpu.make_async_copy(land.at[j, chunk(q)], obig.at[slots[j], rows(c, q)], out_a.at[nq * j + q])
                cp.start()
                local.append(cp)
        for q in range(nq):
            for j in range(3):
                to_sibling(j, q).wait_recv()
                cp = pltpu.make_async_copy(passed.at[j, chunk(q)], obig.at[slots[j], rows(1 - c, q)],
                                           out_b.at[nq * j + q])
                cp.start()
                local.append(cp)
        for j in range(3):
            small_copy(j, slots[j]).wait_recv()
            small_copy(j, k).wait_send()
        for q in range(nq):
            for j in range(3):
                over_ici(j, q).wait_send()
                to_sibling(j, q).wait_send()
        for cp in local:
            cp.wait()

    stage = 2 * _nbytes((3, half, cc), big.dtype)
    return pl.pallas_call(
        body, name=name, in_specs=[ANY] * 2, out_specs=[ANY] * 2,
        out_shape=[pltpu.HBM((4,) + big.shape, big.dtype), pltpu.HBM((4,) + small.shape, small.dtype)],
        scratch_shapes=[pltpu.VMEM((3, half, cc), big.dtype), pltpu.VMEM((3, half, cc), big.dtype)]
        + [pltpu.SemaphoreType.DMA((3 * nq,))] * 6
        + [pltpu.SemaphoreType.DMA((3,)), pltpu.SemaphoreType.DMA((3,)), pltpu.SemaphoreType.DMA((2,))],
        compiler_params=_params(stage + stage // 8 + (4 << 20)),
    )(big, small)


HBM_SPEC = pl.BlockSpec(memory_space=pltpu.HBM)
SEM_SPEC = pl.BlockSpec(memory_space=pltpu.SEMAPHORE)
EFFECT = pltpu.SideEffectType.DATAFLOW_SIDE_EFFECTING


def _split_copies(kind, srcs, lands, send, recv):
    x, y, c = _place()
    if kind == "quarters":
        peers = [(1 - x, y, c), (x, 1 - y, c), (1 - x, 1 - y, c)]
    else:
        peers = [(x ^ ((j >> 2) & 1), y ^ ((j >> 1) & 1), c ^ (j & 1)) for j in range(1, 8)]
    npeer = len(peers)
    out = []
    for t in range(len(srcs)):
        for j, (px, py, pc) in enumerate(peers):
            if kind == "quarters":
                src, mine, theirs = srcs[t], 2 * x + y, 2 * px + py
            else:
                src, mine, theirs = srcs[t].at[2 * px + py, pc], 4 * x + 2 * y + c, 4 * px + 2 * py + pc
            mk = functools.partial(
                pltpu.make_async_remote_copy, src_ref=src, send_sem=send.at[npeer * t + j],
                recv_sem=recv.at[npeer * t + j], device_id=(px, py, pc), device_id_type=MESH)
            out.append((functools.partial(mk, dst_ref=lands[t].at[mine]),
                        functools.partial(mk, dst_ref=lands[t].at[theirs])))
    return out


def _split_start(kind, srcs, land_shapes, after, *, name):
    n = len(srcs)
    ncopies = n * (3 if kind == "quarters" else 7)

    def body(*refs):
        ins, lands = refs[:n], refs[n:2 * n]
        send, recv = refs[2 * n + 1], refs[2 * n + 2]
        token = refs[-1]
        for start, _ in _split_copies(kind, ins, lands, send, recv):
            start().start()
        token[...] = jnp.zeros_like(token)

    lands = [_hbm(lax.empty(shp, a.dtype)) for shp, a in zip(land_shapes, srcs)]
    res = pl.pallas_call(
        body, name=name, in_specs=[HBM_SPEC] * (2 * n) + [ANY],
        out_specs=[SEM_SPEC, SEM_SPEC] + [HBM_SPEC] * (2 * n) + [pl.BlockSpec(memory_space=pltpu.VMEM)],
        out_shape=[pltpu.SemaphoreType.DMA((ncopies,)), pltpu.SemaphoreType.DMA((ncopies,))]
        + [pltpu.HBM(a.shape, a.dtype) for a in srcs] + [pltpu.HBM(shp, a.dtype) for shp, a in zip(land_shapes, srcs)]
        + [jax.ShapeDtypeStruct((8, LANES), F32)],
        input_output_aliases={i: 2 + i for i in range(2 * n)},
        compiler_params=pltpu.CompilerParams(has_side_effects=EFFECT),
    )(*[_hbm(a) for a in srcs], *lands, after)
    return res[0], res[1], list(res[2:2 + n]), list(res[2 + n:2 + 2 * n]), res[-1]


def _split_wait(kind, send, recv, srcs, lands, after, *, name):
    n = len(srcs)

    def body(*refs):
        ins, lnd = refs[:n], refs[n:2 * n]
        snd, rcv = refs[2 * n], refs[2 * n + 1]
        for start, arrive in _split_copies(kind, ins, lnd, snd, rcv):
            start().wait_send()
            arrive().wait_recv()

    res = pl.pallas_call(
        body, name=name, in_specs=[HBM_SPEC] * (2 * n) + [SEM_SPEC, SEM_SPEC] + [ANY] * len(after),
        out_specs=[HBM_SPEC] * (2 * n),
        out_shape=[pltpu.HBM(a.shape, a.dtype) for a in srcs] + [pltpu.HBM(a.shape, a.dtype) for a in lands],
        input_output_aliases={i: i for i in range(2 * n)},
        compiler_params=pltpu.CompilerParams(has_side_effects=EFFECT),
    )(*srcs, *lands, send, recv, *after)
    return list(res[n:])


def _sum8(parts, *, name):
    _, r, c = parts.shape
    t = _pick(r, (128, 64, 32, 16, 8))

    def body(p_ref, o_ref):
        acc = p_ref[0].astype(F32)
        for k in range(1, 8):
            acc = acc + p_ref[k].astype(F32)
        o_ref[...] = acc

    return pl.pallas_call(
        body, name=name, grid=(r // t,), in_specs=[pl.BlockSpec((8, t, c), lambda i: (0, i, 0))],
        out_specs=pl.BlockSpec((t, c), lambda i: (i, 0)), out_shape=pltpu.HBM((r, c), F32),
        compiler_params=_params(2 * 8 * t * c * 2 + 6 * t * c * 4 + (4 << 20)),
    )(_hbm(parts))


def _swap_halves(halves, *, name, chunk_bytes=512 * 1024):
    n = len(halves)
    items = []
    for t, a in enumerate(halves):
        r = a.shape[0]
        k = 1
        while _nbytes(a.shape, a.dtype) // k > chunk_bytes and r % (2 * k) == 0 and (r // (2 * k)) % 8 == 0:
            k *= 2
        items += [(t, q * (r // k), r // k) for q in range(k)]
    m = len(items)

    def body(*refs):
        ins, outs = refs[:n], refs[n:2 * n]
        sbuf, rbuf = refs[2 * n:3 * n], refs[3 * n:4 * n]
        send, recv, loc_own, loc_in, loc_out = refs[4 * n:]
        x, y, c = _place()
        local, stage = [], []
        for t in range(n):
            cp = pltpu.make_async_copy(ins[t], outs[t].at[c], loc_own.at[t])
            cp.start()
            local.append(cp)
        for q, (t, r0, nr) in enumerate(items):
            cp = pltpu.make_async_copy(ins[t].at[pl.ds(r0, nr)], sbuf[t].at[pl.ds(r0, nr)], loc_in.at[q])
            cp.start()
            stage.append(cp)

        def copy(q):
            t, r0, nr = items[q]
            return pltpu.make_async_remote_copy(
                src_ref=sbuf[t].at[pl.ds(r0, nr)], dst_ref=rbuf[t].at[pl.ds(r0, nr)], send_sem=send.at[q],
                recv_sem=recv.at[q], device_id=(x, y, 1 - c), device_id_type=MESH)

        for q in range(m):
            stage[q].wait()
            copy(q).start()
        for q, (t, r0, nr) in enumerate(items):
            copy(q).wait_recv()
            cp = pltpu.make_async_copy(rbuf[t].at[pl.ds(r0, nr)], outs[t].at[1 - c, pl.ds(r0, nr)], loc_out.at[q])
            cp.start()
            local.append(cp)
        for q in range(m):
            copy(q).wait_send()
        for cp in local:
            cp.wait()

    stage_bytes = 2 * sum(_nbytes(a.shape, a.dtype) for a in halves)
    return pl.pallas_call(
        body, name=name, in_specs=[ANY] * n, out_specs=[ANY] * n,
        out_shape=[pltpu.HBM((2,) + a.shape, a.dtype) for a in halves],
        scratch_shapes=[pltpu.VMEM(a.shape, a.dtype) for a in halves] * 2
        + [pltpu.SemaphoreType.DMA((m,)), pltpu.SemaphoreType.DMA((m,)), pltpu.SemaphoreType.DMA((n,)),
           pltpu.SemaphoreType.DMA((m,)), pltpu.SemaphoreType.DMA((m,))],
        compiler_params=_params(stage_bytes + (4 << 20)),
    )(*halves)


def _allreduce_small(p, after, *, name):
    r = p.shape[0]

    def body(p_ref, after_ref, o_ref, buf, send, recv):
        x, y, c = _place()
        me = 4 * x + 2 * y + c
        peers = [(x ^ ((j >> 2) & 1), y ^ ((j >> 1) & 1), c ^ (j & 1)) for j in range(1, 8)]

        def copy(j, slot):
            return pltpu.make_async_remote_copy(
                src_ref=p_ref, dst_ref=buf.at[slot], send_sem=send.at[j], recv_sem=recv.at[j],
                device_id=peers[j], device_id_type=MESH)

        for j in range(7):
            copy(j, me).start()
        buf[me] = p_ref[...]
        for j in range(7):
            px, py, pc = peers[j]
            copy(j, 4 * px + 2 * py + pc).wait_recv()
        for j in range(7):
            copy(j, me).wait_send()
        acc = buf[0]
        for k in range(1, 8):
            acc = acc + buf[k]
        o_ref[...] = acc

    vspec = pl.BlockSpec(memory_space=pltpu.VMEM)
    return pl.pallas_call(
        body, name=name, in_specs=[vspec, ANY], out_specs=vspec, out_shape=jax.ShapeDtypeStruct((r, LANES), F32),
        scratch_shapes=[pltpu.VMEM((8, r, LANES), F32), pltpu.SemaphoreType.DMA((7,)), pltpu.SemaphoreType.DMA((7,))],
    )(p, after)


def _adamw_fn(w, g, m, v):
    m = ADAM_B1 * m + (1.0 - ADAM_B1) * g
    v = ADAM_B2 * v + (1.0 - ADAM_B2) * (g * g)
    m_hat = m / (1.0 - ADAM_B1 ** ADAM_STEP)
    v_hat = v / (1.0 - ADAM_B2 ** ADAM_STEP)
    delta = -ADAM_LR * (m_hat / (jnp.sqrt(v_hat) + ADAM_EPS) + ADAM_WD * w)
    return delta, m, v


def _adamw(w, g, m, v, *, name):
    c = w.shape[1]
    return _rowwise(_adamw_fn, [w, g, m, v], [], [(c, F32)] * 3, name=name, tr=128)


def _pack(vecs, rows):
    flat = jnp.concatenate([a.reshape(-1).astype(F32) for a in vecs])
    return jnp.pad(flat, (0, rows * LANES - flat.shape[0])).reshape(rows, LANES)


def _unpack(p, like):
    flat, out, o = p.reshape(-1), [], 0
    for a in like:
        out.append(flat[o:o + a.size].reshape(a.shape))
        o += a.size
    return out


def kernel(x, mem, g_mix, w_in, b_if, b_gate, conv_w, conv_b, ml_norm_g, g_mem, w_mem_kv, q_norm_g, k_norm_g, w_sb_proj, w_ml_proj, w_x_proj, w_out, g_mlp, w_ff1, w_ff2, loss_target, m_g_mix, m_w_in, m_b_if, m_b_gate, m_conv_w, m_conv_b, m_ml_norm_g, m_g_mem, m_w_mem_kv, m_q_norm_g, m_k_norm_g, m_w_sb_proj, m_w_ml_proj, m_w_x_proj, m_w_out, m_g_mlp, m_w_ff1, m_w_ff2, v_g_mix, v_w_in, v_b_if, v_b_gate, v_conv_w, v_conv_b, v_ml_norm_g, v_g_mem, v_w_mem_kv, v_q_norm_g, v_k_norm_g, v_w_sb_proj, v_w_ml_proj, v_w_x_proj, v_w_out, v_g_mlp, v_w_ff1, v_w_ff2):
    _, s, d = x.shape
    nm = mem.shape[1]
    n_in = 4 * w_in.shape[2]
    dff = 4 * w_ff1.shape[2]
    sbh = d // SB_HD
    hh = ML_HEADS
    dh = d // hh
    nc = s // CHUNK
    assert n_in == 11 * d + 2 * hh and d % (2 * LANES) == 0 and s % LANES == 0
    x2, mem2, tgt = x[0], mem[0], loss_target[0]

    k4 = 2 * lax.axis_index("x") + lax.axis_index("y")
    me = 2 * k4 + lax.axis_index("c")
    g_first = _allgather_two_level(w_in[0].astype(BF16), conv_w[0], name="gather_w_in")
    later = [a[0].astype(BF16) for a in (w_mem_kv, w_sb_proj, w_ml_proj, w_x_proj, w_out, w_ff1, w_ff2)]
    gw_send, gw_recv, gw_src, gw_land, gw_token = _split_start(
        "quarters", later, [(4,) + a.shape for a in later], g_first[0], name="gather_rest_start")
    cols = lambda a: a.transpose(1, 0, 2).reshape(a.shape[1], 4 * a.shape[2])
    rws = lambda a: a.reshape(4 * a.shape[1], a.shape[2])
    qn = n_in // 4
    if_lo, if_hi = 7 * d, 7 * d + 2 * hh

    def cut(lo, hi):
        ks = [(k, max(lo, k * qn), min(hi, (k + 1) * qn)) for k in range(4)]
        return [g_first[0][k, :, a - k * qn:b - k * qn] for k, a, b in ks if a < b]

    w_main = jnp.concatenate(cut(0, if_lo) + cut(if_hi, n_in), axis=1)
    w_if = jnp.pad(jnp.concatenate(cut(if_lo, if_hi), axis=1), ((0, 0), (0, LANES - 2 * hh)))
    conv_wf = cols(g_first[1])
    b_if_p = jnp.pad(b_if, ((0, 0), (0, LANES - 2 * hh)))

    hn = _norm_stream(x2, g_mix, name="norm_in")
    zm = _mm(hn, w_main, after=gw_token, name="proj_in")
    zif = _mm(hn, w_if, name="proj_if")
    y_sb, a_sb = _sb_fwd(zm, sbh, name="sb_fwd")

    def gate_fn(z, b):
        pre = z + b
        lane = lax.broadcasted_iota(jnp.int32, pre.shape, 1)
        return jnp.where(lane < hh, pre, -_softplus(-pre))

    (gcol,) = _rowwise(gate_fn, [zif], [b_if_p], [(LANES, F32)], name="ml_gates", tr=1024)
    grow = gcol[:, :8].T.reshape(8, nc, CHUNK).transpose(1, 0, 2)
    mqk = _conv_fwd(zm, 3 * d, 2 * d, conv_wf, conv_b, name="conv_fwd")
    hm, cst, nst, mst = _ml_fwd(mqk, zm, 5 * d, gcol, grow, d, name="ml_fwd")

    def mlout_fn(hv, o, g):
        ys = [_rms_fwd(hv[:, k * dh:(k + 1) * dh], g[:, k * dh:(k + 1) * dh]) for k in range(hh)]
        return jnp.concatenate(ys, axis=1) * _sigmoid(o)

    (y_ml,) = _rowwise(mlout_fn, [hm, (zm, d, 6)], [ml_norm_g], [(d, BF16)], name="ml_out", tr=512)
    gw_land = _split_wait("quarters", gw_send, gw_recv, gw_src, gw_land, [y_ml, y_sb], name="gather_rest_wait")
    gw = [lax.dynamic_update_index_in_dim(ld, a, k4, 0) for ld, a in zip(gw_land, later)]
    w_kv, w_sbp, w_mlp, w_xp, w_o, w_f1, w_f2 = (cols(gw[0]), rws(gw[1]), rws(gw[2]), rws(gw[3]), rws(gw[4]),
                                                 cols(gw[5]), rws(gw[6]))
    (memn,) = _rowwise(_rms_fwd, [mem2], [g_mem], [(d, BF16)], name="norm_mem")
    kv = _mm(memn, w_kv, name="proj_kv")
    y_x = _xa_fwd(zm, 7 * d, kv, q_norm_g, k_norm_g, d, name="xa_fwd")
    p_sb = _mm(y_sb, w_sbp, name="proj_sb")
    p_ml = _mm(y_ml, w_mlp, name="proj_ml")
    p_x = _mm(y_x, w_xp, name="proj_x")

    def merge_fn(a, b, c, g0, g1, g2, bg):
        return (_sigmoid(g0 + bg[:, :d]) * a + _sigmoid(g1 + bg[:, d:2 * d]) * b + _sigmoid(g2 + bg[:, 2 * d:]) * c)

    gate_cols = [(zm, d, 8), (zm, d, 9), (zm, d, 10)]
    (mixed,) = _rowwise(merge_fn, [p_sb, p_ml, p_x] + gate_cols, [b_gate], [(d, BF16)], name="merge", tr=512)
    x1 = _mm(mixed, w_o, tiles=[x2], name="proj_out")
    h2 = _norm_stream(x1, g_mlp, name="norm_mlp")
    u, act = _mm(h2, w_f1, post=lambda r: (r, jnp.square(jnp.maximum(r, 0.0))), out_dtype=(F32, BF16), name="ff1")
    dy = _mm(act, w_f2, tiles=[x1, tgt], post=lambda r, xv, tv: (r + xv - tv) * (1.0 / d), name="ff2")
    (loss_cols,) = _rowwise(lambda g: (jnp.sum(g * g, axis=0, keepdims=True) * (0.5 * d),), [dy], [], [], [d],
                            name="loss", tr=1024)

    du = _mm(dy, w_f2, tb=True, tiles=[u], post=lambda r, uv: r * 2.0 * jnp.maximum(uv, 0.0), out_dtype=BF16,
             name="ff2_dx")
    dw_f2 = _mm(act, dy, ta=True, name="ff2_dw")
    dw_f1 = _mm(h2, du, ta=True, name="ff1_dw")
    dh2 = _mm(du, w_f1, tb=True, name="ff1_dx")

    def norm_bwd_fn(xv, dyv, res, g):
        dx, dg = _rms_bwd(xv, g, dyv)
        return dx + res, jnp.sum(dg, axis=0, keepdims=True)

    dx1, dg_mlp = _rowwise(norm_bwd_fn, [x1, dh2, dy], [g_mlp], [(d, F32)], [d], name="norm_mlp_bwd", tr=512)
    dmixed = _mm(dx1, w_o, tb=True, name="proj_out_dx")
    dw_o = _mm(mixed, dx1, ta=True, name="proj_out_dw")

    def merge_bwd_fn(dm, a, b, c, g0, g1, g2, bg):
        outs, dgs = [], []
        for p, g, k in ((a, g0, 0), (b, g1, 1), (c, g2, 2)):
            sg = _sigmoid(g + bg[:, k * d:(k + 1) * d])
            outs.append(dm * sg)
            dgs.append(dm * p * sg * (1.0 - sg))
        dgate = jnp.concatenate(dgs, axis=1)
        return (*outs, dgate, jnp.sum(dgate, axis=0, keepdims=True))

    dp_sb, dp_ml, dp_x, dgate, db_gate = _rowwise(
        merge_bwd_fn, [dmixed, p_sb, p_ml, p_x] + gate_cols, [b_gate], [(d, BF16)] * 3 + [(3 * d, BF16)], [3 * d],
        name="merge_bwd", tr=256)
    dw_sbp = _mm(y_sb, dp_sb, ta=True, name="proj_sb_dw")
    dw_mlp = _mm(y_ml, dp_ml, ta=True, name="proj_ml_dw")
    dw_xp = _mm(y_x, dp_x, ta=True, name="proj_x_dw")
    dy_sb = _mm(dp_sb, w_sbp, tb=True, out_dtype=BF16, name="proj_sb_dx")
    dy_ml = _mm(dp_ml, w_mlp, tb=True, name="proj_ml_dx")
    dy_x = _mm(dp_x, w_xp, tb=True, out_dtype=BF16, name="proj_x_dx")

    dzm = _hbm(lax.empty((s, 11 * d), BF16))
    dzm, dkn, dxv, dg_qn = _xa_bwd(zm, 7 * d, kv, q_norm_g, k_norm_g, dy_x, dzm, d, name="xa_bwd")

    def knorm_bwd_fn(kvv, dknv, dvv, g):
        dks, dgs = [], []
        for k in range(X_HEADS):
            sl = slice(k * dh, (k + 1) * dh)
            dk, dg = _rms_bwd(kvv[:, sl], g, dknv[:, sl])
            dks.append(dk)
            dgs.append(jnp.sum(dg, axis=0, keepdims=True))
        return jnp.concatenate(dks + [dvv], axis=1), dgs[0] + dgs[1] + dgs[2] + dgs[3]

    dkv, dg_kn = _rowwise(knorm_bwd_fn, [(kv, d, 0), dkn, dxv], [k_norm_g], [(2 * d, BF16)], [dh], name="xa_knorm_bwd")
    dw_kv = _mm(memn, dkv, ta=True, name="proj_kv_dw")
    dmemn = _mm(dkv, w_kv, tb=True, name="proj_kv_dx")

    def gmem_fn(mv, dv_, g):
        _, dg = _rms_bwd(mv, g, dv_)
        return (jnp.sum(dg, axis=0, keepdims=True),)

    (dg_mem,) = _rowwise(gmem_fn, [mem2, dmemn], [g_mem], [], [d], name="norm_mem_bwd")

    uncols = lambda a: a.reshape(a.shape[0], 4, a.shape[1] // 4).transpose(1, 0, 2)
    unrws = lambda a: a.reshape(4, a.shape[0] // 4, a.shape[1])
    to_parts = lambda q: q.astype(BF16).reshape(4, 2, q.shape[1] // 2, q.shape[2])
    early = [to_parts(q) for q in (uncols(dw_kv), unrws(dw_sbp), unrws(dw_mlp), unrws(dw_xp), unrws(dw_o),
                                   uncols(dw_f1), unrws(dw_f2))]
    ge_send, ge_recv, ge_src, ge_land, ge_token = _split_start(
        "grads", early, [(8,) + a.shape[2:] for a in early], dg_mem, name="exchange_early_start")

    def mlout_bwd_fn(dyv, hv, o, g):
        sg = _sigmoid(o)
        dn = dyv * sg
        dxs, dgs, ys = [], [], []
        for k in range(hh):
            sl = slice(k * dh, (k + 1) * dh)
            ys.append(_rms_fwd(hv[:, sl], g[:, sl]))
            dxk, dgk = _rms_bwd(hv[:, sl], g[:, sl], dn[:, sl])
            dxs.append(dxk)
            dgs.append(dgk)
        do = dyv * jnp.concatenate(ys, axis=1) * sg * (1.0 - sg)
        return jnp.concatenate(dxs, axis=1), do, jnp.sum(jnp.concatenate(dgs, axis=1), axis=0, keepdims=True)

    dhm, dzm, dg_mln = _rowwise(mlout_bwd_fn, [dy_ml, hm, (zm, d, 6)], [ml_norm_g], [(d, F32), (d, BF16)], [d],
                                name="ml_out_bwd", tr=512, into=(dzm, 1, 6))
    dmqk, dzm, dgc, dgr = _ml_bwd(mqk, zm, 5 * d, gcol, grow, cst, nst, mst, dhm, dzm, d, name="ml_bwd")
    dzm, dconv_w, dconv_b = _conv_bwd(zm, 3 * d, 2 * d, conv_wf, conv_b, dmqk, dzm, name="conv_bwd")
    dzm, dsk, dsv = _sb_bwd(zm, dy_sb, a_sb, dzm, ge_token, sbh, name="sb_bwd")
    dgr_t = jnp.pad(dgr.transpose(1, 0, 2).reshape(8, s).T, ((0, 0), (0, LANES - 8)))

    def gate_bwd_fn(a, b, z, bias):
        tot = a + b
        rows_t = tot.shape[0]
        r = lax.broadcasted_iota(jnp.int32, (rows_t, rows_t), 0)
        c = lax.broadcasted_iota(jnp.int32, (rows_t, rows_t), 1)
        sh = CHUNK.bit_length() - 1
        same_chunk = jnp.right_shift(r, sh) == jnp.right_shift(c, sh)
        dlf = _u01dot(((c >= r) & same_chunk).astype(BF16), tot)
        lane = lax.broadcasted_iota(jnp.int32, tot.shape, 1)
        dz = jnp.where(lane < hh, tot, jnp.where(lane < 2 * hh, dlf * _sigmoid(-(z + bias)), 0.0))
        return dz, jnp.sum(dz, axis=0, keepdims=True)

    dzif, db_if_p = _rowwise(gate_bwd_fn, [dgc, dgr_t, zif], [b_if_p], [(LANES, BF16)], [LANES], name="ml_gates_bwd",
                             tr=8 * CHUNK)
    for part, col in ((dsk, d), (dsv, 2 * d), (dgate, 8 * d)):
        dzm = lax.dynamic_update_slice(dzm, part, (0, col))
    dw_main = _mm(hn, dzm, ta=True, out_dtype=BF16, name="proj_in_dw")
    dw_if = _mm(hn, dzif, ta=True, out_dtype=BF16, name="proj_if_dw")

    def dw_quarter(k):
        lo, hi = k * qn, (k + 1) * qn
        segs = [(dw_main, 0, if_lo, 0), (dw_if, if_lo, if_hi, if_lo), (dw_main, if_hi, n_in, 2 * hh)]
        got = [src[:, max(lo, a) - off:min(hi, b) - off] for src, a, b, off in segs if max(lo, a) < min(hi, b)]
        return jnp.concatenate(got, axis=1)

    late = [to_parts(jnp.stack([dw_quarter(k) for k in range(4)]))]
    gl_send, gl_recv, gl_src, gl_land, gl_token = _split_start(
        "grads", late, [(8,) + a.shape[2:] for a in late], dw_if, name="exchange_late_start")
    dhn = _mm(dzm, w_main, tb=True, after=gl_token, name="proj_in_dx")
    dhn = _mm(dzif, w_if, tb=True, tiles=[dhn], name="proj_if_dx")
    dx, dg_mix = _rowwise(norm_bwd_fn, [x2, dhn, dx1], [g_mix], [(d, F32)], [d], name="norm_in_bwd", tr=512)

    own = lambda p: lax.dynamic_index_in_dim(lax.dynamic_index_in_dim(p, k4, 0, keepdims=False),
                                             lax.axis_index("c"), 0, keepdims=False)

    def finish(tag, send, recv, src, land, parts, after, ws, ms, vs):
        land = _split_wait("grads", send, recv, src, land, after, name=f"exchange_{tag}_wait")
        got = [lax.dynamic_update_index_in_dim(ld, own(p), me, 0) for ld, p in zip(land, parts)]
        halves = [_sum8(r, name=f"sum_grads_{tag}{i}") for i, r in enumerate(got)]
        both = _swap_halves(halves, name=f"swap_halves_{tag}")
        gs = [b.reshape(2 * b.shape[1], b.shape[2]) for b in both]
        return gs, [_adamw(w, g, m, v, name=f"adamw_{tag}{i}") for i, (w, g, m, v) in enumerate(zip(ws, gs, ms, vs))]

    first = lambda arrs: [a[0] for a in arrs]
    g_early, out_early = finish(
        "early", ge_send, ge_recv, ge_src, ge_land, early, [dx],
        first([w_mem_kv, w_sb_proj, w_ml_proj, w_x_proj, w_out, w_ff1, w_ff2]),
        first([m_w_mem_kv, m_w_sb_proj, m_w_ml_proj, m_w_x_proj, m_w_out, m_w_ff1, m_w_ff2]),
        first([v_w_mem_kv, v_w_sb_proj, v_w_ml_proj, v_w_x_proj, v_w_out, v_w_ff1, v_w_ff2]))
    g_late, out_late = finish(
        "late", gl_send, gl_recv, gl_src, gl_land, late, [o[0] for o in out_early],
        first([w_in]), first([m_w_in]), first([v_w_in]))
    g_big = [g[None] for g in g_late + g_early]
    big_out = [[o[None] for o in outs] for outs in out_late + out_early]

    small_g = [dg_mix, db_if_p[:, :2 * hh], db_gate, dconv_w, dconv_b, dg_mln, dg_mem, dg_qn, dg_kn, dg_mlp,
               jnp.sum(loss_cols).reshape(1, 1)]
    n_small = sum(a.size for a in small_g)
    rows = -(-n_small // (8 * LANES)) * 8
    g_small = _unpack(_allreduce_small(_pack(small_g, rows), out_late[0][0], name="allreduce_small"), small_g)
    loss = g_small[-1].reshape(())
    qw = conv_w.shape[2]
    g_conv_w = lax.dynamic_slice_in_dim(g_small[3], k4 * qw, qw, axis=1)
    g_small_w = [g_small[0], g_small[1], g_small[2], g_conv_w] + g_small[4:10]
    sm_w = [g_mix, b_if, b_gate, conv_w[0], conv_b, ml_norm_g, g_mem, q_norm_g, k_norm_g, g_mlp]
    sm_m = [m_g_mix, m_b_if, m_b_gate, m_conv_w[0], m_conv_b, m_ml_norm_g, m_g_mem, m_q_norm_g, m_k_norm_g, m_g_mlp]
    sm_v = [v_g_mix, v_b_if, v_b_gate, v_conv_w[0], v_conv_b, v_ml_norm_g, v_g_mem, v_q_norm_g, v_k_norm_g, v_g_mlp]
    n_sw = sum(a.size for a in sm_w)
    rows_w = -(-n_sw // (8 * LANES)) * 8
    sm_out = _adamw(_pack(sm_w, rows_w), _pack(g_small_w, rows_w), _pack(sm_m, rows_w), _pack(sm_v, rows_w),
                    name="adamw_small")
    sm_delta, sm_newm, sm_newv = [_unpack(p, sm_w) for p in sm_out]

    order = ["g_mix", "w_in", "b_if", "b_gate", "conv_w", "conv_b", "ml_norm_g", "g_mem", "w_mem_kv", "q_norm_g",
             "k_norm_g", "w_sb_proj", "w_ml_proj", "w_x_proj", "w_out", "g_mlp", "w_ff1", "w_ff2"]
    small_names = ["g_mix", "b_if", "b_gate", "conv_w", "conv_b", "ml_norm_g", "g_mem", "q_norm_g", "k_norm_g", "g_mlp"]
    big_names = ["w_in", "w_mem_kv", "w_sb_proj", "w_ml_proj", "w_x_proj", "w_out", "w_ff1", "w_ff2"]
    grads, deltas, new_m, new_v = {}, {}, {}, {}
    for i, nme in enumerate(small_names):
        shp = sm_w[i].shape if nme != "conv_w" else conv_w.shape
        grads[nme] = g_small_w[i].reshape(shp)
        deltas[nme], new_m[nme], new_v[nme] = (sm_delta[i].reshape(shp), sm_newm[i].reshape(shp),
                                               sm_newv[i].reshape(shp))
    for i, nme in enumerate(big_names):
        grads[nme] = g_big[i]
        deltas[nme], new_m[nme], new_v[nme] = big_out[i]
    return (loss, dx[None], *[grads[k] for k in order], *[deltas[k] for k in order], *[new_m[k] for k in order],
            *[new_v[k] for k in order])
```
